```python
import jax, jax.numpy as jnp
from jax import lax
import numpy as np

D_MODEL = 1024
BATCH = 8
SEQ = 8192
DEPTH = 1

POOL_WINDOWS = (2, 4, 8, 16)
N_POOL_GROUPS = len(POOL_WINDOWS)
D_POOL = D_MODEL
POOL_GW = D_POOL // N_POOL_GROUPS
POOL_OUT_GW = D_MODEL // N_POOL_GROUPS
D_CONV = D_MODEL
CONV_K = 3
D_IN = D_POOL + 3 * D_CONV + 2 * D_MODEL
D_FF = 2816
FFN_K = 3
N_MOD = 6
EPS = 1e-6

kernel_name = "hybrid_pool_shortconv_convffn_block"


def rmsnorm(x, g):
    xf = x.astype(jnp.float32)
    y = xf * lax.rsqrt(jnp.mean(xf * xf, axis=-1, keepdims=True) + EPS)
    return (y * g.astype(jnp.float32)).astype(x.dtype)


def causal_dwconv(x, w, b):
    k = w.shape[0]
    s = x.shape[1]
    xp = jnp.pad(x, ((0, 0), (k - 1, 0), (0, 0)))
    y = b
    for i in range(k):
        y = y + w[i] * xp[:, i:i + s]
    return y


def causal_multiscale_pool(u):
    bsz, s, _ = u.shape
    ug = u.reshape(bsz, s, N_POOL_GROUPS, POOL_GW)
    cs = jnp.cumsum(ug.astype(jnp.float32), axis=1)
    cs0 = jnp.pad(cs, ((0, 0), (1, 0), (0, 0), (0, 0)))
    t1 = jnp.arange(1, s + 1, dtype=jnp.float32)
    outs = []
    for g, w in enumerate(POOL_WINDOWS):
        upper = cs0[:, 1:, g]
        lower = jnp.pad(cs0[:, :s + 1 - w, g], ((0, 0), (w - 1, 0), (0, 0)))
        cnt = jnp.minimum(t1, float(w))[None, :, None]
        outs.append((upper - lower) / cnt)
    pooled = jnp.stack(outs, axis=2).astype(u.dtype)
    return pooled - ug


def _fwd_setup_inputs(seed: int = 0) -> dict:
    key = jax.random.key(seed)
    ks = jax.random.split(key, 20)
    L, D = DEPTH, D_MODEL
    nrm = lambda k, shp, fan: jax.random.normal(k, shp, jnp.float32) * (fan ** -0.5)
    gain = lambda k, n: 1.0 + 0.05 * jax.random.normal(k, (L, n), jnp.float32)
    return {
        "x": jax.random.normal(ks[0], (BATCH, SEQ, D), jnp.float32),
        "c": jax.random.normal(ks[1], (BATCH, D), jnp.float32),
        "g_pre_mix": gain(ks[2], D),
        "g_post_mix": gain(ks[3], D),
        "g_pre_ffn": gain(ks[4], D),
        "g_post_ffn": gain(ks[5], D),
        "w_ada": 0.5 * nrm(ks[6], (L, D, N_MOD * D), D),
        "b_ada": 0.01 * jax.random.normal(ks[7], (L, N_MOD * D), jnp.float32),
        "w_in": nrm(ks[8], (L, D, D_IN), D),
        "w_pool": nrm(ks[9], (L, N_POOL_GROUPS, POOL_GW, POOL_OUT_GW), POOL_GW),
        "pool_scale": gain(ks[10], D),
        "conv_w": nrm(ks[11], (L, CONV_K, D_CONV), CONV_K),
        "conv_b": 0.01 * jax.random.normal(ks[12], (L, D_CONV), jnp.float32),
        "w_bout": nrm(ks[13], (L, D_CONV, D), D_CONV),
        "w_o": nrm(ks[14], (L, D, D), D),
        "w_up": nrm(ks[15], (L, D, 2 * D_FF), D),
        "ffn_conv_w": nrm(ks[16], (L, FFN_K, 2 * D_FF), FFN_K),
        "ffn_conv_b": 0.01 * jax.random.normal(ks[17], (L, 2 * D_FF), jnp.float32),
        "w_down": nrm(ks[18], (L, D_FF, D), D_FF),
    }


def _fwd_reference(x, c, g_pre_mix, g_post_mix, g_pre_ffn, g_post_ffn, w_ada, b_ada, w_in, w_pool,
              pool_scale, conv_w, conv_b, w_bout, w_o, w_up, ffn_conv_w, ffn_conv_b, w_down):
    bsz, s, d = x.shape
    for l in range(DEPTH):
        mod = c @ w_ada[l] + b_ada[l]
        sh1, sc1, gt1, sh2, sc2, gt2 = [m[:, None, :] for m in jnp.split(mod, N_MOD, axis=-1)]

        h = rmsnorm(x, g_pre_mix[l]) * (1.0 + sc1) + sh1
        proj = h @ w_in[l]
        u_pool, u_x, u_b, u_c, z_a, z_b = jnp.split(
            proj, np.cumsum([D_POOL, D_CONV, D_CONV, D_CONV, D_MODEL])[:].tolist(), axis=-1)

        pg = causal_multiscale_pool(u_pool)
        y_a = jnp.einsum('bsgc,gcd->bsgd', pg, w_pool[l]).reshape(bsz, s, d) * pool_scale[l]

        y_b = (u_b * causal_dwconv(u_c * u_x, conv_w[l], conv_b[l])) @ w_bout[l]

        merged = jax.nn.sigmoid(z_a) * y_a + jax.nn.sigmoid(z_b) * y_b
        x = x + gt1 * rmsnorm(merged @ w_o[l], g_post_mix[l])

        h = rmsnorm(x, g_pre_ffn[l]) * (1.0 + sc2) + sh2
        up = causal_dwconv(h @ w_up[l], ffn_conv_w[l], ffn_conv_b[l])
        gate, val = jnp.split(up, 2, axis=-1)
        ff = (jax.nn.gelu(gate, approximate=True) * val) @ w_down[l]
        x = x + gt2 * rmsnorm(ff, g_post_ffn[l])
    return x


import jax as _jax
import jax.numpy as _jnp

TWIN_FORMAT = 'train_step'
FWD_PARAMS = ['x', 'c', 'g_pre_mix', 'g_post_mix', 'g_pre_ffn', 'g_post_ffn', 'w_ada', 'b_ada', 'w_in', 'w_pool', 'pool_scale', 'conv_w', 'conv_b', 'w_bout', 'w_o', 'w_up', 'ffn_conv_w', 'ffn_conv_b', 'w_down']
TWIN_WEIGHTS = ['g_pre_mix', 'g_post_mix', 'g_pre_ffn', 'g_post_ffn', 'w_ada', 'b_ada', 'w_in', 'w_pool', 'pool_scale', 'conv_w', 'conv_b', 'w_bout', 'w_o', 'w_up', 'ffn_conv_w', 'ffn_conv_b', 'w_down']
TWIN_DIFF_INPUT = 'x'
TWIN_INPUTS = ['x', 'c', 'g_pre_mix', 'g_post_mix', 'g_pre_ffn', 'g_post_ffn', 'w_ada', 'b_ada', 'w_in', 'w_pool', 'pool_scale', 'conv_w', 'conv_b', 'w_bout', 'w_o', 'w_up', 'ffn_conv_w', 'ffn_conv_b', 'w_down', 'loss_target', 'm_g_pre_mix', 'm_g_post_mix', 'm_g_pre_ffn', 'm_g_post_ffn', 'm_w_ada', 'm_b_ada', 'm_w_in', 'm_w_pool', 'm_pool_scale', 'm_conv_w', 'm_conv_b', 'm_w_bout', 'm_w_o', 'm_w_up', 'm_ffn_conv_w', 'm_ffn_conv_b', 'm_w_down', 'v_g_pre_mix', 'v_g_post_mix', 'v_g_pre_ffn', 'v_g_post_ffn', 'v_w_ada', 'v_b_ada', 'v_w_in', 'v_w_pool', 'v_pool_scale', 'v_conv_w', 'v_conv_b', 'v_w_bout', 'v_w_o', 'v_w_up', 'v_ffn_conv_w', 'v_ffn_conv_b', 'v_w_down']
TWIN_OUTPUTS = ['loss', 'grad_x', 'grad_g_pre_mix', 'grad_g_post_mix', 'grad_g_pre_ffn', 'grad_g_post_ffn', 'grad_w_ada', 'grad_b_ada', 'grad_w_in', 'grad_w_pool', 'grad_pool_scale', 'grad_conv_w', 'grad_conv_b', 'grad_w_bout', 'grad_w_o', 'grad_w_up', 'grad_ffn_conv_w', 'grad_ffn_conv_b', 'grad_w_down', 'delta_g_pre_mix', 'delta_g_post_mix', 'delta_g_pre_ffn', 'delta_g_post_ffn', 'delta_w_ada', 'delta_b_ada', 'delta_w_in', 'delta_w_pool', 'delta_pool_scale', 'delta_conv_w', 'delta_conv_b', 'delta_w_bout', 'delta_w_o', 'delta_w_up', 'delta_ffn_conv_w', 'delta_ffn_conv_b', 'delta_w_down', 'new_m_g_pre_mix', 'new_m_g_post_mix', 'new_m_g_pre_ffn', 'new_m_g_post_ffn', 'new_m_w_ada', 'new_m_b_ada', 'new_m_w_in', 'new_m_w_pool', 'new_m_pool_scale', 'new_m_conv_w', 'new_m_conv_b', 'new_m_w_bout', 'new_m_w_o', 'new_m_w_up', 'new_m_ffn_conv_w', 'new_m_ffn_conv_b', 'new_m_w_down', 'new_v_g_pre_mix', 'new_v_g_post_mix', 'new_v_g_pre_ffn', 'new_v_g_post_ffn', 'new_v_w_ada', 'new_v_b_ada', 'new_v_w_in', 'new_v_w_pool', 'new_v_pool_scale', 'new_v_conv_w', 'new_v_conv_b', 'new_v_w_bout', 'new_v_w_o', 'new_v_w_up', 'new_v_ffn_conv_w', 'new_v_ffn_conv_b', 'new_v_w_down']
TWIN_LEAF_KINDS = {'loss': 'loss', 'grad_x': 'grad_x', 'grad_g_pre_mix': 'grad_w', 'grad_g_post_mix': 'grad_w', 'grad_g_pre_ffn': 'grad_w', 'grad_g_post_ffn': 'grad_w', 'grad_w_ada': 'grad_w', 'grad_b_ada': 'grad_w', 'grad_w_in': 'grad_w', 'grad_w_pool': 'grad_w', 'grad_pool_scale': 'grad_w', 'grad_conv_w': 'grad_w', 'grad_conv_b': 'grad_w', 'grad_w_bout': 'grad_w', 'grad_w_o': 'grad_w', 'grad_w_up': 'grad_w', 'grad_ffn_conv_w': 'grad_w', 'grad_ffn_conv_b': 'grad_w', 'grad_w_down': 'grad_w', 'delta_g_pre_mix': 'delta_w', 'delta_g_post_mix': 'delta_w', 'delta_g_pre_ffn': 'delta_w', 'delta_g_post_ffn': 'delta_w', 'delta_w_ada': 'delta_w', 'delta_b_ada': 'delta_w', 'delta_w_in': 'delta_w', 'delta_w_pool': 'delta_w', 'delta_pool_scale': 'delta_w', 'delta_conv_w': 'delta_w', 'delta_conv_b': 'delta_w', 'delta_w_bout': 'delta_w', 'delta_w_o': 'delta_w', 'delta_w_up': 'delta_w', 'delta_ffn_conv_w': 'delta_w', 'delta_ffn_conv_b': 'delta_w', 'delta_w_down': 'delta_w', 'new_m_g_pre_mix': 'new_m', 'new_m_g_post_mix': 'new_m', 'new_m_g_pre_ffn': 'new_m', 'new_m_g_post_ffn': 'new_m', 'new_m_w_ada': 'new_m', 'new_m_b_ada': 'new_m', 'new_m_w_in': 'new_m', 'new_m_w_pool': 'new_m', 'new_m_pool_scale': 'new_m', 'new_m_conv_w': 'new_m', 'new_m_conv_b': 'new_m', 'new_m_w_bout': 'new_m', 'new_m_w_o': 'new_m', 'new_m_w_up': 'new_m', 'new_m_ffn_conv_w': 'new_m', 'new_m_ffn_conv_b': 'new_m', 'new_m_w_down': 'new_m', 'new_v_g_pre_mix': 'new_v', 'new_v_g_post_mix': 'new_v', 'new_v_g_pre_ffn': 'new_v', 'new_v_g_post_ffn': 'new_v', 'new_v_w_ada': 'new_v', 'new_v_b_ada': 'new_v', 'new_v_w_in': 'new_v', 'new_v_w_pool': 'new_v', 'new_v_pool_scale': 'new_v', 'new_v_conv_w': 'new_v', 'new_v_conv_b': 'new_v', 'new_v_w_bout': 'new_v', 'new_v_w_o': 'new_v', 'new_v_w_up': 'new_v', 'new_v_ffn_conv_w': 'new_v', 'new_v_ffn_conv_b': 'new_v', 'new_v_w_down': 'new_v'}


def _forward(args):
    return _fwd_reference(*[args[k] for k in FWD_PARAMS])


def _output_shape():
    def fwd():
        inp = _fwd_setup_inputs(0)
        return _fwd_reference(*[inp[k] for k in FWD_PARAMS])
    out = _jax.eval_shape(fwd)
    return out.shape, out.dtype

N_MICROBATCH = 1
ADAM_LR = 0.001
ADAM_B1 = 0.9
ADAM_B2 = 0.999
ADAM_EPS = 1e-08
ADAM_WD = 0.01
ADAM_STEP = 10
PER_EXAMPLE_BATCH_AXIS = {'x': 0, 'c': 0, 'loss_target': 0}
SHARED_INPUTS = []
_WEIGHT_DTYPES = {'g_pre_mix': _jnp.float32, 'g_post_mix': _jnp.float32, 'g_pre_ffn': _jnp.float32, 'g_post_ffn': _jnp.float32, 'w_ada': _jnp.float32, 'b_ada': _jnp.float32, 'w_in': _jnp.float32, 'w_pool': _jnp.float32, 'pool_scale': _jnp.float32, 'conv_w': _jnp.float32, 'conv_b': _jnp.float32, 'w_bout': _jnp.float32, 'w_o': _jnp.float32, 'w_up': _jnp.float32, 'ffn_conv_w': _jnp.float32, 'ffn_conv_b': _jnp.float32, 'w_down': _jnp.float32}
MOMENT_SCALE = {'g_pre_mix': 3.807884e-01, 'g_post_mix': 1.851877e+01, 'g_pre_ffn': 2.917872e-01, 'g_post_ffn': 1.828450e+01, 'w_ada': 6.548000e+00, 'b_ada': 6.308380e+00, 'w_in': 1.794142e-01, 'w_pool': 1.479271e-01, 'pool_scale': 1.486700e-01, 'conv_w': 2.419184e-01, 'conv_b': 2.920871e-01, 'w_bout': 2.653173e-01, 'w_o': 3.451690e-01, 'w_up': 2.023549e-01, 'ffn_conv_w': 2.419352e-01, 'ffn_conv_b': 3.618130e-01, 'w_down': 4.505576e-01}


def _to_microbatches(a, axis):
    t = _jnp.moveaxis(a, axis, 0)
    t = t.reshape((N_MICROBATCH, t.shape[0] // N_MICROBATCH) + t.shape[1:])
    return _jnp.moveaxis(t, 1, axis + 1)


def setup_inputs(seed: int = 0) -> dict:
    inp = _fwd_setup_inputs(seed)
    key = _jax.random.fold_in(_jax.random.key(seed), 7919)
    shape, _ = _output_shape()
    out = dict(inp)
    out["loss_target"] = _jax.random.normal(_jax.random.fold_in(key, 0), shape, _jnp.float32)
    for i, name in enumerate(TWIN_WEIGHTS):
        w = inp[name].astype(_jnp.float32)
        if MOMENT_SCALE is None:
            s = _jnp.sqrt(_jnp.mean(_jnp.square(w)) + 1e-30)
        else:
            s = MOMENT_SCALE[name]
        km, kv = _jax.random.split(_jax.random.fold_in(key, i + 1))
        out[name] = w
        out["m_" + name] = s * _jax.random.normal(km, w.shape, _jnp.float32)
        out["v_" + name] = (s * s) * _jax.random.uniform(kv, w.shape, _jnp.float32, 0.5, 1.5)
    if N_MICROBATCH > 1:
        for name, axis in PER_EXAMPLE_BATCH_AXIS.items():
            out[name] = _to_microbatches(out[name], axis)
    return {'x': out['x'], 'c': out['c'], 'g_pre_mix': out['g_pre_mix'], 'g_post_mix': out['g_post_mix'], 'g_pre_ffn': out['g_pre_ffn'], 'g_post_ffn': out['g_post_ffn'], 'w_ada': out['w_ada'], 'b_ada': out['b_ada'], 'w_in': out['w_in'], 'w_pool': out['w_pool'], 'pool_scale': out['pool_scale'], 'conv_w': out['conv_w'], 'conv_b': out['conv_b'], 'w_bout': out['w_bout'], 'w_o': out['w_o'], 'w_up': out['w_up'], 'ffn_conv_w': out['ffn_conv_w'], 'ffn_conv_b': out['ffn_conv_b'], 'w_down': out['w_down'], 'loss_target': out['loss_target'], 'm_g_pre_mix': out['m_g_pre_mix'], 'm_g_post_mix': out['m_g_post_mix'], 'm_g_pre_ffn': out['m_g_pre_ffn'], 'm_g_post_ffn': out['m_g_post_ffn'], 'm_w_ada': out['m_w_ada'], 'm_b_ada': out['m_b_ada'], 'm_w_in': out['m_w_in'], 'm_w_pool': out['m_w_pool'], 'm_pool_scale': out['m_pool_scale'], 'm_conv_w': out['m_conv_w'], 'm_conv_b': out['m_conv_b'], 'm_w_bout': out['m_w_bout'], 'm_w_o': out['m_w_o'], 'm_w_up': out['m_w_up'], 'm_ffn_conv_w': out['m_ffn_conv_w'], 'm_ffn_conv_b': out['m_ffn_conv_b'], 'm_w_down': out['m_w_down'], 'v_g_pre_mix': out['v_g_pre_mix'], 'v_g_post_mix': out['v_g_post_mix'], 'v_g_pre_ffn': out['v_g_pre_ffn'], 'v_g_post_ffn': out['v_g_post_ffn'], 'v_w_ada': out['v_w_ada'], 'v_b_ada': out['v_b_ada'], 'v_w_in': out['v_w_in'], 'v_w_pool': out['v_w_pool'], 'v_pool_scale': out['v_pool_scale'], 'v_conv_w': out['v_conv_w'], 'v_conv_b': out['v_conv_b'], 'v_w_bout': out['v_w_bout'], 'v_w_o': out['v_w_o'], 'v_w_up': out['v_w_up'], 'v_ffn_conv_w': out['v_ffn_conv_w'], 'v_ffn_conv_b': out['v_ffn_conv_b'], 'v_w_down': out['v_w_down']}


def _loss(weights, diff, rest, loss_target):
    with _jax.named_scope("forward"):
        args = {**rest, TWIN_DIFF_INPUT: diff, **{k: w.astype(_WEIGHT_DTYPES[k]) for k, w in weights.items()}}
        y = _forward(args)
    with _jax.named_scope("loss_head"):
        err = _jnp.square(y.astype(_jnp.float32) - loss_target)
        return 0.5 * _jnp.sum(_jnp.mean(err, axis=-1)) if err.ndim else 0.5 * err


def _adamw(w, g, m, v):
    m = ADAM_B1 * m + (1.0 - ADAM_B1) * g
    v = ADAM_B2 * v + (1.0 - ADAM_B2) * _jnp.square(g)
    m_hat = m / (1.0 - ADAM_B1 ** ADAM_STEP)
    v_hat = v / (1.0 - ADAM_B2 ** ADAM_STEP)
    delta = -ADAM_LR * (m_hat / (_jnp.sqrt(v_hat) + ADAM_EPS) + ADAM_WD * w)
    return delta, m, v


def reference(x, c, g_pre_mix, g_post_mix, g_pre_ffn, g_post_ffn, w_ada, b_ada, w_in, w_pool, pool_scale, conv_w, conv_b, w_bout, w_o, w_up, ffn_conv_w, ffn_conv_b, w_down, loss_target, m_g_pre_mix, m_g_post_mix, m_g_pre_ffn, m_g_post_ffn, m_w_ada, m_b_ada, m_w_in, m_w_pool, m_pool_scale, m_conv_w, m_conv_b, m_w_bout, m_w_o, m_w_up, m_ffn_conv_w, m_ffn_conv_b, m_w_down, v_g_pre_mix, v_g_post_mix, v_g_pre_ffn, v_g_post_ffn, v_w_ada, v_b_ada, v_w_in, v_w_pool, v_pool_scale, v_conv_w, v_conv_b, v_w_bout, v_w_o, v_w_up, v_ffn_conv_w, v_ffn_conv_b, v_w_down):
    given = dict(x=x, c=c, g_pre_mix=g_pre_mix, g_post_mix=g_post_mix, g_pre_ffn=g_pre_ffn, g_post_ffn=g_post_ffn, w_ada=w_ada, b_ada=b_ada, w_in=w_in, w_pool=w_pool, pool_scale=pool_scale, conv_w=conv_w, conv_b=conv_b, w_bout=w_bout, w_o=w_o, w_up=w_up, ffn_conv_w=ffn_conv_w, ffn_conv_b=ffn_conv_b, w_down=w_down, loss_target=loss_target, m_g_pre_mix=m_g_pre_mix, m_g_post_mix=m_g_post_mix, m_g_pre_ffn=m_g_pre_ffn, m_g_post_ffn=m_g_post_ffn, m_w_ada=m_w_ada, m_b_ada=m_b_ada, m_w_in=m_w_in, m_w_pool=m_w_pool, m_pool_scale=m_pool_scale, m_conv_w=m_conv_w, m_conv_b=m_conv_b, m_w_bout=m_w_bout, m_w_o=m_w_o, m_w_up=m_w_up, m_ffn_conv_w=m_ffn_conv_w, m_ffn_conv_b=m_ffn_conv_b, m_w_down=m_w_down, v_g_pre_mix=v_g_pre_mix, v_g_post_mix=v_g_post_mix, v_g_pre_ffn=v_g_pre_ffn, v_g_post_ffn=v_g_post_ffn, v_w_ada=v_w_ada, v_b_ada=v_b_ada, v_w_in=v_w_in, v_w_pool=v_w_pool, v_pool_scale=v_pool_scale, v_conv_w=v_conv_w, v_conv_b=v_conv_b, v_w_bout=v_w_bout, v_w_o=v_w_o, v_w_up=v_w_up, v_ffn_conv_w=v_ffn_conv_w, v_ffn_conv_b=v_ffn_conv_b, v_w_down=v_w_down)
    weights = {n: given[n] for n in TWIN_WEIGHTS}
    shared = {n: given[n] for n in SHARED_INPUTS}
    per_example = {n: given[n] for n in ['x', 'c']}
    grad_fn = _jax.value_and_grad(_loss, argnums=(0, 1))

    def one_microbatch(ex, loss_target):
        ex = dict(ex)
        diff = ex.pop(TWIN_DIFF_INPUT)
        return grad_fn(weights, diff, {**shared, **ex}, loss_target)

    if N_MICROBATCH == 1:
        loss, (grad_w, grad_x) = one_microbatch(per_example, given["loss_target"])
    else:
        def body(carry, xs):
            loss_sum, grad_sum = carry
            l_k, (gw_k, gx_k) = one_microbatch(xs[0], xs[1])
            with _jax.named_scope("update"):
                return (loss_sum + l_k, _jax.tree.map(_jnp.add, grad_sum, gw_k)), gx_k

        init = (_jnp.zeros((), _jnp.float32), _jax.tree.map(_jnp.zeros_like, weights))
        (loss, grad_w), grad_x = _jax.lax.scan(body, init, (per_example, given["loss_target"]))
    with _jax.named_scope("update"):
        delta_w, new_m, new_v = {}, {}, {}
        for n in TWIN_WEIGHTS:
            delta_w[n], new_m[n], new_v[n] = _adamw(weights[n], grad_w[n], given["m_" + n], given["v_" + n])
    return (loss, grad_x, *[grad_w[n] for n in TWIN_WEIGHTS], *[delta_w[n] for n in TWIN_WEIGHTS],
            *[new_m[n] for n in TWIN_WEIGHTS], *[new_v[n] for n in TWIN_WEIGHTS])
```

```python
import functools
import math

import jax
import jax.numpy as jnp
from jax import lax
from jax.experimental import pallas as pl
from jax.experimental.pallas import tpu as pltpu

F32 = jnp.float32
BF16 = jnp.bfloat16
MESH = pl.DeviceIdType.MESH

NDEV = 8
NCHIP = 4
EPS = 1e-6
POOL_WINDOWS = (2, 4, 8, 16)
POOL_HALO = 16
CONV_HALO = 8
LANES = 128
ADAM_LR = 0.001
ADAM_B1 = 0.9
ADAM_B2 = 0.999
ADAM_EPS = 1e-08
ADAM_WD = 0.01
ADAM_STEP = 10
GELU_C0 = math.sqrt(2.0 / math.pi)
GELU_C1 = 0.044715
VMEM_LIMIT = 56 * 2**20


def _vmem():
    return pl.BlockSpec(memory_space=pltpu.VMEM)


def _any():
    return pl.BlockSpec(memory_space=pl.ANY)


def _params(*sem):
    return pltpu.CompilerParams(dimension_semantics=sem, vmem_limit_bytes=VMEM_LIMIT)


def _sds(shape, dtype):
    return jax.ShapeDtypeStruct(tuple(shape), dtype)


def _position():
    return lax.axis_index("x"), lax.axis_index("y"), lax.axis_index("c")


def _linear(x, y, c):
    return 4 * x + 2 * y + c


def _dot(a, b):
    return jnp.dot(a, b, preferred_element_type=F32)


def _dot_nt(a, b):
    return lax.dot_general(a, b, (((1,), (1,)), ((), ())), preferred_element_type=F32)


def _dot_tn(a, b):
    return lax.dot_general(a, b, (((0,), (0,)), ((), ())), preferred_element_type=F32)


def _colsum(v):
    return jnp.sum(v, axis=0, keepdims=True)


def _rowmean(v):
    return jnp.mean(v, axis=-1, keepdims=True)


def _peers(x, y, c):
    out = []
    for k in range(1, NDEV):
        out.append(((1 - x) if k & 4 else x, (1 - y) if k & 2 else y, (1 - c) if k & 1 else c))
    return out


def _small_allgather(v, name, with_sum):
    r, n = v.shape

    def body(v_ref, gat_ref, *rest):
        if with_sum:
            sum_ref, send_sems, recv_sems, local_sem = rest
        else:
            send_sems, recv_sems, local_sem = rest
        x, y, c = _position()
        me = _linear(x, y, c)
        mine = pltpu.make_async_copy(v_ref, gat_ref.at[me], local_sem)
        mine.start()
        peers = _peers(x, y, c)
        sends = []
        for k, peer in enumerate(peers):
            cp = pltpu.make_async_remote_copy(src_ref=v_ref, dst_ref=gat_ref.at[me], send_sem=send_sems.at[k],
                                              recv_sem=recv_sems.at[k], device_id=peer, device_id_type=MESH)
            cp.start()
            sends.append(cp)
        for k, peer in enumerate(peers):
            pltpu.make_async_remote_copy(src_ref=v_ref, dst_ref=gat_ref.at[_linear(*peer)], send_sem=send_sems.at[k],
                                         recv_sem=recv_sems.at[k], device_id=peer, device_id_type=MESH).wait_recv()
        for cp in sends:
            cp.wait_send()
        mine.wait()
        if with_sum:
            acc = gat_ref[0]
            for j in range(1, NDEV):
                acc = acc + gat_ref[j]
            sum_ref[...] = acc

    out_shape = [_sds((NDEV, r, n), F32)] + ([_sds((r, n), F32)] if with_sum else [])
    return pl.pallas_call(
        body, name=name, out_shape=out_shape, in_specs=[_vmem()], out_specs=[_vmem()] * len(out_shape),
        scratch_shapes=[pltpu.SemaphoreType.DMA((NDEV - 1,)), pltpu.SemaphoreType.DMA((NDEV - 1,)),
                        pltpu.SemaphoreType.DMA(())],
    )(v)


def _allgather_weights(shards):
    n = len(shards)

    def body(*refs):
        src, dst = refs[:n], refs[n:2 * n]
        send_sems, recv_sems, local_sems = refs[2 * n:]
        x, y, c = _position()
        me, sibling = (x, y, c), (x, y, 1 - c)
        chips = [(1 - x, y), (x, 1 - y), (1 - x, 1 - y)]

        def copy(a, k, block, to, from_src=False):
            blk = dst[a].at[_linear(*block)]
            return pltpu.make_async_remote_copy(src_ref=src[a] if from_src else blk, dst_ref=blk,
                                                send_sem=send_sems.at[a, k], recv_sem=recv_sems.at[a, k],
                                                device_id=to, device_id_type=MESH)

        local = [pltpu.make_async_copy(src[a], dst[a].at[_linear(*me)], local_sems.at[a]) for a in range(n)]
        for cp in local:
            cp.start()
        first = []
        for j, chip in enumerate(chips):
            for a in range(n):
                first.append(copy(a, 1 + j, me, (*chip, c), from_src=True))
        for a in range(n):
            first.append(copy(a, 0, me, sibling, from_src=True))
        for cp in first:
            cp.start()
        passed = []
        for j, chip in enumerate(chips):
            for a in range(n):
                copy(a, 1 + j, (*chip, c), me).wait_recv()
                cp = copy(a, 4 + j, (*chip, c), sibling)
                cp.start()
                passed.append(cp)
        for a in range(n):
            copy(a, 0, sibling, me).wait_recv()
        for j, chip in enumerate(chips):
            for a in range(n):
                copy(a, 4 + j, (*chip, 1 - c), me).wait_recv()
        for cp in first + passed:
            cp.wait_send()
        for cp in local:
            cp.wait()

    return pl.pallas_call(
        body, name="allgather_weights",
        out_shape=[_sds((NDEV,) + s.shape, s.dtype) for s in shards],
        in_specs=[_any()] * n, out_specs=[_any()] * n,
        scratch_shapes=[pltpu.SemaphoreType.DMA((n, 7)), pltpu.SemaphoreType.DMA((n, 7)),
                        pltpu.SemaphoreType.DMA((n,))],
    )(*shards)


def _exchange_sibling(grads):
    n = len(grads)

    def body(*refs):
        src, dst = refs[:n], refs[n:2 * n]
        send_sems, recv_sems = refs[2 * n:]
        x, y, c = _position()
        sibling = (x, y, 1 - c)
        sends = []
        for a in range(n):
            for q in range(NCHIP):
                cp = pltpu.make_async_remote_copy(src_ref=src[a].at[2 * q + 1 - c], dst_ref=dst[a].at[q],
                                                  send_sem=send_sems.at[a, q], recv_sem=recv_sems.at[a, q],
                                                  device_id=sibling, device_id_type=MESH)
                cp.start()
                sends.append(cp)
        for cp in sends:
            cp.wait_recv()
        for cp in sends:
            cp.wait_send()

    return pl.pallas_call(
        body, name="rs_exchange_sibling",
        out_shape=[_sds((NCHIP,) + g.shape[1:], g.dtype) for g in grads],
        in_specs=[_any()] * n, out_specs=[_any()] * n,
        scratch_shapes=[pltpu.SemaphoreType.DMA((n, NCHIP)), pltpu.SemaphoreType.DMA((n, NCHIP))],
    )(*grads)


def _exchange_chips(chip_sums):
    n = len(chip_sums)

    def body(*refs):
        src, dst = refs[:n], refs[n:2 * n]
        send_sems, recv_sems = refs[2 * n:]
        x, y, c = _position()
        chips = [(1 - x, y), (x, 1 - y), (1 - x, 1 - y)]
        sends = []
        for j, chip in enumerate(chips):
            for a in range(n):
                cp = pltpu.make_async_remote_copy(src_ref=src[a].at[2 * chip[0] + chip[1]], dst_ref=dst[a].at[j],
                                                  send_sem=send_sems.at[a, j], recv_sem=recv_sems.at[a, j],
                                                  device_id=(*chip, c), device_id_type=MESH)
                cp.start()
                sends.append(cp)
        for cp in sends:
            cp.wait_recv()
        for cp in sends:
            cp.wait_send()

    return pl.pallas_call(
        body, name="rs_exchange_chips",
        out_shape=[_sds((3,) + s.shape[1:], s.dtype) for s in chip_sums],
        in_specs=[_any()] * n, out_specs=[_any()] * n,
        scratch_shapes=[pltpu.SemaphoreType.DMA((n, 3)), pltpu.SemaphoreType.DMA((n, 3))],
    )(*chip_sums)


def _row_block(r):
    for rb in (512, 256, 128, 64, 32, 16):
        if r % rb == 0:
            return rb
    return r


def _chip_sum(grad, from_sibling, core):
    _, r, c = grad.shape
    rb = _row_block(r)

    def body(core_ref, g_ref, s_ref, o_ref):
        del core_ref
        o_ref[...] = (g_ref[...].astype(F32) + s_ref[...].astype(F32)).astype(o_ref.dtype)

    grid_spec = pltpu.PrefetchScalarGridSpec(
        num_scalar_prefetch=1, grid=(NCHIP, r // rb),
        in_specs=[pl.BlockSpec((None, rb, c), lambda q, i, core: (2 * q + core[0], i, 0)),
                  pl.BlockSpec((None, rb, c), lambda q, i, core: (q, i, 0))],
        out_specs=pl.BlockSpec((None, rb, c), lambda q, i, core: (q, i, 0)))
    return pl.pallas_call(body, name="rs_chip_sum", grid_spec=grid_spec, out_shape=_sds((NCHIP, r, c), BF16),
                          compiler_params=_params("parallel", "parallel"))(core, grad, from_sibling)


def _adamw_math(w, g, m, v):
    m2 = ADAM_B1 * m + (1.0 - ADAM_B1) * g
    v2 = ADAM_B2 * v + (1.0 - ADAM_B2) * jnp.square(g)
    m_hat = m2 / (1.0 - ADAM_B1 ** ADAM_STEP)
    v_hat = v2 / (1.0 - ADAM_B2 ** ADAM_STEP)
    delta = -ADAM_LR * (m_hat / (jnp.sqrt(v_hat) + ADAM_EPS) + ADAM_WD * w)
    return delta, m2, v2


def _adamw(w, g, m, v, name):
    r, c = w.shape
    rb = _row_block(r)

    def body(w_ref, g_ref, m_ref, v_ref, d_ref, m2_ref, v2_ref):
        d, m2, v2 = _adamw_math(w_ref[...], g_ref[...], m_ref[...], v_ref[...])
        d_ref[...] = d
        m2_ref[...] = m2
        v2_ref[...] = v2

    blk = pl.BlockSpec((rb, c), lambda i: (i, 0))
    return pl.pallas_call(body, name=name, grid=(r // rb,), in_specs=[blk] * 4, out_specs=[blk] * 3,
                          out_shape=[_sds((r, c), F32)] * 3, compiler_params=_params("parallel"))(w, g, m, v)


def _reduce_adamw(grad, from_sibling, from_chips, sel, w, m, v, name):
    r, c = w.shape
    rb = _row_block(r)

    def body(sel_ref, g_ref, s_ref, c0_ref, c1_ref, c2_ref, w_ref, m_ref, v_ref, go_ref, d_ref, m2_ref, v2_ref):
        del sel_ref
        g = g_ref[...].astype(F32) + s_ref[...].astype(F32)
        g = g + c0_ref[...].astype(F32)
        g = g + c1_ref[...].astype(F32)
        g = g + c2_ref[...].astype(F32)
        d, m2, v2 = _adamw_math(w_ref[...], g, m_ref[...], v_ref[...])
        go_ref[...] = g
        d_ref[...] = d
        m2_ref[...] = m2
        v2_ref[...] = v2

    blk = pl.BlockSpec((rb, c), lambda i, sel: (i, 0))
    grid_spec = pltpu.PrefetchScalarGridSpec(
        num_scalar_prefetch=1, grid=(r // rb,),
        in_specs=[pl.BlockSpec((None, rb, c), lambda i, sel: (sel[0], i, 0)),
                  pl.BlockSpec((None, rb, c), lambda i, sel: (sel[1], i, 0)),
                  pl.BlockSpec((None, rb, c), lambda i, sel: (0, i, 0)),
                  pl.BlockSpec((None, rb, c), lambda i, sel: (1, i, 0)),
                  pl.BlockSpec((None, rb, c), lambda i, sel: (2, i, 0)),
                  blk, blk, blk],
        out_specs=[blk] * 4)
    return pl.pallas_call(body, name=name, grid_spec=grid_spec, out_shape=[_sds((r, c), F32)] * 4,
                          compiler_params=_params("parallel"))(sel, grad, from_sibling, from_chips, from_chips,
                                                               from_chips, w, m, v)


def _mod_piece(c_all, w_ada, b_piece):
    rows, _ = c_all.shape
    n = w_ada.shape[1]

    def body(c_ref, w_ref, b_ref, o_ref):
        o_ref[...] = _dot(c_ref[...], w_ref[...].astype(BF16)) + b_ref[...]

    return pl.pallas_call(body, name="ada_mod", out_shape=_sds((rows, n), F32),
                          in_specs=[_vmem()] * 3, out_specs=_vmem())(c_all, w_ada, b_piece)


def _wada_grad(c_all, dmod_piece):
    d = c_all.shape[1]
    n = dmod_piece.shape[1]

    def body(c_ref, dm_ref, o_ref):
        o_ref[...] = _dot_tn(c_ref[...], dm_ref[...])

    return pl.pallas_call(body, name="ada_wgrad", out_shape=_sds((d, n), F32),
                          in_specs=[_vmem()] * 2, out_specs=_vmem())(c_all, dmod_piece)


def _norm_matmul(xin, vec, w, splits, tm, name):
    t, d = xin.shape
    ch = 1024

    def body(x_ref, vec_ref, w_ref, hb_ref, *z_refs):
        x = x_ref[...]
        r = lax.rsqrt(_rowmean(x * x) + EPS)
        h = ((x * r) * vec_ref[0:1, :]) * vec_ref[1:2, :] + vec_ref[2:3, :]
        hb = h.astype(BF16)
        hb_ref[...] = hb
        off = 0
        for z_ref, width in zip(z_refs, splits):
            for c0 in range(0, width, ch):
                cw = min(ch, width - c0)
                z_ref[:, c0:c0 + cw] = _dot(hb, w_ref[:, off + c0:off + c0 + cw])
            off += width

    row = lambda n: pl.BlockSpec((tm, n), lambda i: (i, 0))
    return pl.pallas_call(
        body, name=name, grid=(t // tm,),
        in_specs=[row(d), _vmem(), _vmem()],
        out_specs=[row(d)] + [row(n) for n in splits],
        out_shape=[_sds((t, d), BF16)] + [_sds((t, n), F32) for n in splits],
        compiler_params=_params("parallel"),
    )(xin, vec, w)


def _inv_count(first_row, tm, window):
    t = first_row + lax.broadcasted_iota(jnp.int32, (tm, 1), 0)
    return 1.0 / jnp.minimum(t + 1, window).astype(F32)


def _mixer_fwd(u_pool, proj5, x, vec, w_pool, w_bout, w_o, tm):
    t, d = x.shape
    gw = d // len(POOL_WINDOWS)

    def body(up_ref, p5_ref, x_ref, vec_ref, wp_ref, wb_ref, wo_ref,
             pg_ref, q_ref, yar_ref, yb_ref, mg_ref, o_ref, x1_ref, ubuf, pbuf):
        i = pl.program_id(0)

        @pl.when(i == 0)
        def _():
            ubuf[0:POOL_HALO, :] = jnp.zeros((POOL_HALO, d), F32)
            pbuf[0:CONV_HALO, :] = jnp.zeros((CONV_HALO, d), F32)

        ubuf[POOL_HALO:, :] = up_ref[...]
        for g, window in enumerate(POOL_WINDOWS):
            cols = slice(g * gw, (g + 1) * gw)
            u = ubuf[:, cols]
            s, shift = u, 1
            while shift < window:
                s = s + pltpu.roll(s, shift, 0)
                shift *= 2
            pg = s[POOL_HALO:, :] * _inv_count(i * tm, tm, window) - u[POOL_HALO:, :]
            pgb = pg.astype(BF16)
            pg_ref[:, cols] = pgb
            yar_ref[:, cols] = _dot(pgb, wp_ref[g])
        ubuf[0:POOL_HALO, :] = ubuf[tm:tm + POOL_HALO, :]

        ux, ub, uc = p5_ref[:, 0:d], p5_ref[:, d:2 * d], p5_ref[:, 2 * d:3 * d]
        p = uc * ux
        pbuf[CONV_HALO:, :] = p
        pp = pbuf[...]
        cv = vec_ref[3:4, :] + vec_ref[4:5, :] * pltpu.roll(pp, 2, 0)[CONV_HALO:, :]
        cv = cv + vec_ref[5:6, :] * pltpu.roll(pp, 1, 0)[CONV_HALO:, :]
        cv = cv + vec_ref[6:7, :] * p
        pbuf[0:CONV_HALO, :] = pbuf[tm:tm + CONV_HALO, :]
        qb = (ub * cv).astype(BF16)
        q_ref[...] = qb
        yb = _dot(qb, wb_ref[...])
        yb_ref[...] = yb

        ya = yar_ref[...] * vec_ref[2:3, :]
        merged = jax.nn.sigmoid(p5_ref[:, 3 * d:4 * d]) * ya + jax.nn.sigmoid(p5_ref[:, 4 * d:5 * d]) * yb
        mb = merged.astype(BF16)
        mg_ref[...] = mb
        o = _dot(mb, wo_ref[...])
        o_ref[...] = o
        r2 = lax.rsqrt(_rowmean(o * o) + EPS)
        x1_ref[...] = x_ref[...] + vec_ref[0:1, :] * ((o * r2) * vec_ref[1:2, :])

    row = lambda n: pl.BlockSpec((tm, n), lambda i: (i, 0))
    return pl.pallas_call(
        body, name="mixer_fwd", grid=(t // tm,),
        in_specs=[row(d), row(5 * d), row(d), _vmem(), _vmem(), _vmem(), _vmem()],
        out_specs=[row(d)] * 7,
        out_shape=[_sds((t, d), BF16), _sds((t, d), BF16), _sds((t, d), F32), _sds((t, d), F32),
                   _sds((t, d), BF16), _sds((t, d), F32), _sds((t, d), F32)],
        scratch_shapes=[pltpu.VMEM((POOL_HALO + tm, d), F32), pltpu.VMEM((CONV_HALO + tm, d), F32)],
        compiler_params=_params("arbitrary"),
    )(u_pool, proj5, x, vec, w_pool, w_bout, w_o)


def _conv3(buf, w0, w1, w2, bias):
    v = buf[...]
    cur = v[CONV_HALO:, :]
    m1 = pltpu.roll(v, 1, 0)[CONV_HALO:, :]
    m2 = pltpu.roll(v, 2, 0)[CONV_HALO:, :]
    y = bias + w0 * m2
    y = y + w1 * m1
    y = y + w2 * cur
    return y, cur, m1, m2


def _gelu_parts(u):
    th = jnp.tanh(GELU_C0 * (u + GELU_C1 * (u * u * u)))
    cdf = 0.5 * (1.0 + th)
    return cdf, th


def _ffn_fwd(upre, x1, target, vec, fcv, w_down, tm):
    t, d = x1.shape
    fp = w_down.shape[0]
    cw = fp // 4

    def body(up_ref, x1_ref, tg_ref, vec_ref, fcv_ref, wd_ref, a_ref, ff_ref, dy_ref, loss_ref,
             gbuf, vbuf, carry):
        i = pl.program_id(0)

        @pl.when(i == 0)
        def _():
            carry[...] = jnp.zeros_like(carry)
            loss_ref[...] = jnp.zeros_like(loss_ref)

        taps = lambda col: (fcv_ref[0:1, col], fcv_ref[1:2, col], fcv_ref[2:3, col])
        for j in range(4):
            gc = slice(j * cw, (j + 1) * cw)
            vc = slice(fp + j * cw, fp + (j + 1) * cw)
            gbuf[0:CONV_HALO, :] = carry[:, gc]
            vbuf[0:CONV_HALO, :] = carry[:, vc]
            gbuf[CONV_HALO:, :] = up_ref[:, gc]
            vbuf[CONV_HALO:, :] = up_ref[:, vc]
            gate = _conv3(gbuf, *taps(gc), fcv_ref[3:4, gc])[0]
            val = _conv3(vbuf, *taps(vc), fcv_ref[3:4, vc])[0]
            cdf, _ = _gelu_parts(gate)
            a_ref[:, gc] = ((gate * cdf) * val).astype(BF16)
            carry[:, gc] = gbuf[tm:tm + CONV_HALO, :]
            carry[:, vc] = vbuf[tm:tm + CONV_HALO, :]
        ff = _dot(a_ref[...], wd_ref[...])
        ff_ref[...] = ff
        r4 = lax.rsqrt(_rowmean(ff * ff) + EPS)
        y = x1_ref[...] + vec_ref[0:1, :] * ((ff * r4) * vec_ref[1:2, :])
        e = y - tg_ref[...]
        dy_ref[...] = e * (1.0 / d)
        loss_ref[...] += jnp.sum(_rowmean(e * e))

    row = lambda n: pl.BlockSpec((tm, n), lambda i: (i, 0))
    return pl.pallas_call(
        body, name="ffn_fwd", grid=(t // tm,),
        in_specs=[row(2 * fp), row(d), row(d), _vmem(), _vmem(), _vmem()],
        out_specs=[row(fp), row(d), row(d), pl.BlockSpec((8, LANES), lambda i: (0, 0))],
        out_shape=[_sds((t, fp), BF16), _sds((t, d), F32), _sds((t, d), F32), _sds((8, LANES), F32)],
        scratch_shapes=[pltpu.VMEM((CONV_HALO + tm, cw), F32), pltpu.VMEM((CONV_HALO + tm, cw), F32),
                        pltpu.VMEM((CONV_HALO, 2 * fp), F32)],
        compiler_params=_params("arbitrary"),
    )(upre, x1, target, vec, fcv, w_down)


def _conv3_bwd(dbuf, w0, w1, w2, tm):
    v = dbuf[...]
    n = v.shape[0]
    dx = w2 * v[0:tm, :]
    dx = dx + w1 * pltpu.roll(v, n - 1, 0)[0:tm, :]
    dx = dx + w0 * pltpu.roll(v, n - 2, 0)[0:tm, :]
    return dx


def _ffn_bwd(dy, ff, upre, vec, fcv, w_down, tm):
    t, d = dy.shape
    fp = w_down.shape[0]
    cw = fp // 4
    nt = t // tm
    halo_blocks = tm // CONV_HALO

    def body(dy_ref, ff_ref, up_ref, halo_ref, vec_ref, fcv_ref, wd_ref,
             dff_ref, dup_ref, red_ref, cred_ref, gbuf, vbuf, dgbuf, dvbuf, carry):
        i = pl.program_id(0)
        tix = nt - 1 - i

        @pl.when(i == 0)
        def _():
            carry[...] = jnp.zeros_like(carry)
            red_ref[...] = jnp.zeros_like(red_ref)
            cred_ref[...] = jnp.zeros_like(cred_ref)

        ff = ff_ref[...]
        dy_v = dy_ref[...]
        r4 = lax.rsqrt(_rowmean(ff * ff) + EPS)
        nh = ff * r4
        gt2, gpost = vec_ref[0:1, :], vec_ref[1:2, :]
        dn = dy_v * gt2
        red_ref[0:1, :] += _colsum(dn * nh)
        red_ref[1:2, :] += _colsum(dy_v * (nh * gpost))
        dnh = dn * gpost
        dff = r4 * (dnh - nh * _rowmean(dnh * nh))
        dffb = dff.astype(BF16)
        dff_ref[...] = dffb

        taps = lambda col: (fcv_ref[0:1, col], fcv_ref[1:2, col], fcv_ref[2:3, col])
        keep = (tix > 0).astype(F32)
        for j in range(4):
            gc = slice(j * cw, (j + 1) * cw)
            vc = slice(fp + j * cw, fp + (j + 1) * cw)
            da = _dot_nt(dffb, wd_ref[gc, :])
            gbuf[0:CONV_HALO, :] = halo_ref[:, gc] * keep
            vbuf[0:CONV_HALO, :] = halo_ref[:, vc] * keep
            gbuf[CONV_HALO:, :] = up_ref[:, gc]
            vbuf[CONV_HALO:, :] = up_ref[:, vc]
            gate, g0, g1, g2 = _conv3(gbuf, *taps(gc), fcv_ref[3:4, gc])
            val, v0, v1, v2 = _conv3(vbuf, *taps(vc), fcv_ref[3:4, vc])
            cdf, th = _gelu_parts(gate)
            dcdf = 0.5 * (1.0 - th * th) * (GELU_C0 * (1.0 + (3.0 * GELU_C1) * (gate * gate)))
            dgate = da * val * (cdf + gate * dcdf)
            dval = da * (gate * cdf)
            cred_ref[3:4, gc] += _colsum(dgate)
            cred_ref[3:4, vc] += _colsum(dval)
            cred_ref[0:1, gc] += _colsum(dgate * g2)
            cred_ref[1:2, gc] += _colsum(dgate * g1)
            cred_ref[2:3, gc] += _colsum(dgate * g0)
            cred_ref[0:1, vc] += _colsum(dval * v2)
            cred_ref[1:2, vc] += _colsum(dval * v1)
            cred_ref[2:3, vc] += _colsum(dval * v0)
            dgbuf[0:tm, :] = dgate
            dvbuf[0:tm, :] = dval
            dgbuf[tm:tm + CONV_HALO, :] = carry[:, gc]
            dvbuf[tm:tm + CONV_HALO, :] = carry[:, vc]
            dup_ref[:, gc] = _conv3_bwd(dgbuf, *taps(gc), tm).astype(BF16)
            dup_ref[:, vc] = _conv3_bwd(dvbuf, *taps(vc), tm).astype(BF16)
            carry[:, gc] = dgbuf[0:CONV_HALO, :]
            carry[:, vc] = dvbuf[0:CONV_HALO, :]

    rev = lambda n: pl.BlockSpec((tm, n), lambda i: (nt - 1 - i, 0))
    halo = pl.BlockSpec((CONV_HALO, 2 * fp), lambda i: (jnp.maximum((nt - 1 - i) * halo_blocks - 1, 0), 0))
    fixed = lambda n: pl.BlockSpec((8, n), lambda i: (0, 0))
    return pl.pallas_call(
        body, name="ffn_bwd", grid=(nt,),
        in_specs=[rev(d), rev(d), rev(2 * fp), halo, _vmem(), _vmem(), _vmem()],
        out_specs=[rev(d), rev(2 * fp), fixed(d), fixed(2 * fp)],
        out_shape=[_sds((t, d), BF16), _sds((t, 2 * fp), BF16), _sds((8, d), F32), _sds((8, 2 * fp), F32)],
        scratch_shapes=[pltpu.VMEM((CONV_HALO + tm, cw), F32), pltpu.VMEM((CONV_HALO + tm, cw), F32),
                        pltpu.VMEM((tm + CONV_HALO, cw), F32), pltpu.VMEM((tm + CONV_HALO, cw), F32),
                        pltpu.VMEM((CONV_HALO, 2 * fp), F32)],
        compiler_params=_params("arbitrary"),
    )(dy, ff, upre, upre, vec, fcv, w_down)


def _dgrad_norm_bwd(dz, w, xin, dres, vec, tm, name):
    t, d = xin.shape
    n = dz.shape[1]

    def body(dz_ref, w_ref, x_ref, dr_ref, vec_ref, dx_ref, red_ref):
        @pl.when(pl.program_id(0) == 0)
        def _():
            red_ref[...] = jnp.zeros_like(red_ref)

        dh = _dot_nt(dz_ref[...], w_ref[...])
        x = x_ref[...]
        r = lax.rsqrt(_rowmean(x * x) + EPS)
        nh = x * r
        g, sc1 = vec_ref[0:1, :], vec_ref[1:2, :]
        red_ref[0:1, :] += _colsum(dh)
        red_ref[1:2, :] += _colsum(dh * (nh * g))
        red_ref[2:3, :] += _colsum(dh * nh * sc1)
        dnh = dh * g * sc1
        dx_ref[...] = dr_ref[...] + r * (dnh - nh * _rowmean(dnh * nh))

    row = lambda k: pl.BlockSpec((tm, k), lambda i: (i, 0))
    return pl.pallas_call(
        body, name=name, grid=(t // tm,),
        in_specs=[row(n), _vmem(), row(d), row(d), _vmem()],
        out_specs=[row(d), pl.BlockSpec((8, d), lambda i: (0, 0))],
        out_shape=[_sds((t, d), F32), _sds((8, d), F32)],
        compiler_params=_params("arbitrary"),
    )(dz, w, xin, dres, vec)


def _mixer_bwd(dx1, o, yar, yb, proj5, vec, w_pool, w_bout, w_o, tm):
    t, d = dx1.shape
    gw = d // len(POOL_WINDOWS)
    nt = t // tm
    halo_blocks = tm // POOL_HALO

    def body(dx1_ref, o_ref, yar_ref, yb_ref, p5_ref, halo_ref, vec_ref, wp_ref, wb_ref, wo_ref,
             do_ref, dyar_ref, dyb_ref, dp_ref, red_ref, pbuf, dpgbuf, dcvbuf):
        i = pl.program_id(0)
        tix = nt - 1 - i

        @pl.when(i == 0)
        def _():
            red_ref[...] = jnp.zeros_like(red_ref)
            dpgbuf[tm:tm + POOL_HALO, :] = jnp.zeros((POOL_HALO, d), F32)
            dcvbuf[tm:tm + CONV_HALO, :] = jnp.zeros((CONV_HALO, d), F32)

        gt1, gpost, pscale = vec_ref[0:1, :], vec_ref[1:2, :], vec_ref[2:3, :]
        dx1_v = dx1_ref[...]
        o = o_ref[...]
        r2 = lax.rsqrt(_rowmean(o * o) + EPS)
        nh = o * r2
        dn = dx1_v * gt1
        red_ref[0:1, :] += _colsum(dn * nh)
        red_ref[1:2, :] += _colsum(dx1_v * (nh * gpost))
        dnh = dn * gpost
        dob = (r2 * (dnh - nh * _rowmean(dnh * nh))).astype(BF16)
        do_ref[...] = dob
        dm = _dot_nt(dob, wo_ref[...])

        sa = jax.nn.sigmoid(p5_ref[:, 3 * d:4 * d])
        yar = yar_ref[...]
        dya = dm * sa
        dp_ref[:, 4 * d:5 * d] = (dm * (yar * pscale) * sa * (1.0 - sa)).astype(BF16)
        red_ref[2:3, :] += _colsum(dya * yar)
        dyarb = (dya * pscale).astype(BF16)
        dyar_ref[...] = dyarb
        sb = jax.nn.sigmoid(p5_ref[:, 4 * d:5 * d])
        dybb = (dm * sb).astype(BF16)
        dyb_ref[...] = dybb
        dp_ref[:, 5 * d:6 * d] = (dm * yb_ref[...] * sb * (1.0 - sb)).astype(BF16)

        n_pool = tm + POOL_HALO
        for g, window in enumerate(POOL_WINDOWS):
            cols = slice(g * gw, (g + 1) * gw)
            dpg = _dot_nt(dyarb[:, cols], wp_ref[g])
            dpgbuf[0:tm, cols] = dpg * _inv_count(tix * tm, tm, window)
            s, shift = dpgbuf[:, cols], 1
            while shift < window:
                s = s + pltpu.roll(s, n_pool - shift, 0)
                shift *= 2
            dp_ref[:, cols] = (s[0:tm, :] - dpg).astype(BF16)
        dpgbuf[tm:tm + POOL_HALO, :] = dpgbuf[0:POOL_HALO, :]

        dq = _dot_nt(dybb, wb_ref[...])
        ux, ub, uc = p5_ref[:, 0:d], p5_ref[:, d:2 * d], p5_ref[:, 2 * d:3 * d]
        keep = (tix > 0).astype(F32)
        pbuf[0:POOL_HALO, :] = (halo_ref[:, 2 * d:3 * d] * halo_ref[:, 0:d]) * keep
        p = uc * ux
        pbuf[POOL_HALO:, :] = p
        pp = pbuf[...]
        m1 = pltpu.roll(pp, 1, 0)[POOL_HALO:, :]
        m2 = pltpu.roll(pp, 2, 0)[POOL_HALO:, :]
        cv = vec_ref[3:4, :] + vec_ref[4:5, :] * m2
        cv = cv + vec_ref[5:6, :] * m1
        cv = cv + vec_ref[6:7, :] * p
        dp_ref[:, 2 * d:3 * d] = (dq * cv).astype(BF16)
        dcv = dq * ub
        red_ref[3:4, :] += _colsum(dcv)
        red_ref[4:5, :] += _colsum(dcv * m2)
        red_ref[5:6, :] += _colsum(dcv * m1)
        red_ref[6:7, :] += _colsum(dcv * p)
        dcvbuf[0:tm, :] = dcv
        dpv = _conv3_bwd(dcvbuf, vec_ref[4:5, :], vec_ref[5:6, :], vec_ref[6:7, :], tm)
        dcvbuf[tm:tm + CONV_HALO, :] = dcvbuf[0:CONV_HALO, :]
        dp_ref[:, d:2 * d] = (dpv * uc).astype(BF16)
        dp_ref[:, 3 * d:4 * d] = (dpv * ux).astype(BF16)

    rev = lambda n: pl.BlockSpec((tm, n), lambda i: (nt - 1 - i, 0))
    halo = pl.BlockSpec((POOL_HALO, 5 * d), lambda i: (jnp.maximum((nt - 1 - i) * halo_blocks - 1, 0), 0))
    return pl.pallas_call(
        body, name="mixer_bwd", grid=(nt,),
        in_specs=[rev(d), rev(d), rev(d), rev(d), rev(5 * d), halo, _vmem(), _vmem(), _vmem(), _vmem()],
        out_specs=[rev(d), rev(d), rev(d), rev(6 * d), pl.BlockSpec((16, d), lambda i: (0, 0))],
        out_shape=[_sds((t, d), BF16), _sds((t, d), BF16), _sds((t, d), BF16), _sds((t, 6 * d), BF16),
                   _sds((16, d), F32)],
        scratch_shapes=[pltpu.VMEM((POOL_HALO + tm, d), F32), pltpu.VMEM((tm + POOL_HALO, d), F32),
                        pltpu.VMEM((tm + CONV_HALO, d), F32)],
        compiler_params=_params("arbitrary"),
    )(dx1, o, yar, yb, proj5, proj5, vec, w_pool, w_bout, w_o)


def _matmul_tn(a, b, bm, bn, tk, by_col_block, name):
    t, m = a.shape
    n = b.shape[1]
    nk = t // tk

    def body(a_ref, b_ref, o_ref, acc_ref):
        k = pl.program_id(2)

        @pl.when(k == 0)
        def _():
            acc_ref[...] = jnp.zeros_like(acc_ref)

        acc_ref[...] += _dot_tn(a_ref[...], b_ref[...])

        @pl.when(k == nk - 1)
        def _():
            o_ref[...] = acc_ref[...].astype(o_ref.dtype)

    if by_col_block:
        out_shape = _sds((n // bn, m, bn), BF16)
        out_spec = pl.BlockSpec((None, bm, bn), lambda i, j, k: (j, i, 0))
    else:
        out_shape = _sds((m, n), BF16)
        out_spec = pl.BlockSpec((bm, bn), lambda i, j, k: (i, j))
    return pl.pallas_call(
        body, name=name, grid=(m // bm, n // bn, nk),
        in_specs=[pl.BlockSpec((tk, bm), lambda i, j, k: (k, i)), pl.BlockSpec((tk, bn), lambda i, j, k: (k, j))],
        out_specs=out_spec, out_shape=out_shape,
        scratch_shapes=[pltpu.VMEM((bm, bn), F32)],
        compiler_params=_params("parallel", "parallel", "arbitrary"),
    )(a, b)


def _matmul_tn_groups(a, b, groups, tk, name):
    t, m = a.shape
    w = m // groups
    nk = t // tk

    def body(a_ref, b_ref, o_ref, acc_ref):
        k = pl.program_id(1)

        @pl.when(k == 0)
        def _():
            acc_ref[...] = jnp.zeros_like(acc_ref)

        acc_ref[...] += _dot_tn(a_ref[...], b_ref[...])

        @pl.when(k == nk - 1)
        def _():
            o_ref[...] = acc_ref[...].astype(o_ref.dtype)

    blk = pl.BlockSpec((tk, w), lambda g, k: (k, g))
    return pl.pallas_call(
        body, name=name, grid=(groups, nk), in_specs=[blk, blk],
        out_specs=pl.BlockSpec((None, w, w), lambda g, k: (g, 0, 0)), out_shape=_sds((groups, w, w), BF16),
        scratch_shapes=[pltpu.VMEM((w, w), F32)],
        compiler_params=_params("parallel", "arbitrary"),
    )(a, b)


def _round_up(n, k):
    return (n + k - 1) // k * k


def _pad_last(a, parts, width):
    lead = a.shape[:-1]
    piece = a.shape[-1] // parts
    a = a.reshape(lead + (parts, piece))
    a = jnp.pad(a, [(0, 0)] * len(lead) + [(0, 0), (0, width - piece)])
    return a.reshape(lead + (parts * width,))


def _unpad_last(a, parts, piece):
    lead = a.shape[:-1]
    width = a.shape[-1] // parts
    return a.reshape(lead + (parts, width))[..., :piece].reshape(lead + (parts * piece,))


def _rows8(rows, width):
    n = _round_up(len(rows), 8)
    rows = list(rows) + [jnp.zeros((1, width), F32)] * (n - len(rows))
    return jnp.concatenate(rows, axis=0)


def kernel(x, c, g_pre_mix, g_post_mix, g_pre_ffn, g_post_ffn, w_ada, b_ada, w_in, w_pool, pool_scale, conv_w, conv_b, w_bout, w_o, w_up, ffn_conv_w, ffn_conv_b, w_down, loss_target, m_g_pre_mix, m_g_post_mix, m_g_pre_ffn, m_g_post_ffn, m_w_ada, m_b_ada, m_w_in, m_w_pool, m_pool_scale, m_conv_w, m_conv_b, m_w_bout, m_w_o, m_w_up, m_ffn_conv_w, m_ffn_conv_b, m_w_down, v_g_pre_mix, v_g_post_mix, v_g_pre_ffn, v_g_post_ffn, v_w_ada, v_b_ada, v_w_in, v_w_pool, v_pool_scale, v_conv_w, v_conv_b, v_w_bout, v_w_o, v_w_up, v_ffn_conv_w, v_ffn_conv_b, v_w_down):
    t, d = x.shape[1], x.shape[2]
    ngroups = len(POOL_WINDOWS)
    gw = d // ngroups
    ada_n = w_ada.shape[2]
    in_n = w_in.shape[2]
    up_n = w_up.shape[2]
    unit = w_down.shape[1]
    unit_p = _round_up(unit, LANES)
    fp = NDEV * unit_p
    ff2 = NDEV * up_n
    tm = min(256, t)
    tm_big = min(512, t)

    xi, yi, ci = _position()
    me = _linear(xi, yi, ci)
    chip = 2 * xi + yi
    core = jnp.reshape(ci, (1,)).astype(jnp.int32)
    sel = jnp.stack([2 * chip + ci, chip]).astype(jnp.int32)

    x2 = x.reshape(t, d)
    target = loss_target.reshape(t, d)

    cw_n = conv_w.shape[2]
    fcw_p = _pad_last(ffn_conv_w[0], 2, unit_p)
    pack = jnp.concatenate([c.reshape(1, d), conv_w[0].reshape(1, 3 * cw_n), fcw_p.reshape(1, 6 * unit_p)], axis=1)
    pack_n = _round_up(pack.shape[1], LANES)
    pack = jnp.pad(pack, ((0, 0), (0, pack_n - pack.shape[1])))
    pack = jnp.pad(pack, ((0, 7), (0, 0)))
    gathered = _small_allgather(pack, "allgather_cond", False)[0][:, 0, :]
    c_all = gathered[:, :d]
    conv_w_full = gathered[:, d:d + 3 * cw_n].reshape(NDEV, 3, cw_n).transpose(1, 0, 2).reshape(3, NDEV * cw_n)
    fcw_full = gathered[:, d + 3 * cw_n:d + 3 * cw_n + 6 * unit_p].reshape(NDEV, 3, 2 * unit_p)
    fcw_full = fcw_full.transpose(1, 0, 2).reshape(3, 2 * fp)
    fcb_full = _pad_last(ffn_conv_b, 2 * NDEV, unit_p)
    fcv = jnp.concatenate([fcw_full, fcb_full, jnp.zeros((4, 2 * fp), F32)], axis=0)

    c16 = jnp.pad(c_all, ((0, 8), (0, 0))).astype(BF16)
    b_piece = lax.dynamic_slice_in_dim(b_ada, me * ada_n, ada_n, axis=1)
    mod_piece = _mod_piece(c16, w_ada[0], b_piece)[:NDEV]
    mod_all = _small_allgather(mod_piece, "allgather_mod", False)[0]
    mod = lax.dynamic_index_in_dim(mod_all, me, axis=1, keepdims=False).reshape(1, NDEV * ada_n)
    sh1, sc1, gt1, sh2, sc2, gt2 = [mod[:, k * d:(k + 1) * d] for k in range(6)]

    w_in_s = w_in[0].astype(BF16)
    w_up_s = _pad_last(w_up[0], 2, unit_p).astype(BF16)
    w_down_s = jnp.pad(w_down[0], ((0, unit_p - unit), (0, 0))).astype(BF16)
    w_bout_s = w_bout[0].astype(BF16)
    w_o_s = w_o[0].astype(BF16)
    w_pool_s = w_pool[0].reshape(-1, gw).astype(BF16)
    g_in, g_up, g_down, g_bout, g_o, g_pool = _allgather_weights(
        [w_in_s, w_up_s, w_down_s, w_bout_s, w_o_s, w_pool_s])
    w_in_f = g_in.transpose(1, 0, 2).reshape(d, NDEV * in_n)
    w_up_f = g_up.transpose(1, 0, 2).reshape(d, 2 * fp)
    w_down_f = g_down.reshape(fp, d)
    w_bout_f = g_bout.reshape(d, d)
    w_o_f = g_o.reshape(d, d)
    w_pool_f = g_pool.reshape(NDEV, ngroups, gw // NDEV, gw).transpose(1, 0, 2, 3).reshape(ngroups, gw, gw)

    vec_pre_mix = _rows8([g_pre_mix, 1.0 + sc1, sh1], d)
    h1b, u_pool, proj5 = _norm_matmul(x2, vec_pre_mix, w_in_f, (d, 5 * d), tm_big, "in_proj")
    vec_mix = _rows8([gt1, g_post_mix, pool_scale, conv_b, conv_w_full[0:1], conv_w_full[1:2], conv_w_full[2:3]], d)
    pgb, qb, yar, yb, mergedb, o, x1 = _mixer_fwd(u_pool, proj5, x2, vec_mix, w_pool_f, w_bout_f, w_o_f, tm)
    vec_pre_ffn = _rows8([g_pre_ffn, 1.0 + sc2, sh2], d)
    h2b, upre = _norm_matmul(x1, vec_pre_ffn, w_up_f, (2 * fp,), tm_big, "up_proj")
    vec_ffn = _rows8([gt2, g_post_ffn], d)
    ab, ff, dy, loss_part = _ffn_fwd(upre, x1, target, vec_ffn, fcv, w_down_f, tm)
    loss = lax.psum(0.5 * loss_part[0, 0], ("x", "y", "c"))

    dffb, dupre, red_ffn, red_fconv = _ffn_bwd(dy, ff, upre, vec_ffn, fcv, w_down_f, tm)
    dx1, red_pre_ffn = _dgrad_norm_bwd(dupre, w_up_f, x1, dy, vec_pre_ffn, tm_big, "up_proj_bwd")
    dob, dyarb, dybb, dproj, red_mix = _mixer_bwd(dx1, o, yar, yb, proj5, vec_mix, w_pool_f, w_bout_f, w_o_f, tm)
    grad_x, red_pre_mix = _dgrad_norm_bwd(dproj, w_in_f, x2, dx1, vec_pre_mix, tm_big, "in_proj_bwd")

    tk = min(512, t)
    gr_in = _matmul_tn(h1b, dproj, d, in_n, tk, True, "wgrad_in")
    gr_up = _matmul_tn(h2b, dupre, d, 2 * unit_p, tk, True, "wgrad_up")
    gr_down = _matmul_tn(ab, dffb, 2 * unit_p, d, tk, False, "wgrad_down").reshape(NDEV, unit_p, d)
    gr_bout = _matmul_tn(qb, dybb, d, d, tk, False, "wgrad_bout").reshape(NDEV, d // NDEV, d)
    gr_o = _matmul_tn(mergedb, dob, d, d, tk, False, "wgrad_o").reshape(NDEV, d // NDEV, d)
    gr_pool = _matmul_tn_groups(pgb, dyarb, ngroups, tk, "wgrad_pool")
    gr_pool = gr_pool.reshape(ngroups, NDEV, gw // NDEV, gw).transpose(1, 0, 2, 3).reshape(NDEV, -1, gw)

    partials = [gr_in, gr_up, gr_down, gr_bout, gr_o, gr_pool]
    from_sibling = _exchange_sibling(partials)
    chip_sums = [_chip_sum(g, s, core) for g, s in zip(partials, from_sibling)]
    from_chips = _exchange_chips(chip_sums)

    def big(k, w, m, v, name, col_parts=0):
        shape = w.shape
        w2, m2, v2 = [a.reshape((-1, shape[-1])) for a in (w, m, v)]
        if col_parts:
            w2, m2, v2 = [_pad_last(a, col_parts, partials[k].shape[2] // col_parts) for a in (w2, m2, v2)]
        outs = _reduce_adamw(partials[k], from_sibling[k], from_chips[k], sel, w2, m2, v2, name)
        if col_parts:
            outs = [_unpad_last(a, col_parts, shape[-1] // col_parts) for a in outs]
        return [a.reshape(shape) for a in outs]

    g_w_in, d_w_in, nm_w_in, nv_w_in = big(0, w_in, m_w_in, v_w_in, "adamw_in")
    g_w_up, d_w_up, nm_w_up, nv_w_up = big(1, w_up, m_w_up, v_w_up, "adamw_up", col_parts=2)
    g_w_down, d_w_down, nm_w_down, nv_w_down = big(2, w_down, m_w_down, v_w_down, "adamw_down")
    g_w_bout, d_w_bout, nm_w_bout, nv_w_bout = big(3, w_bout, m_w_bout, v_w_bout, "adamw_bout")
    g_w_o, d_w_o, nm_w_o, nv_w_o = big(4, w_o, m_w_o, v_w_o, "adamw_o")
    g_w_pool, d_w_pool, nm_w_pool, nv_w_pool = big(5, w_pool, m_w_pool, v_w_pool, "adamw_pool")

    dmod = [red_pre_mix[0:1], red_pre_mix[1:2], red_mix[1:2], red_pre_ffn[0:1], red_pre_ffn[1:2], red_ffn[1:2]]
    small = [red_pre_mix[2:3], red_mix[0:1], red_pre_ffn[2:3], red_ffn[0:1], red_mix[2:3], red_mix[3:4],
             red_mix[4:5], red_mix[5:6], red_mix[6:7]] + dmod
    flat = jnp.concatenate(small + [red_fconv[0:4].reshape(1, 8 * fp)], axis=1)
    flat_n = flat.shape[1]
    width = 8 * LANES
    rows = _round_up(-(-flat_n // width), 8)
    flat = jnp.pad(flat, ((0, 0), (0, rows * width - flat_n))).reshape(rows, width)
    gat, tot = _small_allgather(flat, "allreduce_small", True)
    tot = tot.reshape(1, rows * width)
    gat = gat.reshape(NDEV, rows * width)
    take = lambda k: tot[:, k * d:(k + 1) * d]
    g_g_pre_mix, g_g_post_mix, g_g_pre_ffn, g_g_post_ffn, g_pool_scale, g_conv_b = [take(k) for k in range(6)]
    g_conv_w_full = jnp.concatenate([take(6), take(7), take(8)], axis=0)
    g_conv_w = lax.dynamic_slice_in_dim(g_conv_w_full, me * cw_n, cw_n, axis=1)
    g_b_ada = tot[:, 9 * d:15 * d]
    dmod_all = gat[:, 9 * d:15 * d]
    fconv_tot = tot[:, 15 * d:15 * d + 8 * fp].reshape(4, 2 * fp)
    g_ffn_conv_b = _unpad_last(fconv_tot[3:4], 2 * NDEV, unit)
    g_fcw_mine = lax.dynamic_slice_in_dim(fconv_tot[0:3], me * 2 * unit_p, 2 * unit_p, axis=1)
    g_ffn_conv_w = _unpad_last(g_fcw_mine, 2, unit)
    dmod_piece = lax.dynamic_slice_in_dim(dmod_all, me * ada_n, ada_n, axis=1)
    g_w_ada = _wada_grad(c16, jnp.pad(dmod_piece, ((0, 8), (0, 0))).astype(BF16))

    names_small = [(g_pre_mix, g_g_pre_mix, m_g_pre_mix, v_g_pre_mix), (g_post_mix, g_g_post_mix, m_g_post_mix, v_g_post_mix),
                   (g_pre_ffn, g_g_pre_ffn, m_g_pre_ffn, v_g_pre_ffn), (g_post_ffn, g_g_post_ffn, m_g_post_ffn, v_g_post_ffn),
                   (b_ada, g_b_ada, m_b_ada, v_b_ada), (pool_scale, g_pool_scale, m_pool_scale, v_pool_scale),
                   (conv_w, g_conv_w, m_conv_w, v_conv_w), (conv_b, g_conv_b, m_conv_b, v_conv_b),
                   (ffn_conv_w, g_ffn_conv_w, m_ffn_conv_w, v_ffn_conv_w), (ffn_conv_b, g_ffn_conv_b, m_ffn_conv_b, v_ffn_conv_b)]
    sizes = [w.size for w, _, _, _ in names_small]
    total = sum(sizes)
    prow = _round_up(-(-total // width), 8)

    def pack_small(k):
        a = jnp.concatenate([q[k].reshape(1, -1) for q in names_small], axis=1)
        return jnp.pad(a, ((0, 0), (0, prow * width - total)), constant_values=1.0).reshape(prow, width)

    ds, ms, vs = _adamw(pack_small(0), pack_small(1), pack_small(2), pack_small(3), "adamw_small")

    def unpack_small(a):
        a = a.reshape(-1)
        out, off = [], 0
        for (w, _, _, _), n in zip(names_small, sizes):
            out.append(a[off:off + n].reshape(w.shape))
            off += n
        return out

    (d_g_pre_mix, d_g_post_mix, d_g_pre_ffn, d_g_post_ffn, d_b_ada, d_pool_scale, d_conv_w, d_conv_b,
     d_ffn_conv_w, d_ffn_conv_b) = unpack_small(ds)
    (nm_g_pre_mix, nm_g_post_mix, nm_g_pre_ffn, nm_g_post_ffn, nm_b_ada, nm_pool_scale, nm_conv_w, nm_conv_b,
     nm_ffn_conv_w, nm_ffn_conv_b) = unpack_small(ms)
    (nv_g_pre_mix, nv_g_post_mix, nv_g_pre_ffn, nv_g_post_ffn, nv_b_ada, nv_pool_scale, nv_conv_w, nv_conv_b,
     nv_ffn_conv_w, nv_ffn_conv_b) = unpack_small(vs)
    d_w_ada, nm_w_ada, nv_w_ada = [a.reshape(w_ada.shape) for a in
                                   _adamw(w_ada[0], g_w_ada, m_w_ada[0], v_w_ada[0], "adamw_ada")]

    grads = [g_g_pre_mix, g_g_post_mix, g_g_pre_ffn, g_g_post_ffn, g_w_ada.reshape(w_ada.shape), g_b_ada, g_w_in,
             g_w_pool, g_pool_scale, g_conv_w.reshape(conv_w.shape), g_conv_b, g_w_bout, g_w_o, g_w_up,
             g_ffn_conv_w.reshape(ffn_conv_w.shape), g_ffn_conv_b, g_w_down]
    deltas = [d_g_pre_mix, d_g_post_mix, d_g_pre_ffn, d_g_post_ffn, d_w_ada, d_b_ada, d_w_in, d_w_pool, d_pool_scale,
              d_conv_w, d_conv_b, d_w_bout, d_w_o, d_w_up, d_ffn_conv_w, d_ffn_conv_b, d_w_down]
    new_m = [nm_g_pre_mix, nm_g_post_mix, nm_g_pre_ffn, nm_g_post_ffn, nm_w_ada, nm_b_ada, nm_w_in, nm_w_pool,
             nm_pool_scale, nm_conv_w, nm_conv_b, nm_w_bout, nm_w_o, nm_w_up, nm_ffn_conv_w, nm_ffn_conv_b, nm_w_down]
    new_v = [nv_g_pre_mix, nv_g_post_mix, nv_g_pre_ffn, nv_g_post_ffn, nv_w_ada, nv_b_ada, nv_w_in, nv_w_pool,
             nv_pool_scale, nv_conv_w, nv_conv_b, nv_w_bout, nv_w_o, nv_w_up, nv_ffn_conv_w, nv_ffn_conv_b, nv_w_down]
    return (loss, grad_x.reshape(x.shape), *grads, *deltas, *new_m, *new_v)
```

```python
import functools
import math

import jax
import jax.numpy as jnp
from jax import lax
from jax.experimental import pallas as pl
from jax.experimental.pallas import tpu as pltpu

F32 = jnp.float32
BF16 = jnp.bfloat16
MESH = pl.DeviceIdType.MESH

NDEV = 8
NCHIP = 4
EPS = 1e-6
POOL_WINDOWS = (2, 4, 8, 16)
LANES = 128
ADAM_LR = 0.001
ADAM_B1 = 0.9
ADAM_B2 = 0.999
ADAM_EPS = 1e-08
ADAM_WD = 0.01
ADAM_STEP = 10
GELU_C0 = math.sqrt(2.0 / math.pi)
GELU_C1 = 0.044715
VMEM_LIMIT = 56 * 2**20


def _vmem():
    return pl.BlockSpec(memory_space=pltpu.VMEM)


def _any():
    return pl.BlockSpec(memory_space=pl.ANY)


def _params(*sem):
    return pltpu.CompilerParams(dimension_semantics=sem, vmem_limit_bytes=VMEM_LIMIT)


def _sds(shape, dtype):
    return jax.ShapeDtypeStruct(tuple(shape), dtype)


def _position():
    return lax.axis_index("x"), lax.axis_index("y"), lax.axis_index("c")


def _linear(x, y, c):
    return 4 * x + 2 * y + c


def _dot(a, b):
    return jnp.dot(a, b, preferred_element_type=F32)


def _dot_nt(a, b):
    return lax.dot_general(a, b, (((1,), (1,)), ((), ())), preferred_element_type=F32)


def _dot_tn(a, b):
    return lax.dot_general(a, b, (((0,), (0,)), ((), ())), preferred_element_type=F32)


def _colsum(v):
    return jnp.sum(v, axis=0, keepdims=True)


def _rowmean(v):
    return jnp.mean(v, axis=-1, keepdims=True)


GROUP = 256


def _interleave(v):
    g, n = v.shape
    return jnp.swapaxes(v.reshape(8, g // 8, n), 0, 1).reshape(g, n)


def _deinterleave(v):
    g, n = v.shape
    return jnp.swapaxes(v.reshape(g // 8, 8, n), 0, 1).reshape(g, n)


def _halo_top(cur_last, prev_last):
    rows, n = cur_last.shape
    c3 = cur_last.reshape(rows // 8, 8, n)
    p3 = prev_last.reshape(rows // 8, 8, n)
    sub = lax.broadcasted_iota(jnp.int32, c3.shape, 1)
    return jnp.where(sub == 0, pltpu.roll(p3, 1, 1), pltpu.roll(c3, 1, 1)).reshape(rows, n)


def _halo_bottom(cur_first, next_first):
    rows, n = cur_first.shape
    c3 = cur_first.reshape(rows // 8, 8, n)
    n3 = next_first.reshape(rows // 8, 8, n)
    sub = lax.broadcasted_iota(jnp.int32, c3.shape, 1)
    return jnp.where(sub == 7, pltpu.roll(n3, 7, 1), pltpu.roll(c3, 7, 1)).reshape(rows, n)


def _shift_down(v, halo, k):
    rows = v.shape[0]
    return jnp.concatenate([halo[halo.shape[0] - 8 * k:, :], v[:rows - 8 * k, :]], axis=0)


def _shift_up(v, halo, k):
    return jnp.concatenate([v[8 * k:, :], halo[:8 * k, :]], axis=0)


def _inv_count(first_token, window):
    row = lax.broadcasted_iota(jnp.int32, (GROUP, 1), 0)
    t = first_token + (row % 8) * (GROUP // 8) + row // 8
    return 1.0 / jnp.minimum(t + 1, window).astype(F32)


def _peers(x, y, c):
    out = []
    for k in range(1, NDEV):
        out.append(((1 - x) if k & 4 else x, (1 - y) if k & 2 else y, (1 - c) if k & 1 else c))
    return out


def _small_allgather(v, name, with_sum):
    r, n = v.shape

    def body(v_ref, gat_ref, *rest):
        if with_sum:
            sum_ref, send_sems, recv_sems, local_sem = rest
        else:
            send_sems, recv_sems, local_sem = rest
        x, y, c = _position()
        me = _linear(x, y, c)
        mine = pltpu.make_async_copy(v_ref, gat_ref.at[me], local_sem)
        mine.start()
        peers = _peers(x, y, c)
        sends = []
        for k, peer in enumerate(peers):
            cp = pltpu.make_async_remote_copy(src_ref=v_ref, dst_ref=gat_ref.at[me], send_sem=send_sems.at[k],
                                              recv_sem=recv_sems.at[k], device_id=peer, device_id_type=MESH)
            cp.start()
            sends.append(cp)
        for k, peer in enumerate(peers):
            pltpu.make_async_remote_copy(src_ref=v_ref, dst_ref=gat_ref.at[_linear(*peer)], send_sem=send_sems.at[k],
                                         recv_sem=recv_sems.at[k], device_id=peer, device_id_type=MESH).wait_recv()
        for cp in sends:
            cp.wait_send()
        mine.wait()
        if with_sum:
            acc = gat_ref[0]
            for j in range(1, NDEV):
                acc = acc + gat_ref[j]
            sum_ref[...] = acc

    out_shape = [_sds((NDEV, r, n), F32)] + ([_sds((r, n), F32)] if with_sum else [])
    return pl.pallas_call(
        body, name=name, out_shape=out_shape, in_specs=[_vmem()], out_specs=[_vmem()] * len(out_shape),
        scratch_shapes=[pltpu.SemaphoreType.DMA((NDEV - 1,)), pltpu.SemaphoreType.DMA((NDEV - 1,)),
                        pltpu.SemaphoreType.DMA(())],
    )(v)


def _allgather_weights(shards):
    n = len(shards)

    def body(*refs):
        src, dst = refs[:n], refs[n:2 * n]
        send_sems, recv_sems, local_sems = refs[2 * n:]
        x, y, c = _position()
        me, sibling = (x, y, c), (x, y, 1 - c)
        chips = [(1 - x, y), (x, 1 - y), (1 - x, 1 - y)]

        def copy(a, k, block, to, from_src=False):
            blk = dst[a].at[_linear(*block)]
            return pltpu.make_async_remote_copy(src_ref=src[a] if from_src else blk, dst_ref=blk,
                                                send_sem=send_sems.at[a, k], recv_sem=recv_sems.at[a, k],
                                                device_id=to, device_id_type=MESH)

        local = [pltpu.make_async_copy(src[a], dst[a].at[_linear(*me)], local_sems.at[a]) for a in range(n)]
        for cp in local:
            cp.start()
        first = []
        for j, chip in enumerate(chips):
            for a in range(n):
                first.append(copy(a, 1 + j, me, (*chip, c), from_src=True))
        for a in range(n):
            first.append(copy(a, 0, me, sibling, from_src=True))
        for cp in first:
            cp.start()
        passed = []
        for j, chip in enumerate(chips):
            for a in range(n):
                copy(a, 1 + j, (*chip, c), me).wait_recv()
                cp = copy(a, 4 + j, (*chip, c), sibling)
                cp.start()
                passed.append(cp)
        for a in range(n):
            copy(a, 0, sibling, me).wait_recv()
        for j, chip in enumerate(chips):
            for a in range(n):
                copy(a, 4 + j, (*chip, 1 - c), me).wait_recv()
        for cp in first + passed:
            cp.wait_send()
        for cp in local:
            cp.wait()

    return pl.pallas_call(
        body, name="allgather_weights",
        out_shape=[_sds((NDEV,) + s.shape, s.dtype) for s in shards],
        in_specs=[_any()] * n, out_specs=[_any()] * n,
        scratch_shapes=[pltpu.SemaphoreType.DMA((n, 7)), pltpu.SemaphoreType.DMA((n, 7)),
                        pltpu.SemaphoreType.DMA((n,))],
    )(*shards)


def _exchange_sibling(grads):
    n = len(grads)

    def body(*refs):
        src, dst = refs[:n], refs[n:2 * n]
        send_sems, recv_sems = refs[2 * n:]
        x, y, c = _position()
        sibling = (x, y, 1 - c)
        sends = []
        for a in range(n):
            for q in range(NCHIP):
                cp = pltpu.make_async_remote_copy(src_ref=src[a].at[2 * q + 1 - c], dst_ref=dst[a].at[q],
                                                  send_sem=send_sems.at[a, q], recv_sem=recv_sems.at[a, q],
                                                  device_id=sibling, device_id_type=MESH)
                cp.start()
                sends.append(cp)
        for cp in sends:
            cp.wait_recv()
        for cp in sends:
            cp.wait_send()

    return pl.pallas_call(
        body, name="rs_exchange_sibling",
        out_shape=[_sds((NCHIP,) + g.shape[1:], g.dtype) for g in grads],
        in_specs=[_any()] * n, out_specs=[_any()] * n,
        scratch_shapes=[pltpu.SemaphoreType.DMA((n, NCHIP)), pltpu.SemaphoreType.DMA((n, NCHIP))],
    )(*grads)


def _exchange_chips(chip_sums):
    n = len(chip_sums)

    def body(*refs):
        src, dst = refs[:n], refs[n:2 * n]
        send_sems, recv_sems = refs[2 * n:]
        x, y, c = _position()
        chips = [(1 - x, y), (x, 1 - y), (1 - x, 1 - y)]
        sends = []
        for j, chip in enumerate(chips):
            for a in range(n):
                cp = pltpu.make_async_remote_copy(src_ref=src[a].at[2 * chip[0] + chip[1]], dst_ref=dst[a].at[j],
                                                  send_sem=send_sems.at[a, j], recv_sem=recv_sems.at[a, j],
                                                  device_id=(*chip, c), device_id_type=MESH)
                cp.start()
                sends.append(cp)
        for cp in sends:
            cp.wait_recv()
        for cp in sends:
            cp.wait_send()

    return pl.pallas_call(
        body, name="rs_exchange_chips",
        out_shape=[_sds((3,) + s.shape[1:], s.dtype) for s in chip_sums],
        in_specs=[_any()] * n, out_specs=[_any()] * n,
        scratch_shapes=[pltpu.SemaphoreType.DMA((n, 3)), pltpu.SemaphoreType.DMA((n, 3))],
    )(*chip_sums)


def _row_block(r):
    for rb in (512, 256, 128, 64, 32, 16):
        if r % rb == 0:
            return rb
    return r


def _chip_sum(grad, from_sibling, core):
    _, r, c = grad.shape
    rb = _row_block(r)

    def body(core_ref, g_ref, s_ref, o_ref):
        del core_ref
        o_ref[...] = (g_ref[...].astype(F32) + s_ref[...].astype(F32)).astype(o_ref.dtype)

    grid_spec = pltpu.PrefetchScalarGridSpec(
        num_scalar_prefetch=1, grid=(NCHIP, r // rb),
        in_specs=[pl.BlockSpec((None, rb, c), lambda q, i, core: (2 * q + core[0], i, 0)),
                  pl.BlockSpec((None, rb, c), lambda q, i, core: (q, i, 0))],
        out_specs=pl.BlockSpec((None, rb, c), lambda q, i, core: (q, i, 0)))
    return pl.pallas_call(body, name="rs_chip_sum", grid_spec=grid_spec, out_shape=_sds((NCHIP, r, c), BF16),
                          compiler_params=_params("parallel", "parallel"))(core, grad, from_sibling)


def _adamw_math(w, g, m, v):
    m2 = ADAM_B1 * m + (1.0 - ADAM_B1) * g
    v2 = ADAM_B2 * v + (1.0 - ADAM_B2) * jnp.square(g)
    m_hat = m2 / (1.0 - ADAM_B1 ** ADAM_STEP)
    v_hat = v2 / (1.0 - ADAM_B2 ** ADAM_STEP)
    delta = -ADAM_LR * (m_hat / (jnp.sqrt(v_hat) + ADAM_EPS) + ADAM_WD * w)
    return delta, m2, v2


def _adamw(w, g, m, v, name):
    r, c = w.shape
    rb = _row_block(r)

    def body(w_ref, g_ref, m_ref, v_ref, d_ref, m2_ref, v2_ref):
        d, m2, v2 = _adamw_math(w_ref[...], g_ref[...], m_ref[...], v_ref[...])
        d_ref[...] = d
        m2_ref[...] = m2
        v2_ref[...] = v2

    blk = pl.BlockSpec((rb, c), lambda i: (i, 0))
    return pl.pallas_call(body, name=name, grid=(r // rb,), in_specs=[blk] * 4, out_specs=[blk] * 3,
                          out_shape=[_sds((r, c), F32)] * 3, compiler_params=_params("parallel"))(w, g, m, v)


def _reduce_adamw(grad, from_sibling, from_chips, sel, w, m, v, name):
    r, c = w.shape
    rb = _row_block(r)

    def body(sel_ref, g_ref, s_ref, c0_ref, c1_ref, c2_ref, w_ref, m_ref, v_ref, go_ref, d_ref, m2_ref, v2_ref):
        del sel_ref
        g = g_ref[...].astype(F32) + s_ref[...].astype(F32)
        g = g + c0_ref[...].astype(F32)
        g = g + c1_ref[...].astype(F32)
        g = g + c2_ref[...].astype(F32)
        d, m2, v2 = _adamw_math(w_ref[...], g, m_ref[...], v_ref[...])
        go_ref[...] = g
        d_ref[...] = d
        m2_ref[...] = m2
        v2_ref[...] = v2

    blk = pl.BlockSpec((rb, c), lambda i, sel: (i, 0))
    grid_spec = pltpu.PrefetchScalarGridSpec(
        num_scalar_prefetch=1, grid=(r // rb,),
        in_specs=[pl.BlockSpec((None, rb, c), lambda i, sel: (sel[0], i, 0)),
                  pl.BlockSpec((None, rb, c), lambda i, sel: (sel[1], i, 0)),
                  pl.BlockSpec((None, rb, c), lambda i, sel: (0, i, 0)),
                  pl.BlockSpec((None, rb, c), lambda i, sel: (1, i, 0)),
                  pl.BlockSpec((None, rb, c), lambda i, sel: (2, i, 0)),
                  blk, blk, blk],
        out_specs=[blk] * 4)
    return pl.pallas_call(body, name=name, grid_spec=grid_spec, out_shape=[_sds((r, c), F32)] * 4,
                          compiler_params=_params("parallel"))(sel, grad, from_sibling, from_chips, from_chips,
                                                               from_chips, w, m, v)


def _mod_piece(c_all, w_ada, b_piece):
    rows, _ = c_all.shape
    n = w_ada.shape[1]

    def body(c_ref, w_ref, b_ref, o_ref):
        o_ref[...] = _dot(c_ref[...], w_ref[...].astype(BF16)) + b_ref[...]

    return pl.pallas_call(body, name="ada_mod", out_shape=_sds((rows, n), F32),
                          in_specs=[_vmem()] * 3, out_specs=_vmem())(c_all, w_ada, b_piece)


def _wada_grad(c_all, dmod_piece):
    d = c_all.shape[1]
    n = dmod_piece.shape[1]

    def body(c_ref, dm_ref, o_ref):
        o_ref[...] = _dot_tn(c_ref[...], dm_ref[...])

    return pl.pallas_call(body, name="ada_wgrad", out_shape=_sds((d, n), F32),
                          in_specs=[_vmem()] * 2, out_specs=_vmem())(c_all, dmod_piece)


def _norm_matmul(xin, vec, w, splits, tm, name, interleave):
    t, d = xin.shape
    ch = 1024

    def body(x_ref, vec_ref, w_ref, *outs):
        if interleave:
            xp_ref, hb_ref, *z_refs = outs
            for g0 in range(0, tm, GROUP):
                xp_ref[g0:g0 + GROUP, :] = _interleave(x_ref[g0:g0 + GROUP, :])
            x = xp_ref[...]
        else:
            hb_ref, *z_refs = outs
            x = x_ref[...]
        r = lax.rsqrt(_rowmean(x * x) + EPS)
        h = ((x * r) * vec_ref[0:1, :]) * vec_ref[1:2, :] + vec_ref[2:3, :]
        hb = h.astype(BF16)
        hb_ref[...] = hb
        off = 0
        for z_ref, width in zip(z_refs, splits):
            for c0 in range(0, width, ch):
                cw = min(ch, width - c0)
                z_ref[:, c0:c0 + cw] = _dot(hb, w_ref[:, off + c0:off + c0 + cw])
            off += width

    row = lambda n: pl.BlockSpec((tm, n), lambda i: (i, 0))
    lead = [_sds((t, d), F32)] if interleave else []
    return pl.pallas_call(
        body, name=name, grid=(t // tm,),
        in_specs=[row(d), _vmem(), _vmem()],
        out_specs=[row(d)] * (len(lead) + 1) + [row(n) for n in splits],
        out_shape=lead + [_sds((t, d), BF16)] + [_sds((t, n), F32) for n in splits],
        compiler_params=_params("parallel"),
    )(xin, vec, w)


def _conv_taps(ref, col):
    return ref[0:1, col], ref[1:2, col], ref[2:3, col]


def _mixer_fwd(u_pool, proj5, x, vec, w_pool, w_bout, w_o):
    t, d = x.shape
    tm = GROUP
    gw = d // len(POOL_WINDOWS)
    pool_rows = 8 * (POOL_WINDOWS[-1] - 1)

    def body(up_ref, p5_ref, x_ref, vec_ref, wp_ref, wb_ref, wo_ref,
             pg_ref, q_ref, cv_ref, yar_ref, yb_ref, mg_ref, o_ref, x1_ref, ucarry, pcarry):
        i = pl.program_id(0)

        @pl.when(i == 0)
        def _():
            ucarry[...] = jnp.zeros_like(ucarry)
            pcarry[...] = jnp.zeros_like(pcarry)

        for g, window in enumerate(POOL_WINDOWS):
            cols = slice(g * gw, (g + 1) * gw)
            rows = 8 * (window - 1)
            u = up_ref[:, cols]
            halo = _halo_top(u[tm - rows:, :], ucarry[pool_rows - rows:, cols])
            s, shift = jnp.concatenate([halo, u], axis=0), 1
            while shift < window:
                s = s[8 * shift:, :] + s[:s.shape[0] - 8 * shift, :]
                shift *= 2
            pg = s * _inv_count(i * tm, window) - u
            pgb = pg.astype(BF16)
            pg_ref[:, cols] = pgb
            yar_ref[:, cols] = _dot(pgb, wp_ref[g])
        ucarry[...] = up_ref[tm - pool_rows:, :]

        ux, ub, uc = p5_ref[:, 0:d], p5_ref[:, d:2 * d], p5_ref[:, 2 * d:3 * d]
        p = uc * ux
        halo = _halo_top(p[tm - 16:, :], pcarry[...])
        pcarry[...] = p[tm - 16:, :]
        w0, w1, w2 = vec_ref[4:5, :], vec_ref[5:6, :], vec_ref[6:7, :]
        cv = vec_ref[3:4, :] + w0 * _shift_down(p, halo, 2)
        cv = cv + w1 * _shift_down(p, halo, 1)
        cv = cv + w2 * p
        cv_ref[...] = cv.astype(BF16)
        qb = (ub * cv).astype(BF16)
        q_ref[...] = qb
        yb = _dot(qb, wb_ref[...])
        yb_ref[...] = yb

        ya = yar_ref[...] * vec_ref[2:3, :]
        merged = jax.nn.sigmoid(p5_ref[:, 3 * d:4 * d]) * ya + jax.nn.sigmoid(p5_ref[:, 4 * d:5 * d]) * yb
        mb = merged.astype(BF16)
        mg_ref[...] = mb
        o = _dot(mb, wo_ref[...])
        o_ref[...] = o
        r2 = lax.rsqrt(_rowmean(o * o) + EPS)
        x1_ref[...] = x_ref[...] + vec_ref[0:1, :] * ((o * r2) * vec_ref[1:2, :])

    row = lambda n: pl.BlockSpec((tm, n), lambda i: (i, 0))
    return pl.pallas_call(
        body, name="mixer_fwd", grid=(t // tm,),
        in_specs=[row(d), row(5 * d), row(d), _vmem(), _vmem(), _vmem(), _vmem()],
        out_specs=[row(d)] * 8,
        out_shape=[_sds((t, d), BF16), _sds((t, d), BF16), _sds((t, d), BF16), _sds((t, d), F32),
                   _sds((t, d), F32), _sds((t, d), BF16), _sds((t, d), F32), _sds((t, d), F32)],
        scratch_shapes=[pltpu.VMEM((pool_rows, d), F32), pltpu.VMEM((16, d), F32)],
        compiler_params=_params("arbitrary"),
    )(u_pool, proj5, x, vec, w_pool, w_bout, w_o)


def _gelu_parts(u):
    th = jnp.tanh(GELU_C0 * (u + GELU_C1 * (u * u * u)))
    cdf = 0.5 * (1.0 + th)
    return cdf, th


def _ffn_fwd(upre, x1, target, vec, fcv, w_down):
    t, d = x1.shape
    tm = GROUP
    fp = w_down.shape[0]
    cw = fp // 4

    def body(up_ref, x1_ref, tg_ref, vec_ref, fcv_ref, wd_ref, a_ref, upb_ref, upreb_ref, ff_ref, dy_ref, loss_ref,
             carry):
        i = pl.program_id(0)

        @pl.when(i == 0)
        def _():
            carry[...] = jnp.zeros_like(carry)
            loss_ref[...] = jnp.zeros_like(loss_ref)

        def conv(col):
            v = up_ref[:, col]
            halo = _halo_top(v[tm - 16:, :], carry[:, col])
            carry[:, col] = v[tm - 16:, :]
            w0, w1, w2 = _conv_taps(fcv_ref, col)
            y = fcv_ref[3:4, col] + w0 * _shift_down(v, halo, 2)
            y = y + w1 * _shift_down(v, halo, 1)
            y = y + w2 * v
            upb_ref[:, col] = y.astype(BF16)
            upreb_ref[:, col] = v.astype(BF16)
            return y

        for j in range(4):
            gc = slice(j * cw, (j + 1) * cw)
            vc = slice(fp + j * cw, fp + (j + 1) * cw)
            gate = conv(gc)
            val = conv(vc)
            cdf, _ = _gelu_parts(gate)
            a_ref[:, gc] = ((gate * cdf) * val).astype(BF16)
        ff = _dot(a_ref[...], wd_ref[...])
        ff_ref[...] = ff
        r4 = lax.rsqrt(_rowmean(ff * ff) + EPS)
        y = x1_ref[...] + vec_ref[0:1, :] * ((ff * r4) * vec_ref[1:2, :])
        e = y - _interleave(tg_ref[...])
        dy_ref[...] = e * (1.0 / d)
        loss_ref[...] += jnp.sum(_rowmean(e * e))

    row = lambda n: pl.BlockSpec((tm, n), lambda i: (i, 0))
    return pl.pallas_call(
        body, name="ffn_fwd", grid=(t // tm,),
        in_specs=[row(2 * fp), row(d), row(d), _vmem(), _vmem(), _vmem()],
        out_specs=[row(fp), row(2 * fp), row(2 * fp), row(d), row(d), pl.BlockSpec((8, LANES), lambda i: (0, 0))],
        out_shape=[_sds((t, fp), BF16), _sds((t, 2 * fp), BF16), _sds((t, 2 * fp), BF16), _sds((t, d), F32),
                   _sds((t, d), F32), _sds((8, LANES), F32)],
        scratch_shapes=[pltpu.VMEM((16, 2 * fp), F32)],
        compiler_params=_params("arbitrary"),
    )(upre, x1, target, vec, fcv, w_down)


def _conv3_bwd(dv, carry, col, taps, x):
    halo = _halo_bottom(dv[:16, :], carry[:, col])
    carry[:, col] = dv[:16, :]
    d1 = _shift_up(dv, halo, 1)
    d2 = _shift_up(dv, halo, 2)
    w0, w1, w2 = taps
    dx = w2 * dv
    dx = dx + w1 * d1
    dx = dx + w0 * d2
    return dx, (_colsum(d2 * x), _colsum(d1 * x), _colsum(dv * x))


def _ffn_bwd(dy, ff, upb, upreb, vec, fcv, w_down):
    t, d = dy.shape
    tm = GROUP
    fp = w_down.shape[0]
    cw = fp // 4
    nt = t // tm

    def body(dy_ref, ff_ref, upb_ref, upreb_ref, vec_ref, fcv_ref, wd_ref,
             dff_ref, dup_ref, red_ref, cred_ref, carry):
        i = pl.program_id(0)

        @pl.when(i == 0)
        def _():
            carry[...] = jnp.zeros_like(carry)
            red_ref[...] = jnp.zeros_like(red_ref)
            cred_ref[...] = jnp.zeros_like(cred_ref)

        ff = ff_ref[...]
        dy_v = dy_ref[...]
        r4 = lax.rsqrt(_rowmean(ff * ff) + EPS)
        nh = ff * r4
        gt2, gpost = vec_ref[0:1, :], vec_ref[1:2, :]
        dn = dy_v * gt2
        red_ref[0:1, :] += _colsum(dn * nh)
        red_ref[1:2, :] += _colsum(dy_v * (nh * gpost))
        dnh = dn * gpost
        dff = r4 * (dnh - nh * _rowmean(dnh * nh))
        dffb = dff.astype(BF16)
        dff_ref[...] = dffb

        def conv_bwd(dv, col):
            dx, (t0, t1, t2) = _conv3_bwd(dv, carry, col, _conv_taps(fcv_ref, col), upreb_ref[:, col].astype(F32))
            cred_ref[0:1, col] += t0
            cred_ref[1:2, col] += t1
            cred_ref[2:3, col] += t2
            cred_ref[3:4, col] += _colsum(dv)
            dup_ref[:, col] = dx.astype(BF16)

        for j in range(4):
            gc = slice(j * cw, (j + 1) * cw)
            vc = slice(fp + j * cw, fp + (j + 1) * cw)
            da = _dot_nt(dffb, wd_ref[gc, :])
            gate = upb_ref[:, gc].astype(F32)
            val = upb_ref[:, vc].astype(F32)
            cdf, th = _gelu_parts(gate)
            dcdf = 0.5 * (1.0 - th * th) * (GELU_C0 * (1.0 + (3.0 * GELU_C1) * (gate * gate)))
            conv_bwd(da * val * (cdf + gate * dcdf), gc)
            conv_bwd(da * (gate * cdf), vc)

    rev = lambda n: pl.BlockSpec((tm, n), lambda i: (nt - 1 - i, 0))
    fixed = lambda n: pl.BlockSpec((8, n), lambda i: (0, 0))
    return pl.pallas_call(
        body, name="ffn_bwd", grid=(nt,),
        in_specs=[rev(d), rev(d), rev(2 * fp), rev(2 * fp), _vmem(), _vmem(), _vmem()],
        out_specs=[rev(d), rev(2 * fp), fixed(d), fixed(2 * fp)],
        out_shape=[_sds((t, d), BF16), _sds((t, 2 * fp), BF16), _sds((8, d), F32), _sds((8, 2 * fp), F32)],
        scratch_shapes=[pltpu.VMEM((16, 2 * fp), F32)],
        compiler_params=_params("arbitrary"),
    )(dy, ff, upb, upreb, vec, fcv, w_down)


def _dgrad_norm_bwd(dz, w, xin, dres, vec, tm, name, deinterleave):
    t, d = xin.shape
    n = dz.shape[1]

    def body(dz_ref, w_ref, x_ref, dr_ref, vec_ref, dx_ref, red_ref):
        @pl.when(pl.program_id(0) == 0)
        def _():
            red_ref[...] = jnp.zeros_like(red_ref)

        dh = _dot_nt(dz_ref[...], w_ref[...])
        x = x_ref[...]
        r = lax.rsqrt(_rowmean(x * x) + EPS)
        nh = x * r
        g, sc1 = vec_ref[0:1, :], vec_ref[1:2, :]
        red_ref[0:1, :] += _colsum(dh)
        red_ref[1:2, :] += _colsum(dh * (nh * g))
        red_ref[2:3, :] += _colsum(dh * nh * sc1)
        dnh = dh * g * sc1
        dx = dr_ref[...] + r * (dnh - nh * _rowmean(dnh * nh))
        if deinterleave:
            for g0 in range(0, tm, GROUP):
                dx_ref[g0:g0 + GROUP, :] = _deinterleave(dx[g0:g0 + GROUP, :])
        else:
            dx_ref[...] = dx

    row = lambda k: pl.BlockSpec((tm, k), lambda i: (i, 0))
    return pl.pallas_call(
        body, name=name, grid=(t // tm,),
        in_specs=[row(n), _vmem(), row(d), row(d), _vmem()],
        out_specs=[row(d), pl.BlockSpec((8, d), lambda i: (0, 0))],
        out_shape=[_sds((t, d), F32), _sds((8, d), F32)],
        compiler_params=_params("arbitrary"),
    )(dz, w, xin, dres, vec)


def _mixer_bwd(dx1, o, yar, yb, cvb, proj5, vec, w_pool, w_bout, w_o):
    t, d = dx1.shape
    tm = GROUP
    gw = d // len(POOL_WINDOWS)
    nt = t // tm
    pool_rows = 8 * (POOL_WINDOWS[-1] - 1)

    def body(dx1_ref, o_ref, yar_ref, yb_ref, cv_ref, p5_ref, vec_ref, wp_ref, wb_ref, wo_ref,
             do_ref, dyar_ref, dyb_ref, dp_ref, red_ref, dpgcarry, dcvcarry):
        i = pl.program_id(0)
        tix = nt - 1 - i

        @pl.when(i == 0)
        def _():
            red_ref[...] = jnp.zeros_like(red_ref)
            dpgcarry[...] = jnp.zeros_like(dpgcarry)
            dcvcarry[...] = jnp.zeros_like(dcvcarry)

        gt1, gpost, pscale = vec_ref[0:1, :], vec_ref[1:2, :], vec_ref[2:3, :]
        dx1_v = dx1_ref[...]
        o = o_ref[...]
        r2 = lax.rsqrt(_rowmean(o * o) + EPS)
        nh = o * r2
        dn = dx1_v * gt1
        red_ref[0:1, :] += _colsum(dn * nh)
        red_ref[1:2, :] += _colsum(dx1_v * (nh * gpost))
        dnh = dn * gpost
        dob = (r2 * (dnh - nh * _rowmean(dnh * nh))).astype(BF16)
        do_ref[...] = dob
        dm = _dot_nt(dob, wo_ref[...])

        sa = jax.nn.sigmoid(p5_ref[:, 3 * d:4 * d])
        yar = yar_ref[...]
        dya = dm * sa
        dp_ref[:, 4 * d:5 * d] = (dm * (yar * pscale) * sa * (1.0 - sa)).astype(BF16)
        red_ref[2:3, :] += _colsum(dya * yar)
        dyarb = (dya * pscale).astype(BF16)
        dyar_ref[...] = dyarb
        sb = jax.nn.sigmoid(p5_ref[:, 4 * d:5 * d])
        dybb = (dm * sb).astype(BF16)
        dyb_ref[...] = dybb
        dp_ref[:, 5 * d:6 * d] = (dm * yb_ref[...] * sb * (1.0 - sb)).astype(BF16)

        for g, window in enumerate(POOL_WINDOWS):
            cols = slice(g * gw, (g + 1) * gw)
            rows = 8 * (window - 1)
            dpg = _dot_nt(dyarb[:, cols], wp_ref[g])
            dpgs = dpg * _inv_count(tix * tm, window)
            halo = _halo_bottom(dpgs[:rows, :], dpgcarry[:rows, cols])
            dpgcarry[:, cols] = dpgs[:pool_rows, :]
            s, shift = jnp.concatenate([dpgs, halo], axis=0), 1
            while shift < window:
                s = s[:s.shape[0] - 8 * shift, :] + s[8 * shift:, :]
                shift *= 2
            dp_ref[:, cols] = (s - dpg).astype(BF16)

        dq = _dot_nt(dybb, wb_ref[...])
        ux, ub, uc = p5_ref[:, 0:d], p5_ref[:, d:2 * d], p5_ref[:, 2 * d:3 * d]
        dp_ref[:, 2 * d:3 * d] = (dq * cv_ref[...].astype(F32)).astype(BF16)
        dcv = dq * ub
        taps = (vec_ref[4:5, :], vec_ref[5:6, :], vec_ref[6:7, :])
        dpv, (t0, t1, t2) = _conv3_bwd(dcv, dcvcarry, slice(0, d), taps, uc * ux)
        red_ref[3:4, :] += _colsum(dcv)
        red_ref[4:5, :] += t0
        red_ref[5:6, :] += t1
        red_ref[6:7, :] += t2
        dp_ref[:, d:2 * d] = (dpv * uc).astype(BF16)
        dp_ref[:, 3 * d:4 * d] = (dpv * ux).astype(BF16)

    rev = lambda n: pl.BlockSpec((tm, n), lambda i: (nt - 1 - i, 0))
    return pl.pallas_call(
        body, name="mixer_bwd", grid=(nt,),
        in_specs=[rev(d), rev(d), rev(d), rev(d), rev(d), rev(5 * d), _vmem(), _vmem(), _vmem(), _vmem()],
        out_specs=[rev(d), rev(d), rev(d), rev(6 * d), pl.BlockSpec((16, d), lambda i: (0, 0))],
        out_shape=[_sds((t, d), BF16), _sds((t, d), BF16), _sds((t, d), BF16), _sds((t, 6 * d), BF16),
                   _sds((16, d), F32)],
        scratch_shapes=[pltpu.VMEM((pool_rows, d), F32), pltpu.VMEM((16, d), F32)],
        compiler_params=_params("arbitrary"),
    )(dx1, o, yar, yb, cvb, proj5, vec, w_pool, w_bout, w_o)


def _matmul_tn(a, b, bm, bn, tk, by_col_block, name):
    t, m = a.shape
    n = b.shape[1]
    nk = t // tk

    def body(a_ref, b_ref, o_ref, acc_ref):
        k = pl.program_id(2)

        @pl.when(k == 0)
        def _():
            acc_ref[...] = jnp.zeros_like(acc_ref)

        acc_ref[...] += _dot_tn(a_ref[...], b_ref[...])

        @pl.when(k == nk - 1)
        def _():
            o_ref[...] = acc_ref[...].astype(o_ref.dtype)

    if by_col_block:
        out_shape = _sds((n // bn, m, bn), BF16)
        out_spec = pl.BlockSpec((None, bm, bn), lambda i, j, k: (j, i, 0))
    else:
        out_shape = _sds((m, n), BF16)
        out_spec = pl.BlockSpec((bm, bn), lambda i, j, k: (i, j))
    return pl.pallas_call(
        body, name=name, grid=(m // bm, n // bn, nk),
        in_specs=[pl.BlockSpec((tk, bm), lambda i, j, k: (k, i)), pl.BlockSpec((tk, bn), lambda i, j, k: (k, j))],
        out_specs=out_spec, out_shape=out_shape,
        scratch_shapes=[pltpu.VMEM((bm, bn), F32)],
        compiler_params=_params("parallel", "parallel", "arbitrary"),
    )(a, b)


def _matmul_tn_groups(a, b, groups, tk, name):
    t, m = a.shape
    w = m // groups
    nk = t // tk

    def body(a_ref, b_ref, o_ref, acc_ref):
        k = pl.program_id(1)

        @pl.when(k == 0)
        def _():
            acc_ref[...] = jnp.zeros_like(acc_ref)

        acc_ref[...] += _dot_tn(a_ref[...], b_ref[...])

        @pl.when(k == nk - 1)
        def _():
            o_ref[...] = acc_ref[...].astype(o_ref.dtype)

    blk = pl.BlockSpec((tk, w), lambda g, k: (k, g))
    return pl.pallas_call(
        body, name=name, grid=(groups, nk), in_specs=[blk, blk],
        out_specs=pl.BlockSpec((None, w, w), lambda g, k: (g, 0, 0)), out_shape=_sds((groups, w, w), BF16),
        scratch_shapes=[pltpu.VMEM((w, w), F32)],
        compiler_params=_params("parallel", "arbitrary"),
    )(a, b)


def _round_up(n, k):
    return (n + k - 1) // k * k


def _pad_last(a, parts, width):
    lead = a.shape[:-1]
    piece = a.shape[-1] // parts
    a = a.reshape(lead + (parts, piece))
    a = jnp.pad(a, [(0, 0)] * len(lead) + [(0, 0), (0, width - piece)])
    return a.reshape(lead + (parts * width,))


def _unpad_last(a, parts, piece):
    lead = a.shape[:-1]
    width = a.shape[-1] // parts
    return a.reshape(lead + (parts, width))[..., :piece].reshape(lead + (parts * piece,))


def _rows8(rows, width):
    n = _round_up(len(rows), 8)
    rows = list(rows) + [jnp.zeros((1, width), F32)] * (n - len(rows))
    return jnp.concatenate(rows, axis=0)


def kernel(x, c, g_pre_mix, g_post_mix, g_pre_ffn, g_post_ffn, w_ada, b_ada, w_in, w_pool, pool_scale, conv_w, conv_b, w_bout, w_o, w_up, ffn_conv_w, ffn_conv_b, w_down, loss_target, m_g_pre_mix, m_g_post_mix, m_g_pre_ffn, m_g_post_ffn, m_w_ada, m_b_ada, m_w_in, m_w_pool, m_pool_scale, m_conv_w, m_conv_b, m_w_bout, m_w_o, m_w_up, m_ffn_conv_w, m_ffn_conv_b, m_w_down, v_g_pre_mix, v_g_post_mix, v_g_pre_ffn, v_g_post_ffn, v_w_ada, v_b_ada, v_w_in, v_w_pool, v_pool_scale, v_conv_w, v_conv_b, v_w_bout, v_w_o, v_w_up, v_ffn_conv_w, v_ffn_conv_b, v_w_down):
    t, d = x.shape[1], x.shape[2]
    ngroups = len(POOL_WINDOWS)
    gw = d // ngroups
    ada_n = w_ada.shape[2]
    in_n = w_in.shape[2]
    up_n = w_up.shape[2]
    unit = w_down.shape[1]
    unit_p = _round_up(unit, LANES)
    fp = NDEV * unit_p
    ff2 = NDEV * up_n
    tm = min(256, t)
    tm_big = min(512, t)

    xi, yi, ci = _position()
    me = _linear(xi, yi, ci)
    chip = 2 * xi + yi
    core = jnp.reshape(ci, (1,)).astype(jnp.int32)
    sel = jnp.stack([2 * chip + ci, chip]).astype(jnp.int32)

    x2 = x.reshape(t, d)
    target = loss_target.reshape(t, d)

    cw_n = conv_w.shape[2]
    fcw_p = _pad_last(ffn_conv_w[0], 2, unit_p)
    pack = jnp.concatenate([c.reshape(1, d), conv_w[0].reshape(1, 3 * cw_n), fcw_p.reshape(1, 6 * unit_p)], axis=1)
    pack_n = _round_up(pack.shape[1], LANES)
    pack = jnp.pad(pack, ((0, 0), (0, pack_n - pack.shape[1])))
    pack = jnp.pad(pack, ((0, 7), (0, 0)))
    gathered = _small_allgather(pack, "allgather_cond", False)[0][:, 0, :]
    c_all = gathered[:, :d]
    conv_w_full = gathered[:, d:d + 3 * cw_n].reshape(NDEV, 3, cw_n).transpose(1, 0, 2).reshape(3, NDEV * cw_n)
    fcw_full = gathered[:, d + 3 * cw_n:d + 3 * cw_n + 6 * unit_p].reshape(NDEV, 3, 2 * unit_p)
    fcw_full = fcw_full.transpose(1, 0, 2).reshape(3, 2 * fp)
    fcb_full = _pad_last(ffn_conv_b, 2 * NDEV, unit_p)
    fcv = jnp.concatenate([fcw_full, fcb_full, jnp.zeros((4, 2 * fp), F32)], axis=0)

    c16 = jnp.pad(c_all, ((0, 8), (0, 0))).astype(BF16)
    b_piece = lax.dynamic_slice_in_dim(b_ada, me * ada_n, ada_n, axis=1)
    mod_piece = _mod_piece(c16, w_ada[0], b_piece)[:NDEV]
    mod_all = _small_allgather(mod_piece, "allgather_mod", False)[0]
    mod = lax.dynamic_index_in_dim(mod_all, me, axis=1, keepdims=False).reshape(1, NDEV * ada_n)
    sh1, sc1, gt1, sh2, sc2, gt2 = [mod[:, k * d:(k + 1) * d] for k in range(6)]

    w_in_s = w_in[0].astype(BF16)
    w_up_s = _pad_last(w_up[0], 2, unit_p).astype(BF16)
    w_down_s = jnp.pad(w_down[0], ((0, unit_p - unit), (0, 0))).astype(BF16)
    w_bout_s = w_bout[0].astype(BF16)
    w_o_s = w_o[0].astype(BF16)
    w_pool_s = w_pool[0].reshape(-1, gw).astype(BF16)
    g_in, g_up, g_down, g_bout, g_o, g_pool = _allgather_weights(
        [w_in_s, w_up_s, w_down_s, w_bout_s, w_o_s, w_pool_s])
    w_in_f = g_in.transpose(1, 0, 2).reshape(d, NDEV * in_n)
    w_up_f = g_up.transpose(1, 0, 2).reshape(d, 2 * fp)
    w_down_f = g_down.reshape(fp, d)
    w_bout_f = g_bout.reshape(d, d)
    w_o_f = g_o.reshape(d, d)
    w_pool_f = g_pool.reshape(NDEV, ngroups, gw // NDEV, gw).transpose(1, 0, 2, 3).reshape(ngroups, gw, gw)

    vec_pre_mix = _rows8([g_pre_mix, 1.0 + sc1, sh1], d)
    xp, h1b, u_pool, proj5 = _norm_matmul(x2, vec_pre_mix, w_in_f, (d, 5 * d), tm_big, "in_proj", True)
    vec_mix = _rows8([gt1, g_post_mix, pool_scale, conv_b, conv_w_full[0:1], conv_w_full[1:2], conv_w_full[2:3]], d)
    pgb, qb, cvb, yar, yb, mergedb, o, x1 = _mixer_fwd(u_pool, proj5, xp, vec_mix, w_pool_f, w_bout_f, w_o_f)
    vec_pre_ffn = _rows8([g_pre_ffn, 1.0 + sc2, sh2], d)
    h2b, upre = _norm_matmul(x1, vec_pre_ffn, w_up_f, (2 * fp,), tm_big, "up_proj", False)
    vec_ffn = _rows8([gt2, g_post_ffn], d)
    ab, upb, upreb, ff, dy, loss_part = _ffn_fwd(upre, x1, target, vec_ffn, fcv, w_down_f)
    loss = lax.psum(0.5 * loss_part[0, 0], ("x", "y", "c"))

    dffb, dupre, red_ffn, red_fconv = _ffn_bwd(dy, ff, upb, upreb, vec_ffn, fcv, w_down_f)
    dx1, red_pre_ffn = _dgrad_norm_bwd(dupre, w_up_f, x1, dy, vec_pre_ffn, tm_big, "up_proj_bwd", False)
    dob, dyarb, dybb, dproj, red_mix = _mixer_bwd(dx1, o, yar, yb, cvb, proj5, vec_mix, w_pool_f, w_bout_f, w_o_f)
    grad_x, red_pre_mix = _dgrad_norm_bwd(dproj, w_in_f, xp, dx1, vec_pre_mix, tm_big, "in_proj_bwd", True)

    tk = min(2048, t)
    gr_in = _matmul_tn(h1b, dproj, d, in_n, tk, True, "wgrad_in")
    gr_up = _matmul_tn(h2b, dupre, d, 2 * unit_p, tk, True, "wgrad_up")
    gr_down = _matmul_tn(ab, dffb, 2 * unit_p, d, tk, False, "wgrad_down").reshape(NDEV, unit_p, d)
    gr_bout = _matmul_tn(qb, dybb, d, d, tk, False, "wgrad_bout").reshape(NDEV, d // NDEV, d)
    gr_o = _matmul_tn(mergedb, dob, d, d, tk, False, "wgrad_o").reshape(NDEV, d // NDEV, d)
    gr_pool = _matmul_tn_groups(pgb, dyarb, ngroups, min(4096, t), "wgrad_pool")
    gr_pool = gr_pool.reshape(ngroups, NDEV, gw // NDEV, gw).transpose(1, 0, 2, 3).reshape(NDEV, -1, gw)

    partials = [gr_in, gr_up, gr_down, gr_bout, gr_o, gr_pool]
    from_sibling = _exchange_sibling(partials)
    chip_sums = [_chip_sum(g, s, core) for g, s in zip(partials, from_sibling)]
    from_chips = _exchange_chips(chip_sums)

    def big(k, w, m, v, name, col_parts=0):
        shape = w.shape
        w2, m2, v2 = [a.reshape((-1, shape[-1])) for a in (w, m, v)]
        if col_parts:
            w2, m2, v2 = [_pad_last(a, col_parts, partials[k].shape[2] // col_parts) for a in (w2, m2, v2)]
        outs = _reduce_adamw(partials[k], from_sibling[k], from_chips[k], sel, w2, m2, v2, name)
        if col_parts:
            outs = [_unpad_last(a, col_parts, shape[-1] // col_parts) for a in outs]
        return [a.reshape(shape) for a in outs]

    g_w_in, d_w_in, nm_w_in, nv_w_in = big(0, w_in, m_w_in, v_w_in, "adamw_in")
    g_w_up, d_w_up, nm_w_up, nv_w_up = big(1, w_up, m_w_up, v_w_up, "adamw_up", col_parts=2)
    g_w_down, d_w_down, nm_w_down, nv_w_down = big(2, w_down, m_w_down, v_w_down, "adamw_down")
    g_w_bout, d_w_bout, nm_w_bout, nv_w_bout = big(3, w_bout, m_w_bout, v_w_bout, "adamw_bout")
    g_w_o, d_w_o, nm_w_o, nv_w_o = big(4, w_o, m_w_o, v_w_o, "adamw_o")
    g_w_pool, d_w_pool, nm_w_pool, nv_w_pool = big(5, w_pool, m_w_pool, v_w_pool, "adamw_pool")

    dmod = [red_pre_mix[0:1], red_pre_mix[1:2], red_mix[1:2], red_pre_ffn[0:1], red_pre_ffn[1:2], red_ffn[1:2]]
    small = [red_pre_mix[2:3], red_mix[0:1], red_pre_ffn[2:3], red_ffn[0:1], red_mix[2:3], red_mix[3:4],
             red_mix[4:5], red_mix[5:6], red_mix[6:7]] + dmod
    flat = jnp.concatenate(small + [red_fconv[0:4].reshape(1, 8 * fp)], axis=1)
    flat_n = flat.shape[1]
    width = 8 * LANES
    rows = _round_up(-(-flat_n // width), 8)
    flat = jnp.pad(flat, ((0, 0), (0, rows * width - flat_n))).reshape(rows, width)
    gat, tot = _small_allgather(flat, "allreduce_small", True)
    tot = tot.reshape(1, rows * width)
    gat = gat.reshape(NDEV, rows * width)
    take = lambda k: tot[:, k * d:(k + 1) * d]
    g_g_pre_mix, g_g_post_mix, g_g_pre_ffn, g_g_post_ffn, g_pool_scale, g_conv_b = [take(k) for k in range(6)]
    g_conv_w_full = jnp.concatenate([take(6), take(7), take(8)], axis=0)
    g_conv_w = lax.dynamic_slice_in_dim(g_conv_w_full, me * cw_n, cw_n, axis=1)
    g_b_ada = tot[:, 9 * d:15 * d]
    dmod_all = gat[:, 9 * d:15 * d]
    fconv_tot = tot[:, 15 * d:15 * d + 8 * fp].reshape(4, 2 * fp)
    g_ffn_conv_b = _unpad_last(fconv_tot[3:4], 2 * NDEV, unit)
    g_fcw_mine = lax.dynamic_slice_in_dim(fconv_tot[0:3], me * 2 * unit_p, 2 * unit_p, axis=1)
    g_ffn_conv_w = _unpad_last(g_fcw_mine, 2, unit)
    dmod_piece = lax.dynamic_slice_in_dim(dmod_all, me * ada_n, ada_n, axis=1)
    g_w_ada = _wada_grad(c16, jnp.pad(dmod_piece, ((0, 8), (0, 0))).astype(BF16))

    names_small = [(g_pre_mix, g_g_pre_mix, m_g_pre_mix, v_g_pre_mix), (g_post_mix, g_g_post_mix, m_g_post_mix, v_g_post_mix),
                   (g_pre_ffn, g_g_pre_ffn, m_g_pre_ffn, v_g_pre_ffn), (g_post_ffn, g_g_post_ffn, m_g_post_ffn, v_g_post_ffn),
                   (b_ada, g_b_ada, m_b_ada, v_b_ada), (pool_scale, g_pool_scale, m_pool_scale, v_pool_scale),
                   (conv_w, g_conv_w, m_conv_w, v_conv_w), (conv_b, g_conv_b, m_conv_b, v_conv_b),
                   (ffn_conv_w, g_ffn_conv_w, m_ffn_conv_w, v_ffn_conv_w), (ffn_conv_b, g_ffn_conv_b, m_ffn_conv_b, v_ffn_conv_b)]
    sizes = [w.size for w, _, _, _ in names_small]
    total = sum(sizes)
    prow = _round_up(-(-total // width), 8)

    def pack_small(k):
        a = jnp.concatenate([q[k].reshape(1, -1) for q in names_small], axis=1)
        return jnp.pad(a, ((0, 0), (0, prow * width - total)), constant_values=1.0).reshape(prow, width)

    ds, ms, vs = _adamw(pack_small(0), pack_small(1), pack_small(2), pack_small(3), "adamw_small")

    def unpack_small(a):
        a = a.reshape(-1)
        out, off = [], 0
        for (w, _, _, _), n in zip(names_small, sizes):
            out.append(a[off:off + n].reshape(w.shape))
            off += n
        return out

    (d_g_pre_mix, d_g_post_mix, d_g_pre_ffn, d_g_post_ffn, d_b_ada, d_pool_scale, d_conv_w, d_conv_b,
     d_ffn_conv_w, d_ffn_conv_b) = unpack_small(ds)
    (nm_g_pre_mix, nm_g_post_mix, nm_g_pre_ffn, nm_g_post_ffn, nm_b_ada, nm_pool_scale, nm_conv_w, nm_conv_b,
     nm_ffn_conv_w, nm_ffn_conv_b) = unpack_small(ms)
    (nv_g_pre_mix, nv_g_post_mix, nv_g_pre_ffn, nv_g_post_ffn, nv_b_ada, nv_pool_scale, nv_conv_w, nv_conv_b,
     nv_ffn_conv_w, nv_ffn_conv_b) = unpack_small(vs)
    d_w_ada, nm_w_ada, nv_w_ada = [a.reshape(w_ada.shape) for a in
                                   _adamw(w_ada[0], g_w_ada, m_w_ada[0], v_w_ada[0], "adamw_ada")]

    grads = [g_g_pre_mix, g_g_post_mix, g_g_pre_ffn, g_g_post_ffn, g_w_ada.reshape(w_ada.shape), g_b_ada, g_w_in,
             g_w_pool, g_pool_scale, g_conv_w.reshape(conv_w.shape), g_conv_b, g_w_bout, g_w_o, g_w_up,
             g_ffn_conv_w.reshape(ffn_conv_w.shape), g_ffn_conv_b, g_w_down]
    deltas = [d_g_pre_mix, d_g_post_mix, d_g_pre_ffn, d_g_post_ffn, d_w_ada, d_b_ada, d_w_in, d_w_pool, d_pool_scale,
              d_conv_w, d_conv_b, d_w_bout, d_w_o, d_w_up, d_ffn_conv_w, d_ffn_conv_b, d_w_down]
    new_m = [nm_g_pre_mix, nm_g_post_mix, nm_g_pre_ffn, nm_g_post_ffn, nm_w_ada, nm_b_ada, nm_w_in, nm_w_pool,
             nm_pool_scale, nm_conv_w, nm_conv_b, nm_w_bout, nm_w_o, nm_w_up, nm_ffn_conv_w, nm_ffn_conv_b, nm_w_down]
    new_v = [nv_g_pre_mix, nv_g_post_mix, nv_g_pre_ffn, nv_g_post_ffn, nv_w_ada, nv_b_ada, nv_w_in, nv_w_pool,
             nv_pool_scale, nv_conv_w, nv_conv_b, nv_w_bout, nv_w_o, nv_w_up, nv_ffn_conv_w, nv_ffn_conv_b, nv_w_down]
    return (loss, grad_x.reshape(x.shape), *grads, *deltas, *new_m, *new_v)
```

```python
import functools
import math

import jax
import jax.numpy as jnp
from jax import lax
from jax.experimental import pallas as pl
from jax.experimental.pallas import tpu as pltpu

F32 = jnp.float32
BF16 = jnp.bfloat16
MESH = pl.DeviceIdType.MESH

NDEV = 8
NCHIP = 4
EPS = 1e-6
POOL_WINDOWS = (2, 4, 8, 16)
LANES = 128
ADAM_LR = 0.001
ADAM_B1 = 0.9
ADAM_B2 = 0.999
ADAM_EPS = 1e-08
ADAM_WD = 0.01
ADAM_STEP = 10
GELU_C0 = math.sqrt(2.0 / math.pi)
GELU_C1 = 0.044715
VMEM_LIMIT = 56 * 2**20


def _vmem():
    return pl.BlockSpec(memory_space=pltpu.VMEM)


def _any():
    return pl.BlockSpec(memory_space=pl.ANY)


def _params(*sem):
    return pltpu.CompilerParams(dimension_semantics=sem, vmem_limit_bytes=VMEM_LIMIT)


def _sds(shape, dtype):
    return jax.ShapeDtypeStruct(tuple(shape), dtype)


def _position():
    return lax.axis_index("x"), lax.axis_index("y"), lax.axis_index("c")


def _linear(x, y, c):
    return 4 * x + 2 * y + c


def _dot(a, b):
    return jnp.dot(a, b, preferred_element_type=F32)


def _dot_nt(a, b):
    return lax.dot_general(a, b, (((1,), (1,)), ((), ())), preferred_element_type=F32)


def _dot_tn(a, b):
    return lax.dot_general(a, b, (((0,), (0,)), ((), ())), preferred_element_type=F32)


def _colsum(v):
    return jnp.sum(v, axis=0, keepdims=True)


def _rowmean(v):
    return jnp.mean(v, axis=-1, keepdims=True)


GROUP = 256


def _interleave(v):
    g, n = v.shape
    return jnp.swapaxes(v.reshape(8, g // 8, n), 0, 1).reshape(g, n)


def _deinterleave(v):
    g, n = v.shape
    return jnp.swapaxes(v.reshape(g // 8, 8, n), 0, 1).reshape(g, n)


def _halo_top(cur_last, prev_last):
    rows, n = cur_last.shape
    c3 = cur_last.reshape(rows // 8, 8, n)
    p3 = prev_last.reshape(rows // 8, 8, n)
    sub = lax.broadcasted_iota(jnp.int32, c3.shape, 1)
    return jnp.where(sub == 0, pltpu.roll(p3, 1, 1), pltpu.roll(c3, 1, 1)).reshape(rows, n)


def _halo_bottom(cur_first, next_first):
    rows, n = cur_first.shape
    c3 = cur_first.reshape(rows // 8, 8, n)
    n3 = next_first.reshape(rows // 8, 8, n)
    sub = lax.broadcasted_iota(jnp.int32, c3.shape, 1)
    return jnp.where(sub == 7, pltpu.roll(n3, 7, 1), pltpu.roll(c3, 7, 1)).reshape(rows, n)


def _shift_down(v, halo, k):
    rows = v.shape[0]
    return jnp.concatenate([halo[halo.shape[0] - 8 * k:, :], v[:rows - 8 * k, :]], axis=0)


def _shift_up(v, halo, k):
    return jnp.concatenate([v[8 * k:, :], halo[:8 * k, :]], axis=0)


def _inv_count(first_token, window):
    row = lax.broadcasted_iota(jnp.int32, (GROUP, 1), 0)
    t = first_token + (row % 8) * (GROUP // 8) + row // 8
    return 1.0 / jnp.minimum(t + 1, window).astype(F32)


def _peers(x, y, c):
    out = []
    for k in range(1, NDEV):
        out.append(((1 - x) if k & 4 else x, (1 - y) if k & 2 else y, (1 - c) if k & 1 else c))
    return out


def _small_allgather(v, name, with_sum):
    r, n = v.shape

    def body(v_ref, gat_ref, *rest):
        if with_sum:
            sum_ref, send_sems, recv_sems, local_sem = rest
        else:
            send_sems, recv_sems, local_sem = rest
        x, y, c = _position()
        me = _linear(x, y, c)
        mine = pltpu.make_async_copy(v_ref, gat_ref.at[me], local_sem)
        mine.start()
        peers = _peers(x, y, c)
        sends = []
        for k, peer in enumerate(peers):
            cp = pltpu.make_async_remote_copy(src_ref=v_ref, dst_ref=gat_ref.at[me], send_sem=send_sems.at[k],
                                              recv_sem=recv_sems.at[k], device_id=peer, device_id_type=MESH)
            cp.start()
            sends.append(cp)
        for k, peer in enumerate(peers):
            pltpu.make_async_remote_copy(src_ref=v_ref, dst_ref=gat_ref.at[_linear(*peer)], send_sem=send_sems.at[k],
                                         recv_sem=recv_sems.at[k], device_id=peer, device_id_type=MESH).wait_recv()
        for cp in sends:
            cp.wait_send()
        mine.wait()
        if with_sum:
            acc = gat_ref[0]
            for j in range(1, NDEV):
                acc = acc + gat_ref[j]
            sum_ref[...] = acc

    out_shape = [_sds((NDEV, r, n), F32)] + ([_sds((r, n), F32)] if with_sum else [])
    return pl.pallas_call(
        body, name=name, out_shape=out_shape, in_specs=[_vmem()], out_specs=[_vmem()] * len(out_shape),
        scratch_shapes=[pltpu.SemaphoreType.DMA((NDEV - 1,)), pltpu.SemaphoreType.DMA((NDEV - 1,)),
                        pltpu.SemaphoreType.DMA(())],
    )(v)


def _allgather_weights(shards):
    n = len(shards)

    def body(*refs):
        src, dst = refs[:n], refs[n:2 * n]
        send_sems, recv_sems, local_sems = refs[2 * n:]
        x, y, c = _position()
        me, sibling = (x, y, c), (x, y, 1 - c)
        chips = [(1 - x, y), (x, 1 - y), (1 - x, 1 - y)]

        def copy(a, k, block, to, from_src=False):
            blk = dst[a].at[_linear(*block)]
            return pltpu.make_async_remote_copy(src_ref=src[a] if from_src else blk, dst_ref=blk,
                                                send_sem=send_sems.at[a, k], recv_sem=recv_sems.at[a, k],
                                                device_id=to, device_id_type=MESH)

        local = [pltpu.make_async_copy(src[a], dst[a].at[_linear(*me)], local_sems.at[a]) for a in range(n)]
        for cp in local:
            cp.start()
        first = []
        for j, chip in enumerate(chips):
            for a in range(n):
                first.append(copy(a, 1 + j, me, (*chip, c), from_src=True))
        for a in range(n):
            first.append(copy(a, 0, me, sibling, from_src=True))
        for cp in first:
            cp.start()
        passed = []
        for j, chip in enumerate(chips):
            for a in range(n):
                copy(a, 1 + j, (*chip, c), me).wait_recv()
                cp = copy(a, 4 + j, (*chip, c), sibling)
                cp.start()
                passed.append(cp)
        for a in range(n):
            copy(a, 0, sibling, me).wait_recv()
        for j, chip in enumerate(chips):
            for a in range(n):
                copy(a, 4 + j, (*chip, 1 - c), me).wait_recv()
        for cp in first + passed:
            cp.wait_send()
        for cp in local:
            cp.wait()

    return pl.pallas_call(
        body, name="allgather_weights",
        out_shape=[_sds((NDEV,) + s.shape, s.dtype) for s in shards],
        in_specs=[_any()] * n, out_specs=[_any()] * n,
        scratch_shapes=[pltpu.SemaphoreType.DMA((n, 7)), pltpu.SemaphoreType.DMA((n, 7)),
                        pltpu.SemaphoreType.DMA((n,))],
    )(*shards)


def _exchange_sibling(grads):
    n = len(grads)

    def body(*refs):
        src, dst = refs[:n], refs[n:2 * n]
        send_sems, recv_sems = refs[2 * n:]
        x, y, c = _position()
        sibling = (x, y, 1 - c)
        sends = []
        for a in range(n):
            for q in range(NCHIP):
                cp = pltpu.make_async_remote_copy(src_ref=src[a].at[2 * q + 1 - c], dst_ref=dst[a].at[q],
                                                  send_sem=send_sems.at[a, q], recv_sem=recv_sems.at[a, q],
                                                  device_id=sibling, device_id_type=MESH)
                cp.start()
                sends.append(cp)
        for cp in sends:
            cp.wait_recv()
        for cp in sends:
            cp.wait_send()

    return pl.pallas_call(
        body, name="rs_exchange_sibling",
        out_shape=[_sds((NCHIP,) + g.shape[1:], g.dtype) for g in grads],
        in_specs=[_any()] * n, out_specs=[_any()] * n,
        scratch_shapes=[pltpu.SemaphoreType.DMA((n, NCHIP)), pltpu.SemaphoreType.DMA((n, NCHIP))],
    )(*grads)


def _exchange_chips(chip_sums):
    n = len(chip_sums)

    def body(*refs):
        src, dst = refs[:n], refs[n:2 * n]
        send_sems, recv_sems = refs[2 * n:]
        x, y, c = _position()
        chips = [(1 - x, y), (x, 1 - y), (1 - x, 1 - y)]
        sends = []
        for j, chip in enumerate(chips):
            for a in range(n):
                cp = pltpu.make_async_remote_copy(src_ref=src[a].at[2 * chip[0] + chip[1]], dst_ref=dst[a].at[j],
                                                  send_sem=send_sems.at[a, j], recv_sem=recv_sems.at[a, j],
                                                  device_id=(*chip, c), device_id_type=MESH)
                cp.start()
                sends.append(cp)
        for cp in sends:
            cp.wait_recv()
        for cp in sends:
            cp.wait_send()

    return pl.pallas_call(
        body, name="rs_exchange_chips",
        out_shape=[_sds((3,) + s.shape[1:], s.dtype) for s in chip_sums],
        in_specs=[_any()] * n, out_specs=[_any()] * n,
        scratch_shapes=[pltpu.SemaphoreType.DMA((n, 3)), pltpu.SemaphoreType.DMA((n, 3))],
    )(*chip_sums)


def _row_block(r):
    for rb in (512, 256, 128, 64, 32, 16):
        if r % rb == 0:
            return rb
    return r


def _chip_sum(grad, from_sibling, core):
    _, r, c = grad.shape
    rb = _row_block(r)

    def body(core_ref, g_ref, s_ref, o_ref):
        del core_ref
        o_ref[...] = (g_ref[...].astype(F32) + s_ref[...].astype(F32)).astype(o_ref.dtype)

    grid_spec = pltpu.PrefetchScalarGridSpec(
        num_scalar_prefetch=1, grid=(NCHIP, r // rb),
        in_specs=[pl.BlockSpec((None, rb, c), lambda q, i, core: (2 * q + core[0], i, 0)),
                  pl.BlockSpec((None, rb, c), lambda q, i, core: (q, i, 0))],
        out_specs=pl.BlockSpec((None, rb, c), lambda q, i, core: (q, i, 0)))
    return pl.pallas_call(body, name="rs_chip_sum", grid_spec=grid_spec, out_shape=_sds((NCHIP, r, c), BF16),
                          compiler_params=_params("parallel", "parallel"))(core, grad, from_sibling)


def _adamw_math(w, g, m, v):
    m2 = ADAM_B1 * m + (1.0 - ADAM_B1) * g
    v2 = ADAM_B2 * v + (1.0 - ADAM_B2) * jnp.square(g)
    m_hat = m2 / (1.0 - ADAM_B1 ** ADAM_STEP)
    v_hat = v2 / (1.0 - ADAM_B2 ** ADAM_STEP)
    delta = -ADAM_LR * (m_hat / (jnp.sqrt(v_hat) + ADAM_EPS) + ADAM_WD * w)
    return delta, m2, v2


def _adamw(w, g, m, v, name):
    r, c = w.shape
    rb = _row_block(r)

    def body(w_ref, g_ref, m_ref, v_ref, d_ref, m2_ref, v2_ref):
        d, m2, v2 = _adamw_math(w_ref[...], g_ref[...], m_ref[...], v_ref[...])
        d_ref[...] = d
        m2_ref[...] = m2
        v2_ref[...] = v2

    blk = pl.BlockSpec((rb, c), lambda i: (i, 0))
    return pl.pallas_call(body, name=name, grid=(r // rb,), in_specs=[blk] * 4, out_specs=[blk] * 3,
                          out_shape=[_sds((r, c), F32)] * 3, compiler_params=_params("parallel"))(w, g, m, v)


def _reduce_adamw(grad, from_sibling, from_chips, sel, w, m, v, name):
    r, c = w.shape
    rb = _row_block(r)

    def body(sel_ref, g_ref, s_ref, c0_ref, c1_ref, c2_ref, w_ref, m_ref, v_ref, go_ref, d_ref, m2_ref, v2_ref):
        del sel_ref
        g = g_ref[...].astype(F32) + s_ref[...].astype(F32)
        g = g + c0_ref[...].astype(F32)
        g = g + c1_ref[...].astype(F32)
        g = g + c2_ref[...].astype(F32)
        d, m2, v2 = _adamw_math(w_ref[...], g, m_ref[...], v_ref[...])
        go_ref[...] = g
        d_ref[...] = d
        m2_ref[...] = m2
        v2_ref[...] = v2

    blk = pl.BlockSpec((rb, c), lambda i, sel: (i, 0))
    grid_spec = pltpu.PrefetchScalarGridSpec(
        num_scalar_prefetch=1, grid=(r // rb,),
        in_specs=[pl.BlockSpec((None, rb, c), lambda i, sel: (sel[0], i, 0)),
                  pl.BlockSpec((None, rb, c), lambda i, sel: (sel[1], i, 0)),
                  pl.BlockSpec((None, rb, c), lambda i, sel: (0, i, 0)),
                  pl.BlockSpec((None, rb, c), lambda i, sel: (1, i, 0)),
                  pl.BlockSpec((None, rb, c), lambda i, sel: (2, i, 0)),
                  blk, blk, blk],
        out_specs=[blk] * 4)
    return pl.pallas_call(body, name=name, grid_spec=grid_spec, out_shape=[_sds((r, c), F32)] * 4,
                          compiler_params=_params("parallel"))(sel, grad, from_sibling, from_chips, from_chips,
                                                               from_chips, w, m, v)


def _mod_piece(c_all, w_ada, b_piece):
    rows, _ = c_all.shape
    n = w_ada.shape[1]

    def body(c_ref, w_ref, b_ref, o_ref):
        o_ref[...] = _dot(c_ref[...], w_ref[...].astype(BF16)) + b_ref[...]

    return pl.pallas_call(body, name="ada_mod", out_shape=_sds((rows, n), F32),
                          in_specs=[_vmem()] * 3, out_specs=_vmem())(c_all, w_ada, b_piece)


def _wada_grad(c_all, dmod_piece):
    d = c_all.shape[1]
    n = dmod_piece.shape[1]

    def body(c_ref, dm_ref, o_ref):
        o_ref[...] = _dot_tn(c_ref[...], dm_ref[...])

    return pl.pallas_call(body, name="ada_wgrad", out_shape=_sds((d, n), F32),
                          in_specs=[_vmem()] * 2, out_specs=_vmem())(c_all, dmod_piece)


def _norm_matmul(xin, vec, w, splits, tm, name, interleave):
    t, d = xin.shape
    ch = 1024

    def body(x_ref, vec_ref, w_ref, *outs):
        if interleave:
            xp_ref, hb_ref, *z_refs = outs
            for g0 in range(0, tm, GROUP):
                xp_ref[g0:g0 + GROUP, :] = _interleave(x_ref[g0:g0 + GROUP, :])
            x = xp_ref[...]
        else:
            hb_ref, *z_refs = outs
            x = x_ref[...]
        r = lax.rsqrt(_rowmean(x * x) + EPS)
        h = ((x * r) * vec_ref[0:1, :]) * vec_ref[1:2, :] + vec_ref[2:3, :]
        hb = h.astype(BF16)
        hb_ref[...] = hb
        off = 0
        for z_ref, width in zip(z_refs, splits):
            for c0 in range(0, width, ch):
                cw = min(ch, width - c0)
                z_ref[:, c0:c0 + cw] = _dot(hb, w_ref[:, off + c0:off + c0 + cw])
            off += width

    row = lambda n: pl.BlockSpec((tm, n), lambda i: (i, 0))
    lead = [_sds((t, d), F32)] if interleave else []
    return pl.pallas_call(
        body, name=name, grid=(t // tm,),
        in_specs=[row(d), _vmem(), _vmem()],
        out_specs=[row(d)] * (len(lead) + 1) + [row(n) for n in splits],
        out_shape=lead + [_sds((t, d), BF16)] + [_sds((t, n), F32) for n in splits],
        compiler_params=_params("parallel"),
    )(xin, vec, w)


def _conv_taps(ref, col):
    return ref[0:1, col], ref[1:2, col], ref[2:3, col]


def _mixer_fwd(u_pool, proj5, x, vec, w_pool, w_bout, w_o):
    t, d = x.shape
    tm = GROUP
    gw = d // len(POOL_WINDOWS)
    pool_rows = 8 * (POOL_WINDOWS[-1] - 1)

    def body(up_ref, p5_ref, x_ref, vec_ref, wp_ref, wb_ref, wo_ref,
             pg_ref, q_ref, cv_ref, yar_ref, yb_ref, mg_ref, o_ref, x1_ref, ucarry, pcarry):
        i = pl.program_id(0)

        @pl.when(i == 0)
        def _():
            ucarry[...] = jnp.zeros_like(ucarry)
            pcarry[...] = jnp.zeros_like(pcarry)

        for g, window in enumerate(POOL_WINDOWS):
            cols = slice(g * gw, (g + 1) * gw)
            rows = 8 * (window - 1)
            u = up_ref[:, cols]
            halo = _halo_top(u[tm - rows:, :], ucarry[pool_rows - rows:, cols])
            s, shift = jnp.concatenate([halo, u], axis=0), 1
            while shift < window:
                s = s[8 * shift:, :] + s[:s.shape[0] - 8 * shift, :]
                shift *= 2
            pg = s * _inv_count(i * tm, window) - u
            pgb = pg.astype(BF16)
            pg_ref[:, cols] = pgb
            yar_ref[:, cols] = _dot(pgb, wp_ref[g])
        ucarry[...] = up_ref[tm - pool_rows:, :]

        ux, ub, uc = p5_ref[:, 0:d], p5_ref[:, d:2 * d], p5_ref[:, 2 * d:3 * d]
        p = uc * ux
        halo = _halo_top(p[tm - 16:, :], pcarry[...])
        pcarry[...] = p[tm - 16:, :]
        w0, w1, w2 = vec_ref[4:5, :], vec_ref[5:6, :], vec_ref[6:7, :]
        cv = vec_ref[3:4, :] + w0 * _shift_down(p, halo, 2)
        cv = cv + w1 * _shift_down(p, halo, 1)
        cv = cv + w2 * p
        cv_ref[...] = cv.astype(BF16)
        qb = (ub * cv).astype(BF16)
        q_ref[...] = qb
        yb = _dot(qb, wb_ref[...])
        yb_ref[...] = yb

        ya = yar_ref[...] * vec_ref[2:3, :]
        merged = jax.nn.sigmoid(p5_ref[:, 3 * d:4 * d]) * ya + jax.nn.sigmoid(p5_ref[:, 4 * d:5 * d]) * yb
        mb = merged.astype(BF16)
        mg_ref[...] = mb
        o = _dot(mb, wo_ref[...])
        o_ref[...] = o
        r2 = lax.rsqrt(_rowmean(o * o) + EPS)
        x1_ref[...] = x_ref[...] + vec_ref[0:1, :] * ((o * r2) * vec_ref[1:2, :])

    row = lambda n: pl.BlockSpec((tm, n), lambda i: (i, 0))
    return pl.pallas_call(
        body, name="mixer_fwd", grid=(t // tm,),
        in_specs=[row(d), row(5 * d), row(d), _vmem(), _vmem(), _vmem(), _vmem()],
        out_specs=[row(d)] * 8,
        out_shape=[_sds((t, d), BF16), _sds((t, d), BF16), _sds((t, d), BF16), _sds((t, d), F32),
                   _sds((t, d), F32), _sds((t, d), BF16), _sds((t, d), F32), _sds((t, d), F32)],
        scratch_shapes=[pltpu.VMEM((pool_rows, d), F32), pltpu.VMEM((16, d), F32)],
        compiler_params=_params("arbitrary"),
    )(u_pool, proj5, x, vec, w_pool, w_bout, w_o)


def _gelu_parts(u):
    th = jnp.tanh(GELU_C0 * (u + GELU_C1 * (u * u * u)))
    cdf = 0.5 * (1.0 + th)
    return cdf, th


def _ffn_fwd(upre, x1, target, vec, fcv, w_down):
    t, d = x1.shape
    tm = GROUP
    fp = w_down.shape[0]
    cw = fp // 4

    def body(up_ref, x1_ref, tg_ref, vec_ref, fcv_ref, wd_ref, a_ref, upb_ref, upreb_ref, ff_ref, dy_ref, loss_ref,
             carry):
        i = pl.program_id(0)

        @pl.when(i == 0)
        def _():
            carry[...] = jnp.zeros_like(carry)
            loss_ref[...] = jnp.zeros_like(loss_ref)

        def conv(col):
            v = up_ref[:, col]
            halo = _halo_top(v[tm - 16:, :], carry[:, col])
            carry[:, col] = v[tm - 16:, :]
            w0, w1, w2 = _conv_taps(fcv_ref, col)
            y = fcv_ref[3:4, col] + w0 * _shift_down(v, halo, 2)
            y = y + w1 * _shift_down(v, halo, 1)
            y = y + w2 * v
            upb_ref[:, col] = y.astype(BF16)
            upreb_ref[:, col] = v.astype(BF16)
            return y

        for j in range(4):
            gc = slice(j * cw, (j + 1) * cw)
            vc = slice(fp + j * cw, fp + (j + 1) * cw)
            gate = conv(gc)
            val = conv(vc)
            cdf, _ = _gelu_parts(gate)
            a_ref[:, gc] = ((gate * cdf) * val).astype(BF16)
        ff = _dot(a_ref[...], wd_ref[...])
        ff_ref[...] = ff
        r4 = lax.rsqrt(_rowmean(ff * ff) + EPS)
        y = x1_ref[...] + vec_ref[0:1, :] * ((ff * r4) * vec_ref[1:2, :])
        e = y - _interleave(tg_ref[...])
        dy_ref[...] = e * (1.0 / d)
        loss_ref[...] += jnp.sum(_rowmean(e * e))

    row = lambda n: pl.BlockSpec((tm, n), lambda i: (i, 0))
    return pl.pallas_call(
        body, name="ffn_fwd", grid=(t // tm,),
        in_specs=[row(2 * fp), row(d), row(d), _vmem(), _vmem(), _vmem()],
        out_specs=[row(fp), row(2 * fp), row(2 * fp), row(d), row(d), pl.BlockSpec((8, LANES), lambda i: (0, 0))],
        out_shape=[_sds((t, fp), BF16), _sds((t, 2 * fp), BF16), _sds((t, 2 * fp), BF16), _sds((t, d), F32),
                   _sds((t, d), F32), _sds((8, LANES), F32)],
        scratch_shapes=[pltpu.VMEM((16, 2 * fp), F32)],
        compiler_params=_params("arbitrary"),
    )(upre, x1, target, vec, fcv, w_down)


def _conv3_bwd(dv, carry, col, taps, x):
    halo = _halo_bottom(dv[:16, :], carry[:, col])
    carry[:, col] = dv[:16, :]
    d1 = _shift_up(dv, halo, 1)
    d2 = _shift_up(dv, halo, 2)
    w0, w1, w2 = taps
    dx = w2 * dv
    dx = dx + w1 * d1
    dx = dx + w0 * d2
    return dx, (_colsum(d2 * x), _colsum(d1 * x), _colsum(dv * x))


def _ffn_bwd(dy, ff, upb, upreb, vec, fcv, w_down):
    t, d = dy.shape
    tm = GROUP
    fp = w_down.shape[0]
    cw = fp // 4
    nt = t // tm

    def body(dy_ref, ff_ref, upb_ref, upreb_ref, vec_ref, fcv_ref, wd_ref,
             dff_ref, dup_ref, red_ref, cred_ref, carry):
        i = pl.program_id(0)

        @pl.when(i == 0)
        def _():
            carry[...] = jnp.zeros_like(carry)
            red_ref[...] = jnp.zeros_like(red_ref)
            cred_ref[...] = jnp.zeros_like(cred_ref)

        ff = ff_ref[...]
        dy_v = dy_ref[...]
        r4 = lax.rsqrt(_rowmean(ff * ff) + EPS)
        nh = ff * r4
        gt2, gpost = vec_ref[0:1, :], vec_ref[1:2, :]
        dn = dy_v * gt2
        red_ref[0:1, :] += _colsum(dn * nh)
        red_ref[1:2, :] += _colsum(dy_v * (nh * gpost))
        dnh = dn * gpost
        dff = r4 * (dnh - nh * _rowmean(dnh * nh))
        dffb = dff.astype(BF16)
        dff_ref[...] = dffb

        def conv_bwd(dv, col):
            dx, (t0, t1, t2) = _conv3_bwd(dv, carry, col, _conv_taps(fcv_ref, col), upreb_ref[:, col].astype(F32))
            cred_ref[0:1, col] += t0
            cred_ref[1:2, col] += t1
            cred_ref[2:3, col] += t2
            cred_ref[3:4, col] += _colsum(dv)
            dup_ref[:, col] = dx.astype(BF16)

        for j in range(4):
            gc = slice(j * cw, (j + 1) * cw)
            vc = slice(fp + j * cw, fp + (j + 1) * cw)
            da = _dot_nt(dffb, wd_ref[gc, :])
            gate = upb_ref[:, gc].astype(F32)
            val = upb_ref[:, vc].astype(F32)
            cdf, th = _gelu_parts(gate)
            dcdf = 0.5 * (1.0 - th * th) * (GELU_C0 * (1.0 + (3.0 * GELU_C1) * (gate * gate)))
            conv_bwd(da * val * (cdf + gate * dcdf), gc)
            conv_bwd(da * (gate * cdf), vc)

    rev = lambda n: pl.BlockSpec((tm, n), lambda i: (nt - 1 - i, 0))
    fixed = lambda n: pl.BlockSpec((8, n), lambda i: (0, 0))
    return pl.pallas_call(
        body, name="ffn_bwd", grid=(nt,),
        in_specs=[rev(d), rev(d), rev(2 * fp), rev(2 * fp), _vmem(), _vmem(), _vmem()],
        out_specs=[rev(d), rev(2 * fp), fixed(d), fixed(2 * fp)],
        out_shape=[_sds((t, d), BF16), _sds((t, 2 * fp), BF16), _sds((8, d), F32), _sds((8, 2 * fp), F32)],
        scratch_shapes=[pltpu.VMEM((16, 2 * fp), F32)],
        compiler_params=_params("arbitrary"),
    )(dy, ff, upb, upreb, vec, fcv, w_down)


def _dgrad_norm_bwd(dz, w, xin, dres, vec, tm, name, deinterleave):
    t, d = xin.shape
    n = dz.shape[1]

    def body(dz_ref, w_ref, x_ref, dr_ref, vec_ref, dx_ref, red_ref):
        @pl.when(pl.program_id(0) == 0)
        def _():
            red_ref[...] = jnp.zeros_like(red_ref)

        dh = _dot_nt(dz_ref[...], w_ref[...])
        x = x_ref[...]
        r = lax.rsqrt(_rowmean(x * x) + EPS)
        nh = x * r
        g, sc1 = vec_ref[0:1, :], vec_ref[1:2, :]
        red_ref[0:1, :] += _colsum(dh)
        red_ref[1:2, :] += _colsum(dh * (nh * g))
        red_ref[2:3, :] += _colsum(dh * nh * sc1)
        dnh = dh * g * sc1
        dx = dr_ref[...] + r * (dnh - nh * _rowmean(dnh * nh))
        if deinterleave:
            for g0 in range(0, tm, GROUP):
                dx_ref[g0:g0 + GROUP, :] = _deinterleave(dx[g0:g0 + GROUP, :])
        else:
            dx_ref[...] = dx

    row = lambda k: pl.BlockSpec((tm, k), lambda i: (i, 0))
    return pl.pallas_call(
        body, name=name, grid=(t // tm,),
        in_specs=[row(n), _vmem(), row(d), row(d), _vmem()],
        out_specs=[row(d), pl.BlockSpec((8, d), lambda i: (0, 0))],
        out_shape=[_sds((t, d), F32), _sds((8, d), F32)],
        compiler_params=_params("arbitrary"),
    )(dz, w, xin, dres, vec)


def _mixer_bwd(dx1, o, yar, yb, cvb, proj5, vec, w_pool, w_bout, w_o):
    t, d = dx1.shape
    tm = GROUP
    gw = d // len(POOL_WINDOWS)
    nt = t // tm
    pool_rows = 8 * (POOL_WINDOWS[-1] - 1)

    def body(dx1_ref, o_ref, yar_ref, yb_ref, cv_ref, p5_ref, vec_ref, wp_ref, wb_ref, wo_ref,
             do_ref, dyar_ref, dyb_ref, dp_ref, red_ref, dpgcarry, dcvcarry):
        i = pl.program_id(0)
        tix = nt - 1 - i

        @pl.when(i == 0)
        def _():
            red_ref[...] = jnp.zeros_like(red_ref)
            dpgcarry[...] = jnp.zeros_like(dpgcarry)
            dcvcarry[...] = jnp.zeros_like(dcvcarry)

        gt1, gpost, pscale = vec_ref[0:1, :], vec_ref[1:2, :], vec_ref[2:3, :]
        dx1_v = dx1_ref[...]
        o = o_ref[...]
        r2 = lax.rsqrt(_rowmean(o * o) + EPS)
        nh = o * r2
        dn = dx1_v * gt1
        red_ref[0:1, :] += _colsum(dn * nh)
        red_ref[1:2, :] += _colsum(dx1_v * (nh * gpost))
        dnh = dn * gpost
        dob = (r2 * (dnh - nh * _rowmean(dnh * nh))).astype(BF16)
        do_ref[...] = dob
        dm = _dot_nt(dob, wo_ref[...])

        sa = jax.nn.sigmoid(p5_ref[:, 3 * d:4 * d])
        yar = yar_ref[...]
        dya = dm * sa
        dp_ref[:, 4 * d:5 * d] = (dm * (yar * pscale) * sa * (1.0 - sa)).astype(BF16)
        red_ref[2:3, :] += _colsum(dya * yar)
        dyarb = (dya * pscale).astype(BF16)
        dyar_ref[...] = dyarb
        sb = jax.nn.sigmoid(p5_ref[:, 4 * d:5 * d])
        dybb = (dm * sb).astype(BF16)
        dyb_ref[...] = dybb
        dp_ref[:, 5 * d:6 * d] = (dm * yb_ref[...] * sb * (1.0 - sb)).astype(BF16)

        for g, window in enumerate(POOL_WINDOWS):
            cols = slice(g * gw, (g + 1) * gw)
            rows = 8 * (window - 1)
            dpg = _dot_nt(dyarb[:, cols], wp_ref[g])
            dpgs = dpg * _inv_count(tix * tm, window)
            halo = _halo_bottom(dpgs[:rows, :], dpgcarry[:rows, cols])
            dpgcarry[:, cols] = dpgs[:pool_rows, :]
            s, shift = jnp.concatenate([dpgs, halo], axis=0), 1
            while shift < window:
                s = s[:s.shape[0] - 8 * shift, :] + s[8 * shift:, :]
                shift *= 2
            dp_ref[:, cols] = (s - dpg).astype(BF16)

        dq = _dot_nt(dybb, wb_ref[...])
        ux, ub, uc = p5_ref[:, 0:d], p5_ref[:, d:2 * d], p5_ref[:, 2 * d:3 * d]
        dp_ref[:, 2 * d:3 * d] = (dq * cv_ref[...].astype(F32)).astype(BF16)
        dcv = dq * ub
        taps = (vec_ref[4:5, :], vec_ref[5:6, :], vec_ref[6:7, :])
        dpv, (t0, t1, t2) = _conv3_bwd(dcv, dcvcarry, slice(0, d), taps, uc * ux)
        red_ref[3:4, :] += _colsum(dcv)
        red_ref[4:5, :] += t0
        red_ref[5:6, :] += t1
        red_ref[6:7, :] += t2
        dp_ref[:, d:2 * d] = (dpv * uc).astype(BF16)
        dp_ref[:, 3 * d:4 * d] = (dpv * ux).astype(BF16)

    rev = lambda n: pl.BlockSpec((tm, n), lambda i: (nt - 1 - i, 0))
    return pl.pallas_call(
        body, name="mixer_bwd", grid=(nt,),
        in_specs=[rev(d), rev(d), rev(d), rev(d), rev(d), rev(5 * d), _vmem(), _vmem(), _vmem(), _vmem()],
        out_specs=[rev(d), rev(d), rev(d), rev(6 * d), pl.BlockSpec((16, d), lambda i: (0, 0))],
        out_shape=[_sds((t, d), BF16), _sds((t, d), BF16), _sds((t, d), BF16), _sds((t, 6 * d), BF16),
                   _sds((16, d), F32)],
        scratch_shapes=[pltpu.VMEM((pool_rows, d), F32), pltpu.VMEM((16, d), F32)],
        compiler_params=_params("arbitrary"),
    )(dx1, o, yar, yb, cvb, proj5, vec, w_pool, w_bout, w_o)


def _prenorm(x, vp_ref):
    r = lax.rsqrt(_rowmean(x * x) + EPS)
    nh = x * r
    return (nh * vp_ref[0:1, :]) * vp_ref[1:2, :] + vp_ref[2:3, :], r, nh


def _prenorm_bwd(dh, r, nh, vp_ref, red_ref):
    g, sc1 = vp_ref[0:1, :], vp_ref[1:2, :]
    red_ref[0:1, :] += _colsum(dh)
    red_ref[1:2, :] += _colsum(dh * (nh * g))
    red_ref[2:3, :] += _colsum(dh * nh * sc1)
    dnh = dh * g * sc1
    return r * (dnh - nh * _rowmean(dnh * nh))


def _postnorm_bwd(dres, z, gate, gpost, red_ref):
    r = lax.rsqrt(_rowmean(z * z) + EPS)
    nh = z * r
    dn = dres * gate
    red_ref[0:1, :] += _colsum(dn * nh)
    red_ref[1:2, :] += _colsum(dres * (nh * gpost))
    dnh = dn * gpost
    return r * (dnh - nh * _rowmean(dnh * nh))


def _mixer_block_fwd(x, vec_pre, vec, w_in, w_pool, w_bout, w_o):
    t, d = x.shape
    tm = GROUP
    gw = d // len(POOL_WINDOWS)
    pool_rows = 8 * (POOL_WINDOWS[-1] - 1)

    def body(x_ref, vp_ref, vec_ref, win_ref, wp_ref, wb_ref, wo_ref,
             hb_ref, p5_ref, pg_ref, q_ref, cv_ref, yar_ref, yb_ref, mg_ref, o_ref, x1_ref, mbuf, ucarry, pcarry):
        i = pl.program_id(0)

        @pl.when(i == 0)
        def _():
            ucarry[...] = jnp.zeros_like(ucarry)
            pcarry[...] = jnp.zeros_like(pcarry)

        xp = _interleave(x_ref[...])
        hb = _prenorm(xp, vp_ref)[0].astype(BF16)
        hb_ref[...] = hb
        proj = lambda k: _dot(hb, win_ref[:, k * d:(k + 1) * d])

        za = proj(4)
        p5_ref[:, 3 * d:4 * d] = za.astype(BF16)
        sa = jax.nn.sigmoid(za)
        u_pool = proj(0)
        for g, window in enumerate(POOL_WINDOWS):
            cols = slice(g * gw, (g + 1) * gw)
            rows = 8 * (window - 1)
            u = u_pool[:, cols]
            halo = _halo_top(u[tm - rows:, :], ucarry[pool_rows - rows:, cols])
            s, shift = jnp.concatenate([halo, u], axis=0), 1
            while shift < window:
                s = s[8 * shift:, :] + s[:s.shape[0] - 8 * shift, :]
                shift *= 2
            pgb = (s * _inv_count(i * tm, window) - u).astype(BF16)
            pg_ref[:, cols] = pgb
            yar = _dot(pgb, wp_ref[g])
            yar_ref[:, cols] = yar.astype(BF16)
            mbuf[:, cols] = sa[:, cols] * (yar * vec_ref[2:3, cols])
        ucarry[...] = u_pool[tm - pool_rows:, :]

        ux = proj(1)
        uc = proj(3)
        p5_ref[:, 0:d] = ux.astype(BF16)
        p5_ref[:, 2 * d:3 * d] = uc.astype(BF16)
        p = uc * ux
        halo = _halo_top(p[tm - 16:, :], pcarry[...])
        pcarry[...] = p[tm - 16:, :]
        cv = vec_ref[3:4, :] + vec_ref[4:5, :] * _shift_down(p, halo, 2)
        cv = cv + vec_ref[5:6, :] * _shift_down(p, halo, 1)
        cv = cv + vec_ref[6:7, :] * p
        cv_ref[...] = cv.astype(BF16)
        ub = proj(2)
        p5_ref[:, d:2 * d] = ub.astype(BF16)
        qb = (ub * cv).astype(BF16)
        q_ref[...] = qb
        yb = _dot(qb, wb_ref[...])
        yb_ref[...] = yb.astype(BF16)

        zb = proj(5)
        p5_ref[:, 4 * d:5 * d] = zb.astype(BF16)
        mb = (mbuf[...] + jax.nn.sigmoid(zb) * yb).astype(BF16)
        mg_ref[...] = mb
        o = _dot(mb, wo_ref[...])
        o_ref[...] = o.astype(BF16)
        r2 = lax.rsqrt(_rowmean(o * o) + EPS)
        x1_ref[...] = xp + vec_ref[0:1, :] * ((o * r2) * vec_ref[1:2, :])

    row = lambda n: pl.BlockSpec((tm, n), lambda i: (i, 0))
    return pl.pallas_call(
        body, name="mixer_block_fwd", grid=(t // tm,),
        in_specs=[row(d)] + [_vmem()] * 6,
        out_specs=[row(d), row(5 * d)] + [row(d)] * 8,
        out_shape=[_sds((t, d), BF16), _sds((t, 5 * d), BF16)] + [_sds((t, d), BF16)] * 7 + [_sds((t, d), F32)],
        scratch_shapes=[pltpu.VMEM((tm, d), F32), pltpu.VMEM((pool_rows, d), F32), pltpu.VMEM((16, d), F32)],
        compiler_params=_params("arbitrary"),
    )(x, vec_pre, vec, w_in, w_pool, w_bout, w_o)


def _ffn_block_fwd(x1, target, vec_pre, vec, fcv, w_up, w_down):
    t, d = x1.shape
    tm = GROUP
    fp = w_down.shape[0]
    cw = fp // 4

    def body(x1_ref, tg_ref, vp_ref, vec_ref, fcv_ref, wu_ref, wd_ref,
             hb_ref, upb_ref, upreb_ref, a_ref, ffb_ref, dy_ref, loss_ref, carry):
        i = pl.program_id(0)

        @pl.when(i == 0)
        def _():
            carry[...] = jnp.zeros_like(carry)
            loss_ref[...] = jnp.zeros_like(loss_ref)

        x1 = x1_ref[...]
        hb = _prenorm(x1, vp_ref)[0].astype(BF16)
        hb_ref[...] = hb

        def conv(col):
            v = _dot(hb, wu_ref[:, col])
            halo = _halo_top(v[tm - 16:, :], carry[:, col])
            carry[:, col] = v[tm - 16:, :]
            w0, w1, w2 = _conv_taps(fcv_ref, col)
            y = fcv_ref[3:4, col] + w0 * _shift_down(v, halo, 2)
            y = y + w1 * _shift_down(v, halo, 1)
            y = y + w2 * v
            upb_ref[:, col] = y.astype(BF16)
            upreb_ref[:, col] = v.astype(BF16)
            return y

        ff = None
        for j in range(4):
            gc = slice(j * cw, (j + 1) * cw)
            gate = conv(gc)
            val = conv(slice(fp + j * cw, fp + (j + 1) * cw))
            ab = ((gate * _gelu_parts(gate)[0]) * val).astype(BF16)
            a_ref[:, gc] = ab
            part = _dot(ab, wd_ref[gc, :])
            ff = part if ff is None else ff + part
        ffb_ref[...] = ff.astype(BF16)
        r4 = lax.rsqrt(_rowmean(ff * ff) + EPS)
        y = x1 + vec_ref[0:1, :] * ((ff * r4) * vec_ref[1:2, :])
        e = y - _interleave(tg_ref[...])
        dy_ref[...] = e * (1.0 / d)
        loss_ref[...] += jnp.sum(_rowmean(e * e))

    row = lambda n: pl.BlockSpec((tm, n), lambda i: (i, 0))
    return pl.pallas_call(
        body, name="ffn_block_fwd", grid=(t // tm,),
        in_specs=[row(d), row(d)] + [_vmem()] * 5,
        out_specs=[row(d), row(2 * fp), row(2 * fp), row(fp), row(d), row(d), pl.BlockSpec((8, LANES), lambda i: (0, 0))],
        out_shape=[_sds((t, d), BF16), _sds((t, 2 * fp), BF16), _sds((t, 2 * fp), BF16), _sds((t, fp), BF16),
                   _sds((t, d), BF16), _sds((t, d), F32), _sds((8, LANES), F32)],
        scratch_shapes=[pltpu.VMEM((16, 2 * fp), F32)],
        compiler_params=_params("arbitrary"),
    )(x1, target, vec_pre, vec, fcv, w_up, w_down)


def _ffn_block_bwd(dy, ffb, x1, upb, upreb, vec_pre, vec, fcv, w_up, w_down):
    t, d = dy.shape
    tm = GROUP
    fp = w_down.shape[0]
    cw = fp // 4
    nt = t // tm

    def body(dy_ref, ff_ref, x1_ref, upb_ref, upreb_ref, vp_ref, vec_ref, fcv_ref, wu_ref, wd_ref,
             dff_ref, dup_ref, dx1_ref, red_ref, cred_ref, pred_ref, carry):
        @pl.when(pl.program_id(0) == 0)
        def _():
            carry[...] = jnp.zeros_like(carry)
            red_ref[...] = jnp.zeros_like(red_ref)
            cred_ref[...] = jnp.zeros_like(cred_ref)
            pred_ref[...] = jnp.zeros_like(pred_ref)

        dy_v = dy_ref[...]
        dffb = _postnorm_bwd(dy_v, ff_ref[...].astype(F32), vec_ref[0:1, :], vec_ref[1:2, :], red_ref).astype(BF16)
        dff_ref[...] = dffb

        def conv_bwd(dv, col):
            dx, (t0, t1, t2) = _conv3_bwd(dv, carry, col, _conv_taps(fcv_ref, col), upreb_ref[:, col].astype(F32))
            cred_ref[0:1, col] += t0
            cred_ref[1:2, col] += t1
            cred_ref[2:3, col] += t2
            cred_ref[3:4, col] += _colsum(dv)
            dxb = dx.astype(BF16)
            dup_ref[:, col] = dxb
            return _dot_nt(dxb, wu_ref[:, col])

        dh = None
        for j in range(4):
            gc = slice(j * cw, (j + 1) * cw)
            vc = slice(fp + j * cw, fp + (j + 1) * cw)
            da = _dot_nt(dffb, wd_ref[gc, :])
            gate = upb_ref[:, gc].astype(F32)
            val = upb_ref[:, vc].astype(F32)
            cdf, th = _gelu_parts(gate)
            dcdf = 0.5 * (1.0 - th * th) * (GELU_C0 * (1.0 + (3.0 * GELU_C1) * (gate * gate)))
            part = conv_bwd(da * val * (cdf + gate * dcdf), gc) + conv_bwd(da * (gate * cdf), vc)
            dh = part if dh is None else dh + part

        _, r, nh = _prenorm(x1_ref[...], vp_ref)
        dx1_ref[...] = dy_v + _prenorm_bwd(dh, r, nh, vp_ref, pred_ref)

    rev = lambda n: pl.BlockSpec((tm, n), lambda i: (nt - 1 - i, 0))
    fixed = lambda n: pl.BlockSpec((8, n), lambda i: (0, 0))
    return pl.pallas_call(
        body, name="ffn_block_bwd", grid=(nt,),
        in_specs=[rev(d), rev(d), rev(d), rev(2 * fp), rev(2 * fp)] + [_vmem()] * 5,
        out_specs=[rev(d), rev(2 * fp), rev(d), fixed(d), fixed(2 * fp), fixed(d)],
        out_shape=[_sds((t, d), BF16), _sds((t, 2 * fp), BF16), _sds((t, d), F32), _sds((8, d), F32),
                   _sds((8, 2 * fp), F32), _sds((8, d), F32)],
        scratch_shapes=[pltpu.VMEM((16, 2 * fp), F32)],
        compiler_params=_params("arbitrary"),
    )(dy, ffb, x1, upb, upreb, vec_pre, vec, fcv, w_up, w_down)


def _mixer_block_bwd(dx1, ob, yarb, ybb, cvb, p5b, x, vec_pre, vec, w_in, w_pool, w_bout, w_o):
    t, d = dx1.shape
    tm = GROUP
    gw = d // len(POOL_WINDOWS)
    nt = t // tm
    pool_rows = 8 * (POOL_WINDOWS[-1] - 1)

    def body(dx1_ref, o_ref, yar_ref, yb_ref, cv_ref, p5_ref, x_ref, vp_ref, vec_ref, win_ref, wp_ref, wb_ref, wo_ref,
             do_ref, dyar_ref, dyb_ref, dp_ref, gx_ref, red_ref, pred_ref, dpgcarry, dcvcarry):
        i = pl.program_id(0)
        tix = nt - 1 - i

        @pl.when(i == 0)
        def _():
            red_ref[...] = jnp.zeros_like(red_ref)
            pred_ref[...] = jnp.zeros_like(pred_ref)
            dpgcarry[...] = jnp.zeros_like(dpgcarry)
            dcvcarry[...] = jnp.zeros_like(dcvcarry)

        pscale = vec_ref[2:3, :]
        dx1_v = dx1_ref[...]
        dob = _postnorm_bwd(dx1_v, o_ref[...].astype(F32), vec_ref[0:1, :], vec_ref[1:2, :], red_ref).astype(BF16)
        do_ref[...] = dob
        dm = _dot_nt(dob, wo_ref[...])

        def dproj(cols, value):
            vb = value.astype(BF16)
            dp_ref[:, cols] = vb
            return _dot_nt(vb, win_ref[:, cols])

        sa = jax.nn.sigmoid(p5_ref[:, 3 * d:4 * d].astype(F32))
        yar = yar_ref[...].astype(F32)
        dya = dm * sa
        dh = dproj(slice(4 * d, 5 * d), dm * (yar * pscale) * sa * (1.0 - sa))
        red_ref[2:3, :] += _colsum(dya * yar)
        dyarb = (dya * pscale).astype(BF16)
        dyar_ref[...] = dyarb
        sb = jax.nn.sigmoid(p5_ref[:, 4 * d:5 * d].astype(F32))
        dybb = (dm * sb).astype(BF16)
        dyb_ref[...] = dybb
        dh = dh + dproj(slice(5 * d, 6 * d), dm * yb_ref[...].astype(F32) * sb * (1.0 - sb))

        for g, window in enumerate(POOL_WINDOWS):
            cols = slice(g * gw, (g + 1) * gw)
            rows = 8 * (window - 1)
            dpg = _dot_nt(dyarb[:, cols], wp_ref[g])
            dpgs = dpg * _inv_count(tix * tm, window)
            halo = _halo_bottom(dpgs[:rows, :], dpgcarry[:rows, cols])
            dpgcarry[:, cols] = dpgs[:pool_rows, :]
            s, shift = jnp.concatenate([dpgs, halo], axis=0), 1
            while shift < window:
                s = s[:s.shape[0] - 8 * shift, :] + s[8 * shift:, :]
                shift *= 2
            dh = dh + dproj(cols, s - dpg)

        dq = _dot_nt(dybb, wb_ref[...])
        ux = p5_ref[:, 0:d].astype(F32)
        uc = p5_ref[:, 2 * d:3 * d].astype(F32)
        dh = dh + dproj(slice(2 * d, 3 * d), dq * cv_ref[...].astype(F32))
        dcv = dq * p5_ref[:, d:2 * d].astype(F32)
        taps = (vec_ref[4:5, :], vec_ref[5:6, :], vec_ref[6:7, :])
        dpv, (t0, t1, t2) = _conv3_bwd(dcv, dcvcarry, slice(0, d), taps, uc * ux)
        red_ref[3:4, :] += _colsum(dcv)
        red_ref[4:5, :] += t0
        red_ref[5:6, :] += t1
        red_ref[6:7, :] += t2
        dh = dh + dproj(slice(d, 2 * d), dpv * uc)
        dh = dh + dproj(slice(3 * d, 4 * d), dpv * ux)

        _, r, nh = _prenorm(_interleave(x_ref[...]), vp_ref)
        gx_ref[...] = _deinterleave(dx1_v + _prenorm_bwd(dh, r, nh, vp_ref, pred_ref))

    rev = lambda n: pl.BlockSpec((tm, n), lambda i: (nt - 1 - i, 0))
    return pl.pallas_call(
        body, name="mixer_block_bwd", grid=(nt,),
        in_specs=[rev(d)] * 5 + [rev(5 * d), rev(d)] + [_vmem()] * 6,
        out_specs=[rev(d), rev(d), rev(d), rev(6 * d), rev(d), pl.BlockSpec((16, d), lambda i: (0, 0)),
                   pl.BlockSpec((8, d), lambda i: (0, 0))],
        out_shape=[_sds((t, d), BF16), _sds((t, d), BF16), _sds((t, d), BF16), _sds((t, 6 * d), BF16),
                   _sds((t, d), F32), _sds((16, d), F32), _sds((8, d), F32)],
        scratch_shapes=[pltpu.VMEM((pool_rows, d), F32), pltpu.VMEM((16, d), F32)],
        compiler_params=_params("arbitrary"),
    )(dx1, ob, yarb, ybb, cvb, p5b, x, vec_pre, vec, w_in, w_pool, w_bout, w_o)


def _matmul_tn(a, b, bm, bn, tk, by_col_block, name):
    t, m = a.shape
    n = b.shape[1]
    nk = t // tk

    def body(a_ref, b_ref, o_ref, acc_ref):
        k = pl.program_id(2)

        @pl.when(k == 0)
        def _():
            acc_ref[...] = jnp.zeros_like(acc_ref)

        acc_ref[...] += _dot_tn(a_ref[...], b_ref[...])

        @pl.when(k == nk - 1)
        def _():
            o_ref[...] = acc_ref[...].astype(o_ref.dtype)

    if by_col_block:
        out_shape = _sds((n // bn, m, bn), BF16)
        out_spec = pl.BlockSpec((None, bm, bn), lambda i, j, k: (j, i, 0))
    else:
        out_shape = _sds((m, n), BF16)
        out_spec = pl.BlockSpec((bm, bn), lambda i, j, k: (i, j))
    return pl.pallas_call(
        body, name=name, grid=(m // bm, n // bn, nk),
        in_specs=[pl.BlockSpec((tk, bm), lambda i, j, k: (k, i)), pl.BlockSpec((tk, bn), lambda i, j, k: (k, j))],
        out_specs=out_spec, out_shape=out_shape,
        scratch_shapes=[pltpu.VMEM((bm, bn), F32)],
        compiler_params=_params("parallel", "parallel", "arbitrary"),
    )(a, b)


def _matmul_tn_groups(a, b, groups, tk, name):
    t, m = a.shape
    w = m // groups
    nk = t // tk

    def body(a_ref, b_ref, o_ref, acc_ref):
        k = pl.program_id(1)

        @pl.when(k == 0)
        def _():
            acc_ref[...] = jnp.zeros_like(acc_ref)

        acc_ref[...] += _dot_tn(a_ref[...], b_ref[...])

        @pl.when(k == nk - 1)
        def _():
            o_ref[...] = acc_ref[...].astype(o_ref.dtype)

    blk = pl.BlockSpec((tk, w), lambda g, k: (k, g))
    return pl.pallas_call(
        body, name=name, grid=(groups, nk), in_specs=[blk, blk],
        out_specs=pl.BlockSpec((None, w, w), lambda g, k: (g, 0, 0)), out_shape=_sds((groups, w, w), BF16),
        scratch_shapes=[pltpu.VMEM((w, w), F32)],
        compiler_params=_params("parallel", "arbitrary"),
    )(a, b)


def _round_up(n, k):
    return (n + k - 1) // k * k


def _pad_last(a, parts, width):
    lead = a.shape[:-1]
    piece = a.shape[-1] // parts
    a = a.reshape(lead + (parts, piece))
    a = jnp.pad(a, [(0, 0)] * len(lead) + [(0, 0), (0, width - piece)])
    return a.reshape(lead + (parts * width,))


def _unpad_last(a, parts, piece):
    lead = a.shape[:-1]
    width = a.shape[-1] // parts
    return a.reshape(lead + (parts, width))[..., :piece].reshape(lead + (parts * piece,))


def _rows8(rows, width):
    n = _round_up(len(rows), 8)
    rows = list(rows) + [jnp.zeros((1, width), F32)] * (n - len(rows))
    return jnp.concatenate(rows, axis=0)


def kernel(x, c, g_pre_mix, g_post_mix, g_pre_ffn, g_post_ffn, w_ada, b_ada, w_in, w_pool, pool_scale, conv_w, conv_b, w_bout, w_o, w_up, ffn_conv_w, ffn_conv_b, w_down, loss_target, m_g_pre_mix, m_g_post_mix, m_g_pre_ffn, m_g_post_ffn, m_w_ada, m_b_ada, m_w_in, m_w_pool, m_pool_scale, m_conv_w, m_conv_b, m_w_bout, m_w_o, m_w_up, m_ffn_conv_w, m_ffn_conv_b, m_w_down, v_g_pre_mix, v_g_post_mix, v_g_pre_ffn, v_g_post_ffn, v_w_ada, v_b_ada, v_w_in, v_w_pool, v_pool_scale, v_conv_w, v_conv_b, v_w_bout, v_w_o, v_w_up, v_ffn_conv_w, v_ffn_conv_b, v_w_down):
    t, d = x.shape[1], x.shape[2]
    ngroups = len(POOL_WINDOWS)
    gw = d // ngroups
    ada_n = w_ada.shape[2]
    in_n = w_in.shape[2]
    up_n = w_up.shape[2]
    unit = w_down.shape[1]
    unit_p = _round_up(unit, LANES)
    fp = NDEV * unit_p
    ff2 = NDEV * up_n
    tm = min(256, t)
    tm_big = min(512, t)

    xi, yi, ci = _position()
    me = _linear(xi, yi, ci)
    chip = 2 * xi + yi
    core = jnp.reshape(ci, (1,)).astype(jnp.int32)
    sel = jnp.stack([2 * chip + ci, chip]).astype(jnp.int32)

    x2 = x.reshape(t, d)
    target = loss_target.reshape(t, d)

    cw_n = conv_w.shape[2]
    fcw_p = _pad_last(ffn_conv_w[0], 2, unit_p)
    pack = jnp.concatenate([c.reshape(1, d), conv_w[0].reshape(1, 3 * cw_n), fcw_p.reshape(1, 6 * unit_p)], axis=1)
    pack_n = _round_up(pack.shape[1], LANES)
    pack = jnp.pad(pack, ((0, 0), (0, pack_n - pack.shape[1])))
    pack = jnp.pad(pack, ((0, 7), (0, 0)))
    gathered = _small_allgather(pack, "allgather_cond", False)[0][:, 0, :]
    c_all = gathered[:, :d]
    conv_w_full = gathered[:, d:d + 3 * cw_n].reshape(NDEV, 3, cw_n).transpose(1, 0, 2).reshape(3, NDEV * cw_n)
    fcw_full = gathered[:, d + 3 * cw_n:d + 3 * cw_n + 6 * unit_p].reshape(NDEV, 3, 2 * unit_p)
    fcw_full = fcw_full.transpose(1, 0, 2).reshape(3, 2 * fp)
    fcb_full = _pad_last(ffn_conv_b, 2 * NDEV, unit_p)
    fcv = jnp.concatenate([fcw_full, fcb_full, jnp.zeros((4, 2 * fp), F32)], axis=0)

    c16 = jnp.pad(c_all, ((0, 8), (0, 0))).astype(BF16)
    b_piece = lax.dynamic_slice_in_dim(b_ada, me * ada_n, ada_n, axis=1)
    mod_piece = _mod_piece(c16, w_ada[0], b_piece)[:NDEV]
    mod_all = _small_allgather(mod_piece, "allgather_mod", False)[0]
    mod = lax.dynamic_index_in_dim(mod_all, me, axis=1, keepdims=False).reshape(1, NDEV * ada_n)
    sh1, sc1, gt1, sh2, sc2, gt2 = [mod[:, k * d:(k + 1) * d] for k in range(6)]

    w_in_s = w_in[0].astype(BF16)
    w_up_s = _pad_last(w_up[0], 2, unit_p).astype(BF16)
    w_down_s = jnp.pad(w_down[0], ((0, unit_p - unit), (0, 0))).astype(BF16)
    w_bout_s = w_bout[0].astype(BF16)
    w_o_s = w_o[0].astype(BF16)
    w_pool_s = w_pool[0].reshape(-1, gw).astype(BF16)
    g_in, g_up, g_down, g_bout, g_o, g_pool = _allgather_weights(
        [w_in_s, w_up_s, w_down_s, w_bout_s, w_o_s, w_pool_s])
    w_in_f = g_in.transpose(1, 0, 2).reshape(d, NDEV * in_n)
    w_up_f = g_up.transpose(1, 0, 2).reshape(d, 2 * fp)
    w_down_f = g_down.reshape(fp, d)
    w_bout_f = g_bout.reshape(d, d)
    w_o_f = g_o.reshape(d, d)
    w_pool_f = g_pool.reshape(NDEV, ngroups, gw // NDEV, gw).transpose(1, 0, 2, 3).reshape(ngroups, gw, gw)

    vec_pre_mix = _rows8([g_pre_mix, 1.0 + sc1, sh1], d)
    vec_mix = _rows8([gt1, g_post_mix, pool_scale, conv_b, conv_w_full[0:1], conv_w_full[1:2], conv_w_full[2:3]], d)
    h1b, p5b, pgb, qb, cvb, yarb, ybb, mergedb, ob, x1 = _mixer_block_fwd(
        x2, vec_pre_mix, vec_mix, w_in_f, w_pool_f, w_bout_f, w_o_f)
    vec_pre_ffn = _rows8([g_pre_ffn, 1.0 + sc2, sh2], d)
    vec_ffn = _rows8([gt2, g_post_ffn], d)
    h2b, upb, upreb, ab, ffb, dy, loss_part = _ffn_block_fwd(x1, target, vec_pre_ffn, vec_ffn, fcv, w_up_f, w_down_f)
    loss = lax.psum(0.5 * loss_part[0, 0], ("x", "y", "c"))

    dffb, dupre, dx1, red_ffn, red_fconv, red_pre_ffn = _ffn_block_bwd(
        dy, ffb, x1, upb, upreb, vec_pre_ffn, vec_ffn, fcv, w_up_f, w_down_f)
    dob, dyarb, dybb, dproj, grad_x, red_mix, red_pre_mix = _mixer_block_bwd(
        dx1, ob, yarb, ybb, cvb, p5b, x2, vec_pre_mix, vec_mix, w_in_f, w_pool_f, w_bout_f, w_o_f)

    tk = min(2048, t)
    gr_in = _matmul_tn(h1b, dproj, d, in_n, tk, True, "wgrad_in")
    gr_up = _matmul_tn(h2b, dupre, d, 2 * unit_p, tk, True, "wgrad_up")
    gr_down = _matmul_tn(ab, dffb, 2 * unit_p, d, tk, False, "wgrad_down").reshape(NDEV, unit_p, d)
    gr_bout = _matmul_tn(qb, dybb, d, d, tk, False, "wgrad_bout").reshape(NDEV, d // NDEV, d)
    gr_o = _matmul_tn(mergedb, dob, d, d, tk, False, "wgrad_o").reshape(NDEV, d // NDEV, d)
    gr_pool = _matmul_tn_groups(pgb, dyarb, ngroups, min(4096, t), "wgrad_pool")
    gr_pool = gr_pool.reshape(ngroups, NDEV, gw // NDEV, gw).transpose(1, 0, 2, 3).reshape(NDEV, -1, gw)

    partials = [gr_in, gr_up, gr_down, gr_bout, gr_o, gr_pool]
    from_sibling = _exchange_sibling(partials)
    chip_sums = [_chip_sum(g, s, core) for g, s in zip(partials, from_sibling)]
    from_chips = _exchange_chips(chip_sums)

    def big(k, w, m, v, name, col_parts=0):
        shape = w.shape
        w2, m2, v2 = [a.reshape((-1, shape[-1])) for a in (w, m, v)]
        if col_parts:
            w2, m2, v2 = [_pad_last(a, col_parts, partials[k].shape[2] // col_parts) for a in (w2, m2, v2)]
        outs = _reduce_adamw(partials[k], from_sibling[k], from_chips[k], sel, w2, m2, v2, name)
        if col_parts:
            outs = [_unpad_last(a, col_parts, shape[-1] // col_parts) for a in outs]
        return [a.reshape(shape) for a in outs]

    g_w_in, d_w_in, nm_w_in, nv_w_in = big(0, w_in, m_w_in, v_w_in, "adamw_in")
    g_w_up, d_w_up, nm_w_up, nv_w_up = big(1, w_up, m_w_up, v_w_up, "adamw_up", col_parts=2)
    g_w_down, d_w_down, nm_w_down, nv_w_down = big(2, w_down, m_w_down, v_w_down, "adamw_down")
    g_w_bout, d_w_bout, nm_w_bout, nv_w_bout = big(3, w_bout, m_w_bout, v_w_bout, "adamw_bout")
    g_w_o, d_w_o, nm_w_o, nv_w_o = big(4, w_o, m_w_o, v_w_o, "adamw_o")
    g_w_pool, d_w_pool, nm_w_pool, nv_w_pool = big(5, w_pool, m_w_pool, v_w_pool, "adamw_pool")

    dmod = [red_pre_mix[0:1], red_pre_mix[1:2], red_mix[1:2], red_pre_ffn[0:1], red_pre_ffn[1:2], red_ffn[1:2]]
    small = [red_pre_mix[2:3], red_mix[0:1], red_pre_ffn[2:3], red_ffn[0:1], red_mix[2:3], red_mix[3:4],
             red_mix[4:5], red_mix[5:6], red_mix[6:7]] + dmod
    flat = jnp.concatenate(small + [red_fconv[0:4].reshape(1, 8 * fp)], axis=1)
    flat_n = flat.shape[1]
    width = 8 * LANES
    rows = _round_up(-(-flat_n // width), 8)
    flat = jnp.pad(flat, ((0, 0), (0, rows * width - flat_n))).reshape(rows, width)
    gat, tot = _small_allgather(flat, "allreduce_small", True)
    tot = tot.reshape(1, rows * width)
    gat = gat.reshape(NDEV, rows * width)
    take = lambda k: tot[:, k * d:(k + 1) * d]
    g_g_pre_mix, g_g_post_mix, g_g_pre_ffn, g_g_post_ffn, g_pool_scale, g_conv_b = [take(k) for k in range(6)]
    g_conv_w_full = jnp.concatenate([take(6), take(7), take(8)], axis=0)
    g_conv_w = lax.dynamic_slice_in_dim(g_conv_w_full, me * cw_n, cw_n, axis=1)
    g_b_ada = tot[:, 9 * d:15 * d]
    dmod_all = gat[:, 9 * d:15 * d]
    fconv_tot = tot[:, 15 * d:15 * d + 8 * fp].reshape(4, 2 * fp)
    g_ffn_conv_b = _unpad_last(fconv_tot[3:4], 2 * NDEV, unit)
    g_fcw_mine = lax.dynamic_slice_in_dim(fconv_tot[0:3], me * 2 * unit_p, 2 * unit_p, axis=1)
    g_ffn_conv_w = _unpad_last(g_fcw_mine, 2, unit)
    dmod_piece = lax.dynamic_slice_in_dim(dmod_all, me * ada_n, ada_n, axis=1)
    g_w_ada = _wada_grad(c16, jnp.pad(dmod_piece, ((0, 8), (0, 0))).astype(BF16))

    names_small = [(g_pre_mix, g_g_pre_mix, m_g_pre_mix, v_g_pre_mix), (g_post_mix, g_g_post_mix, m_g_post_mix, v_g_post_mix),
                   (g_pre_ffn, g_g_pre_ffn, m_g_pre_ffn, v_g_pre_ffn), (g_post_ffn, g_g_post_ffn, m_g_post_ffn, v_g_post_ffn),
                   (b_ada, g_b_ada, m_b_ada, v_b_ada), (pool_scale, g_pool_scale, m_pool_scale, v_pool_scale),
                   (conv_w, g_conv_w, m_conv_w, v_conv_w), (conv_b, g_conv_b, m_conv_b, v_conv_b),
                   (ffn_conv_w, g_ffn_conv_w, m_ffn_conv_w, v_ffn_conv_w), (ffn_conv_b, g_ffn_conv_b, m_ffn_conv_b, v_ffn_conv_b)]
    sizes = [w.size for w, _, _, _ in names_small]
    total = sum(sizes)
    prow = _round_up(-(-total // width), 8)

    def pack_small(k):
        a = jnp.concatenate([q[k].reshape(1, -1) for q in names_small], axis=1)
        return jnp.pad(a, ((0, 0), (0, prow * width - total)), constant_values=1.0).reshape(prow, width)

    ds, ms, vs = _adamw(pack_small(0), pack_small(1), pack_small(2), pack_small(3), "adamw_small")

    def unpack_small(a):
        a = a.reshape(-1)
        out, off = [], 0
        for (w, _, _, _), n in zip(names_small, sizes):
            out.append(a[off:off + n].reshape(w.shape))
            off += n
        return out

    (d_g_pre_mix, d_g_post_mix, d_g_pre_ffn, d_g_post_ffn, d_b_ada, d_pool_scale, d_conv_w, d_conv_b,
     d_ffn_conv_w, d_ffn_conv_b) = unpack_small(ds)
    (nm_g_pre_mix, nm_g_post_mix, nm_g_pre_ffn, nm_g_post_ffn, nm_b_ada, nm_pool_scale, nm_conv_w, nm_conv_b,
     nm_ffn_conv_w, nm_ffn_conv_b) = unpack_small(ms)
    (nv_g_pre_mix, nv_g_post_mix, nv_g_pre_ffn, nv_g_post_ffn, nv_b_ada, nv_pool_scale, nv_conv_w, nv_conv_b,
     nv_ffn_conv_w, nv_ffn_conv_b) = unpack_small(vs)
    d_w_ada, nm_w_ada, nv_w_ada = [a.reshape(w_ada.shape) for a in
                                   _adamw(w_ada[0], g_w_ada, m_w_ada[0], v_w_ada[0], "adamw_ada")]

    grads = [g_g_pre_mix, g_g_post_mix, g_g_pre_ffn, g_g_post_ffn, g_w_ada.reshape(w_ada.shape), g_b_ada, g_w_in,
             g_w_pool, g_pool_scale, g_conv_w.reshape(conv_w.shape), g_conv_b, g_w_bout, g_w_o, g_w_up,
             g_ffn_conv_w.reshape(ffn_conv_w.shape), g_ffn_conv_b, g_w_down]
    deltas = [d_g_pre_mix, d_g_post_mix, d_g_pre_ffn, d_g_post_ffn, d_w_ada, d_b_ada, d_w_in, d_w_pool, d_pool_scale,
              d_conv_w, d_conv_b, d_w_bout, d_w_o, d_w_up, d_ffn_conv_w, d_ffn_conv_b, d_w_down]
    new_m = [nm_g_pre_mix, nm_g_post_mix, nm_g_pre_ffn, nm_g_post_ffn, nm_w_ada, nm_b_ada, nm_w_in, nm_w_pool,
             nm_pool_scale, nm_conv_w, nm_conv_b, nm_w_bout, nm_w_o, nm_w_up, nm_ffn_conv_w, nm_ffn_conv_b, nm_w_down]
    new_v = [nv_g_pre_mix, nv_g_post_mix, nv_g_pre_ffn, nv_g_post_ffn, nv_w_ada, nv_b_ada, nv_w_in, nv_w_pool,
             nv_pool_scale, nv_conv_w, nv_conv_b, nv_w_bout, nv_w_o, nv_w_up, nv_ffn_conv_w, nv_ffn_conv_b, nv_w_down]
    return (loss, grad_x.reshape(x.shape), *grads, *deltas, *new_m, *new_v)
```

```python
import math

import jax
import jax.numpy as jnp
from jax import lax
from jax.experimental import pallas as pl
from jax.experimental.pallas import tpu as pltpu

F32 = jnp.float32
BF16 = jnp.bfloat16
MESH = pl.DeviceIdType.MESH

NDEV = 8
NCHIP = 4
EPS = 1e-6
POOL_WINDOWS = (2, 4, 8, 16)
LANES = 128
ADAM_LR = 0.001
ADAM_B1 = 0.9
ADAM_B2 = 0.999
ADAM_EPS = 1e-08
ADAM_WD = 0.01
ADAM_STEP = 10
GELU_C0 = math.sqrt(2.0 / math.pi)
GELU_C1 = 0.044715
VMEM_LIMIT = 56 * 2**20


def _vmem():
    return pl.BlockSpec(memory_space=pltpu.VMEM)


def _any():
    return pl.BlockSpec(memory_space=pl.ANY)


def _params(*sem):
    return pltpu.CompilerParams(dimension_semantics=sem, vmem_limit_bytes=VMEM_LIMIT)


def _sds(shape, dtype):
    return jax.ShapeDtypeStruct(tuple(shape), dtype)


def _position():
    return lax.axis_index("x"), lax.axis_index("y"), lax.axis_index("c")


def _linear(x, y, c):
    return 4 * x + 2 * y + c


def _dot(a, b):
    return jnp.dot(a, b, preferred_element_type=F32)


def _dot_nt(a, b):
    return lax.dot_general(a, b, (((1,), (1,)), ((), ())), preferred_element_type=F32)


def _dot_tn(a, b):
    return lax.dot_general(a, b, (((0,), (0,)), ((), ())), preferred_element_type=F32)


def _colsum(v):
    return jnp.sum(v, axis=0, keepdims=True)


def _rowmean(v):
    return jnp.mean(v, axis=-1, keepdims=True)


GROUP = 256


def _interleave(v):
    g, n = v.shape
    return jnp.swapaxes(v.reshape(8, g // 8, n), 0, 1).reshape(g, n)


def _deinterleave(v):
    g, n = v.shape
    return jnp.swapaxes(v.reshape(g // 8, 8, n), 0, 1).reshape(g, n)


def _halo_top(cur_last, prev_last):
    rows, n = cur_last.shape
    c3 = cur_last.reshape(rows // 8, 8, n)
    p3 = prev_last.reshape(rows // 8, 8, n)
    sub = lax.broadcasted_iota(jnp.int32, c3.shape, 1)
    return jnp.where(sub == 0, pltpu.roll(p3, 1, 1), pltpu.roll(c3, 1, 1)).reshape(rows, n)


def _halo_bottom(cur_first, next_first):
    rows, n = cur_first.shape
    c3 = cur_first.reshape(rows // 8, 8, n)
    n3 = next_first.reshape(rows // 8, 8, n)
    sub = lax.broadcasted_iota(jnp.int32, c3.shape, 1)
    return jnp.where(sub == 7, pltpu.roll(n3, 7, 1), pltpu.roll(c3, 7, 1)).reshape(rows, n)


def _shift_down(v, halo, k):
    rows = v.shape[0]
    return jnp.concatenate([halo[halo.shape[0] - 8 * k:, :], v[:rows - 8 * k, :]], axis=0)


def _shift_up(v, halo, k):
    return jnp.concatenate([v[8 * k:, :], halo[:8 * k, :]], axis=0)


def _inv_count(first_token, window):
    row = lax.broadcasted_iota(jnp.int32, (GROUP, 1), 0)
    t = first_token + (row % 8) * (GROUP // 8) + row // 8
    return 1.0 / jnp.minimum(t + 1, window).astype(F32)


class _Job:
    def __init__(self, inputs, out_shape, scratch, phases):
        self.inputs, self.out_shape, self.scratch, self.phases = list(inputs), list(out_shape), list(scratch), phases


def _call(body, name, grid, in_specs, out_specs, out_shape, scratch_shapes, params, operands, job=None):
    if job is None:
        return pl.pallas_call(body, name=name, grid=grid, in_specs=in_specs, out_specs=out_specs, out_shape=out_shape,
                              scratch_shapes=scratch_shapes, compiler_params=params)(*operands)
    n_in, n_out, n_scr = len(in_specs), len(out_specs), len(scratch_shapes)
    j_in, j_out = len(job.inputs), len(job.out_shape)
    steps = math.prod(grid)

    def hosted(*refs):
        own_in, refs = refs[:n_in], refs[n_in:]
        jin, refs = refs[:j_in], refs[j_in:]
        own_out, refs = refs[:n_out], refs[n_out:]
        jout, refs = refs[:j_out], refs[j_out:]
        own_scr, jscr = refs[:n_scr], refs[n_scr:]
        step = pl.program_id(0)
        for axis in range(1, len(grid)):
            step = step * grid[axis] + pl.program_id(axis)
        for frac, fn in job.phases[:-1]:
            pl.when(step == int(frac * (steps - 1)))(lambda fn=fn: fn(jin, jout, jscr))
        body(*own_in, *own_out, *own_scr)
        pl.when(step == steps - 1)(lambda: job.phases[-1][1](jin, jout, jscr))

    return pl.pallas_call(
        hosted, name=name, grid=grid, in_specs=list(in_specs) + [_any()] * j_in,
        out_specs=list(out_specs) + [_any()] * j_out, out_shape=list(out_shape) + job.out_shape,
        scratch_shapes=list(scratch_shapes) + job.scratch, compiler_params=params)(*operands, *job.inputs)


def _run_job(job, name):
    n_in, n_out = len(job.inputs), len(job.out_shape)

    def body(*refs):
        for _, fn in job.phases:
            fn(refs[:n_in], refs[n_in:n_in + n_out], refs[n_in + n_out:])

    return pl.pallas_call(body, name=name, out_shape=job.out_shape, in_specs=[_any()] * n_in,
                          out_specs=[_any()] * n_out, scratch_shapes=job.scratch)(*job.inputs)


def _peers(x, y, c):
    out = []
    for k in range(1, NDEV):
        out.append(((1 - x) if k & 4 else x, (1 - y) if k & 2 else y, (1 - c) if k & 1 else c))
    return out


def _small_allgather(v, name, with_sum):
    r, n = v.shape

    def body(v_ref, gat_ref, *rest):
        if with_sum:
            sum_ref, send_sems, recv_sems, local_sem = rest
        else:
            send_sems, recv_sems, local_sem = rest
        x, y, c = _position()
        me = _linear(x, y, c)
        mine = pltpu.make_async_copy(v_ref, gat_ref.at[me], local_sem)
        mine.start()
        peers = _peers(x, y, c)
        sends = []
        for k, peer in enumerate(peers):
            cp = pltpu.make_async_remote_copy(src_ref=v_ref, dst_ref=gat_ref.at[me], send_sem=send_sems.at[k],
                                              recv_sem=recv_sems.at[k], device_id=peer, device_id_type=MESH)
            cp.start()
            sends.append(cp)
        for k, peer in enumerate(peers):
            pltpu.make_async_remote_copy(src_ref=v_ref, dst_ref=gat_ref.at[_linear(*peer)], send_sem=send_sems.at[k],
                                         recv_sem=recv_sems.at[k], device_id=peer, device_id_type=MESH).wait_recv()
        for cp in sends:
            cp.wait_send()
        mine.wait()
        if with_sum:
            acc = gat_ref[0]
            for j in range(1, NDEV):
                acc = acc + gat_ref[j]
            sum_ref[...] = acc

    out_shape = [_sds((NDEV, r, n), F32)] + ([_sds((r, n), F32)] if with_sum else [])
    return pl.pallas_call(
        body, name=name, out_shape=out_shape, in_specs=[_vmem()], out_specs=[_vmem()] * len(out_shape),
        scratch_shapes=[pltpu.SemaphoreType.DMA((NDEV - 1,)), pltpu.SemaphoreType.DMA((NDEV - 1,)),
                        pltpu.SemaphoreType.DMA(())],
    )(v)


def _gathered(shard, layout):
    if layout == "rows":
        return (NDEV,) + shard.shape, lambda ref, j: ref.at[j]
    if layout == "cols":
        r, c = shard.shape
        return (r, NDEV * c), lambda ref, j: ref.at[:, pl.ds(pl.multiple_of(j * c, LANES), c)]
    g, r, c = shard.shape
    return (g, NDEV * r, c), lambda ref, j: ref.at[:, pl.ds(pl.multiple_of(j * r, 16), r), :]


def _allgather_job(shards, layouts, forward_at):
    n = len(shards)
    specs = [_gathered(s, l) for s, l in zip(shards, layouts)]

    def copies(src, dst, sems):
        send_sems, recv_sems, _ = sems
        x, y, c = _position()
        me, sibling = (x, y, c), (x, y, 1 - c)
        chips = [(1 - x, y), (x, 1 - y), (1 - x, 1 - y)]

        def copy(a, k, block, to, from_src=False):
            blk = specs[a][1](dst[a], _linear(*block))
            return pltpu.make_async_remote_copy(src_ref=src[a] if from_src else blk, dst_ref=blk,
                                                send_sem=send_sems.at[a, k], recv_sem=recv_sems.at[a, k],
                                                device_id=to, device_id_type=MESH)
        return copy, me, sibling, chips, c

    def local(src, dst, sems):
        x, y, c = _position()
        return [pltpu.make_async_copy(src[a], specs[a][1](dst[a], _linear(x, y, c)), sems[2].at[a]) for a in range(n)]

    def first(src, dst, sems):
        copy, me, sibling, chips, c = copies(src, dst, sems)
        out = [copy(a, 1 + j, me, (*chip, c), from_src=True) for j, chip in enumerate(chips) for a in range(n)]
        return out + [copy(a, 0, me, sibling, from_src=True) for a in range(n)]

    def passed(src, dst, sems):
        copy, me, sibling, chips, c = copies(src, dst, sems)
        return [copy(a, 4 + j, (*chip, c), sibling) for j, chip in enumerate(chips) for a in range(n)]

    def start(src, dst, sems):
        for cp in local(src, dst, sems) + first(src, dst, sems):
            cp.start()

    def forward(src, dst, sems):
        copy, me, sibling, chips, c = copies(src, dst, sems)
        for j, chip in enumerate(chips):
            for a in range(n):
                copy(a, 1 + j, (*chip, c), me).wait_recv()
        for cp in passed(src, dst, sems):
            cp.start()

    def finish(src, dst, sems):
        copy, me, sibling, chips, c = copies(src, dst, sems)
        for a in range(n):
            copy(a, 0, sibling, me).wait_recv()
        for j, chip in enumerate(chips):
            for a in range(n):
                copy(a, 4 + j, (*chip, 1 - c), me).wait_recv()
        for cp in first(src, dst, sems) + passed(src, dst, sems):
            cp.wait_send()
        for cp in local(src, dst, sems):
            cp.wait()

    return _Job(shards, [_sds(spec[0], s.dtype) for spec, s in zip(specs, shards)],
                [pltpu.SemaphoreType.DMA((n, 7)), pltpu.SemaphoreType.DMA((n, 7)), pltpu.SemaphoreType.DMA((n,))],
                [(0.0, start), (forward_at, forward), (1.0, finish)])


def _sibling_job(grads):
    n = len(grads)

    def copies(src, dst, sems):
        x, y, c = _position()
        return [pltpu.make_async_remote_copy(src_ref=src[a].at[2 * q + 1 - c], dst_ref=dst[a].at[q],
                                             send_sem=sems[0].at[a, q], recv_sem=sems[1].at[a, q],
                                             device_id=(x, y, 1 - c), device_id_type=MESH)
                for a in range(n) for q in range(NCHIP)]

    return _exchange_job(grads, NCHIP, copies)


def _chips_job(chip_sums):
    n = len(chip_sums)

    def copies(src, dst, sems):
        x, y, c = _position()
        chips = [(1 - x, y), (x, 1 - y), (1 - x, 1 - y)]
        return [pltpu.make_async_remote_copy(src_ref=src[a].at[2 * chip[0] + chip[1]], dst_ref=dst[a].at[j],
                                             send_sem=sems[0].at[a, j], recv_sem=sems[1].at[a, j],
                                             device_id=(*chip, c), device_id_type=MESH)
                for j, chip in enumerate(chips) for a in range(n)]

    return _exchange_job(chip_sums, 3, copies)


def _exchange_job(arrays, slots, copies):
    n = len(arrays)

    def start(src, dst, sems):
        for cp in copies(src, dst, sems):
            cp.start()

    def finish(src, dst, sems):
        cps = copies(src, dst, sems)
        for cp in cps:
            cp.wait_recv()
        for cp in cps:
            cp.wait_send()

    return _Job(arrays, [_sds((slots,) + a.shape[1:], a.dtype) for a in arrays],
                [pltpu.SemaphoreType.DMA((n, slots)), pltpu.SemaphoreType.DMA((n, slots))],
                [(0.0, start), (1.0, finish)])


def _row_block(r):
    for rb in (512, 256, 128, 64, 32, 16):
        if r % rb == 0:
            return rb
    return r


def _chip_sum(grad, from_sibling, core):
    _, r, c = grad.shape
    rb = _row_block(r)

    def body(core_ref, g_ref, s_ref, o_ref):
        del core_ref
        o_ref[...] = (g_ref[...].astype(F32) + s_ref[...].astype(F32)).astype(o_ref.dtype)

    grid_spec = pltpu.PrefetchScalarGridSpec(
        num_scalar_prefetch=1, grid=(NCHIP, r // rb),
        in_specs=[pl.BlockSpec((None, rb, c), lambda q, i, core: (2 * q + core[0], i, 0)),
                  pl.BlockSpec((None, rb, c), lambda q, i, core: (q, i, 0))],
        out_specs=pl.BlockSpec((None, rb, c), lambda q, i, core: (q, i, 0)))
    return pl.pallas_call(body, name="rs_chip_sum", grid_spec=grid_spec, out_shape=_sds((NCHIP, r, c), BF16),
                          compiler_params=_params("parallel", "parallel"))(core, grad, from_sibling)


def _adamw_math(w, g, m, v):
    m2 = ADAM_B1 * m + (1.0 - ADAM_B1) * g
    v2 = ADAM_B2 * v + (1.0 - ADAM_B2) * jnp.square(g)
    m_hat = m2 / (1.0 - ADAM_B1 ** ADAM_STEP)
    v_hat = v2 / (1.0 - ADAM_B2 ** ADAM_STEP)
    delta = -ADAM_LR * (m_hat / (jnp.sqrt(v_hat) + ADAM_EPS) + ADAM_WD * w)
    return delta, m2, v2


def _adamw(w, g, m, v, name):
    r, c = w.shape
    rb = _row_block(r)

    def body(w_ref, g_ref, m_ref, v_ref, d_ref, m2_ref, v2_ref):
        d, m2, v2 = _adamw_math(w_ref[...], g_ref[...], m_ref[...], v_ref[...])
        d_ref[...] = d
        m2_ref[...] = m2
        v2_ref[...] = v2

    blk = pl.BlockSpec((rb, c), lambda i: (i, 0))
    return pl.pallas_call(body, name=name, grid=(r // rb,), in_specs=[blk] * 4, out_specs=[blk] * 3,
                          out_shape=[_sds((r, c), F32)] * 3, compiler_params=_params("parallel"))(w, g, m, v)


def _reduce_adamw(grad, from_sibling, from_chips, sel, w, m, v, name):
    r, c = w.shape
    rb = _row_block(r)

    def body(sel_ref, g_ref, s_ref, c0_ref, c1_ref, c2_ref, w_ref, m_ref, v_ref, go_ref, d_ref, m2_ref, v2_ref):
        del sel_ref
        g = g_ref[...].astype(F32) + s_ref[...].astype(F32)
        g = g + c0_ref[...].astype(F32)
        g = g + c1_ref[...].astype(F32)
        g = g + c2_ref[...].astype(F32)
        d, m2, v2 = _adamw_math(w_ref[...], g, m_ref[...], v_ref[...])
        go_ref[...] = g
        d_ref[...] = d
        m2_ref[...] = m2
        v2_ref[...] = v2

    blk = pl.BlockSpec((rb, c), lambda i, sel: (i, 0))
    grid_spec = pltpu.PrefetchScalarGridSpec(
        num_scalar_prefetch=1, grid=(r // rb,),
        in_specs=[pl.BlockSpec((None, rb, c), lambda i, sel: (sel[0], i, 0)),
                  pl.BlockSpec((None, rb, c), lambda i, sel: (sel[1], i, 0)),
                  pl.BlockSpec((None, rb, c), lambda i, sel: (0, i, 0)),
                  pl.BlockSpec((None, rb, c), lambda i, sel: (1, i, 0)),
                  pl.BlockSpec((None, rb, c), lambda i, sel: (2, i, 0)),
                  blk, blk, blk],
        out_specs=[blk] * 4)
    return pl.pallas_call(body, name=name, grid_spec=grid_spec, out_shape=[_sds((r, c), F32)] * 4,
                          compiler_params=_params("parallel"))(sel, grad, from_sibling, from_chips, from_chips,
                                                               from_chips, w, m, v)


def _mod_piece(c_all, w_ada, b_piece):
    rows, _ = c_all.shape
    n = w_ada.shape[1]

    def body(c_ref, w_ref, b_ref, o_ref):
        o_ref[...] = _dot(c_ref[...], w_ref[...].astype(BF16)) + b_ref[...]

    return pl.pallas_call(body, name="ada_mod", out_shape=_sds((rows, n), F32),
                          in_specs=[_vmem()] * 3, out_specs=_vmem())(c_all, w_ada, b_piece)


def _wada_grad(c_all, dmod_piece):
    d = c_all.shape[1]
    n = dmod_piece.shape[1]

    def body(c_ref, dm_ref, o_ref):
        o_ref[...] = _dot_tn(c_ref[...], dm_ref[...])

    return pl.pallas_call(body, name="ada_wgrad", out_shape=_sds((d, n), F32),
                          in_specs=[_vmem()] * 2, out_specs=_vmem())(c_all, dmod_piece)


def _conv_taps(ref, col):
    return ref[0:1, col], ref[1:2, col], ref[2:3, col]


def _gelu_parts(u):
    th = jnp.tanh(GELU_C0 * (u + GELU_C1 * (u * u * u)))
    cdf = 0.5 * (1.0 + th)
    return cdf, th


def _conv3_bwd(dv, carry, col, taps, x):
    halo = _halo_bottom(dv[:16, :], carry[:, col])
    carry[:, col] = dv[:16, :]
    d1 = _shift_up(dv, halo, 1)
    d2 = _shift_up(dv, halo, 2)
    w0, w1, w2 = taps
    dx = w2 * dv
    dx = dx + w1 * d1
    dx = dx + w0 * d2
    return dx, (_colsum(d2 * x), _colsum(d1 * x), _colsum(dv * x))


def _prenorm(x, vp_ref):
    r = lax.rsqrt(_rowmean(x * x) + EPS)
    nh = x * r
    return (nh * vp_ref[0:1, :]) * vp_ref[1:2, :] + vp_ref[2:3, :], r, nh


def _prenorm_bwd(dh, r, nh, vp_ref, red_ref):
    g, sc1 = vp_ref[0:1, :], vp_ref[1:2, :]
    red_ref[0:1, :] += _colsum(dh)
    red_ref[1:2, :] += _colsum(dh * (nh * g))
    red_ref[2:3, :] += _colsum(dh * nh * sc1)
    dnh = dh * g * sc1
    return r * (dnh - nh * _rowmean(dnh * nh))


def _postnorm_bwd(dres, z, gate, gpost, red_ref):
    r = lax.rsqrt(_rowmean(z * z) + EPS)
    nh = z * r
    dn = dres * gate
    red_ref[0:1, :] += _colsum(dn * nh)
    red_ref[1:2, :] += _colsum(dres * (nh * gpost))
    dnh = dn * gpost
    return r * (dnh - nh * _rowmean(dnh * nh))


def _mixer_block_fwd(x, vec_pre, vec, w_in, w_pool, w_bout, w_o, job):
    t, d = x.shape
    tm = GROUP
    gw = d // len(POOL_WINDOWS)
    pool_rows = 8 * (POOL_WINDOWS[-1] - 1)

    def body(x_ref, vp_ref, vec_ref, win_ref, wp_ref, wb_ref, wo_ref,
             hb_ref, p5_ref, pg_ref, qm_ref, cv_ref, yar_ref, yb_ref, o_ref, x1_ref, mbuf, ucarry, pcarry):
        i = pl.program_id(0)

        @pl.when(i == 0)
        def _():
            ucarry[...] = jnp.zeros_like(ucarry)
            pcarry[...] = jnp.zeros_like(pcarry)

        xp = _interleave(x_ref[...])
        hb = _prenorm(xp, vp_ref)[0].astype(BF16)
        hb_ref[...] = hb
        proj = lambda k: _dot(hb, win_ref[:, k * d:(k + 1) * d])

        za = proj(4)
        p5_ref[:, 3 * d:4 * d] = za.astype(BF16)
        sa = jax.nn.sigmoid(za)
        u_pool = proj(0)
        for g, window in enumerate(POOL_WINDOWS):
            cols = slice(g * gw, (g + 1) * gw)
            rows = 8 * (window - 1)
            u = u_pool[:, cols]
            halo = _halo_top(u[tm - rows:, :], ucarry[pool_rows - rows:, cols])
            s, shift = jnp.concatenate([halo, u], axis=0), 1
            while shift < window:
                s = s[8 * shift:, :] + s[:s.shape[0] - 8 * shift, :]
                shift *= 2
            pgb = (s * _inv_count(i * tm, window) - u).astype(BF16)
            pg_ref[:, cols] = pgb
            yar = _dot(pgb, wp_ref[g])
            yar_ref[:, cols] = yar.astype(BF16)
            mbuf[:, cols] = sa[:, cols] * (yar * vec_ref[2:3, cols])
        ucarry[...] = u_pool[tm - pool_rows:, :]

        ux = proj(1)
        uc = proj(3)
        p5_ref[:, 0:d] = ux.astype(BF16)
        p5_ref[:, 2 * d:3 * d] = uc.astype(BF16)
        p = uc * ux
        halo = _halo_top(p[tm - 16:, :], pcarry[...])
        pcarry[...] = p[tm - 16:, :]
        cv = vec_ref[3:4, :] + vec_ref[4:5, :] * _shift_down(p, halo, 2)
        cv = cv + vec_ref[5:6, :] * _shift_down(p, halo, 1)
        cv = cv + vec_ref[6:7, :] * p
        cv_ref[...] = cv.astype(BF16)
        ub = proj(2)
        p5_ref[:, d:2 * d] = ub.astype(BF16)
        qb = (ub * cv).astype(BF16)
        qm_ref[:, 0:d] = qb
        yb = _dot(qb, wb_ref[...])
        yb_ref[...] = yb.astype(BF16)

        zb = proj(5)
        p5_ref[:, 4 * d:5 * d] = zb.astype(BF16)
        mb = (mbuf[...] + jax.nn.sigmoid(zb) * yb).astype(BF16)
        qm_ref[:, d:2 * d] = mb
        o = _dot(mb, wo_ref[...])
        o_ref[...] = o.astype(BF16)
        r2 = lax.rsqrt(_rowmean(o * o) + EPS)
        x1_ref[...] = xp + vec_ref[0:1, :] * ((o * r2) * vec_ref[1:2, :])

    row = lambda n: pl.BlockSpec((tm, n), lambda i: (i, 0))
    widths = [d, 5 * d, d, 2 * d, d, d, d, d]
    return _call(
        body, "mixer_block_fwd", (t // tm,), [row(d)] + [_vmem()] * 6, [row(n) for n in widths] + [row(d)],
        [_sds((t, n), BF16) for n in widths] + [_sds((t, d), F32)],
        [pltpu.VMEM((tm, d), F32), pltpu.VMEM((pool_rows, d), F32), pltpu.VMEM((16, d), F32)],
        _params("arbitrary"), (x, vec_pre, vec, w_in, w_pool, w_bout, w_o), job)


def _ffn_block_fwd(x1, target, vec_pre, vec, fcv, w_up, w_down):
    t, d = x1.shape
    tm = GROUP
    fp = w_down.shape[0]
    cw = fp // 4

    def body(x1_ref, tg_ref, vp_ref, vec_ref, fcv_ref, wu_ref, wd_ref,
             hb_ref, upb_ref, upreb_ref, a_ref, ffb_ref, dy_ref, loss_ref, carry):
        i = pl.program_id(0)

        @pl.when(i == 0)
        def _():
            carry[...] = jnp.zeros_like(carry)
            loss_ref[...] = jnp.zeros_like(loss_ref)

        x1 = x1_ref[...]
        hb = _prenorm(x1, vp_ref)[0].astype(BF16)
        hb_ref[...] = hb

        def conv(col):
            v = _dot(hb, wu_ref[:, col])
            halo = _halo_top(v[tm - 16:, :], carry[:, col])
            carry[:, col] = v[tm - 16:, :]
            w0, w1, w2 = _conv_taps(fcv_ref, col)
            y = fcv_ref[3:4, col] + w0 * _shift_down(v, halo, 2)
            y = y + w1 * _shift_down(v, halo, 1)
            y = y + w2 * v
            upb_ref[:, col] = y.astype(BF16)
            upreb_ref[:, col] = v.astype(BF16)
            return y

        ff = None
        for j in range(4):
            gc = slice(j * cw, (j + 1) * cw)
            gate = conv(gc)
            val = conv(slice(fp + j * cw, fp + (j + 1) * cw))
            ab = ((gate * _gelu_parts(gate)[0]) * val).astype(BF16)
            a_ref[:, gc] = ab
            part = _dot(ab, wd_ref[gc, :])
            ff = part if ff is None else ff + part
        ffb_ref[...] = ff.astype(BF16)
        r4 = lax.rsqrt(_rowmean(ff * ff) + EPS)
        y = x1 + vec_ref[0:1, :] * ((ff * r4) * vec_ref[1:2, :])
        e = y - _interleave(tg_ref[...])
        dy_ref[...] = e * (1.0 / d)
        loss_ref[...] += jnp.sum(_rowmean(e * e))

    row = lambda n: pl.BlockSpec((tm, n), lambda i: (i, 0))
    return pl.pallas_call(
        body, name="ffn_block_fwd", grid=(t // tm,),
        in_specs=[row(d), row(d)] + [_vmem()] * 5,
        out_specs=[row(d), row(2 * fp), row(2 * fp), row(fp), row(d), row(d), pl.BlockSpec((8, LANES), lambda i: (0, 0))],
        out_shape=[_sds((t, d), BF16), _sds((t, 2 * fp), BF16), _sds((t, 2 * fp), BF16), _sds((t, fp), BF16),
                   _sds((t, d), BF16), _sds((t, d), F32), _sds((8, LANES), F32)],
        scratch_shapes=[pltpu.VMEM((16, 2 * fp), F32)],
        compiler_params=_params("arbitrary"),
    )(x1, target, vec_pre, vec, fcv, w_up, w_down)


def _ffn_block_bwd(dy, ffb, x1, upb, upreb, vec_pre, vec, fcv, w_up, w_down):
    t, d = dy.shape
    tm = GROUP
    fp = w_down.shape[0]
    cw = fp // 4
    nt = t // tm

    def body(dy_ref, ff_ref, x1_ref, upb_ref, upreb_ref, vp_ref, vec_ref, fcv_ref, wu_ref, wd_ref,
             dff_ref, dup_ref, dx1_ref, red_ref, cred_ref, pred_ref, carry):
        @pl.when(pl.program_id(0) == 0)
        def _():
            carry[...] = jnp.zeros_like(carry)
            red_ref[...] = jnp.zeros_like(red_ref)
            cred_ref[...] = jnp.zeros_like(cred_ref)
            pred_ref[...] = jnp.zeros_like(pred_ref)

        dy_v = dy_ref[...]
        dffb = _postnorm_bwd(dy_v, ff_ref[...].astype(F32), vec_ref[0:1, :], vec_ref[1:2, :], red_ref).astype(BF16)
        dff_ref[...] = dffb

        def conv_bwd(dv, col):
            dx, (t0, t1, t2) = _conv3_bwd(dv, carry, col, _conv_taps(fcv_ref, col), upreb_ref[:, col].astype(F32))
            cred_ref[0:1, col] += t0
            cred_ref[1:2, col] += t1
            cred_ref[2:3, col] += t2
            cred_ref[3:4, col] += _colsum(dv)
            dxb = dx.astype(BF16)
            dup_ref[:, col] = dxb
            return _dot_nt(dxb, wu_ref[:, col])

        dh = None
        for j in range(4):
            gc = slice(j * cw, (j + 1) * cw)
            vc = slice(fp + j * cw, fp + (j + 1) * cw)
            da = _dot_nt(dffb, wd_ref[gc, :])
            gate = upb_ref[:, gc].astype(F32)
            val = upb_ref[:, vc].astype(F32)
            cdf, th = _gelu_parts(gate)
            dcdf = 0.5 * (1.0 - th * th) * (GELU_C0 * (1.0 + (3.0 * GELU_C1) * (gate * gate)))
            part = conv_bwd(da * val * (cdf + gate * dcdf), gc) + conv_bwd(da * (gate * cdf), vc)
            dh = part if dh is None else dh + part

        _, r, nh = _prenorm(x1_ref[...], vp_ref)
        dx1_ref[...] = dy_v + _prenorm_bwd(dh, r, nh, vp_ref, pred_ref)

    rev = lambda n: pl.BlockSpec((tm, n), lambda i: (nt - 1 - i, 0))
    fixed = lambda n: pl.BlockSpec((8, n), lambda i: (0, 0))
    return pl.pallas_call(
        body, name="ffn_block_bwd", grid=(nt,),
        in_specs=[rev(d), rev(d), rev(d), rev(2 * fp), rev(2 * fp)] + [_vmem()] * 5,
        out_specs=[rev(d), rev(2 * fp), rev(d), fixed(d), fixed(2 * fp), fixed(d)],
        out_shape=[_sds((t, d), BF16), _sds((t, 2 * fp), BF16), _sds((t, d), F32), _sds((8, d), F32),
                   _sds((8, 2 * fp), F32), _sds((8, d), F32)],
        scratch_shapes=[pltpu.VMEM((16, 2 * fp), F32)],
        compiler_params=_params("arbitrary"),
    )(dy, ffb, x1, upb, upreb, vec_pre, vec, fcv, w_up, w_down)


def _mixer_block_bwd(dx1, ob, yarb, ybb, cvb, p5b, x, vec_pre, vec, w_in, w_pool, w_bout, w_o, job):
    t, d = dx1.shape
    tm = GROUP
    gw = d // len(POOL_WINDOWS)
    nt = t // tm
    pool_rows = 8 * (POOL_WINDOWS[-1] - 1)

    def body(dx1_ref, o_ref, yar_ref, yb_ref, cv_ref, p5_ref, x_ref, vp_ref, vec_ref, win_ref, wp_ref, wb_ref, wo_ref,
             dyar_ref, dqm_ref, dp_ref, gx_ref, red_ref, pred_ref, dpgcarry, dcvcarry):
        i = pl.program_id(0)
        tix = nt - 1 - i

        @pl.when(i == 0)
        def _():
            red_ref[...] = jnp.zeros_like(red_ref)
            pred_ref[...] = jnp.zeros_like(pred_ref)
            dpgcarry[...] = jnp.zeros_like(dpgcarry)
            dcvcarry[...] = jnp.zeros_like(dcvcarry)

        pscale = vec_ref[2:3, :]
        dx1_v = dx1_ref[...]
        dob = _postnorm_bwd(dx1_v, o_ref[...].astype(F32), vec_ref[0:1, :], vec_ref[1:2, :], red_ref).astype(BF16)
        dqm_ref[:, d:2 * d] = dob
        dm = _dot_nt(dob, wo_ref[...])

        def dproj(cols, value):
            vb = value.astype(BF16)
            dp_ref[:, cols] = vb
            return _dot_nt(vb, win_ref[:, cols])

        sa = jax.nn.sigmoid(p5_ref[:, 3 * d:4 * d].astype(F32))
        yar = yar_ref[...].astype(F32)
        dya = dm * sa
        dh = dproj(slice(4 * d, 5 * d), dm * (yar * pscale) * sa * (1.0 - sa))
        red_ref[2:3, :] += _colsum(dya * yar)
        dyarb = (dya * pscale).astype(BF16)
        dyar_ref[...] = dyarb
        sb = jax.nn.sigmoid(p5_ref[:, 4 * d:5 * d].astype(F32))
        dybb = (dm * sb).astype(BF16)
        dqm_ref[:, 0:d] = dybb
        dh = dh + dproj(slice(5 * d, 6 * d), dm * yb_ref[...].astype(F32) * sb * (1.0 - sb))

        for g, window in enumerate(POOL_WINDOWS):
            cols = slice(g * gw, (g + 1) * gw)
            rows = 8 * (window - 1)
            dpg = _dot_nt(dyarb[:, cols], wp_ref[g])
            dpgs = dpg * _inv_count(tix * tm, window)
            halo = _halo_bottom(dpgs[:rows, :], dpgcarry[:rows, cols])
            dpgcarry[:, cols] = dpgs[:pool_rows, :]
            s, shift = jnp.concatenate([dpgs, halo], axis=0), 1
            while shift < window:
                s = s[:s.shape[0] - 8 * shift, :] + s[8 * shift:, :]
                shift *= 2
            dh = dh + dproj(cols, s - dpg)

        dq = _dot_nt(dybb, wb_ref[...])
        ux = p5_ref[:, 0:d].astype(F32)
        uc = p5_ref[:, 2 * d:3 * d].astype(F32)
        dh = dh + dproj(slice(2 * d, 3 * d), dq * cv_ref[...].astype(F32))
        dcv = dq * p5_ref[:, d:2 * d].astype(F32)
        taps = (vec_ref[4:5, :], vec_ref[5:6, :], vec_ref[6:7, :])
        dpv, (t0, t1, t2) = _conv3_bwd(dcv, dcvcarry, slice(0, d), taps, uc * ux)
        red_ref[3:4, :] += _colsum(dcv)
        red_ref[4:5, :] += t0
        red_ref[5:6, :] += t1
        red_ref[6:7, :] += t2
        dh = dh + dproj(slice(d, 2 * d), dpv * uc)
        dh = dh + dproj(slice(3 * d, 4 * d), dpv * ux)

        _, r, nh = _prenorm(_interleave(x_ref[...]), vp_ref)
        gx_ref[...] = _deinterleave(dx1_v + _prenorm_bwd(dh, r, nh, vp_ref, pred_ref))

    rev = lambda n: pl.BlockSpec((tm, n), lambda i: (nt - 1 - i, 0))
    return _call(
        body, "mixer_block_bwd", (nt,), [rev(d)] * 5 + [rev(5 * d), rev(d)] + [_vmem()] * 6,
        [rev(d), rev(2 * d), rev(6 * d), rev(d), pl.BlockSpec((16, d), lambda i: (0, 0)),
         pl.BlockSpec((8, d), lambda i: (0, 0))],
        [_sds((t, d), BF16), _sds((t, 2 * d), BF16), _sds((t, 6 * d), BF16), _sds((t, d), F32), _sds((16, d), F32),
         _sds((8, d), F32)],
        [pltpu.VMEM((pool_rows, d), F32), pltpu.VMEM((16, d), F32)],
        _params("arbitrary"), (dx1, ob, yarb, ybb, cvb, p5b, x, vec_pre, vec, w_in, w_pool, w_bout, w_o), job)


def _matmul_tn(a, b, bm, bn, tk, by_col_block, name):
    t, m = a.shape
    n = b.shape[1]
    nk = t // tk

    def body(a_ref, b_ref, o_ref, acc_ref):
        k = pl.program_id(2)

        @pl.when(k == 0)
        def _():
            acc_ref[...] = jnp.zeros_like(acc_ref)

        acc_ref[...] += _dot_tn(a_ref[...], b_ref[...])

        @pl.when(k == nk - 1)
        def _():
            o_ref[...] = acc_ref[...].astype(o_ref.dtype)

    if by_col_block:
        out_shape = _sds((n // bn, m, bn), BF16)
        out_spec = pl.BlockSpec((None, bm, bn), lambda i, j, k: (j, i, 0))
    else:
        out_shape = _sds((m, n), BF16)
        out_spec = pl.BlockSpec((bm, bn), lambda i, j, k: (i, j))
    return pl.pallas_call(
        body, name=name, grid=(m // bm, n // bn, nk),
        in_specs=[pl.BlockSpec((tk, bm), lambda i, j, k: (k, i)), pl.BlockSpec((tk, bn), lambda i, j, k: (k, j))],
        out_specs=out_spec, out_shape=out_shape,
        scratch_shapes=[pltpu.VMEM((bm, bn), F32)],
        compiler_params=_params("parallel", "parallel", "arbitrary"),
    )(a, b)


def _matmul_tn_groups(a, b, groups, tk, name, job=None):
    t, m = a.shape
    w = m // groups
    nk = t // tk

    def body(a_ref, b_ref, o_ref, acc_ref):
        k = pl.program_id(1)

        @pl.when(k == 0)
        def _():
            acc_ref[...] = jnp.zeros_like(acc_ref)

        acc_ref[...] += _dot_tn(a_ref[...], b_ref[...])

        @pl.when(k == nk - 1)
        def _():
            o_ref[...] = acc_ref[...].astype(o_ref.dtype)

    blk = pl.BlockSpec((tk, w), lambda g, k: (k, g))
    out = _call(body, name, (groups, nk), [blk, blk], [pl.BlockSpec((None, w, w), lambda g, k: (g, 0, 0))],
                [_sds((groups, w, w), BF16)], [pltpu.VMEM((w, w), F32)], _params("arbitrary", "arbitrary"), (a, b), job)
    return out if job is not None else out[0]


def _round_up(n, k):
    return (n + k - 1) // k * k


def _pad_last(a, parts, width):
    lead = a.shape[:-1]
    piece = a.shape[-1] // parts
    a = a.reshape(lead + (parts, piece))
    a = jnp.pad(a, [(0, 0)] * len(lead) + [(0, 0), (0, width - piece)])
    return a.reshape(lead + (parts * width,))


def _unpad_last(a, parts, piece):
    lead = a.shape[:-1]
    width = a.shape[-1] // parts
    return a.reshape(lead + (parts, width))[..., :piece].reshape(lead + (parts * piece,))


def _rows8(rows, width):
    n = _round_up(len(rows), 8)
    rows = list(rows) + [jnp.zeros((1, width), F32)] * (n - len(rows))
    return jnp.concatenate(rows, axis=0)


def kernel(x, c, g_pre_mix, g_post_mix, g_pre_ffn, g_post_ffn, w_ada, b_ada, w_in, w_pool, pool_scale, conv_w, conv_b, w_bout, w_o, w_up, ffn_conv_w, ffn_conv_b, w_down, loss_target, m_g_pre_mix, m_g_post_mix, m_g_pre_ffn, m_g_post_ffn, m_w_ada, m_b_ada, m_w_in, m_w_pool, m_pool_scale, m_conv_w, m_conv_b, m_w_bout, m_w_o, m_w_up, m_ffn_conv_w, m_ffn_conv_b, m_w_down, v_g_pre_mix, v_g_post_mix, v_g_pre_ffn, v_g_post_ffn, v_w_ada, v_b_ada, v_w_in, v_w_pool, v_pool_scale, v_conv_w, v_conv_b, v_w_bout, v_w_o, v_w_up, v_ffn_conv_w, v_ffn_conv_b, v_w_down):
    t, d = x.shape[1], x.shape[2]
    ngroups = len(POOL_WINDOWS)
    gw = d // ngroups
    ada_n = w_ada.shape[2]
    in_n = w_in.shape[2]
    unit = w_down.shape[1]
    unit_p = _round_up(unit, LANES)
    fp = NDEV * unit_p

    xi, yi, ci = _position()
    me = _linear(xi, yi, ci)
    chip = 2 * xi + yi
    core = jnp.reshape(ci, (1,)).astype(jnp.int32)
    sel = jnp.stack([2 * chip + ci, chip]).astype(jnp.int32)

    x2 = x.reshape(t, d)
    target = loss_target.reshape(t, d)

    w_in_f, g_bout, g_o, w_pool_f = _run_job(_allgather_job(
        [w_in[0].astype(BF16), w_bout[0].astype(BF16), w_o[0].astype(BF16), w_pool[0].astype(BF16)],
        ["cols", "rows", "rows", "mid"], 0.5), "allgather_mixer_weights")
    w_bout_f = g_bout.reshape(d, d)
    w_o_f = g_o.reshape(d, d)

    cw_n = conv_w.shape[2]
    fcw_p = _pad_last(ffn_conv_w[0], 2, unit_p)
    pack = jnp.concatenate([c.reshape(1, d), conv_w[0].reshape(1, 3 * cw_n), fcw_p.reshape(1, 6 * unit_p)], axis=1)
    pack_n = _round_up(pack.shape[1], LANES)
    pack = jnp.pad(pack, ((0, 0), (0, pack_n - pack.shape[1])))
    pack = jnp.pad(pack, ((0, 7), (0, 0)))
    gathered = _small_allgather(pack, "allgather_cond", False)[0][:, 0, :]
    c_all = gathered[:, :d]
    conv_w_full = gathered[:, d:d + 3 * cw_n].reshape(NDEV, 3, cw_n).transpose(1, 0, 2).reshape(3, NDEV * cw_n)
    fcw_full = gathered[:, d + 3 * cw_n:d + 3 * cw_n + 6 * unit_p].reshape(NDEV, 3, 2 * unit_p)
    fcw_full = fcw_full.transpose(1, 0, 2).reshape(3, 2 * fp)
    fcb_full = _pad_last(ffn_conv_b, 2 * NDEV, unit_p)
    fcv = jnp.concatenate([fcw_full, fcb_full, jnp.zeros((4, 2 * fp), F32)], axis=0)

    c16 = jnp.pad(c_all, ((0, 8), (0, 0))).astype(BF16)
    b_piece = lax.dynamic_slice_in_dim(b_ada, me * ada_n, ada_n, axis=1)
    mod_piece = _mod_piece(c16, w_ada[0], b_piece)[:NDEV]
    mod_all = _small_allgather(mod_piece, "allgather_mod", False)[0]
    mod = lax.dynamic_index_in_dim(mod_all, me, axis=1, keepdims=False).reshape(1, NDEV * ada_n)
    sh1, sc1, gt1, sh2, sc2, gt2 = [mod[:, k * d:(k + 1) * d] for k in range(6)]

    ffn_weights = _allgather_job(
        [_pad_last(w_up[0], 2, unit_p).astype(BF16), jnp.pad(w_down[0], ((0, unit_p - unit), (0, 0))).astype(BF16)],
        ["cols", "rows"], 0.75)
    vec_pre_mix = _rows8([g_pre_mix, 1.0 + sc1, sh1], d)
    vec_mix = _rows8([gt1, g_post_mix, pool_scale, conv_b, conv_w_full[0:1], conv_w_full[1:2], conv_w_full[2:3]], d)
    h1b, p5b, pgb, qmb, cvb, yarb, ybb, ob, x1, w_up_f, g_down = _mixer_block_fwd(
        x2, vec_pre_mix, vec_mix, w_in_f, w_pool_f, w_bout_f, w_o_f, ffn_weights)
    w_down_f = g_down.reshape(fp, d)
    vec_pre_ffn = _rows8([g_pre_ffn, 1.0 + sc2, sh2], d)
    vec_ffn = _rows8([gt2, g_post_ffn], d)
    h2b, upb, upreb, ab, ffb, dy, loss_part = _ffn_block_fwd(x1, target, vec_pre_ffn, vec_ffn, fcv, w_up_f, w_down_f)
    loss = lax.psum(0.5 * loss_part[0, 0], ("x", "y", "c"))

    tk = min(2048, t)
    chip_sum = lambda gs, ss: [_chip_sum(g, s, core) for g, s in zip(gs, ss)]
    dffb, dupre, dx1, red_ffn, red_fconv, red_pre_ffn = _ffn_block_bwd(
        dy, ffb, x1, upb, upreb, vec_pre_ffn, vec_ffn, fcv, w_up_f, w_down_f)
    gr_up = _matmul_tn(h2b, dupre, d, 2 * unit_p, tk, True, "wgrad_up")
    gr_down = _matmul_tn(ab, dffb, 2 * unit_p, d, tk, False, "wgrad_down").reshape(NDEV, unit_p, d)
    sib_ffn = _run_job(_sibling_job([gr_up, gr_down]), "rs_sibling_ffn")
    dyarb, dqmb, dproj, grad_x, red_mix, red_pre_mix, fc_up, fc_down = _mixer_block_bwd(
        dx1, ob, yarb, ybb, cvb, p5b, x2, vec_pre_mix, vec_mix, w_in_f, w_pool_f, w_bout_f, w_o_f,
        _chips_job(chip_sum([gr_up, gr_down], sib_ffn)))
    gr_in = _matmul_tn(h1b, dproj, d, in_n, tk, True, "wgrad_in")
    sib_in = _run_job(_sibling_job([gr_in]), "rs_sibling_in")
    gr_qm, fc_in = _matmul_tn_groups(qmb, dqmb, 2, tk, "wgrad_bout_o", _chips_job(chip_sum([gr_in], sib_in)))
    gr_bout = gr_qm[0].reshape(NDEV, d // NDEV, d)
    gr_o = gr_qm[1].reshape(NDEV, d // NDEV, d)
    gr_pool = _matmul_tn_groups(pgb, dyarb, ngroups, min(4096, t), "wgrad_pool")
    gr_pool = gr_pool.reshape(ngroups, NDEV, gw // NDEV, gw).transpose(1, 0, 2, 3).reshape(NDEV, -1, gw)
    rest = [gr_bout, gr_o, gr_pool]
    sib_rest = _run_job(_sibling_job(rest), "rs_sibling_rest")
    fc_rest = _run_job(_chips_job(chip_sum(rest, sib_rest)), "rs_chips_rest")

    def big(grad, from_sibling, from_chips, w, m, v, name, col_parts=0):
        shape = w.shape
        w2, m2, v2 = [a.reshape((-1, shape[-1])) for a in (w, m, v)]
        if col_parts:
            w2, m2, v2 = [_pad_last(a, col_parts, grad.shape[2] // col_parts) for a in (w2, m2, v2)]
        outs = _reduce_adamw(grad, from_sibling, from_chips, sel, w2, m2, v2, name)
        if col_parts:
            outs = [_unpad_last(a, col_parts, shape[-1] // col_parts) for a in outs]
        return [a.reshape(shape) for a in outs]

    g_w_up, d_w_up, nm_w_up, nv_w_up = big(gr_up, sib_ffn[0], fc_up, w_up, m_w_up, v_w_up, "adamw_up", col_parts=2)
    g_w_down, d_w_down, nm_w_down, nv_w_down = big(gr_down, sib_ffn[1], fc_down, w_down, m_w_down, v_w_down, "adamw_down")
    g_w_in, d_w_in, nm_w_in, nv_w_in = big(gr_in, sib_in[0], fc_in, w_in, m_w_in, v_w_in, "adamw_in")
    g_w_bout, d_w_bout, nm_w_bout, nv_w_bout = big(gr_bout, sib_rest[0], fc_rest[0], w_bout, m_w_bout, v_w_bout, "adamw_bout")
    g_w_o, d_w_o, nm_w_o, nv_w_o = big(gr_o, sib_rest[1], fc_rest[1], w_o, m_w_o, v_w_o, "adamw_o")
    g_w_pool, d_w_pool, nm_w_pool, nv_w_pool = big(gr_pool, sib_rest[2], fc_rest[2], w_pool, m_w_pool, v_w_pool, "adamw_pool")

    dmod = [red_pre_mix[0:1], red_pre_mix[1:2], red_mix[1:2], red_pre_ffn[0:1], red_pre_ffn[1:2], red_ffn[1:2]]
    small = [red_pre_mix[2:3], red_mix[0:1], red_pre_ffn[2:3], red_ffn[0:1], red_mix[2:3], red_mix[3:4],
             red_mix[4:5], red_mix[5:6], red_mix[6:7]] + dmod
    flat = jnp.concatenate(small + [red_fconv[0:4].reshape(1, 8 * fp)], axis=1)
    flat_n = flat.shape[1]
    width = 8 * LANES
    rows = _round_up(-(-flat_n // width), 8)
    flat = jnp.pad(flat, ((0, 0), (0, rows * width - flat_n))).reshape(rows, width)
    gat, tot = _small_allgather(flat, "allreduce_small", True)
    tot = tot.reshape(1, rows * width)
    gat = gat.reshape(NDEV, rows * width)
    take = lambda k: tot[:, k * d:(k + 1) * d]
    g_g_pre_mix, g_g_post_mix, g_g_pre_ffn, g_g_post_ffn, g_pool_scale, g_conv_b = [take(k) for k in range(6)]
    g_conv_w_full = jnp.concatenate([take(6), take(7), take(8)], axis=0)
    g_conv_w = lax.dynamic_slice_in_dim(g_conv_w_full, me * cw_n, cw_n, axis=1)
    g_b_ada = tot[:, 9 * d:15 * d]
    dmod_all = gat[:, 9 * d:15 * d]
    fconv_tot = tot[:, 15 * d:15 * d + 8 * fp].reshape(4, 2 * fp)
    g_ffn_conv_b = _unpad_last(fconv_tot[3:4], 2 * NDEV, unit)
    g_fcw_mine = lax.dynamic_slice_in_dim(fconv_tot[0:3], me * 2 * unit_p, 2 * unit_p, axis=1)
    g_ffn_conv_w = _unpad_last(g_fcw_mine, 2, unit)
    dmod_piece = lax.dynamic_slice_in_dim(dmod_all, me * ada_n, ada_n, axis=1)
    g_w_ada = _wada_grad(c16, jnp.pad(dmod_piece, ((0, 8), (0, 0))).astype(BF16))

    names_small = [(g_pre_mix, g_g_pre_mix, m_g_pre_mix, v_g_pre_mix), (g_post_mix, g_g_post_mix, m_g_post_mix, v_g_post_mix),
                   (g_pre_ffn, g_g_pre_ffn, m_g_pre_ffn, v_g_pre_ffn), (g_post_ffn, g_g_post_ffn, m_g_post_ffn, v_g_post_ffn),
                   (b_ada, g_b_ada, m_b_ada, v_b_ada), (pool_scale, g_pool_scale, m_pool_scale, v_pool_scale),
                   (conv_w, g_conv_w, m_conv_w, v_conv_w), (conv_b, g_conv_b, m_conv_b, v_conv_b),
                   (ffn_conv_w, g_ffn_conv_w, m_ffn_conv_w, v_ffn_conv_w), (ffn_conv_b, g_ffn_conv_b, m_ffn_conv_b, v_ffn_conv_b)]
    sizes = [w.size for w, _, _, _ in names_small]
    total = sum(sizes)
    prow = _round_up(-(-total // width), 8)

    def pack_small(k):
        a = jnp.concatenate([q[k].reshape(1, -1) for q in names_small], axis=1)
        return jnp.pad(a, ((0, 0), (0, prow * width - total)), constant_values=1.0).reshape(prow, width)

    ds, ms, vs = _adamw(pack_small(0), pack_small(1), pack_small(2), pack_small(3), "adamw_small")

    def unpack_small(a):
        a = a.reshape(-1)
        out, off = [], 0
        for (w, _, _, _), n in zip(names_small, sizes):
            out.append(a[off:off + n].reshape(w.shape))
            off += n
        return out

    (d_g_pre_mix, d_g_post_mix, d_g_pre_ffn, d_g_post_ffn, d_b_ada, d_pool_scale, d_conv_w, d_conv_b,
     d_ffn_conv_w, d_ffn_conv_b) = unpack_small(ds)
    (nm_g_pre_mix, nm_g_post_mix, nm_g_pre_ffn, nm_g_post_ffn, nm_b_ada, nm_pool_scale, nm_conv_w, nm_conv_b,
     nm_ffn_conv_w, nm_ffn_conv_b) = unpack_small(ms)
    (nv_g_pre_mix, nv_g_post_mix, nv_g_pre_ffn, nv_g_post_ffn, nv_b_ada, nv_pool_scale, nv_conv_w, nv_conv_b,
     nv_ffn_conv_w, nv_ffn_conv_b) = unpack_small(vs)
    d_w_ada, nm_w_ada, nv_w_ada = [a.reshape(w_ada.shape) for a in
                                   _adamw(w_ada[0], g_w_ada, m_w_ada[0], v_w_ada[0], "adamw_ada")]

    grads = [g_g_pre_mix, g_g_post_mix, g_g_pre_ffn, g_g_post_ffn, g_w_ada.reshape(w_ada.shape), g_b_ada, g_w_in,
             g_w_pool, g_pool_scale, g_conv_w.reshape(conv_w.shape), g_conv_b, g_w_bout, g_w_o, g_w_up,
             g_ffn_conv_w.reshape(ffn_conv_w.shape), g_ffn_conv_b, g_w_down]
    deltas = [d_g_pre_mix, d_g_post_mix, d_g_pre_ffn, d_g_post_ffn, d_w_ada, d_b_ada, d_w_in, d_w_pool, d_pool_scale,
              d_conv_w, d_conv_b, d_w_bout, d_w_o, d_w_up, d_ffn_conv_w, d_ffn_conv_b, d_w_down]
    new_m = [nm_g_pre_mix, nm_g_post_mix, nm_g_pre_ffn, nm_g_post_ffn, nm_w_ada, nm_b_ada, nm_w_in, nm_w_pool,
             nm_pool_scale, nm_conv_w, nm_conv_b, nm_w_bout, nm_w_o, nm_w_up, nm_ffn_conv_w, nm_ffn_conv_b, nm_w_down]
    new_v = [nv_g_pre_mix, nv_g_post_mix, nv_g_pre_ffn, nv_g_post_ffn, nv_w_ada, nv_b_ada, nv_w_in, nv_w_pool,
             nv_pool_scale, nv_conv_w, nv_conv_b, nv_w_bout, nv_w_o, nv_w_up, nv_ffn_conv_w, nv_ffn_conv_b, nv_w_down]
    return (loss, grad_x.reshape(x.shape), *grads, *deltas, *new_m, *new_v)
```

```python
import math

import jax
import jax.numpy as jnp
from jax import lax
from jax.experimental import pallas as pl
from jax.experimental.pallas import tpu as pltpu

F32 = jnp.float32
BF16 = jnp.bfloat16
MESH = pl.DeviceIdType.MESH

NDEV = 8
NCHIP = 4
EPS = 1e-6
POOL_WINDOWS = (2, 4, 8, 16)
LANES = 128
ADAM_LR = 0.001
ADAM_B1 = 0.9
ADAM_B2 = 0.999
ADAM_EPS = 1e-08
ADAM_WD = 0.01
ADAM_STEP = 10
GELU_C0 = math.sqrt(2.0 / math.pi)
GELU_C1 = 0.044715
VMEM_LIMIT = 56 * 2**20


def _vmem():
    return pl.BlockSpec(memory_space=pltpu.VMEM)


def _any():
    return pl.BlockSpec(memory_space=pl.ANY)


def _params(*sem):
    return pltpu.CompilerParams(dimension_semantics=sem, vmem_limit_bytes=VMEM_LIMIT)


def _sds(shape, dtype):
    return jax.ShapeDtypeStruct(tuple(shape), dtype)


def _position():
    return lax.axis_index("x"), lax.axis_index("y"), lax.axis_index("c")


def _linear(x, y, c):
    return 4 * x + 2 * y + c


def _dot(a, b):
    return jnp.dot(a, b, preferred_element_type=F32)


def _dot_nt(a, b):
    return lax.dot_general(a, b, (((1,), (1,)), ((), ())), preferred_element_type=F32)


def _dot_tn(a, b):
    return lax.dot_general(a, b, (((0,), (0,)), ((), ())), preferred_element_type=F32)


def _colsum(v):
    return jnp.sum(v, axis=0, keepdims=True)


def _rowmean(v):
    return jnp.mean(v, axis=-1, keepdims=True)


GROUP = 256


def _interleave(v):
    g, n = v.shape
    return jnp.swapaxes(v.reshape(8, g // 8, n), 0, 1).reshape(g, n)


def _deinterleave(v):
    g, n = v.shape
    return jnp.swapaxes(v.reshape(g // 8, 8, n), 0, 1).reshape(g, n)


def _halo_top(cur_last, prev_last):
    rows, n = cur_last.shape
    c3 = cur_last.reshape(rows // 8, 8, n)
    p3 = prev_last.reshape(rows // 8, 8, n)
    sub = lax.broadcasted_iota(jnp.int32, c3.shape, 1)
    return jnp.where(sub == 0, pltpu.roll(p3, 1, 1), pltpu.roll(c3, 1, 1)).reshape(rows, n)


def _halo_bottom(cur_first, next_first):
    rows, n = cur_first.shape
    c3 = cur_first.reshape(rows // 8, 8, n)
    n3 = next_first.reshape(rows // 8, 8, n)
    sub = lax.broadcasted_iota(jnp.int32, c3.shape, 1)
    return jnp.where(sub == 7, pltpu.roll(n3, 7, 1), pltpu.roll(c3, 7, 1)).reshape(rows, n)


def _shift_down(v, halo, k):
    rows = v.shape[0]
    return jnp.concatenate([halo[halo.shape[0] - 8 * k:, :], v[:rows - 8 * k, :]], axis=0)


def _shift_up(v, halo, k):
    return jnp.concatenate([v[8 * k:, :], halo[:8 * k, :]], axis=0)


def _inv_count(first_token, window):
    row = lax.broadcasted_iota(jnp.int32, (GROUP, 1), 0)
    t = first_token + (row % 8) * (GROUP // 8) + row // 8
    return 1.0 / jnp.minimum(t + 1, window).astype(F32)


class _Job:
    def __init__(self, inputs, out_shape, scratch, phases):
        self.inputs, self.out_shape, self.scratch, self.phases = list(inputs), list(out_shape), list(scratch), phases


def _call(body, name, grid, in_specs, out_specs, out_shape, scratch_shapes, params, operands, job=None):
    if job is None:
        return pl.pallas_call(body, name=name, grid=grid, in_specs=in_specs, out_specs=out_specs, out_shape=out_shape,
                              scratch_shapes=scratch_shapes, compiler_params=params)(*operands)
    n_in, n_out, n_scr = len(in_specs), len(out_specs), len(scratch_shapes)
    j_in, j_out = len(job.inputs), len(job.out_shape)
    steps = math.prod(grid)

    def hosted(*refs):
        own_in, refs = refs[:n_in], refs[n_in:]
        jin, refs = refs[:j_in], refs[j_in:]
        own_out, refs = refs[:n_out], refs[n_out:]
        jout, refs = refs[:j_out], refs[j_out:]
        own_scr, jscr = refs[:n_scr], refs[n_scr:]
        step = pl.program_id(0)
        for axis in range(1, len(grid)):
            step = step * grid[axis] + pl.program_id(axis)
        for frac, fn in job.phases[:-1]:
            pl.when(step == int(frac * (steps - 1)))(lambda fn=fn: fn(jin, jout, jscr))
        body(*own_in, *own_out, *own_scr)
        pl.when(step == steps - 1)(lambda: job.phases[-1][1](jin, jout, jscr))

    return pl.pallas_call(
        hosted, name=name, grid=grid, in_specs=list(in_specs) + [_any()] * j_in,
        out_specs=list(out_specs) + [_any()] * j_out, out_shape=list(out_shape) + job.out_shape,
        scratch_shapes=list(scratch_shapes) + job.scratch, compiler_params=params)(*operands, *job.inputs)


def _run_job(job, name):
    n_in, n_out = len(job.inputs), len(job.out_shape)

    def body(*refs):
        for _, fn in job.phases:
            fn(refs[:n_in], refs[n_in:n_in + n_out], refs[n_in + n_out:])

    return pl.pallas_call(body, name=name, out_shape=job.out_shape, in_specs=[_any()] * n_in,
                          out_specs=[_any()] * n_out, scratch_shapes=job.scratch)(*job.inputs)


def _peers(x, y, c):
    out = []
    for k in range(1, NDEV):
        out.append(((1 - x) if k & 4 else x, (1 - y) if k & 2 else y, (1 - c) if k & 1 else c))
    return out


def _small_allgather(v, name, with_sum):
    r, n = v.shape

    def body(v_ref, gat_ref, *rest):
        if with_sum:
            sum_ref, send_sems, recv_sems, local_sem = rest
        else:
            send_sems, recv_sems, local_sem = rest
        x, y, c = _position()
        me = _linear(x, y, c)
        mine = pltpu.make_async_copy(v_ref, gat_ref.at[me], local_sem)
        mine.start()
        peers = _peers(x, y, c)
        sends = []
        for k, peer in enumerate(peers):
            cp = pltpu.make_async_remote_copy(src_ref=v_ref, dst_ref=gat_ref.at[me], send_sem=send_sems.at[k],
                                              recv_sem=recv_sems.at[k], device_id=peer, device_id_type=MESH)
            cp.start()
            sends.append(cp)
        for k, peer in enumerate(peers):
            pltpu.make_async_remote_copy(src_ref=v_ref, dst_ref=gat_ref.at[_linear(*peer)], send_sem=send_sems.at[k],
                                         recv_sem=recv_sems.at[k], device_id=peer, device_id_type=MESH).wait_recv()
        for cp in sends:
            cp.wait_send()
        mine.wait()
        if with_sum:
            acc = gat_ref[0]
            for j in range(1, NDEV):
                acc = acc + gat_ref[j]
            sum_ref[...] = acc

    out_shape = [_sds((NDEV, r, n), F32)] + ([_sds((r, n), F32)] if with_sum else [])
    return pl.pallas_call(
        body, name=name, out_shape=out_shape, in_specs=[_vmem()], out_specs=[_vmem()] * len(out_shape),
        scratch_shapes=[pltpu.SemaphoreType.DMA((NDEV - 1,)), pltpu.SemaphoreType.DMA((NDEV - 1,)),
                        pltpu.SemaphoreType.DMA(())],
    )(v)


def _gathered(shard, layout):
    if layout == "rows":
        return (NDEV,) + shard.shape, lambda ref, j: ref.at[j]
    if layout == "cols":
        r, c = shard.shape
        return (r, NDEV * c), lambda ref, j: ref.at[:, pl.ds(pl.multiple_of(j * c, LANES), c)]
    g, r, c = shard.shape
    return (g, NDEV * r, c), lambda ref, j: ref.at[:, pl.ds(pl.multiple_of(j * r, 16), r), :]


def _allgather_job(shards, layouts, forward_at):
    n = len(shards)
    specs = [_gathered(s, l) for s, l in zip(shards, layouts)]

    def copies(src, dst, sems):
        send_sems, recv_sems, _ = sems
        x, y, c = _position()
        me, sibling = (x, y, c), (x, y, 1 - c)
        chips = [(1 - x, y), (x, 1 - y), (1 - x, 1 - y)]

        def copy(a, k, block, to, from_src=False):
            blk = specs[a][1](dst[a], _linear(*block))
            return pltpu.make_async_remote_copy(src_ref=src[a] if from_src else blk, dst_ref=blk,
                                                send_sem=send_sems.at[a, k], recv_sem=recv_sems.at[a, k],
                                                device_id=to, device_id_type=MESH)
        return copy, me, sibling, chips, c

    def local(src, dst, sems):
        x, y, c = _position()
        return [pltpu.make_async_copy(src[a], specs[a][1](dst[a], _linear(x, y, c)), sems[2].at[a]) for a in range(n)]

    def first(src, dst, sems):
        copy, me, sibling, chips, c = copies(src, dst, sems)
        out = [copy(a, 1 + j, me, (*chip, c), from_src=True) for j, chip in enumerate(chips) for a in range(n)]
        return out + [copy(a, 0, me, sibling, from_src=True) for a in range(n)]

    def passed(src, dst, sems):
        copy, me, sibling, chips, c = copies(src, dst, sems)
        return [copy(a, 4 + j, (*chip, c), sibling) for j, chip in enumerate(chips) for a in range(n)]

    def start(src, dst, sems):
        for cp in local(src, dst, sems) + first(src, dst, sems):
            cp.start()

    def forward(src, dst, sems):
        copy, me, sibling, chips, c = copies(src, dst, sems)
        for j, chip in enumerate(chips):
            for a in range(n):
                copy(a, 1 + j, (*chip, c), me).wait_recv()
        for cp in passed(src, dst, sems):
            cp.start()

    def finish(src, dst, sems):
        copy, me, sibling, chips, c = copies(src, dst, sems)
        for a in range(n):
            copy(a, 0, sibling, me).wait_recv()
        for j, chip in enumerate(chips):
            for a in range(n):
                copy(a, 4 + j, (*chip, 1 - c), me).wait_recv()
        for cp in first(src, dst, sems) + passed(src, dst, sems):
            cp.wait_send()
        for cp in local(src, dst, sems):
            cp.wait()

    return _Job(shards, [_sds(spec[0], s.dtype) for spec, s in zip(specs, shards)],
                [pltpu.SemaphoreType.DMA((n, 7)), pltpu.SemaphoreType.DMA((n, 7)), pltpu.SemaphoreType.DMA((n,))],
                [(0.0, start), (forward_at, forward), (1.0, finish)])


def _sibling_job(grads):
    n = len(grads)

    def copies(src, dst, sems):
        x, y, c = _position()
        return [pltpu.make_async_remote_copy(src_ref=src[a].at[2 * q + 1 - c], dst_ref=dst[a].at[q],
                                             send_sem=sems[0].at[a, q], recv_sem=sems[1].at[a, q],
                                             device_id=(x, y, 1 - c), device_id_type=MESH)
                for a in range(n) for q in range(NCHIP)]

    return _exchange_job(grads, NCHIP, copies)


def _chips_job(chip_sums):
    n = len(chip_sums)

    def copies(src, dst, sems):
        x, y, c = _position()
        chips = [(1 - x, y), (x, 1 - y), (1 - x, 1 - y)]
        return [pltpu.make_async_remote_copy(src_ref=src[a].at[2 * chip[0] + chip[1]], dst_ref=dst[a].at[j],
                                             send_sem=sems[0].at[a, j], recv_sem=sems[1].at[a, j],
                                             device_id=(*chip, c), device_id_type=MESH)
                for j, chip in enumerate(chips) for a in range(n)]

    return _exchange_job(chip_sums, 3, copies)


def _exchange_job(arrays, slots, copies):
    n = len(arrays)

    def start(src, dst, sems):
        for cp in copies(src, dst, sems):
            cp.start()

    def finish(src, dst, sems):
        cps = copies(src, dst, sems)
        for cp in cps:
            cp.wait_recv()
        for cp in cps:
            cp.wait_send()

    return _Job(arrays, [_sds((slots,) + a.shape[1:], a.dtype) for a in arrays],
                [pltpu.SemaphoreType.DMA((n, slots)), pltpu.SemaphoreType.DMA((n, slots))],
                [(0.0, start), (1.0, finish)])


def _row_block(r):
    for rb in (512, 256, 128, 64, 32, 16):
        if r % rb == 0:
            return rb
    return r


def _chip_sum(grad, from_sibling, core):
    _, r, c = grad.shape
    rb = _row_block(r)

    def body(core_ref, g_ref, s_ref, o_ref):
        del core_ref
        o_ref[...] = (g_ref[...].astype(F32) + s_ref[...].astype(F32)).astype(o_ref.dtype)

    grid_spec = pltpu.PrefetchScalarGridSpec(
        num_scalar_prefetch=1, grid=(NCHIP, r // rb),
        in_specs=[pl.BlockSpec((None, rb, c), lambda q, i, core: (2 * q + core[0], i, 0)),
                  pl.BlockSpec((None, rb, c), lambda q, i, core: (q, i, 0))],
        out_specs=pl.BlockSpec((None, rb, c), lambda q, i, core: (q, i, 0)))
    return pl.pallas_call(body, name="rs_chip_sum", grid_spec=grid_spec, out_shape=_sds((NCHIP, r, c), BF16),
                          compiler_params=_params("parallel", "parallel"))(core, grad, from_sibling)


def _adamw_math(w, g, m, v):
    m2 = ADAM_B1 * m + (1.0 - ADAM_B1) * g
    v2 = ADAM_B2 * v + (1.0 - ADAM_B2) * jnp.square(g)
    m_hat = m2 / (1.0 - ADAM_B1 ** ADAM_STEP)
    v_hat = v2 / (1.0 - ADAM_B2 ** ADAM_STEP)
    delta = -ADAM_LR * (m_hat / (jnp.sqrt(v_hat) + ADAM_EPS) + ADAM_WD * w)
    return delta, m2, v2


def _adamw(w, g, m, v, name):
    r, c = w.shape
    rb = _row_block(r)

    def body(w_ref, g_ref, m_ref, v_ref, d_ref, m2_ref, v2_ref):
        d, m2, v2 = _adamw_math(w_ref[...], g_ref[...], m_ref[...], v_ref[...])
        d_ref[...] = d
        m2_ref[...] = m2
        v2_ref[...] = v2

    blk = pl.BlockSpec((rb, c), lambda i: (i, 0))
    return pl.pallas_call(body, name=name, grid=(r // rb,), in_specs=[blk] * 4, out_specs=[blk] * 3,
                          out_shape=[_sds((r, c), F32)] * 3, compiler_params=_params("parallel"))(w, g, m, v)


def _reduce_adamw(grad, from_sibling, from_chips, sel, w, m, v, name, col_parts=1):
    r, c = w.shape
    cp = grad.shape[2]
    rb = _row_block(r)

    def body(sel_ref, g_ref, s_ref, c0_ref, c1_ref, c2_ref, w_ref, m_ref, v_ref, go_ref, d_ref, m2_ref, v2_ref):
        del sel_ref
        g = g_ref[...].astype(F32) + s_ref[...].astype(F32)
        g = g + c0_ref[...].astype(F32)
        g = g + c1_ref[...].astype(F32)
        g = g + c2_ref[...].astype(F32)
        if col_parts > 1:
            wide, real = cp // col_parts, c // col_parts
            g = jnp.concatenate([g[:, p * wide:p * wide + real] for p in range(col_parts)], axis=1)
        d, m2, v2 = _adamw_math(w_ref[...], g, m_ref[...], v_ref[...])
        go_ref[...] = g
        d_ref[...] = d
        m2_ref[...] = m2
        v2_ref[...] = v2

    blk = pl.BlockSpec((rb, c), lambda i, sel: (i, 0))
    grid_spec = pltpu.PrefetchScalarGridSpec(
        num_scalar_prefetch=1, grid=(r // rb,),
        in_specs=[pl.BlockSpec((None, rb, cp), lambda i, sel: (sel[0], i, 0)),
                  pl.BlockSpec((None, rb, cp), lambda i, sel: (sel[1], i, 0)),
                  pl.BlockSpec((None, rb, cp), lambda i, sel: (0, i, 0)),
                  pl.BlockSpec((None, rb, cp), lambda i, sel: (1, i, 0)),
                  pl.BlockSpec((None, rb, cp), lambda i, sel: (2, i, 0)),
                  blk, blk, blk],
        out_specs=[blk] * 4)
    return pl.pallas_call(body, name=name, grid_spec=grid_spec, out_shape=[_sds((r, c), F32)] * 4,
                          compiler_params=_params("parallel"))(sel, grad, from_sibling, from_chips, from_chips,
                                                               from_chips, w, m, v)


def _mod_piece(c_all, w_ada, b_piece):
    rows, _ = c_all.shape
    n = w_ada.shape[1]

    def body(c_ref, w_ref, b_ref, o_ref):
        o_ref[...] = _dot(c_ref[...], w_ref[...].astype(BF16)) + b_ref[...]

    return pl.pallas_call(body, name="ada_mod", out_shape=_sds((rows, n), F32),
                          in_specs=[_vmem()] * 3, out_specs=_vmem())(c_all, w_ada, b_piece)


def _wada_grad(c_all, dmod_piece):
    d = c_all.shape[1]
    n = dmod_piece.shape[1]

    def body(c_ref, dm_ref, o_ref):
        o_ref[...] = _dot_tn(c_ref[...], dm_ref[...])

    return pl.pallas_call(body, name="ada_wgrad", out_shape=_sds((d, n), F32),
                          in_specs=[_vmem()] * 2, out_specs=_vmem())(c_all, dmod_piece)


def _conv_taps(ref, col):
    return ref[0:1, col], ref[1:2, col], ref[2:3, col]


def _gelu_parts(u):
    th = jnp.tanh(GELU_C0 * (u + GELU_C1 * (u * u * u)))
    cdf = 0.5 * (1.0 + th)
    return cdf, th


def _conv3_bwd(dv, carry, col, taps, x):
    halo = _halo_bottom(dv[:16, :], carry[:, col])
    carry[:, col] = dv[:16, :]
    d1 = _shift_up(dv, halo, 1)
    d2 = _shift_up(dv, halo, 2)
    w0, w1, w2 = taps
    dx = w2 * dv
    dx = dx + w1 * d1
    dx = dx + w0 * d2
    return dx, (_colsum(d2 * x), _colsum(d1 * x), _colsum(dv * x))


def _prenorm(x, vp_ref):
    r = lax.rsqrt(_rowmean(x * x) + EPS)
    nh = x * r
    return (nh * vp_ref[0:1, :]) * vp_ref[1:2, :] + vp_ref[2:3, :], r, nh


def _prenorm_bwd(dh, r, nh, vp_ref, red_ref):
    g, sc1 = vp_ref[0:1, :], vp_ref[1:2, :]
    red_ref[0:1, :] += _colsum(dh)
    red_ref[1:2, :] += _colsum(dh * (nh * g))
    red_ref[2:3, :] += _colsum(dh * nh * sc1)
    dnh = dh * g * sc1
    return r * (dnh - nh * _rowmean(dnh * nh))


def _postnorm_bwd(dres, z, gate, gpost, red_ref):
    r = lax.rsqrt(_rowmean(z * z) + EPS)
    nh = z * r
    dn = dres * gate
    red_ref[0:1, :] += _colsum(dn * nh)
    red_ref[1:2, :] += _colsum(dres * (nh * gpost))
    dnh = dn * gpost
    return r * (dnh - nh * _rowmean(dnh * nh))


def _mixer_block_fwd(x, vec_pre, vec, w_in, w_pool, w_bout, w_o, job):
    t, d = x.shape
    tm = GROUP
    gw = d // len(POOL_WINDOWS)
    pool_rows = 8 * (POOL_WINDOWS[-1] - 1)

    def body(x_ref, vp_ref, vec_ref, win_ref, wp_ref, wb_ref, wo_ref,
             hb_ref, p5_ref, pg_ref, qm_ref, cv_ref, yar_ref, yb_ref, o_ref, x1_ref, mbuf, ucarry, pcarry):
        i = pl.program_id(0)

        @pl.when(i == 0)
        def _():
            ucarry[...] = jnp.zeros_like(ucarry)
            pcarry[...] = jnp.zeros_like(pcarry)

        xp = _interleave(x_ref[...])
        hb = _prenorm(xp, vp_ref)[0].astype(BF16)
        hb_ref[...] = hb
        proj = lambda k: _dot(hb, win_ref[:, k * d:(k + 1) * d])

        za = proj(4)
        p5_ref[:, 3 * d:4 * d] = za.astype(BF16)
        sa = jax.nn.sigmoid(za)
        u_pool = proj(0)
        for g, window in enumerate(POOL_WINDOWS):
            cols = slice(g * gw, (g + 1) * gw)
            rows = 8 * (window - 1)
            u = u_pool[:, cols]
            halo = _halo_top(u[tm - rows:, :], ucarry[pool_rows - rows:, cols])
            s, shift = jnp.concatenate([halo, u], axis=0), 1
            while shift < window:
                s = s[8 * shift:, :] + s[:s.shape[0] - 8 * shift, :]
                shift *= 2
            pgb = (s * _inv_count(i * tm, window) - u).astype(BF16)
            pg_ref[:, cols] = pgb
            yar = _dot(pgb, wp_ref[g])
            yar_ref[:, cols] = yar.astype(BF16)
            mbuf[:, cols] = sa[:, cols] * (yar * vec_ref[2:3, cols])
        ucarry[...] = u_pool[tm - pool_rows:, :]

        ux = proj(1)
        uc = proj(3)
        p5_ref[:, 0:d] = ux.astype(BF16)
        p5_ref[:, 2 * d:3 * d] = uc.astype(BF16)
        p = uc * ux
        halo = _halo_top(p[tm - 16:, :], pcarry[...])
        pcarry[...] = p[tm - 16:, :]
        cv = vec_ref[3:4, :] + vec_ref[4:5, :] * _shift_down(p, halo, 2)
        cv = cv + vec_ref[5:6, :] * _shift_down(p, halo, 1)
        cv = cv + vec_ref[6:7, :] * p
        cv_ref[...] = cv.astype(BF16)
        ub = proj(2)
        p5_ref[:, d:2 * d] = ub.astype(BF16)
        qb = (ub * cv).astype(BF16)
        qm_ref[:, 0:d] = qb
        yb = _dot(qb, wb_ref[...])
        yb_ref[...] = yb.astype(BF16)

        zb = proj(5)
        p5_ref[:, 4 * d:5 * d] = zb.astype(BF16)
        mb = (mbuf[...] + jax.nn.sigmoid(zb) * yb).astype(BF16)
        qm_ref[:, d:2 * d] = mb
        o = _dot(mb, wo_ref[...])
        o_ref[...] = o.astype(BF16)
        r2 = lax.rsqrt(_rowmean(o * o) + EPS)
        x1_ref[...] = xp + vec_ref[0:1, :] * ((o * r2) * vec_ref[1:2, :])

    row = lambda n: pl.BlockSpec((tm, n), lambda i: (i, 0))
    widths = [d, 5 * d, d, 2 * d, d, d, d, d]
    return _call(
        body, "mixer_block_fwd", (t // tm,), [row(d)] + [_vmem()] * 6, [row(n) for n in widths] + [row(d)],
        [_sds((t, n), BF16) for n in widths] + [_sds((t, d), F32)],
        [pltpu.VMEM((tm, d), F32), pltpu.VMEM((pool_rows, d), F32), pltpu.VMEM((16, d), F32)],
        _params("arbitrary"), (x, vec_pre, vec, w_in, w_pool, w_bout, w_o), job)


def _ffn_block_fwd(x1, target, vec_pre, vec, fcv, w_up, w_down):
    t, d = x1.shape
    tm = GROUP
    fp = w_down.shape[0]
    cw = fp // 4

    def body(x1_ref, tg_ref, vp_ref, vec_ref, fcv_ref, wu_ref, wd_ref,
             hb_ref, upb_ref, upreb_ref, a_ref, ffb_ref, dy_ref, loss_ref, carry):
        i = pl.program_id(0)

        @pl.when(i == 0)
        def _():
            carry[...] = jnp.zeros_like(carry)
            loss_ref[...] = jnp.zeros_like(loss_ref)

        x1 = x1_ref[...]
        hb = _prenorm(x1, vp_ref)[0].astype(BF16)
        hb_ref[...] = hb

        cols = [(slice(j * cw, (j + 1) * cw), slice(fp + j * cw, fp + (j + 1) * cw)) for j in range(4)]
        up = lambda j: tuple(_dot(hb, wu_ref[:, col]) for col in cols[j])
        ahead = up(0)

        def conv(v, col):
            halo = _halo_top(v[tm - 16:, :], carry[:, col])
            carry[:, col] = v[tm - 16:, :]
            w0, w1, w2 = _conv_taps(fcv_ref, col)
            y = fcv_ref[3:4, col] + w0 * _shift_down(v, halo, 2)
            y = y + w1 * _shift_down(v, halo, 1)
            y = y + w2 * v
            upb_ref[:, col] = y.astype(BF16)
            upreb_ref[:, col] = v.astype(BF16)
            return y

        ff = None
        for j in range(4):
            gc, vc = cols[j]
            cur, ahead = ahead, (up(j + 1) if j < 3 else None)
            gate = conv(cur[0], gc)
            val = conv(cur[1], vc)
            ab = ((gate * _gelu_parts(gate)[0]) * val).astype(BF16)
            a_ref[:, gc] = ab
            part = _dot(ab, wd_ref[gc, :])
            ff = part if ff is None else ff + part
        ffb_ref[...] = ff.astype(BF16)
        r4 = lax.rsqrt(_rowmean(ff * ff) + EPS)
        y = x1 + vec_ref[0:1, :] * ((ff * r4) * vec_ref[1:2, :])
        e = y - _interleave(tg_ref[...])
        dy_ref[...] = e * (1.0 / d)
        loss_ref[...] += jnp.sum(_rowmean(e * e))

    row = lambda n: pl.BlockSpec((tm, n), lambda i: (i, 0))
    return pl.pallas_call(
        body, name="ffn_block_fwd", grid=(t // tm,),
        in_specs=[row(d), row(d)] + [_vmem()] * 5,
        out_specs=[row(d), row(2 * fp), row(2 * fp), row(fp), row(d), row(d), pl.BlockSpec((8, LANES), lambda i: (0, 0))],
        out_shape=[_sds((t, d), BF16), _sds((t, 2 * fp), BF16), _sds((t, 2 * fp), BF16), _sds((t, fp), BF16),
                   _sds((t, d), BF16), _sds((t, d), F32), _sds((8, LANES), F32)],
        scratch_shapes=[pltpu.VMEM((16, 2 * fp), F32)],
        compiler_params=_params("arbitrary"),
    )(x1, target, vec_pre, vec, fcv, w_up, w_down)


def _ffn_block_bwd(dy, ffb, x1, upb, upreb, vec_pre, vec, fcv, w_up, w_down):
    t, d = dy.shape
    tm = GROUP
    fp = w_down.shape[0]
    cw = fp // 4
    nt = t // tm

    def body(dy_ref, ff_ref, x1_ref, upb_ref, upreb_ref, vp_ref, vec_ref, fcv_ref, wu_ref, wd_ref,
             dff_ref, dup_ref, dx1_ref, red_ref, cred_ref, pred_ref, carry):
        @pl.when(pl.program_id(0) == 0)
        def _():
            carry[...] = jnp.zeros_like(carry)
            red_ref[...] = jnp.zeros_like(red_ref)
            cred_ref[...] = jnp.zeros_like(cred_ref)
            pred_ref[...] = jnp.zeros_like(pred_ref)

        dy_v = dy_ref[...]
        dffb = _postnorm_bwd(dy_v, ff_ref[...].astype(F32), vec_ref[0:1, :], vec_ref[1:2, :], red_ref).astype(BF16)
        dff_ref[...] = dffb

        def conv_bwd(dv, col):
            dx, (t0, t1, t2) = _conv3_bwd(dv, carry, col, _conv_taps(fcv_ref, col), upreb_ref[:, col].astype(F32))
            cred_ref[0:1, col] += t0
            cred_ref[1:2, col] += t1
            cred_ref[2:3, col] += t2
            cred_ref[3:4, col] += _colsum(dv)
            dxb = dx.astype(BF16)
            dup_ref[:, col] = dxb
            return _dot_nt(dxb, wu_ref[:, col])

        dh = None
        for j in range(4):
            gc = slice(j * cw, (j + 1) * cw)
            vc = slice(fp + j * cw, fp + (j + 1) * cw)
            da = _dot_nt(dffb, wd_ref[gc, :])
            gate = upb_ref[:, gc].astype(F32)
            val = upb_ref[:, vc].astype(F32)
            cdf, th = _gelu_parts(gate)
            dcdf = 0.5 * (1.0 - th * th) * (GELU_C0 * (1.0 + (3.0 * GELU_C1) * (gate * gate)))
            part = conv_bwd(da * val * (cdf + gate * dcdf), gc) + conv_bwd(da * (gate * cdf), vc)
            dh = part if dh is None else dh + part

        _, r, nh = _prenorm(x1_ref[...], vp_ref)
        dx1_ref[...] = dy_v + _prenorm_bwd(dh, r, nh, vp_ref, pred_ref)

    rev = lambda n: pl.BlockSpec((tm, n), lambda i: (nt - 1 - i, 0))
    fixed = lambda n: pl.BlockSpec((8, n), lambda i: (0, 0))
    return pl.pallas_call(
        body, name="ffn_block_bwd", grid=(nt,),
        in_specs=[rev(d), rev(d), rev(d), rev(2 * fp), rev(2 * fp)] + [_vmem()] * 5,
        out_specs=[rev(d), rev(2 * fp), rev(d), fixed(d), fixed(2 * fp), fixed(d)],
        out_shape=[_sds((t, d), BF16), _sds((t, 2 * fp), BF16), _sds((t, d), F32), _sds((8, d), F32),
                   _sds((8, 2 * fp), F32), _sds((8, d), F32)],
        scratch_shapes=[pltpu.VMEM((16, 2 * fp), F32)],
        compiler_params=_params("arbitrary"),
    )(dy, ffb, x1, upb, upreb, vec_pre, vec, fcv, w_up, w_down)


def _mixer_block_bwd(dx1, ob, yarb, ybb, cvb, p5b, x, vec_pre, vec, w_in, w_pool, w_bout, w_o, job):
    t, d = dx1.shape
    tm = GROUP
    gw = d // len(POOL_WINDOWS)
    nt = t // tm
    pool_rows = 8 * (POOL_WINDOWS[-1] - 1)

    def body(dx1_ref, o_ref, yar_ref, yb_ref, cv_ref, p5_ref, x_ref, vp_ref, vec_ref, win_ref, wp_ref, wb_ref, wo_ref,
             dyar_ref, dqm_ref, dp_ref, gx_ref, red_ref, pred_ref, dpgcarry, dcvcarry):
        i = pl.program_id(0)
        tix = nt - 1 - i

        @pl.when(i == 0)
        def _():
            red_ref[...] = jnp.zeros_like(red_ref)
            pred_ref[...] = jnp.zeros_like(pred_ref)
            dpgcarry[...] = jnp.zeros_like(dpgcarry)
            dcvcarry[...] = jnp.zeros_like(dcvcarry)

        pscale = vec_ref[2:3, :]
        dx1_v = dx1_ref[...]
        dob = _postnorm_bwd(dx1_v, o_ref[...].astype(F32), vec_ref[0:1, :], vec_ref[1:2, :], red_ref).astype(BF16)
        dqm_ref[:, d:2 * d] = dob
        dm = _dot_nt(dob, wo_ref[...])

        def dproj(cols, value):
            vb = value.astype(BF16)
            dp_ref[:, cols] = vb
            return _dot_nt(vb, win_ref[:, cols])

        sa = jax.nn.sigmoid(p5_ref[:, 3 * d:4 * d].astype(F32))
        yar = yar_ref[...].astype(F32)
        dya = dm * sa
        dh = dproj(slice(4 * d, 5 * d), dm * (yar * pscale) * sa * (1.0 - sa))
        red_ref[2:3, :] += _colsum(dya * yar)
        dyarb = (dya * pscale).astype(BF16)
        dyar_ref[...] = dyarb
        sb = jax.nn.sigmoid(p5_ref[:, 4 * d:5 * d].astype(F32))
        dybb = (dm * sb).astype(BF16)
        dqm_ref[:, 0:d] = dybb
        dh = dh + dproj(slice(5 * d, 6 * d), dm * yb_ref[...].astype(F32) * sb * (1.0 - sb))

        for g, window in enumerate(POOL_WINDOWS):
            cols = slice(g * gw, (g + 1) * gw)
            rows = 8 * (window - 1)
            dpg = _dot_nt(dyarb[:, cols], wp_ref[g])
            dpgs = dpg * _inv_count(tix * tm, window)
            halo = _halo_bottom(dpgs[:rows, :], dpgcarry[:rows, cols])
            dpgcarry[:, cols] = dpgs[:pool_rows, :]
            s, shift = jnp.concatenate([dpgs, halo], axis=0), 1
            while shift < window:
                s = s[:s.shape[0] - 8 * shift, :] + s[8 * shift:, :]
                shift *= 2
            dh = dh + dproj(cols, s - dpg)

        dq = _dot_nt(dybb, wb_ref[...])
        ux = p5_ref[:, 0:d].astype(F32)
        uc = p5_ref[:, 2 * d:3 * d].astype(F32)
        dh = dh + dproj(slice(2 * d, 3 * d), dq * cv_ref[...].astype(F32))
        dcv = dq * p5_ref[:, d:2 * d].astype(F32)
        taps = (vec_ref[4:5, :], vec_ref[5:6, :], vec_ref[6:7, :])
        dpv, (t0, t1, t2) = _conv3_bwd(dcv, dcvcarry, slice(0, d), taps, uc * ux)
        red_ref[3:4, :] += _colsum(dcv)
        red_ref[4:5, :] += t0
        red_ref[5:6, :] += t1
        red_ref[6:7, :] += t2
        dh = dh + dproj(slice(d, 2 * d), dpv * uc)
        dh = dh + dproj(slice(3 * d, 4 * d), dpv * ux)

        _, r, nh = _prenorm(_interleave(x_ref[...]), vp_ref)
        gx_ref[...] = _deinterleave(dx1_v + _prenorm_bwd(dh, r, nh, vp_ref, pred_ref))

    rev = lambda n: pl.BlockSpec((tm, n), lambda i: (nt - 1 - i, 0))
    return _call(
        body, "mixer_block_bwd", (nt,), [rev(d)] * 5 + [rev(5 * d), rev(d)] + [_vmem()] * 6,
        [rev(d), rev(2 * d), rev(6 * d), rev(d), pl.BlockSpec((16, d), lambda i: (0, 0)),
         pl.BlockSpec((8, d), lambda i: (0, 0))],
        [_sds((t, d), BF16), _sds((t, 2 * d), BF16), _sds((t, 6 * d), BF16), _sds((t, d), F32), _sds((16, d), F32),
         _sds((8, d), F32)],
        [pltpu.VMEM((pool_rows, d), F32), pltpu.VMEM((16, d), F32)],
        _params("arbitrary"), (dx1, ob, yarb, ybb, cvb, p5b, x, vec_pre, vec, w_in, w_pool, w_bout, w_o), job)


def _matmul_tn(a, b, bm, bn, tk, by_col_block, name, job=None):
    t, m = a.shape
    n = b.shape[1]
    nk = t // tk

    def body(a_ref, b_ref, o_ref, acc_ref):
        k = pl.program_id(2)

        @pl.when(k == 0)
        def _():
            acc_ref[...] = jnp.zeros_like(acc_ref)

        acc_ref[...] += _dot_tn(a_ref[...], b_ref[...])

        @pl.when(k == nk - 1)
        def _():
            o_ref[...] = acc_ref[...].astype(o_ref.dtype)

    if by_col_block:
        out_shape = _sds((n // bn, m, bn), BF16)
        out_spec = pl.BlockSpec((None, bm, bn), lambda i, j, k: (j, i, 0))
    else:
        out_shape = _sds((m, n), BF16)
        out_spec = pl.BlockSpec((bm, bn), lambda i, j, k: (i, j))
    out = _call(body, name, (m // bm, n // bn, nk),
                [pl.BlockSpec((tk, bm), lambda i, j, k: (k, i)), pl.BlockSpec((tk, bn), lambda i, j, k: (k, j))],
                [out_spec], [out_shape], [pltpu.VMEM((bm, bn), F32)],
                _params("arbitrary", "arbitrary", "arbitrary"), (a, b), job)
    return out if job is not None else out[0]


def _matmul_tn_groups(a, b, groups, tk, name, job=None):
    t, m = a.shape
    w = m // groups
    nk = t // tk

    def body(a_ref, b_ref, o_ref, acc_ref):
        k = pl.program_id(1)

        @pl.when(k == 0)
        def _():
            acc_ref[...] = jnp.zeros_like(acc_ref)

        acc_ref[...] += _dot_tn(a_ref[...], b_ref[...])

        @pl.when(k == nk - 1)
        def _():
            o_ref[...] = acc_ref[...].astype(o_ref.dtype)

    blk = pl.BlockSpec((tk, w), lambda g, k: (k, g))
    out = _call(body, name, (groups, nk), [blk, blk], [pl.BlockSpec((None, w, w), lambda g, k: (g, 0, 0))],
                [_sds((groups, w, w), BF16)], [pltpu.VMEM((w, w), F32)], _params("arbitrary", "arbitrary"), (a, b), job)
    return out if job is not None else out[0]


def _round_up(n, k):
    return (n + k - 1) // k * k


def _pad_last(a, parts, width):
    lead = a.shape[:-1]
    piece = a.shape[-1] // parts
    a = a.reshape(lead + (parts, piece))
    a = jnp.pad(a, [(0, 0)] * len(lead) + [(0, 0), (0, width - piece)])
    return a.reshape(lead + (parts * width,))


def _unpad_last(a, parts, piece):
    lead = a.shape[:-1]
    width = a.shape[-1] // parts
    return a.reshape(lead + (parts, width))[..., :piece].reshape(lead + (parts * piece,))


def _rows8(rows, width):
    n = _round_up(len(rows), 8)
    rows = list(rows) + [jnp.zeros((1, width), F32)] * (n - len(rows))
    return jnp.concatenate(rows, axis=0)


def kernel(x, c, g_pre_mix, g_post_mix, g_pre_ffn, g_post_ffn, w_ada, b_ada, w_in, w_pool, pool_scale, conv_w, conv_b, w_bout, w_o, w_up, ffn_conv_w, ffn_conv_b, w_down, loss_target, m_g_pre_mix, m_g_post_mix, m_g_pre_ffn, m_g_post_ffn, m_w_ada, m_b_ada, m_w_in, m_w_pool, m_pool_scale, m_conv_w, m_conv_b, m_w_bout, m_w_o, m_w_up, m_ffn_conv_w, m_ffn_conv_b, m_w_down, v_g_pre_mix, v_g_post_mix, v_g_pre_ffn, v_g_post_ffn, v_w_ada, v_b_ada, v_w_in, v_w_pool, v_pool_scale, v_conv_w, v_conv_b, v_w_bout, v_w_o, v_w_up, v_ffn_conv_w, v_ffn_conv_b, v_w_down):
    t, d = x.shape[1], x.shape[2]
    ngroups = len(POOL_WINDOWS)
    gw = d // ngroups
    ada_n = w_ada.shape[2]
    in_n = w_in.shape[2]
    unit = w_down.shape[1]
    unit_p = _round_up(unit, LANES)
    fp = NDEV * unit_p

    xi, yi, ci = _position()
    me = _linear(xi, yi, ci)
    chip = 2 * xi + yi
    core = jnp.reshape(ci, (1,)).astype(jnp.int32)
    sel = jnp.stack([2 * chip + ci, chip]).astype(jnp.int32)

    x2 = x.reshape(t, d)
    target = loss_target.reshape(t, d)

    w_in_f, g_bout, g_o, w_pool_f = _run_job(_allgather_job(
        [w_in[0].astype(BF16), w_bout[0].astype(BF16), w_o[0].astype(BF16), w_pool[0].astype(BF16)],
        ["cols", "rows", "rows", "mid"], 0.5), "allgather_mixer_weights")
    w_bout_f = g_bout.reshape(d, d)
    w_o_f = g_o.reshape(d, d)

    cw_n = conv_w.shape[2]
    fcw_p = _pad_last(ffn_conv_w[0], 2, unit_p)
    pack = jnp.concatenate([c.reshape(1, d), conv_w[0].reshape(1, 3 * cw_n), fcw_p.reshape(1, 6 * unit_p)], axis=1)
    pack_n = _round_up(pack.shape[1], LANES)
    pack = jnp.pad(pack, ((0, 0), (0, pack_n - pack.shape[1])))
    pack = jnp.pad(pack, ((0, 7), (0, 0)))
    gathered = _small_allgather(pack, "allgather_cond", False)[0][:, 0, :]
    c_all = gathered[:, :d]
    conv_w_full = gathered[:, d:d + 3 * cw_n].reshape(NDEV, 3, cw_n).transpose(1, 0, 2).reshape(3, NDEV * cw_n)
    fcw_full = gathered[:, d + 3 * cw_n:d + 3 * cw_n + 6 * unit_p].reshape(NDEV, 3, 2 * unit_p)
    fcw_full = fcw_full.transpose(1, 0, 2).reshape(3, 2 * fp)
    fcb_full = _pad_last(ffn_conv_b, 2 * NDEV, unit_p)
    fcv = jnp.concatenate([fcw_full, fcb_full, jnp.zeros((4, 2 * fp), F32)], axis=0)

    c16 = jnp.pad(c_all, ((0, 8), (0, 0))).astype(BF16)
    b_piece = lax.dynamic_slice_in_dim(b_ada, me * ada_n, ada_n, axis=1)
    mod_piece = _mod_piece(c16, w_ada[0], b_piece)[:NDEV]
    mod_all = _small_allgather(mod_piece, "allgather_mod", False)[0]
    mod = lax.dynamic_index_in_dim(mod_all, me, axis=1, keepdims=False).reshape(1, NDEV * ada_n)
    sh1, sc1, gt1, sh2, sc2, gt2 = [mod[:, k * d:(k + 1) * d] for k in range(6)]

    ffn_weights = _allgather_job(
        [_pad_last(w_up[0], 2, unit_p).astype(BF16), jnp.pad(w_down[0], ((0, unit_p - unit), (0, 0))).astype(BF16)],
        ["cols", "rows"], 0.75)
    vec_pre_mix = _rows8([g_pre_mix, 1.0 + sc1, sh1], d)
    vec_mix = _rows8([gt1, g_post_mix, pool_scale, conv_b, conv_w_full[0:1], conv_w_full[1:2], conv_w_full[2:3]], d)
    h1b, p5b, pgb, qmb, cvb, yarb, ybb, ob, x1, w_up_f, g_down = _mixer_block_fwd(
        x2, vec_pre_mix, vec_mix, w_in_f, w_pool_f, w_bout_f, w_o_f, ffn_weights)
    w_down_f = g_down.reshape(fp, d)
    vec_pre_ffn = _rows8([g_pre_ffn, 1.0 + sc2, sh2], d)
    vec_ffn = _rows8([gt2, g_post_ffn], d)
    h2b, upb, upreb, ab, ffb, dy, loss_part = _ffn_block_fwd(x1, target, vec_pre_ffn, vec_ffn, fcv, w_up_f, w_down_f)

    tk = min(2048, t)
    chip_sum = lambda gs, ss: [_chip_sum(g, s, core) for g, s in zip(gs, ss)]
    dffb, dupre, dx1, red_ffn, red_fconv, red_pre_ffn = _ffn_block_bwd(
        dy, ffb, x1, upb, upreb, vec_pre_ffn, vec_ffn, fcv, w_up_f, w_down_f)
    gr_up = _matmul_tn(h2b, dupre, d, 2 * unit_p, tk, True, "wgrad_up")
    gr_down = _matmul_tn(ab, dffb, 2 * unit_p, d, tk, False, "wgrad_down").reshape(NDEV, unit_p, d)
    sib_ffn = _run_job(_sibling_job([gr_up, gr_down]), "rs_sibling_ffn")
    dyarb, dqmb, dproj, grad_x, red_mix, red_pre_mix = _mixer_block_bwd(
        dx1, ob, yarb, ybb, cvb, p5b, x2, vec_pre_mix, vec_mix, w_in_f, w_pool_f, w_bout_f, w_o_f, None)
    gr_in, fc_up, fc_down = _matmul_tn(h1b, dproj, d, in_n, tk, True, "wgrad_in",
                                       _chips_job(chip_sum([gr_up, gr_down], sib_ffn)))
    sib_in = _run_job(_sibling_job([gr_in]), "rs_sibling_in")
    gr_qm, fc_in = _matmul_tn_groups(qmb, dqmb, 2, tk, "wgrad_bout_o", _chips_job(chip_sum([gr_in], sib_in)))
    gr_bout = gr_qm[0].reshape(NDEV, d // NDEV, d)
    gr_o = gr_qm[1].reshape(NDEV, d // NDEV, d)
    gr_pool = _matmul_tn_groups(pgb, dyarb, ngroups, min(4096, t), "wgrad_pool")
    gr_pool = gr_pool.reshape(ngroups, NDEV, gw // NDEV, gw).transpose(1, 0, 2, 3).reshape(NDEV, -1, gw)
    rest = [gr_bout, gr_o, gr_pool]
    sib_rest = _run_job(_sibling_job(rest), "rs_sibling_rest")
    fc_rest = _run_job(_chips_job(chip_sum(rest, sib_rest)), "rs_chips_rest")

    def big(grad, from_sibling, from_chips, w, m, v, name, col_parts=1):
        shape = w.shape
        w2, m2, v2 = [a.reshape((-1, shape[-1])) for a in (w, m, v)]
        outs = _reduce_adamw(grad, from_sibling, from_chips, sel, w2, m2, v2, name, col_parts)
        return [a.reshape(shape) for a in outs]

    g_w_up, d_w_up, nm_w_up, nv_w_up = big(gr_up, sib_ffn[0], fc_up, w_up, m_w_up, v_w_up, "adamw_up", col_parts=2)
    g_w_down, d_w_down, nm_w_down, nv_w_down = big(gr_down, sib_ffn[1], fc_down, w_down, m_w_down, v_w_down, "adamw_down")
    g_w_in, d_w_in, nm_w_in, nv_w_in = big(gr_in, sib_in[0], fc_in, w_in, m_w_in, v_w_in, "adamw_in")
    g_w_bout, d_w_bout, nm_w_bout, nv_w_bout = big(gr_bout, sib_rest[0], fc_rest[0], w_bout, m_w_bout, v_w_bout, "adamw_bout")
    g_w_o, d_w_o, nm_w_o, nv_w_o = big(gr_o, sib_rest[1], fc_rest[1], w_o, m_w_o, v_w_o, "adamw_o")
    g_w_pool, d_w_pool, nm_w_pool, nv_w_pool = big(gr_pool, sib_rest[2], fc_rest[2], w_pool, m_w_pool, v_w_pool, "adamw_pool")

    dmod = [red_pre_mix[0:1], red_pre_mix[1:2], red_mix[1:2], red_pre_ffn[0:1], red_pre_ffn[1:2], red_ffn[1:2]]
    small = [red_pre_mix[2:3], red_mix[0:1], red_pre_ffn[2:3], red_ffn[0:1], red_mix[2:3], red_mix[3:4],
             red_mix[4:5], red_mix[5:6], red_mix[6:7]] + dmod
    flat = jnp.concatenate(small + [red_fconv[0:4].reshape(1, 8 * fp), loss_part[0:1, 0:1]], axis=1)
    flat_n = flat.shape[1]
    width = 8 * LANES
    rows = _round_up(-(-flat_n // width), 8)
    flat = jnp.pad(flat, ((0, 0), (0, rows * width - flat_n))).reshape(rows, width)
    gat, tot = _small_allgather(flat, "allreduce_small", True)
    tot = tot.reshape(1, rows * width)
    gat = gat.reshape(NDEV, rows * width)
    take = lambda k: tot[:, k * d:(k + 1) * d]
    g_g_pre_mix, g_g_post_mix, g_g_pre_ffn, g_g_post_ffn, g_pool_scale, g_conv_b = [take(k) for k in range(6)]
    g_conv_w_full = jnp.concatenate([take(6), take(7), take(8)], axis=0)
    g_conv_w = lax.dynamic_slice_in_dim(g_conv_w_full, me * cw_n, cw_n, axis=1)
    g_b_ada = tot[:, 9 * d:15 * d]
    dmod_all = gat[:, 9 * d:15 * d]
    fconv_tot = tot[:, 15 * d:15 * d + 8 * fp].reshape(4, 2 * fp)
    loss = 0.5 * tot[0, 15 * d + 8 * fp]
    g_ffn_conv_b = _unpad_last(fconv_tot[3:4], 2 * NDEV, unit)
    g_fcw_mine = lax.dynamic_slice_in_dim(fconv_tot[0:3], me * 2 * unit_p, 2 * unit_p, axis=1)
    g_ffn_conv_w = _unpad_last(g_fcw_mine, 2, unit)
    dmod_piece = lax.dynamic_slice_in_dim(dmod_all, me * ada_n, ada_n, axis=1)
    g_w_ada = _wada_grad(c16, jnp.pad(dmod_piece, ((0, 8), (0, 0))).astype(BF16))

    names_small = [(g_pre_mix, g_g_pre_mix, m_g_pre_mix, v_g_pre_mix), (g_post_mix, g_g_post_mix, m_g_post_mix, v_g_post_mix),
                   (g_pre_ffn, g_g_pre_ffn, m_g_pre_ffn, v_g_pre_ffn), (g_post_ffn, g_g_post_ffn, m_g_post_ffn, v_g_post_ffn),
                   (b_ada, g_b_ada, m_b_ada, v_b_ada), (pool_scale, g_pool_scale, m_pool_scale, v_pool_scale),
                   (conv_w, g_conv_w, m_conv_w, v_conv_w), (conv_b, g_conv_b, m_conv_b, v_conv_b),
                   (ffn_conv_w, g_ffn_conv_w, m_ffn_conv_w, v_ffn_conv_w), (ffn_conv_b, g_ffn_conv_b, m_ffn_conv_b, v_ffn_conv_b)]
    sizes = [w.size for w, _, _, _ in names_small]
    total = sum(sizes)
    prow = _round_up(-(-total // width), 8)

    def pack_small(k):
        a = jnp.concatenate([q[k].reshape(1, -1) for q in names_small], axis=1)
        return jnp.pad(a, ((0, 0), (0, prow * width - total)), constant_values=1.0).reshape(prow, width)

    ds, ms, vs = _adamw(pack_small(0), pack_small(1), pack_small(2), pack_small(3), "adamw_small")

    def unpack_small(a):
        a = a.reshape(-1)
        out, off = [], 0
        for (w, _, _, _), n in zip(names_small, sizes):
            out.append(a[off:off + n].reshape(w.shape))
            off += n
        return out

    (d_g_pre_mix, d_g_post_mix, d_g_pre_ffn, d_g_post_ffn, d_b_ada, d_pool_scale, d_conv_w, d_conv_b,
     d_ffn_conv_w, d_ffn_conv_b) = unpack_small(ds)
    (nm_g_pre_mix, nm_g_post_mix, nm_g_pre_ffn, nm_g_post_ffn, nm_b_ada, nm_pool_scale, nm_conv_w, nm_conv_b,
     nm_ffn_conv_w, nm_ffn_conv_b) = unpack_small(ms)
    (nv_g_pre_mix, nv_g_post_mix, nv_g_pre_ffn, nv_g_post_ffn, nv_b_ada, nv_pool_scale, nv_conv_w, nv_conv_b,
     nv_ffn_conv_w, nv_ffn_conv_b) = unpack_small(vs)
    d_w_ada, nm_w_ada, nv_w_ada = [a.reshape(w_ada.shape) for a in
                                   _adamw(w_ada[0], g_w_ada, m_w_ada[0], v_w_ada[0], "adamw_ada")]

    grads = [g_g_pre_mix, g_g_post_mix, g_g_pre_ffn, g_g_post_ffn, g_w_ada.reshape(w_ada.shape), g_b_ada, g_w_in,
             g_w_pool, g_pool_scale, g_conv_w.reshape(conv_w.shape), g_conv_b, g_w_bout, g_w_o, g_w_up,
             g_ffn_conv_w.reshape(ffn_conv_w.shape), g_ffn_conv_b, g_w_down]
    deltas = [d_g_pre_mix, d_g_post_mix, d_g_pre_ffn, d_g_post_ffn, d_w_ada, d_b_ada, d_w_in, d_w_pool, d_pool_scale,
              d_conv_w, d_conv_b, d_w_bout, d_w_o, d_w_up, d_ffn_conv_w, d_ffn_conv_b, d_w_down]
    new_m = [nm_g_pre_mix, nm_g_post_mix, nm_g_pre_ffn, nm_g_post_ffn, nm_w_ada, nm_b_ada, nm_w_in, nm_w_pool,
             nm_pool_scale, nm_conv_w, nm_conv_b, nm_w_bout, nm_w_o, nm_w_up, nm_ffn_conv_w, nm_ffn_conv_b, nm_w_down]
    new_v = [nv_g_pre_mix, nv_g_post_mix, nv_g_pre_ffn, nv_g_post_ffn, nv_w_ada, nv_b_ada, nv_w_in, nv_w_pool,
             nv_pool_scale, nv_conv_w, nv_conv_b, nv_w_bout, nv_w_o, nv_w_up, nv_ffn_conv_w, nv_ffn_conv_b, nv_w_down]
    return (loss, grad_x.reshape(x.shape), *grads, *deltas, *new_m, *new_v)
```

```python
import math

import jax
import jax.numpy as jnp
from jax import lax
from jax.experimental import pallas as pl
from jax.experimental.pallas import tpu as pltpu

F32 = jnp.float32
BF16 = jnp.bfloat16
MESH = pl.DeviceIdType.MESH

NDEV = 8
NCHIP = 4
EPS = 1e-6
POOL_WINDOWS = (2, 4, 8, 16)
LANES = 128
ADAM_LR = 0.001
ADAM_B1 = 0.9
ADAM_B2 = 0.999
ADAM_EPS = 1e-08
ADAM_WD = 0.01
ADAM_STEP = 10
GELU_C0 = math.sqrt(2.0 / math.pi)
GELU_C1 = 0.044715
VMEM_LIMIT = 56 * 2**20


def _vmem():
    return pl.BlockSpec(memory_space=pltpu.VMEM)


def _any():
    return pl.BlockSpec(memory_space=pl.ANY)


def _params(*sem):
    return pltpu.CompilerParams(dimension_semantics=sem, vmem_limit_bytes=VMEM_LIMIT)


def _sds(shape, dtype):
    return jax.ShapeDtypeStruct(tuple(shape), dtype)


def _position():
    return lax.axis_index("x"), lax.axis_index("y"), lax.axis_index("c")


def _linear(x, y, c):
    return 4 * x + 2 * y + c


def _dot(a, b):
    return jnp.dot(a, b, preferred_element_type=F32)


def _dot_nt(a, b):
    return lax.dot_general(a, b, (((1,), (1,)), ((), ())), preferred_element_type=F32)


def _dot_tn(a, b):
    return lax.dot_general(a, b, (((0,), (0,)), ((), ())), preferred_element_type=F32)


def _colsum(v):
    return jnp.sum(v, axis=0, keepdims=True)


def _rowmean(v):
    return jnp.mean(v, axis=-1, keepdims=True)


GROUP = 256


def _interleave(v):
    g, n = v.shape
    return jnp.swapaxes(v.reshape(8, g // 8, n), 0, 1).reshape(g, n)


def _deinterleave(v):
    g, n = v.shape
    return jnp.swapaxes(v.reshape(g // 8, 8, n), 0, 1).reshape(g, n)


def _halo_top(cur_last, prev_last):
    rows, n = cur_last.shape
    c3 = cur_last.reshape(rows // 8, 8, n)
    p3 = prev_last.reshape(rows // 8, 8, n)
    sub = lax.broadcasted_iota(jnp.int32, c3.shape, 1)
    return jnp.where(sub == 0, pltpu.roll(p3, 1, 1), pltpu.roll(c3, 1, 1)).reshape(rows, n)


def _halo_bottom(cur_first, next_first):
    rows, n = cur_first.shape
    c3 = cur_first.reshape(rows // 8, 8, n)
    n3 = next_first.reshape(rows // 8, 8, n)
    sub = lax.broadcasted_iota(jnp.int32, c3.shape, 1)
    return jnp.where(sub == 7, pltpu.roll(n3, 7, 1), pltpu.roll(c3, 7, 1)).reshape(rows, n)


def _shift_down(v, halo, k):
    rows = v.shape[0]
    return jnp.concatenate([halo[halo.shape[0] - 8 * k:, :], v[:rows - 8 * k, :]], axis=0)


def _shift_up(v, halo, k):
    return jnp.concatenate([v[8 * k:, :], halo[:8 * k, :]], axis=0)


def _inv_count(first_token, window):
    row = lax.broadcasted_iota(jnp.int32, (GROUP, 1), 0)
    t = first_token + (row % 8) * (GROUP // 8) + row // 8
    return 1.0 / jnp.minimum(t + 1, window).astype(F32)


class _Job:
    def __init__(self, inputs, out_shape, scratch, phases):
        self.inputs, self.out_shape, self.scratch, self.phases = list(inputs), list(out_shape), list(scratch), phases


def _call(body, name, grid, in_specs, out_specs, out_shape, scratch_shapes, params, operands, job=None):
    if job is None:
        return pl.pallas_call(body, name=name, grid=grid, in_specs=in_specs, out_specs=out_specs, out_shape=out_shape,
                              scratch_shapes=scratch_shapes, compiler_params=params)(*operands)
    n_in, n_out, n_scr = len(in_specs), len(out_specs), len(scratch_shapes)
    j_in, j_out = len(job.inputs), len(job.out_shape)
    steps = math.prod(grid)

    def hosted(*refs):
        own_in, refs = refs[:n_in], refs[n_in:]
        jin, refs = refs[:j_in], refs[j_in:]
        own_out, refs = refs[:n_out], refs[n_out:]
        jout, refs = refs[:j_out], refs[j_out:]
        own_scr, jscr = refs[:n_scr], refs[n_scr:]
        step = pl.program_id(0)
        for axis in range(1, len(grid)):
            step = step * grid[axis] + pl.program_id(axis)
        for frac, fn in job.phases[:-1]:
            pl.when(step == int(frac * (steps - 1)))(lambda fn=fn: fn(jin, jout, jscr))
        body(*own_in, *own_out, *own_scr)
        pl.when(step == steps - 1)(lambda: job.phases[-1][1](jin, jout, jscr))

    return pl.pallas_call(
        hosted, name=name, grid=grid, in_specs=list(in_specs) + [_any()] * j_in,
        out_specs=list(out_specs) + [_any()] * j_out, out_shape=list(out_shape) + job.out_shape,
        scratch_shapes=list(scratch_shapes) + job.scratch, compiler_params=params)(*operands, *job.inputs)


def _run_job(job, name):
    n_in, n_out = len(job.inputs), len(job.out_shape)

    def body(*refs):
        for _, fn in job.phases:
            fn(refs[:n_in], refs[n_in:n_in + n_out], refs[n_in + n_out:])

    return pl.pallas_call(body, name=name, out_shape=job.out_shape, in_specs=[_any()] * n_in,
                          out_specs=[_any()] * n_out, scratch_shapes=job.scratch)(*job.inputs)


def _peers(x, y, c):
    out = []
    for k in range(1, NDEV):
        out.append(((1 - x) if k & 4 else x, (1 - y) if k & 2 else y, (1 - c) if k & 1 else c))
    return out


def _small_allreduce(v, name, job):
    r, n = v.shape
    j_in, j_out = len(job.inputs), len(job.out_shape)

    def body(v_ref, *rest):
        jin, rest = rest[:j_in], rest[j_in:]
        gat_ref, sum_ref = rest[:2]
        jout, rest = rest[2:2 + j_out], rest[2 + j_out:]
        send_sems, recv_sems, local_sem = rest[:3]
        jscr = rest[3:]
        job.phases[0][1](jin, jout, jscr)
        x, y, c = _position()
        me = _linear(x, y, c)
        mine = pltpu.make_async_copy(v_ref, gat_ref.at[me], local_sem)
        mine.start()
        peers = _peers(x, y, c)
        sends = []
        for k, peer in enumerate(peers):
            cp = pltpu.make_async_remote_copy(src_ref=v_ref, dst_ref=gat_ref.at[me], send_sem=send_sems.at[k],
                                              recv_sem=recv_sems.at[k], device_id=peer, device_id_type=MESH)
            cp.start()
            sends.append(cp)
        for k, peer in enumerate(peers):
            pltpu.make_async_remote_copy(src_ref=v_ref, dst_ref=gat_ref.at[_linear(*peer)], send_sem=send_sems.at[k],
                                         recv_sem=recv_sems.at[k], device_id=peer, device_id_type=MESH).wait_recv()
        for cp in sends:
            cp.wait_send()
        mine.wait()
        acc = gat_ref[0]
        for j in range(1, NDEV):
            acc = acc + gat_ref[j]
        sum_ref[...] = acc
        job.phases[-1][1](jin, jout, jscr)

    return pl.pallas_call(
        body, name=name, out_shape=[_sds((NDEV, r, n), F32), _sds((r, n), F32)] + job.out_shape,
        in_specs=[_vmem()] + [_any()] * j_in, out_specs=[_vmem()] * 2 + [_any()] * j_out,
        scratch_shapes=[pltpu.SemaphoreType.DMA((NDEV - 1,)), pltpu.SemaphoreType.DMA((NDEV - 1,)),
                        pltpu.SemaphoreType.DMA(())] + job.scratch,
    )(v, *job.inputs)


def _exchange_rows(src_for, dst_ref, sems):
    send_sems, recv_sems, local_sem = sems
    x, y, c = _position()
    me = _linear(x, y, c)
    row = lambda j: dst_ref.at[pl.ds(j, 1), :]
    mine = pltpu.make_async_copy(src_for(me), row(me), local_sem)
    mine.start()
    peers = _peers(x, y, c)
    sends = []
    for k, peer in enumerate(peers):
        cp = pltpu.make_async_remote_copy(src_ref=src_for(_linear(*peer)), dst_ref=row(me), send_sem=send_sems.at[k],
                                          recv_sem=recv_sems.at[k], device_id=peer, device_id_type=MESH)
        cp.start()
        sends.append(cp)
    for k, peer in enumerate(peers):
        pltpu.make_async_remote_copy(src_ref=src_for(me), dst_ref=row(_linear(*peer)), send_sem=send_sems.at[k],
                                     recv_sem=recv_sems.at[k], device_id=peer, device_id_type=MESH).wait_recv()
    for cp in sends:
        cp.wait_send()
    mine.wait()


def _gather_weights_and_modulation(job, pack, w_ada, b_piece, d):
    n = pack.shape[1]
    m = w_ada.shape[1]
    j_in, j_out = len(job.inputs), len(job.out_shape)
    row_sems = [pltpu.SemaphoreType.DMA((NDEV - 1,)), pltpu.SemaphoreType.DMA((NDEV - 1,)), pltpu.SemaphoreType.DMA(())]

    def body(pack_ref, wada_ref, bp_ref, *rest):
        jin, rest = rest[:j_in], rest[j_in:]
        gat_ref, mod_ref = rest[:2]
        jout, rest = rest[2:2 + j_out], rest[2 + j_out:]
        sems1, sems2, piece, jscr = rest[0:3], rest[3:6], rest[6], rest[7:]
        start, forward, finish = [fn for _, fn in job.phases]
        start(jin, jout, jscr)
        _exchange_rows(lambda j: pack_ref, gat_ref, sems1)
        c16 = jnp.concatenate([gat_ref[:, 0:d], jnp.zeros((NDEV, d), F32)], axis=0).astype(BF16)
        piece[...] = (_dot(c16, wada_ref[...].astype(BF16)) + bp_ref[...])[0:NDEV, :]
        _exchange_rows(lambda j: piece.at[pl.ds(j, 1), :], mod_ref, sems2)
        forward(jin, jout, jscr)
        finish(jin, jout, jscr)

    return pl.pallas_call(
        body, name="gather_weights_and_modulation",
        out_shape=[_sds((NDEV, n), F32), _sds((NDEV, m), F32)] + job.out_shape,
        in_specs=[_vmem()] * 3 + [_any()] * j_in, out_specs=[_vmem()] * 2 + [_any()] * j_out,
        scratch_shapes=row_sems + row_sems + [pltpu.VMEM((NDEV, m), F32)] + job.scratch,
    )(pack, w_ada, b_piece, *job.inputs)


def _gathered(shard, layout):
    if layout == "rows":
        return (NDEV,) + shard.shape, lambda ref, j: ref.at[j]
    if layout == "cols":
        r, c = shard.shape
        return (r, NDEV * c), lambda ref, j: ref.at[:, pl.ds(pl.multiple_of(j * c, LANES), c)]
    g, r, c = shard.shape
    return (g, NDEV * r, c), lambda ref, j: ref.at[:, pl.ds(pl.multiple_of(j * r, 16), r), :]


def _allgather_job(shards, layouts, forward_at):
    n = len(shards)
    specs = [_gathered(s, l) for s, l in zip(shards, layouts)]

    def copies(src, dst, sems):
        send_sems, recv_sems, _ = sems
        x, y, c = _position()
        me, sibling = (x, y, c), (x, y, 1 - c)
        chips = [(1 - x, y), (x, 1 - y), (1 - x, 1 - y)]

        def copy(a, k, block, to, from_src=False):
            blk = specs[a][1](dst[a], _linear(*block))
            return pltpu.make_async_remote_copy(src_ref=src[a] if from_src else blk, dst_ref=blk,
                                                send_sem=send_sems.at[a, k], recv_sem=recv_sems.at[a, k],
                                                device_id=to, device_id_type=MESH)
        return copy, me, sibling, chips, c

    def local(src, dst, sems):
        x, y, c = _position()
        return [pltpu.make_async_copy(src[a], specs[a][1](dst[a], _linear(x, y, c)), sems[2].at[a]) for a in range(n)]

    def first(src, dst, sems):
        copy, me, sibling, chips, c = copies(src, dst, sems)
        out = [copy(a, 1 + j, me, (*chip, c), from_src=True) for j, chip in enumerate(chips) for a in range(n)]
        return out + [copy(a, 0, me, sibling, from_src=True) for a in range(n)]

    def passed(src, dst, sems):
        copy, me, sibling, chips, c = copies(src, dst, sems)
        return [copy(a, 4 + j, (*chip, c), sibling) for j, chip in enumerate(chips) for a in range(n)]

    def start(src, dst, sems):
        for cp in local(src, dst, sems) + first(src, dst, sems):
            cp.start()

    def forward(src, dst, sems):
        copy, me, sibling, chips, c = copies(src, dst, sems)
        for j, chip in enumerate(chips):
            for a in range(n):
                copy(a, 1 + j, (*chip, c), me).wait_recv()
        for cp in passed(src, dst, sems):
            cp.start()

    def finish(src, dst, sems):
        copy, me, sibling, chips, c = copies(src, dst, sems)
        for a in range(n):
            copy(a, 0, sibling, me).wait_recv()
        for j, chip in enumerate(chips):
            for a in range(n):
                copy(a, 4 + j, (*chip, 1 - c), me).wait_recv()
        for cp in first(src, dst, sems) + passed(src, dst, sems):
            cp.wait_send()
        for cp in local(src, dst, sems):
            cp.wait()

    return _Job(shards, [_sds(spec[0], s.dtype) for spec, s in zip(specs, shards)],
                [pltpu.SemaphoreType.DMA((n, 7)), pltpu.SemaphoreType.DMA((n, 7)), pltpu.SemaphoreType.DMA((n,))],
                [(0.0, start), (forward_at, forward), (1.0, finish)])


def _sibling_job(grads):
    n = len(grads)

    def copies(src, dst, sems):
        x, y, c = _position()
        return [pltpu.make_async_remote_copy(src_ref=src[a].at[2 * q + 1 - c], dst_ref=dst[a].at[q],
                                             send_sem=sems[0].at[a, q], recv_sem=sems[1].at[a, q],
                                             device_id=(x, y, 1 - c), device_id_type=MESH)
                for a in range(n) for q in range(NCHIP)]

    return _exchange_job(grads, NCHIP, copies)


def _chips_job(chip_sums):
    n = len(chip_sums)

    def copies(src, dst, sems):
        x, y, c = _position()
        chips = [(1 - x, y), (x, 1 - y), (1 - x, 1 - y)]
        return [pltpu.make_async_remote_copy(src_ref=src[a].at[2 * chip[0] + chip[1]], dst_ref=dst[a].at[j],
                                             send_sem=sems[0].at[a, j], recv_sem=sems[1].at[a, j],
                                             device_id=(*chip, c), device_id_type=MESH)
                for j, chip in enumerate(chips) for a in range(n)]

    return _exchange_job(chip_sums, 3, copies)


def _exchange_job(arrays, slots, copies):
    n = len(arrays)

    def start(src, dst, sems):
        for cp in copies(src, dst, sems):
            cp.start()

    def finish(src, dst, sems):
        cps = copies(src, dst, sems)
        for cp in cps:
            cp.wait_recv()
        for cp in cps:
            cp.wait_send()

    return _Job(arrays, [_sds((slots,) + a.shape[1:], a.dtype) for a in arrays],
                [pltpu.SemaphoreType.DMA((n, slots)), pltpu.SemaphoreType.DMA((n, slots))],
                [(0.0, start), (1.0, finish)])


def _row_block(r):
    for rb in (512, 256, 128, 64, 32, 16):
        if r % rb == 0:
            return rb
    return r


def _chip_sum(grad, from_sibling, core):
    _, r, c = grad.shape
    rb = _row_block(r)

    def body(core_ref, g_ref, s_ref, o_ref):
        del core_ref
        o_ref[...] = (g_ref[...].astype(F32) + s_ref[...].astype(F32)).astype(o_ref.dtype)

    grid_spec = pltpu.PrefetchScalarGridSpec(
        num_scalar_prefetch=1, grid=(NCHIP, r // rb),
        in_specs=[pl.BlockSpec((None, rb, c), lambda q, i, core: (2 * q + core[0], i, 0)),
                  pl.BlockSpec((None, rb, c), lambda q, i, core: (q, i, 0))],
        out_specs=pl.BlockSpec((None, rb, c), lambda q, i, core: (q, i, 0)))
    return pl.pallas_call(body, name="rs_chip_sum", grid_spec=grid_spec, out_shape=_sds((NCHIP, r, c), BF16),
                          compiler_params=_params("parallel", "parallel"))(core, grad, from_sibling)


def _adamw_math(w, g, m, v):
    m2 = ADAM_B1 * m + (1.0 - ADAM_B1) * g
    v2 = ADAM_B2 * v + (1.0 - ADAM_B2) * jnp.square(g)
    m_hat = m2 / (1.0 - ADAM_B1 ** ADAM_STEP)
    v_hat = v2 / (1.0 - ADAM_B2 ** ADAM_STEP)
    delta = -ADAM_LR * (m_hat / (jnp.sqrt(v_hat) + ADAM_EPS) + ADAM_WD * w)
    return delta, m2, v2


def _adamw(w, g, m, v, name):
    r, c = w.shape
    rb = _row_block(r)

    def body(w_ref, g_ref, m_ref, v_ref, d_ref, m2_ref, v2_ref):
        d, m2, v2 = _adamw_math(w_ref[...], g_ref[...], m_ref[...], v_ref[...])
        d_ref[...] = d
        m2_ref[...] = m2
        v2_ref[...] = v2

    blk = pl.BlockSpec((rb, c), lambda i: (i, 0))
    return pl.pallas_call(body, name=name, grid=(r // rb,), in_specs=[blk] * 4, out_specs=[blk] * 3,
                          out_shape=[_sds((r, c), F32)] * 3, compiler_params=_params("parallel"))(w, g, m, v)


def _reduce_adamw(grad, from_sibling, from_chips, sel, w, m, v, name, col_parts=1):
    r, c = w.shape
    cp = grad.shape[2]
    rb = _row_block(r)

    def body(sel_ref, g_ref, s_ref, c0_ref, c1_ref, c2_ref, w_ref, m_ref, v_ref, go_ref, d_ref, m2_ref, v2_ref):
        del sel_ref
        g = g_ref[...].astype(F32) + s_ref[...].astype(F32)
        g = g + c0_ref[...].astype(F32)
        g = g + c1_ref[...].astype(F32)
        g = g + c2_ref[...].astype(F32)
        if col_parts > 1:
            wide, real = cp // col_parts, c // col_parts
            g = jnp.concatenate([g[:, p * wide:p * wide + real] for p in range(col_parts)], axis=1)
        d, m2, v2 = _adamw_math(w_ref[...], g, m_ref[...], v_ref[...])
        go_ref[...] = g
        d_ref[...] = d
        m2_ref[...] = m2
        v2_ref[...] = v2

    blk = pl.BlockSpec((rb, c), lambda i, sel: (i, 0))
    grid_spec = pltpu.PrefetchScalarGridSpec(
        num_scalar_prefetch=1, grid=(r // rb,),
        in_specs=[pl.BlockSpec((None, rb, cp), lambda i, sel: (sel[0], i, 0)),
                  pl.BlockSpec((None, rb, cp), lambda i, sel: (sel[1], i, 0)),
                  pl.BlockSpec((None, rb, cp), lambda i, sel: (0, i, 0)),
                  pl.BlockSpec((None, rb, cp), lambda i, sel: (1, i, 0)),
                  pl.BlockSpec((None, rb, cp), lambda i, sel: (2, i, 0)),
                  blk, blk, blk],
        out_specs=[blk] * 4)
    return pl.pallas_call(body, name=name, grid_spec=grid_spec, out_shape=[_sds((r, c), F32)] * 4,
                          compiler_params=_params("parallel"))(sel, grad, from_sibling, from_chips, from_chips,
                                                               from_chips, w, m, v)


def _wada_grad(c_all, dmod_piece):
    d = c_all.shape[1]
    n = dmod_piece.shape[1]

    def body(c_ref, dm_ref, o_ref):
        o_ref[...] = _dot_tn(c_ref[...], dm_ref[...])

    return pl.pallas_call(body, name="ada_wgrad", out_shape=_sds((d, n), F32),
                          in_specs=[_vmem()] * 2, out_specs=_vmem())(c_all, dmod_piece)


def _conv_taps(ref, col):
    return ref[0:1, col], ref[1:2, col], ref[2:3, col]


def _gelu_parts(u):
    th = jnp.tanh(GELU_C0 * (u + GELU_C1 * (u * u * u)))
    cdf = 0.5 * (1.0 + th)
    return cdf, th


def _conv3_bwd(dv, carry, col, taps, x):
    halo = _halo_bottom(dv[:16, :], carry[:, col])
    carry[:, col] = dv[:16, :]
    d1 = _shift_up(dv, halo, 1)
    d2 = _shift_up(dv, halo, 2)
    w0, w1, w2 = taps
    dx = w2 * dv
    dx = dx + w1 * d1
    dx = dx + w0 * d2
    return dx, (_colsum(d2 * x), _colsum(d1 * x), _colsum(dv * x))


def _prenorm(x, vp_ref):
    r = lax.rsqrt(_rowmean(x * x) + EPS)
    nh = x * r
    return (nh * vp_ref[0:1, :]) * vp_ref[1:2, :] + vp_ref[2:3, :], r, nh


def _prenorm_bwd(dh, r, nh, vp_ref, red_ref):
    g, sc1 = vp_ref[0:1, :], vp_ref[1:2, :]
    red_ref[0:1, :] += _colsum(dh)
    red_ref[1:2, :] += _colsum(dh * (nh * g))
    red_ref[2:3, :] += _colsum(dh * nh * sc1)
    dnh = dh * g * sc1
    return r * (dnh - nh * _rowmean(dnh * nh))


def _postnorm_bwd(dres, z, gate, gpost, red_ref):
    r = lax.rsqrt(_rowmean(z * z) + EPS)
    nh = z * r
    dn = dres * gate
    red_ref[0:1, :] += _colsum(dn * nh)
    red_ref[1:2, :] += _colsum(dres * (nh * gpost))
    dnh = dn * gpost
    return r * (dnh - nh * _rowmean(dnh * nh))


def _mixer_block_fwd(x, vec_pre, vec, w_in, w_pool, w_bout, w_o, job):
    t, d = x.shape
    tm = GROUP
    gw = d // len(POOL_WINDOWS)
    pool_rows = 8 * (POOL_WINDOWS[-1] - 1)

    def body(x_ref, vp_ref, vec_ref, win_ref, wp_ref, wb_ref, wo_ref,
             hb_ref, p5_ref, pg_ref, qm_ref, cv_ref, yar_ref, yb_ref, o_ref, x1_ref, mbuf, ucarry, pcarry):
        i = pl.program_id(0)

        @pl.when(i == 0)
        def _():
            ucarry[...] = jnp.zeros_like(ucarry)
            pcarry[...] = jnp.zeros_like(pcarry)

        xp = _interleave(x_ref[...])
        hb = _prenorm(xp, vp_ref)[0].astype(BF16)
        hb_ref[...] = hb
        proj = lambda k: _dot(hb, win_ref[:, k * d:(k + 1) * d])

        za = proj(4)
        p5_ref[:, 3 * d:4 * d] = za.astype(BF16)
        sa = jax.nn.sigmoid(za)
        u_pool = proj(0)
        for g, window in enumerate(POOL_WINDOWS):
            cols = slice(g * gw, (g + 1) * gw)
            rows = 8 * (window - 1)
            u = u_pool[:, cols]
            halo = _halo_top(u[tm - rows:, :], ucarry[pool_rows - rows:, cols])
            s, shift = jnp.concatenate([halo, u], axis=0), 1
            while shift < window:
                s = s[8 * shift:, :] + s[:s.shape[0] - 8 * shift, :]
                shift *= 2
            pgb = (s * _inv_count(i * tm, window) - u).astype(BF16)
            pg_ref[:, cols] = pgb
            yar = _dot(pgb, wp_ref[g])
            yar_ref[:, cols] = yar
            mbuf[:, cols] = sa[:, cols] * (yar * vec_ref[2:3, cols])
        ucarry[...] = u_pool[tm - pool_rows:, :]

        ux = proj(1)
        uc = proj(3)
        p5_ref[:, 0:d] = ux.astype(BF16)
        p5_ref[:, 2 * d:3 * d] = uc.astype(BF16)
        p = uc * ux
        halo = _halo_top(p[tm - 16:, :], pcarry[...])
        pcarry[...] = p[tm - 16:, :]
        cv = vec_ref[3:4, :] + vec_ref[4:5, :] * _shift_down(p, halo, 2)
        cv = cv + vec_ref[5:6, :] * _shift_down(p, halo, 1)
        cv = cv + vec_ref[6:7, :] * p
        cv_ref[...] = cv
        ub = proj(2)
        p5_ref[:, d:2 * d] = ub.astype(BF16)
        qb = (ub * cv).astype(BF16)
        qm_ref[:, 0:d] = qb
        yb = _dot(qb, wb_ref[...])
        yb_ref[...] = yb

        zb = proj(5)
        p5_ref[:, 4 * d:5 * d] = zb.astype(BF16)
        mb = (mbuf[...] + jax.nn.sigmoid(zb) * yb).astype(BF16)
        qm_ref[:, d:2 * d] = mb
        o = _dot(mb, wo_ref[...])
        o_ref[...] = o
        r2 = lax.rsqrt(_rowmean(o * o) + EPS)
        x1_ref[...] = xp + vec_ref[0:1, :] * ((o * r2) * vec_ref[1:2, :])

    row = lambda n: pl.BlockSpec((tm, n), lambda i: (i, 0))
    widths = [d, 5 * d, d, 2 * d, d, d, d, d, d]
    return _call(
        body, "mixer_block_fwd", (t // tm,), [row(d)] + [_vmem()] * 6, [row(n) for n in widths],
        [_sds((t, n), BF16) for n in widths[:4]] + [_sds((t, n), F32) for n in widths[4:]],
        [pltpu.VMEM((tm, d), F32), pltpu.VMEM((pool_rows, d), F32), pltpu.VMEM((16, d), F32)],
        _params("arbitrary"), (x, vec_pre, vec, w_in, w_pool, w_bout, w_o), job)


def _ffn_block_fwd(x1, target, vec_pre, vec, fcv, w_up, w_down):
    t, d = x1.shape
    tm = GROUP
    fp = w_down.shape[0]
    cw = fp // 4

    def body(x1_ref, tg_ref, vp_ref, vec_ref, fcv_ref, wu_ref, wd_ref,
             hb_ref, upb_ref, upreb_ref, a_ref, ffb_ref, dy_ref, loss_ref, carry):
        i = pl.program_id(0)

        @pl.when(i == 0)
        def _():
            carry[...] = jnp.zeros_like(carry)
            loss_ref[...] = jnp.zeros_like(loss_ref)

        x1 = x1_ref[...]
        hb = _prenorm(x1, vp_ref)[0].astype(BF16)
        hb_ref[...] = hb

        cols = [(slice(j * cw, (j + 1) * cw), slice(fp + j * cw, fp + (j + 1) * cw)) for j in range(4)]
        up = lambda j: tuple(_dot(hb, wu_ref[:, col]) for col in cols[j])
        ahead = up(0)

        def conv(v, col):
            halo = _halo_top(v[tm - 16:, :], carry[:, col])
            carry[:, col] = v[tm - 16:, :]
            w0, w1, w2 = _conv_taps(fcv_ref, col)
            y = fcv_ref[3:4, col] + w0 * _shift_down(v, halo, 2)
            y = y + w1 * _shift_down(v, halo, 1)
            y = y + w2 * v
            upb_ref[:, col] = y.astype(BF16)
            upreb_ref[:, col] = v.astype(BF16)
            return y

        ff = None
        for j in range(4):
            gc, vc = cols[j]
            cur, ahead = ahead, (up(j + 1) if j < 3 else None)
            gate = conv(cur[0], gc)
            val = conv(cur[1], vc)
            ab = ((gate * _gelu_parts(gate)[0]) * val).astype(BF16)
            a_ref[:, gc] = ab
            part = _dot(ab, wd_ref[gc, :])
            ff = part if ff is None else ff + part
        ffb_ref[...] = ff.astype(BF16)
        r4 = lax.rsqrt(_rowmean(ff * ff) + EPS)
        y = x1 + vec_ref[0:1, :] * ((ff * r4) * vec_ref[1:2, :])
        e = y - _interleave(tg_ref[...])
        dy_ref[...] = e * (1.0 / d)
        loss_ref[...] += jnp.sum(_rowmean(e * e))

    row = lambda n: pl.BlockSpec((tm, n), lambda i: (i, 0))
    return pl.pallas_call(
        body, name="ffn_block_fwd", grid=(t // tm,),
        in_specs=[row(d), row(d)] + [_vmem()] * 5,
        out_specs=[row(d), row(2 * fp), row(2 * fp), row(fp), row(d), row(d), pl.BlockSpec((8, LANES), lambda i: (0, 0))],
        out_shape=[_sds((t, d), BF16), _sds((t, 2 * fp), BF16), _sds((t, 2 * fp), BF16), _sds((t, fp), BF16),
                   _sds((t, d), BF16), _sds((t, d), F32), _sds((8, LANES), F32)],
        scratch_shapes=[pltpu.VMEM((16, 2 * fp), F32)],
        compiler_params=_params("arbitrary"),
    )(x1, target, vec_pre, vec, fcv, w_up, w_down)


def _ffn_block_bwd(dy, ffb, x1, upb, upreb, vec_pre, vec, fcv, w_up, w_down):
    t, d = dy.shape
    tm = GROUP
    fp = w_down.shape[0]
    cw = fp // 4
    nt = t // tm

    def body(dy_ref, ff_ref, x1_ref, upb_ref, upreb_ref, vp_ref, vec_ref, fcv_ref, wu_ref, wd_ref,
             dff_ref, dup_ref, dx1_ref, red_ref, cred_ref, pred_ref, carry):
        @pl.when(pl.program_id(0) == 0)
        def _():
            carry[...] = jnp.zeros_like(carry)
            red_ref[...] = jnp.zeros_like(red_ref)
            cred_ref[...] = jnp.zeros_like(cred_ref)
            pred_ref[...] = jnp.zeros_like(pred_ref)

        dy_v = dy_ref[...]
        dffb = _postnorm_bwd(dy_v, ff_ref[...].astype(F32), vec_ref[0:1, :], vec_ref[1:2, :], red_ref).astype(BF16)
        dff_ref[...] = dffb

        def conv_bwd(dv, col):
            dx, (t0, t1, t2) = _conv3_bwd(dv, carry, col, _conv_taps(fcv_ref, col), upreb_ref[:, col].astype(F32))
            cred_ref[0:1, col] += t0
            cred_ref[1:2, col] += t1
            cred_ref[2:3, col] += t2
            cred_ref[3:4, col] += _colsum(dv)
            dxb = dx.astype(BF16)
            dup_ref[:, col] = dxb
            return _dot_nt(dxb, wu_ref[:, col])

        dh = None
        for j in range(4):
            gc = slice(j * cw, (j + 1) * cw)
            vc = slice(fp + j * cw, fp + (j + 1) * cw)
            da = _dot_nt(dffb, wd_ref[gc, :])
            gate = upb_ref[:, gc].astype(F32)
            val = upb_ref[:, vc].astype(F32)
            cdf, th = _gelu_parts(gate)
            dcdf = 0.5 * (1.0 - th * th) * (GELU_C0 * (1.0 + (3.0 * GELU_C1) * (gate * gate)))
            part = conv_bwd(da * val * (cdf + gate * dcdf), gc) + conv_bwd(da * (gate * cdf), vc)
            dh = part if dh is None else dh + part

        _, r, nh = _prenorm(x1_ref[...], vp_ref)
        dx1_ref[...] = dy_v + _prenorm_bwd(dh, r, nh, vp_ref, pred_ref)

    rev = lambda n: pl.BlockSpec((tm, n), lambda i: (nt - 1 - i, 0))
    fixed = lambda n: pl.BlockSpec((8, n), lambda i: (0, 0))
    return pl.pallas_call(
        body, name="ffn_block_bwd", grid=(nt,),
        in_specs=[rev(d), rev(d), rev(d), rev(2 * fp), rev(2 * fp)] + [_vmem()] * 5,
        out_specs=[rev(d), rev(2 * fp), rev(d), fixed(d), fixed(2 * fp), fixed(d)],
        out_shape=[_sds((t, d), BF16), _sds((t, 2 * fp), BF16), _sds((t, d), F32), _sds((8, d), F32),
                   _sds((8, 2 * fp), F32), _sds((8, d), F32)],
        scratch_shapes=[pltpu.VMEM((16, 2 * fp), F32)],
        compiler_params=_params("arbitrary"),
    )(dy, ffb, x1, upb, upreb, vec_pre, vec, fcv, w_up, w_down)


def _mixer_block_bwd(dx1, ob, yarb, ybb, cvb, p5b, x, vec_pre, vec, w_in, w_pool, w_bout, w_o, job):
    t, d = dx1.shape
    tm = GROUP
    gw = d // len(POOL_WINDOWS)
    nt = t // tm
    pool_rows = 8 * (POOL_WINDOWS[-1] - 1)

    def body(dx1_ref, o_ref, yar_ref, yb_ref, cv_ref, p5_ref, x_ref, vp_ref, vec_ref, win_ref, wp_ref, wb_ref, wo_ref,
             dyar_ref, dqm_ref, dp_ref, gx_ref, red_ref, pred_ref, dpgcarry, dcvcarry):
        i = pl.program_id(0)
        tix = nt - 1 - i

        @pl.when(i == 0)
        def _():
            red_ref[...] = jnp.zeros_like(red_ref)
            pred_ref[...] = jnp.zeros_like(pred_ref)
            dpgcarry[...] = jnp.zeros_like(dpgcarry)
            dcvcarry[...] = jnp.zeros_like(dcvcarry)

        pscale = vec_ref[2:3, :]
        dx1_v = dx1_ref[...]
        dob = _postnorm_bwd(dx1_v, o_ref[...].astype(F32), vec_ref[0:1, :], vec_ref[1:2, :], red_ref).astype(BF16)
        dqm_ref[:, d:2 * d] = dob
        dm = _dot_nt(dob, wo_ref[...])

        def dproj(cols, value):
            vb = value.astype(BF16)
            dp_ref[:, cols] = vb
            return _dot_nt(vb, win_ref[:, cols])

        sa = jax.nn.sigmoid(p5_ref[:, 3 * d:4 * d].astype(F32))
        yar = yar_ref[...].astype(F32)
        dya = dm * sa
        dh = dproj(slice(4 * d, 5 * d), dm * (yar * pscale) * sa * (1.0 - sa))
        red_ref[2:3, :] += _colsum(dya * yar)
        dyarb = (dya * pscale).astype(BF16)
        dyar_ref[...] = dyarb
        sb = jax.nn.sigmoid(p5_ref[:, 4 * d:5 * d].astype(F32))
        dybb = (dm * sb).astype(BF16)
        dqm_ref[:, 0:d] = dybb
        dh = dh + dproj(slice(5 * d, 6 * d), dm * yb_ref[...].astype(F32) * sb * (1.0 - sb))

        for g, window in enumerate(POOL_WINDOWS):
            cols = slice(g * gw, (g + 1) * gw)
            rows = 8 * (window - 1)
            dpg = _dot_nt(dyarb[:, cols], wp_ref[g])
            dpgs = dpg * _inv_count(tix * tm, window)
            halo = _halo_bottom(dpgs[:rows, :], dpgcarry[:rows, cols])
            dpgcarry[:, cols] = dpgs[:pool_rows, :]
            s, shift = jnp.concatenate([dpgs, halo], axis=0), 1
            while shift < window:
                s = s[:s.shape[0] - 8 * shift, :] + s[8 * shift:, :]
                shift *= 2
            dh = dh + dproj(cols, s - dpg)

        dq = _dot_nt(dybb, wb_ref[...])
        ux = p5_ref[:, 0:d].astype(F32)
        uc = p5_ref[:, 2 * d:3 * d].astype(F32)
        dh = dh + dproj(slice(2 * d, 3 * d), dq * cv_ref[...].astype(F32))
        dcv = dq * p5_ref[:, d:2 * d].astype(F32)
        taps = (vec_ref[4:5, :], vec_ref[5:6, :], vec_ref[6:7, :])
        dpv, (t0, t1, t2) = _conv3_bwd(dcv, dcvcarry, slice(0, d), taps, uc * ux)
        red_ref[3:4, :] += _colsum(dcv)
        red_ref[4:5, :] += t0
        red_ref[5:6, :] += t1
        red_ref[6:7, :] += t2
        dh = dh + dproj(slice(d, 2 * d), dpv * uc)
        dh = dh + dproj(slice(3 * d, 4 * d), dpv * ux)

        _, r, nh = _prenorm(_interleave(x_ref[...]), vp_ref)
        gx_ref[...] = _deinterleave(dx1_v + _prenorm_bwd(dh, r, nh, vp_ref, pred_ref))

    rev = lambda n: pl.BlockSpec((tm, n), lambda i: (nt - 1 - i, 0))
    return _call(
        body, "mixer_block_bwd", (nt,), [rev(d)] * 5 + [rev(5 * d), rev(d)] + [_vmem()] * 6,
        [rev(d), rev(2 * d), rev(6 * d), rev(d), pl.BlockSpec((16, d), lambda i: (0, 0)),
         pl.BlockSpec((8, d), lambda i: (0, 0))],
        [_sds((t, d), BF16), _sds((t, 2 * d), BF16), _sds((t, 6 * d), BF16), _sds((t, d), F32), _sds((16, d), F32),
         _sds((8, d), F32)],
        [pltpu.VMEM((pool_rows, d), F32), pltpu.VMEM((16, d), F32)],
        _params("arbitrary"), (dx1, ob, yarb, ybb, cvb, p5b, x, vec_pre, vec, w_in, w_pool, w_bout, w_o), job)


def _matmul_tn(a, b, bm, bn, tk, by_col_block, name, job=None):
    t, m = a.shape
    n = b.shape[1]
    nk = t // tk

    def body(a_ref, b_ref, o_ref, acc_ref):
        k = pl.program_id(2)

        @pl.when(k == 0)
        def _():
            acc_ref[...] = jnp.zeros_like(acc_ref)

        acc_ref[...] += _dot_tn(a_ref[...], b_ref[...])

        @pl.when(k == nk - 1)
        def _():
            o_ref[...] = acc_ref[...].astype(o_ref.dtype)

    if by_col_block:
        out_shape = _sds((n // bn, m, bn), BF16)
        out_spec = pl.BlockSpec((None, bm, bn), lambda i, j, k: (j, i, 0))
    else:
        out_shape = _sds((m, n), BF16)
        out_spec = pl.BlockSpec((bm, bn), lambda i, j, k: (i, j))
    out = _call(body, name, (m // bm, n // bn, nk),
                [pl.BlockSpec((tk, bm), lambda i, j, k: (k, i)), pl.BlockSpec((tk, bn), lambda i, j, k: (k, j))],
                [out_spec], [out_shape], [pltpu.VMEM((bm, bn), F32)],
                _params("arbitrary", "arbitrary", "arbitrary"), (a, b), job)
    return out if job is not None else out[0]


def _matmul_tn_groups(a, b, groups, tk, name, job=None):
    t, m = a.shape
    w = m // groups
    nk = t // tk

    def body(a_ref, b_ref, o_ref, acc_ref):
        k = pl.program_id(1)

        @pl.when(k == 0)
        def _():
            acc_ref[...] = jnp.zeros_like(acc_ref)

        acc_ref[...] += _dot_tn(a_ref[...], b_ref[...])

        @pl.when(k == nk - 1)
        def _():
            o_ref[...] = acc_ref[...].astype(o_ref.dtype)

    blk = pl.BlockSpec((tk, w), lambda g, k: (k, g))
    out = _call(body, name, (groups, nk), [blk, blk], [pl.BlockSpec((None, w, w), lambda g, k: (g, 0, 0))],
                [_sds((groups, w, w), BF16)], [pltpu.VMEM((w, w), F32)], _params("arbitrary", "arbitrary"), (a, b), job)
    return out if job is not None else out[0]


def _round_up(n, k):
    return (n + k - 1) // k * k


def _pad_last(a, parts, width):
    lead = a.shape[:-1]
    piece = a.shape[-1] // parts
    a = a.reshape(lead + (parts, piece))
    a = jnp.pad(a, [(0, 0)] * len(lead) + [(0, 0), (0, width - piece)])
    return a.reshape(lead + (parts * width,))


def _unpad_last(a, parts, piece):
    lead = a.shape[:-1]
    width = a.shape[-1] // parts
    return a.reshape(lead + (parts, width))[..., :piece].reshape(lead + (parts * piece,))


def _rows8(rows, width):
    n = _round_up(len(rows), 8)
    rows = list(rows) + [jnp.zeros((1, width), F32)] * (n - len(rows))
    return jnp.concatenate(rows, axis=0)


def kernel(x, c, g_pre_mix, g_post_mix, g_pre_ffn, g_post_ffn, w_ada, b_ada, w_in, w_pool, pool_scale, conv_w, conv_b, w_bout, w_o, w_up, ffn_conv_w, ffn_conv_b, w_down, loss_target, m_g_pre_mix, m_g_post_mix, m_g_pre_ffn, m_g_post_ffn, m_w_ada, m_b_ada, m_w_in, m_w_pool, m_pool_scale, m_conv_w, m_conv_b, m_w_bout, m_w_o, m_w_up, m_ffn_conv_w, m_ffn_conv_b, m_w_down, v_g_pre_mix, v_g_post_mix, v_g_pre_ffn, v_g_post_ffn, v_w_ada, v_b_ada, v_w_in, v_w_pool, v_pool_scale, v_conv_w, v_conv_b, v_w_bout, v_w_o, v_w_up, v_ffn_conv_w, v_ffn_conv_b, v_w_down):
    t, d = x.shape[1], x.shape[2]
    ngroups = len(POOL_WINDOWS)
    gw = d // ngroups
    ada_n = w_ada.shape[2]
    in_n = w_in.shape[2]
    unit = w_down.shape[1]
    unit_p = _round_up(unit, LANES)
    fp = NDEV * unit_p

    xi, yi, ci = _position()
    me = _linear(xi, yi, ci)
    chip = 2 * xi + yi
    core = jnp.reshape(ci, (1,)).astype(jnp.int32)
    sel = jnp.stack([2 * chip + ci, chip]).astype(jnp.int32)

    x2 = x.reshape(t, d)
    target = loss_target.reshape(t, d)

    cw_n = conv_w.shape[2]
    fcw_p = _pad_last(ffn_conv_w[0], 2, unit_p)
    pack = jnp.concatenate([c.reshape(1, d), conv_w[0].reshape(1, 3 * cw_n), fcw_p.reshape(1, 6 * unit_p)], axis=1)
    pack = jnp.pad(pack, ((0, 0), (0, _round_up(pack.shape[1], LANES) - pack.shape[1])))
    b_piece = lax.dynamic_slice_in_dim(b_ada, me * ada_n, ada_n, axis=1)
    mixer_weights = _allgather_job(
        [w_in[0].astype(BF16), w_bout[0].astype(BF16), w_o[0].astype(BF16), w_pool[0].astype(BF16)],
        ["cols", "rows", "rows", "mid"], 0.5)
    gathered, mod_rows, w_in_f, g_bout, g_o, w_pool_f = _gather_weights_and_modulation(
        mixer_weights, pack, w_ada[0], b_piece, d)
    w_bout_f = g_bout.reshape(d, d)
    w_o_f = g_o.reshape(d, d)
    c_all = gathered[:, :d]
    c16 = jnp.pad(c_all, ((0, 8), (0, 0))).astype(BF16)
    conv_w_full = gathered[:, d:d + 3 * cw_n].reshape(NDEV, 3, cw_n).transpose(1, 0, 2).reshape(3, NDEV * cw_n)
    fcw_full = gathered[:, d + 3 * cw_n:d + 3 * cw_n + 6 * unit_p].reshape(NDEV, 3, 2 * unit_p)
    fcw_full = fcw_full.transpose(1, 0, 2).reshape(3, 2 * fp)
    fcb_full = _pad_last(ffn_conv_b, 2 * NDEV, unit_p)
    fcv = jnp.concatenate([fcw_full, fcb_full, jnp.zeros((4, 2 * fp), F32)], axis=0)
    mod = mod_rows.reshape(1, NDEV * ada_n)
    sh1, sc1, gt1, sh2, sc2, gt2 = [mod[:, k * d:(k + 1) * d] for k in range(6)]

    ffn_weights = _allgather_job(
        [_pad_last(w_up[0], 2, unit_p).astype(BF16), jnp.pad(w_down[0], ((0, unit_p - unit), (0, 0))).astype(BF16)],
        ["cols", "rows"], 0.75)
    vec_pre_mix = _rows8([g_pre_mix, 1.0 + sc1, sh1], d)
    vec_mix = _rows8([gt1, g_post_mix, pool_scale, conv_b, conv_w_full[0:1], conv_w_full[1:2], conv_w_full[2:3]], d)
    h1b, p5b, pgb, qmb, cvb, yarb, ybb, ob, x1, w_up_f, g_down = _mixer_block_fwd(
        x2, vec_pre_mix, vec_mix, w_in_f, w_pool_f, w_bout_f, w_o_f, ffn_weights)
    w_down_f = g_down.reshape(fp, d)
    vec_pre_ffn = _rows8([g_pre_ffn, 1.0 + sc2, sh2], d)
    vec_ffn = _rows8([gt2, g_post_ffn], d)
    h2b, upb, upreb, ab, ffb, dy, loss_part = _ffn_block_fwd(x1, target, vec_pre_ffn, vec_ffn, fcv, w_up_f, w_down_f)

    tk = min(2048, t)
    chip_sum = lambda gs, ss: [_chip_sum(g, s, core) for g, s in zip(gs, ss)]
    dffb, dupre, dx1, red_ffn, red_fconv, red_pre_ffn = _ffn_block_bwd(
        dy, ffb, x1, upb, upreb, vec_pre_ffn, vec_ffn, fcv, w_up_f, w_down_f)
    gr_up = _matmul_tn(h2b, dupre, d, 2 * unit_p, tk, True, "wgrad_up")
    gr_down = _matmul_tn(ab, dffb, 2 * unit_p, d, tk, False, "wgrad_down").reshape(NDEV, unit_p, d)
    sib_ffn = _run_job(_sibling_job([gr_up, gr_down]), "rs_sibling_ffn")
    dyarb, dqmb, dproj, grad_x, red_mix, red_pre_mix = _mixer_block_bwd(
        dx1, ob, yarb, ybb, cvb, p5b, x2, vec_pre_mix, vec_mix, w_in_f, w_pool_f, w_bout_f, w_o_f, None)
    gr_in, fc_up, fc_down = _matmul_tn(h1b, dproj, d, in_n, tk, True, "wgrad_in",
                                       _chips_job(chip_sum([gr_up, gr_down], sib_ffn)))
    sib_in = _run_job(_sibling_job([gr_in]), "rs_sibling_in")
    gr_qm, fc_in = _matmul_tn_groups(qmb, dqmb, 2, tk, "wgrad_bout_o", _chips_job(chip_sum([gr_in], sib_in)))
    gr_bout = gr_qm[0].reshape(NDEV, d // NDEV, d)
    gr_o = gr_qm[1].reshape(NDEV, d // NDEV, d)
    gr_pool = _matmul_tn_groups(pgb, dyarb, ngroups, min(4096, t), "wgrad_pool")
    gr_pool = gr_pool.reshape(ngroups, NDEV, gw // NDEV, gw).transpose(1, 0, 2, 3).reshape(NDEV, -1, gw)
    rest = [gr_bout, gr_o, gr_pool]
    sib_rest = _run_job(_sibling_job(rest), "rs_sibling_rest")

    dmod = [red_pre_mix[0:1], red_pre_mix[1:2], red_mix[1:2], red_pre_ffn[0:1], red_pre_ffn[1:2], red_ffn[1:2]]
    small = [red_pre_mix[2:3], red_mix[0:1], red_pre_ffn[2:3], red_ffn[0:1], red_mix[2:3], red_mix[3:4],
             red_mix[4:5], red_mix[5:6], red_mix[6:7]] + dmod
    flat = jnp.concatenate(small + [red_fconv[0:4].reshape(1, 8 * fp), loss_part[0:1, 0:1]], axis=1)
    flat_n = flat.shape[1]
    width = 8 * LANES
    rows = _round_up(-(-flat_n // width), 8)
    flat = jnp.pad(flat, ((0, 0), (0, rows * width - flat_n))).reshape(rows, width)
    gat, tot, *fc_rest = _small_allreduce(flat, "allreduce_small_rs_rest", _chips_job(chip_sum(rest, sib_rest)))

    def big(grad, from_sibling, from_chips, w, m, v, name, col_parts=1):
        shape = w.shape
        w2, m2, v2 = [a.reshape((-1, shape[-1])) for a in (w, m, v)]
        outs = _reduce_adamw(grad, from_sibling, from_chips, sel, w2, m2, v2, name, col_parts)
        return [a.reshape(shape) for a in outs]

    g_w_up, d_w_up, nm_w_up, nv_w_up = big(gr_up, sib_ffn[0], fc_up, w_up, m_w_up, v_w_up, "adamw_up", col_parts=2)
    g_w_down, d_w_down, nm_w_down, nv_w_down = big(gr_down, sib_ffn[1], fc_down, w_down, m_w_down, v_w_down, "adamw_down")
    g_w_in, d_w_in, nm_w_in, nv_w_in = big(gr_in, sib_in[0], fc_in, w_in, m_w_in, v_w_in, "adamw_in")
    g_w_bout, d_w_bout, nm_w_bout, nv_w_bout = big(gr_bout, sib_rest[0], fc_rest[0], w_bout, m_w_bout, v_w_bout, "adamw_bout")
    g_w_o, d_w_o, nm_w_o, nv_w_o = big(gr_o, sib_rest[1], fc_rest[1], w_o, m_w_o, v_w_o, "adamw_o")
    g_w_pool, d_w_pool, nm_w_pool, nv_w_pool = big(gr_pool, sib_rest[2], fc_rest[2], w_pool, m_w_pool, v_w_pool, "adamw_pool")

    tot = tot.reshape(1, rows * width)
    gat = gat.reshape(NDEV, rows * width)
    take = lambda k: tot[:, k * d:(k + 1) * d]
    g_g_pre_mix, g_g_post_mix, g_g_pre_ffn, g_g_post_ffn, g_pool_scale, g_conv_b = [take(k) for k in range(6)]
    g_conv_w_full = jnp.concatenate([take(6), take(7), take(8)], axis=0)
    g_conv_w = lax.dynamic_slice_in_dim(g_conv_w_full, me * cw_n, cw_n, axis=1)
    g_b_ada = tot[:, 9 * d:15 * d]
    dmod_all = gat[:, 9 * d:15 * d]
    fconv_tot = tot[:, 15 * d:15 * d + 8 * fp].reshape(4, 2 * fp)
    loss = 0.5 * tot[0, 15 * d + 8 * fp]
    g_ffn_conv_b = _unpad_last(fconv_tot[3:4], 2 * NDEV, unit)
    g_fcw_mine = lax.dynamic_slice_in_dim(fconv_tot[0:3], me * 2 * unit_p, 2 * unit_p, axis=1)
    g_ffn_conv_w = _unpad_last(g_fcw_mine, 2, unit)
    dmod_piece = lax.dynamic_slice_in_dim(dmod_all, me * ada_n, ada_n, axis=1)
    g_w_ada = _wada_grad(c16, jnp.pad(dmod_piece, ((0, 8), (0, 0))).astype(BF16))

    names_small = [(g_pre_mix, g_g_pre_mix, m_g_pre_mix, v_g_pre_mix), (g_post_mix, g_g_post_mix, m_g_post_mix, v_g_post_mix),
                   (g_pre_ffn, g_g_pre_ffn, m_g_pre_ffn, v_g_pre_ffn), (g_post_ffn, g_g_post_ffn, m_g_post_ffn, v_g_post_ffn),
                   (b_ada, g_b_ada, m_b_ada, v_b_ada), (pool_scale, g_pool_scale, m_pool_scale, v_pool_scale),
                   (conv_w, g_conv_w, m_conv_w, v_conv_w), (conv_b, g_conv_b, m_conv_b, v_conv_b),
                   (ffn_conv_w, g_ffn_conv_w, m_ffn_conv_w, v_ffn_conv_w), (ffn_conv_b, g_ffn_conv_b, m_ffn_conv_b, v_ffn_conv_b)]
    sizes = [w.size for w, _, _, _ in names_small]
    total = sum(sizes)
    prow = _round_up(-(-total // width), 8)

    def pack_small(k):
        a = jnp.concatenate([q[k].reshape(1, -1) for q in names_small], axis=1)
        return jnp.pad(a, ((0, 0), (0, prow * width - total)), constant_values=1.0).reshape(prow, width)

    ds, ms, vs = _adamw(pack_small(0), pack_small(1), pack_small(2), pack_small(3), "adamw_small")

    def unpack_small(a):
        a = a.reshape(-1)
        out, off = [], 0
        for (w, _, _, _), n in zip(names_small, sizes):
            out.append(a[off:off + n].reshape(w.shape))
            off += n
        return out

    (d_g_pre_mix, d_g_post_mix, d_g_pre_ffn, d_g_post_ffn, d_b_ada, d_pool_scale, d_conv_w, d_conv_b,
     d_ffn_conv_w, d_ffn_conv_b) = unpack_small(ds)
    (nm_g_pre_mix, nm_g_post_mix, nm_g_pre_ffn, nm_g_post_ffn, nm_b_ada, nm_pool_scale, nm_conv_w, nm_conv_b,
     nm_ffn_conv_w, nm_ffn_conv_b) = unpack_small(ms)
    (nv_g_pre_mix, nv_g_post_mix, nv_g_pre_ffn, nv_g_post_ffn, nv_b_ada, nv_pool_scale, nv_conv_w, nv_conv_b,
     nv_ffn_conv_w, nv_ffn_conv_b) = unpack_small(vs)
    d_w_ada, nm_w_ada, nv_w_ada = [a.reshape(w_ada.shape) for a in
                                   _adamw(w_ada[0], g_w_ada, m_w_ada[0], v_w_ada[0], "adamw_ada")]

    grads = [g_g_pre_mix, g_g_post_mix, g_g_pre_ffn, g_g_post_ffn, g_w_ada.reshape(w_ada.shape), g_b_ada, g_w_in,
             g_w_pool, g_pool_scale, g_conv_w.reshape(conv_w.shape), g_conv_b, g_w_bout, g_w_o, g_w_up,
             g_ffn_conv_w.reshape(ffn_conv_w.shape), g_ffn_conv_b, g_w_down]
    deltas = [d_g_pre_mix, d_g_post_mix, d_g_pre_ffn, d_g_post_ffn, d_w_ada, d_b_ada, d_w_in, d_w_pool, d_pool_scale,
              d_conv_w, d_conv_b, d_w_bout, d_w_o, d_w_up, d_ffn_conv_w, d_ffn_conv_b, d_w_down]
    new_m = [nm_g_pre_mix, nm_g_post_mix, nm_g_pre_ffn, nm_g_post_ffn, nm_w_ada, nm_b_ada, nm_w_in, nm_w_pool,
             nm_pool_scale, nm_conv_w, nm_conv_b, nm_w_bout, nm_w_o, nm_w_up, nm_ffn_conv_w, nm_ffn_conv_b, nm_w_down]
    new_v = [nv_g_pre_mix, nv_g_post_mix, nv_g_pre_ffn, nv_g_post_ffn, nv_w_ada, nv_b_ada, nv_w_in, nv_w_pool,
             nv_pool_scale, nv_conv_w, nv_conv_b, nv_w_bout, nv_w_o, nv_w_up, nv_ffn_conv_w, nv_ffn_conv_b, nv_w_down]
    return (loss, grad_x.reshape(x.shape), *grads, *deltas, *new_m, *new_v)
```

```python
import math

import jax
import jax.numpy as jnp
from jax import lax
from jax.experimental import pallas as pl
from jax.experimental.pallas import tpu as pltpu

F32 = jnp.float32
BF16 = jnp.bfloat16
MESH = pl.DeviceIdType.MESH

NDEV = 8
NCHIP = 4
EPS = 1e-6
POOL_WINDOWS = (2, 4, 8, 16)
LANES = 128
ADAM_LR = 0.001
ADAM_B1 = 0.9
ADAM_B2 = 0.999
ADAM_EPS = 1e-08
ADAM_WD = 0.01
ADAM_STEP = 10
GELU_C0 = math.sqrt(2.0 / math.pi)
GELU_C1 = 0.044715
VMEM_LIMIT = 56 * 2**20


def _vmem():
    return pl.BlockSpec(memory_space=pltpu.VMEM)


def _any():
    return pl.BlockSpec(memory_space=pl.ANY)


def _params(*sem):
    return pltpu.CompilerParams(dimension_semantics=sem, vmem_limit_bytes=VMEM_LIMIT)


def _sds(shape, dtype):
    return jax.ShapeDtypeStruct(tuple(shape), dtype)


def _position():
    return lax.axis_index("x"), lax.axis_index("y"), lax.axis_index("c")


def _linear(x, y, c):
    return 4 * x + 2 * y + c


def _dot(a, b):
    return jnp.dot(a, b, preferred_element_type=F32)


def _dot_nt(a, b):
    return lax.dot_general(a, b, (((1,), (1,)), ((), ())), preferred_element_type=F32)


def _dot_tn(a, b):
    return lax.dot_general(a, b, (((0,), (0,)), ((), ())), preferred_element_type=F32)


def _colsum(v):
    return jnp.sum(v, axis=0, keepdims=True)


def _rowmean(v):
    return jnp.mean(v, axis=-1, keepdims=True)


GROUP = 256


def _interleave(v):
    g, n = v.shape
    return jnp.swapaxes(v.reshape(8, g // 8, n), 0, 1).reshape(g, n)


def _deinterleave(v):
    g, n = v.shape
    return jnp.swapaxes(v.reshape(g // 8, 8, n), 0, 1).reshape(g, n)


def _halo_top(cur_last, prev_last):
    rows, n = cur_last.shape
    c3 = cur_last.reshape(rows // 8, 8, n)
    p3 = prev_last.reshape(rows // 8, 8, n)
    sub = lax.broadcasted_iota(jnp.int32, c3.shape, 1)
    return jnp.where(sub == 0, pltpu.roll(p3, 1, 1), pltpu.roll(c3, 1, 1)).reshape(rows, n)


def _halo_bottom(cur_first, next_first):
    rows, n = cur_first.shape
    c3 = cur_first.reshape(rows // 8, 8, n)
    n3 = next_first.reshape(rows // 8, 8, n)
    sub = lax.broadcasted_iota(jnp.int32, c3.shape, 1)
    return jnp.where(sub == 7, pltpu.roll(n3, 7, 1), pltpu.roll(c3, 7, 1)).reshape(rows, n)


def _shift_down(v, halo, k):
    rows = v.shape[0]
    return jnp.concatenate([halo[halo.shape[0] - 8 * k:, :], v[:rows - 8 * k, :]], axis=0)


def _shift_up(v, halo, k):
    return jnp.concatenate([v[8 * k:, :], halo[:8 * k, :]], axis=0)


def _inv_count(first_token, window):
    row = lax.broadcasted_iota(jnp.int32, (GROUP, 1), 0)
    t = first_token + (row % 8) * (GROUP // 8) + row // 8
    return 1.0 / jnp.minimum(t + 1, window).astype(F32)


class _Job:
    def __init__(self, inputs, out_shape, scratch, phases):
        self.inputs, self.out_shape, self.scratch, self.phases = list(inputs), list(out_shape), list(scratch), phases


def _call(body, name, grid, in_specs, out_specs, out_shape, scratch_shapes, params, operands, job=None):
    if job is None:
        return pl.pallas_call(body, name=name, grid=grid, in_specs=in_specs, out_specs=out_specs, out_shape=out_shape,
                              scratch_shapes=scratch_shapes, compiler_params=params)(*operands)
    n_in, n_out, n_scr = len(in_specs), len(out_specs), len(scratch_shapes)
    j_in, j_out = len(job.inputs), len(job.out_shape)
    steps = math.prod(grid)

    def hosted(*refs):
        own_in, refs = refs[:n_in], refs[n_in:]
        jin, refs = refs[:j_in], refs[j_in:]
        own_out, refs = refs[:n_out], refs[n_out:]
        jout, refs = refs[:j_out], refs[j_out:]
        own_scr, jscr = refs[:n_scr], refs[n_scr:]
        step = pl.program_id(0)
        for axis in range(1, len(grid)):
            step = step * grid[axis] + pl.program_id(axis)
        for frac, fn in job.phases[:-1]:
            pl.when(step == int(frac * (steps - 1)))(lambda fn=fn: fn(jin, jout, jscr))
        body(*own_in, *own_out, *own_scr)
        pl.when(step == steps - 1)(lambda: job.phases[-1][1](jin, jout, jscr))

    return pl.pallas_call(
        hosted, name=name, grid=grid, in_specs=list(in_specs) + [_any()] * j_in,
        out_specs=list(out_specs) + [_any()] * j_out, out_shape=list(out_shape) + job.out_shape,
        scratch_shapes=list(scratch_shapes) + job.scratch, compiler_params=params)(*operands, *job.inputs)


def _run_job(job, name):
    n_in, n_out = len(job.inputs), len(job.out_shape)

    def body(*refs):
        for _, fn in job.phases:
            fn(refs[:n_in], refs[n_in:n_in + n_out], refs[n_in + n_out:])

    return pl.pallas_call(body, name=name, out_shape=job.out_shape, in_specs=[_any()] * n_in,
                          out_specs=[_any()] * n_out, scratch_shapes=job.scratch)(*job.inputs)


def _peers(x, y, c):
    out = []
    for k in range(1, NDEV):
        out.append(((1 - x) if k & 4 else x, (1 - y) if k & 2 else y, (1 - c) if k & 1 else c))
    return out


def _small_allreduce(v, name, job):
    r, n = v.shape
    j_in, j_out = len(job.inputs), len(job.out_shape)

    def body(v_ref, *rest):
        jin, rest = rest[:j_in], rest[j_in:]
        gat_ref, sum_ref = rest[:2]
        jout, rest = rest[2:2 + j_out], rest[2 + j_out:]
        send_sems, recv_sems, local_sem = rest[:3]
        jscr = rest[3:]
        job.phases[0][1](jin, jout, jscr)
        x, y, c = _position()
        me = _linear(x, y, c)
        mine = pltpu.make_async_copy(v_ref, gat_ref.at[me], local_sem)
        mine.start()
        peers = _peers(x, y, c)
        sends = []
        for k, peer in enumerate(peers):
            cp = pltpu.make_async_remote_copy(src_ref=v_ref, dst_ref=gat_ref.at[me], send_sem=send_sems.at[k],
                                              recv_sem=recv_sems.at[k], device_id=peer, device_id_type=MESH)
            cp.start()
            sends.append(cp)
        for k, peer in enumerate(peers):
            pltpu.make_async_remote_copy(src_ref=v_ref, dst_ref=gat_ref.at[_linear(*peer)], send_sem=send_sems.at[k],
                                         recv_sem=recv_sems.at[k], device_id=peer, device_id_type=MESH).wait_recv()
        for cp in sends:
            cp.wait_send()
        mine.wait()
        acc = gat_ref[0]
        for j in range(1, NDEV):
            acc = acc + gat_ref[j]
        sum_ref[...] = acc
        job.phases[-1][1](jin, jout, jscr)

    return pl.pallas_call(
        body, name=name, out_shape=[_sds((NDEV, r, n), F32), _sds((r, n), F32)] + job.out_shape,
        in_specs=[_vmem()] + [_any()] * j_in, out_specs=[_vmem()] * 2 + [_any()] * j_out,
        scratch_shapes=[pltpu.SemaphoreType.DMA((NDEV - 1,)), pltpu.SemaphoreType.DMA((NDEV - 1,)),
                        pltpu.SemaphoreType.DMA(())] + job.scratch,
    )(v, *job.inputs)


def _exchange_rows(src_for, dst_ref, sems):
    send_sems, recv_sems, local_sem = sems
    x, y, c = _position()
    me = _linear(x, y, c)
    row = lambda j: dst_ref.at[pl.ds(j, 1), :]
    mine = pltpu.make_async_copy(src_for(me), row(me), local_sem)
    mine.start()
    peers = _peers(x, y, c)
    sends = []
    for k, peer in enumerate(peers):
        cp = pltpu.make_async_remote_copy(src_ref=src_for(_linear(*peer)), dst_ref=row(me), send_sem=send_sems.at[k],
                                          recv_sem=recv_sems.at[k], device_id=peer, device_id_type=MESH)
        cp.start()
        sends.append(cp)
    for k, peer in enumerate(peers):
        pltpu.make_async_remote_copy(src_ref=src_for(me), dst_ref=row(_linear(*peer)), send_sem=send_sems.at[k],
                                     recv_sem=recv_sems.at[k], device_id=peer, device_id_type=MESH).wait_recv()
    for cp in sends:
        cp.wait_send()
    mine.wait()


def _gather_weights_and_modulation(job, pack, w_ada, b_piece, d):
    n = pack.shape[1]
    m = w_ada.shape[1]
    j_in, j_out = len(job.inputs), len(job.out_shape)
    row_sems = [pltpu.SemaphoreType.DMA((NDEV - 1,)), pltpu.SemaphoreType.DMA((NDEV - 1,)), pltpu.SemaphoreType.DMA(())]

    def body(pack_ref, wada_ref, bp_ref, *rest):
        jin, rest = rest[:j_in], rest[j_in:]
        gat_ref, mod_ref = rest[:2]
        jout, rest = rest[2:2 + j_out], rest[2 + j_out:]
        sems1, sems2, piece, jscr = rest[0:3], rest[3:6], rest[6], rest[7:]
        start, forward, finish = [fn for _, fn in job.phases]
        start(jin, jout, jscr)
        _exchange_rows(lambda j: pack_ref, gat_ref, sems1)
        c16 = jnp.concatenate([gat_ref[:, 0:d], jnp.zeros((NDEV, d), F32)], axis=0).astype(BF16)
        piece[...] = (_dot(c16, wada_ref[...].astype(BF16)) + bp_ref[...])[0:NDEV, :]
        _exchange_rows(lambda j: piece.at[pl.ds(j, 1), :], mod_ref, sems2)
        forward(jin, jout, jscr)
        finish(jin, jout, jscr)

    return pl.pallas_call(
        body, name="gather_weights_and_modulation",
        out_shape=[_sds((NDEV, n), F32), _sds((NDEV, m), F32)] + job.out_shape,
        in_specs=[_vmem()] * 3 + [_any()] * j_in, out_specs=[_vmem()] * 2 + [_any()] * j_out,
        scratch_shapes=row_sems + row_sems + [pltpu.VMEM((NDEV, m), F32)] + job.scratch,
    )(pack, w_ada, b_piece, *job.inputs)


def _gathered(shard, layout):
    if layout == "rows":
        return (NDEV,) + shard.shape, lambda ref, j: ref.at[j]
    if layout == "cols":
        r, c = shard.shape
        return (r, NDEV * c), lambda ref, j: ref.at[:, pl.ds(pl.multiple_of(j * c, LANES), c)]
    g, r, c = shard.shape
    return (g, NDEV * r, c), lambda ref, j: ref.at[:, pl.ds(pl.multiple_of(j * r, 16), r), :]


def _allgather_job(shards, layouts, forward_at):
    n = len(shards)
    specs = [_gathered(s, l) for s, l in zip(shards, layouts)]

    def copies(src, dst, sems):
        send_sems, recv_sems, _ = sems
        x, y, c = _position()
        me, sibling = (x, y, c), (x, y, 1 - c)
        chips = [(1 - x, y), (x, 1 - y), (1 - x, 1 - y)]

        def copy(a, k, block, to, from_src=False):
            blk = specs[a][1](dst[a], _linear(*block))
            return pltpu.make_async_remote_copy(src_ref=src[a] if from_src else blk, dst_ref=blk,
                                                send_sem=send_sems.at[a, k], recv_sem=recv_sems.at[a, k],
                                                device_id=to, device_id_type=MESH)
        return copy, me, sibling, chips, c

    def local(src, dst, sems):
        x, y, c = _position()
        return [pltpu.make_async_copy(src[a], specs[a][1](dst[a], _linear(x, y, c)), sems[2].at[a]) for a in range(n)]

    def first(src, dst, sems):
        copy, me, sibling, chips, c = copies(src, dst, sems)
        out = [copy(a, 1 + j, me, (*chip, c), from_src=True) for j, chip in enumerate(chips) for a in range(n)]
        return out + [copy(a, 0, me, sibling, from_src=True) for a in range(n)]

    def passed(src, dst, sems):
        copy, me, sibling, chips, c = copies(src, dst, sems)
        return [copy(a, 4 + j, (*chip, c), sibling) for j, chip in enumerate(chips) for a in range(n)]

    def start(src, dst, sems):
        for cp in local(src, dst, sems) + first(src, dst, sems):
            cp.start()

    def forward(src, dst, sems):
        copy, me, sibling, chips, c = copies(src, dst, sems)
        for j, chip in enumerate(chips):
            for a in range(n):
                copy(a, 1 + j, (*chip, c), me).wait_recv()
        for cp in passed(src, dst, sems):
            cp.start()

    def finish(src, dst, sems):
        copy, me, sibling, chips, c = copies(src, dst, sems)
        for a in range(n):
            copy(a, 0, sibling, me).wait_recv()
        for j, chip in enumerate(chips):
            for a in range(n):
                copy(a, 4 + j, (*chip, 1 - c), me).wait_recv()
        for cp in first(src, dst, sems) + passed(src, dst, sems):
            cp.wait_send()
        for cp in local(src, dst, sems):
            cp.wait()

    return _Job(shards, [_sds(spec[0], s.dtype) for spec, s in zip(specs, shards)],
                [pltpu.SemaphoreType.DMA((n, 7)), pltpu.SemaphoreType.DMA((n, 7)), pltpu.SemaphoreType.DMA((n,))],
                [(0.0, start), (forward_at, forward), (1.0, finish)])


def _sibling_job(grads):
    n = len(grads)

    def copies(src, dst, sems):
        x, y, c = _position()
        return [pltpu.make_async_remote_copy(src_ref=src[a].at[2 * q + 1 - c], dst_ref=dst[a].at[q],
                                             send_sem=sems[0].at[a, q], recv_sem=sems[1].at[a, q],
                                             device_id=(x, y, 1 - c), device_id_type=MESH)
                for a in range(n) for q in range(NCHIP)]

    return _exchange_job(grads, NCHIP, copies)


def _chips_job(chip_sums):
    n = len(chip_sums)

    def copies(src, dst, sems):
        x, y, c = _position()
        chips = [(1 - x, y), (x, 1 - y), (1 - x, 1 - y)]
        return [pltpu.make_async_remote_copy(src_ref=src[a].at[2 * chip[0] + chip[1]], dst_ref=dst[a].at[j],
                                             send_sem=sems[0].at[a, j], recv_sem=sems[1].at[a, j],
                                             device_id=(*chip, c), device_id_type=MESH)
                for j, chip in enumerate(chips) for a in range(n)]

    return _exchange_job(chip_sums, 3, copies)


def _exchange_job(arrays, slots, copies):
    n = len(arrays)

    def start(src, dst, sems):
        for cp in copies(src, dst, sems):
            cp.start()

    def finish(src, dst, sems):
        cps = copies(src, dst, sems)
        for cp in cps:
            cp.wait_recv()
        for cp in cps:
            cp.wait_send()

    return _Job(arrays, [_sds((slots,) + a.shape[1:], a.dtype) for a in arrays],
                [pltpu.SemaphoreType.DMA((n, slots)), pltpu.SemaphoreType.DMA((n, slots))],
                [(0.0, start), (1.0, finish)])


def _row_block(r):
    for rb in (512, 256, 128, 64, 32, 16):
        if r % rb == 0:
            return rb
    return r


def _chip_sum(grad, from_sibling, core):
    _, r, c = grad.shape
    rb = _row_block(r)

    def body(core_ref, g_ref, s_ref, o_ref):
        del core_ref
        o_ref[...] = (g_ref[...].astype(F32) + s_ref[...].astype(F32)).astype(o_ref.dtype)

    grid_spec = pltpu.PrefetchScalarGridSpec(
        num_scalar_prefetch=1, grid=(NCHIP, r // rb),
        in_specs=[pl.BlockSpec((None, rb, c), lambda q, i, core: (2 * q + core[0], i, 0)),
                  pl.BlockSpec((None, rb, c), lambda q, i, core: (q, i, 0))],
        out_specs=pl.BlockSpec((None, rb, c), lambda q, i, core: (q, i, 0)))
    return pl.pallas_call(body, name="rs_chip_sum", grid_spec=grid_spec, out_shape=_sds((NCHIP, r, c), BF16),
                          compiler_params=_params("parallel", "parallel"))(core, grad, from_sibling)


def _adamw_math(w, g, m, v):
    m2 = ADAM_B1 * m + (1.0 - ADAM_B1) * g
    v2 = ADAM_B2 * v + (1.0 - ADAM_B2) * jnp.square(g)
    m_hat = m2 / (1.0 - ADAM_B1 ** ADAM_STEP)
    v_hat = v2 / (1.0 - ADAM_B2 ** ADAM_STEP)
    delta = -ADAM_LR * (m_hat / (jnp.sqrt(v_hat) + ADAM_EPS) + ADAM_WD * w)
    return delta, m2, v2


def _adamw(w, g, m, v, name):
    r, c = w.shape
    rb = _row_block(r)

    def body(w_ref, g_ref, m_ref, v_ref, d_ref, m2_ref, v2_ref):
        d, m2, v2 = _adamw_math(w_ref[...], g_ref[...], m_ref[...], v_ref[...])
        d_ref[...] = d
        m2_ref[...] = m2
        v2_ref[...] = v2

    blk = pl.BlockSpec((rb, c), lambda i: (i, 0))
    return pl.pallas_call(body, name=name, grid=(r // rb,), in_specs=[blk] * 4, out_specs=[blk] * 3,
                          out_shape=[_sds((r, c), F32)] * 3, compiler_params=_params("parallel"))(w, g, m, v)


def _reduce_adamw(grad, from_sibling, from_chips, sel, w, m, v, name):
    r, c = w.shape
    cp = c
    rb = _row_block(r)

    def body(sel_ref, g_ref, s_ref, c0_ref, c1_ref, c2_ref, w_ref, m_ref, v_ref, go_ref, d_ref, m2_ref, v2_ref):
        del sel_ref
        g = g_ref[...].astype(F32) + s_ref[...].astype(F32)
        g = g + c0_ref[...].astype(F32)
        g = g + c1_ref[...].astype(F32)
        g = g + c2_ref[...].astype(F32)
        d, m2, v2 = _adamw_math(w_ref[...], g, m_ref[...], v_ref[...])
        go_ref[...] = g
        d_ref[...] = d
        m2_ref[...] = m2
        v2_ref[...] = v2

    blk = pl.BlockSpec((rb, c), lambda i, sel: (i, 0))
    grid_spec = pltpu.PrefetchScalarGridSpec(
        num_scalar_prefetch=1, grid=(r // rb,),
        in_specs=[pl.BlockSpec((None, rb, cp), lambda i, sel: (sel[0], i, 0)),
                  pl.BlockSpec((None, rb, cp), lambda i, sel: (sel[1], i, 0)),
                  pl.BlockSpec((None, rb, cp), lambda i, sel: (0, i, 0)),
                  pl.BlockSpec((None, rb, cp), lambda i, sel: (1, i, 0)),
                  pl.BlockSpec((None, rb, cp), lambda i, sel: (2, i, 0)),
                  blk, blk, blk],
        out_specs=[blk] * 4)
    return pl.pallas_call(body, name=name, grid_spec=grid_spec, out_shape=[_sds((r, c), F32)] * 4,
                          compiler_params=_params("parallel"))(sel, grad, from_sibling, from_chips, from_chips,
                                                               from_chips, w, m, v)


def _wada_grad(c_all, dmod_piece):
    d = c_all.shape[1]
    n = dmod_piece.shape[1]

    def body(c_ref, dm_ref, o_ref):
        o_ref[...] = _dot_tn(c_ref[...], dm_ref[...])

    return pl.pallas_call(body, name="ada_wgrad", out_shape=_sds((d, n), F32),
                          in_specs=[_vmem()] * 2, out_specs=_vmem())(c_all, dmod_piece)


def _column_chunks(width):
    for n in (4, 2):
        if width % (n * LANES) == 0:
            return n
    return 1


def _conv_taps(ref, col):
    return ref[0:1, col], ref[1:2, col], ref[2:3, col]


def _gelu_parts(u):
    th = jnp.tanh(GELU_C0 * (u + GELU_C1 * (u * u * u)))
    cdf = 0.5 * (1.0 + th)
    return cdf, th


def _conv3_bwd(dv, carry, col, taps, x):
    halo = _halo_bottom(dv[:16, :], carry[:, col])
    carry[:, col] = dv[:16, :]
    d1 = _shift_up(dv, halo, 1)
    d2 = _shift_up(dv, halo, 2)
    w0, w1, w2 = taps
    dx = w2 * dv
    dx = dx + w1 * d1
    dx = dx + w0 * d2
    return dx, (_colsum(d2 * x), _colsum(d1 * x), _colsum(dv * x))


def _prenorm(x, vp_ref):
    r = lax.rsqrt(_rowmean(x * x) + EPS)
    nh = x * r
    return (nh * vp_ref[0:1, :]) * vp_ref[1:2, :] + vp_ref[2:3, :], r, nh


def _prenorm_bwd(dh, r, nh, vp_ref, red_ref):
    g, sc1 = vp_ref[0:1, :], vp_ref[1:2, :]
    red_ref[0:1, :] += _colsum(dh)
    red_ref[1:2, :] += _colsum(dh * (nh * g))
    red_ref[2:3, :] += _colsum(dh * nh * sc1)
    dnh = dh * g * sc1
    return r * (dnh - nh * _rowmean(dnh * nh))


def _postnorm_bwd(dres, z, gate, gpost, red_ref):
    r = lax.rsqrt(_rowmean(z * z) + EPS)
    nh = z * r
    dn = dres * gate
    red_ref[0:1, :] += _colsum(dn * nh)
    red_ref[1:2, :] += _colsum(dres * (nh * gpost))
    dnh = dn * gpost
    return r * (dnh - nh * _rowmean(dnh * nh))


def _mixer_block_fwd(x, vec_pre, vec, w_in, w_pool, w_bout, w_o, job):
    t, d = x.shape
    tm = GROUP
    gw = d // len(POOL_WINDOWS)
    pool_rows = 8 * (POOL_WINDOWS[-1] - 1)

    def body(x_ref, vp_ref, vec_ref, win_ref, wp_ref, wb_ref, wo_ref,
             hb_ref, p5_ref, pg_ref, qm_ref, cv_ref, yar_ref, yb_ref, o_ref, x1_ref, mbuf, ucarry, pcarry):
        i = pl.program_id(0)

        @pl.when(i == 0)
        def _():
            ucarry[...] = jnp.zeros_like(ucarry)
            pcarry[...] = jnp.zeros_like(pcarry)

        xp = _interleave(x_ref[...])
        hb = _prenorm(xp, vp_ref)[0].astype(BF16)
        hb_ref[...] = hb
        proj = lambda k: _dot(hb, win_ref[:, k * d:(k + 1) * d])

        za = proj(4)
        p5_ref[:, 3 * d:4 * d] = za.astype(BF16)
        sa = jax.nn.sigmoid(za)
        u_pool = proj(0)
        for g, window in enumerate(POOL_WINDOWS):
            cols = slice(g * gw, (g + 1) * gw)
            rows = 8 * (window - 1)
            u = u_pool[:, cols]
            halo = _halo_top(u[tm - rows:, :], ucarry[pool_rows - rows:, cols])
            s, shift = jnp.concatenate([halo, u], axis=0), 1
            while shift < window:
                s = s[8 * shift:, :] + s[:s.shape[0] - 8 * shift, :]
                shift *= 2
            pgb = (s * _inv_count(i * tm, window) - u).astype(BF16)
            pg_ref[:, cols] = pgb
            yar = _dot(pgb, wp_ref[g])
            yar_ref[:, cols] = yar
            mbuf[:, cols] = sa[:, cols] * (yar * vec_ref[2:3, cols])
        ucarry[...] = u_pool[tm - pool_rows:, :]

        ux = proj(1)
        uc = proj(3)
        p5_ref[:, 0:d] = ux.astype(BF16)
        p5_ref[:, 2 * d:3 * d] = uc.astype(BF16)
        p = uc * ux
        halo = _halo_top(p[tm - 16:, :], pcarry[...])
        pcarry[...] = p[tm - 16:, :]
        cv = vec_ref[3:4, :] + vec_ref[4:5, :] * _shift_down(p, halo, 2)
        cv = cv + vec_ref[5:6, :] * _shift_down(p, halo, 1)
        cv = cv + vec_ref[6:7, :] * p
        cv_ref[...] = cv
        ub = proj(2)
        p5_ref[:, d:2 * d] = ub.astype(BF16)
        qb = (ub * cv).astype(BF16)
        qm_ref[:, 0:d] = qb
        yb = _dot(qb, wb_ref[...])
        yb_ref[...] = yb

        zb = proj(5)
        p5_ref[:, 4 * d:5 * d] = zb.astype(BF16)
        mb = (mbuf[...] + jax.nn.sigmoid(zb) * yb).astype(BF16)
        qm_ref[:, d:2 * d] = mb
        o = _dot(mb, wo_ref[...])
        o_ref[...] = o
        r2 = lax.rsqrt(_rowmean(o * o) + EPS)
        x1_ref[...] = xp + vec_ref[0:1, :] * ((o * r2) * vec_ref[1:2, :])

    row = lambda n: pl.BlockSpec((tm, n), lambda i: (i, 0))
    widths = [d, 5 * d, d, 2 * d, d, d, d, d, d]
    return _call(
        body, "mixer_block_fwd", (t // tm,), [row(d)] + [_vmem()] * 6, [row(n) for n in widths],
        [_sds((t, n), BF16) for n in widths[:4]] + [_sds((t, n), F32) for n in widths[4:]],
        [pltpu.VMEM((tm, d), F32), pltpu.VMEM((pool_rows, d), F32), pltpu.VMEM((16, d), F32)],
        _params("arbitrary"), (x, vec_pre, vec, w_in, w_pool, w_bout, w_o), job)


def _ffn_block_fwd(x1, target, vec_pre, vec, fcv, w_up, w_down):
    t, d = x1.shape
    tm = GROUP
    fp = w_down.shape[0]
    nch = _column_chunks(fp)
    cw = fp // nch

    def body(x1_ref, tg_ref, vp_ref, vec_ref, fcv_ref, wu_ref, wd_ref,
             hb_ref, upb_ref, upreb_ref, a_ref, ffb_ref, dy_ref, loss_ref, carry):
        i = pl.program_id(0)

        @pl.when(i == 0)
        def _():
            carry[...] = jnp.zeros_like(carry)
            loss_ref[...] = jnp.zeros_like(loss_ref)

        x1 = x1_ref[...]
        hb = _prenorm(x1, vp_ref)[0].astype(BF16)
        hb_ref[...] = hb

        cols = [(slice(j * cw, (j + 1) * cw), slice(fp + j * cw, fp + (j + 1) * cw)) for j in range(nch)]
        up = lambda j: tuple(_dot(hb, wu_ref[:, col]) for col in cols[j])
        ahead = up(0)

        def conv(v, col):
            halo = _halo_top(v[tm - 16:, :], carry[:, col])
            carry[:, col] = v[tm - 16:, :]
            w0, w1, w2 = _conv_taps(fcv_ref, col)
            y = fcv_ref[3:4, col] + w0 * _shift_down(v, halo, 2)
            y = y + w1 * _shift_down(v, halo, 1)
            y = y + w2 * v
            upb_ref[:, col] = y.astype(BF16)
            upreb_ref[:, col] = v.astype(BF16)
            return y

        ff = None
        for j in range(nch):
            gc, vc = cols[j]
            cur, ahead = ahead, (up(j + 1) if j < nch - 1 else None)
            gate = conv(cur[0], gc)
            val = conv(cur[1], vc)
            ab = ((gate * _gelu_parts(gate)[0]) * val).astype(BF16)
            a_ref[:, gc] = ab
            part = _dot(ab, wd_ref[gc, :])
            ff = part if ff is None else ff + part
        ffb_ref[...] = ff.astype(BF16)
        r4 = lax.rsqrt(_rowmean(ff * ff) + EPS)
        y = x1 + vec_ref[0:1, :] * ((ff * r4) * vec_ref[1:2, :])
        e = y - _interleave(tg_ref[...])
        dy_ref[...] = e * (1.0 / d)
        loss_ref[...] += jnp.sum(_rowmean(e * e))

    row = lambda n: pl.BlockSpec((tm, n), lambda i: (i, 0))
    return pl.pallas_call(
        body, name="ffn_block_fwd", grid=(t // tm,),
        in_specs=[row(d), row(d)] + [_vmem()] * 5,
        out_specs=[row(d), row(2 * fp), row(2 * fp), row(fp), row(d), row(d), pl.BlockSpec((8, LANES), lambda i: (0, 0))],
        out_shape=[_sds((t, d), BF16), _sds((t, 2 * fp), BF16), _sds((t, 2 * fp), BF16), _sds((t, fp), BF16),
                   _sds((t, d), BF16), _sds((t, d), F32), _sds((8, LANES), F32)],
        scratch_shapes=[pltpu.VMEM((16, 2 * fp), F32)],
        compiler_params=_params("arbitrary"),
    )(x1, target, vec_pre, vec, fcv, w_up, w_down)


def _ffn_block_bwd(dy, ffb, x1, upb, upreb, vec_pre, vec, fcv, w_up, w_down):
    t, d = dy.shape
    tm = GROUP
    fp = w_down.shape[0]
    nch = _column_chunks(fp)
    cw = fp // nch
    nt = t // tm

    def body(dy_ref, ff_ref, x1_ref, upb_ref, upreb_ref, vp_ref, vec_ref, fcv_ref, wu_ref, wd_ref,
             dff_ref, dup_ref, dx1_ref, red_ref, cred_ref, pred_ref, carry):
        @pl.when(pl.program_id(0) == 0)
        def _():
            carry[...] = jnp.zeros_like(carry)
            red_ref[...] = jnp.zeros_like(red_ref)
            cred_ref[...] = jnp.zeros_like(cred_ref)
            pred_ref[...] = jnp.zeros_like(pred_ref)

        dy_v = dy_ref[...]
        dffb = _postnorm_bwd(dy_v, ff_ref[...].astype(F32), vec_ref[0:1, :], vec_ref[1:2, :], red_ref).astype(BF16)
        dff_ref[...] = dffb

        def conv_bwd(dv, col):
            dx, (t0, t1, t2) = _conv3_bwd(dv, carry, col, _conv_taps(fcv_ref, col), upreb_ref[:, col].astype(F32))
            cred_ref[0:1, col] += t0
            cred_ref[1:2, col] += t1
            cred_ref[2:3, col] += t2
            cred_ref[3:4, col] += _colsum(dv)
            dxb = dx.astype(BF16)
            dup_ref[:, col] = dxb
            return _dot_nt(dxb, wu_ref[:, col])

        dh = None
        for j in range(nch):
            gc = slice(j * cw, (j + 1) * cw)
            vc = slice(fp + j * cw, fp + (j + 1) * cw)
            da = _dot_nt(dffb, wd_ref[gc, :])
            gate = upb_ref[:, gc].astype(F32)
            val = upb_ref[:, vc].astype(F32)
            cdf, th = _gelu_parts(gate)
            dcdf = 0.5 * (1.0 - th * th) * (GELU_C0 * (1.0 + (3.0 * GELU_C1) * (gate * gate)))
            part = conv_bwd(da * val * (cdf + gate * dcdf), gc) + conv_bwd(da * (gate * cdf), vc)
            dh = part if dh is None else dh + part

        _, r, nh = _prenorm(x1_ref[...], vp_ref)
        dx1_ref[...] = dy_v + _prenorm_bwd(dh, r, nh, vp_ref, pred_ref)

    rev = lambda n: pl.BlockSpec((tm, n), lambda i: (nt - 1 - i, 0))
    fixed = lambda n: pl.BlockSpec((8, n), lambda i: (0, 0))
    return pl.pallas_call(
        body, name="ffn_block_bwd", grid=(nt,),
        in_specs=[rev(d), rev(d), rev(d), rev(2 * fp), rev(2 * fp)] + [_vmem()] * 5,
        out_specs=[rev(d), rev(2 * fp), rev(d), fixed(d), fixed(2 * fp), fixed(d)],
        out_shape=[_sds((t, d), BF16), _sds((t, 2 * fp), BF16), _sds((t, d), F32), _sds((8, d), F32),
                   _sds((8, 2 * fp), F32), _sds((8, d), F32)],
        scratch_shapes=[pltpu.VMEM((16, 2 * fp), F32)],
        compiler_params=_params("arbitrary"),
    )(dy, ffb, x1, upb, upreb, vec_pre, vec, fcv, w_up, w_down)


def _mixer_block_bwd(dx1, ob, yarb, ybb, cvb, p5b, x, vec_pre, vec, w_in, w_pool, w_bout, w_o, job):
    t, d = dx1.shape
    tm = GROUP
    gw = d // len(POOL_WINDOWS)
    nt = t // tm
    pool_rows = 8 * (POOL_WINDOWS[-1] - 1)

    def body(dx1_ref, o_ref, yar_ref, yb_ref, cv_ref, p5_ref, x_ref, vp_ref, vec_ref, win_ref, wp_ref, wb_ref, wo_ref,
             dyar_ref, dqm_ref, dp_ref, gx_ref, red_ref, pred_ref, dpgcarry, dcvcarry):
        i = pl.program_id(0)
        tix = nt - 1 - i

        @pl.when(i == 0)
        def _():
            red_ref[...] = jnp.zeros_like(red_ref)
            pred_ref[...] = jnp.zeros_like(pred_ref)
            dpgcarry[...] = jnp.zeros_like(dpgcarry)
            dcvcarry[...] = jnp.zeros_like(dcvcarry)

        pscale = vec_ref[2:3, :]
        dx1_v = dx1_ref[...]
        dob = _postnorm_bwd(dx1_v, o_ref[...].astype(F32), vec_ref[0:1, :], vec_ref[1:2, :], red_ref).astype(BF16)
        dqm_ref[:, d:2 * d] = dob
        dm = _dot_nt(dob, wo_ref[...])

        def dproj(cols, value):
            vb = value.astype(BF16)
            dp_ref[:, cols] = vb
            return _dot_nt(vb, win_ref[:, cols])

        sa = jax.nn.sigmoid(p5_ref[:, 3 * d:4 * d].astype(F32))
        yar = yar_ref[...].astype(F32)
        dya = dm * sa
        dh = dproj(slice(4 * d, 5 * d), dm * (yar * pscale) * sa * (1.0 - sa))
        red_ref[2:3, :] += _colsum(dya * yar)
        dyarb = (dya * pscale).astype(BF16)
        dyar_ref[...] = dyarb
        sb = jax.nn.sigmoid(p5_ref[:, 4 * d:5 * d].astype(F32))
        dybb = (dm * sb).astype(BF16)
        dqm_ref[:, 0:d] = dybb
        dh = dh + dproj(slice(5 * d, 6 * d), dm * yb_ref[...].astype(F32) * sb * (1.0 - sb))

        for g, window in enumerate(POOL_WINDOWS):
            cols = slice(g * gw, (g + 1) * gw)
            rows = 8 * (window - 1)
            dpg = _dot_nt(dyarb[:, cols], wp_ref[g])
            dpgs = dpg * _inv_count(tix * tm, window)
            halo = _halo_bottom(dpgs[:rows, :], dpgcarry[:rows, cols])
            dpgcarry[:, cols] = dpgs[:pool_rows, :]
            s, shift = jnp.concatenate([dpgs, halo], axis=0), 1
            while shift < window:
                s = s[:s.shape[0] - 8 * shift, :] + s[8 * shift:, :]
                shift *= 2
            dh = dh + dproj(cols, s - dpg)

        dq = _dot_nt(dybb, wb_ref[...])
        ux = p5_ref[:, 0:d].astype(F32)
        uc = p5_ref[:, 2 * d:3 * d].astype(F32)
        dh = dh + dproj(slice(2 * d, 3 * d), dq * cv_ref[...].astype(F32))
        dcv = dq * p5_ref[:, d:2 * d].astype(F32)
        taps = (vec_ref[4:5, :], vec_ref[5:6, :], vec_ref[6:7, :])
        dpv, (t0, t1, t2) = _conv3_bwd(dcv, dcvcarry, slice(0, d), taps, uc * ux)
        red_ref[3:4, :] += _colsum(dcv)
        red_ref[4:5, :] += t0
        red_ref[5:6, :] += t1
        red_ref[6:7, :] += t2
        dh = dh + dproj(slice(d, 2 * d), dpv * uc)
        dh = dh + dproj(slice(3 * d, 4 * d), dpv * ux)

        _, r, nh = _prenorm(_interleave(x_ref[...]), vp_ref)
        gx_ref[...] = _deinterleave(dx1_v + _prenorm_bwd(dh, r, nh, vp_ref, pred_ref))

    rev = lambda n: pl.BlockSpec((tm, n), lambda i: (nt - 1 - i, 0))
    return _call(
        body, "mixer_block_bwd", (nt,), [rev(d)] * 5 + [rev(5 * d), rev(d)] + [_vmem()] * 6,
        [rev(d), rev(2 * d), rev(6 * d), rev(d), pl.BlockSpec((16, d), lambda i: (0, 0)),
         pl.BlockSpec((8, d), lambda i: (0, 0))],
        [_sds((t, d), BF16), _sds((t, 2 * d), BF16), _sds((t, 6 * d), BF16), _sds((t, d), F32), _sds((16, d), F32),
         _sds((8, d), F32)],
        [pltpu.VMEM((pool_rows, d), F32), pltpu.VMEM((16, d), F32)],
        _params("arbitrary"), (dx1, ob, yarb, ybb, cvb, p5b, x, vec_pre, vec, w_in, w_pool, w_bout, w_o), job)


def _matmul_tn(a, b, bm, bn, tk, by_col_block, name, job=None):
    t, m = a.shape
    n = b.shape[1]
    nk = t // tk

    def body(a_ref, b_ref, o_ref, acc_ref):
        k = pl.program_id(2)

        @pl.when(k == 0)
        def _():
            acc_ref[...] = jnp.zeros_like(acc_ref)

        acc_ref[...] += _dot_tn(a_ref[...], b_ref[...])

        @pl.when(k == nk - 1)
        def _():
            o_ref[...] = acc_ref[...].astype(o_ref.dtype)

    if by_col_block:
        out_shape = _sds((n // bn, m, bn), BF16)
        out_spec = pl.BlockSpec((None, bm, bn), lambda i, j, k: (j, i, 0))
    else:
        out_shape = _sds((m, n), BF16)
        out_spec = pl.BlockSpec((bm, bn), lambda i, j, k: (i, j))
    out = _call(body, name, (m // bm, n // bn, nk),
                [pl.BlockSpec((tk, bm), lambda i, j, k: (k, i)), pl.BlockSpec((tk, bn), lambda i, j, k: (k, j))],
                [out_spec], [out_shape], [pltpu.VMEM((bm, bn), F32)],
                _params("arbitrary", "arbitrary", "arbitrary"), (a, b), job)
    return out if job is not None else out[0]


def _matmul_tn_groups(a, b, groups, tk, name, job=None):
    t, m = a.shape
    w = m // groups
    nk = t // tk

    def body(a_ref, b_ref, o_ref, acc_ref):
        k = pl.program_id(1)

        @pl.when(k == 0)
        def _():
            acc_ref[...] = jnp.zeros_like(acc_ref)

        acc_ref[...] += _dot_tn(a_ref[...], b_ref[...])

        @pl.when(k == nk - 1)
        def _():
            o_ref[...] = acc_ref[...].astype(o_ref.dtype)

    blk = pl.BlockSpec((tk, w), lambda g, k: (k, g))
    out = _call(body, name, (groups, nk), [blk, blk], [pl.BlockSpec((None, w, w), lambda g, k: (g, 0, 0))],
                [_sds((groups, w, w), BF16)], [pltpu.VMEM((w, w), F32)], _params("arbitrary", "arbitrary"), (a, b), job)
    return out if job is not None else out[0]


def _round_up(n, k):
    return (n + k - 1) // k * k


def _rows8(rows, width):
    n = _round_up(len(rows), 8)
    rows = list(rows) + [jnp.zeros((1, width), F32)] * (n - len(rows))
    return jnp.concatenate(rows, axis=0)


def kernel(x, c, g_pre_mix, g_post_mix, g_pre_ffn, g_post_ffn, w_ada, b_ada, w_in, w_pool, pool_scale, conv_w, conv_b, w_bout, w_o, w_up, ffn_conv_w, ffn_conv_b, w_down, loss_target, m_g_pre_mix, m_g_post_mix, m_g_pre_ffn, m_g_post_ffn, m_w_ada, m_b_ada, m_w_in, m_w_pool, m_pool_scale, m_conv_w, m_conv_b, m_w_bout, m_w_o, m_w_up, m_ffn_conv_w, m_ffn_conv_b, m_w_down, v_g_pre_mix, v_g_post_mix, v_g_pre_ffn, v_g_post_ffn, v_w_ada, v_b_ada, v_w_in, v_w_pool, v_pool_scale, v_conv_w, v_conv_b, v_w_bout, v_w_o, v_w_up, v_ffn_conv_w, v_ffn_conv_b, v_w_down):
    t, d = x.shape[1], x.shape[2]
    ngroups = len(POOL_WINDOWS)
    gw = d // ngroups
    ada_n = w_ada.shape[2]
    in_n = w_in.shape[2]
    up_n = w_up.shape[2]
    fp = NDEV * w_down.shape[1]

    xi, yi, ci = _position()
    me = _linear(xi, yi, ci)
    chip = 2 * xi + yi
    core = jnp.reshape(ci, (1,)).astype(jnp.int32)
    sel = jnp.stack([2 * chip + ci, chip]).astype(jnp.int32)

    x2 = x.reshape(t, d)
    target = loss_target.reshape(t, d)

    cw_n = conv_w.shape[2]
    pack = jnp.concatenate([c.reshape(1, d), conv_w[0].reshape(1, 3 * cw_n), ffn_conv_w[0].reshape(1, 3 * up_n)], axis=1)
    pack = jnp.pad(pack, ((0, 0), (0, _round_up(pack.shape[1], LANES) - pack.shape[1])))
    b_piece = lax.dynamic_slice_in_dim(b_ada, me * ada_n, ada_n, axis=1)
    mixer_weights = _allgather_job(
        [w_in[0].astype(BF16), w_bout[0].astype(BF16), w_o[0].astype(BF16), w_pool[0].astype(BF16)],
        ["cols", "rows", "rows", "mid"], 0.5)
    gathered, mod_rows, w_in_f, g_bout, g_o, w_pool_f = _gather_weights_and_modulation(
        mixer_weights, pack, w_ada[0], b_piece, d)
    w_bout_f = g_bout.reshape(d, d)
    w_o_f = g_o.reshape(d, d)
    c_all = gathered[:, :d]
    c16 = jnp.pad(c_all, ((0, 8), (0, 0))).astype(BF16)
    conv_w_full = gathered[:, d:d + 3 * cw_n].reshape(NDEV, 3, cw_n).transpose(1, 0, 2).reshape(3, NDEV * cw_n)
    fcw_full = gathered[:, d + 3 * cw_n:d + 3 * cw_n + 3 * up_n].reshape(NDEV, 3, up_n)
    fcw_full = fcw_full.transpose(1, 0, 2).reshape(3, 2 * fp)
    fcv = jnp.concatenate([fcw_full, ffn_conv_b, jnp.zeros((4, 2 * fp), F32)], axis=0)
    mod = mod_rows.reshape(1, NDEV * ada_n)
    sh1, sc1, gt1, sh2, sc2, gt2 = [mod[:, k * d:(k + 1) * d] for k in range(6)]

    ffn_weights = _allgather_job([w_up[0].astype(BF16), w_down[0].astype(BF16)], ["rows", "rows"], 0.75)
    vec_pre_mix = _rows8([g_pre_mix, 1.0 + sc1, sh1], d)
    vec_mix = _rows8([gt1, g_post_mix, pool_scale, conv_b, conv_w_full[0:1], conv_w_full[1:2], conv_w_full[2:3]], d)
    h1b, p5b, pgb, qmb, cvb, yarb, ybb, ob, x1, g_up, g_down = _mixer_block_fwd(
        x2, vec_pre_mix, vec_mix, w_in_f, w_pool_f, w_bout_f, w_o_f, ffn_weights)
    w_up_f = g_up.transpose(1, 0, 2).reshape(d, 2 * fp)
    w_down_f = g_down.reshape(fp, d)
    vec_pre_ffn = _rows8([g_pre_ffn, 1.0 + sc2, sh2], d)
    vec_ffn = _rows8([gt2, g_post_ffn], d)
    h2b, upb, upreb, ab, ffb, dy, loss_part = _ffn_block_fwd(x1, target, vec_pre_ffn, vec_ffn, fcv, w_up_f, w_down_f)

    tk = min(2048, t)
    chip_sum = lambda gs, ss: [_chip_sum(g, s, core) for g, s in zip(gs, ss)]
    dffb, dupre, dx1, red_ffn, red_fconv, red_pre_ffn = _ffn_block_bwd(
        dy, ffb, x1, upb, upreb, vec_pre_ffn, vec_ffn, fcv, w_up_f, w_down_f)
    chunk = fp // _column_chunks(fp)
    gr_up = _matmul_tn(h2b, dupre, d, chunk, tk, False, "wgrad_up")
    gr_up = gr_up.reshape(d, NDEV, up_n).transpose(1, 0, 2)
    gr_down = _matmul_tn(ab, dffb, chunk, d, tk, False, "wgrad_down").reshape(NDEV, fp // NDEV, d)
    sib_ffn = _run_job(_sibling_job([gr_up, gr_down]), "rs_sibling_ffn")
    dyarb, dqmb, dproj, grad_x, red_mix, red_pre_mix = _mixer_block_bwd(
        dx1, ob, yarb, ybb, cvb, p5b, x2, vec_pre_mix, vec_mix, w_in_f, w_pool_f, w_bout_f, w_o_f, None)
    gr_in, fc_up, fc_down = _matmul_tn(h1b, dproj, d, in_n, tk, True, "wgrad_in",
                                       _chips_job(chip_sum([gr_up, gr_down], sib_ffn)))
    sib_in = _run_job(_sibling_job([gr_in]), "rs_sibling_in")
    gr_qm, fc_in = _matmul_tn_groups(qmb, dqmb, 2, tk, "wgrad_bout_o", _chips_job(chip_sum([gr_in], sib_in)))
    gr_bout = gr_qm[0].reshape(NDEV, d // NDEV, d)
    gr_o = gr_qm[1].reshape(NDEV, d // NDEV, d)
    gr_pool = _matmul_tn_groups(pgb, dyarb, ngroups, min(4096, t), "wgrad_pool")
    gr_pool = gr_pool.reshape(ngroups, NDEV, gw // NDEV, gw).transpose(1, 0, 2, 3).reshape(NDEV, -1, gw)
    rest = [gr_bout, gr_o, gr_pool]
    sib_rest = _run_job(_sibling_job(rest), "rs_sibling_rest")

    dmod = [red_pre_mix[0:1], red_pre_mix[1:2], red_mix[1:2], red_pre_ffn[0:1], red_pre_ffn[1:2], red_ffn[1:2]]
    small = [red_pre_mix[2:3], red_mix[0:1], red_pre_ffn[2:3], red_ffn[0:1], red_mix[2:3], red_mix[3:4],
             red_mix[4:5], red_mix[5:6], red_mix[6:7]] + dmod
    flat = jnp.concatenate(small + [red_fconv[0:4].reshape(1, 8 * fp), loss_part[0:1, 0:1]], axis=1)
    flat_n = flat.shape[1]
    width = 8 * LANES
    rows = _round_up(-(-flat_n // width), 8)
    flat = jnp.pad(flat, ((0, 0), (0, rows * width - flat_n))).reshape(rows, width)
    gat, tot, *fc_rest = _small_allreduce(flat, "allreduce_small_rs_rest", _chips_job(chip_sum(rest, sib_rest)))

    def big(grad, from_sibling, from_chips, w, m, v, name):
        shape = w.shape
        w2, m2, v2 = [a.reshape((-1, shape[-1])) for a in (w, m, v)]
        outs = _reduce_adamw(grad, from_sibling, from_chips, sel, w2, m2, v2, name)
        return [a.reshape(shape) for a in outs]

    g_w_up, d_w_up, nm_w_up, nv_w_up = big(gr_up, sib_ffn[0], fc_up, w_up, m_w_up, v_w_up, "adamw_up")
    g_w_down, d_w_down, nm_w_down, nv_w_down = big(gr_down, sib_ffn[1], fc_down, w_down, m_w_down, v_w_down, "adamw_down")
    g_w_in, d_w_in, nm_w_in, nv_w_in = big(gr_in, sib_in[0], fc_in, w_in, m_w_in, v_w_in, "adamw_in")
    g_w_bout, d_w_bout, nm_w_bout, nv_w_bout = big(gr_bout, sib_rest[0], fc_rest[0], w_bout, m_w_bout, v_w_bout, "adamw_bout")
    g_w_o, d_w_o, nm_w_o, nv_w_o = big(gr_o, sib_rest[1], fc_rest[1], w_o, m_w_o, v_w_o, "adamw_o")
    g_w_pool, d_w_pool, nm_w_pool, nv_w_pool = big(gr_pool, sib_rest[2], fc_rest[2], w_pool, m_w_pool, v_w_pool, "adamw_pool")

    tot = tot.reshape(1, rows * width)
    gat = gat.reshape(NDEV, rows * width)
    take = lambda k: tot[:, k * d:(k + 1) * d]
    g_g_pre_mix, g_g_post_mix, g_g_pre_ffn, g_g_post_ffn, g_pool_scale, g_conv_b = [take(k) for k in range(6)]
    g_conv_w_full = jnp.concatenate([take(6), take(7), take(8)], axis=0)
    g_conv_w = lax.dynamic_slice_in_dim(g_conv_w_full, me * cw_n, cw_n, axis=1)
    g_b_ada = tot[:, 9 * d:15 * d]
    dmod_all = gat[:, 9 * d:15 * d]
    fconv_tot = tot[:, 15 * d:15 * d + 8 * fp].reshape(4, 2 * fp)
    loss = 0.5 * tot[0, 15 * d + 8 * fp]
    g_ffn_conv_b = fconv_tot[3:4]
    g_ffn_conv_w = lax.dynamic_slice_in_dim(fconv_tot[0:3], me * up_n, up_n, axis=1)
    dmod_piece = lax.dynamic_slice_in_dim(dmod_all, me * ada_n, ada_n, axis=1)
    g_w_ada = _wada_grad(c16, jnp.pad(dmod_piece, ((0, 8), (0, 0))).astype(BF16))

    names_small = [(g_pre_mix, g_g_pre_mix, m_g_pre_mix, v_g_pre_mix), (g_post_mix, g_g_post_mix, m_g_post_mix, v_g_post_mix),
                   (g_pre_ffn, g_g_pre_ffn, m_g_pre_ffn, v_g_pre_ffn), (g_post_ffn, g_g_post_ffn, m_g_post_ffn, v_g_post_ffn),
                   (b_ada, g_b_ada, m_b_ada, v_b_ada), (pool_scale, g_pool_scale, m_pool_scale, v_pool_scale),
                   (conv_w, g_conv_w, m_conv_w, v_conv_w), (conv_b, g_conv_b, m_conv_b, v_conv_b),
                   (ffn_conv_w, g_ffn_conv_w, m_ffn_conv_w, v_ffn_conv_w), (ffn_conv_b, g_ffn_conv_b, m_ffn_conv_b, v_ffn_conv_b)]
    sizes = [w.size for w, _, _, _ in names_small]
    total = sum(sizes)
    prow = _round_up(-(-total // width), 8)

    def pack_small(k):
        a = jnp.concatenate([q[k].reshape(1, -1) for q in names_small], axis=1)
        return jnp.pad(a, ((0, 0), (0, prow * width - total)), constant_values=1.0).reshape(prow, width)

    ds, ms, vs = _adamw(pack_small(0), pack_small(1), pack_small(2), pack_small(3), "adamw_small")

    def unpack_small(a):
        a = a.reshape(-1)
        out, off = [], 0
        for (w, _, _, _), n in zip(names_small, sizes):
            out.append(a[off:off + n].reshape(w.shape))
            off += n
        return out

    (d_g_pre_mix, d_g_post_mix, d_g_pre_ffn, d_g_post_ffn, d_b_ada, d_pool_scale, d_conv_w, d_conv_b,
     d_ffn_conv_w, d_ffn_conv_b) = unpack_small(ds)
    (nm_g_pre_mix, nm_g_post_mix, nm_g_pre_ffn, nm_g_post_ffn, nm_b_ada, nm_pool_scale, nm_conv_w, nm_conv_b,
     nm_ffn_conv_w, nm_ffn_conv_b) = unpack_small(ms)
    (nv_g_pre_mix, nv_g_post_mix, nv_g_pre_ffn, nv_g_post_ffn, nv_b_ada, nv_pool_scale, nv_conv_w, nv_conv_b,
     nv_ffn_conv_w, nv_ffn_conv_b) = unpack_small(vs)
    d_w_ada, nm_w_ada, nv_w_ada = [a.reshape(w_ada.shape) for a in
                                   _adamw(w_ada[0], g_w_ada, m_w_ada[0], v_w_ada[0], "adamw_ada")]

    grads = [g_g_pre_mix, g_g_post_mix, g_g_pre_ffn, g_g_post_ffn, g_w_ada.reshape(w_ada.shape), g_b_ada, g_w_in,
             g_w_pool, g_pool_scale, g_conv_w.reshape(conv_w.shape), g_conv_b, g_w_bout, g_w_o, g_w_up,
             g_ffn_conv_w.reshape(ffn_conv_w.shape), g_ffn_conv_b, g_w_down]
    deltas = [d_g_pre_mix, d_g_post_mix, d_g_pre_ffn, d_g_post_ffn, d_w_ada, d_b_ada, d_w_in, d_w_pool, d_pool_scale,
              d_conv_w, d_conv_b, d_w_bout, d_w_o, d_w_up, d_ffn_conv_w, d_ffn_conv_b, d_w_down]
    new_m = [nm_g_pre_mix, nm_g_post_mix, nm_g_pre_ffn, nm_g_post_ffn, nm_w_ada, nm_b_ada, nm_w_in, nm_w_pool,
             nm_pool_scale, nm_conv_w, nm_conv_b, nm_w_bout, nm_w_o, nm_w_up, nm_ffn_conv_w, nm_ffn_conv_b, nm_w_down]
    new_v = [nv_g_pre_mix, nv_g_post_mix, nv_g_pre_ffn, nv_g_post_ffn, nv_w_ada, nv_b_ada, nv_w_in, nv_w_pool,
             nv_pool_scale, nv_conv_w, nv_conv_b, nv_w_bout, nv_w_o, nv_w_up, nv_ffn_conv_w, nv_ffn_conv_b, nv_w_down]
    return (loss, grad_x.reshape(x.shape), *grads, *deltas, *new_m, *new_v)
```

```python
import math

import jax
import jax.numpy as jnp
from jax import lax
from jax.experimental import pallas as pl
from jax.experimental.pallas import tpu as pltpu

F32 = jnp.float32
BF16 = jnp.bfloat16
MESH = pl.DeviceIdType.MESH

NDEV = 8
NCHIP = 4
EPS = 1e-6
POOL_WINDOWS = (2, 4, 8, 16)
LANES = 128
ADAM_LR = 0.001
ADAM_B1 = 0.9
ADAM_B2 = 0.999
ADAM_EPS = 1e-08
ADAM_WD = 0.01
ADAM_STEP = 10
GELU_C0 = math.sqrt(2.0 / math.pi)
GELU_C1 = 0.044715
VMEM_LIMIT = 56 * 2**20


def _vmem():
    return pl.BlockSpec(memory_space=pltpu.VMEM)


def _any():
    return pl.BlockSpec(memory_space=pl.ANY)


def _params(*sem):
    return pltpu.CompilerParams(dimension_semantics=sem, vmem_limit_bytes=VMEM_LIMIT)


def _sds(shape, dtype):
    return jax.ShapeDtypeStruct(tuple(shape), dtype)


def _position():
    return lax.axis_index("x"), lax.axis_index("y"), lax.axis_index("c")


def _linear(x, y, c):
    return 4 * x + 2 * y + c


def _dot(a, b):
    return jnp.dot(a, b, preferred_element_type=F32)


def _dot_nt(a, b):
    return lax.dot_general(a, b, (((1,), (1,)), ((), ())), preferred_element_type=F32)


def _dot_tn(a, b):
    return lax.dot_general(a, b, (((0,), (0,)), ((), ())), preferred_element_type=F32)


def _colsum(v):
    return jnp.sum(v, axis=0, keepdims=True)


def _rowmean(v):
    return jnp.mean(v, axis=-1, keepdims=True)


GROUP = 256


def _interleave(v):
    g, n = v.shape
    return jnp.swapaxes(v.reshape(8, g // 8, n), 0, 1).reshape(g, n)


def _deinterleave(v):
    g, n = v.shape
    return jnp.swapaxes(v.reshape(g // 8, 8, n), 0, 1).reshape(g, n)


def _halo_top(cur_last, prev_last):
    rows, n = cur_last.shape
    c3 = cur_last.reshape(rows // 8, 8, n)
    p3 = prev_last.reshape(rows // 8, 8, n)
    sub = lax.broadcasted_iota(jnp.int32, c3.shape, 1)
    return jnp.where(sub == 0, pltpu.roll(p3, 1, 1), pltpu.roll(c3, 1, 1)).reshape(rows, n)


def _halo_bottom(cur_first, next_first):
    rows, n = cur_first.shape
    c3 = cur_first.reshape(rows // 8, 8, n)
    n3 = next_first.reshape(rows // 8, 8, n)
    sub = lax.broadcasted_iota(jnp.int32, c3.shape, 1)
    return jnp.where(sub == 7, pltpu.roll(n3, 7, 1), pltpu.roll(c3, 7, 1)).reshape(rows, n)


def _shift_down(v, halo, k):
    rows = v.shape[0]
    return jnp.concatenate([halo[halo.shape[0] - 8 * k:, :], v[:rows - 8 * k, :]], axis=0)


def _shift_up(v, halo, k):
    return jnp.concatenate([v[8 * k:, :], halo[:8 * k, :]], axis=0)


def _inv_count(first_token, window):
    row = lax.broadcasted_iota(jnp.int32, (GROUP, 1), 0)
    t = first_token + (row % 8) * (GROUP // 8) + row // 8
    return 1.0 / jnp.minimum(t + 1, window).astype(F32)


class _Job:
    def __init__(self, inputs, out_shape, scratch, phases):
        self.inputs, self.out_shape, self.scratch, self.phases = list(inputs), list(out_shape), list(scratch), phases


def _call(body, name, grid, in_specs, out_specs, out_shape, scratch_shapes, params, operands, job=None):
    if job is None:
        return pl.pallas_call(body, name=name, grid=grid, in_specs=in_specs, out_specs=out_specs, out_shape=out_shape,
                              scratch_shapes=scratch_shapes, compiler_params=params)(*operands)
    n_in, n_out, n_scr = len(in_specs), len(out_specs), len(scratch_shapes)
    j_in, j_out = len(job.inputs), len(job.out_shape)
    steps = math.prod(grid)

    def hosted(*refs):
        own_in, refs = refs[:n_in], refs[n_in:]
        jin, refs = refs[:j_in], refs[j_in:]
        own_out, refs = refs[:n_out], refs[n_out:]
        jout, refs = refs[:j_out], refs[j_out:]
        own_scr, jscr = refs[:n_scr], refs[n_scr:]
        step = pl.program_id(0)
        for axis in range(1, len(grid)):
            step = step * grid[axis] + pl.program_id(axis)
        for frac, fn in job.phases[:-1]:
            pl.when(step == int(frac * (steps - 1)))(lambda fn=fn: fn(jin, jout, jscr))
        body(*own_in, *own_out, *own_scr)
        pl.when(step == steps - 1)(lambda: job.phases[-1][1](jin, jout, jscr))

    return pl.pallas_call(
        hosted, name=name, grid=grid, in_specs=list(in_specs) + [_any()] * j_in,
        out_specs=list(out_specs) + [_any()] * j_out, out_shape=list(out_shape) + job.out_shape,
        scratch_shapes=list(scratch_shapes) + job.scratch, compiler_params=params)(*operands, *job.inputs)


def _run_job(job, name):
    n_in, n_out = len(job.inputs), len(job.out_shape)

    def body(*refs):
        for _, fn in job.phases:
            fn(refs[:n_in], refs[n_in:n_in + n_out], refs[n_in + n_out:])

    return pl.pallas_call(body, name=name, out_shape=job.out_shape, in_specs=[_any()] * n_in,
                          out_specs=[_any()] * n_out, scratch_shapes=job.scratch)(*job.inputs)


def _peers(x, y, c):
    out = []
    for k in range(1, NDEV):
        out.append(((1 - x) if k & 4 else x, (1 - y) if k & 2 else y, (1 - c) if k & 1 else c))
    return out


def _small_allreduce(v, name, job):
    r, n = v.shape
    j_in, j_out = len(job.inputs), len(job.out_shape)

    def body(v_ref, *rest):
        jin, rest = rest[:j_in], rest[j_in:]
        gat_ref, sum_ref = rest[:2]
        jout, rest = rest[2:2 + j_out], rest[2 + j_out:]
        send_sems, recv_sems, local_sem = rest[:3]
        jscr = rest[3:]
        job.phases[0][1](jin, jout, jscr)
        x, y, c = _position()
        me = _linear(x, y, c)
        mine = pltpu.make_async_copy(v_ref, gat_ref.at[me], local_sem)
        mine.start()
        peers = _peers(x, y, c)
        sends = []
        for k, peer in enumerate(peers):
            cp = pltpu.make_async_remote_copy(src_ref=v_ref, dst_ref=gat_ref.at[me], send_sem=send_sems.at[k],
                                              recv_sem=recv_sems.at[k], device_id=peer, device_id_type=MESH)
            cp.start()
            sends.append(cp)
        for k, peer in enumerate(peers):
            pltpu.make_async_remote_copy(src_ref=v_ref, dst_ref=gat_ref.at[_linear(*peer)], send_sem=send_sems.at[k],
                                         recv_sem=recv_sems.at[k], device_id=peer, device_id_type=MESH).wait_recv()
        for cp in sends:
            cp.wait_send()
        mine.wait()
        acc = gat_ref[0]
        for j in range(1, NDEV):
            acc = acc + gat_ref[j]
        sum_ref[...] = acc
        job.phases[-1][1](jin, jout, jscr)

    return pl.pallas_call(
        body, name=name, out_shape=[_sds((NDEV, r, n), F32), _sds((r, n), F32)] + job.out_shape,
        in_specs=[_vmem()] + [_any()] * j_in, out_specs=[_vmem()] * 2 + [_any()] * j_out,
        scratch_shapes=[pltpu.SemaphoreType.DMA((NDEV - 1,)), pltpu.SemaphoreType.DMA((NDEV - 1,)),
                        pltpu.SemaphoreType.DMA(())] + job.scratch,
    )(v, *job.inputs)


def _exchange_rows(src_for, dst_ref, sems):
    send_sems, recv_sems, local_sem = sems
    x, y, c = _position()
    me = _linear(x, y, c)
    row = lambda j: dst_ref.at[pl.ds(j, 1), :]
    mine = pltpu.make_async_copy(src_for(me), row(me), local_sem)
    mine.start()
    peers = _peers(x, y, c)
    sends = []
    for k, peer in enumerate(peers):
        cp = pltpu.make_async_remote_copy(src_ref=src_for(_linear(*peer)), dst_ref=row(me), send_sem=send_sems.at[k],
                                          recv_sem=recv_sems.at[k], device_id=peer, device_id_type=MESH)
        cp.start()
        sends.append(cp)
    for k, peer in enumerate(peers):
        pltpu.make_async_remote_copy(src_ref=src_for(me), dst_ref=row(_linear(*peer)), send_sem=send_sems.at[k],
                                     recv_sem=recv_sems.at[k], device_id=peer, device_id_type=MESH).wait_recv()
    for cp in sends:
        cp.wait_send()
    mine.wait()


def _gather_weights_and_modulation(job, pack, w_ada, b_piece, d):
    n = pack.shape[1]
    m = w_ada.shape[1]
    j_in, j_out = len(job.inputs), len(job.out_shape)
    row_sems = [pltpu.SemaphoreType.DMA((NDEV - 1,)), pltpu.SemaphoreType.DMA((NDEV - 1,)), pltpu.SemaphoreType.DMA(())]

    def body(pack_ref, wada_ref, bp_ref, *rest):
        jin, rest = rest[:j_in], rest[j_in:]
        gat_ref, mod_ref = rest[:2]
        jout, rest = rest[2:2 + j_out], rest[2 + j_out:]
        sems1, sems2, piece, jscr = rest[0:3], rest[3:6], rest[6], rest[7:]
        start, forward, finish = [fn for _, fn in job.phases]
        start(jin, jout, jscr)
        _exchange_rows(lambda j: pack_ref, gat_ref, sems1)
        c16 = jnp.concatenate([gat_ref[:, 0:d], jnp.zeros((NDEV, d), F32)], axis=0).astype(BF16)
        piece[...] = (_dot(c16, wada_ref[...].astype(BF16)) + bp_ref[...])[0:NDEV, :]
        _exchange_rows(lambda j: piece.at[pl.ds(j, 1), :], mod_ref, sems2)
        forward(jin, jout, jscr)
        finish(jin, jout, jscr)

    return pl.pallas_call(
        body, name="gather_weights_and_modulation",
        out_shape=[_sds((NDEV, n), F32), _sds((NDEV, m), F32)] + job.out_shape,
        in_specs=[_vmem()] * 3 + [_any()] * j_in, out_specs=[_vmem()] * 2 + [_any()] * j_out,
        scratch_shapes=row_sems + row_sems + [pltpu.VMEM((NDEV, m), F32)] + job.scratch,
    )(pack, w_ada, b_piece, *job.inputs)


def _gathered(shard, layout):
    if layout == "rows":
        return (NDEV,) + shard.shape, lambda ref, j: ref.at[j]
    if layout == "cols":
        r, c = shard.shape
        return (r, NDEV * c), lambda ref, j: ref.at[:, pl.ds(pl.multiple_of(j * c, LANES), c)]
    g, r, c = shard.shape
    return (g, NDEV * r, c), lambda ref, j: ref.at[:, pl.ds(pl.multiple_of(j * r, 16), r), :]


def _allgather_job(shards, layouts, forward_at):
    n = len(shards)
    specs = [_gathered(s, l) for s, l in zip(shards, layouts)]

    def copies(src, dst, sems):
        send_sems, recv_sems, _ = sems
        x, y, c = _position()
        me, sibling = (x, y, c), (x, y, 1 - c)
        chips = [(1 - x, y), (x, 1 - y), (1 - x, 1 - y)]

        def copy(a, k, block, to, from_src=False):
            blk = specs[a][1](dst[a], _linear(*block))
            return pltpu.make_async_remote_copy(src_ref=src[a] if from_src else blk, dst_ref=blk,
                                                send_sem=send_sems.at[a, k], recv_sem=recv_sems.at[a, k],
                                                device_id=to, device_id_type=MESH)
        return copy, me, sibling, chips, c

    def local(src, dst, sems):
        x, y, c = _position()
        return [pltpu.make_async_copy(src[a], specs[a][1](dst[a], _linear(x, y, c)), sems[2].at[a]) for a in range(n)]

    def first(src, dst, sems):
        copy, me, sibling, chips, c = copies(src, dst, sems)
        out = [copy(a, 1 + j, me, (*chip, c), from_src=True) for j, chip in enumerate(chips) for a in range(n)]
        return out + [copy(a, 0, me, sibling, from_src=True) for a in range(n)]

    def passed(src, dst, sems):
        copy, me, sibling, chips, c = copies(src, dst, sems)
        return [copy(a, 4 + j, (*chip, c), sibling) for j, chip in enumerate(chips) for a in range(n)]

    def start(src, dst, sems):
        for cp in local(src, dst, sems) + first(src, dst, sems):
            cp.start()

    def forward(src, dst, sems):
        copy, me, sibling, chips, c = copies(src, dst, sems)
        for j, chip in enumerate(chips):
            for a in range(n):
                copy(a, 1 + j, (*chip, c), me).wait_recv()
        for cp in passed(src, dst, sems):
            cp.start()

    def finish(src, dst, sems):
        copy, me, sibling, chips, c = copies(src, dst, sems)
        for a in range(n):
            copy(a, 0, sibling, me).wait_recv()
        for j, chip in enumerate(chips):
            for a in range(n):
                copy(a, 4 + j, (*chip, 1 - c), me).wait_recv()
        for cp in first(src, dst, sems) + passed(src, dst, sems):
            cp.wait_send()
        for cp in local(src, dst, sems):
            cp.wait()

    return _Job(shards, [_sds(spec[0], s.dtype) for spec, s in zip(specs, shards)],
                [pltpu.SemaphoreType.DMA((n, 7)), pltpu.SemaphoreType.DMA((n, 7)), pltpu.SemaphoreType.DMA((n,))],
                [(0.0, start), (forward_at, forward), (1.0, finish)])


def _sibling_job(grads):
    n = len(grads)

    def copies(src, dst, sems):
        x, y, c = _position()
        return [pltpu.make_async_remote_copy(src_ref=src[a].at[2 * q + 1 - c], dst_ref=dst[a].at[q],
                                             send_sem=sems[0].at[a, q], recv_sem=sems[1].at[a, q],
                                             device_id=(x, y, 1 - c), device_id_type=MESH)
                for a in range(n) for q in range(NCHIP)]

    return _exchange_job(grads, NCHIP, copies)


def _chips_job(chip_sums):
    n = len(chip_sums)

    def copies(src, dst, sems):
        x, y, c = _position()
        chips = [(1 - x, y), (x, 1 - y), (1 - x, 1 - y)]
        return [pltpu.make_async_remote_copy(src_ref=src[a].at[2 * chip[0] + chip[1]], dst_ref=dst[a].at[j],
                                             send_sem=sems[0].at[a, j], recv_sem=sems[1].at[a, j],
                                             device_id=(*chip, c), device_id_type=MESH)
                for j, chip in enumerate(chips) for a in range(n)]

    return _exchange_job(chip_sums, 3, copies)


def _exchange_job(arrays, slots, copies):
    n = len(arrays)

    def start(src, dst, sems):
        for cp in copies(src, dst, sems):
            cp.start()

    def finish(src, dst, sems):
        cps = copies(src, dst, sems)
        for cp in cps:
            cp.wait_recv()
        for cp in cps:
            cp.wait_send()

    return _Job(arrays, [_sds((slots,) + a.shape[1:], a.dtype) for a in arrays],
                [pltpu.SemaphoreType.DMA((n, slots)), pltpu.SemaphoreType.DMA((n, slots))],
                [(0.0, start), (1.0, finish)])


def _row_block(r):
    for rb in (512, 256, 128, 64, 32, 16):
        if r % rb == 0:
            return rb
    return r


def _chip_sum(grad, from_sibling, core):
    _, r, c = grad.shape
    rb = _row_block(r)

    def body(core_ref, g_ref, s_ref, o_ref):
        del core_ref
        o_ref[...] = (g_ref[...].astype(F32) + s_ref[...].astype(F32)).astype(o_ref.dtype)

    grid_spec = pltpu.PrefetchScalarGridSpec(
        num_scalar_prefetch=1, grid=(NCHIP, r // rb),
        in_specs=[pl.BlockSpec((None, rb, c), lambda q, i, core: (2 * q + core[0], i, 0)),
                  pl.BlockSpec((None, rb, c), lambda q, i, core: (q, i, 0))],
        out_specs=pl.BlockSpec((None, rb, c), lambda q, i, core: (q, i, 0)))
    return pl.pallas_call(body, name="rs_chip_sum", grid_spec=grid_spec, out_shape=_sds((NCHIP, r, c), BF16),
                          compiler_params=_params("parallel", "parallel"))(core, grad, from_sibling)


def _adamw_math(w, g, m, v):
    m2 = ADAM_B1 * m + (1.0 - ADAM_B1) * g
    v2 = ADAM_B2 * v + (1.0 - ADAM_B2) * jnp.square(g)
    m_hat = m2 / (1.0 - ADAM_B1 ** ADAM_STEP)
    v_hat = v2 / (1.0 - ADAM_B2 ** ADAM_STEP)
    delta = -ADAM_LR * (m_hat / (jnp.sqrt(v_hat) + ADAM_EPS) + ADAM_WD * w)
    return delta, m2, v2


def _adamw(w, g, m, v, name):
    r, c = w.shape
    rb = _row_block(r)

    def body(w_ref, g_ref, m_ref, v_ref, d_ref, m2_ref, v2_ref):
        d, m2, v2 = _adamw_math(w_ref[...], g_ref[...], m_ref[...], v_ref[...])
        d_ref[...] = d
        m2_ref[...] = m2
        v2_ref[...] = v2

    blk = pl.BlockSpec((rb, c), lambda i: (i, 0))
    return pl.pallas_call(body, name=name, grid=(r // rb,), in_specs=[blk] * 4, out_specs=[blk] * 3,
                          out_shape=[_sds((r, c), F32)] * 3, compiler_params=_params("parallel"))(w, g, m, v)


def _reduce_adamw(grad, from_sibling, from_chips, sel, w, m, v, name):
    r, c = w.shape
    cp = c
    rb = _row_block(r)

    def body(sel_ref, g_ref, s_ref, c0_ref, c1_ref, c2_ref, w_ref, m_ref, v_ref, go_ref, d_ref, m2_ref, v2_ref):
        del sel_ref
        g = g_ref[...].astype(F32) + s_ref[...].astype(F32)
        g = g + c0_ref[...].astype(F32)
        g = g + c1_ref[...].astype(F32)
        g = g + c2_ref[...].astype(F32)
        d, m2, v2 = _adamw_math(w_ref[...], g, m_ref[...], v_ref[...])
        go_ref[...] = g
        d_ref[...] = d
        m2_ref[...] = m2
        v2_ref[...] = v2

    blk = pl.BlockSpec((rb, c), lambda i, sel: (i, 0))
    grid_spec = pltpu.PrefetchScalarGridSpec(
        num_scalar_prefetch=1, grid=(r // rb,),
        in_specs=[pl.BlockSpec((None, rb, cp), lambda i, sel: (sel[0], i, 0)),
                  pl.BlockSpec((None, rb, cp), lambda i, sel: (sel[1], i, 0)),
                  pl.BlockSpec((None, rb, cp), lambda i, sel: (0, i, 0)),
                  pl.BlockSpec((None, rb, cp), lambda i, sel: (1, i, 0)),
                  pl.BlockSpec((None, rb, cp), lambda i, sel: (2, i, 0)),
                  blk, blk, blk],
        out_specs=[blk] * 4)
    return pl.pallas_call(body, name=name, grid_spec=grid_spec, out_shape=[_sds((r, c), F32)] * 4,
                          compiler_params=_params("parallel"))(sel, grad, from_sibling, from_chips, from_chips,
                                                               from_chips, w, m, v)


def _wada_grad(c_all, dmod_piece):
    d = c_all.shape[1]
    n = dmod_piece.shape[1]

    def body(c_ref, dm_ref, o_ref):
        o_ref[...] = _dot_tn(c_ref[...], dm_ref[...])

    return pl.pallas_call(body, name="ada_wgrad", out_shape=_sds((d, n), F32),
                          in_specs=[_vmem()] * 2, out_specs=_vmem())(c_all, dmod_piece)


def _column_chunks(width):
    for n in (4, 2):
        if width % (n * LANES) == 0:
            return n
    return 1


def _conv_taps(ref, col):
    return ref[0:1, col], ref[1:2, col], ref[2:3, col]


def _gelu_parts(u):
    th = jnp.tanh(GELU_C0 * (u + GELU_C1 * (u * u * u)))
    cdf = 0.5 * (1.0 + th)
    return cdf, th


def _conv3_bwd(dv, carry, col, taps, x):
    halo = _halo_bottom(dv[:16, :], carry[:, col])
    carry[:, col] = dv[:16, :]
    d1 = _shift_up(dv, halo, 1)
    d2 = _shift_up(dv, halo, 2)
    w0, w1, w2 = taps
    dx = w2 * dv
    dx = dx + w1 * d1
    dx = dx + w0 * d2
    return dx, (_colsum(d2 * x), _colsum(d1 * x), _colsum(dv * x))


def _prenorm(x, vp_ref):
    r = lax.rsqrt(_rowmean(x * x) + EPS)
    nh = x * r
    return (nh * vp_ref[0:1, :]) * vp_ref[1:2, :] + vp_ref[2:3, :], r, nh


def _prenorm_bwd(dh, r, nh, vp_ref, red_ref):
    g, sc1 = vp_ref[0:1, :], vp_ref[1:2, :]
    red_ref[0:1, :] += _colsum(dh)
    red_ref[1:2, :] += _colsum(dh * (nh * g))
    red_ref[2:3, :] += _colsum(dh * nh * sc1)
    dnh = dh * g * sc1
    return r * (dnh - nh * _rowmean(dnh * nh))


def _postnorm_bwd(dres, z, gate, gpost, red_ref):
    r = lax.rsqrt(_rowmean(z * z) + EPS)
    nh = z * r
    dn = dres * gate
    red_ref[0:1, :] += _colsum(dn * nh)
    red_ref[1:2, :] += _colsum(dres * (nh * gpost))
    dnh = dn * gpost
    return r * (dnh - nh * _rowmean(dnh * nh))


def _mixer_block_fwd(x, vec_pre, vec, w_in, w_pool, w_bout, w_o, job):
    t, d = x.shape
    tm = GROUP
    gw = d // len(POOL_WINDOWS)
    pool_rows = 8 * (POOL_WINDOWS[-1] - 1)

    def body(x_ref, vp_ref, vec_ref, win_ref, wp_ref, wb_ref, wo_ref,
             hb_ref, p5_ref, qm_ref, cv_ref, yar_ref, yb_ref, o_ref, x1_ref, mbuf, ucarry, pcarry):
        i = pl.program_id(0)

        @pl.when(i == 0)
        def _():
            ucarry[...] = jnp.zeros_like(ucarry)
            pcarry[...] = jnp.zeros_like(pcarry)

        xp = _interleave(x_ref[...])
        hb = _prenorm(xp, vp_ref)[0].astype(BF16)
        hb_ref[...] = hb
        proj = lambda k: _dot(hb, win_ref[:, k * d:(k + 1) * d])

        za = proj(4)
        p5_ref[:, 3 * d:4 * d] = za.astype(BF16)
        sa = jax.nn.sigmoid(za)
        u_pool = proj(0)
        for g, window in enumerate(POOL_WINDOWS):
            cols = slice(g * gw, (g + 1) * gw)
            rows = 8 * (window - 1)
            u = u_pool[:, cols]
            halo = _halo_top(u[tm - rows:, :], ucarry[pool_rows - rows:, cols])
            s, shift = jnp.concatenate([halo, u], axis=0), 1
            while shift < window:
                s = s[8 * shift:, :] + s[:s.shape[0] - 8 * shift, :]
                shift *= 2
            pgb = (s * _inv_count(i * tm, window) - u).astype(BF16)
            qm_ref[:, 2 * d + g * gw:2 * d + (g + 1) * gw] = pgb
            yar = _dot(pgb, wp_ref[g])
            yar_ref[:, cols] = yar
            mbuf[:, cols] = sa[:, cols] * (yar * vec_ref[2:3, cols])
        ucarry[...] = u_pool[tm - pool_rows:, :]

        ux = proj(1)
        uc = proj(3)
        p5_ref[:, 0:d] = ux.astype(BF16)
        p5_ref[:, 2 * d:3 * d] = uc.astype(BF16)
        p = uc * ux
        halo = _halo_top(p[tm - 16:, :], pcarry[...])
        pcarry[...] = p[tm - 16:, :]
        cv = vec_ref[3:4, :] + vec_ref[4:5, :] * _shift_down(p, halo, 2)
        cv = cv + vec_ref[5:6, :] * _shift_down(p, halo, 1)
        cv = cv + vec_ref[6:7, :] * p
        cv_ref[...] = cv
        ub = proj(2)
        p5_ref[:, d:2 * d] = ub.astype(BF16)
        qb = (ub * cv).astype(BF16)
        qm_ref[:, 0:d] = qb
        yb = _dot(qb, wb_ref[...])
        yb_ref[...] = yb

        zb = proj(5)
        p5_ref[:, 4 * d:5 * d] = zb.astype(BF16)
        mb = (mbuf[...] + jax.nn.sigmoid(zb) * yb).astype(BF16)
        qm_ref[:, d:2 * d] = mb
        o = _dot(mb, wo_ref[...])
        o_ref[...] = o
        r2 = lax.rsqrt(_rowmean(o * o) + EPS)
        x1_ref[...] = xp + vec_ref[0:1, :] * ((o * r2) * vec_ref[1:2, :])

    row = lambda n: pl.BlockSpec((tm, n), lambda i: (i, 0))
    widths = [d, 5 * d, 3 * d, d, d, d, d, d]
    return _call(
        body, "mixer_block_fwd", (t // tm,), [row(d)] + [_vmem()] * 6, [row(n) for n in widths],
        [_sds((t, n), BF16) for n in widths[:3]] + [_sds((t, n), F32) for n in widths[3:]],
        [pltpu.VMEM((tm, d), F32), pltpu.VMEM((pool_rows, d), F32), pltpu.VMEM((16, d), F32)],
        _params("arbitrary"), (x, vec_pre, vec, w_in, w_pool, w_bout, w_o), job)


def _ffn_block_fwd(x1, target, vec_pre, vec, fcv, w_up, w_down):
    t, d = x1.shape
    tm = GROUP
    fp = w_down.shape[0]
    nch = _column_chunks(fp)
    cw = fp // nch

    def body(x1_ref, tg_ref, vp_ref, vec_ref, fcv_ref, wu_ref, wd_ref,
             hb_ref, upb_ref, upreb_ref, a_ref, ffb_ref, dy_ref, loss_ref, carry):
        i = pl.program_id(0)

        @pl.when(i == 0)
        def _():
            carry[...] = jnp.zeros_like(carry)
            loss_ref[...] = jnp.zeros_like(loss_ref)

        x1 = x1_ref[...]
        hb = _prenorm(x1, vp_ref)[0].astype(BF16)
        hb_ref[...] = hb

        cols = [(slice(j * cw, (j + 1) * cw), slice(fp + j * cw, fp + (j + 1) * cw)) for j in range(nch)]
        up_gate = _dot(hb, wu_ref[:, 0:fp])
        up_val = _dot(hb, wu_ref[:, fp:2 * fp])

        def conv(v, col):
            halo = _halo_top(v[tm - 16:, :], carry[:, col])
            carry[:, col] = v[tm - 16:, :]
            w0, w1, w2 = _conv_taps(fcv_ref, col)
            y = fcv_ref[3:4, col] + w0 * _shift_down(v, halo, 2)
            y = y + w1 * _shift_down(v, halo, 1)
            y = y + w2 * v
            upb_ref[:, col] = y.astype(BF16)
            upreb_ref[:, col] = v.astype(BF16)
            return y

        ff = None
        for j in range(nch):
            gc, vc = cols[j]
            gate = conv(up_gate[:, gc], gc)
            val = conv(up_val[:, gc], vc)
            ab = ((gate * _gelu_parts(gate)[0]) * val).astype(BF16)
            a_ref[:, gc] = ab
            part = _dot(ab, wd_ref[gc, :])
            ff = part if ff is None else ff + part
        ffb_ref[...] = ff.astype(BF16)
        r4 = lax.rsqrt(_rowmean(ff * ff) + EPS)
        y = x1 + vec_ref[0:1, :] * ((ff * r4) * vec_ref[1:2, :])
        e = y - _interleave(tg_ref[...])
        dy_ref[...] = e * (1.0 / d)
        loss_ref[...] += jnp.sum(_rowmean(e * e))

    row = lambda n: pl.BlockSpec((tm, n), lambda i: (i, 0))
    return pl.pallas_call(
        body, name="ffn_block_fwd", grid=(t // tm,),
        in_specs=[row(d), row(d)] + [_vmem()] * 5,
        out_specs=[row(d), row(2 * fp), row(2 * fp), row(fp), row(d), row(d), pl.BlockSpec((8, LANES), lambda i: (0, 0))],
        out_shape=[_sds((t, d), BF16), _sds((t, 2 * fp), BF16), _sds((t, 2 * fp), BF16), _sds((t, fp), BF16),
                   _sds((t, d), BF16), _sds((t, d), F32), _sds((8, LANES), F32)],
        scratch_shapes=[pltpu.VMEM((16, 2 * fp), F32)],
        compiler_params=_params("arbitrary"),
    )(x1, target, vec_pre, vec, fcv, w_up, w_down)


def _ffn_block_bwd(dy, ffb, x1, upb, upreb, vec_pre, vec, fcv, w_up, w_down):
    t, d = dy.shape
    tm = GROUP
    fp = w_down.shape[0]
    nch = _column_chunks(fp)
    cw = fp // nch
    nt = t // tm

    def body(dy_ref, ff_ref, x1_ref, upb_ref, upreb_ref, vp_ref, vec_ref, fcv_ref, wu_ref, wd_ref,
             dff_ref, dup_ref, dx1_ref, red_ref, cred_ref, pred_ref, carry):
        @pl.when(pl.program_id(0) == 0)
        def _():
            carry[...] = jnp.zeros_like(carry)
            red_ref[...] = jnp.zeros_like(red_ref)
            cred_ref[...] = jnp.zeros_like(cred_ref)
            pred_ref[...] = jnp.zeros_like(pred_ref)

        dy_v = dy_ref[...]
        dffb = _postnorm_bwd(dy_v, ff_ref[...].astype(F32), vec_ref[0:1, :], vec_ref[1:2, :], red_ref).astype(BF16)
        dff_ref[...] = dffb

        def conv_bwd(dv, col):
            dx, (t0, t1, t2) = _conv3_bwd(dv, carry, col, _conv_taps(fcv_ref, col), upreb_ref[:, col].astype(F32))
            cred_ref[0:1, col] += t0
            cred_ref[1:2, col] += t1
            cred_ref[2:3, col] += t2
            cred_ref[3:4, col] += _colsum(dv)
            dxb = dx.astype(BF16)
            dup_ref[:, col] = dxb
            return _dot_nt(dxb, wu_ref[:, col])

        dh = None
        da_full = _dot_nt(dffb, wd_ref[...])
        for j in range(nch):
            gc = slice(j * cw, (j + 1) * cw)
            vc = slice(fp + j * cw, fp + (j + 1) * cw)
            da = da_full[:, gc]
            gate = upb_ref[:, gc].astype(F32)
            val = upb_ref[:, vc].astype(F32)
            cdf, th = _gelu_parts(gate)
            dcdf = 0.5 * (1.0 - th * th) * (GELU_C0 * (1.0 + (3.0 * GELU_C1) * (gate * gate)))
            part = conv_bwd(da * val * (cdf + gate * dcdf), gc) + conv_bwd(da * (gate * cdf), vc)
            dh = part if dh is None else dh + part

        _, r, nh = _prenorm(x1_ref[...], vp_ref)
        dx1_ref[...] = dy_v + _prenorm_bwd(dh, r, nh, vp_ref, pred_ref)

    rev = lambda n: pl.BlockSpec((tm, n), lambda i: (nt - 1 - i, 0))
    fixed = lambda n: pl.BlockSpec((8, n), lambda i: (0, 0))
    return pl.pallas_call(
        body, name="ffn_block_bwd", grid=(nt,),
        in_specs=[rev(d), rev(d), rev(d), rev(2 * fp), rev(2 * fp)] + [_vmem()] * 5,
        out_specs=[rev(d), rev(2 * fp), rev(d), fixed(d), fixed(2 * fp), fixed(d)],
        out_shape=[_sds((t, d), BF16), _sds((t, 2 * fp), BF16), _sds((t, d), F32), _sds((8, d), F32),
                   _sds((8, 2 * fp), F32), _sds((8, d), F32)],
        scratch_shapes=[pltpu.VMEM((16, 2 * fp), F32)],
        compiler_params=_params("arbitrary"),
    )(dy, ffb, x1, upb, upreb, vec_pre, vec, fcv, w_up, w_down)


def _mixer_block_bwd(dx1, ob, yarb, ybb, cvb, p5b, x, vec_pre, vec, w_in, w_pool, w_bout, w_o, job):
    t, d = dx1.shape
    tm = GROUP
    gw = d // len(POOL_WINDOWS)
    nt = t // tm
    pool_rows = 8 * (POOL_WINDOWS[-1] - 1)

    def body(dx1_ref, o_ref, yar_ref, yb_ref, cv_ref, p5_ref, x_ref, vp_ref, vec_ref, win_ref, wp_ref, wb_ref, wo_ref,
             dqm_ref, dp_ref, gx_ref, red_ref, pred_ref, dpgcarry, dcvcarry):
        i = pl.program_id(0)
        tix = nt - 1 - i

        @pl.when(i == 0)
        def _():
            red_ref[...] = jnp.zeros_like(red_ref)
            pred_ref[...] = jnp.zeros_like(pred_ref)
            dpgcarry[...] = jnp.zeros_like(dpgcarry)
            dcvcarry[...] = jnp.zeros_like(dcvcarry)

        pscale = vec_ref[2:3, :]
        dx1_v = dx1_ref[...]
        dob = _postnorm_bwd(dx1_v, o_ref[...].astype(F32), vec_ref[0:1, :], vec_ref[1:2, :], red_ref).astype(BF16)
        dqm_ref[:, d:2 * d] = dob
        dm = _dot_nt(dob, wo_ref[...])

        def dproj(cols, value):
            vb = value.astype(BF16)
            dp_ref[:, cols] = vb
            return _dot_nt(vb, win_ref[:, cols])

        sa = jax.nn.sigmoid(p5_ref[:, 3 * d:4 * d].astype(F32))
        yar = yar_ref[...].astype(F32)
        dya = dm * sa
        dh = dproj(slice(4 * d, 5 * d), dm * (yar * pscale) * sa * (1.0 - sa))
        red_ref[2:3, :] += _colsum(dya * yar)
        dyarb = (dya * pscale).astype(BF16)
        dqm_ref[:, 2 * d:3 * d] = dyarb
        sb = jax.nn.sigmoid(p5_ref[:, 4 * d:5 * d].astype(F32))
        dybb = (dm * sb).astype(BF16)
        dqm_ref[:, 0:d] = dybb
        dh = dh + dproj(slice(5 * d, 6 * d), dm * yb_ref[...].astype(F32) * sb * (1.0 - sb))

        for g, window in enumerate(POOL_WINDOWS):
            cols = slice(g * gw, (g + 1) * gw)
            rows = 8 * (window - 1)
            dpg = _dot_nt(dyarb[:, cols], wp_ref[g])
            dpgs = dpg * _inv_count(tix * tm, window)
            halo = _halo_bottom(dpgs[:rows, :], dpgcarry[:rows, cols])
            dpgcarry[:, cols] = dpgs[:pool_rows, :]
            s, shift = jnp.concatenate([dpgs, halo], axis=0), 1
            while shift < window:
                s = s[:s.shape[0] - 8 * shift, :] + s[8 * shift:, :]
                shift *= 2
            dh = dh + dproj(cols, s - dpg)

        dq = _dot_nt(dybb, wb_ref[...])
        ux = p5_ref[:, 0:d].astype(F32)
        uc = p5_ref[:, 2 * d:3 * d].astype(F32)
        dh = dh + dproj(slice(2 * d, 3 * d), dq * cv_ref[...].astype(F32))
        dcv = dq * p5_ref[:, d:2 * d].astype(F32)
        taps = (vec_ref[4:5, :], vec_ref[5:6, :], vec_ref[6:7, :])
        dpv, (t0, t1, t2) = _conv3_bwd(dcv, dcvcarry, slice(0, d), taps, uc * ux)
        red_ref[3:4, :] += _colsum(dcv)
        red_ref[4:5, :] += t0
        red_ref[5:6, :] += t1
        red_ref[6:7, :] += t2
        dh = dh + dproj(slice(d, 2 * d), dpv * uc)
        dh = dh + dproj(slice(3 * d, 4 * d), dpv * ux)

        _, r, nh = _prenorm(_interleave(x_ref[...]), vp_ref)
        gx_ref[...] = _deinterleave(dx1_v + _prenorm_bwd(dh, r, nh, vp_ref, pred_ref))

    rev = lambda n: pl.BlockSpec((tm, n), lambda i: (nt - 1 - i, 0))
    return _call(
        body, "mixer_block_bwd", (nt,), [rev(d)] * 5 + [rev(5 * d), rev(d)] + [_vmem()] * 6,
        [rev(3 * d), rev(6 * d), rev(d), pl.BlockSpec((16, d), lambda i: (0, 0)),
         pl.BlockSpec((8, d), lambda i: (0, 0))],
        [_sds((t, 3 * d), BF16), _sds((t, 6 * d), BF16), _sds((t, d), F32), _sds((16, d), F32), _sds((8, d), F32)],
        [pltpu.VMEM((pool_rows, d), F32), pltpu.VMEM((16, d), F32)],
        _params("arbitrary"), (dx1, ob, yarb, ybb, cvb, p5b, x, vec_pre, vec, w_in, w_pool, w_bout, w_o), job)


def _matmul_tn(a, b, bm, bn, tk, by_col_block, name, job=None):
    t, m = a.shape
    n = b.shape[1]
    nk = t // tk
    parts = int(by_col_block)
    piece = bn // max(parts, 1)

    def body(a_ref, b_ref, o_ref, acc_ref):
        k = pl.program_id(2)

        @pl.when(k == 0)
        def _():
            acc_ref[...] = jnp.zeros_like(acc_ref)

        acc_ref[...] += _dot_tn(a_ref[...], b_ref[...])

        @pl.when(k == nk - 1)
        def _():
            if parts:
                acc = acc_ref[...]
                for p in range(parts):
                    o_ref[p] = acc[:, p * piece:(p + 1) * piece].astype(o_ref.dtype)
            else:
                o_ref[...] = acc_ref[...].astype(o_ref.dtype)

    if by_col_block:
        out_shape = _sds((parts * n // bn, m, piece), BF16)
        out_spec = pl.BlockSpec((parts, bm, piece), lambda i, j, k: (j, i, 0))
    else:
        out_shape = _sds((m, n), BF16)
        out_spec = pl.BlockSpec((bm, bn), lambda i, j, k: (i, j))
    out = _call(body, name, (m // bm, n // bn, nk),
                [pl.BlockSpec((tk, bm), lambda i, j, k: (k, i)), pl.BlockSpec((tk, bn), lambda i, j, k: (k, j))],
                [out_spec], [out_shape], [pltpu.VMEM((bm, bn), F32)],
                _params("arbitrary", "arbitrary", "arbitrary"), (a, b), job)
    return out if job is not None else out[0]


def _matmul_tn_groups(a, b, groups, tk, name, job=None):
    t, m = a.shape
    w = m // groups
    nk = t // tk

    def body(a_ref, b_ref, o_ref, acc_ref):
        k = pl.program_id(1)

        @pl.when(k == 0)
        def _():
            acc_ref[...] = jnp.zeros_like(acc_ref)

        acc_ref[...] += _dot_tn(a_ref[...], b_ref[...])

        @pl.when(k == nk - 1)
        def _():
            o_ref[...] = acc_ref[...].astype(o_ref.dtype)

    blk = pl.BlockSpec((tk, w), lambda g, k: (k, g))
    out = _call(body, name, (groups, nk), [blk, blk], [pl.BlockSpec((None, w, w), lambda g, k: (g, 0, 0))],
                [_sds((groups, w, w), BF16)], [pltpu.VMEM((w, w), F32)], _params("arbitrary", "arbitrary"), (a, b), job)
    return out if job is not None else out[0]


def _round_up(n, k):
    return (n + k - 1) // k * k


def _rows8(rows, width):
    n = _round_up(len(rows), 8)
    rows = list(rows) + [jnp.zeros((1, width), F32)] * (n - len(rows))
    return jnp.concatenate(rows, axis=0)


def kernel(x, c, g_pre_mix, g_post_mix, g_pre_ffn, g_post_ffn, w_ada, b_ada, w_in, w_pool, pool_scale, conv_w, conv_b, w_bout, w_o, w_up, ffn_conv_w, ffn_conv_b, w_down, loss_target, m_g_pre_mix, m_g_post_mix, m_g_pre_ffn, m_g_post_ffn, m_w_ada, m_b_ada, m_w_in, m_w_pool, m_pool_scale, m_conv_w, m_conv_b, m_w_bout, m_w_o, m_w_up, m_ffn_conv_w, m_ffn_conv_b, m_w_down, v_g_pre_mix, v_g_post_mix, v_g_pre_ffn, v_g_post_ffn, v_w_ada, v_b_ada, v_w_in, v_w_pool, v_pool_scale, v_conv_w, v_conv_b, v_w_bout, v_w_o, v_w_up, v_ffn_conv_w, v_ffn_conv_b, v_w_down):
    t, d = x.shape[1], x.shape[2]
    ngroups = len(POOL_WINDOWS)
    gw = d // ngroups
    ada_n = w_ada.shape[2]
    in_n = w_in.shape[2]
    up_n = w_up.shape[2]
    fp = NDEV * w_down.shape[1]

    xi, yi, ci = _position()
    me = _linear(xi, yi, ci)
    chip = 2 * xi + yi
    core = jnp.reshape(ci, (1,)).astype(jnp.int32)
    sel = jnp.stack([2 * chip + ci, chip]).astype(jnp.int32)

    x2 = x.reshape(t, d)
    target = loss_target.reshape(t, d)

    cw_n = conv_w.shape[2]
    pack = jnp.concatenate([c.reshape(1, d), conv_w[0].reshape(1, 3 * cw_n), ffn_conv_w[0].reshape(1, 3 * up_n)], axis=1)
    pack = jnp.pad(pack, ((0, 0), (0, _round_up(pack.shape[1], LANES) - pack.shape[1])))
    b_piece = lax.dynamic_slice_in_dim(b_ada, me * ada_n, ada_n, axis=1)
    mixer_weights = _allgather_job(
        [w_in[0].astype(BF16), w_bout[0].astype(BF16), w_o[0].astype(BF16), w_pool[0].astype(BF16)],
        ["cols", "rows", "rows", "mid"], 0.5)
    gathered, mod_rows, w_in_f, g_bout, g_o, w_pool_f = _gather_weights_and_modulation(
        mixer_weights, pack, w_ada[0], b_piece, d)
    w_bout_f = g_bout.reshape(d, d)
    w_o_f = g_o.reshape(d, d)
    c_all = gathered[:, :d]
    c16 = jnp.pad(c_all, ((0, 8), (0, 0))).astype(BF16)
    conv_w_full = gathered[:, d:d + 3 * cw_n].reshape(NDEV, 3, cw_n).transpose(1, 0, 2).reshape(3, NDEV * cw_n)
    fcw_full = gathered[:, d + 3 * cw_n:d + 3 * cw_n + 3 * up_n].reshape(NDEV, 3, up_n)
    fcw_full = fcw_full.transpose(1, 0, 2).reshape(3, 2 * fp)
    fcv = jnp.concatenate([fcw_full, ffn_conv_b, jnp.zeros((4, 2 * fp), F32)], axis=0)
    mod = mod_rows.reshape(1, NDEV * ada_n)
    sh1, sc1, gt1, sh2, sc2, gt2 = [mod[:, k * d:(k + 1) * d] for k in range(6)]

    ffn_weights = _allgather_job([w_up[0].astype(BF16), w_down[0].astype(BF16)], ["rows", "rows"], 0.75)
    vec_pre_mix = _rows8([g_pre_mix, 1.0 + sc1, sh1], d)
    vec_mix = _rows8([gt1, g_post_mix, pool_scale, conv_b, conv_w_full[0:1], conv_w_full[1:2], conv_w_full[2:3]], d)
    h1b, p5b, qmb, cvb, yarb, ybb, ob, x1, g_up, g_down = _mixer_block_fwd(
        x2, vec_pre_mix, vec_mix, w_in_f, w_pool_f, w_bout_f, w_o_f, ffn_weights)
    w_up_f = g_up.transpose(1, 0, 2).reshape(d, 2 * fp)
    w_down_f = g_down.reshape(fp, d)
    vec_pre_ffn = _rows8([g_pre_ffn, 1.0 + sc2, sh2], d)
    vec_ffn = _rows8([gt2, g_post_ffn], d)
    h2b, upb, upreb, ab, ffb, dy, loss_part = _ffn_block_fwd(x1, target, vec_pre_ffn, vec_ffn, fcv, w_up_f, w_down_f)

    tk = min(2048, t)
    chip_sum = lambda gs, ss: [_chip_sum(g, s, core) for g, s in zip(gs, ss)]
    dffb, dupre, dx1, red_ffn, red_fconv, red_pre_ffn = _ffn_block_bwd(
        dy, ffb, x1, upb, upreb, vec_pre_ffn, vec_ffn, fcv, w_up_f, w_down_f)
    chunk = fp // _column_chunks(fp)
    gr_up = _matmul_tn(h2b, dupre, d, chunk, tk, chunk // up_n, "wgrad_up")
    gr_down = _matmul_tn(ab, dffb, chunk, d, tk, False, "wgrad_down").reshape(NDEV, fp // NDEV, d)
    sib_ffn = _run_job(_sibling_job([gr_up, gr_down]), "rs_sibling_ffn")
    dqmb, dproj, grad_x, red_mix, red_pre_mix = _mixer_block_bwd(
        dx1, ob, yarb, ybb, cvb, p5b, x2, vec_pre_mix, vec_mix, w_in_f, w_pool_f, w_bout_f, w_o_f, None)
    gr_in, fc_up, fc_down = _matmul_tn(h1b, dproj, d, in_n, tk, True, "wgrad_in",
                                       _chips_job(chip_sum([gr_up, gr_down], sib_ffn)))
    sib_in = _run_job(_sibling_job([gr_in]), "rs_sibling_in")
    gr_qmp, fc_in = _matmul_tn_groups(qmb, dqmb, 3, tk, "wgrad_bout_o_pool", _chips_job(chip_sum([gr_in], sib_in)))
    gr_bout = gr_qmp[0].reshape(NDEV, d // NDEV, d)
    gr_o = gr_qmp[1].reshape(NDEV, d // NDEV, d)
    gr_pool = jnp.stack([gr_qmp[2, g * gw:(g + 1) * gw, g * gw:(g + 1) * gw] for g in range(ngroups)])
    gr_pool = gr_pool.reshape(ngroups, NDEV, gw // NDEV, gw).transpose(1, 0, 2, 3).reshape(NDEV, -1, gw)
    rest = [gr_bout, gr_o, gr_pool]
    sib_rest = _run_job(_sibling_job(rest), "rs_sibling_rest")

    dmod = [red_pre_mix[0:1], red_pre_mix[1:2], red_mix[1:2], red_pre_ffn[0:1], red_pre_ffn[1:2], red_ffn[1:2]]
    small = [red_pre_mix[2:3], red_mix[0:1], red_pre_ffn[2:3], red_ffn[0:1], red_mix[2:3], red_mix[3:4],
             red_mix[4:5], red_mix[5:6], red_mix[6:7]] + dmod
    flat = jnp.concatenate(small + [red_fconv[0:4].reshape(1, 8 * fp), loss_part[0:1, 0:1]], axis=1)
    flat_n = flat.shape[1]
    width = 8 * LANES
    rows = _round_up(-(-flat_n // width), 8)
    flat = jnp.pad(flat, ((0, 0), (0, rows * width - flat_n))).reshape(rows, width)
    gat, tot, *fc_rest = _small_allreduce(flat, "allreduce_small_rs_rest", _chips_job(chip_sum(rest, sib_rest)))

    def big(grad, from_sibling, from_chips, w, m, v, name):
        shape = w.shape
        w2, m2, v2 = [a.reshape((-1, shape[-1])) for a in (w, m, v)]
        outs = _reduce_adamw(grad, from_sibling, from_chips, sel, w2, m2, v2, name)
        return [a.reshape(shape) for a in outs]

    g_w_up, d_w_up, nm_w_up, nv_w_up = big(gr_up, sib_ffn[0], fc_up, w_up, m_w_up, v_w_up, "adamw_up")
    g_w_down, d_w_down, nm_w_down, nv_w_down = big(gr_down, sib_ffn[1], fc_down, w_down, m_w_down, v_w_down, "adamw_down")
    g_w_in, d_w_in, nm_w_in, nv_w_in = big(gr_in, sib_in[0], fc_in, w_in, m_w_in, v_w_in, "adamw_in")
    g_w_bout, d_w_bout, nm_w_bout, nv_w_bout = big(gr_bout, sib_rest[0], fc_rest[0], w_bout, m_w_bout, v_w_bout, "adamw_bout")
    g_w_o, d_w_o, nm_w_o, nv_w_o = big(gr_o, sib_rest[1], fc_rest[1], w_o, m_w_o, v_w_o, "adamw_o")
    g_w_pool, d_w_pool, nm_w_pool, nv_w_pool = big(gr_pool, sib_rest[2], fc_rest[2], w_pool, m_w_pool, v_w_pool, "adamw_pool")

    tot = tot.reshape(1, rows * width)
    gat = gat.reshape(NDEV, rows * width)
    take = lambda k: tot[:, k * d:(k + 1) * d]
    g_g_pre_mix, g_g_post_mix, g_g_pre_ffn, g_g_post_ffn, g_pool_scale, g_conv_b = [take(k) for k in range(6)]
    g_conv_w_full = jnp.concatenate([take(6), take(7), take(8)], axis=0)
    g_conv_w = lax.dynamic_slice_in_dim(g_conv_w_full, me * cw_n, cw_n, axis=1)
    g_b_ada = tot[:, 9 * d:15 * d]
    dmod_all = gat[:, 9 * d:15 * d]
    fconv_tot = tot[:, 15 * d:15 * d + 8 * fp].reshape(4, 2 * fp)
    loss = 0.5 * tot[0, 15 * d + 8 * fp]
    g_ffn_conv_b = fconv_tot[3:4]
    g_ffn_conv_w = lax.dynamic_slice_in_dim(fconv_tot[0:3], me * up_n, up_n, axis=1)
    dmod_piece = lax.dynamic_slice_in_dim(dmod_all, me * ada_n, ada_n, axis=1)
    g_w_ada = _wada_grad(c16, jnp.pad(dmod_piece, ((0, 8), (0, 0))).astype(BF16))

    names_small = [(g_pre_mix, g_g_pre_mix, m_g_pre_mix, v_g_pre_mix), (g_post_mix, g_g_post_mix, m_g_post_mix, v_g_post_mix),
                   (g_pre_ffn, g_g_pre_ffn, m_g_pre_ffn, v_g_pre_ffn), (g_post_ffn, g_g_post_ffn, m_g_post_ffn, v_g_post_ffn),
                   (b_ada, g_b_ada, m_b_ada, v_b_ada), (pool_scale, g_pool_scale, m_pool_scale, v_pool_scale),
                   (conv_w, g_conv_w, m_conv_w, v_conv_w), (conv_b, g_conv_b, m_conv_b, v_conv_b),
                   (ffn_conv_w, g_ffn_conv_w, m_ffn_conv_w, v_ffn_conv_w), (ffn_conv_b, g_ffn_conv_b, m_ffn_conv_b, v_ffn_conv_b)]
    sizes = [w.size for w, _, _, _ in names_small]
    total = sum(sizes)
    prow = _round_up(-(-total // width), 8)

    def pack_small(k):
        a = jnp.concatenate([q[k].reshape(1, -1) for q in names_small], axis=1)
        return jnp.pad(a, ((0, 0), (0, prow * width - total)), constant_values=1.0).reshape(prow, width)

    ds, ms, vs = _adamw(pack_small(0), pack_small(1), pack_small(2), pack_small(3), "adamw_small")

    def unpack_small(a):
        a = a.reshape(-1)
        out, off = [], 0
        for (w, _, _, _), n in zip(names_small, sizes):
            out.append(a[off:off + n].reshape(w.shape))
            off += n
        return out

    (d_g_pre_mix, d_g_post_mix, d_g_pre_ffn, d_g_post_ffn, d_b_ada, d_pool_scale, d_conv_w, d_conv_b,
     d_ffn_conv_w, d_ffn_conv_b) = unpack_small(ds)
    (nm_g_pre_mix, nm_g_post_mix, nm_g_pre_ffn, nm_g_post_ffn, nm_b_ada, nm_pool_scale, nm_conv_w, nm_conv_b,
     nm_ffn_conv_w, nm_ffn_conv_b) = unpack_small(ms)
    (nv_g_pre_mix, nv_g_post_mix, nv_g_pre_ffn, nv_g_post_ffn, nv_b_ada, nv_pool_scale, nv_conv_w, nv_conv_b,
     nv_ffn_conv_w, nv_ffn_conv_b) = unpack_small(vs)
    d_w_ada, nm_w_ada, nv_w_ada = [a.reshape(w_ada.shape) for a in
                                   _adamw(w_ada[0], g_w_ada, m_w_ada[0], v_w_ada[0], "adamw_ada")]

    grads = [g_g_pre_mix, g_g_post_mix, g_g_pre_ffn, g_g_post_ffn, g_w_ada.reshape(w_ada.shape), g_b_ada, g_w_in,
             g_w_pool, g_pool_scale, g_conv_w.reshape(conv_w.shape), g_conv_b, g_w_bout, g_w_o, g_w_up,
             g_ffn_conv_w.reshape(ffn_conv_w.shape), g_ffn_conv_b, g_w_down]
    deltas = [d_g_pre_mix, d_g_post_mix, d_g_pre_ffn, d_g_post_ffn, d_w_ada, d_b_ada, d_w_in, d_w_pool, d_pool_scale,
              d_conv_w, d_conv_b, d_w_bout, d_w_o, d_w_up, d_ffn_conv_w, d_ffn_conv_b, d_w_down]
    new_m = [nm_g_pre_mix, nm_g_post_mix, nm_g_pre_ffn, nm_g_post_ffn, nm_w_ada, nm_b_ada, nm_w_in, nm_w_pool,
             nm_pool_scale, nm_conv_w, nm_conv_b, nm_w_bout, nm_w_o, nm_w_up, nm_ffn_conv_w, nm_ffn_conv_b, nm_w_down]
    new_v = [nv_g_pre_mix, nv_g_post_mix, nv_g_pre_ffn, nv_g_post_ffn, nv_w_ada, nv_b_ada, nv_w_in, nv_w_pool,
             nv_pool_scale, nv_conv_w, nv_conv_b, nv_w_bout, nv_w_o, nv_w_up, nv_ffn_conv_w, nv_ffn_conv_b, nv_w_down]
    return (loss, grad_x.reshape(x.shape), *grads, *deltas, *new_m, *new_v)
```

```python
import math

import jax
import jax.numpy as jnp
from jax import lax
from jax.experimental import pallas as pl
from jax.experimental.pallas import tpu as pltpu

F32 = jnp.float32
BF16 = jnp.bfloat16
MESH = pl.DeviceIdType.MESH

NDEV = 8
NCHIP = 4
EPS = 1e-6
POOL_WINDOWS = (2, 4, 8, 16)
LANES = 128
ADAM_LR = 0.001
ADAM_B1 = 0.9
ADAM_B2 = 0.999
ADAM_EPS = 1e-08
ADAM_WD = 0.01
ADAM_STEP = 10
GELU_C0 = math.sqrt(2.0 / math.pi)
GELU_C1 = 0.044715
VMEM_LIMIT = 56 * 2**20


def _vmem():
    return pl.BlockSpec(memory_space=pltpu.VMEM)


def _any():
    return pl.BlockSpec(memory_space=pl.ANY)


def _params(*sem):
    return pltpu.CompilerParams(dimension_semantics=sem, vmem_limit_bytes=VMEM_LIMIT)


def _sds(shape, dtype):
    return jax.ShapeDtypeStruct(tuple(shape), dtype)


def _position():
    return lax.axis_index("x"), lax.axis_index("y"), lax.axis_index("c")


def _linear(x, y, c):
    return 4 * x + 2 * y + c


def _dot(a, b):
    return jnp.dot(a, b, preferred_element_type=F32)


def _dot_nt(a, b):
    return lax.dot_general(a, b, (((1,), (1,)), ((), ())), preferred_element_type=F32)


def _dot_tn(a, b):
    return lax.dot_general(a, b, (((0,), (0,)), ((), ())), preferred_element_type=F32)


def _colsum(v):
    return jnp.sum(v, axis=0, keepdims=True)


def _rowmean(v):
    return jnp.mean(v, axis=-1, keepdims=True)


GROUP = 256


def _interleave(v):
    g, n = v.shape
    return jnp.swapaxes(v.reshape(8, g // 8, n), 0, 1).reshape(g, n)


def _deinterleave(v):
    g, n = v.shape
    return jnp.swapaxes(v.reshape(g // 8, 8, n), 0, 1).reshape(g, n)


def _halo_top(cur_last, prev_last):
    rows, n = cur_last.shape
    c3 = cur_last.reshape(rows // 8, 8, n)
    p3 = prev_last.reshape(rows // 8, 8, n)
    sub = lax.broadcasted_iota(jnp.int32, c3.shape, 1)
    return jnp.where(sub == 0, pltpu.roll(p3, 1, 1), pltpu.roll(c3, 1, 1)).reshape(rows, n)


def _halo_bottom(cur_first, next_first):
    rows, n = cur_first.shape
    c3 = cur_first.reshape(rows // 8, 8, n)
    n3 = next_first.reshape(rows // 8, 8, n)
    sub = lax.broadcasted_iota(jnp.int32, c3.shape, 1)
    return jnp.where(sub == 7, pltpu.roll(n3, 7, 1), pltpu.roll(c3, 7, 1)).reshape(rows, n)


def _shift_down(v, halo, k):
    rows = v.shape[0]
    return jnp.concatenate([halo[halo.shape[0] - 8 * k:, :], v[:rows - 8 * k, :]], axis=0)


def _shift_up(v, halo, k):
    return jnp.concatenate([v[8 * k:, :], halo[:8 * k, :]], axis=0)


def _inv_count(first_token, window):
    row = lax.broadcasted_iota(jnp.int32, (GROUP, 1), 0)
    t = first_token + (row % 8) * (GROUP // 8) + row // 8
    return 1.0 / jnp.minimum(t + 1, window).astype(F32)


class _Job:
    def __init__(self, inputs, out_shape, scratch, phases):
        self.inputs, self.out_shape, self.scratch, self.phases = list(inputs), list(out_shape), list(scratch), phases


def _call(body, name, grid, in_specs, out_specs, out_shape, scratch_shapes, params, operands, job=None):
    if job is None:
        return pl.pallas_call(body, name=name, grid=grid, in_specs=in_specs, out_specs=out_specs, out_shape=out_shape,
                              scratch_shapes=scratch_shapes, compiler_params=params)(*operands)
    n_in, n_out, n_scr = len(in_specs), len(out_specs), len(scratch_shapes)
    j_in, j_out = len(job.inputs), len(job.out_shape)
    steps = math.prod(grid)

    def hosted(*refs):
        own_in, refs = refs[:n_in], refs[n_in:]
        jin, refs = refs[:j_in], refs[j_in:]
        own_out, refs = refs[:n_out], refs[n_out:]
        jout, refs = refs[:j_out], refs[j_out:]
        own_scr, jscr = refs[:n_scr], refs[n_scr:]
        step = pl.program_id(0)
        for axis in range(1, len(grid)):
            step = step * grid[axis] + pl.program_id(axis)
        for frac, fn in job.phases[:-1]:
            pl.when(step == int(frac * (steps - 1)))(lambda fn=fn: fn(jin, jout, jscr))
        body(*own_in, *own_out, *own_scr)
        pl.when(step == steps - 1)(lambda: job.phases[-1][1](jin, jout, jscr))

    return pl.pallas_call(
        hosted, name=name, grid=grid, in_specs=list(in_specs) + [_any()] * j_in,
        out_specs=list(out_specs) + [_any()] * j_out, out_shape=list(out_shape) + job.out_shape,
        scratch_shapes=list(scratch_shapes) + job.scratch, compiler_params=params)(*operands, *job.inputs)


def _run_job(job, name):
    n_in, n_out = len(job.inputs), len(job.out_shape)

    def body(*refs):
        for _, fn in job.phases:
            fn(refs[:n_in], refs[n_in:n_in + n_out], refs[n_in + n_out:])

    return pl.pallas_call(body, name=name, out_shape=job.out_shape, in_specs=[_any()] * n_in,
                          out_specs=[_any()] * n_out, scratch_shapes=job.scratch)(*job.inputs)


def _peers(x, y, c):
    out = []
    for k in range(1, NDEV):
        out.append(((1 - x) if k & 4 else x, (1 - y) if k & 2 else y, (1 - c) if k & 1 else c))
    return out


def _small_allreduce(v, name, job):
    r, n = v.shape
    j_in, j_out = len(job.inputs), len(job.out_shape)

    def body(v_ref, *rest):
        jin, rest = rest[:j_in], rest[j_in:]
        gat_ref, sum_ref = rest[:2]
        jout, rest = rest[2:2 + j_out], rest[2 + j_out:]
        send_sems, recv_sems, local_sem = rest[:3]
        jscr = rest[3:]
        job.phases[0][1](jin, jout, jscr)
        x, y, c = _position()
        me = _linear(x, y, c)
        mine = pltpu.make_async_copy(v_ref, gat_ref.at[me], local_sem)
        mine.start()
        peers = _peers(x, y, c)
        sends = []
        for k, peer in enumerate(peers):
            cp = pltpu.make_async_remote_copy(src_ref=v_ref, dst_ref=gat_ref.at[me], send_sem=send_sems.at[k],
                                              recv_sem=recv_sems.at[k], device_id=peer, device_id_type=MESH)
            cp.start()
            sends.append(cp)
        for k, peer in enumerate(peers):
            pltpu.make_async_remote_copy(src_ref=v_ref, dst_ref=gat_ref.at[_linear(*peer)], send_sem=send_sems.at[k],
                                         recv_sem=recv_sems.at[k], device_id=peer, device_id_type=MESH).wait_recv()
        for cp in sends:
            cp.wait_send()
        mine.wait()
        acc = gat_ref[0]
        for j in range(1, NDEV):
            acc = acc + gat_ref[j]
        sum_ref[...] = acc
        job.phases[-1][1](jin, jout, jscr)

    return pl.pallas_call(
        body, name=name, out_shape=[_sds((NDEV, r, n), F32), _sds((r, n), F32)] + job.out_shape,
        in_specs=[_vmem()] + [_any()] * j_in, out_specs=[_vmem()] * 2 + [_any()] * j_out,
        scratch_shapes=[pltpu.SemaphoreType.DMA((NDEV - 1,)), pltpu.SemaphoreType.DMA((NDEV - 1,)),
                        pltpu.SemaphoreType.DMA(())] + job.scratch,
    )(v, *job.inputs)


def _exchange_rows(src_for, dst_ref, sems):
    send_sems, recv_sems, local_sem = sems
    x, y, c = _position()
    me = _linear(x, y, c)
    row = lambda j: dst_ref.at[pl.ds(j, 1), :]
    mine = pltpu.make_async_copy(src_for(me), row(me), local_sem)
    mine.start()
    peers = _peers(x, y, c)
    sends = []
    for k, peer in enumerate(peers):
        cp = pltpu.make_async_remote_copy(src_ref=src_for(_linear(*peer)), dst_ref=row(me), send_sem=send_sems.at[k],
                                          recv_sem=recv_sems.at[k], device_id=peer, device_id_type=MESH)
        cp.start()
        sends.append(cp)
    for k, peer in enumerate(peers):
        pltpu.make_async_remote_copy(src_ref=src_for(me), dst_ref=row(_linear(*peer)), send_sem=send_sems.at[k],
                                     recv_sem=recv_sems.at[k], device_id=peer, device_id_type=MESH).wait_recv()
    for cp in sends:
        cp.wait_send()
    mine.wait()


def _gather_weights_and_modulation(job, pack, w_ada, b_piece, d):
    n = pack.shape[1]
    m = w_ada.shape[1]
    j_in, j_out = len(job.inputs), len(job.out_shape)
    row_sems = [pltpu.SemaphoreType.DMA((NDEV - 1,)), pltpu.SemaphoreType.DMA((NDEV - 1,)), pltpu.SemaphoreType.DMA(())]

    def body(pack_ref, wada_ref, bp_ref, *rest):
        jin, rest = rest[:j_in], rest[j_in:]
        gat_ref, mod_ref = rest[:2]
        jout, rest = rest[2:2 + j_out], rest[2 + j_out:]
        sems1, sems2, piece, jscr = rest[0:3], rest[3:6], rest[6], rest[7:]
        phases = [fn for _, fn in job.phases]
        phases[0](jin, jout, jscr)
        _exchange_rows(lambda j: pack_ref, gat_ref, sems1)
        c16 = jnp.concatenate([gat_ref[:, 0:d], jnp.zeros((NDEV, d), F32)], axis=0).astype(BF16)
        piece[...] = (_dot(c16, wada_ref[...].astype(BF16)) + bp_ref[...])[0:NDEV, :]
        _exchange_rows(lambda j: piece.at[pl.ds(j, 1), :], mod_ref, sems2)
        for fn in phases[1:]:
            fn(jin, jout, jscr)

    return pl.pallas_call(
        body, name="gather_weights_and_modulation",
        out_shape=[_sds((NDEV, n), F32), _sds((NDEV, m), F32)] + job.out_shape,
        in_specs=[_vmem()] * 3 + [_any()] * j_in, out_specs=[_vmem()] * 2 + [_any()] * j_out,
        scratch_shapes=row_sems + row_sems + [pltpu.VMEM((NDEV, m), F32)] + job.scratch,
    )(pack, w_ada, b_piece, *job.inputs)


def _gathered(shard, layout):
    if layout == "rows":
        return (NDEV,) + shard.shape, lambda ref, j: ref.at[j]
    if layout == "cols":
        r, c = shard.shape
        return (r, NDEV * c), lambda ref, j: ref.at[:, pl.ds(pl.multiple_of(j * c, LANES), c)]
    g, r, c = shard.shape
    return (g, NDEV * r, c), lambda ref, j: ref.at[:, pl.ds(pl.multiple_of(j * r, 16), r), :]


def _allgather_job(shards, layouts, relay_at, forward_at):
    n = len(shards)
    specs = [_gathered(s, l) for s, l in zip(shards, layouts)]
    halves = [s.shape[0] // 2 for s in shards]

    def plan(src, dst, sems):
        send_sems, recv_sems, _ = sems
        x, y, c = _position()
        me, sibling, xn, yn, dg = (x, y, c), (x, y, 1 - c), (1 - x, y, c), (x, 1 - y, c), (1 - x, 1 - y, c)

        def copy(a, k, block, to, half=None, from_src=False):
            blk = specs[a][1](dst[a], _linear(*block))
            if half is not None:
                blk = blk.at[pl.ds(half * halves[a], halves[a])]
            return pltpu.make_async_remote_copy(src_ref=src[a] if from_src else blk, dst_ref=blk,
                                                send_sem=send_sems.at[a, k], recv_sem=recv_sems.at[a, k],
                                                device_id=to, device_id_type=MESH)
        return copy, me, sibling, xn, yn, dg

    def local(src, dst, sems):
        x, y, c = _position()
        return [pltpu.make_async_copy(src[a], specs[a][1](dst[a], _linear(x, y, c)), sems[2].at[a]) for a in range(n)]

    def own(src, dst, sems):
        copy, me, sibling, xn, yn, dg = plan(src, dst, sems)
        return [copy(a, k, me, to, from_src=True) for k, to in ((1, xn), (2, yn), (0, sibling)) for a in range(n)]

    def relayed(src, dst, sems):
        copy, me, sibling, xn, yn, dg = plan(src, dst, sems)
        return ([copy(a, 3, xn, yn, half=0) for a in range(n)] + [copy(a, 5, xn, sibling) for a in range(n)],
                [copy(a, 4, yn, xn, half=1) for a in range(n)] + [copy(a, 6, yn, sibling) for a in range(n)])

    def diagonal(src, dst, sems):
        copy, me, sibling, xn, yn, dg = plan(src, dst, sems)
        return [copy(a, 7, dg, sibling) for a in range(n)]

    def start(src, dst, sems):
        for cp in local(src, dst, sems) + own(src, dst, sems):
            cp.start()

    def relay(src, dst, sems):
        copy, me, sibling, xn, yn, dg = plan(src, dst, sems)
        from_x, from_y = relayed(src, dst, sems)
        for a in range(n):
            copy(a, 1, xn, me).wait_recv()
        for cp in from_x:
            cp.start()
        for a in range(n):
            copy(a, 2, yn, me).wait_recv()
        for cp in from_y:
            cp.start()

    def forward(src, dst, sems):
        copy, me, sibling, xn, yn, dg = plan(src, dst, sems)
        for a in range(n):
            copy(a, 3, dg, me, half=0).wait_recv()
            copy(a, 4, dg, me, half=1).wait_recv()
        for cp in diagonal(src, dst, sems):
            cp.start()

    def finish(src, dst, sems):
        copy, me, sibling, xn, yn, dg = plan(src, dst, sems)
        other = lambda dev: (dev[0], dev[1], 1 - dev[2])
        for a in range(n):
            copy(a, 0, sibling, me).wait_recv()
            for k, dev in ((5, xn), (6, yn), (7, dg)):
                copy(a, k, other(dev), me).wait_recv()
        from_x, from_y = relayed(src, dst, sems)
        for cp in own(src, dst, sems) + from_x + from_y + diagonal(src, dst, sems):
            cp.wait_send()
        for cp in local(src, dst, sems):
            cp.wait()

    return _Job(shards, [_sds(spec[0], s.dtype) for spec, s in zip(specs, shards)],
                [pltpu.SemaphoreType.DMA((n, 8)), pltpu.SemaphoreType.DMA((n, 8)), pltpu.SemaphoreType.DMA((n,))],
                [(0.0, start), (relay_at, relay), (forward_at, forward), (1.0, finish)])


def _sibling_job(grads):
    n = len(grads)

    def copies(src, dst, sems):
        x, y, c = _position()
        return [pltpu.make_async_remote_copy(src_ref=src[a].at[2 * q + 1 - c], dst_ref=dst[a].at[q],
                                             send_sem=sems[0].at[a, q], recv_sem=sems[1].at[a, q],
                                             device_id=(x, y, 1 - c), device_id_type=MESH)
                for a in range(n) for q in range(NCHIP)]

    return _exchange_job(grads, NCHIP, copies)


def _chips_job(chip_sums):
    n = len(chip_sums)

    def copies(src, dst, sems):
        x, y, c = _position()
        chips = [(1 - x, y), (x, 1 - y), (1 - x, 1 - y)]
        return [pltpu.make_async_remote_copy(src_ref=src[a].at[2 * chip[0] + chip[1]], dst_ref=dst[a].at[j],
                                             send_sem=sems[0].at[a, j], recv_sem=sems[1].at[a, j],
                                             device_id=(*chip, c), device_id_type=MESH)
                for j, chip in enumerate(chips) for a in range(n)]

    return _exchange_job(chip_sums, 3, copies)


def _exchange_job(arrays, slots, copies):
    n = len(arrays)

    def start(src, dst, sems):
        for cp in copies(src, dst, sems):
            cp.start()

    def finish(src, dst, sems):
        cps = copies(src, dst, sems)
        for cp in cps:
            cp.wait_recv()
        for cp in cps:
            cp.wait_send()

    return _Job(arrays, [_sds((slots,) + a.shape[1:], a.dtype) for a in arrays],
                [pltpu.SemaphoreType.DMA((n, slots)), pltpu.SemaphoreType.DMA((n, slots))],
                [(0.0, start), (1.0, finish)])


def _row_block(r):
    for rb in (512, 256, 128, 64, 32, 16):
        if r % rb == 0:
            return rb
    return r


def _chip_sum(grad, from_sibling, core):
    _, r, c = grad.shape
    rb = _row_block(r)

    def body(core_ref, g_ref, s_ref, o_ref):
        del core_ref
        o_ref[...] = (g_ref[...].astype(F32) + s_ref[...].astype(F32)).astype(o_ref.dtype)

    grid_spec = pltpu.PrefetchScalarGridSpec(
        num_scalar_prefetch=1, grid=(NCHIP, r // rb),
        in_specs=[pl.BlockSpec((None, rb, c), lambda q, i, core: (2 * q + core[0], i, 0)),
                  pl.BlockSpec((None, rb, c), lambda q, i, core: (q, i, 0))],
        out_specs=pl.BlockSpec((None, rb, c), lambda q, i, core: (q, i, 0)))
    return pl.pallas_call(body, name="rs_chip_sum", grid_spec=grid_spec, out_shape=_sds((NCHIP, r, c), BF16),
                          compiler_params=_params("parallel", "parallel"))(core, grad, from_sibling)


def _adamw_math(w, g, m, v):
    m2 = ADAM_B1 * m + (1.0 - ADAM_B1) * g
    v2 = ADAM_B2 * v + (1.0 - ADAM_B2) * jnp.square(g)
    m_hat = m2 / (1.0 - ADAM_B1 ** ADAM_STEP)
    v_hat = v2 / (1.0 - ADAM_B2 ** ADAM_STEP)
    delta = -ADAM_LR * (m_hat / (jnp.sqrt(v_hat) + ADAM_EPS) + ADAM_WD * w)
    return delta, m2, v2


def _adamw(w, g, m, v, name):
    r, c = w.shape
    rb = _row_block(r)

    def body(w_ref, g_ref, m_ref, v_ref, d_ref, m2_ref, v2_ref):
        d, m2, v2 = _adamw_math(w_ref[...], g_ref[...], m_ref[...], v_ref[...])
        d_ref[...] = d
        m2_ref[...] = m2
        v2_ref[...] = v2

    blk = pl.BlockSpec((rb, c), lambda i: (i, 0))
    return pl.pallas_call(body, name=name, grid=(r // rb,), in_specs=[blk] * 4, out_specs=[blk] * 3,
                          out_shape=[_sds((r, c), F32)] * 3, compiler_params=_params("parallel"))(w, g, m, v)


def _reduce_adamw(grad, from_sibling, from_chips, sel, w, m, v, name):
    r, c = w.shape
    cp = grad.shape[2]
    rb = _row_block(r)

    def body(sel_ref, g_ref, s_ref, c0_ref, c1_ref, c2_ref, w_ref, m_ref, v_ref, go_ref, d_ref, m2_ref, v2_ref):
        del sel_ref
        g = g_ref[...].astype(F32) + s_ref[...].astype(F32)
        g = g + c0_ref[...].astype(F32)
        g = g + c1_ref[...].astype(F32)
        g = g + c2_ref[...].astype(F32)
        g = g[:, 0:c]
        d, m2, v2 = _adamw_math(w_ref[...], g, m_ref[...], v_ref[...])
        go_ref[...] = g
        d_ref[...] = d
        m2_ref[...] = m2
        v2_ref[...] = v2

    blk = pl.BlockSpec((rb, c), lambda i, sel: (i, 0))
    grid_spec = pltpu.PrefetchScalarGridSpec(
        num_scalar_prefetch=1, grid=(r // rb,),
        in_specs=[pl.BlockSpec((None, rb, cp), lambda i, sel: (sel[0], i, 0)),
                  pl.BlockSpec((None, rb, cp), lambda i, sel: (sel[1], i, 0)),
                  pl.BlockSpec((None, rb, cp), lambda i, sel: (0, i, 0)),
                  pl.BlockSpec((None, rb, cp), lambda i, sel: (1, i, 0)),
                  pl.BlockSpec((None, rb, cp), lambda i, sel: (2, i, 0)),
                  blk, blk, blk],
        out_specs=[blk] * 4)
    return pl.pallas_call(body, name=name, grid_spec=grid_spec, out_shape=[_sds((r, c), F32)] * 4,
                          compiler_params=_params("parallel"))(sel, grad, from_sibling, from_chips, from_chips,
                                                               from_chips, w, m, v)


def _wada_grad(c_all, dmod_piece):
    d = c_all.shape[1]
    n = dmod_piece.shape[1]

    def body(c_ref, dm_ref, o_ref):
        o_ref[...] = _dot_tn(c_ref[...], dm_ref[...])

    return pl.pallas_call(body, name="ada_wgrad", out_shape=_sds((d, n), F32),
                          in_specs=[_vmem()] * 2, out_specs=_vmem())(c_all, dmod_piece)


def _column_chunks(width):
    for n in (4, 2):
        if width % (n * LANES) == 0:
            return n
    return 1


def _conv_taps(ref, col):
    return ref[0:1, col], ref[1:2, col], ref[2:3, col]


def _gelu_parts(u):
    th = jnp.tanh(GELU_C0 * (u + GELU_C1 * (u * u * u)))
    cdf = 0.5 * (1.0 + th)
    return cdf, th


def _conv3_bwd(dv, carry, col, taps, x):
    halo = _halo_bottom(dv[:16, :], carry[:, col])
    carry[:, col] = dv[:16, :]
    d1 = _shift_up(dv, halo, 1)
    d2 = _shift_up(dv, halo, 2)
    w0, w1, w2 = taps
    dx = w2 * dv
    dx = dx + w1 * d1
    dx = dx + w0 * d2
    return dx, (_colsum(d2 * x), _colsum(d1 * x), _colsum(dv * x))


def _prenorm(x, vp_ref):
    r = lax.rsqrt(_rowmean(x * x) + EPS)
    nh = x * r
    return (nh * vp_ref[0:1, :]) * vp_ref[1:2, :] + vp_ref[2:3, :], r, nh


def _prenorm_bwd(dh, r, nh, vp_ref, red_ref):
    g, sc1 = vp_ref[0:1, :], vp_ref[1:2, :]
    red_ref[0:1, :] += _colsum(dh)
    red_ref[1:2, :] += _colsum(dh * (nh * g))
    red_ref[2:3, :] += _colsum(dh * nh * sc1)
    dnh = dh * g * sc1
    return r * (dnh - nh * _rowmean(dnh * nh))


def _postnorm_bwd(dres, z, gate, gpost, red_ref):
    r = lax.rsqrt(_rowmean(z * z) + EPS)
    nh = z * r
    dn = dres * gate
    red_ref[0:1, :] += _colsum(dn * nh)
    red_ref[1:2, :] += _colsum(dres * (nh * gpost))
    dnh = dn * gpost
    return r * (dnh - nh * _rowmean(dnh * nh))


def _mixer_block_fwd(x, vec_pre, vec, w_in, w_pool, w_bout, w_o, job):
    t, d = x.shape
    tm = GROUP
    gw = d // len(POOL_WINDOWS)
    pool_rows = 8 * (POOL_WINDOWS[-1] - 1)

    def body(x_ref, vp_ref, vec_ref, win_ref, wp_ref, wb_ref, wo_ref,
             hb_ref, p5_ref, qm_ref, cv_ref, yar_ref, yb_ref, o_ref, x1_ref, mbuf, ucarry, pcarry):
        i = pl.program_id(0)

        @pl.when(i == 0)
        def _():
            ucarry[...] = jnp.zeros_like(ucarry)
            pcarry[...] = jnp.zeros_like(pcarry)

        xp = _interleave(x_ref[...])
        hb = _prenorm(xp, vp_ref)[0].astype(BF16)
        hb_ref[...] = hb
        proj = lambda k: _dot(hb, win_ref[:, k * d:(k + 1) * d])

        za = proj(4)
        p5_ref[:, 3 * d:4 * d] = za.astype(BF16)
        sa = jax.nn.sigmoid(za)
        u_pool = proj(0)
        for g, window in enumerate(POOL_WINDOWS):
            cols = slice(g * gw, (g + 1) * gw)
            rows = 8 * (window - 1)
            u = u_pool[:, cols]
            halo = _halo_top(u[tm - rows:, :], ucarry[pool_rows - rows:, cols])
            s, shift = jnp.concatenate([halo, u], axis=0), 1
            while shift < window:
                s = s[8 * shift:, :] + s[:s.shape[0] - 8 * shift, :]
                shift *= 2
            pgb = (s * _inv_count(i * tm, window) - u).astype(BF16)
            qm_ref[:, 2 * d + g * gw:2 * d + (g + 1) * gw] = pgb
            yar = _dot(pgb, wp_ref[g])
            yar_ref[:, cols] = yar
            mbuf[:, cols] = sa[:, cols] * (yar * vec_ref[2:3, cols])
        ucarry[...] = u_pool[tm - pool_rows:, :]

        ux = proj(1)
        uc = proj(3)
        p5_ref[:, 0:d] = ux.astype(BF16)
        p5_ref[:, 2 * d:3 * d] = uc.astype(BF16)
        p = uc * ux
        halo = _halo_top(p[tm - 16:, :], pcarry[...])
        pcarry[...] = p[tm - 16:, :]
        cv = vec_ref[3:4, :] + vec_ref[4:5, :] * _shift_down(p, halo, 2)
        cv = cv + vec_ref[5:6, :] * _shift_down(p, halo, 1)
        cv = cv + vec_ref[6:7, :] * p
        cv_ref[...] = cv
        ub = proj(2)
        p5_ref[:, d:2 * d] = ub.astype(BF16)
        qb = (ub * cv).astype(BF16)
        qm_ref[:, 0:d] = qb
        yb = _dot(qb, wb_ref[...])
        yb_ref[...] = yb

        zb = proj(5)
        p5_ref[:, 4 * d:5 * d] = zb.astype(BF16)
        mb = (mbuf[...] + jax.nn.sigmoid(zb) * yb).astype(BF16)
        qm_ref[:, d:2 * d] = mb
        o = _dot(mb, wo_ref[...])
        o_ref[...] = o
        r2 = lax.rsqrt(_rowmean(o * o) + EPS)
        x1_ref[...] = xp + vec_ref[0:1, :] * ((o * r2) * vec_ref[1:2, :])

    row = lambda n: pl.BlockSpec((tm, n), lambda i: (i, 0))
    widths = [d, 5 * d, 3 * d, d, d, d, d, d]
    return _call(
        body, "mixer_block_fwd", (t // tm,), [row(d)] + [_vmem()] * 6, [row(n) for n in widths],
        [_sds((t, n), BF16) for n in widths[:3]] + [_sds((t, n), F32) for n in widths[3:]],
        [pltpu.VMEM((tm, d), F32), pltpu.VMEM((pool_rows, d), F32), pltpu.VMEM((16, d), F32)],
        _params("arbitrary"), (x, vec_pre, vec, w_in, w_pool, w_bout, w_o), job)


def _ffn_block_fwd(x1, target, vec_pre, vec, fcv, w_up, w_down):
    t, d = x1.shape
    tm = GROUP
    fp = w_down.shape[0]
    nch = _column_chunks(fp)
    cw = fp // nch

    def body(x1_ref, tg_ref, vp_ref, vec_ref, fcv_ref, wu_ref, wd_ref,
             hb_ref, upb_ref, upreb_ref, a_ref, ffb_ref, dy_ref, loss_ref, carry):
        i = pl.program_id(0)

        @pl.when(i == 0)
        def _():
            carry[...] = jnp.zeros_like(carry)
            loss_ref[...] = jnp.zeros_like(loss_ref)

        x1 = x1_ref[...]
        hb = _prenorm(x1, vp_ref)[0].astype(BF16)
        hb_ref[...] = hb

        cols = [(slice(j * cw, (j + 1) * cw), slice(fp + j * cw, fp + (j + 1) * cw)) for j in range(nch)]
        up_gate = _dot(hb, wu_ref[:, 0:fp])
        up_val = _dot(hb, wu_ref[:, fp:2 * fp])

        def conv(v, col):
            halo = _halo_top(v[tm - 16:, :], carry[:, col])
            carry[:, col] = v[tm - 16:, :]
            w0, w1, w2 = _conv_taps(fcv_ref, col)
            y = fcv_ref[3:4, col] + w0 * _shift_down(v, halo, 2)
            y = y + w1 * _shift_down(v, halo, 1)
            y = y + w2 * v
            upb_ref[:, col] = y.astype(BF16)
            upreb_ref[:, col] = v.astype(BF16)
            return y

        ff = None
        for j in range(nch):
            gc, vc = cols[j]
            gate = conv(up_gate[:, gc], gc)
            val = conv(up_val[:, gc], vc)
            ab = ((gate * _gelu_parts(gate)[0]) * val).astype(BF16)
            a_ref[:, gc] = ab
            part = _dot(ab, wd_ref[gc, :])
            ff = part if ff is None else ff + part
        ffb_ref[...] = ff.astype(BF16)
        r4 = lax.rsqrt(_rowmean(ff * ff) + EPS)
        y = x1 + vec_ref[0:1, :] * ((ff * r4) * vec_ref[1:2, :])
        e = y - _interleave(tg_ref[...])
        dy_ref[...] = e * (1.0 / d)
        loss_ref[...] += jnp.sum(_rowmean(e * e))

    row = lambda n: pl.BlockSpec((tm, n), lambda i: (i, 0))
    return pl.pallas_call(
        body, name="ffn_block_fwd", grid=(t // tm,),
        in_specs=[row(d), row(d)] + [_vmem()] * 5,
        out_specs=[row(d), row(2 * fp), row(2 * fp), row(fp), row(d), row(d), pl.BlockSpec((8, LANES), lambda i: (0, 0))],
        out_shape=[_sds((t, d), BF16), _sds((t, 2 * fp), BF16), _sds((t, 2 * fp), BF16), _sds((t, fp), BF16),
                   _sds((t, d), BF16), _sds((t, d), F32), _sds((8, LANES), F32)],
        scratch_shapes=[pltpu.VMEM((16, 2 * fp), F32)],
        compiler_params=_params("arbitrary"),
    )(x1, target, vec_pre, vec, fcv, w_up, w_down)


def _ffn_block_bwd(dy, ffb, x1, upb, upreb, vec_pre, vec, fcv, w_up, w_down):
    t, d = dy.shape
    tm = GROUP
    fp = w_down.shape[0]
    nch = _column_chunks(fp)
    cw = fp // nch
    nt = t // tm

    def body(dy_ref, ff_ref, x1_ref, upb_ref, upreb_ref, vp_ref, vec_ref, fcv_ref, wu_ref, wd_ref,
             dff_ref, dup_ref, dx1_ref, red_ref, cred_ref, pred_ref, carry):
        @pl.when(pl.program_id(0) == 0)
        def _():
            carry[...] = jnp.zeros_like(carry)
            red_ref[...] = jnp.zeros_like(red_ref)
            cred_ref[...] = jnp.zeros_like(cred_ref)
            pred_ref[...] = jnp.zeros_like(pred_ref)

        dy_v = dy_ref[...]
        dffb = _postnorm_bwd(dy_v, ff_ref[...].astype(F32), vec_ref[0:1, :], vec_ref[1:2, :], red_ref).astype(BF16)
        dff_ref[...] = dffb

        def conv_bwd(dv, col):
            dx, (t0, t1, t2) = _conv3_bwd(dv, carry, col, _conv_taps(fcv_ref, col), upreb_ref[:, col].astype(F32))
            cred_ref[0:1, col] += t0
            cred_ref[1:2, col] += t1
            cred_ref[2:3, col] += t2
            cred_ref[3:4, col] += _colsum(dv)
            dxb = dx.astype(BF16)
            dup_ref[:, col] = dxb
            return _dot_nt(dxb, wu_ref[:, col])

        dh = None
        for j in range(nch):
            gc = slice(j * cw, (j + 1) * cw)
            vc = slice(fp + j * cw, fp + (j + 1) * cw)
            da = _dot_nt(dffb, wd_ref[gc, :])
            gate = upb_ref[:, gc].astype(F32)
            val = upb_ref[:, vc].astype(F32)
            cdf, th = _gelu_parts(gate)
            dcdf = 0.5 * (1.0 - th * th) * (GELU_C0 * (1.0 + (3.0 * GELU_C1) * (gate * gate)))
            part = conv_bwd(da * val * (cdf + gate * dcdf), gc) + conv_bwd(da * (gate * cdf), vc)
            dh = part if dh is None else dh + part

        _, r, nh = _prenorm(x1_ref[...], vp_ref)
        dx1_ref[...] = dy_v + _prenorm_bwd(dh, r, nh, vp_ref, pred_ref)

    rev = lambda n: pl.BlockSpec((tm, n), lambda i: (nt - 1 - i, 0))
    fixed = lambda n: pl.BlockSpec((8, n), lambda i: (0, 0))
    return pl.pallas_call(
        body, name="ffn_block_bwd", grid=(nt,),
        in_specs=[rev(d), rev(d), rev(d), rev(2 * fp), rev(2 * fp)] + [_vmem()] * 5,
        out_specs=[rev(d), rev(2 * fp), rev(d), fixed(d), fixed(2 * fp), fixed(d)],
        out_shape=[_sds((t, d), BF16), _sds((t, 2 * fp), BF16), _sds((t, d), F32), _sds((8, d), F32),
                   _sds((8, 2 * fp), F32), _sds((8, d), F32)],
        scratch_shapes=[pltpu.VMEM((16, 2 * fp), F32)],
        compiler_params=_params("arbitrary"),
    )(dy, ffb, x1, upb, upreb, vec_pre, vec, fcv, w_up, w_down)


def _mixer_block_bwd(dx1, ob, yarb, ybb, cvb, p5b, x, vec_pre, vec, w_in, w_pool, w_bout, w_o, job):
    t, d = dx1.shape
    tm = GROUP
    gw = d // len(POOL_WINDOWS)
    nt = t // tm
    pool_rows = 8 * (POOL_WINDOWS[-1] - 1)

    def body(dx1_ref, o_ref, yar_ref, yb_ref, cv_ref, p5_ref, x_ref, vp_ref, vec_ref, win_ref, wp_ref, wb_ref, wo_ref,
             dqm_ref, dp_ref, gx_ref, red_ref, pred_ref, dpgcarry, dcvcarry):
        i = pl.program_id(0)
        tix = nt - 1 - i

        @pl.when(i == 0)
        def _():
            red_ref[...] = jnp.zeros_like(red_ref)
            pred_ref[...] = jnp.zeros_like(pred_ref)
            dpgcarry[...] = jnp.zeros_like(dpgcarry)
            dcvcarry[...] = jnp.zeros_like(dcvcarry)

        pscale = vec_ref[2:3, :]
        dx1_v = dx1_ref[...]
        dob = _postnorm_bwd(dx1_v, o_ref[...].astype(F32), vec_ref[0:1, :], vec_ref[1:2, :], red_ref).astype(BF16)
        dqm_ref[:, d:2 * d] = dob
        dm = _dot_nt(dob, wo_ref[...])

        def dproj(cols, value):
            vb = value.astype(BF16)
            dp_ref[:, cols] = vb
            return _dot_nt(vb, win_ref[:, cols])

        sa = jax.nn.sigmoid(p5_ref[:, 3 * d:4 * d].astype(F32))
        yar = yar_ref[...].astype(F32)
        dya = dm * sa
        dh = dproj(slice(4 * d, 5 * d), dm * (yar * pscale) * sa * (1.0 - sa))
        red_ref[2:3, :] += _colsum(dya * yar)
        dyarb = (dya * pscale).astype(BF16)
        dqm_ref[:, 2 * d:3 * d] = dyarb
        sb = jax.nn.sigmoid(p5_ref[:, 4 * d:5 * d].astype(F32))
        dybb = (dm * sb).astype(BF16)
        dqm_ref[:, 0:d] = dybb
        dh = dh + dproj(slice(5 * d, 6 * d), dm * yb_ref[...].astype(F32) * sb * (1.0 - sb))

        for g, window in enumerate(POOL_WINDOWS):
            cols = slice(g * gw, (g + 1) * gw)
            rows = 8 * (window - 1)
            dpg = _dot_nt(dyarb[:, cols], wp_ref[g])
            dpgs = dpg * _inv_count(tix * tm, window)
            halo = _halo_bottom(dpgs[:rows, :], dpgcarry[:rows, cols])
            dpgcarry[:, cols] = dpgs[:pool_rows, :]
            s, shift = jnp.concatenate([dpgs, halo], axis=0), 1
            while shift < window:
                s = s[:s.shape[0] - 8 * shift, :] + s[8 * shift:, :]
                shift *= 2
            dh = dh + dproj(cols, s - dpg)

        dq = _dot_nt(dybb, wb_ref[...])
        ux = p5_ref[:, 0:d].astype(F32)
        uc = p5_ref[:, 2 * d:3 * d].astype(F32)
        dh = dh + dproj(slice(2 * d, 3 * d), dq * cv_ref[...].astype(F32))
        dcv = dq * p5_ref[:, d:2 * d].astype(F32)
        taps = (vec_ref[4:5, :], vec_ref[5:6, :], vec_ref[6:7, :])
        dpv, (t0, t1, t2) = _conv3_bwd(dcv, dcvcarry, slice(0, d), taps, uc * ux)
        red_ref[3:4, :] += _colsum(dcv)
        red_ref[4:5, :] += t0
        red_ref[5:6, :] += t1
        red_ref[6:7, :] += t2
        dh = dh + dproj(slice(d, 2 * d), dpv * uc)
        dh = dh + dproj(slice(3 * d, 4 * d), dpv * ux)

        _, r, nh = _prenorm(_interleave(x_ref[...]), vp_ref)
        gx_ref[...] = _deinterleave(dx1_v + _prenorm_bwd(dh, r, nh, vp_ref, pred_ref))

    rev = lambda n: pl.BlockSpec((tm, n), lambda i: (nt - 1 - i, 0))
    return _call(
        body, "mixer_block_bwd", (nt,), [rev(d)] * 5 + [rev(5 * d), rev(d)] + [_vmem()] * 6,
        [rev(3 * d), rev(6 * d), rev(d), pl.BlockSpec((16, d), lambda i: (0, 0)),
         pl.BlockSpec((8, d), lambda i: (0, 0))],
        [_sds((t, 3 * d), BF16), _sds((t, 6 * d), BF16), _sds((t, d), F32), _sds((16, d), F32), _sds((8, d), F32)],
        [pltpu.VMEM((pool_rows, d), F32), pltpu.VMEM((16, d), F32)],
        _params("arbitrary"), (dx1, ob, yarb, ybb, cvb, p5b, x, vec_pre, vec, w_in, w_pool, w_bout, w_o), job)


def _matmul_tn(a, b, bm, bn, tk, by_col_block, name, job=None):
    t, m = a.shape
    n = b.shape[1]
    nk = t // tk
    parts = int(by_col_block)
    piece = bn // max(parts, 1)
    wide = _round_up(piece, LANES)

    def body(a_ref, b_ref, o_ref, acc_ref):
        k = pl.program_id(2)

        @pl.when(k == 0)
        def _():
            acc_ref[...] = jnp.zeros_like(acc_ref)

        acc_ref[...] += _dot_tn(a_ref[...], b_ref[...])

        @pl.when(k == nk - 1)
        def _():
            if parts:
                acc = acc_ref[...]
                for p in range(parts):
                    if wide > piece:
                        o_ref[p] = jnp.zeros((bm, wide), o_ref.dtype)
                    o_ref[p, :, 0:piece] = acc[:, p * piece:(p + 1) * piece].astype(o_ref.dtype)
            else:
                o_ref[...] = acc_ref[...].astype(o_ref.dtype)

    if by_col_block:
        out_shape = _sds((parts * n // bn, m, wide), BF16)
        out_spec = pl.BlockSpec((parts, bm, wide), lambda i, j, k: (j, i, 0))
    else:
        out_shape = _sds((m, n), BF16)
        out_spec = pl.BlockSpec((bm, bn), lambda i, j, k: (i, j))
    out = _call(body, name, (m // bm, n // bn, nk),
                [pl.BlockSpec((tk, bm), lambda i, j, k: (k, i)), pl.BlockSpec((tk, bn), lambda i, j, k: (k, j))],
                [out_spec], [out_shape], [pltpu.VMEM((bm, bn), F32)],
                _params("arbitrary", "arbitrary", "arbitrary"), (a, b), job)
    return out if job is not None else out[0]


def _matmul_tn_groups(a, b, groups, tk, name, job=None):
    t, m = a.shape
    w = m // groups
    nk = t // tk

    def body(a_ref, b_ref, o_ref, acc_ref):
        k = pl.program_id(1)

        @pl.when(k == 0)
        def _():
            acc_ref[...] = jnp.zeros_like(acc_ref)

        acc_ref[...] += _dot_tn(a_ref[...], b_ref[...])

        @pl.when(k == nk - 1)
        def _():
            o_ref[...] = acc_ref[...].astype(o_ref.dtype)

    blk = pl.BlockSpec((tk, w), lambda g, k: (k, g))
    out = _call(body, name, (groups, nk), [blk, blk], [pl.BlockSpec((None, w, w), lambda g, k: (g, 0, 0))],
                [_sds((groups, w, w), BF16)], [pltpu.VMEM((w, w), F32)], _params("arbitrary", "arbitrary"), (a, b), job)
    return out if job is not None else out[0]


def _round_up(n, k):
    return (n + k - 1) // k * k


def _rows8(rows, width):
    n = _round_up(len(rows), 8)
    rows = list(rows) + [jnp.zeros((1, width), F32)] * (n - len(rows))
    return jnp.concatenate(rows, axis=0)


def kernel(x, c, g_pre_mix, g_post_mix, g_pre_ffn, g_post_ffn, w_ada, b_ada, w_in, w_pool, pool_scale, conv_w, conv_b, w_bout, w_o, w_up, ffn_conv_w, ffn_conv_b, w_down, loss_target, m_g_pre_mix, m_g_post_mix, m_g_pre_ffn, m_g_post_ffn, m_w_ada, m_b_ada, m_w_in, m_w_pool, m_pool_scale, m_conv_w, m_conv_b, m_w_bout, m_w_o, m_w_up, m_ffn_conv_w, m_ffn_conv_b, m_w_down, v_g_pre_mix, v_g_post_mix, v_g_pre_ffn, v_g_post_ffn, v_w_ada, v_b_ada, v_w_in, v_w_pool, v_pool_scale, v_conv_w, v_conv_b, v_w_bout, v_w_o, v_w_up, v_ffn_conv_w, v_ffn_conv_b, v_w_down):
    t, d = x.shape[1], x.shape[2]
    ngroups = len(POOL_WINDOWS)
    gw = d // ngroups
    ada_n = w_ada.shape[2]
    in_n = w_in.shape[2]
    up_n = w_up.shape[2]
    fp = NDEV * w_down.shape[1]

    xi, yi, ci = _position()
    me = _linear(xi, yi, ci)
    chip = 2 * xi + yi
    core = jnp.reshape(ci, (1,)).astype(jnp.int32)
    sel = jnp.stack([2 * chip + ci, chip]).astype(jnp.int32)

    x2 = x.reshape(t, d)
    target = loss_target.reshape(t, d)

    cw_n = conv_w.shape[2]
    pack = jnp.concatenate([c.reshape(1, d), conv_w[0].reshape(1, 3 * cw_n), ffn_conv_w[0].reshape(1, 3 * up_n)], axis=1)
    pack = jnp.pad(pack, ((0, 0), (0, _round_up(pack.shape[1], LANES) - pack.shape[1])))
    b_piece = lax.dynamic_slice_in_dim(b_ada, me * ada_n, ada_n, axis=1)
    mixer_weights = _allgather_job(
        [w_in[0].astype(BF16), w_bout[0].astype(BF16), w_o[0].astype(BF16), w_pool[0].astype(BF16)],
        ["cols", "rows", "rows", "mid"], 0.5, 0.75)
    gathered, mod_rows, w_in_f, g_bout, g_o, w_pool_f = _gather_weights_and_modulation(
        mixer_weights, pack, w_ada[0], b_piece, d)
    w_bout_f = g_bout.reshape(d, d)
    w_o_f = g_o.reshape(d, d)
    c_all = gathered[:, :d]
    c16 = jnp.pad(c_all, ((0, 8), (0, 0))).astype(BF16)
    conv_w_full = gathered[:, d:d + 3 * cw_n].reshape(NDEV, 3, cw_n).transpose(1, 0, 2).reshape(3, NDEV * cw_n)
    fcw_full = gathered[:, d + 3 * cw_n:d + 3 * cw_n + 3 * up_n].reshape(NDEV, 3, up_n)
    fcw_full = fcw_full.transpose(1, 0, 2).reshape(3, 2 * fp)
    fcv = jnp.concatenate([fcw_full, ffn_conv_b, jnp.zeros((4, 2 * fp), F32)], axis=0)
    mod = mod_rows.reshape(1, NDEV * ada_n)
    sh1, sc1, gt1, sh2, sc2, gt2 = [mod[:, k * d:(k + 1) * d] for k in range(6)]

    ffn_weights = _allgather_job([w_up[0].astype(BF16), w_down[0].astype(BF16)], ["rows", "rows"], 0.5, 0.8)
    vec_pre_mix = _rows8([g_pre_mix, 1.0 + sc1, sh1], d)
    vec_mix = _rows8([gt1, g_post_mix, pool_scale, conv_b, conv_w_full[0:1], conv_w_full[1:2], conv_w_full[2:3]], d)
    h1b, p5b, qmb, cvb, yarb, ybb, ob, x1, g_up, g_down = _mixer_block_fwd(
        x2, vec_pre_mix, vec_mix, w_in_f, w_pool_f, w_bout_f, w_o_f, ffn_weights)
    w_up_f = g_up.transpose(1, 0, 2).reshape(d, 2 * fp)
    w_down_f = g_down.reshape(fp, d)
    vec_pre_ffn = _rows8([g_pre_ffn, 1.0 + sc2, sh2], d)
    vec_ffn = _rows8([gt2, g_post_ffn], d)
    h2b, upb, upreb, ab, ffb, dy, loss_part = _ffn_block_fwd(x1, target, vec_pre_ffn, vec_ffn, fcv, w_up_f, w_down_f)

    tk = min(2048, t)
    chip_sum = lambda gs, ss: [_chip_sum(g, s, core) for g, s in zip(gs, ss)]
    dffb, dupre, dx1, red_ffn, red_fconv, red_pre_ffn = _ffn_block_bwd(
        dy, ffb, x1, upb, upreb, vec_pre_ffn, vec_ffn, fcv, w_up_f, w_down_f)
    chunk = fp // _column_chunks(fp)
    gr_up = _matmul_tn(h2b, dupre, d, chunk, tk, chunk // up_n, "wgrad_up")
    gr_down = _matmul_tn(ab, dffb, chunk, d, tk, False, "wgrad_down").reshape(NDEV, fp // NDEV, d)
    sib_ffn = _run_job(_sibling_job([gr_up, gr_down]), "rs_sibling_ffn")
    dqmb, dproj, grad_x, red_mix, red_pre_mix = _mixer_block_bwd(
        dx1, ob, yarb, ybb, cvb, p5b, x2, vec_pre_mix, vec_mix, w_in_f, w_pool_f, w_bout_f, w_o_f, None)
    gr_in, fc_up, fc_down = _matmul_tn(h1b, dproj, d, in_n, tk, True, "wgrad_in",
                                       _chips_job(chip_sum([gr_up, gr_down], sib_ffn)))
    sib_in = _run_job(_sibling_job([gr_in]), "rs_sibling_in")
    gr_qmp, fc_in = _matmul_tn_groups(qmb, dqmb, 3, tk, "wgrad_bout_o_pool", _chips_job(chip_sum([gr_in], sib_in)))
    gr_bout = gr_qmp[0].reshape(NDEV, d // NDEV, d)
    gr_o = gr_qmp[1].reshape(NDEV, d // NDEV, d)
    gr_pool = jnp.stack([gr_qmp[2, g * gw:(g + 1) * gw, g * gw:(g + 1) * gw] for g in range(ngroups)])
    gr_pool = gr_pool.reshape(ngroups, NDEV, gw // NDEV, gw).transpose(1, 0, 2, 3).reshape(NDEV, -1, gw)
    rest = [gr_bout, gr_o, gr_pool]
    sib_rest = _run_job(_sibling_job(rest), "rs_sibling_rest")

    dmod = [red_pre_mix[0:1], red_pre_mix[1:2], red_mix[1:2], red_pre_ffn[0:1], red_pre_ffn[1:2], red_ffn[1:2]]
    small = [red_pre_mix[2:3], red_mix[0:1], red_pre_ffn[2:3], red_ffn[0:1], red_mix[2:3], red_mix[3:4],
             red_mix[4:5], red_mix[5:6], red_mix[6:7]] + dmod
    flat = jnp.concatenate(small + [red_fconv[0:4].reshape(1, 8 * fp), loss_part[0:1, 0:1]], axis=1)
    flat_n = flat.shape[1]
    width = 8 * LANES
    rows = _round_up(-(-flat_n // width), 8)
    flat = jnp.pad(flat, ((0, 0), (0, rows * width - flat_n))).reshape(rows, width)
    gat, tot, *fc_rest = _small_allreduce(flat, "allreduce_small_rs_rest", _chips_job(chip_sum(rest, sib_rest)))

    def big(grad, from_sibling, from_chips, w, m, v, name):
        shape = w.shape
        w2, m2, v2 = [a.reshape((-1, shape[-1])) for a in (w, m, v)]
        outs = _reduce_adamw(grad, from_sibling, from_chips, sel, w2, m2, v2, name)
        return [a.reshape(shape) for a in outs]

    g_w_up, d_w_up, nm_w_up, nv_w_up = big(gr_up, sib_ffn[0], fc_up, w_up, m_w_up, v_w_up, "adamw_up")
    g_w_down, d_w_down, nm_w_down, nv_w_down = big(gr_down, sib_ffn[1], fc_down, w_down, m_w_down, v_w_down, "adamw_down")
    g_w_in, d_w_in, nm_w_in, nv_w_in = big(gr_in, sib_in[0], fc_in, w_in, m_w_in, v_w_in, "adamw_in")
    g_w_bout, d_w_bout, nm_w_bout, nv_w_bout = big(gr_bout, sib_rest[0], fc_rest[0], w_bout, m_w_bout, v_w_bout, "adamw_bout")
    g_w_o, d_w_o, nm_w_o, nv_w_o = big(gr_o, sib_rest[1], fc_rest[1], w_o, m_w_o, v_w_o, "adamw_o")
    g_w_pool, d_w_pool, nm_w_pool, nv_w_pool = big(gr_pool, sib_rest[2], fc_rest[2], w_pool, m_w_pool, v_w_pool, "adamw_pool")

    tot = tot.reshape(1, rows * width)
    gat = gat.reshape(NDEV, rows * width)
    take = lambda k: tot[:, k * d:(k + 1) * d]
    g_g_pre_mix, g_g_post_mix, g_g_pre_ffn, g_g_post_ffn, g_pool_scale, g_conv_b = [take(k) for k in range(6)]
    g_conv_w_full = jnp.concatenate([take(6), take(7), take(8)], axis=0)
    g_conv_w = lax.dynamic_slice_in_dim(g_conv_w_full, me * cw_n, cw_n, axis=1)
    g_b_ada = tot[:, 9 * d:15 * d]
    dmod_all = gat[:, 9 * d:15 * d]
    fconv_tot = tot[:, 15 * d:15 * d + 8 * fp].reshape(4, 2 * fp)
    loss = 0.5 * tot[0, 15 * d + 8 * fp]
    g_ffn_conv_b = fconv_tot[3:4]
    g_ffn_conv_w = lax.dynamic_slice_in_dim(fconv_tot[0:3], me * up_n, up_n, axis=1)
    dmod_piece = lax.dynamic_slice_in_dim(dmod_all, me * ada_n, ada_n, axis=1)
    g_w_ada = _wada_grad(c16, jnp.pad(dmod_piece, ((0, 8), (0, 0))).astype(BF16))

    names_small = [(g_pre_mix, g_g_pre_mix, m_g_pre_mix, v_g_pre_mix), (g_post_mix, g_g_post_mix, m_g_post_mix, v_g_post_mix),
                   (g_pre_ffn, g_g_pre_ffn, m_g_pre_ffn, v_g_pre_ffn), (g_post_ffn, g_g_post_ffn, m_g_post_ffn, v_g_post_ffn),
                   (b_ada, g_b_ada, m_b_ada, v_b_ada), (pool_scale, g_pool_scale, m_pool_scale, v_pool_scale),
                   (conv_w, g_conv_w, m_conv_w, v_conv_w), (conv_b, g_conv_b, m_conv_b, v_conv_b),
                   (ffn_conv_w, g_ffn_conv_w, m_ffn_conv_w, v_ffn_conv_w), (ffn_conv_b, g_ffn_conv_b, m_ffn_conv_b, v_ffn_conv_b)]
    sizes = [w.size for w, _, _, _ in names_small]
    total = sum(sizes)
    prow = _round_up(-(-total // width), 8)

    def pack_small(k):
        a = jnp.concatenate([q[k].reshape(1, -1) for q in names_small], axis=1)
        return jnp.pad(a, ((0, 0), (0, prow * width - total)), constant_values=1.0).reshape(prow, width)

    ds, ms, vs = _adamw(pack_small(0), pack_small(1), pack_small(2), pack_small(3), "adamw_small")

    def unpack_small(a):
        a = a.reshape(-1)
        out, off = [], 0
        for (w, _, _, _), n in zip(names_small, sizes):
            out.append(a[off:off + n].reshape(w.shape))
            off += n
        return out

    (d_g_pre_mix, d_g_post_mix, d_g_pre_ffn, d_g_post_ffn, d_b_ada, d_pool_scale, d_conv_w, d_conv_b,
     d_ffn_conv_w, d_ffn_conv_b) = unpack_small(ds)
    (nm_g_pre_mix, nm_g_post_mix, nm_g_pre_ffn, nm_g_post_ffn, nm_b_ada, nm_pool_scale, nm_conv_w, nm_conv_b,
     nm_ffn_conv_w, nm_ffn_conv_b) = unpack_small(ms)
    (nv_g_pre_mix, nv_g_post_mix, nv_g_pre_ffn, nv_g_post_ffn, nv_b_ada, nv_pool_scale, nv_conv_w, nv_conv_b,
     nv_ffn_conv_w, nv_ffn_conv_b) = unpack_small(vs)
    d_w_ada, nm_w_ada, nv_w_ada = [a.reshape(w_ada.shape) for a in
                                   _adamw(w_ada[0], g_w_ada, m_w_ada[0], v_w_ada[0], "adamw_ada")]

    grads = [g_g_pre_mix, g_g_post_mix, g_g_pre_ffn, g_g_post_ffn, g_w_ada.reshape(w_ada.shape), g_b_ada, g_w_in,
             g_w_pool, g_pool_scale, g_conv_w.reshape(conv_w.shape), g_conv_b, g_w_bout, g_w_o, g_w_up,
             g_ffn_conv_w.reshape(ffn_conv_w.shape), g_ffn_conv_b, g_w_down]
    deltas = [d_g_pre_mix, d_g_post_mix, d_g_pre_ffn, d_g_post_ffn, d_w_ada, d_b_ada, d_w_in, d_w_pool, d_pool_scale,
              d_conv_w, d_conv_b, d_w_bout, d_w_o, d_w_up, d_ffn_conv_w, d_ffn_conv_b, d_w_down]
    new_m = [nm_g_pre_mix, nm_g_post_mix, nm_g_pre_ffn, nm_g_post_ffn, nm_w_ada, nm_b_ada, nm_w_in, nm_w_pool,
             nm_pool_scale, nm_conv_w, nm_conv_b, nm_w_bout, nm_w_o, nm_w_up, nm_ffn_conv_w, nm_ffn_conv_b, nm_w_down]
    new_v = [nv_g_pre_mix, nv_g_post_mix, nv_g_pre_ffn, nv_g_post_ffn, nv_w_ada, nv_b_ada, nv_w_in, nv_w_pool,
             nv_pool_scale, nv_conv_w, nv_conv_b, nv_w_bout, nv_w_o, nv_w_up, nv_ffn_conv_w, nv_ffn_conv_b, nv_w_down]
    return (loss, grad_x.reshape(x.shape), *grads, *deltas, *new_m, *new_v)
```

```python
import math

import jax
import jax.numpy as jnp
from jax import lax
from jax.experimental import pallas as pl
from jax.experimental.pallas import tpu as pltpu

F32 = jnp.float32
BF16 = jnp.bfloat16
MESH = pl.DeviceIdType.MESH

NDEV = 8
NCHIP = 4
EPS = 1e-6
POOL_WINDOWS = (2, 4, 8, 16)
LANES = 128
ADAM_LR = 0.001
ADAM_B1 = 0.9
ADAM_B2 = 0.999
ADAM_EPS = 1e-08
ADAM_WD = 0.01
ADAM_STEP = 10
GELU_C0 = math.sqrt(2.0 / math.pi)
GELU_C1 = 0.044715
VMEM_LIMIT = 56 * 2**20


def _vmem():
    return pl.BlockSpec(memory_space=pltpu.VMEM)


def _any():
    return pl.BlockSpec(memory_space=pl.ANY)


def _params(*sem):
    return pltpu.CompilerParams(dimension_semantics=sem, vmem_limit_bytes=VMEM_LIMIT)


def _sds(shape, dtype):
    return jax.ShapeDtypeStruct(tuple(shape), dtype)


def _position():
    return lax.axis_index("x"), lax.axis_index("y"), lax.axis_index("c")


def _linear(x, y, c):
    return 4 * x + 2 * y + c


def _dot(a, b):
    return jnp.dot(a, b, preferred_element_type=F32)


def _dot_nt(a, b):
    return lax.dot_general(a, b, (((1,), (1,)), ((), ())), preferred_element_type=F32)


def _dot_tn(a, b):
    return lax.dot_general(a, b, (((0,), (0,)), ((), ())), preferred_element_type=F32)


def _colsum(v):
    return jnp.sum(v, axis=0, keepdims=True)


def _rowmean(v):
    return jnp.mean(v, axis=-1, keepdims=True)


GROUP = 256


def _interleave(v):
    g, n = v.shape
    return jnp.swapaxes(v.reshape(8, g // 8, n), 0, 1).reshape(g, n)


def _deinterleave(v):
    g, n = v.shape
    return jnp.swapaxes(v.reshape(g // 8, 8, n), 0, 1).reshape(g, n)


def _halo_top(cur_last, prev_last):
    rows, n = cur_last.shape
    c3 = cur_last.reshape(rows // 8, 8, n)
    p3 = prev_last.reshape(rows // 8, 8, n)
    sub = lax.broadcasted_iota(jnp.int32, c3.shape, 1)
    return jnp.where(sub == 0, pltpu.roll(p3, 1, 1), pltpu.roll(c3, 1, 1)).reshape(rows, n)


def _halo_bottom(cur_first, next_first):
    rows, n = cur_first.shape
    c3 = cur_first.reshape(rows // 8, 8, n)
    n3 = next_first.reshape(rows // 8, 8, n)
    sub = lax.broadcasted_iota(jnp.int32, c3.shape, 1)
    return jnp.where(sub == 7, pltpu.roll(n3, 7, 1), pltpu.roll(c3, 7, 1)).reshape(rows, n)


def _shift_down(v, halo, k):
    rows = v.shape[0]
    return jnp.concatenate([halo[halo.shape[0] - 8 * k:, :], v[:rows - 8 * k, :]], axis=0)


def _shift_up(v, halo, k):
    return jnp.concatenate([v[8 * k:, :], halo[:8 * k, :]], axis=0)


def _inv_count(first_token, window):
    row = lax.broadcasted_iota(jnp.int32, (GROUP, 1), 0)
    t = first_token + (row % 8) * (GROUP // 8) + row // 8
    return 1.0 / jnp.minimum(t + 1, window).astype(F32)


class _Job:
    def __init__(self, inputs, out_shape, scratch, phases):
        self.inputs, self.out_shape, self.scratch, self.phases = list(inputs), list(out_shape), list(scratch), phases


def _call(body, name, grid, in_specs, out_specs, out_shape, scratch_shapes, params, operands, job=None):
    if job is None:
        return pl.pallas_call(body, name=name, grid=grid, in_specs=in_specs, out_specs=out_specs, out_shape=out_shape,
                              scratch_shapes=scratch_shapes, compiler_params=params)(*operands)
    n_in, n_out, n_scr = len(in_specs), len(out_specs), len(scratch_shapes)
    j_in, j_out = len(job.inputs), len(job.out_shape)
    steps = math.prod(grid)

    def hosted(*refs):
        own_in, refs = refs[:n_in], refs[n_in:]
        jin, refs = refs[:j_in], refs[j_in:]
        own_out, refs = refs[:n_out], refs[n_out:]
        jout, refs = refs[:j_out], refs[j_out:]
        own_scr, jscr = refs[:n_scr], refs[n_scr:]
        step = pl.program_id(0)
        for axis in range(1, len(grid)):
            step = step * grid[axis] + pl.program_id(axis)
        for frac, fn in job.phases[:-1]:
            pl.when(step == int(frac * (steps - 1)))(lambda fn=fn: fn(jin, jout, jscr))
        body(*own_in, *own_out, *own_scr)
        pl.when(step == steps - 1)(lambda: job.phases[-1][1](jin, jout, jscr))

    return pl.pallas_call(
        hosted, name=name, grid=grid, in_specs=list(in_specs) + [_any()] * j_in,
        out_specs=list(out_specs) + [_any()] * j_out, out_shape=list(out_shape) + job.out_shape,
        scratch_shapes=list(scratch_shapes) + job.scratch, compiler_params=params)(*operands, *job.inputs)


def _run_job(job, name):
    n_in, n_out = len(job.inputs), len(job.out_shape)

    def body(*refs):
        for _, fn in job.phases:
            fn(refs[:n_in], refs[n_in:n_in + n_out], refs[n_in + n_out:])

    return pl.pallas_call(body, name=name, out_shape=job.out_shape, in_specs=[_any()] * n_in,
                          out_specs=[_any()] * n_out, scratch_shapes=job.scratch)(*job.inputs)


def _peers(x, y, c):
    out = []
    for k in range(1, NDEV):
        out.append(((1 - x) if k & 4 else x, (1 - y) if k & 2 else y, (1 - c) if k & 1 else c))
    return out


def _small_allreduce(v, name, job):
    r, n = v.shape
    j_in, j_out = len(job.inputs), len(job.out_shape)

    def body(v_ref, *rest):
        jin, rest = rest[:j_in], rest[j_in:]
        gat_ref, sum_ref = rest[:2]
        jout, rest = rest[2:2 + j_out], rest[2 + j_out:]
        send_sems, recv_sems, local_sem = rest[:3]
        jscr = rest[3:]
        job.phases[0][1](jin, jout, jscr)
        x, y, c = _position()
        me = _linear(x, y, c)
        mine = pltpu.make_async_copy(v_ref, gat_ref.at[me], local_sem)
        mine.start()
        peers = _peers(x, y, c)
        sends = []
        for k, peer in enumerate(peers):
            cp = pltpu.make_async_remote_copy(src_ref=v_ref, dst_ref=gat_ref.at[me], send_sem=send_sems.at[k],
                                              recv_sem=recv_sems.at[k], device_id=peer, device_id_type=MESH)
            cp.start()
            sends.append(cp)
        for k, peer in enumerate(peers):
            pltpu.make_async_remote_copy(src_ref=v_ref, dst_ref=gat_ref.at[_linear(*peer)], send_sem=send_sems.at[k],
                                         recv_sem=recv_sems.at[k], device_id=peer, device_id_type=MESH).wait_recv()
        for cp in sends:
            cp.wait_send()
        mine.wait()
        acc = gat_ref[0]
        for j in range(1, NDEV):
            acc = acc + gat_ref[j]
        sum_ref[...] = acc
        job.phases[-1][1](jin, jout, jscr)

    return pl.pallas_call(
        body, name=name, out_shape=[_sds((NDEV, r, n), F32), _sds((r, n), F32)] + job.out_shape,
        in_specs=[_vmem()] + [_any()] * j_in, out_specs=[_vmem()] * 2 + [_any()] * j_out,
        scratch_shapes=[pltpu.SemaphoreType.DMA((NDEV - 1,)), pltpu.SemaphoreType.DMA((NDEV - 1,)),
                        pltpu.SemaphoreType.DMA(())] + job.scratch,
    )(v, *job.inputs)


def _exchange_rows(src_for, dst_ref, sems):
    send_sems, recv_sems, local_sem = sems
    x, y, c = _position()
    me = _linear(x, y, c)
    row = lambda j: dst_ref.at[pl.ds(j, 1), :]
    mine = pltpu.make_async_copy(src_for(me), row(me), local_sem)
    mine.start()
    peers = _peers(x, y, c)
    sends = []
    for k, peer in enumerate(peers):
        cp = pltpu.make_async_remote_copy(src_ref=src_for(_linear(*peer)), dst_ref=row(me), send_sem=send_sems.at[k],
                                          recv_sem=recv_sems.at[k], device_id=peer, device_id_type=MESH)
        cp.start()
        sends.append(cp)
    for k, peer in enumerate(peers):
        pltpu.make_async_remote_copy(src_ref=src_for(me), dst_ref=row(_linear(*peer)), send_sem=send_sems.at[k],
                                     recv_sem=recv_sems.at[k], device_id=peer, device_id_type=MESH).wait_recv()
    for cp in sends:
        cp.wait_send()
    mine.wait()


def _gather_weights_and_modulation(job, pack, w_ada, b_piece, d):
    n = pack.shape[1]
    m = w_ada.shape[1]
    j_in, j_out = len(job.inputs), len(job.out_shape)
    row_sems = [pltpu.SemaphoreType.DMA((NDEV - 1,)), pltpu.SemaphoreType.DMA((NDEV - 1,)), pltpu.SemaphoreType.DMA(())]

    def body(pack_ref, wada_ref, bp_ref, *rest):
        jin, rest = rest[:j_in], rest[j_in:]
        gat_ref, mod_ref = rest[:2]
        jout, rest = rest[2:2 + j_out], rest[2 + j_out:]
        sems1, sems2, piece, jscr = rest[0:3], rest[3:6], rest[6], rest[7:]
        phases = [fn for _, fn in job.phases]
        phases[0](jin, jout, jscr)
        _exchange_rows(lambda j: pack_ref, gat_ref, sems1)
        c16 = jnp.concatenate([gat_ref[:, 0:d], jnp.zeros((NDEV, d), F32)], axis=0).astype(BF16)
        piece[...] = (_dot(c16, wada_ref[...].astype(BF16)) + bp_ref[...])[0:NDEV, :]
        _exchange_rows(lambda j: piece.at[pl.ds(j, 1), :], mod_ref, sems2)
        for fn in phases[1:]:
            fn(jin, jout, jscr)

    return pl.pallas_call(
        body, name="gather_weights_and_modulation",
        out_shape=[_sds((NDEV, n), F32), _sds((NDEV, m), F32)] + job.out_shape,
        in_specs=[_vmem()] * 3 + [_any()] * j_in, out_specs=[_vmem()] * 2 + [_any()] * j_out,
        scratch_shapes=row_sems + row_sems + [pltpu.VMEM((NDEV, m), F32)] + job.scratch,
    )(pack, w_ada, b_piece, *job.inputs)


def _gathered(shard, layout):
    if layout == "rows":
        return (NDEV,) + shard.shape, lambda ref, j: ref.at[j]
    if layout == "cols":
        r, c = shard.shape
        return (r, NDEV * c), lambda ref, j: ref.at[:, pl.ds(pl.multiple_of(j * c, LANES), c)]
    g, r, c = shard.shape
    return (g, NDEV * r, c), lambda ref, j: ref.at[:, pl.ds(pl.multiple_of(j * r, 16), r), :]


def _allgather_job(shards, layouts, relay_at, forward_at):
    n = len(shards)
    specs = [_gathered(s, l) for s, l in zip(shards, layouts)]
    halves = [s.shape[0] // 2 for s in shards]

    def plan(src, dst, sems):
        send_sems, recv_sems, _ = sems
        x, y, c = _position()
        me, sibling, xn, yn, dg = (x, y, c), (x, y, 1 - c), (1 - x, y, c), (x, 1 - y, c), (1 - x, 1 - y, c)

        def copy(a, k, block, to, half=None, from_src=False):
            blk = specs[a][1](dst[a], _linear(*block))
            if half is not None:
                blk = blk.at[pl.ds(half * halves[a], halves[a])]
            return pltpu.make_async_remote_copy(src_ref=src[a] if from_src else blk, dst_ref=blk,
                                                send_sem=send_sems.at[a, k], recv_sem=recv_sems.at[a, k],
                                                device_id=to, device_id_type=MESH)
        return copy, me, sibling, xn, yn, dg

    def local(src, dst, sems):
        x, y, c = _position()
        return [pltpu.make_async_copy(src[a], specs[a][1](dst[a], _linear(x, y, c)), sems[2].at[a]) for a in range(n)]

    def own(src, dst, sems):
        copy, me, sibling, xn, yn, dg = plan(src, dst, sems)
        return [copy(a, k, me, to, from_src=True) for k, to in ((1, xn), (2, yn), (0, sibling)) for a in range(n)]

    def relayed(src, dst, sems):
        copy, me, sibling, xn, yn, dg = plan(src, dst, sems)
        return ([copy(a, 3, xn, yn, half=0) for a in range(n)] + [copy(a, 5, xn, sibling) for a in range(n)],
                [copy(a, 4, yn, xn, half=1) for a in range(n)] + [copy(a, 6, yn, sibling) for a in range(n)])

    def diagonal(src, dst, sems):
        copy, me, sibling, xn, yn, dg = plan(src, dst, sems)
        return [copy(a, 7, dg, sibling) for a in range(n)]

    def start(src, dst, sems):
        for cp in local(src, dst, sems) + own(src, dst, sems):
            cp.start()

    def relay(src, dst, sems):
        copy, me, sibling, xn, yn, dg = plan(src, dst, sems)
        from_x, from_y = relayed(src, dst, sems)
        for a in range(n):
            copy(a, 1, xn, me).wait_recv()
        for cp in from_x:
            cp.start()
        for a in range(n):
            copy(a, 2, yn, me).wait_recv()
        for cp in from_y:
            cp.start()

    def forward(src, dst, sems):
        copy, me, sibling, xn, yn, dg = plan(src, dst, sems)
        for a in range(n):
            copy(a, 3, dg, me, half=0).wait_recv()
            copy(a, 4, dg, me, half=1).wait_recv()
        for cp in diagonal(src, dst, sems):
            cp.start()

    def finish(src, dst, sems):
        copy, me, sibling, xn, yn, dg = plan(src, dst, sems)
        other = lambda dev: (dev[0], dev[1], 1 - dev[2])
        for a in range(n):
            copy(a, 0, sibling, me).wait_recv()
            for k, dev in ((5, xn), (6, yn), (7, dg)):
                copy(a, k, other(dev), me).wait_recv()
        from_x, from_y = relayed(src, dst, sems)
        for cp in own(src, dst, sems) + from_x + from_y + diagonal(src, dst, sems):
            cp.wait_send()
        for cp in local(src, dst, sems):
            cp.wait()

    return _Job(shards, [_sds(spec[0], s.dtype) for spec, s in zip(specs, shards)],
                [pltpu.SemaphoreType.DMA((n, 8)), pltpu.SemaphoreType.DMA((n, 8)), pltpu.SemaphoreType.DMA((n,))],
                [(0.0, start), (relay_at, relay), (forward_at, forward), (1.0, finish)])


def _sibling_job(grads):
    n = len(grads)

    def copies(src, dst, sems):
        x, y, c = _position()
        return [pltpu.make_async_remote_copy(src_ref=src[a].at[2 * q + 1 - c], dst_ref=dst[a].at[q],
                                             send_sem=sems[0].at[a, q], recv_sem=sems[1].at[a, q],
                                             device_id=(x, y, 1 - c), device_id_type=MESH)
                for a in range(n) for q in range(NCHIP)]

    return _exchange_job(grads, NCHIP, copies)


def _chips_job(chip_sums):
    n = len(chip_sums)

    def copies(src, dst, sems):
        x, y, c = _position()
        chips = [(1 - x, y), (x, 1 - y), (1 - x, 1 - y)]
        return [pltpu.make_async_remote_copy(src_ref=src[a].at[2 * chip[0] + chip[1]], dst_ref=dst[a].at[j],
                                             send_sem=sems[0].at[a, j], recv_sem=sems[1].at[a, j],
                                             device_id=(*chip, c), device_id_type=MESH)
                for j, chip in enumerate(chips) for a in range(n)]

    return _exchange_job(chip_sums, 3, copies)


def _exchange_job(arrays, slots, copies):
    n = len(arrays)

    def start(src, dst, sems):
        for cp in copies(src, dst, sems):
            cp.start()

    def finish(src, dst, sems):
        cps = copies(src, dst, sems)
        for cp in cps:
            cp.wait_recv()
        for cp in cps:
            cp.wait_send()

    return _Job(arrays, [_sds((slots,) + a.shape[1:], a.dtype) for a in arrays],
                [pltpu.SemaphoreType.DMA((n, slots)), pltpu.SemaphoreType.DMA((n, slots))],
                [(0.0, start), (1.0, finish)])


def _row_block(r):
    if r <= 512:
        return r
    for rb in range(512, 15, -16):
        if r % rb == 0:
            return rb
    return r


def _chip_sums(grads, from_sibling, core, name):
    n = len(grads)

    def body(core_ref, *refs):
        del core_ref
        for a in range(n):
            refs[2 * n + a][...] = (refs[a][...].astype(F32) + refs[n + a][...].astype(F32)).astype(BF16)

    block = lambda g, index: pl.BlockSpec((None,) + g.shape[1:], index)
    grid_spec = pltpu.PrefetchScalarGridSpec(
        num_scalar_prefetch=1, grid=(NCHIP,),
        in_specs=[block(g, lambda q, core: (2 * q + core[0], 0, 0)) for g in grads]
        + [block(g, lambda q, core: (q, 0, 0)) for g in grads],
        out_specs=[block(g, lambda q, core: (q, 0, 0)) for g in grads])
    return pl.pallas_call(body, name=name, grid_spec=grid_spec,
                          out_shape=[_sds((NCHIP,) + g.shape[1:], BF16) for g in grads],
                          compiler_params=_params("parallel"))(core, *grads, *from_sibling)


def _adamw_math(w, g, m, v):
    m2 = ADAM_B1 * m + (1.0 - ADAM_B1) * g
    v2 = ADAM_B2 * v + (1.0 - ADAM_B2) * jnp.square(g)
    m_hat = m2 / (1.0 - ADAM_B1 ** ADAM_STEP)
    v_hat = v2 / (1.0 - ADAM_B2 ** ADAM_STEP)
    delta = -ADAM_LR * (m_hat / (jnp.sqrt(v_hat) + ADAM_EPS) + ADAM_WD * w)
    return delta, m2, v2


def _adamw(w, g, m, v, name):
    r, c = w.shape
    rb = _row_block(r)

    def body(w_ref, g_ref, m_ref, v_ref, d_ref, m2_ref, v2_ref):
        d, m2, v2 = _adamw_math(w_ref[...], g_ref[...], m_ref[...], v_ref[...])
        d_ref[...] = d
        m2_ref[...] = m2
        v2_ref[...] = v2

    blk = pl.BlockSpec((rb, c), lambda i: (i, 0))
    return pl.pallas_call(body, name=name, grid=(r // rb,), in_specs=[blk] * 4, out_specs=[blk] * 3,
                          out_shape=[_sds((r, c), F32)] * 3, compiler_params=_params("parallel"))(w, g, m, v)


def _reduce_adamw(grad, from_sibling, from_chips, sel, w, m, v, name):
    r, c = w.shape
    cp = grad.shape[2]
    rb = _row_block(r)

    def body(sel_ref, g_ref, s_ref, c0_ref, c1_ref, c2_ref, w_ref, m_ref, v_ref, go_ref, d_ref, m2_ref, v2_ref):
        del sel_ref
        g = g_ref[...].astype(F32) + s_ref[...].astype(F32)
        g = g + c0_ref[...].astype(F32)
        g = g + c1_ref[...].astype(F32)
        g = g + c2_ref[...].astype(F32)
        g = g[:, 0:c]
        d, m2, v2 = _adamw_math(w_ref[...], g, m_ref[...], v_ref[...])
        go_ref[...] = g
        d_ref[...] = d
        m2_ref[...] = m2
        v2_ref[...] = v2

    blk = pl.BlockSpec((rb, c), lambda i, sel: (i, 0))
    grid_spec = pltpu.PrefetchScalarGridSpec(
        num_scalar_prefetch=1, grid=(r // rb,),
        in_specs=[pl.BlockSpec((None, rb, cp), lambda i, sel: (sel[0], i, 0)),
                  pl.BlockSpec((None, rb, cp), lambda i, sel: (sel[1], i, 0)),
                  pl.BlockSpec((None, rb, cp), lambda i, sel: (0, i, 0)),
                  pl.BlockSpec((None, rb, cp), lambda i, sel: (1, i, 0)),
                  pl.BlockSpec((None, rb, cp), lambda i, sel: (2, i, 0)),
                  blk, blk, blk],
        out_specs=[blk] * 4)
    return pl.pallas_call(body, name=name, grid_spec=grid_spec, out_shape=[_sds((r, c), F32)] * 4,
                          compiler_params=_params("parallel"))(sel, grad, from_sibling, from_chips, from_chips,
                                                               from_chips, w, m, v)


def _reduce_adamw_group(items, sel, name):
    n = len(items)

    def body(sel_ref, *refs):
        del sel_ref
        ins, outs = refs[:8 * n], refs[8 * n:]
        for a in range(n):
            g_ref, s_ref, c0_ref, c1_ref, c2_ref, w_ref, m_ref, v_ref = ins[8 * a:8 * a + 8]
            g = g_ref[...].astype(F32) + s_ref[...].astype(F32)
            g = g + c0_ref[...].astype(F32)
            g = g + c1_ref[...].astype(F32)
            g = g + c2_ref[...].astype(F32)
            d, m2, v2 = _adamw_math(w_ref[...], g, m_ref[...], v_ref[...])
            for ref, val in zip(outs[4 * a:4 * a + 4], (g, d, m2, v2)):
                ref[...] = val

    in_specs, out_specs, out_shape, operands = [], [], [], []
    for grad, from_sibling, from_chips, w, m, v in items:
        slot = lambda index, shape=grad.shape[1:]: pl.BlockSpec((None,) + shape, index)
        full = pl.BlockSpec(w.shape, lambda i, sel: (0, 0))
        in_specs += [slot(lambda i, sel: (sel[0], 0, 0)), slot(lambda i, sel: (sel[1], 0, 0)),
                     slot(lambda i, sel: (0, 0, 0)), slot(lambda i, sel: (1, 0, 0)), slot(lambda i, sel: (2, 0, 0)),
                     full, full, full]
        out_specs += [full] * 4
        out_shape += [_sds(w.shape, F32)] * 4
        operands += [grad, from_sibling, from_chips, from_chips, from_chips, w, m, v]
    grid_spec = pltpu.PrefetchScalarGridSpec(num_scalar_prefetch=1, grid=(1,), in_specs=in_specs, out_specs=out_specs)
    outs = pl.pallas_call(body, name=name, grid_spec=grid_spec, out_shape=out_shape,
                          compiler_params=_params("arbitrary"))(sel, *operands)
    return [outs[4 * a:4 * a + 4] for a in range(n)]


def _wada_grad(c_all, dmod_piece):
    d = c_all.shape[1]
    n = dmod_piece.shape[1]

    def body(c_ref, dm_ref, o_ref):
        o_ref[...] = _dot_tn(c_ref[...], dm_ref[...])

    return pl.pallas_call(body, name="ada_wgrad", out_shape=_sds((d, n), F32),
                          in_specs=[_vmem()] * 2, out_specs=_vmem())(c_all, dmod_piece)


def _column_chunks(width):
    for n in (4, 2):
        if width % (n * LANES) == 0:
            return n
    return 1


def _conv_taps(ref, col):
    return ref[0:1, col], ref[1:2, col], ref[2:3, col]


def _gelu_parts(u):
    th = jnp.tanh(GELU_C0 * (u + GELU_C1 * (u * u * u)))
    cdf = 0.5 * (1.0 + th)
    return cdf, th


def _conv3_bwd(dv, carry, col, taps, x):
    halo = _halo_bottom(dv[:16, :], carry[:, col])
    carry[:, col] = dv[:16, :]
    d1 = _shift_up(dv, halo, 1)
    d2 = _shift_up(dv, halo, 2)
    w0, w1, w2 = taps
    dx = w2 * dv
    dx = dx + w1 * d1
    dx = dx + w0 * d2
    return dx, (_colsum(d2 * x), _colsum(d1 * x), _colsum(dv * x))


def _prenorm(x, vp_ref):
    r = lax.rsqrt(_rowmean(x * x) + EPS)
    nh = x * r
    return (nh * vp_ref[0:1, :]) * vp_ref[1:2, :] + vp_ref[2:3, :], r, nh


def _prenorm_bwd(dh, r, nh, vp_ref, red_ref):
    g, sc1 = vp_ref[0:1, :], vp_ref[1:2, :]
    red_ref[0:1, :] += _colsum(dh)
    red_ref[1:2, :] += _colsum(dh * (nh * g))
    red_ref[2:3, :] += _colsum(dh * nh * sc1)
    dnh = dh * g * sc1
    return r * (dnh - nh * _rowmean(dnh * nh))


def _postnorm_bwd(dres, z, gate, gpost, red_ref):
    r = lax.rsqrt(_rowmean(z * z) + EPS)
    nh = z * r
    dn = dres * gate
    red_ref[0:1, :] += _colsum(dn * nh)
    red_ref[1:2, :] += _colsum(dres * (nh * gpost))
    dnh = dn * gpost
    return r * (dnh - nh * _rowmean(dnh * nh))


def _mixer_block_fwd(x, vec_pre, vec, w_in, w_pool, w_bout, w_o, job):
    t, d = x.shape
    tm = GROUP
    gw = d // len(POOL_WINDOWS)
    pool_rows = 8 * (POOL_WINDOWS[-1] - 1)

    def body(x_ref, vp_ref, vec_ref, win_ref, wp_ref, wb_ref, wo_ref,
             hb_ref, p5_ref, qm_ref, cv_ref, yar_ref, yb_ref, o_ref, x1_ref, mbuf, ucarry, pcarry):
        i = pl.program_id(0)

        @pl.when(i == 0)
        def _():
            ucarry[...] = jnp.zeros_like(ucarry)
            pcarry[...] = jnp.zeros_like(pcarry)

        xp = _interleave(x_ref[...])
        hb = _prenorm(xp, vp_ref)[0].astype(BF16)
        hb_ref[...] = hb
        proj = lambda k: _dot(hb, win_ref[:, k * d:(k + 1) * d])

        za = proj(4)
        p5_ref[:, 3 * d:4 * d] = za.astype(BF16)
        sa = jax.nn.sigmoid(za)
        u_pool = proj(0)
        for g, window in enumerate(POOL_WINDOWS):
            cols = slice(g * gw, (g + 1) * gw)
            rows = 8 * (window - 1)
            u = u_pool[:, cols]
            halo = _halo_top(u[tm - rows:, :], ucarry[pool_rows - rows:, cols])
            s, shift = jnp.concatenate([halo, u], axis=0), 1
            while shift < window:
                s = s[8 * shift:, :] + s[:s.shape[0] - 8 * shift, :]
                shift *= 2
            pgb = (s * _inv_count(i * tm, window) - u).astype(BF16)
            qm_ref[:, 2 * d + g * gw:2 * d + (g + 1) * gw] = pgb
            yar = _dot(pgb, wp_ref[g])
            yar_ref[:, cols] = yar
            mbuf[:, cols] = sa[:, cols] * (yar * vec_ref[2:3, cols])
        ucarry[...] = u_pool[tm - pool_rows:, :]

        ux = proj(1)
        uc = proj(3)
        p5_ref[:, 0:d] = ux.astype(BF16)
        p5_ref[:, 2 * d:3 * d] = uc.astype(BF16)
        p = uc * ux
        halo = _halo_top(p[tm - 16:, :], pcarry[...])
        pcarry[...] = p[tm - 16:, :]
        cv = vec_ref[3:4, :] + vec_ref[4:5, :] * _shift_down(p, halo, 2)
        cv = cv + vec_ref[5:6, :] * _shift_down(p, halo, 1)
        cv = cv + vec_ref[6:7, :] * p
        cv_ref[...] = cv
        ub = proj(2)
        p5_ref[:, d:2 * d] = ub.astype(BF16)
        qb = (ub * cv).astype(BF16)
        qm_ref[:, 0:d] = qb
        yb = _dot(qb, wb_ref[...])
        yb_ref[...] = yb

        zb = proj(5)
        p5_ref[:, 4 * d:5 * d] = zb.astype(BF16)
        mb = (mbuf[...] + jax.nn.sigmoid(zb) * yb).astype(BF16)
        qm_ref[:, d:2 * d] = mb
        o = _dot(mb, wo_ref[...])
        o_ref[...] = o
        r2 = lax.rsqrt(_rowmean(o * o) + EPS)
        x1_ref[...] = xp + vec_ref[0:1, :] * ((o * r2) * vec_ref[1:2, :])

    row = lambda n: pl.BlockSpec((tm, n), lambda i: (i, 0))
    widths = [d, 5 * d, 3 * d, d, d, d, d, d]
    return _call(
        body, "mixer_block_fwd", (t // tm,), [row(d)] + [_vmem()] * 6, [row(n) for n in widths],
        [_sds((t, n), BF16) for n in widths[:3]] + [_sds((t, n), F32) for n in widths[3:]],
        [pltpu.VMEM((tm, d), F32), pltpu.VMEM((pool_rows, d), F32), pltpu.VMEM((16, d), F32)],
        _params("arbitrary"), (x, vec_pre, vec, w_in, w_pool, w_bout, w_o), job)


def _ffn_block_fwd(x1, target, vec_pre, vec, fcv, w_up, w_down):
    t, d = x1.shape
    tm = GROUP
    fp = w_down.shape[0]
    nch = _column_chunks(fp)
    cw = fp // nch

    def body(x1_ref, tg_ref, vp_ref, vec_ref, fcv_ref, wu_ref, wd_ref,
             hb_ref, upb_ref, upreb_ref, a_ref, ffb_ref, dy_ref, loss_ref, carry):
        i = pl.program_id(0)

        @pl.when(i == 0)
        def _():
            carry[...] = jnp.zeros_like(carry)
            loss_ref[...] = jnp.zeros_like(loss_ref)

        x1 = x1_ref[...]
        hb = _prenorm(x1, vp_ref)[0].astype(BF16)
        hb_ref[...] = hb

        cols = [(slice(j * cw, (j + 1) * cw), slice(fp + j * cw, fp + (j + 1) * cw)) for j in range(nch)]
        up_gate = _dot(hb, wu_ref[:, 0:fp])
        up_val = _dot(hb, wu_ref[:, fp:2 * fp])

        def conv(v, col):
            halo = _halo_top(v[tm - 16:, :], carry[:, col])
            carry[:, col] = v[tm - 16:, :]
            w0, w1, w2 = _conv_taps(fcv_ref, col)
            y = fcv_ref[3:4, col] + w0 * _shift_down(v, halo, 2)
            y = y + w1 * _shift_down(v, halo, 1)
            y = y + w2 * v
            upb_ref[:, col] = y.astype(BF16)
            upreb_ref[:, col] = v.astype(BF16)
            return y

        ff = None
        for j in range(nch):
            gc, vc = cols[j]
            gate = conv(up_gate[:, gc], gc)
            val = conv(up_val[:, gc], vc)
            ab = ((gate * _gelu_parts(gate)[0]) * val).astype(BF16)
            a_ref[:, gc] = ab
            part = _dot(ab, wd_ref[gc, :])
            ff = part if ff is None else ff + part
        ffb_ref[...] = ff.astype(BF16)
        r4 = lax.rsqrt(_rowmean(ff * ff) + EPS)
        y = x1 + vec_ref[0:1, :] * ((ff * r4) * vec_ref[1:2, :])
        e = y - _interleave(tg_ref[...])
        dy_ref[...] = e * (1.0 / d)
        loss_ref[...] += jnp.sum(_rowmean(e * e))

    row = lambda n: pl.BlockSpec((tm, n), lambda i: (i, 0))
    return pl.pallas_call(
        body, name="ffn_block_fwd", grid=(t // tm,),
        in_specs=[row(d), row(d)] + [_vmem()] * 5,
        out_specs=[row(d), row(2 * fp), row(2 * fp), row(fp), row(d), row(d), pl.BlockSpec((8, LANES), lambda i: (0, 0))],
        out_shape=[_sds((t, d), BF16), _sds((t, 2 * fp), BF16), _sds((t, 2 * fp), BF16), _sds((t, fp), BF16),
                   _sds((t, d), BF16), _sds((t, d), F32), _sds((8, LANES), F32)],
        scratch_shapes=[pltpu.VMEM((16, 2 * fp), F32)],
        compiler_params=_params("arbitrary"),
    )(x1, target, vec_pre, vec, fcv, w_up, w_down)


def _ffn_block_bwd(dy, ffb, x1, upb, upreb, vec_pre, vec, fcv, w_up, w_down):
    t, d = dy.shape
    tm = GROUP
    fp = w_down.shape[0]
    nch = _column_chunks(fp)
    cw = fp // nch
    nt = t // tm

    def body(dy_ref, ff_ref, x1_ref, upb_ref, upreb_ref, vp_ref, vec_ref, fcv_ref, wu_ref, wd_ref,
             dff_ref, dup_ref, dx1_ref, red_ref, cred_ref, pred_ref, carry):
        @pl.when(pl.program_id(0) == 0)
        def _():
            carry[...] = jnp.zeros_like(carry)
            red_ref[...] = jnp.zeros_like(red_ref)
            cred_ref[...] = jnp.zeros_like(cred_ref)
            pred_ref[...] = jnp.zeros_like(pred_ref)

        dy_v = dy_ref[...]
        dffb = _postnorm_bwd(dy_v, ff_ref[...].astype(F32), vec_ref[0:1, :], vec_ref[1:2, :], red_ref).astype(BF16)
        dff_ref[...] = dffb

        def conv_bwd(dv, col):
            dx, (t0, t1, t2) = _conv3_bwd(dv, carry, col, _conv_taps(fcv_ref, col), upreb_ref[:, col].astype(F32))
            cred_ref[0:1, col] += t0
            cred_ref[1:2, col] += t1
            cred_ref[2:3, col] += t2
            cred_ref[3:4, col] += _colsum(dv)
            dxb = dx.astype(BF16)
            dup_ref[:, col] = dxb
            return _dot_nt(dxb, wu_ref[:, col])

        dh = None
        for j in range(nch):
            gc = slice(j * cw, (j + 1) * cw)
            vc = slice(fp + j * cw, fp + (j + 1) * cw)
            da = _dot_nt(dffb, wd_ref[gc, :])
            gate = upb_ref[:, gc].astype(F32)
            val = upb_ref[:, vc].astype(F32)
            cdf, th = _gelu_parts(gate)
            dcdf = 0.5 * (1.0 - th * th) * (GELU_C0 * (1.0 + (3.0 * GELU_C1) * (gate * gate)))
            part = conv_bwd(da * val * (cdf + gate * dcdf), gc) + conv_bwd(da * (gate * cdf), vc)
            dh = part if dh is None else dh + part

        _, r, nh = _prenorm(x1_ref[...], vp_ref)
        dx1_ref[...] = dy_v + _prenorm_bwd(dh, r, nh, vp_ref, pred_ref)

    rev = lambda n: pl.BlockSpec((tm, n), lambda i: (nt - 1 - i, 0))
    fixed = lambda n: pl.BlockSpec((8, n), lambda i: (0, 0))
    return pl.pallas_call(
        body, name="ffn_block_bwd", grid=(nt,),
        in_specs=[rev(d), rev(d), rev(d), rev(2 * fp), rev(2 * fp)] + [_vmem()] * 5,
        out_specs=[rev(d), rev(2 * fp), rev(d), fixed(d), fixed(2 * fp), fixed(d)],
        out_shape=[_sds((t, d), BF16), _sds((t, 2 * fp), BF16), _sds((t, d), F32), _sds((8, d), F32),
                   _sds((8, 2 * fp), F32), _sds((8, d), F32)],
        scratch_shapes=[pltpu.VMEM((16, 2 * fp), F32)],
        compiler_params=_params("arbitrary"),
    )(dy, ffb, x1, upb, upreb, vec_pre, vec, fcv, w_up, w_down)


def _mixer_block_bwd(dx1, ob, yarb, ybb, cvb, p5b, x, vec_pre, vec, w_in, w_pool, w_bout, w_o, job):
    t, d = dx1.shape
    tm = GROUP
    gw = d // len(POOL_WINDOWS)
    nt = t // tm
    pool_rows = 8 * (POOL_WINDOWS[-1] - 1)

    def body(dx1_ref, o_ref, yar_ref, yb_ref, cv_ref, p5_ref, x_ref, vp_ref, vec_ref, win_ref, wp_ref, wb_ref, wo_ref,
             dqm_ref, dp_ref, gx_ref, red_ref, pred_ref, dpgcarry, dcvcarry):
        i = pl.program_id(0)
        tix = nt - 1 - i

        @pl.when(i == 0)
        def _():
            red_ref[...] = jnp.zeros_like(red_ref)
            pred_ref[...] = jnp.zeros_like(pred_ref)
            dpgcarry[...] = jnp.zeros_like(dpgcarry)
            dcvcarry[...] = jnp.zeros_like(dcvcarry)

        pscale = vec_ref[2:3, :]
        dx1_v = dx1_ref[...]
        dob = _postnorm_bwd(dx1_v, o_ref[...].astype(F32), vec_ref[0:1, :], vec_ref[1:2, :], red_ref).astype(BF16)
        dqm_ref[:, d:2 * d] = dob
        dm = _dot_nt(dob, wo_ref[...])

        def dproj(cols, value):
            vb = value.astype(BF16)
            dp_ref[:, cols] = vb
            return _dot_nt(vb, win_ref[:, cols])

        sa = jax.nn.sigmoid(p5_ref[:, 3 * d:4 * d].astype(F32))
        yar = yar_ref[...].astype(F32)
        dya = dm * sa
        dh = dproj(slice(4 * d, 5 * d), dm * (yar * pscale) * sa * (1.0 - sa))
        red_ref[2:3, :] += _colsum(dya * yar)
        dyarb = (dya * pscale).astype(BF16)
        dqm_ref[:, 2 * d:3 * d] = dyarb
        sb = jax.nn.sigmoid(p5_ref[:, 4 * d:5 * d].astype(F32))
        dybb = (dm * sb).astype(BF16)
        dqm_ref[:, 0:d] = dybb
        dh = dh + dproj(slice(5 * d, 6 * d), dm * yb_ref[...].astype(F32) * sb * (1.0 - sb))

        for g, window in enumerate(POOL_WINDOWS):
            cols = slice(g * gw, (g + 1) * gw)
            rows = 8 * (window - 1)
            dpg = _dot_nt(dyarb[:, cols], wp_ref[g])
            dpgs = dpg * _inv_count(tix * tm, window)
            halo = _halo_bottom(dpgs[:rows, :], dpgcarry[:rows, cols])
            dpgcarry[:, cols] = dpgs[:pool_rows, :]
            s, shift = jnp.concatenate([dpgs, halo], axis=0), 1
            while shift < window:
                s = s[:s.shape[0] - 8 * shift, :] + s[8 * shift:, :]
                shift *= 2
            dh = dh + dproj(cols, s - dpg)

        dq = _dot_nt(dybb, wb_ref[...])
        ux = p5_ref[:, 0:d].astype(F32)
        uc = p5_ref[:, 2 * d:3 * d].astype(F32)
        dh = dh + dproj(slice(2 * d, 3 * d), dq * cv_ref[...].astype(F32))
        dcv = dq * p5_ref[:, d:2 * d].astype(F32)
        taps = (vec_ref[4:5, :], vec_ref[5:6, :], vec_ref[6:7, :])
        dpv, (t0, t1, t2) = _conv3_bwd(dcv, dcvcarry, slice(0, d), taps, uc * ux)
        red_ref[3:4, :] += _colsum(dcv)
        red_ref[4:5, :] += t0
        red_ref[5:6, :] += t1
        red_ref[6:7, :] += t2
        dh = dh + dproj(slice(d, 2 * d), dpv * uc)
        dh = dh + dproj(slice(3 * d, 4 * d), dpv * ux)

        _, r, nh = _prenorm(_interleave(x_ref[...]), vp_ref)
        gx_ref[...] = _deinterleave(dx1_v + _prenorm_bwd(dh, r, nh, vp_ref, pred_ref))

    rev = lambda n: pl.BlockSpec((tm, n), lambda i: (nt - 1 - i, 0))
    return _call(
        body, "mixer_block_bwd", (nt,), [rev(d)] * 5 + [rev(5 * d), rev(d)] + [_vmem()] * 6,
        [rev(3 * d), rev(6 * d), rev(d), pl.BlockSpec((16, d), lambda i: (0, 0)),
         pl.BlockSpec((8, d), lambda i: (0, 0))],
        [_sds((t, 3 * d), BF16), _sds((t, 6 * d), BF16), _sds((t, d), F32), _sds((16, d), F32), _sds((8, d), F32)],
        [pltpu.VMEM((pool_rows, d), F32), pltpu.VMEM((16, d), F32)],
        _params("arbitrary"), (dx1, ob, yarb, ybb, cvb, p5b, x, vec_pre, vec, w_in, w_pool, w_bout, w_o), job)


def _matmul_tn(a, b, bm, bn, tk, by_col_block, name, job=None):
    t, m = a.shape
    n = b.shape[1]
    nk = t // tk
    parts = int(by_col_block)
    piece = bn // max(parts, 1)
    wide = _round_up(piece, LANES)

    def body(a_ref, b_ref, o_ref, acc_ref):
        k = pl.program_id(2)

        @pl.when(k == 0)
        def _():
            acc_ref[...] = jnp.zeros_like(acc_ref)

        acc_ref[...] += _dot_tn(a_ref[...], b_ref[...])

        @pl.when(k == nk - 1)
        def _():
            if parts:
                acc = acc_ref[...]
                for p in range(parts):
                    if wide > piece:
                        o_ref[p] = jnp.zeros((bm, wide), o_ref.dtype)
                    o_ref[p, :, 0:piece] = acc[:, p * piece:(p + 1) * piece].astype(o_ref.dtype)
            else:
                o_ref[...] = acc_ref[...].astype(o_ref.dtype)

    if by_col_block:
        out_shape = _sds((parts * n // bn, m, wide), BF16)
        out_spec = pl.BlockSpec((parts, bm, wide), lambda i, j, k: (j, i, 0))
    else:
        out_shape = _sds((m, n), BF16)
        out_spec = pl.BlockSpec((bm, bn), lambda i, j, k: (i, j))
    out = _call(body, name, (m // bm, n // bn, nk),
                [pl.BlockSpec((tk, bm), lambda i, j, k: (k, i)), pl.BlockSpec((tk, bn), lambda i, j, k: (k, j))],
                [out_spec], [out_shape], [pltpu.VMEM((bm, bn), F32)],
                _params("arbitrary", "arbitrary", "arbitrary"), (a, b), job)
    return out if job is not None else out[0]


def _side_by_side(blocks, name):
    n, r, c = blocks.shape
    rb = _row_block(r) // 2

    def body(in_ref, o_ref):
        for j in range(n):
            o_ref[:, j * c:(j + 1) * c] = in_ref[j]

    return pl.pallas_call(
        body, name=name, grid=(r // rb,), in_specs=[pl.BlockSpec((n, rb, c), lambda i: (0, i, 0))],
        out_specs=pl.BlockSpec((rb, n * c), lambda i: (i, 0)), out_shape=_sds((r, n * c), blocks.dtype),
        compiler_params=_params("parallel"))(blocks)


def _matmul_tn_groups(a, b, groups, tk, name, job=None):
    t, m = a.shape
    w = m // groups
    nk = t // tk

    def body(a_ref, b_ref, o_ref, acc_ref):
        k = pl.program_id(1)

        @pl.when(k == 0)
        def _():
            acc_ref[...] = jnp.zeros_like(acc_ref)

        acc_ref[...] += _dot_tn(a_ref[...], b_ref[...])

        @pl.when(k == nk - 1)
        def _():
            o_ref[...] = acc_ref[...].astype(o_ref.dtype)

    blk = pl.BlockSpec((tk, w), lambda g, k: (k, g))
    out = _call(body, name, (groups, nk), [blk, blk], [pl.BlockSpec((None, w, w), lambda g, k: (g, 0, 0))],
                [_sds((groups, w, w), BF16)], [pltpu.VMEM((w, w), F32)], _params("arbitrary", "arbitrary"), (a, b), job)
    return out if job is not None else out[0]


def _round_up(n, k):
    return (n + k - 1) // k * k


def _rows8(rows, width):
    n = _round_up(len(rows), 8)
    rows = list(rows) + [jnp.zeros((1, width), F32)] * (n - len(rows))
    return jnp.concatenate(rows, axis=0)


def kernel(x, c, g_pre_mix, g_post_mix, g_pre_ffn, g_post_ffn, w_ada, b_ada, w_in, w_pool, pool_scale, conv_w, conv_b, w_bout, w_o, w_up, ffn_conv_w, ffn_conv_b, w_down, loss_target, m_g_pre_mix, m_g_post_mix, m_g_pre_ffn, m_g_post_ffn, m_w_ada, m_b_ada, m_w_in, m_w_pool, m_pool_scale, m_conv_w, m_conv_b, m_w_bout, m_w_o, m_w_up, m_ffn_conv_w, m_ffn_conv_b, m_w_down, v_g_pre_mix, v_g_post_mix, v_g_pre_ffn, v_g_post_ffn, v_w_ada, v_b_ada, v_w_in, v_w_pool, v_pool_scale, v_conv_w, v_conv_b, v_w_bout, v_w_o, v_w_up, v_ffn_conv_w, v_ffn_conv_b, v_w_down):
    t, d = x.shape[1], x.shape[2]
    ngroups = len(POOL_WINDOWS)
    gw = d // ngroups
    ada_n = w_ada.shape[2]
    in_n = w_in.shape[2]
    up_n = w_up.shape[2]
    fp = NDEV * w_down.shape[1]

    xi, yi, ci = _position()
    me = _linear(xi, yi, ci)
    chip = 2 * xi + yi
    core = jnp.reshape(ci, (1,)).astype(jnp.int32)
    sel = jnp.stack([2 * chip + ci, chip]).astype(jnp.int32)

    x2 = x.reshape(t, d)
    target = loss_target.reshape(t, d)

    cw_n = conv_w.shape[2]
    pack = jnp.concatenate([c.reshape(1, d), conv_w[0].reshape(1, 3 * cw_n), ffn_conv_w[0].reshape(1, 3 * up_n)], axis=1)
    pack = jnp.pad(pack, ((0, 0), (0, _round_up(pack.shape[1], LANES) - pack.shape[1])))
    b_piece = lax.dynamic_slice_in_dim(b_ada, me * ada_n, ada_n, axis=1)
    mixer_weights = _allgather_job(
        [w_in[0].astype(BF16), w_bout[0].astype(BF16), w_o[0].astype(BF16), w_pool[0].astype(BF16)],
        ["cols", "rows", "rows", "mid"], 0.5, 0.75)
    gathered, mod_rows, w_in_f, g_bout, g_o, w_pool_f = _gather_weights_and_modulation(
        mixer_weights, pack, w_ada[0], b_piece, d)
    w_bout_f = g_bout.reshape(d, d)
    w_o_f = g_o.reshape(d, d)
    c_all = gathered[:, :d]
    c16 = jnp.pad(c_all, ((0, 8), (0, 0))).astype(BF16)
    conv_w_full = gathered[:, d:d + 3 * cw_n].reshape(NDEV, 3, cw_n).transpose(1, 0, 2).reshape(3, NDEV * cw_n)
    fcw_full = gathered[:, d + 3 * cw_n:d + 3 * cw_n + 3 * up_n].reshape(NDEV, 3, up_n)
    fcw_full = fcw_full.transpose(1, 0, 2).reshape(3, 2 * fp)
    fcv = jnp.concatenate([fcw_full, ffn_conv_b, jnp.zeros((4, 2 * fp), F32)], axis=0)
    mod = mod_rows.reshape(1, NDEV * ada_n)
    sh1, sc1, gt1, sh2, sc2, gt2 = [mod[:, k * d:(k + 1) * d] for k in range(6)]

    ffn_weights = _allgather_job([w_up[0].astype(BF16), w_down[0].astype(BF16)], ["rows", "rows"], 0.5, 0.8)
    vec_pre_mix = _rows8([g_pre_mix, 1.0 + sc1, sh1], d)
    vec_mix = _rows8([gt1, g_post_mix, pool_scale, conv_b, conv_w_full[0:1], conv_w_full[1:2], conv_w_full[2:3]], d)
    h1b, p5b, qmb, cvb, yarb, ybb, ob, x1, g_up, g_down = _mixer_block_fwd(
        x2, vec_pre_mix, vec_mix, w_in_f, w_pool_f, w_bout_f, w_o_f, ffn_weights)
    w_up_f = _side_by_side(g_up, "w_up_side_by_side")
    w_down_f = g_down.reshape(fp, d)
    vec_pre_ffn = _rows8([g_pre_ffn, 1.0 + sc2, sh2], d)
    vec_ffn = _rows8([gt2, g_post_ffn], d)
    h2b, upb, upreb, ab, ffb, dy, loss_part = _ffn_block_fwd(x1, target, vec_pre_ffn, vec_ffn, fcv, w_up_f, w_down_f)

    tk = min(2048, t)
    chip_sum = lambda gs, ss, name: _chip_sums(gs, ss, core, name)
    dffb, dupre, dx1, red_ffn, red_fconv, red_pre_ffn = _ffn_block_bwd(
        dy, ffb, x1, upb, upreb, vec_pre_ffn, vec_ffn, fcv, w_up_f, w_down_f)
    chunk = fp // _column_chunks(fp)
    gr_up = _matmul_tn(h2b, dupre, d, chunk, tk, chunk // up_n, "wgrad_up")
    gr_down = _matmul_tn(ab, dffb, chunk, d, tk, False, "wgrad_down").reshape(NDEV, fp // NDEV, d)
    sib_ffn = _run_job(_sibling_job([gr_up, gr_down]), "rs_sibling_ffn")
    dqmb, dproj, grad_x, red_mix, red_pre_mix = _mixer_block_bwd(
        dx1, ob, yarb, ybb, cvb, p5b, x2, vec_pre_mix, vec_mix, w_in_f, w_pool_f, w_bout_f, w_o_f, None)
    gr_in, fc_up, fc_down = _matmul_tn(h1b, dproj, d, in_n, tk, True, "wgrad_in",
                                       _chips_job(chip_sum([gr_up, gr_down], sib_ffn, "rs_chip_sum_ffn")))
    sib_in = _run_job(_sibling_job([gr_in]), "rs_sibling_in")
    gr_qmp, fc_in = _matmul_tn_groups(qmb, dqmb, 3, tk, "wgrad_bout_o_pool",
                                      _chips_job(chip_sum([gr_in], sib_in, "rs_chip_sum_in")))
    gr_bout = gr_qmp[0].reshape(NDEV, d // NDEV, d)
    gr_o = gr_qmp[1].reshape(NDEV, d // NDEV, d)
    gr_pool = jnp.stack([gr_qmp[2, g * gw:(g + 1) * gw, g * gw:(g + 1) * gw] for g in range(ngroups)])
    gr_pool = gr_pool.reshape(ngroups, NDEV, gw // NDEV, gw).transpose(1, 0, 2, 3).reshape(NDEV, -1, gw)
    rest = [gr_bout, gr_o, gr_pool]
    sib_rest = _run_job(_sibling_job(rest), "rs_sibling_rest")

    dmod = [red_pre_mix[0:1], red_pre_mix[1:2], red_mix[1:2], red_pre_ffn[0:1], red_pre_ffn[1:2], red_ffn[1:2]]
    small = [red_pre_mix[2:3], red_mix[0:1], red_pre_ffn[2:3], red_ffn[0:1], red_mix[2:3], red_mix[3:4],
             red_mix[4:5], red_mix[5:6], red_mix[6:7]] + dmod
    flat = jnp.concatenate(small + [red_fconv[0:4].reshape(1, 8 * fp), loss_part[0:1, 0:1]], axis=1)
    flat_n = flat.shape[1]
    width = 8 * LANES
    rows = _round_up(-(-flat_n // width), 8)
    flat = jnp.pad(flat, ((0, 0), (0, rows * width - flat_n))).reshape(rows, width)
    gat, tot, *fc_rest = _small_allreduce(flat, "allreduce_small_rs_rest",
                                          _chips_job(chip_sum(rest, sib_rest, "rs_chip_sum_rest")))

    def big(grad, from_sibling, from_chips, w, m, v, name):
        shape = w.shape
        w2, m2, v2 = [a.reshape((-1, shape[-1])) for a in (w, m, v)]
        outs = _reduce_adamw(grad, from_sibling, from_chips, sel, w2, m2, v2, name)
        return [a.reshape(shape) for a in outs]

    g_w_up, d_w_up, nm_w_up, nv_w_up = big(gr_up, sib_ffn[0], fc_up, w_up, m_w_up, v_w_up, "adamw_up")
    g_w_down, d_w_down, nm_w_down, nv_w_down = big(gr_down, sib_ffn[1], fc_down, w_down, m_w_down, v_w_down, "adamw_down")
    g_w_in, d_w_in, nm_w_in, nv_w_in = big(gr_in, sib_in[0], fc_in, w_in, m_w_in, v_w_in, "adamw_in")
    flat2 = lambda a: a.reshape((-1, a.shape[-1]))
    rest_w = [(w_bout, m_w_bout, v_w_bout), (w_o, m_w_o, v_w_o), (w_pool, m_w_pool, v_w_pool)]
    rest_out = _reduce_adamw_group(
        [(g, s, f, flat2(w), flat2(m), flat2(v)) for g, s, f, (w, m, v) in zip(rest, sib_rest, fc_rest, rest_w)],
        sel, "adamw_bout_o_pool")
    (g_w_bout, d_w_bout, nm_w_bout, nv_w_bout), (g_w_o, d_w_o, nm_w_o, nv_w_o), (g_w_pool, d_w_pool, nm_w_pool, nv_w_pool) = [
        [a.reshape(w.shape) for a in outs] for outs, (w, _, _) in zip(rest_out, rest_w)]

    tot = tot.reshape(1, rows * width)
    gat = gat.reshape(NDEV, rows * width)
    take = lambda k: tot[:, k * d:(k + 1) * d]
    g_g_pre_mix, g_g_post_mix, g_g_pre_ffn, g_g_post_ffn, g_pool_scale, g_conv_b = [take(k) for k in range(6)]
    g_conv_w_full = jnp.concatenate([take(6), take(7), take(8)], axis=0)
    g_conv_w = lax.dynamic_slice_in_dim(g_conv_w_full, me * cw_n, cw_n, axis=1)
    g_b_ada = tot[:, 9 * d:15 * d]
    dmod_all = gat[:, 9 * d:15 * d]
    fconv_tot = tot[:, 15 * d:15 * d + 8 * fp].reshape(4, 2 * fp)
    loss = 0.5 * tot[0, 15 * d + 8 * fp]
    g_ffn_conv_b = fconv_tot[3:4]
    g_ffn_conv_w = lax.dynamic_slice_in_dim(fconv_tot[0:3], me * up_n, up_n, axis=1)
    dmod_piece = lax.dynamic_slice_in_dim(dmod_all, me * ada_n, ada_n, axis=1)
    g_w_ada = _wada_grad(c16, jnp.pad(dmod_piece, ((0, 8), (0, 0))).astype(BF16))

    names_small = [(g_pre_mix, g_g_pre_mix, m_g_pre_mix, v_g_pre_mix), (g_post_mix, g_g_post_mix, m_g_post_mix, v_g_post_mix),
                   (g_pre_ffn, g_g_pre_ffn, m_g_pre_ffn, v_g_pre_ffn), (g_post_ffn, g_g_post_ffn, m_g_post_ffn, v_g_post_ffn),
                   (b_ada, g_b_ada, m_b_ada, v_b_ada), (pool_scale, g_pool_scale, m_pool_scale, v_pool_scale),
                   (conv_w, g_conv_w, m_conv_w, v_conv_w), (conv_b, g_conv_b, m_conv_b, v_conv_b),
                   (ffn_conv_w, g_ffn_conv_w, m_ffn_conv_w, v_ffn_conv_w), (ffn_conv_b, g_ffn_conv_b, m_ffn_conv_b, v_ffn_conv_b)]
    sizes = [w.size for w, _, _, _ in names_small]
    total = sum(sizes)
    prow = _round_up(-(-total // width), 8)

    def pack_small(k):
        a = jnp.concatenate([q[k].reshape(1, -1) for q in names_small], axis=1)
        return jnp.pad(a, ((0, 0), (0, prow * width - total)), constant_values=1.0).reshape(prow, width)

    ds, ms, vs = _adamw(pack_small(0), pack_small(1), pack_small(2), pack_small(3), "adamw_small")

    def unpack_small(a):
        a = a.reshape(-1)
        out, off = [], 0
        for (w, _, _, _), n in zip(names_small, sizes):
            out.append(a[off:off + n].reshape(w.shape))
            off += n
        return out

    (d_g_pre_mix, d_g_post_mix, d_g_pre_ffn, d_g_post_ffn, d_b_ada, d_pool_scale, d_conv_w, d_conv_b,
     d_ffn_conv_w, d_ffn_conv_b) = unpack_small(ds)
    (nm_g_pre_mix, nm_g_post_mix, nm_g_pre_ffn, nm_g_post_ffn, nm_b_ada, nm_pool_scale, nm_conv_w, nm_conv_b,
     nm_ffn_conv_w, nm_ffn_conv_b) = unpack_small(ms)
    (nv_g_pre_mix, nv_g_post_mix, nv_g_pre_ffn, nv_g_post_ffn, nv_b_ada, nv_pool_scale, nv_conv_w, nv_conv_b,
     nv_ffn_conv_w, nv_ffn_conv_b) = unpack_small(vs)
    d_w_ada, nm_w_ada, nv_w_ada = [a.reshape(w_ada.shape) for a in
                                   _adamw(w_ada[0], g_w_ada, m_w_ada[0], v_w_ada[0], "adamw_ada")]

    grads = [g_g_pre_mix, g_g_post_mix, g_g_pre_ffn, g_g_post_ffn, g_w_ada.reshape(w_ada.shape), g_b_ada, g_w_in,
             g_w_pool, g_pool_scale, g_conv_w.reshape(conv_w.shape), g_conv_b, g_w_bout, g_w_o, g_w_up,
             g_ffn_conv_w.reshape(ffn_conv_w.shape), g_ffn_conv_b, g_w_down]
    deltas = [d_g_pre_mix, d_g_post_mix, d_g_pre_ffn, d_g_post_ffn, d_w_ada, d_b_ada, d_w_in, d_w_pool, d_pool_scale,
              d_conv_w, d_conv_b, d_w_bout, d_w_o, d_w_up, d_ffn_conv_w, d_ffn_conv_b, d_w_down]
    new_m = [nm_g_pre_mix, nm_g_post_mix, nm_g_pre_ffn, nm_g_post_ffn, nm_w_ada, nm_b_ada, nm_w_in, nm_w_pool,
             nm_pool_scale, nm_conv_w, nm_conv_b, nm_w_bout, nm_w_o, nm_w_up, nm_ffn_conv_w, nm_ffn_conv_b, nm_w_down]
    new_v = [nv_g_pre_mix, nv_g_post_mix, nv_g_pre_ffn, nv_g_post_ffn, nv_w_ada, nv_b_ada, nv_w_in, nv_w_pool,
             nv_pool_scale, nv_conv_w, nv_conv_b, nv_w_bout, nv_w_o, nv_w_up, nv_ffn_conv_w, nv_ffn_conv_b, nv_w_down]
    return (loss, grad_x.reshape(x.shape), *grads, *deltas, *new_m, *new_v)
```

```python
import math

import jax
import jax.numpy as jnp
from jax import lax
from jax.experimental import pallas as pl
from jax.experimental.pallas import tpu as pltpu

F32 = jnp.float32
BF16 = jnp.bfloat16
MESH = pl.DeviceIdType.MESH

NDEV = 8
NCHIP = 4
EPS = 1e-6
POOL_WINDOWS = (2, 4, 8, 16)
LANES = 128
ADAM_LR = 0.001
ADAM_B1 = 0.9
ADAM_B2 = 0.999
ADAM_EPS = 1e-08
ADAM_WD = 0.01
ADAM_STEP = 10
GELU_C0 = math.sqrt(2.0 / math.pi)
GELU_C1 = 0.044715
VMEM_LIMIT = 56 * 2**20


def _vmem():
    return pl.BlockSpec(memory_space=pltpu.VMEM)


def _any():
    return pl.BlockSpec(memory_space=pl.ANY)


def _params(*sem):
    return pltpu.CompilerParams(dimension_semantics=sem, vmem_limit_bytes=VMEM_LIMIT)


def _sds(shape, dtype):
    return jax.ShapeDtypeStruct(tuple(shape), dtype)


def _position():
    return lax.axis_index("x"), lax.axis_index("y"), lax.axis_index("c")


def _linear(x, y, c):
    return 4 * x + 2 * y + c


def _dot(a, b):
    return jnp.dot(a, b, preferred_element_type=F32)


def _dot_nt(a, b):
    return lax.dot_general(a, b, (((1,), (1,)), ((), ())), preferred_element_type=F32)


def _dot_tn(a, b):
    return lax.dot_general(a, b, (((0,), (0,)), ((), ())), preferred_element_type=F32)


def _colsum(v):
    return jnp.sum(v, axis=0, keepdims=True)


def _rowmean(v):
    return jnp.mean(v, axis=-1, keepdims=True)


GROUP = 256


def _interleave(v):
    g, n = v.shape
    return jnp.swapaxes(v.reshape(8, g // 8, n), 0, 1).reshape(g, n)


def _deinterleave(v):
    g, n = v.shape
    return jnp.swapaxes(v.reshape(g // 8, 8, n), 0, 1).reshape(g, n)


def _halo_top(cur_last, prev_last):
    rows, n = cur_last.shape
    c3 = cur_last.reshape(rows // 8, 8, n)
    p3 = prev_last.reshape(rows // 8, 8, n)
    sub = lax.broadcasted_iota(jnp.int32, c3.shape, 1)
    return jnp.where(sub == 0, pltpu.roll(p3, 1, 1), pltpu.roll(c3, 1, 1)).reshape(rows, n)


def _halo_bottom(cur_first, next_first):
    rows, n = cur_first.shape
    c3 = cur_first.reshape(rows // 8, 8, n)
    n3 = next_first.reshape(rows // 8, 8, n)
    sub = lax.broadcasted_iota(jnp.int32, c3.shape, 1)
    return jnp.where(sub == 7, pltpu.roll(n3, 7, 1), pltpu.roll(c3, 7, 1)).reshape(rows, n)


def _shift_down(v, halo, k):
    rows = v.shape[0]
    return jnp.concatenate([halo[halo.shape[0] - 8 * k:, :], v[:rows - 8 * k, :]], axis=0)


def _shift_up(v, halo, k):
    return jnp.concatenate([v[8 * k:, :], halo[:8 * k, :]], axis=0)


def _inv_count(first_token, window):
    row = lax.broadcasted_iota(jnp.int32, (GROUP, 1), 0)
    t = first_token + (row % 8) * (GROUP // 8) + row // 8
    return 1.0 / jnp.minimum(t + 1, window).astype(F32)


class _Job:
    def __init__(self, inputs, out_shape, scratch, phases):
        self.inputs, self.out_shape, self.scratch, self.phases = list(inputs), list(out_shape), list(scratch), phases


def _call(body, name, grid, in_specs, out_specs, out_shape, scratch_shapes, params, operands, job=None):
    if job is None:
        return pl.pallas_call(body, name=name, grid=grid, in_specs=in_specs, out_specs=out_specs, out_shape=out_shape,
                              scratch_shapes=scratch_shapes, compiler_params=params)(*operands)
    n_in, n_out, n_scr = len(in_specs), len(out_specs), len(scratch_shapes)
    j_in, j_out = len(job.inputs), len(job.out_shape)
    steps = math.prod(grid)

    def hosted(*refs):
        own_in, refs = refs[:n_in], refs[n_in:]
        jin, refs = refs[:j_in], refs[j_in:]
        own_out, refs = refs[:n_out], refs[n_out:]
        jout, refs = refs[:j_out], refs[j_out:]
        own_scr, jscr = refs[:n_scr], refs[n_scr:]
        step = pl.program_id(0)
        for axis in range(1, len(grid)):
            step = step * grid[axis] + pl.program_id(axis)
        for frac, fn in job.phases[:-1]:
            pl.when(step == int(frac * (steps - 1)))(lambda fn=fn: fn(jin, jout, jscr))
        body(*own_in, *own_out, *own_scr)
        pl.when(step == steps - 1)(lambda: job.phases[-1][1](jin, jout, jscr))

    return pl.pallas_call(
        hosted, name=name, grid=grid, in_specs=list(in_specs) + [_any()] * j_in,
        out_specs=list(out_specs) + [_any()] * j_out, out_shape=list(out_shape) + job.out_shape,
        scratch_shapes=list(scratch_shapes) + job.scratch, compiler_params=params)(*operands, *job.inputs)


def _run_job(job, name):
    n_in, n_out = len(job.inputs), len(job.out_shape)

    def body(*refs):
        for _, fn in job.phases:
            fn(refs[:n_in], refs[n_in:n_in + n_out], refs[n_in + n_out:])

    return pl.pallas_call(body, name=name, out_shape=job.out_shape, in_specs=[_any()] * n_in,
                          out_specs=[_any()] * n_out, scratch_shapes=job.scratch)(*job.inputs)


def _peers(x, y, c):
    out = []
    for k in range(1, NDEV):
        out.append(((1 - x) if k & 4 else x, (1 - y) if k & 2 else y, (1 - c) if k & 1 else c))
    return out


def _small_allreduce(v, name, job):
    r, n = v.shape
    j_in, j_out = len(job.inputs), len(job.out_shape)

    def body(v_ref, *rest):
        jin, rest = rest[:j_in], rest[j_in:]
        gat_ref, sum_ref = rest[:2]
        jout, rest = rest[2:2 + j_out], rest[2 + j_out:]
        send_sems, recv_sems, local_sem = rest[:3]
        jscr = rest[3:]
        job.phases[0][1](jin, jout, jscr)
        x, y, c = _position()
        me = _linear(x, y, c)
        mine = pltpu.make_async_copy(v_ref, gat_ref.at[me], local_sem)
        mine.start()
        peers = _peers(x, y, c)
        sends = []
        for k, peer in enumerate(peers):
            cp = pltpu.make_async_remote_copy(src_ref=v_ref, dst_ref=gat_ref.at[me], send_sem=send_sems.at[k],
                                              recv_sem=recv_sems.at[k], device_id=peer, device_id_type=MESH)
            cp.start()
            sends.append(cp)
        for k, peer in enumerate(peers):
            pltpu.make_async_remote_copy(src_ref=v_ref, dst_ref=gat_ref.at[_linear(*peer)], send_sem=send_sems.at[k],
                                         recv_sem=recv_sems.at[k], device_id=peer, device_id_type=MESH).wait_recv()
        for cp in sends:
            cp.wait_send()
        mine.wait()
        acc = gat_ref[0]
        for j in range(1, NDEV):
            acc = acc + gat_ref[j]
        sum_ref[...] = acc
        job.phases[-1][1](jin, jout, jscr)

    return pl.pallas_call(
        body, name=name, out_shape=[_sds((NDEV, r, n), F32), _sds((r, n), F32)] + job.out_shape,
        in_specs=[_vmem()] + [_any()] * j_in, out_specs=[_vmem()] * 2 + [_any()] * j_out,
        scratch_shapes=[pltpu.SemaphoreType.DMA((NDEV - 1,)), pltpu.SemaphoreType.DMA((NDEV - 1,)),
                        pltpu.SemaphoreType.DMA(())] + job.scratch,
    )(v, *job.inputs)


def _exchange_rows(src_for, dst_ref, sems):
    send_sems, recv_sems, local_sem = sems
    x, y, c = _position()
    me = _linear(x, y, c)
    row = lambda j: dst_ref.at[pl.ds(j, 1), :]
    mine = pltpu.make_async_copy(src_for(me), row(me), local_sem)
    mine.start()
    peers = _peers(x, y, c)
    sends = []
    for k, peer in enumerate(peers):
        cp = pltpu.make_async_remote_copy(src_ref=src_for(_linear(*peer)), dst_ref=row(me), send_sem=send_sems.at[k],
                                          recv_sem=recv_sems.at[k], device_id=peer, device_id_type=MESH)
        cp.start()
        sends.append(cp)
    for k, peer in enumerate(peers):
        pltpu.make_async_remote_copy(src_ref=src_for(me), dst_ref=row(_linear(*peer)), send_sem=send_sems.at[k],
                                     recv_sem=recv_sems.at[k], device_id=peer, device_id_type=MESH).wait_recv()
    for cp in sends:
        cp.wait_send()
    mine.wait()


def _gather_weights_and_modulation(job, pack, w_ada, b_piece, d):
    n = pack.shape[1]
    m = w_ada.shape[1]
    j_in, j_out = len(job.inputs), len(job.out_shape)
    row_sems = [pltpu.SemaphoreType.DMA((NDEV - 1,)), pltpu.SemaphoreType.DMA((NDEV - 1,)), pltpu.SemaphoreType.DMA(())]

    def body(pack_ref, wada_ref, bp_ref, *rest):
        jin, rest = rest[:j_in], rest[j_in:]
        gat_ref, mod_ref = rest[:2]
        jout, rest = rest[2:2 + j_out], rest[2 + j_out:]
        sems1, sems2, piece, jscr = rest[0:3], rest[3:6], rest[6], rest[7:]
        phases = [fn for _, fn in job.phases]
        phases[0](jin, jout, jscr)
        _exchange_rows(lambda j: pack_ref, gat_ref, sems1)
        c16 = jnp.concatenate([gat_ref[:, 0:d], jnp.zeros((NDEV, d), F32)], axis=0).astype(BF16)
        piece[...] = (_dot(c16, wada_ref[...].astype(BF16)) + bp_ref[...])[0:NDEV, :]
        _exchange_rows(lambda j: piece.at[pl.ds(j, 1), :], mod_ref, sems2)
        for fn in phases[1:]:
            fn(jin, jout, jscr)

    return pl.pallas_call(
        body, name="gather_weights_and_modulation",
        out_shape=[_sds((NDEV, n), F32), _sds((NDEV, m), F32)] + job.out_shape,
        in_specs=[_vmem()] * 3 + [_any()] * j_in, out_specs=[_vmem()] * 2 + [_any()] * j_out,
        scratch_shapes=row_sems + row_sems + [pltpu.VMEM((NDEV, m), F32)] + job.scratch,
    )(pack, w_ada, b_piece, *job.inputs)


def _gathered(shard, layout):
    if layout == "rows":
        return (NDEV,) + shard.shape, lambda ref, j: ref.at[j]
    if layout == "cols":
        r, c = shard.shape
        return (r, NDEV * c), lambda ref, j: ref.at[:, pl.ds(pl.multiple_of(j * c, LANES), c)]
    g, r, c = shard.shape
    return (g, NDEV * r, c), lambda ref, j: ref.at[:, pl.ds(pl.multiple_of(j * r, 16), r), :]


def _allgather_job(shards, layouts, relay_at, forward_at):
    n = len(shards)
    specs = [_gathered(s, l) for s, l in zip(shards, layouts)]
    halves = [s.shape[0] // 2 for s in shards]

    def plan(src, dst, sems):
        send_sems, recv_sems, _ = sems
        x, y, c = _position()
        me, sibling, xn, yn, dg = (x, y, c), (x, y, 1 - c), (1 - x, y, c), (x, 1 - y, c), (1 - x, 1 - y, c)

        def copy(a, k, block, to, half=None, from_src=False):
            blk = specs[a][1](dst[a], _linear(*block))
            if half is not None:
                blk = blk.at[pl.ds(half * halves[a], halves[a])]
            return pltpu.make_async_remote_copy(src_ref=src[a] if from_src else blk, dst_ref=blk,
                                                send_sem=send_sems.at[a, k], recv_sem=recv_sems.at[a, k],
                                                device_id=to, device_id_type=MESH)
        return copy, me, sibling, xn, yn, dg

    def local(src, dst, sems):
        x, y, c = _position()
        return [pltpu.make_async_copy(src[a], specs[a][1](dst[a], _linear(x, y, c)), sems[2].at[a]) for a in range(n)]

    def own(src, dst, sems):
        copy, me, sibling, xn, yn, dg = plan(src, dst, sems)
        return [copy(a, k, me, to, from_src=True) for k, to in ((1, xn), (2, yn), (0, sibling)) for a in range(n)]

    def relayed(src, dst, sems):
        copy, me, sibling, xn, yn, dg = plan(src, dst, sems)
        return ([copy(a, 3, xn, yn, half=0) for a in range(n)] + [copy(a, 5, xn, sibling) for a in range(n)],
                [copy(a, 4, yn, xn, half=1) for a in range(n)] + [copy(a, 6, yn, sibling) for a in range(n)])

    def diagonal(src, dst, sems):
        copy, me, sibling, xn, yn, dg = plan(src, dst, sems)
        return [copy(a, 7, dg, sibling) for a in range(n)]

    def start(src, dst, sems):
        for cp in local(src, dst, sems) + own(src, dst, sems):
            cp.start()

    def relay(src, dst, sems):
        copy, me, sibling, xn, yn, dg = plan(src, dst, sems)
        from_x, from_y = relayed(src, dst, sems)
        for a in range(n):
            copy(a, 1, xn, me).wait_recv()
        for cp in from_x:
            cp.start()
        for a in range(n):
            copy(a, 2, yn, me).wait_recv()
        for cp in from_y:
            cp.start()

    def forward(src, dst, sems):
        copy, me, sibling, xn, yn, dg = plan(src, dst, sems)
        for a in range(n):
            copy(a, 3, dg, me, half=0).wait_recv()
            copy(a, 4, dg, me, half=1).wait_recv()
        for cp in diagonal(src, dst, sems):
            cp.start()

    def finish(src, dst, sems):
        copy, me, sibling, xn, yn, dg = plan(src, dst, sems)
        other = lambda dev: (dev[0], dev[1], 1 - dev[2])
        for a in range(n):
            copy(a, 0, sibling, me).wait_recv()
            for k, dev in ((5, xn), (6, yn), (7, dg)):
                copy(a, k, other(dev), me).wait_recv()
        from_x, from_y = relayed(src, dst, sems)
        for cp in own(src, dst, sems) + from_x + from_y + diagonal(src, dst, sems):
            cp.wait_send()
        for cp in local(src, dst, sems):
            cp.wait()

    return _Job(shards, [_sds(spec[0], s.dtype) for spec, s in zip(specs, shards)],
                [pltpu.SemaphoreType.DMA((n, 8)), pltpu.SemaphoreType.DMA((n, 8)), pltpu.SemaphoreType.DMA((n,))],
                [(0.0, start), (relay_at, relay), (forward_at, forward), (1.0, finish)])


def _sibling_job(grads):
    n = len(grads)

    def copies(src, dst, sems):
        x, y, c = _position()
        return [pltpu.make_async_remote_copy(src_ref=src[a].at[2 * q + 1 - c], dst_ref=dst[a].at[q],
                                             send_sem=sems[0].at[a, q], recv_sem=sems[1].at[a, q],
                                             device_id=(x, y, 1 - c), device_id_type=MESH)
                for a in range(n) for q in range(NCHIP)]

    return _exchange_job(grads, NCHIP, copies)


def _chips_job(chip_sums):
    n = len(chip_sums)

    def copies(src, dst, sems):
        x, y, c = _position()
        chips = [(1 - x, y), (x, 1 - y), (1 - x, 1 - y)]
        return [pltpu.make_async_remote_copy(src_ref=src[a].at[2 * chip[0] + chip[1]], dst_ref=dst[a].at[j],
                                             send_sem=sems[0].at[a, j], recv_sem=sems[1].at[a, j],
                                             device_id=(*chip, c), device_id_type=MESH)
                for j, chip in enumerate(chips) for a in range(n)]

    return _exchange_job(chip_sums, 3, copies)


def _exchange_job(arrays, slots, copies):
    n = len(arrays)

    def start(src, dst, sems):
        for cp in copies(src, dst, sems):
            cp.start()

    def finish(src, dst, sems):
        cps = copies(src, dst, sems)
        for cp in cps:
            cp.wait_recv()
        for cp in cps:
            cp.wait_send()

    return _Job(arrays, [_sds((slots,) + a.shape[1:], a.dtype) for a in arrays],
                [pltpu.SemaphoreType.DMA((n, slots)), pltpu.SemaphoreType.DMA((n, slots))],
                [(0.0, start), (1.0, finish)])


def _row_block(r):
    if r <= 512:
        return r
    for rb in range(512, 15, -16):
        if r % rb == 0:
            return rb
    return r


def _chip_sums(grads, from_sibling, core, name):
    n = len(grads)

    def body(core_ref, *refs):
        del core_ref
        for a in range(n):
            refs[2 * n + a][...] = (refs[a][...].astype(F32) + refs[n + a][...].astype(F32)).astype(BF16)

    block = lambda g, index: pl.BlockSpec((None,) + g.shape[1:], index)
    grid_spec = pltpu.PrefetchScalarGridSpec(
        num_scalar_prefetch=1, grid=(NCHIP,),
        in_specs=[block(g, lambda q, core: (2 * q + core[0], 0, 0)) for g in grads]
        + [block(g, lambda q, core: (q, 0, 0)) for g in grads],
        out_specs=[block(g, lambda q, core: (q, 0, 0)) for g in grads])
    return pl.pallas_call(body, name=name, grid_spec=grid_spec,
                          out_shape=[_sds((NCHIP,) + g.shape[1:], BF16) for g in grads],
                          compiler_params=_params("parallel"))(core, *grads, *from_sibling)


def _adamw_math(w, g, m, v):
    m2 = ADAM_B1 * m + (1.0 - ADAM_B1) * g
    v2 = ADAM_B2 * v + (1.0 - ADAM_B2) * jnp.square(g)
    m_hat = m2 / (1.0 - ADAM_B1 ** ADAM_STEP)
    v_hat = v2 / (1.0 - ADAM_B2 ** ADAM_STEP)
    delta = -ADAM_LR * (m_hat / (jnp.sqrt(v_hat) + ADAM_EPS) + ADAM_WD * w)
    return delta, m2, v2


def _adamw(w, g, m, v, name):
    r, c = w.shape
    rb = _row_block(r)

    def body(w_ref, g_ref, m_ref, v_ref, d_ref, m2_ref, v2_ref):
        d, m2, v2 = _adamw_math(w_ref[...], g_ref[...], m_ref[...], v_ref[...])
        d_ref[...] = d
        m2_ref[...] = m2
        v2_ref[...] = v2

    blk = pl.BlockSpec((rb, c), lambda i: (i, 0))
    return pl.pallas_call(body, name=name, grid=(r // rb,), in_specs=[blk] * 4, out_specs=[blk] * 3,
                          out_shape=[_sds((r, c), F32)] * 3, compiler_params=_params("parallel"))(w, g, m, v)


def _reduce_adamw(grad, from_sibling, from_chips, sel, w, m, v, name):
    r, c = w.shape
    cp = grad.shape[2]
    rb = _row_block(r)

    def body(sel_ref, g_ref, s_ref, c0_ref, c1_ref, c2_ref, w_ref, m_ref, v_ref, go_ref, d_ref, m2_ref, v2_ref):
        del sel_ref
        g = g_ref[...].astype(F32) + s_ref[...].astype(F32)
        g = g + c0_ref[...].astype(F32)
        g = g + c1_ref[...].astype(F32)
        g = g + c2_ref[...].astype(F32)
        g = g[:, 0:c]
        d, m2, v2 = _adamw_math(w_ref[...], g, m_ref[...], v_ref[...])
        go_ref[...] = g
        d_ref[...] = d
        m2_ref[...] = m2
        v2_ref[...] = v2

    blk = pl.BlockSpec((rb, c), lambda i, sel: (i, 0))
    grid_spec = pltpu.PrefetchScalarGridSpec(
        num_scalar_prefetch=1, grid=(r // rb,),
        in_specs=[pl.BlockSpec((None, rb, cp), lambda i, sel: (sel[0], i, 0)),
                  pl.BlockSpec((None, rb, cp), lambda i, sel: (sel[1], i, 0)),
                  pl.BlockSpec((None, rb, cp), lambda i, sel: (0, i, 0)),
                  pl.BlockSpec((None, rb, cp), lambda i, sel: (1, i, 0)),
                  pl.BlockSpec((None, rb, cp), lambda i, sel: (2, i, 0)),
                  blk, blk, blk],
        out_specs=[blk] * 4)
    return pl.pallas_call(body, name=name, grid_spec=grid_spec, out_shape=[_sds((r, c), F32)] * 4,
                          compiler_params=_params("parallel"))(sel, grad, from_sibling, from_chips, from_chips,
                                                               from_chips, w, m, v)


def _reduce_adamw_group(items, sel, name):
    n = len(items)

    def body(sel_ref, *refs):
        del sel_ref
        ins, outs = refs[:8 * n], refs[8 * n:]
        for a in range(n):
            g_ref, s_ref, c0_ref, c1_ref, c2_ref, w_ref, m_ref, v_ref = ins[8 * a:8 * a + 8]
            g = g_ref[...].astype(F32) + s_ref[...].astype(F32)
            g = g + c0_ref[...].astype(F32)
            g = g + c1_ref[...].astype(F32)
            g = g + c2_ref[...].astype(F32)
            d, m2, v2 = _adamw_math(w_ref[...], g, m_ref[...], v_ref[...])
            for ref, val in zip(outs[4 * a:4 * a + 4], (g, d, m2, v2)):
                ref[...] = val

    in_specs, out_specs, out_shape, operands = [], [], [], []
    for grad, from_sibling, from_chips, w, m, v in items:
        slot = lambda index, shape=grad.shape[1:]: pl.BlockSpec((None,) + shape, index)
        full = pl.BlockSpec(w.shape, lambda i, sel: (0, 0))
        in_specs += [slot(lambda i, sel: (sel[0], 0, 0)), slot(lambda i, sel: (sel[1], 0, 0)),
                     slot(lambda i, sel: (0, 0, 0)), slot(lambda i, sel: (1, 0, 0)), slot(lambda i, sel: (2, 0, 0)),
                     full, full, full]
        out_specs += [full] * 4
        out_shape += [_sds(w.shape, F32)] * 4
        operands += [grad, from_sibling, from_chips, from_chips, from_chips, w, m, v]
    grid_spec = pltpu.PrefetchScalarGridSpec(num_scalar_prefetch=1, grid=(1,), in_specs=in_specs, out_specs=out_specs)
    outs = pl.pallas_call(body, name=name, grid_spec=grid_spec, out_shape=out_shape,
                          compiler_params=_params("arbitrary"))(sel, *operands)
    return [outs[4 * a:4 * a + 4] for a in range(n)]


def _wada_grad(c_all, dmod_piece):
    d = c_all.shape[1]
    n = dmod_piece.shape[1]

    def body(c_ref, dm_ref, o_ref):
        o_ref[...] = _dot_tn(c_ref[...], dm_ref[...])

    return pl.pallas_call(body, name="ada_wgrad", out_shape=_sds((d, n), F32),
                          in_specs=[_vmem()] * 2, out_specs=_vmem())(c_all, dmod_piece)


def _column_chunks(width):
    for n in (4, 2):
        if width % (n * LANES) == 0:
            return n
    return 1


def _conv_taps(ref, col):
    return ref[0:1, col], ref[1:2, col], ref[2:3, col]


def _gelu_parts(u):
    th = jnp.tanh(GELU_C0 * (u + GELU_C1 * (u * u * u)))
    cdf = 0.5 * (1.0 + th)
    return cdf, th


def _conv3_bwd(dv, carry, col, taps, x):
    halo = _halo_bottom(dv[:16, :], carry[:, col])
    carry[:, col] = dv[:16, :]
    d1 = _shift_up(dv, halo, 1)
    d2 = _shift_up(dv, halo, 2)
    w0, w1, w2 = taps
    dx = w2 * dv
    dx = dx + w1 * d1
    dx = dx + w0 * d2
    return dx, (_colsum(d2 * x), _colsum(d1 * x), _colsum(dv * x))


def _prenorm(x, vp_ref):
    r = lax.rsqrt(_rowmean(x * x) + EPS)
    nh = x * r
    return (nh * vp_ref[0:1, :]) * vp_ref[1:2, :] + vp_ref[2:3, :], r, nh


def _prenorm_bwd(dh, r, nh, vp_ref, red_ref):
    g, sc1 = vp_ref[0:1, :], vp_ref[1:2, :]
    red_ref[0:1, :] += _colsum(dh)
    red_ref[1:2, :] += _colsum(dh * (nh * g))
    red_ref[2:3, :] += _colsum(dh * nh * sc1)
    dnh = dh * g * sc1
    return r * (dnh - nh * _rowmean(dnh * nh))


def _postnorm_bwd(dres, z, gate, gpost, red_ref):
    r = lax.rsqrt(_rowmean(z * z) + EPS)
    nh = z * r
    dn = dres * gate
    red_ref[0:1, :] += _colsum(dn * nh)
    red_ref[1:2, :] += _colsum(dres * (nh * gpost))
    dnh = dn * gpost
    return r * (dnh - nh * _rowmean(dnh * nh))


def _mixer_block_fwd(x, vec_pre, vec, w_in, w_pool, w_bout, w_o, job):
    t, d = x.shape
    tm = GROUP
    gw = d // len(POOL_WINDOWS)
    pool_rows = 8 * (POOL_WINDOWS[-1] - 1)

    def body(x_ref, vp_ref, vec_ref, win_ref, wp_ref, wb_ref, wo_ref,
             hb_ref, p5_ref, qm_ref, cv_ref, yar_ref, yb_ref, o_ref, x1_ref, mbuf, ucarry, pcarry):
        i = pl.program_id(0)

        @pl.when(i == 0)
        def _():
            ucarry[...] = jnp.zeros_like(ucarry)
            pcarry[...] = jnp.zeros_like(pcarry)

        xp = _interleave(x_ref[...])
        hb = _prenorm(xp, vp_ref)[0].astype(BF16)
        hb_ref[...] = hb
        proj = lambda k: _dot(hb, win_ref[:, k * d:(k + 1) * d])

        za = proj(4)
        p5_ref[:, 3 * d:4 * d] = za.astype(BF16)
        sa = jax.nn.sigmoid(za)
        u_pool = proj(0)
        for g, window in enumerate(POOL_WINDOWS):
            cols = slice(g * gw, (g + 1) * gw)
            rows = 8 * (window - 1)
            u = u_pool[:, cols]
            halo = _halo_top(u[tm - rows:, :], ucarry[pool_rows - rows:, cols])
            s, shift = jnp.concatenate([halo, u], axis=0), 1
            while shift < window:
                s = s[8 * shift:, :] + s[:s.shape[0] - 8 * shift, :]
                shift *= 2
            pgb = (s * _inv_count(i * tm, window) - u).astype(BF16)
            qm_ref[:, 2 * d + g * gw:2 * d + (g + 1) * gw] = pgb
            yar = _dot(pgb, wp_ref[g])
            yar_ref[:, cols] = yar
            mbuf[:, cols] = sa[:, cols] * (yar * vec_ref[2:3, cols])
        ucarry[...] = u_pool[tm - pool_rows:, :]

        ux = proj(1)
        uc = proj(3)
        p5_ref[:, 0:d] = ux.astype(BF16)
        p5_ref[:, 2 * d:3 * d] = uc.astype(BF16)
        p = uc * ux
        halo = _halo_top(p[tm - 16:, :], pcarry[...])
        pcarry[...] = p[tm - 16:, :]
        cv = vec_ref[3:4, :] + vec_ref[4:5, :] * _shift_down(p, halo, 2)
        cv = cv + vec_ref[5:6, :] * _shift_down(p, halo, 1)
        cv = cv + vec_ref[6:7, :] * p
        cv_ref[...] = cv
        ub = proj(2)
        p5_ref[:, d:2 * d] = ub.astype(BF16)
        qb = (ub * cv).astype(BF16)
        qm_ref[:, 0:d] = qb
        yb = _dot(qb, wb_ref[...])
        yb_ref[...] = yb

        zb = proj(5)
        p5_ref[:, 4 * d:5 * d] = zb.astype(BF16)
        mb = (mbuf[...] + jax.nn.sigmoid(zb) * yb).astype(BF16)
        qm_ref[:, d:2 * d] = mb
        o = _dot(mb, wo_ref[...])
        o_ref[...] = o
        r2 = lax.rsqrt(_rowmean(o * o) + EPS)
        x1_ref[...] = xp + vec_ref[0:1, :] * ((o * r2) * vec_ref[1:2, :])

    row = lambda n: pl.BlockSpec((tm, n), lambda i: (i, 0))
    widths = [d, 5 * d, 3 * d, d, d, d, d, d]
    return _call(
        body, "mixer_block_fwd", (t // tm,), [row(d)] + [_vmem()] * 6, [row(n) for n in widths],
        [_sds((t, n), BF16) for n in widths[:3]] + [_sds((t, n), F32) for n in widths[3:]],
        [pltpu.VMEM((tm, d), F32), pltpu.VMEM((pool_rows, d), F32), pltpu.VMEM((16, d), F32)],
        _params("arbitrary"), (x, vec_pre, vec, w_in, w_pool, w_bout, w_o), job)


def _ffn_block_fwd(x1, target, vec_pre, vec, fcv, w_up, w_down):
    t, d = x1.shape
    tm = GROUP
    fp = w_down.shape[0]
    nch = _column_chunks(fp)
    cw = fp // nch

    def body(x1_ref, tg_ref, vp_ref, vec_ref, fcv_ref, wu_ref, wd_ref,
             hb_ref, upb_ref, upreb_ref, a_ref, ffb_ref, dy_ref, loss_ref, carry):
        i = pl.program_id(0)

        @pl.when(i == 0)
        def _():
            carry[...] = jnp.zeros_like(carry)
            loss_ref[...] = jnp.zeros_like(loss_ref)

        x1 = x1_ref[...]
        hb = _prenorm(x1, vp_ref)[0].astype(BF16)
        hb_ref[...] = hb

        cols = [(slice(j * cw, (j + 1) * cw), slice(fp + j * cw, fp + (j + 1) * cw)) for j in range(nch)]
        up_gate = _dot(hb, wu_ref[:, 0:fp])
        up_val = _dot(hb, wu_ref[:, fp:2 * fp])

        def conv(v, col):
            halo = _halo_top(v[tm - 16:, :], carry[:, col])
            carry[:, col] = v[tm - 16:, :]
            w0, w1, w2 = _conv_taps(fcv_ref, col)
            y = fcv_ref[3:4, col] + w0 * _shift_down(v, halo, 2)
            y = y + w1 * _shift_down(v, halo, 1)
            y = y + w2 * v
            upb_ref[:, col] = y.astype(BF16)
            upreb_ref[:, col] = v.astype(BF16)
            return y

        ff = None
        for j in range(nch):
            gc, vc = cols[j]
            gate = conv(up_gate[:, gc], gc)
            val = conv(up_val[:, gc], vc)
            ab = ((gate * _gelu_parts(gate)[0]) * val).astype(BF16)
            a_ref[:, gc] = ab
            part = _dot(ab, wd_ref[gc, :])
            ff = part if ff is None else ff + part
        ffb_ref[...] = ff.astype(BF16)
        r4 = lax.rsqrt(_rowmean(ff * ff) + EPS)
        y = x1 + vec_ref[0:1, :] * ((ff * r4) * vec_ref[1:2, :])
        e = y - _interleave(tg_ref[...])
        dy_ref[...] = e * (1.0 / d)
        loss_ref[...] += jnp.sum(_rowmean(e * e))

    row = lambda n: pl.BlockSpec((tm, n), lambda i: (i, 0))
    return pl.pallas_call(
        body, name="ffn_block_fwd", grid=(t // tm,),
        in_specs=[row(d), row(d)] + [_vmem()] * 5,
        out_specs=[row(d), row(2 * fp), row(2 * fp), row(fp), row(d), row(d), pl.BlockSpec((8, LANES), lambda i: (0, 0))],
        out_shape=[_sds((t, d), BF16), _sds((t, 2 * fp), BF16), _sds((t, 2 * fp), BF16), _sds((t, fp), BF16),
                   _sds((t, d), BF16), _sds((t, d), F32), _sds((8, LANES), F32)],
        scratch_shapes=[pltpu.VMEM((16, 2 * fp), F32)],
        compiler_params=_params("arbitrary"),
    )(x1, target, vec_pre, vec, fcv, w_up, w_down)


def _ffn_block_bwd(dy, ffb, x1, upb, upreb, vec_pre, vec, fcv, w_up, w_down):
    t, d = dy.shape
    tm = GROUP
    fp = w_down.shape[0]
    nch = _column_chunks(fp)
    cw = fp // nch
    nt = t // tm

    def body(dy_ref, ff_ref, x1_ref, upb_ref, upreb_ref, vp_ref, vec_ref, fcv_ref, wu_ref, wd_ref,
             dff_ref, dup_ref, dx1_ref, red_ref, cred_ref, pred_ref, carry):
        @pl.when(pl.program_id(0) == 0)
        def _():
            carry[...] = jnp.zeros_like(carry)
            red_ref[...] = jnp.zeros_like(red_ref)
            cred_ref[...] = jnp.zeros_like(cred_ref)
            pred_ref[...] = jnp.zeros_like(pred_ref)

        dy_v = dy_ref[...]
        dffb = _postnorm_bwd(dy_v, ff_ref[...].astype(F32), vec_ref[0:1, :], vec_ref[1:2, :], red_ref).astype(BF16)
        dff_ref[...] = dffb

        def conv_bwd(dv, col):
            dx, (t0, t1, t2) = _conv3_bwd(dv, carry, col, _conv_taps(fcv_ref, col), upreb_ref[:, col].astype(F32))
            cred_ref[0:1, col] += t0
            cred_ref[1:2, col] += t1
            cred_ref[2:3, col] += t2
            cred_ref[3:4, col] += _colsum(dv)
            dxb = dx.astype(BF16)
            dup_ref[:, col] = dxb
            return _dot_nt(dxb, wu_ref[:, col])

        dh = None
        for j in range(nch):
            gc = slice(j * cw, (j + 1) * cw)
            vc = slice(fp + j * cw, fp + (j + 1) * cw)
            da = _dot_nt(dffb, wd_ref[gc, :])
            gate = upb_ref[:, gc].astype(F32)
            val = upb_ref[:, vc].astype(F32)
            cdf, th = _gelu_parts(gate)
            dcdf = 0.5 * (1.0 - th * th) * (GELU_C0 * (1.0 + (3.0 * GELU_C1) * (gate * gate)))
            part = conv_bwd(da * val * (cdf + gate * dcdf), gc) + conv_bwd(da * (gate * cdf), vc)
            dh = part if dh is None else dh + part

        _, r, nh = _prenorm(x1_ref[...], vp_ref)
        dx1_ref[...] = dy_v + _prenorm_bwd(dh, r, nh, vp_ref, pred_ref)

    rev = lambda n: pl.BlockSpec((tm, n), lambda i: (nt - 1 - i, 0))
    fixed = lambda n: pl.BlockSpec((8, n), lambda i: (0, 0))
    return pl.pallas_call(
        body, name="ffn_block_bwd", grid=(nt,),
        in_specs=[rev(d), rev(d), rev(d), rev(2 * fp), rev(2 * fp)] + [_vmem()] * 5,
        out_specs=[rev(d), rev(2 * fp), rev(d), fixed(d), fixed(2 * fp), fixed(d)],
        out_shape=[_sds((t, d), BF16), _sds((t, 2 * fp), BF16), _sds((t, d), F32), _sds((8, d), F32),
                   _sds((8, 2 * fp), F32), _sds((8, d), F32)],
        scratch_shapes=[pltpu.VMEM((16, 2 * fp), F32)],
        compiler_params=_params("arbitrary"),
    )(dy, ffb, x1, upb, upreb, vec_pre, vec, fcv, w_up, w_down)


def _mixer_block_bwd(dx1, ob, yarb, ybb, cvb, p5b, x, vec_pre, vec, w_in, w_pool, w_bout, w_o, job):
    t, d = dx1.shape
    tm = GROUP
    gw = d // len(POOL_WINDOWS)
    nt = t // tm
    pool_rows = 8 * (POOL_WINDOWS[-1] - 1)

    def body(dx1_ref, o_ref, yar_ref, yb_ref, cv_ref, p5_ref, x_ref, vp_ref, vec_ref, win_ref, wp_ref, wb_ref, wo_ref,
             dqm_ref, dp_ref, gx_ref, red_ref, pred_ref, dpgcarry, dcvcarry):
        i = pl.program_id(0)
        tix = nt - 1 - i

        @pl.when(i == 0)
        def _():
            red_ref[...] = jnp.zeros_like(red_ref)
            pred_ref[...] = jnp.zeros_like(pred_ref)
            dpgcarry[...] = jnp.zeros_like(dpgcarry)
            dcvcarry[...] = jnp.zeros_like(dcvcarry)

        pscale = vec_ref[2:3, :]
        dx1_v = dx1_ref[...]
        dob = _postnorm_bwd(dx1_v, o_ref[...].astype(F32), vec_ref[0:1, :], vec_ref[1:2, :], red_ref).astype(BF16)
        dqm_ref[:, d:2 * d] = dob
        dm = _dot_nt(dob, wo_ref[...])

        def dproj(cols, value):
            vb = value.astype(BF16)
            dp_ref[:, cols] = vb
            return _dot_nt(vb, win_ref[:, cols])

        sa = jax.nn.sigmoid(p5_ref[:, 3 * d:4 * d].astype(F32))
        yar = yar_ref[...].astype(F32)
        dya = dm * sa
        dh = dproj(slice(4 * d, 5 * d), dm * (yar * pscale) * sa * (1.0 - sa))
        red_ref[2:3, :] += _colsum(dya * yar)
        dyarb = (dya * pscale).astype(BF16)
        dqm_ref[:, 2 * d:3 * d] = dyarb
        sb = jax.nn.sigmoid(p5_ref[:, 4 * d:5 * d].astype(F32))
        dybb = (dm * sb).astype(BF16)
        dqm_ref[:, 0:d] = dybb
        dh = dh + dproj(slice(5 * d, 6 * d), dm * yb_ref[...].astype(F32) * sb * (1.0 - sb))

        for g, window in enumerate(POOL_WINDOWS):
            cols = slice(g * gw, (g + 1) * gw)
            rows = 8 * (window - 1)
            dpg = _dot_nt(dyarb[:, cols], wp_ref[g])
            dpgs = dpg * _inv_count(tix * tm, window)
            halo = _halo_bottom(dpgs[:rows, :], dpgcarry[:rows, cols])
            dpgcarry[:, cols] = dpgs[:pool_rows, :]
            s, shift = jnp.concatenate([dpgs, halo], axis=0), 1
            while shift < window:
                s = s[:s.shape[0] - 8 * shift, :] + s[8 * shift:, :]
                shift *= 2
            dh = dh + dproj(cols, s - dpg)

        dq = _dot_nt(dybb, wb_ref[...])
        ux = p5_ref[:, 0:d].astype(F32)
        uc = p5_ref[:, 2 * d:3 * d].astype(F32)
        dh = dh + dproj(slice(2 * d, 3 * d), dq * cv_ref[...].astype(F32))
        dcv = dq * p5_ref[:, d:2 * d].astype(F32)
        taps = (vec_ref[4:5, :], vec_ref[5:6, :], vec_ref[6:7, :])
        dpv, (t0, t1, t2) = _conv3_bwd(dcv, dcvcarry, slice(0, d), taps, uc * ux)
        red_ref[3:4, :] += _colsum(dcv)
        red_ref[4:5, :] += t0
        red_ref[5:6, :] += t1
        red_ref[6:7, :] += t2
        dh = dh + dproj(slice(d, 2 * d), dpv * uc)
        dh = dh + dproj(slice(3 * d, 4 * d), dpv * ux)

        _, r, nh = _prenorm(_interleave(x_ref[...]), vp_ref)
        gx_ref[...] = _deinterleave(dx1_v + _prenorm_bwd(dh, r, nh, vp_ref, pred_ref))

    rev = lambda n: pl.BlockSpec((tm, n), lambda i: (nt - 1 - i, 0))
    return _call(
        body, "mixer_block_bwd", (nt,), [rev(d)] * 5 + [rev(5 * d), rev(d)] + [_vmem()] * 6,
        [rev(3 * d), rev(6 * d), rev(d), pl.BlockSpec((16, d), lambda i: (0, 0)),
         pl.BlockSpec((8, d), lambda i: (0, 0))],
        [_sds((t, 3 * d), BF16), _sds((t, 6 * d), BF16), _sds((t, d), F32), _sds((16, d), F32), _sds((8, d), F32)],
        [pltpu.VMEM((pool_rows, d), F32), pltpu.VMEM((16, d), F32)],
        _params("arbitrary"), (dx1, ob, yarb, ybb, cvb, p5b, x, vec_pre, vec, w_in, w_pool, w_bout, w_o), job)


def _matmul_tn(a, b, bm, bn, tk, by_col_block, name, job=None):
    t, m = a.shape
    n = b.shape[1]
    nk = t // tk
    parts = int(by_col_block)
    piece = bn // max(parts, 1)
    wide = _round_up(piece, LANES)

    def body(a_ref, b_ref, o_ref, acc_ref):
        k = pl.program_id(2)

        @pl.when(k == 0)
        def _():
            acc_ref[...] = jnp.zeros_like(acc_ref)

        acc_ref[...] += _dot_tn(a_ref[...], b_ref[...])

        @pl.when(k == nk - 1)
        def _():
            if parts:
                acc = acc_ref[...]
                for p in range(parts):
                    if wide > piece:
                        o_ref[p] = jnp.zeros((bm, wide), o_ref.dtype)
                    o_ref[p, :, 0:piece] = acc[:, p * piece:(p + 1) * piece].astype(o_ref.dtype)
            else:
                o_ref[...] = acc_ref[...].astype(o_ref.dtype)

    if by_col_block:
        out_shape = _sds((parts * n // bn, m, wide), BF16)
        out_spec = pl.BlockSpec((parts, bm, wide), lambda i, j, k: (j, i, 0))
    else:
        out_shape = _sds((m, n), BF16)
        out_spec = pl.BlockSpec((bm, bn), lambda i, j, k: (i, j))
    out = _call(body, name, (m // bm, n // bn, nk),
                [pl.BlockSpec((tk, bm), lambda i, j, k: (k, i)), pl.BlockSpec((tk, bn), lambda i, j, k: (k, j))],
                [out_spec], [out_shape], [pltpu.VMEM((bm, bn), F32)],
                _params("arbitrary", "arbitrary", "arbitrary"), (a, b), job)
    return out if job is not None else out[0]


def _side_by_side(blocks, name):
    n, r, c = blocks.shape
    rb = _row_block(r) // 2

    def body(in_ref, o_ref):
        for j in range(n):
            o_ref[:, j * c:(j + 1) * c] = in_ref[j]

    return pl.pallas_call(
        body, name=name, grid=(r // rb,), in_specs=[pl.BlockSpec((n, rb, c), lambda i: (0, i, 0))],
        out_specs=pl.BlockSpec((rb, n * c), lambda i: (i, 0)), out_shape=_sds((r, n * c), blocks.dtype),
        compiler_params=_params("parallel"))(blocks)


def _matmul_tn_groups(a, b, groups, tk, name, job=None):
    t, m = a.shape
    w = m // groups
    nk = t // tk

    def body(a_ref, b_ref, o_ref, acc_ref):
        k = pl.program_id(1)

        @pl.when(k == 0)
        def _():
            acc_ref[...] = jnp.zeros_like(acc_ref)

        acc_ref[...] += _dot_tn(a_ref[...], b_ref[...])

        @pl.when(k == nk - 1)
        def _():
            o_ref[...] = acc_ref[...].astype(o_ref.dtype)

    blk = pl.BlockSpec((tk, w), lambda g, k: (k, g))
    out = _call(body, name, (groups, nk), [blk, blk], [pl.BlockSpec((None, w, w), lambda g, k: (g, 0, 0))],
                [_sds((groups, w, w), BF16)], [pltpu.VMEM((w, w), F32)], _params("arbitrary", "arbitrary"), (a, b), job)
    return out if job is not None else out[0]


def _round_up(n, k):
    return (n + k - 1) // k * k


def _rows8(rows, width):
    n = _round_up(len(rows), 8)
    rows = list(rows) + [jnp.zeros((1, width), F32)] * (n - len(rows))
    return jnp.concatenate(rows, axis=0)


def kernel(x, c, g_pre_mix, g_post_mix, g_pre_ffn, g_post_ffn, w_ada, b_ada, w_in, w_pool, pool_scale, conv_w, conv_b, w_bout, w_o, w_up, ffn_conv_w, ffn_conv_b, w_down, loss_target, m_g_pre_mix, m_g_post_mix, m_g_pre_ffn, m_g_post_ffn, m_w_ada, m_b_ada, m_w_in, m_w_pool, m_pool_scale, m_conv_w, m_conv_b, m_w_bout, m_w_o, m_w_up, m_ffn_conv_w, m_ffn_conv_b, m_w_down, v_g_pre_mix, v_g_post_mix, v_g_pre_ffn, v_g_post_ffn, v_w_ada, v_b_ada, v_w_in, v_w_pool, v_pool_scale, v_conv_w, v_conv_b, v_w_bout, v_w_o, v_w_up, v_ffn_conv_w, v_ffn_conv_b, v_w_down):
    t, d = x.shape[1], x.shape[2]
    ngroups = len(POOL_WINDOWS)
    gw = d // ngroups
    ada_n = w_ada.shape[2]
    in_n = w_in.shape[2]
    up_n = w_up.shape[2]
    fp = NDEV * w_down.shape[1]

    xi, yi, ci = _position()
    me = _linear(xi, yi, ci)
    chip = 2 * xi + yi
    core = jnp.reshape(ci, (1,)).astype(jnp.int32)
    sel = jnp.stack([2 * chip + ci, chip]).astype(jnp.int32)

    x2 = x.reshape(t, d)
    target = loss_target.reshape(t, d)

    cw_n = conv_w.shape[2]
    pack = jnp.concatenate([c.reshape(1, d), conv_w[0].reshape(1, 3 * cw_n), ffn_conv_w[0].reshape(1, 3 * up_n)], axis=1)
    pack = jnp.pad(pack, ((0, 0), (0, _round_up(pack.shape[1], LANES) - pack.shape[1])))
    b_piece = lax.dynamic_slice_in_dim(b_ada, me * ada_n, ada_n, axis=1)
    mixer_weights = _allgather_job(
        [w_in[0].astype(BF16), w_bout[0].astype(BF16), w_o[0].astype(BF16), w_pool[0].astype(BF16)],
        ["cols", "rows", "rows", "mid"], 0.5, 0.75)
    gathered, mod_rows, w_in_f, g_bout, g_o, w_pool_f = _gather_weights_and_modulation(
        mixer_weights, pack, w_ada[0], b_piece, d)
    w_bout_f = g_bout.reshape(d, d)
    w_o_f = g_o.reshape(d, d)
    c_all = gathered[:, :d]
    c16 = jnp.pad(c_all, ((0, 8), (0, 0))).astype(BF16)
    conv_w_full = gathered[:, d:d + 3 * cw_n].reshape(NDEV, 3, cw_n).transpose(1, 0, 2).reshape(3, NDEV * cw_n)
    fcw_full = gathered[:, d + 3 * cw_n:d + 3 * cw_n + 3 * up_n].reshape(NDEV, 3, up_n)
    fcw_full = fcw_full.transpose(1, 0, 2).reshape(3, 2 * fp)
    fcv = jnp.concatenate([fcw_full, ffn_conv_b, jnp.zeros((4, 2 * fp), F32)], axis=0)
    mod = mod_rows.reshape(1, NDEV * ada_n)
    sh1, sc1, gt1, sh2, sc2, gt2 = [mod[:, k * d:(k + 1) * d] for k in range(6)]

    ffn_weights = _allgather_job([w_up[0].astype(BF16), w_down[0].astype(BF16)], ["rows", "rows"], 0.5, 0.8)
    vec_pre_mix = _rows8([g_pre_mix, 1.0 + sc1, sh1], d)
    vec_mix = _rows8([gt1, g_post_mix, pool_scale, conv_b, conv_w_full[0:1], conv_w_full[1:2], conv_w_full[2:3]], d)
    h1b, p5b, qmb, cvb, yarb, ybb, ob, x1, g_up, g_down = _mixer_block_fwd(
        x2, vec_pre_mix, vec_mix, w_in_f, w_pool_f, w_bout_f, w_o_f, ffn_weights)
    w_up_f = _side_by_side(g_up, "w_up_side_by_side")
    w_down_f = g_down.reshape(fp, d)
    vec_pre_ffn = _rows8([g_pre_ffn, 1.0 + sc2, sh2], d)
    vec_ffn = _rows8([gt2, g_post_ffn], d)
    h2b, upb, upreb, ab, ffb, dy, loss_part = _ffn_block_fwd(x1, target, vec_pre_ffn, vec_ffn, fcv, w_up_f, w_down_f)

    tk, tk_wide = min(4096, t), min(2048, t)
    chip_sum = lambda gs, ss, name: _chip_sums(gs, ss, core, name)
    dffb, dupre, dx1, red_ffn, red_fconv, red_pre_ffn = _ffn_block_bwd(
        dy, ffb, x1, upb, upreb, vec_pre_ffn, vec_ffn, fcv, w_up_f, w_down_f)
    chunk = fp // _column_chunks(fp)
    gr_up = _matmul_tn(h2b, dupre, d, chunk, tk_wide, chunk // up_n, "wgrad_up")
    gr_down, sib_up = _matmul_tn(ab, dffb, chunk, d, tk_wide, False, "wgrad_down", _sibling_job([gr_up]))
    gr_down = gr_down.reshape(NDEV, fp // NDEV, d)
    sib_ffn = [sib_up] + list(_run_job(_sibling_job([gr_down]), "rs_sibling_down"))
    dqmb, dproj, grad_x, red_mix, red_pre_mix = _mixer_block_bwd(
        dx1, ob, yarb, ybb, cvb, p5b, x2, vec_pre_mix, vec_mix, w_in_f, w_pool_f, w_bout_f, w_o_f, None)
    gr_in, fc_up, fc_down = _matmul_tn(h1b, dproj, d, in_n, tk, True, "wgrad_in",
                                       _chips_job(chip_sum([gr_up, gr_down], sib_ffn, "rs_chip_sum_ffn")))
    sib_in = _run_job(_sibling_job([gr_in]), "rs_sibling_in")
    gr_qmp, fc_in = _matmul_tn_groups(qmb, dqmb, 3, tk, "wgrad_bout_o_pool",
                                      _chips_job(chip_sum([gr_in], sib_in, "rs_chip_sum_in")))
    gr_bout = gr_qmp[0].reshape(NDEV, d // NDEV, d)
    gr_o = gr_qmp[1].reshape(NDEV, d // NDEV, d)
    gr_pool = jnp.stack([gr_qmp[2, g * gw:(g + 1) * gw, g * gw:(g + 1) * gw] for g in range(ngroups)])
    gr_pool = gr_pool.reshape(ngroups, NDEV, gw // NDEV, gw).transpose(1, 0, 2, 3).reshape(NDEV, -1, gw)
    rest = [gr_bout, gr_o, gr_pool]
    sib_rest = _run_job(_sibling_job(rest), "rs_sibling_rest")

    dmod = [red_pre_mix[0:1], red_pre_mix[1:2], red_mix[1:2], red_pre_ffn[0:1], red_pre_ffn[1:2], red_ffn[1:2]]
    small = [red_pre_mix[2:3], red_mix[0:1], red_pre_ffn[2:3], red_ffn[0:1], red_mix[2:3], red_mix[3:4],
             red_mix[4:5], red_mix[5:6], red_mix[6:7]] + dmod
    flat = jnp.concatenate(small + [red_fconv[0:4].reshape(1, 8 * fp), loss_part[0:1, 0:1]], axis=1)
    flat_n = flat.shape[1]
    width = 8 * LANES
    rows = _round_up(-(-flat_n // width), 8)
    flat = jnp.pad(flat, ((0, 0), (0, rows * width - flat_n))).reshape(rows, width)
    gat, tot, *fc_rest = _small_allreduce(flat, "allreduce_small_rs_rest",
                                          _chips_job(chip_sum(rest, sib_rest, "rs_chip_sum_rest")))

    def big(grad, from_sibling, from_chips, w, m, v, name):
        shape = w.shape
        w2, m2, v2 = [a.reshape((-1, shape[-1])) for a in (w, m, v)]
        outs = _reduce_adamw(grad, from_sibling, from_chips, sel, w2, m2, v2, name)
        return [a.reshape(shape) for a in outs]

    g_w_up, d_w_up, nm_w_up, nv_w_up = big(gr_up, sib_ffn[0], fc_up, w_up, m_w_up, v_w_up, "adamw_up")
    g_w_down, d_w_down, nm_w_down, nv_w_down = big(gr_down, sib_ffn[1], fc_down, w_down, m_w_down, v_w_down, "adamw_down")
    g_w_in, d_w_in, nm_w_in, nv_w_in = big(gr_in, sib_in[0], fc_in, w_in, m_w_in, v_w_in, "adamw_in")
    flat2 = lambda a: a.reshape((-1, a.shape[-1]))
    rest_w = [(w_bout, m_w_bout, v_w_bout), (w_o, m_w_o, v_w_o), (w_pool, m_w_pool, v_w_pool)]
    rest_out = _reduce_adamw_group(
        [(g, s, f, flat2(w), flat2(m), flat2(v)) for g, s, f, (w, m, v) in zip(rest, sib_rest, fc_rest, rest_w)],
        sel, "adamw_bout_o_pool")
    (g_w_bout, d_w_bout, nm_w_bout, nv_w_bout), (g_w_o, d_w_o, nm_w_o, nv_w_o), (g_w_pool, d_w_pool, nm_w_pool, nv_w_pool) = [
        [a.reshape(w.shape) for a in outs] for outs, (w, _, _) in zip(rest_out, rest_w)]

    tot = tot.reshape(1, rows * width)
    gat = gat.reshape(NDEV, rows * width)
    take = lambda k: tot[:, k * d:(k + 1) * d]
    g_g_pre_mix, g_g_post_mix, g_g_pre_ffn, g_g_post_ffn, g_pool_scale, g_conv_b = [take(k) for k in range(6)]
    g_conv_w_full = jnp.concatenate([take(6), take(7), take(8)], axis=0)
    g_conv_w = lax.dynamic_slice_in_dim(g_conv_w_full, me * cw_n, cw_n, axis=1)
    g_b_ada = tot[:, 9 * d:15 * d]
    dmod_all = gat[:, 9 * d:15 * d]
    fconv_tot = tot[:, 15 * d:15 * d + 8 * fp].reshape(4, 2 * fp)
    loss = 0.5 * tot[0, 15 * d + 8 * fp]
    g_ffn_conv_b = fconv_tot[3:4]
    g_ffn_conv_w = lax.dynamic_slice_in_dim(fconv_tot[0:3], me * up_n, up_n, axis=1)
    dmod_piece = lax.dynamic_slice_in_dim(dmod_all, me * ada_n, ada_n, axis=1)
    g_w_ada = _wada_grad(c16, jnp.pad(dmod_piece, ((0, 8), (0, 0))).astype(BF16))

    names_small = [(g_pre_mix, g_g_pre_mix, m_g_pre_mix, v_g_pre_mix), (g_post_mix, g_g_post_mix, m_g_post_mix, v_g_post_mix),
                   (g_pre_ffn, g_g_pre_ffn, m_g_pre_ffn, v_g_pre_ffn), (g_post_ffn, g_g_post_ffn, m_g_post_ffn, v_g_post_ffn),
                   (b_ada, g_b_ada, m_b_ada, v_b_ada), (pool_scale, g_pool_scale, m_pool_scale, v_pool_scale),
                   (conv_w, g_conv_w, m_conv_w, v_conv_w), (conv_b, g_conv_b, m_conv_b, v_conv_b),
                   (ffn_conv_w, g_ffn_conv_w, m_ffn_conv_w, v_ffn_conv_w), (ffn_conv_b, g_ffn_conv_b, m_ffn_conv_b, v_ffn_conv_b)]
    sizes = [w.size for w, _, _, _ in names_small]
    total = sum(sizes)
    prow = _round_up(-(-total // width), 8)

    def pack_small(k):
        a = jnp.concatenate([q[k].reshape(1, -1) for q in names_small], axis=1)
        return jnp.pad(a, ((0, 0), (0, prow * width - total)), constant_values=1.0).reshape(prow, width)

    ds, ms, vs = _adamw(pack_small(0), pack_small(1), pack_small(2), pack_small(3), "adamw_small")

    def unpack_small(a):
        a = a.reshape(-1)
        out, off = [], 0
        for (w, _, _, _), n in zip(names_small, sizes):
            out.append(a[off:off + n].reshape(w.shape))
            off += n
        return out

    (d_g_pre_mix, d_g_post_mix, d_g_pre_ffn, d_g_post_ffn, d_b_ada, d_pool_scale, d_conv_w, d_conv_b,
     d_ffn_conv_w, d_ffn_conv_b) = unpack_small(ds)
    (nm_g_pre_mix, nm_g_post_mix, nm_g_pre_ffn, nm_g_post_ffn, nm_b_ada, nm_pool_scale, nm_conv_w, nm_conv_b,
     nm_ffn_conv_w, nm_ffn_conv_b) = unpack_small(ms)
    (nv_g_pre_mix, nv_g_post_mix, nv_g_pre_ffn, nv_g_post_ffn, nv_b_ada, nv_pool_scale, nv_conv_w, nv_conv_b,
     nv_ffn_conv_w, nv_ffn_conv_b) = unpack_small(vs)
    d_w_ada, nm_w_ada, nv_w_ada = [a.reshape(w_ada.shape) for a in
                                   _adamw(w_ada[0], g_w_ada, m_w_ada[0], v_w_ada[0], "adamw_ada")]

    grads = [g_g_pre_mix, g_g_post_mix, g_g_pre_ffn, g_g_post_ffn, g_w_ada.reshape(w_ada.shape), g_b_ada, g_w_in,
             g_w_pool, g_pool_scale, g_conv_w.reshape(conv_w.shape), g_conv_b, g_w_bout, g_w_o, g_w_up,
             g_ffn_conv_w.reshape(ffn_conv_w.shape), g_ffn_conv_b, g_w_down]
    deltas = [d_g_pre_mix, d_g_post_mix, d_g_pre_ffn, d_g_post_ffn, d_w_ada, d_b_ada, d_w_in, d_w_pool, d_pool_scale,
              d_conv_w, d_conv_b, d_w_bout, d_w_o, d_w_up, d_ffn_conv_w, d_ffn_conv_b, d_w_down]
    new_m = [nm_g_pre_mix, nm_g_post_mix, nm_g_pre_ffn, nm_g_post_ffn, nm_w_ada, nm_b_ada, nm_w_in, nm_w_pool,
             nm_pool_scale, nm_conv_w, nm_conv_b, nm_w_bout, nm_w_o, nm_w_up, nm_ffn_conv_w, nm_ffn_conv_b, nm_w_down]
    new_v = [nv_g_pre_mix, nv_g_post_mix, nv_g_pre_ffn, nv_g_post_ffn, nv_w_ada, nv_b_ada, nv_w_in, nv_w_pool,
             nv_pool_scale, nv_conv_w, nv_conv_b, nv_w_bout, nv_w_o, nv_w_up, nv_ffn_conv_w, nv_ffn_conv_b, nv_w_down]
    return (loss, grad_x.reshape(x.shape), *grads, *deltas, *new_m, *new_v)
```

```python
import math

import jax
import jax.numpy as jnp
from jax import lax
from jax.experimental import pallas as pl
from jax.experimental.pallas import tpu as pltpu

F32 = jnp.float32
BF16 = jnp.bfloat16
MESH = pl.DeviceIdType.MESH

NDEV = 8
NCHIP = 4
EPS = 1e-6
POOL_WINDOWS = (2, 4, 8, 16)
LANES = 128
ADAM_LR = 0.001
ADAM_B1 = 0.9
ADAM_B2 = 0.999
ADAM_EPS = 1e-08
ADAM_WD = 0.01
ADAM_STEP = 10
GELU_C0 = math.sqrt(2.0 / math.pi)
GELU_C1 = 0.044715
VMEM_LIMIT = 56 * 2**20


def _vmem():
    return pl.BlockSpec(memory_space=pltpu.VMEM)


def _any():
    return pl.BlockSpec(memory_space=pl.ANY)


def _params(*sem):
    return pltpu.CompilerParams(dimension_semantics=sem, vmem_limit_bytes=VMEM_LIMIT)


def _sds(shape, dtype):
    return jax.ShapeDtypeStruct(tuple(shape), dtype)


def _position():
    return lax.axis_index("x"), lax.axis_index("y"), lax.axis_index("c")


def _linear(x, y, c):
    return 4 * x + 2 * y + c


def _dot(a, b):
    return jnp.dot(a, b, preferred_element_type=F32)


def _dot_nt(a, b):
    return lax.dot_general(a, b, (((1,), (1,)), ((), ())), preferred_element_type=F32)


def _dot_tn(a, b):
    return lax.dot_general(a, b, (((0,), (0,)), ((), ())), preferred_element_type=F32)


def _colsum(v):
    return jnp.sum(v, axis=0, keepdims=True)


def _rowmean(v):
    return jnp.mean(v, axis=-1, keepdims=True)


GROUP = 256


def _interleave(v):
    g, n = v.shape
    return jnp.swapaxes(v.reshape(8, g // 8, n), 0, 1).reshape(g, n)


def _deinterleave(v):
    g, n = v.shape
    return jnp.swapaxes(v.reshape(g // 8, 8, n), 0, 1).reshape(g, n)


def _halo_top(cur_last, prev_last):
    rows, n = cur_last.shape
    c3 = cur_last.reshape(rows // 8, 8, n)
    p3 = prev_last.reshape(rows // 8, 8, n)
    sub = lax.broadcasted_iota(jnp.int32, c3.shape, 1)
    return jnp.where(sub == 0, pltpu.roll(p3, 1, 1), pltpu.roll(c3, 1, 1)).reshape(rows, n)


def _halo_bottom(cur_first, next_first):
    rows, n = cur_first.shape
    c3 = cur_first.reshape(rows // 8, 8, n)
    n3 = next_first.reshape(rows // 8, 8, n)
    sub = lax.broadcasted_iota(jnp.int32, c3.shape, 1)
    return jnp.where(sub == 7, pltpu.roll(n3, 7, 1), pltpu.roll(c3, 7, 1)).reshape(rows, n)


def _shift_down(v, halo, k):
    rows = v.shape[0]
    return jnp.concatenate([halo[halo.shape[0] - 8 * k:, :], v[:rows - 8 * k, :]], axis=0)


def _shift_up(v, halo, k):
    return jnp.concatenate([v[8 * k:, :], halo[:8 * k, :]], axis=0)


def _inv_count(first_token, window):
    row = lax.broadcasted_iota(jnp.int32, (GROUP, 1), 0)
    t = first_token + (row % 8) * (GROUP // 8) + row // 8
    return 1.0 / jnp.minimum(t + 1, window).astype(F32)


class _Job:
    def __init__(self, inputs, out_shape, scratch, phases):
        self.inputs, self.out_shape, self.scratch, self.phases = list(inputs), list(out_shape), list(scratch), phases


def _call(body, name, grid, in_specs, out_specs, out_shape, scratch_shapes, params, operands, job=None):
    if job is None:
        return pl.pallas_call(body, name=name, grid=grid, in_specs=in_specs, out_specs=out_specs, out_shape=out_shape,
                              scratch_shapes=scratch_shapes, compiler_params=params)(*operands)
    n_in, n_out, n_scr = len(in_specs), len(out_specs), len(scratch_shapes)
    j_in, j_out = len(job.inputs), len(job.out_shape)
    steps = math.prod(grid)

    def hosted(*refs):
        own_in, refs = refs[:n_in], refs[n_in:]
        jin, refs = refs[:j_in], refs[j_in:]
        own_out, refs = refs[:n_out], refs[n_out:]
        jout, refs = refs[:j_out], refs[j_out:]
        own_scr, jscr = refs[:n_scr], refs[n_scr:]
        step = pl.program_id(0)
        for axis in range(1, len(grid)):
            step = step * grid[axis] + pl.program_id(axis)
        for frac, fn in job.phases[:-1]:
            pl.when(step == int(frac * (steps - 1)))(lambda fn=fn: fn(jin, jout, jscr))
        body(*own_in, *own_out, *own_scr)
        pl.when(step == steps - 1)(lambda: job.phases[-1][1](jin, jout, jscr))

    return pl.pallas_call(
        hosted, name=name, grid=grid, in_specs=list(in_specs) + [_any()] * j_in,
        out_specs=list(out_specs) + [_any()] * j_out, out_shape=list(out_shape) + job.out_shape,
        scratch_shapes=list(scratch_shapes) + job.scratch, compiler_params=params)(*operands, *job.inputs)


def _run_job(job, name):
    n_in, n_out = len(job.inputs), len(job.out_shape)

    def body(*refs):
        for _, fn in job.phases:
            fn(refs[:n_in], refs[n_in:n_in + n_out], refs[n_in + n_out:])

    return pl.pallas_call(body, name=name, out_shape=job.out_shape, in_specs=[_any()] * n_in,
                          out_specs=[_any()] * n_out, scratch_shapes=job.scratch)(*job.inputs)


def _peers(x, y, c):
    out = []
    for k in range(1, NDEV):
        out.append(((1 - x) if k & 4 else x, (1 - y) if k & 2 else y, (1 - c) if k & 1 else c))
    return out


def _small_allreduce(v, name, job):
    r, n = v.shape
    j_in, j_out = len(job.inputs), len(job.out_shape)

    def body(v_ref, *rest):
        jin, rest = rest[:j_in], rest[j_in:]
        gat_ref, sum_ref = rest[:2]
        jout, rest = rest[2:2 + j_out], rest[2 + j_out:]
        send_sems, recv_sems, local_sem = rest[:3]
        jscr = rest[3:]
        job.phases[0][1](jin, jout, jscr)
        x, y, c = _position()
        me = _linear(x, y, c)
        mine = pltpu.make_async_copy(v_ref, gat_ref.at[me], local_sem)
        mine.start()
        peers = _peers(x, y, c)
        sends = []
        for k, peer in enumerate(peers):
            cp = pltpu.make_async_remote_copy(src_ref=v_ref, dst_ref=gat_ref.at[me], send_sem=send_sems.at[k],
                                              recv_sem=recv_sems.at[k], device_id=peer, device_id_type=MESH)
            cp.start()
            sends.append(cp)
        for k, peer in enumerate(peers):
            pltpu.make_async_remote_copy(src_ref=v_ref, dst_ref=gat_ref.at[_linear(*peer)], send_sem=send_sems.at[k],
                                         recv_sem=recv_sems.at[k], device_id=peer, device_id_type=MESH).wait_recv()
        for cp in sends:
            cp.wait_send()
        mine.wait()
        acc = gat_ref[0]
        for j in range(1, NDEV):
            acc = acc + gat_ref[j]
        sum_ref[...] = acc
        job.phases[-1][1](jin, jout, jscr)

    return pl.pallas_call(
        body, name=name, out_shape=[_sds((NDEV, r, n), F32), _sds((r, n), F32)] + job.out_shape,
        in_specs=[_vmem()] + [_any()] * j_in, out_specs=[_vmem()] * 2 + [_any()] * j_out,
        scratch_shapes=[pltpu.SemaphoreType.DMA((NDEV - 1,)), pltpu.SemaphoreType.DMA((NDEV - 1,)),
                        pltpu.SemaphoreType.DMA(())] + job.scratch,
    )(v, *job.inputs)


def _exchange_rows(src_for, dst_ref, sems):
    send_sems, recv_sems, local_sem = sems
    x, y, c = _position()
    me = _linear(x, y, c)
    row = lambda j: dst_ref.at[pl.ds(j, 1), :]
    mine = pltpu.make_async_copy(src_for(me), row(me), local_sem)
    mine.start()
    peers = _peers(x, y, c)
    sends = []
    for k, peer in enumerate(peers):
        cp = pltpu.make_async_remote_copy(src_ref=src_for(_linear(*peer)), dst_ref=row(me), send_sem=send_sems.at[k],
                                          recv_sem=recv_sems.at[k], device_id=peer, device_id_type=MESH)
        cp.start()
        sends.append(cp)
    for k, peer in enumerate(peers):
        pltpu.make_async_remote_copy(src_ref=src_for(me), dst_ref=row(_linear(*peer)), send_sem=send_sems.at[k],
                                     recv_sem=recv_sems.at[k], device_id=peer, device_id_type=MESH).wait_recv()
    for cp in sends:
        cp.wait_send()
    mine.wait()


def _gather_weights_and_modulation(job, pack, w_ada, b_piece, d):
    n = pack.shape[1]
    m = w_ada.shape[1]
    j_in, j_out = len(job.inputs), len(job.out_shape)
    row_sems = [pltpu.SemaphoreType.DMA((NDEV - 1,)), pltpu.SemaphoreType.DMA((NDEV - 1,)), pltpu.SemaphoreType.DMA(())]

    def body(pack_ref, wada_ref, bp_ref, *rest):
        jin, rest = rest[:j_in], rest[j_in:]
        gat_ref, mod_ref = rest[:2]
        jout, rest = rest[2:2 + j_out], rest[2 + j_out:]
        sems1, sems2, piece, jscr = rest[0:3], rest[3:6], rest[6], rest[7:]
        phases = [fn for _, fn in job.phases]
        phases[0](jin, jout, jscr)
        _exchange_rows(lambda j: pack_ref, gat_ref, sems1)
        c16 = jnp.concatenate([gat_ref[:, 0:d], jnp.zeros((NDEV, d), F32)], axis=0).astype(BF16)
        piece[...] = (_dot(c16, wada_ref[...].astype(BF16)) + bp_ref[...])[0:NDEV, :]
        _exchange_rows(lambda j: piece.at[pl.ds(j, 1), :], mod_ref, sems2)
        for fn in phases[1:]:
            fn(jin, jout, jscr)

    return pl.pallas_call(
        body, name="gather_weights_and_modulation",
        out_shape=[_sds((NDEV, n), F32), _sds((NDEV, m), F32)] + job.out_shape,
        in_specs=[_vmem()] * 3 + [_any()] * j_in, out_specs=[_vmem()] * 2 + [_any()] * j_out,
        scratch_shapes=row_sems + row_sems + [pltpu.VMEM((NDEV, m), F32)] + job.scratch,
    )(pack, w_ada, b_piece, *job.inputs)


def _gathered(shard, layout):
    if layout == "rows":
        return (NDEV,) + shard.shape, lambda ref, j: ref.at[j]
    if layout == "cols":
        r, c = shard.shape
        return (r, NDEV * c), lambda ref, j: ref.at[:, pl.ds(pl.multiple_of(j * c, LANES), c)]
    g, r, c = shard.shape
    return (g, NDEV * r, c), lambda ref, j: ref.at[:, pl.ds(pl.multiple_of(j * r, 16), r), :]


def _allgather_job(shards, layouts, relay_at, forward_at):
    n = len(shards)
    specs = [_gathered(s, l) for s, l in zip(shards, layouts)]
    halves = [s.shape[0] // 2 for s in shards]

    def plan(src, dst, sems):
        send_sems, recv_sems, _ = sems
        x, y, c = _position()
        me, sibling, xn, yn, dg = (x, y, c), (x, y, 1 - c), (1 - x, y, c), (x, 1 - y, c), (1 - x, 1 - y, c)

        def copy(a, k, block, to, half=None, from_src=False):
            blk = specs[a][1](dst[a], _linear(*block))
            if half is not None:
                blk = blk.at[pl.ds(half * halves[a], halves[a])]
            return pltpu.make_async_remote_copy(src_ref=src[a] if from_src else blk, dst_ref=blk,
                                                send_sem=send_sems.at[a, k], recv_sem=recv_sems.at[a, k],
                                                device_id=to, device_id_type=MESH)
        return copy, me, sibling, xn, yn, dg

    def local(src, dst, sems):
        x, y, c = _position()
        return [pltpu.make_async_copy(src[a], specs[a][1](dst[a], _linear(x, y, c)), sems[2].at[a]) for a in range(n)]

    def own(src, dst, sems):
        copy, me, sibling, xn, yn, dg = plan(src, dst, sems)
        return [copy(a, k, me, to, from_src=True) for k, to in ((1, xn), (2, yn), (0, sibling)) for a in range(n)]

    def relayed(src, dst, sems):
        copy, me, sibling, xn, yn, dg = plan(src, dst, sems)
        return ([copy(a, 3, xn, yn, half=0) for a in range(n)] + [copy(a, 5, xn, sibling) for a in range(n)],
                [copy(a, 4, yn, xn, half=1) for a in range(n)] + [copy(a, 6, yn, sibling) for a in range(n)])

    def diagonal(src, dst, sems):
        copy, me, sibling, xn, yn, dg = plan(src, dst, sems)
        return [copy(a, 7, dg, sibling) for a in range(n)]

    def start(src, dst, sems):
        for cp in local(src, dst, sems) + own(src, dst, sems):
            cp.start()

    def relay(src, dst, sems):
        copy, me, sibling, xn, yn, dg = plan(src, dst, sems)
        from_x, from_y = relayed(src, dst, sems)
        for a in range(n):
            copy(a, 1, xn, me).wait_recv()
        for cp in from_x:
            cp.start()
        for a in range(n):
            copy(a, 2, yn, me).wait_recv()
        for cp in from_y:
            cp.start()

    def forward(src, dst, sems):
        copy, me, sibling, xn, yn, dg = plan(src, dst, sems)
        for a in range(n):
            copy(a, 3, dg, me, half=0).wait_recv()
            copy(a, 4, dg, me, half=1).wait_recv()
        for cp in diagonal(src, dst, sems):
            cp.start()

    def finish(src, dst, sems):
        copy, me, sibling, xn, yn, dg = plan(src, dst, sems)
        other = lambda dev: (dev[0], dev[1], 1 - dev[2])
        for a in range(n):
            copy(a, 0, sibling, me).wait_recv()
            for k, dev in ((5, xn), (6, yn), (7, dg)):
                copy(a, k, other(dev), me).wait_recv()
        from_x, from_y = relayed(src, dst, sems)
        for cp in own(src, dst, sems) + from_x + from_y + diagonal(src, dst, sems):
            cp.wait_send()
        for cp in local(src, dst, sems):
            cp.wait()

    return _Job(shards, [_sds(spec[0], s.dtype) for spec, s in zip(specs, shards)],
                [pltpu.SemaphoreType.DMA((n, 8)), pltpu.SemaphoreType.DMA((n, 8)), pltpu.SemaphoreType.DMA((n,))],
                [(0.0, start), (relay_at, relay), (forward_at, forward), (1.0, finish)])


def _sibling_job(grads):
    n = len(grads)

    def copies(src, dst, sems):
        x, y, c = _position()
        return [pltpu.make_async_remote_copy(src_ref=src[a].at[2 * q + 1 - c], dst_ref=dst[a].at[q],
                                             send_sem=sems[0].at[a, q], recv_sem=sems[1].at[a, q],
                                             device_id=(x, y, 1 - c), device_id_type=MESH)
                for a in range(n) for q in range(NCHIP)]

    return _exchange_job(grads, NCHIP, copies)


def _chips_job(chip_sums):
    n = len(chip_sums)

    def copies(src, dst, sems):
        x, y, c = _position()
        chips = [(1 - x, y), (x, 1 - y), (1 - x, 1 - y)]
        return [pltpu.make_async_remote_copy(src_ref=src[a].at[2 * chip[0] + chip[1]], dst_ref=dst[a].at[j],
                                             send_sem=sems[0].at[a, j], recv_sem=sems[1].at[a, j],
                                             device_id=(*chip, c), device_id_type=MESH)
                for j, chip in enumerate(chips) for a in range(n)]

    return _exchange_job(chip_sums, 3, copies)


def _exchange_job(arrays, slots, copies):
    n = len(arrays)

    def start(src, dst, sems):
        for cp in copies(src, dst, sems):
            cp.start()

    def finish(src, dst, sems):
        cps = copies(src, dst, sems)
        for cp in cps:
            cp.wait_recv()
        for cp in cps:
            cp.wait_send()

    return _Job(arrays, [_sds((slots,) + a.shape[1:], a.dtype) for a in arrays],
                [pltpu.SemaphoreType.DMA((n, slots)), pltpu.SemaphoreType.DMA((n, slots))],
                [(0.0, start), (1.0, finish)])


def _row_block(r):
    if r <= 512:
        return r
    for rb in range(512, 15, -16):
        if r % rb == 0:
            return rb
    return r


def _chip_sums(grads, from_sibling, core, name):
    n = len(grads)

    def body(core_ref, *refs):
        del core_ref
        for a in range(n):
            refs[2 * n + a][...] = (refs[a][...].astype(F32) + refs[n + a][...].astype(F32)).astype(BF16)

    block = lambda g, index: pl.BlockSpec((None,) + g.shape[1:], index)
    grid_spec = pltpu.PrefetchScalarGridSpec(
        num_scalar_prefetch=1, grid=(NCHIP,),
        in_specs=[block(g, lambda q, core: (2 * q + core[0], 0, 0)) for g in grads]
        + [block(g, lambda q, core: (q, 0, 0)) for g in grads],
        out_specs=[block(g, lambda q, core: (q, 0, 0)) for g in grads])
    return pl.pallas_call(body, name=name, grid_spec=grid_spec,
                          out_shape=[_sds((NCHIP,) + g.shape[1:], BF16) for g in grads],
                          compiler_params=_params("parallel"))(core, *grads, *from_sibling)


def _adamw_math(w, g, m, v):
    m2 = ADAM_B1 * m + (1.0 - ADAM_B1) * g
    v2 = ADAM_B2 * v + (1.0 - ADAM_B2) * jnp.square(g)
    m_hat = m2 / (1.0 - ADAM_B1 ** ADAM_STEP)
    v_hat = v2 / (1.0 - ADAM_B2 ** ADAM_STEP)
    delta = -ADAM_LR * (m_hat / (jnp.sqrt(v_hat) + ADAM_EPS) + ADAM_WD * w)
    return delta, m2, v2


def _adamw(w, g, m, v, name):
    r, c = w.shape
    rb = _row_block(r)

    def body(w_ref, g_ref, m_ref, v_ref, d_ref, m2_ref, v2_ref):
        d, m2, v2 = _adamw_math(w_ref[...], g_ref[...], m_ref[...], v_ref[...])
        d_ref[...] = d
        m2_ref[...] = m2
        v2_ref[...] = v2

    blk = pl.BlockSpec((rb, c), lambda i: (i, 0))
    return pl.pallas_call(body, name=name, grid=(r // rb,), in_specs=[blk] * 4, out_specs=[blk] * 3,
                          out_shape=[_sds((r, c), F32)] * 3, compiler_params=_params("parallel"))(w, g, m, v)


def _reduce_adamw(grad, from_sibling, from_chips, sel, w, m, v, name, transposed=False):
    r, c = w.shape[::-1] if transposed else w.shape
    cp = grad.shape[2]
    rb = _row_block(r)

    def body(sel_ref, g_ref, s_ref, c0_ref, c1_ref, c2_ref, w_ref, m_ref, v_ref, go_ref, d_ref, m2_ref, v2_ref):
        del sel_ref
        g = g_ref[...].astype(F32) + s_ref[...].astype(F32)
        g = g + c0_ref[...].astype(F32)
        g = g + c1_ref[...].astype(F32)
        g = g + c2_ref[...].astype(F32)
        g = g.T[0:c, :] if transposed else g[:, 0:c]
        d, m2, v2 = _adamw_math(w_ref[...], g, m_ref[...], v_ref[...])
        go_ref[...] = g
        d_ref[...] = d
        m2_ref[...] = m2
        v2_ref[...] = v2

    blk = pl.BlockSpec((c, rb), lambda i, sel: (0, i)) if transposed else pl.BlockSpec((rb, c), lambda i, sel: (i, 0))
    grid_spec = pltpu.PrefetchScalarGridSpec(
        num_scalar_prefetch=1, grid=(r // rb,),
        in_specs=[pl.BlockSpec((None, rb, cp), lambda i, sel: (sel[0], i, 0)),
                  pl.BlockSpec((None, rb, cp), lambda i, sel: (sel[1], i, 0)),
                  pl.BlockSpec((None, rb, cp), lambda i, sel: (0, i, 0)),
                  pl.BlockSpec((None, rb, cp), lambda i, sel: (1, i, 0)),
                  pl.BlockSpec((None, rb, cp), lambda i, sel: (2, i, 0)),
                  blk, blk, blk],
        out_specs=[blk] * 4)
    return pl.pallas_call(body, name=name, grid_spec=grid_spec, out_shape=[_sds(w.shape, F32)] * 4,
                          compiler_params=_params("parallel"))(sel, grad, from_sibling, from_chips, from_chips,
                                                               from_chips, w, m, v)


def _reduce_adamw_group(items, sel, name):
    n = len(items)

    def body(sel_ref, *refs):
        del sel_ref
        ins, outs = refs[:8 * n], refs[8 * n:]
        for a in range(n):
            g_ref, s_ref, c0_ref, c1_ref, c2_ref, w_ref, m_ref, v_ref = ins[8 * a:8 * a + 8]
            g = g_ref[...].astype(F32) + s_ref[...].astype(F32)
            g = g + c0_ref[...].astype(F32)
            g = g + c1_ref[...].astype(F32)
            g = g + c2_ref[...].astype(F32)
            d, m2, v2 = _adamw_math(w_ref[...], g, m_ref[...], v_ref[...])
            for ref, val in zip(outs[4 * a:4 * a + 4], (g, d, m2, v2)):
                ref[...] = val

    in_specs, out_specs, out_shape, operands = [], [], [], []
    for grad, from_sibling, from_chips, w, m, v in items:
        slot = lambda index, shape=grad.shape[1:]: pl.BlockSpec((None,) + shape, index)
        full = pl.BlockSpec(w.shape, lambda i, sel: (0, 0))
        in_specs += [slot(lambda i, sel: (sel[0], 0, 0)), slot(lambda i, sel: (sel[1], 0, 0)),
                     slot(lambda i, sel: (0, 0, 0)), slot(lambda i, sel: (1, 0, 0)), slot(lambda i, sel: (2, 0, 0)),
                     full, full, full]
        out_specs += [full] * 4
        out_shape += [_sds(w.shape, F32)] * 4
        operands += [grad, from_sibling, from_chips, from_chips, from_chips, w, m, v]
    grid_spec = pltpu.PrefetchScalarGridSpec(num_scalar_prefetch=1, grid=(1,), in_specs=in_specs, out_specs=out_specs)
    outs = pl.pallas_call(body, name=name, grid_spec=grid_spec, out_shape=out_shape,
                          compiler_params=_params("arbitrary"))(sel, *operands)
    return [outs[4 * a:4 * a + 4] for a in range(n)]


def _wada_grad(c_all, dmod_piece):
    d = c_all.shape[1]
    n = dmod_piece.shape[1]

    def body(c_ref, dm_ref, o_ref):
        o_ref[...] = _dot_tn(c_ref[...], dm_ref[...])

    return pl.pallas_call(body, name="ada_wgrad", out_shape=_sds((d, n), F32),
                          in_specs=[_vmem()] * 2, out_specs=_vmem())(c_all, dmod_piece)


def _column_chunks(width):
    for n in (4, 2):
        if width % (n * LANES) == 0:
            return n
    return 1


def _conv_taps(ref, col):
    return ref[0:1, col], ref[1:2, col], ref[2:3, col]


def _gelu_parts(u):
    th = jnp.tanh(GELU_C0 * (u + GELU_C1 * (u * u * u)))
    cdf = 0.5 * (1.0 + th)
    return cdf, th


def _conv3_bwd(dv, carry, col, taps, x):
    halo = _halo_bottom(dv[:16, :], carry[:, col])
    carry[:, col] = dv[:16, :]
    d1 = _shift_up(dv, halo, 1)
    d2 = _shift_up(dv, halo, 2)
    w0, w1, w2 = taps
    dx = w2 * dv
    dx = dx + w1 * d1
    dx = dx + w0 * d2
    return dx, (_colsum(d2 * x), _colsum(d1 * x), _colsum(dv * x))


def _prenorm(x, vp_ref):
    r = lax.rsqrt(_rowmean(x * x) + EPS)
    nh = x * r
    return (nh * vp_ref[0:1, :]) * vp_ref[1:2, :] + vp_ref[2:3, :], r, nh


def _prenorm_bwd(dh, r, nh, vp_ref, red_ref):
    g, sc1 = vp_ref[0:1, :], vp_ref[1:2, :]
    red_ref[0:1, :] += _colsum(dh)
    red_ref[1:2, :] += _colsum(dh * (nh * g))
    red_ref[2:3, :] += _colsum(dh * nh * sc1)
    dnh = dh * g * sc1
    return r * (dnh - nh * _rowmean(dnh * nh))


def _postnorm_bwd(dres, z, gate, gpost, red_ref):
    r = lax.rsqrt(_rowmean(z * z) + EPS)
    nh = z * r
    dn = dres * gate
    red_ref[0:1, :] += _colsum(dn * nh)
    red_ref[1:2, :] += _colsum(dres * (nh * gpost))
    dnh = dn * gpost
    return r * (dnh - nh * _rowmean(dnh * nh))


def _mixer_block_fwd(x, vec_pre, vec, w_in, w_pool, w_bout, w_o, job):
    t, d = x.shape
    tm = GROUP
    gw = d // len(POOL_WINDOWS)
    pool_rows = 8 * (POOL_WINDOWS[-1] - 1)

    def body(x_ref, vp_ref, vec_ref, win_ref, wp_ref, wb_ref, wo_ref,
             hb_ref, p5_ref, qm_ref, cv_ref, yar_ref, yb_ref, o_ref, x1_ref, mbuf, ucarry, pcarry):
        i = pl.program_id(0)

        @pl.when(i == 0)
        def _():
            ucarry[...] = jnp.zeros_like(ucarry)
            pcarry[...] = jnp.zeros_like(pcarry)

        xp = _interleave(x_ref[...])
        hb = _prenorm(xp, vp_ref)[0].astype(BF16)
        hb_ref[...] = hb
        proj = lambda k: _dot(hb, win_ref[:, k * d:(k + 1) * d])

        za = proj(4)
        p5_ref[:, 3 * d:4 * d] = za.astype(BF16)
        sa = jax.nn.sigmoid(za)
        u_pool = proj(0)
        for g, window in enumerate(POOL_WINDOWS):
            cols = slice(g * gw, (g + 1) * gw)
            rows = 8 * (window - 1)
            u = u_pool[:, cols]
            halo = _halo_top(u[tm - rows:, :], ucarry[pool_rows - rows:, cols])
            s, shift = jnp.concatenate([halo, u], axis=0), 1
            while shift < window:
                s = s[8 * shift:, :] + s[:s.shape[0] - 8 * shift, :]
                shift *= 2
            pgb = (s * _inv_count(i * tm, window) - u).astype(BF16)
            qm_ref[:, 2 * d + g * gw:2 * d + (g + 1) * gw] = pgb
            yar = _dot(pgb, wp_ref[g])
            yar_ref[:, cols] = yar
            mbuf[:, cols] = sa[:, cols] * (yar * vec_ref[2:3, cols])
        ucarry[...] = u_pool[tm - pool_rows:, :]

        ux = proj(1)
        uc = proj(3)
        p5_ref[:, 0:d] = ux.astype(BF16)
        p5_ref[:, 2 * d:3 * d] = uc.astype(BF16)
        p = uc * ux
        halo = _halo_top(p[tm - 16:, :], pcarry[...])
        pcarry[...] = p[tm - 16:, :]
        cv = vec_ref[3:4, :] + vec_ref[4:5, :] * _shift_down(p, halo, 2)
        cv = cv + vec_ref[5:6, :] * _shift_down(p, halo, 1)
        cv = cv + vec_ref[6:7, :] * p
        cv_ref[...] = cv
        ub = proj(2)
        p5_ref[:, d:2 * d] = ub.astype(BF16)
        qb = (ub * cv).astype(BF16)
        qm_ref[:, 0:d] = qb
        yb = _dot(qb, wb_ref[...])
        yb_ref[...] = yb

        zb = proj(5)
        p5_ref[:, 4 * d:5 * d] = zb.astype(BF16)
        mb = (mbuf[...] + jax.nn.sigmoid(zb) * yb).astype(BF16)
        qm_ref[:, d:2 * d] = mb
        o = _dot(mb, wo_ref[...])
        o_ref[...] = o
        r2 = lax.rsqrt(_rowmean(o * o) + EPS)
        x1_ref[...] = xp + vec_ref[0:1, :] * ((o * r2) * vec_ref[1:2, :])

    row = lambda n: pl.BlockSpec((tm, n), lambda i: (i, 0))
    widths = [d, 5 * d, 3 * d, d, d, d, d, d]
    return _call(
        body, "mixer_block_fwd", (t // tm,), [row(d)] + [_vmem()] * 6, [row(n) for n in widths],
        [_sds((t, n), BF16) for n in widths[:3]] + [_sds((t, n), F32) for n in widths[3:]],
        [pltpu.VMEM((tm, d), F32), pltpu.VMEM((pool_rows, d), F32), pltpu.VMEM((16, d), F32)],
        _params("arbitrary"), (x, vec_pre, vec, w_in, w_pool, w_bout, w_o), job)


def _ffn_block_fwd(x1, target, vec_pre, vec, fcv, w_up, w_down):
    t, d = x1.shape
    tm = GROUP
    fp = w_down.shape[0]
    nch = _column_chunks(fp)
    cw = fp // nch

    def body(x1_ref, tg_ref, vp_ref, vec_ref, fcv_ref, wu_ref, wd_ref,
             hb_ref, upb_ref, upreb_ref, a_ref, ffb_ref, dy_ref, loss_ref, carry):
        i = pl.program_id(0)

        @pl.when(i == 0)
        def _():
            carry[...] = jnp.zeros_like(carry)
            loss_ref[...] = jnp.zeros_like(loss_ref)

        x1 = x1_ref[...]
        hb = _prenorm(x1, vp_ref)[0].astype(BF16)
        hb_ref[...] = hb

        cols = [(slice(j * cw, (j + 1) * cw), slice(fp + j * cw, fp + (j + 1) * cw)) for j in range(nch)]
        up_gate = _dot(hb, wu_ref[:, 0:fp])
        up_val = _dot(hb, wu_ref[:, fp:2 * fp])

        def conv(v, col):
            halo = _halo_top(v[tm - 16:, :], carry[:, col])
            carry[:, col] = v[tm - 16:, :]
            w0, w1, w2 = _conv_taps(fcv_ref, col)
            y = fcv_ref[3:4, col] + w0 * _shift_down(v, halo, 2)
            y = y + w1 * _shift_down(v, halo, 1)
            y = y + w2 * v
            upb_ref[:, col] = y.astype(BF16)
            upreb_ref[:, col] = v.astype(BF16)
            return y

        ff = None
        for j in range(nch):
            gc, vc = cols[j]
            gate = conv(up_gate[:, gc], gc)
            val = conv(up_val[:, gc], vc)
            ab = ((gate * _gelu_parts(gate)[0]) * val).astype(BF16)
            a_ref[:, gc] = ab
            part = _dot(ab, wd_ref[gc, :])
            ff = part if ff is None else ff + part
        ffb_ref[...] = ff.astype(BF16)
        r4 = lax.rsqrt(_rowmean(ff * ff) + EPS)
        y = x1 + vec_ref[0:1, :] * ((ff * r4) * vec_ref[1:2, :])
        e = y - _interleave(tg_ref[...])
        dy_ref[...] = e * (1.0 / d)
        loss_ref[...] += jnp.sum(_rowmean(e * e))

    row = lambda n: pl.BlockSpec((tm, n), lambda i: (i, 0))
    return pl.pallas_call(
        body, name="ffn_block_fwd", grid=(t // tm,),
        in_specs=[row(d), row(d)] + [_vmem()] * 5,
        out_specs=[row(d), row(2 * fp), row(2 * fp), row(fp), row(d), row(d), pl.BlockSpec((8, LANES), lambda i: (0, 0))],
        out_shape=[_sds((t, d), BF16), _sds((t, 2 * fp), BF16), _sds((t, 2 * fp), BF16), _sds((t, fp), BF16),
                   _sds((t, d), BF16), _sds((t, d), F32), _sds((8, LANES), F32)],
        scratch_shapes=[pltpu.VMEM((16, 2 * fp), F32)],
        compiler_params=_params("arbitrary"),
    )(x1, target, vec_pre, vec, fcv, w_up, w_down)


def _ffn_block_bwd(dy, ffb, x1, upb, upreb, vec_pre, vec, fcv, w_up, w_down):
    t, d = dy.shape
    tm = GROUP
    fp = w_down.shape[0]
    nch = _column_chunks(fp)
    cw = fp // nch
    nt = t // tm

    def body(dy_ref, ff_ref, x1_ref, upb_ref, upreb_ref, vp_ref, vec_ref, fcv_ref, wu_ref, wd_ref,
             dff_ref, dup_ref, dx1_ref, red_ref, cred_ref, pred_ref, carry):
        @pl.when(pl.program_id(0) == 0)
        def _():
            carry[...] = jnp.zeros_like(carry)
            red_ref[...] = jnp.zeros_like(red_ref)
            cred_ref[...] = jnp.zeros_like(cred_ref)
            pred_ref[...] = jnp.zeros_like(pred_ref)

        dy_v = dy_ref[...]
        dffb = _postnorm_bwd(dy_v, ff_ref[...].astype(F32), vec_ref[0:1, :], vec_ref[1:2, :], red_ref).astype(BF16)
        dff_ref[...] = dffb

        def conv_bwd(dv, col):
            dx, (t0, t1, t2) = _conv3_bwd(dv, carry, col, _conv_taps(fcv_ref, col), upreb_ref[:, col].astype(F32))
            cred_ref[0:1, col] += t0
            cred_ref[1:2, col] += t1
            cred_ref[2:3, col] += t2
            cred_ref[3:4, col] += _colsum(dv)
            dxb = dx.astype(BF16)
            dup_ref[:, col] = dxb
            return _dot_nt(dxb, wu_ref[:, col])

        dh = None
        for j in range(nch):
            gc = slice(j * cw, (j + 1) * cw)
            vc = slice(fp + j * cw, fp + (j + 1) * cw)
            da = _dot_nt(dffb, wd_ref[gc, :])
            gate = upb_ref[:, gc].astype(F32)
            val = upb_ref[:, vc].astype(F32)
            cdf, th = _gelu_parts(gate)
            dcdf = 0.5 * (1.0 - th * th) * (GELU_C0 * (1.0 + (3.0 * GELU_C1) * (gate * gate)))
            part = conv_bwd(da * val * (cdf + gate * dcdf), gc) + conv_bwd(da * (gate * cdf), vc)
            dh = part if dh is None else dh + part

        _, r, nh = _prenorm(x1_ref[...], vp_ref)
        dx1_ref[...] = dy_v + _prenorm_bwd(dh, r, nh, vp_ref, pred_ref)

    rev = lambda n: pl.BlockSpec((tm, n), lambda i: (nt - 1 - i, 0))
    fixed = lambda n: pl.BlockSpec((8, n), lambda i: (0, 0))
    return pl.pallas_call(
        body, name="ffn_block_bwd", grid=(nt,),
        in_specs=[rev(d), rev(d), rev(d), rev(2 * fp), rev(2 * fp)] + [_vmem()] * 5,
        out_specs=[rev(d), rev(2 * fp), rev(d), fixed(d), fixed(2 * fp), fixed(d)],
        out_shape=[_sds((t, d), BF16), _sds((t, 2 * fp), BF16), _sds((t, d), F32), _sds((8, d), F32),
                   _sds((8, 2 * fp), F32), _sds((8, d), F32)],
        scratch_shapes=[pltpu.VMEM((16, 2 * fp), F32)],
        compiler_params=_params("arbitrary"),
    )(dy, ffb, x1, upb, upreb, vec_pre, vec, fcv, w_up, w_down)


def _mixer_block_bwd(dx1, ob, yarb, ybb, cvb, p5b, x, vec_pre, vec, w_in, w_pool, w_bout, w_o, job):
    t, d = dx1.shape
    tm = GROUP
    gw = d // len(POOL_WINDOWS)
    nt = t // tm
    pool_rows = 8 * (POOL_WINDOWS[-1] - 1)

    def body(dx1_ref, o_ref, yar_ref, yb_ref, cv_ref, p5_ref, x_ref, vp_ref, vec_ref, win_ref, wp_ref, wb_ref, wo_ref,
             dqm_ref, dp_ref, gx_ref, red_ref, pred_ref, dpgcarry, dcvcarry):
        i = pl.program_id(0)
        tix = nt - 1 - i

        @pl.when(i == 0)
        def _():
            red_ref[...] = jnp.zeros_like(red_ref)
            pred_ref[...] = jnp.zeros_like(pred_ref)
            dpgcarry[...] = jnp.zeros_like(dpgcarry)
            dcvcarry[...] = jnp.zeros_like(dcvcarry)

        pscale = vec_ref[2:3, :]
        dx1_v = dx1_ref[...]
        dob = _postnorm_bwd(dx1_v, o_ref[...].astype(F32), vec_ref[0:1, :], vec_ref[1:2, :], red_ref).astype(BF16)
        dqm_ref[:, d:2 * d] = dob
        dm = _dot_nt(dob, wo_ref[...])

        def dproj(cols, value):
            vb = value.astype(BF16)
            dp_ref[:, cols] = vb
            return _dot_nt(vb, win_ref[:, cols])

        sa = jax.nn.sigmoid(p5_ref[:, 3 * d:4 * d].astype(F32))
        yar = yar_ref[...].astype(F32)
        dya = dm * sa
        dh = dproj(slice(4 * d, 5 * d), dm * (yar * pscale) * sa * (1.0 - sa))
        red_ref[2:3, :] += _colsum(dya * yar)
        dyarb = (dya * pscale).astype(BF16)
        dqm_ref[:, 2 * d:3 * d] = dyarb
        sb = jax.nn.sigmoid(p5_ref[:, 4 * d:5 * d].astype(F32))
        dybb = (dm * sb).astype(BF16)
        dqm_ref[:, 0:d] = dybb
        dh = dh + dproj(slice(5 * d, 6 * d), dm * yb_ref[...].astype(F32) * sb * (1.0 - sb))

        for g, window in enumerate(POOL_WINDOWS):
            cols = slice(g * gw, (g + 1) * gw)
            rows = 8 * (window - 1)
            dpg = _dot_nt(dyarb[:, cols], wp_ref[g])
            dpgs = dpg * _inv_count(tix * tm, window)
            halo = _halo_bottom(dpgs[:rows, :], dpgcarry[:rows, cols])
            dpgcarry[:, cols] = dpgs[:pool_rows, :]
            s, shift = jnp.concatenate([dpgs, halo], axis=0), 1
            while shift < window:
                s = s[:s.shape[0] - 8 * shift, :] + s[8 * shift:, :]
                shift *= 2
            dh = dh + dproj(cols, s - dpg)

        dq = _dot_nt(dybb, wb_ref[...])
        ux = p5_ref[:, 0:d].astype(F32)
        uc = p5_ref[:, 2 * d:3 * d].astype(F32)
        dh = dh + dproj(slice(2 * d, 3 * d), dq * cv_ref[...].astype(F32))
        dcv = dq * p5_ref[:, d:2 * d].astype(F32)
        taps = (vec_ref[4:5, :], vec_ref[5:6, :], vec_ref[6:7, :])
        dpv, (t0, t1, t2) = _conv3_bwd(dcv, dcvcarry, slice(0, d), taps, uc * ux)
        red_ref[3:4, :] += _colsum(dcv)
        red_ref[4:5, :] += t0
        red_ref[5:6, :] += t1
        red_ref[6:7, :] += t2
        dh = dh + dproj(slice(d, 2 * d), dpv * uc)
        dh = dh + dproj(slice(3 * d, 4 * d), dpv * ux)

        _, r, nh = _prenorm(_interleave(x_ref[...]), vp_ref)
        gx_ref[...] = _deinterleave(dx1_v + _prenorm_bwd(dh, r, nh, vp_ref, pred_ref))

    rev = lambda n: pl.BlockSpec((tm, n), lambda i: (nt - 1 - i, 0))
    return _call(
        body, "mixer_block_bwd", (nt,), [rev(d)] * 5 + [rev(5 * d), rev(d)] + [_vmem()] * 6,
        [rev(3 * d), rev(6 * d), rev(d), pl.BlockSpec((16, d), lambda i: (0, 0)),
         pl.BlockSpec((8, d), lambda i: (0, 0))],
        [_sds((t, 3 * d), BF16), _sds((t, 6 * d), BF16), _sds((t, d), F32), _sds((16, d), F32), _sds((8, d), F32)],
        [pltpu.VMEM((pool_rows, d), F32), pltpu.VMEM((16, d), F32)],
        _params("arbitrary"), (dx1, ob, yarb, ybb, cvb, p5b, x, vec_pre, vec, w_in, w_pool, w_bout, w_o), job)


def _matmul_tn(a, b, bm, bn, tk, by_col_block, name, job=None):
    t, m = a.shape
    n = b.shape[1]
    nk = t // tk
    parts = int(by_col_block)
    piece = bn // max(parts, 1)
    wide = _round_up(piece, LANES)

    def body(a_ref, b_ref, o_ref, acc_ref):
        k = pl.program_id(2)

        @pl.when(k == 0)
        def _():
            acc_ref[...] = jnp.zeros_like(acc_ref)

        acc_ref[...] += _dot_tn(a_ref[...], b_ref[...])

        @pl.when(k == nk - 1)
        def _():
            if parts:
                acc = acc_ref[...]
                for p in range(parts):
                    if wide > piece:
                        o_ref[p] = jnp.zeros((bm, wide), o_ref.dtype)
                    o_ref[p, :, 0:piece] = acc[:, p * piece:(p + 1) * piece].astype(o_ref.dtype)
            else:
                o_ref[...] = acc_ref[...].astype(o_ref.dtype)

    if by_col_block:
        out_shape = _sds((parts * n // bn, m, wide), BF16)
        out_spec = pl.BlockSpec((parts, bm, wide), lambda i, j, k: (j, i, 0))
    else:
        out_shape = _sds((m, n), BF16)
        out_spec = pl.BlockSpec((bm, bn), lambda i, j, k: (i, j))
    out = _call(body, name, (m // bm, n // bn, nk),
                [pl.BlockSpec((tk, bm), lambda i, j, k: (k, i)), pl.BlockSpec((tk, bn), lambda i, j, k: (k, j))],
                [out_spec], [out_shape], [pltpu.VMEM((bm, bn), F32)],
                _params("arbitrary", "arbitrary", "arbitrary"), (a, b), job)
    return out if job is not None else out[0]


def _side_by_side(blocks, name):
    n, r, c = blocks.shape
    rb = _row_block(r) // 2

    def body(in_ref, o_ref):
        for j in range(n):
            o_ref[:, j * c:(j + 1) * c] = in_ref[j]

    return pl.pallas_call(
        body, name=name, grid=(r // rb,), in_specs=[pl.BlockSpec((n, rb, c), lambda i: (0, i, 0))],
        out_specs=pl.BlockSpec((rb, n * c), lambda i: (i, 0)), out_shape=_sds((r, n * c), blocks.dtype),
        compiler_params=_params("parallel"))(blocks)


def _matmul_tn_groups(a, b, groups, tk, name, job=None):
    t, m = a.shape
    w = m // groups
    nk = t // tk

    def body(a_ref, b_ref, o_ref, acc_ref):
        k = pl.program_id(1)

        @pl.when(k == 0)
        def _():
            acc_ref[...] = jnp.zeros_like(acc_ref)

        acc_ref[...] += _dot_tn(a_ref[...], b_ref[...])

        @pl.when(k == nk - 1)
        def _():
            o_ref[...] = acc_ref[...].astype(o_ref.dtype)

    blk = pl.BlockSpec((tk, w), lambda g, k: (k, g))
    out = _call(body, name, (groups, nk), [blk, blk], [pl.BlockSpec((None, w, w), lambda g, k: (g, 0, 0))],
                [_sds((groups, w, w), BF16)], [pltpu.VMEM((w, w), F32)], _params("arbitrary", "arbitrary"), (a, b), job)
    return out if job is not None else out[0]


def _round_up(n, k):
    return (n + k - 1) // k * k


def _rows8(rows, width):
    n = _round_up(len(rows), 8)
    rows = list(rows) + [jnp.zeros((1, width), F32)] * (n - len(rows))
    return jnp.concatenate(rows, axis=0)


def kernel(x, c, g_pre_mix, g_post_mix, g_pre_ffn, g_post_ffn, w_ada, b_ada, w_in, w_pool, pool_scale, conv_w, conv_b, w_bout, w_o, w_up, ffn_conv_w, ffn_conv_b, w_down, loss_target, m_g_pre_mix, m_g_post_mix, m_g_pre_ffn, m_g_post_ffn, m_w_ada, m_b_ada, m_w_in, m_w_pool, m_pool_scale, m_conv_w, m_conv_b, m_w_bout, m_w_o, m_w_up, m_ffn_conv_w, m_ffn_conv_b, m_w_down, v_g_pre_mix, v_g_post_mix, v_g_pre_ffn, v_g_post_ffn, v_w_ada, v_b_ada, v_w_in, v_w_pool, v_pool_scale, v_conv_w, v_conv_b, v_w_bout, v_w_o, v_w_up, v_ffn_conv_w, v_ffn_conv_b, v_w_down):
    t, d = x.shape[1], x.shape[2]
    ngroups = len(POOL_WINDOWS)
    gw = d // ngroups
    ada_n = w_ada.shape[2]
    in_n = w_in.shape[2]
    up_n = w_up.shape[2]
    fp = NDEV * w_down.shape[1]

    xi, yi, ci = _position()
    me = _linear(xi, yi, ci)
    chip = 2 * xi + yi
    core = jnp.reshape(ci, (1,)).astype(jnp.int32)
    sel = jnp.stack([2 * chip + ci, chip]).astype(jnp.int32)

    x2 = x.reshape(t, d)
    target = loss_target.reshape(t, d)

    cw_n = conv_w.shape[2]
    pack = jnp.concatenate([c.reshape(1, d), conv_w[0].reshape(1, 3 * cw_n), ffn_conv_w[0].reshape(1, 3 * up_n)], axis=1)
    pack = jnp.pad(pack, ((0, 0), (0, _round_up(pack.shape[1], LANES) - pack.shape[1])))
    b_piece = lax.dynamic_slice_in_dim(b_ada, me * ada_n, ada_n, axis=1)
    mixer_weights = _allgather_job(
        [w_in[0].astype(BF16), w_bout[0].astype(BF16), w_o[0].astype(BF16), w_pool[0].astype(BF16)],
        ["cols", "rows", "rows", "mid"], 0.5, 0.75)
    gathered, mod_rows, w_in_f, g_bout, g_o, w_pool_f = _gather_weights_and_modulation(
        mixer_weights, pack, w_ada[0], b_piece, d)
    w_bout_f = g_bout.reshape(d, d)
    w_o_f = g_o.reshape(d, d)
    c_all = gathered[:, :d]
    c16 = jnp.pad(c_all, ((0, 8), (0, 0))).astype(BF16)
    conv_w_full = gathered[:, d:d + 3 * cw_n].reshape(NDEV, 3, cw_n).transpose(1, 0, 2).reshape(3, NDEV * cw_n)
    fcw_full = gathered[:, d + 3 * cw_n:d + 3 * cw_n + 3 * up_n].reshape(NDEV, 3, up_n)
    fcw_full = fcw_full.transpose(1, 0, 2).reshape(3, 2 * fp)
    fcv = jnp.concatenate([fcw_full, ffn_conv_b, jnp.zeros((4, 2 * fp), F32)], axis=0)
    mod = mod_rows.reshape(1, NDEV * ada_n)
    sh1, sc1, gt1, sh2, sc2, gt2 = [mod[:, k * d:(k + 1) * d] for k in range(6)]

    ffn_weights = _allgather_job([w_up[0].astype(BF16), w_down[0].astype(BF16)], ["rows", "rows"], 0.5, 0.8)
    vec_pre_mix = _rows8([g_pre_mix, 1.0 + sc1, sh1], d)
    vec_mix = _rows8([gt1, g_post_mix, pool_scale, conv_b, conv_w_full[0:1], conv_w_full[1:2], conv_w_full[2:3]], d)
    h1b, p5b, qmb, cvb, yarb, ybb, ob, x1, g_up, g_down = _mixer_block_fwd(
        x2, vec_pre_mix, vec_mix, w_in_f, w_pool_f, w_bout_f, w_o_f, ffn_weights)
    w_up_f = _side_by_side(g_up, "w_up_side_by_side")
    w_down_f = g_down.reshape(fp, d)
    vec_pre_ffn = _rows8([g_pre_ffn, 1.0 + sc2, sh2], d)
    vec_ffn = _rows8([gt2, g_post_ffn], d)
    h2b, upb, upreb, ab, ffb, dy, loss_part = _ffn_block_fwd(x1, target, vec_pre_ffn, vec_ffn, fcv, w_up_f, w_down_f)

    tk, tk_wide = min(4096, t), min(2048, t)
    chip_sum = lambda gs, ss, name: _chip_sums(gs, ss, core, name)
    dffb, dupre, dx1, red_ffn, red_fconv, red_pre_ffn = _ffn_block_bwd(
        dy, ffb, x1, upb, upreb, vec_pre_ffn, vec_ffn, fcv, w_up_f, w_down_f)
    chunk = fp // _column_chunks(fp)
    gr_up = _matmul_tn(h2b, dupre, d, chunk, tk_wide, chunk // up_n, "wgrad_up")
    gr_down, sib_up = _matmul_tn(ab, dffb, chunk, d, tk_wide, False, "wgrad_down", _sibling_job([gr_up]))
    gr_down = gr_down.reshape(NDEV, fp // NDEV, d)
    sib_ffn = [sib_up] + list(_run_job(_sibling_job([gr_down]), "rs_sibling_down"))
    dqmb, dproj, grad_x, red_mix, red_pre_mix = _mixer_block_bwd(
        dx1, ob, yarb, ybb, cvb, p5b, x2, vec_pre_mix, vec_mix, w_in_f, w_pool_f, w_bout_f, w_o_f, None)
    gr_in, fc_up, fc_down = _matmul_tn(h1b, dproj, d, in_n, tk, True, "wgrad_in",
                                       _chips_job(chip_sum([gr_up, gr_down], sib_ffn, "rs_chip_sum_ffn")))
    sib_in = _run_job(_sibling_job([gr_in]), "rs_sibling_in")
    gr_qmp, fc_in = _matmul_tn_groups(qmb, dqmb, 3, tk, "wgrad_bout_o_pool",
                                      _chips_job(chip_sum([gr_in], sib_in, "rs_chip_sum_in")))
    gr_bout = gr_qmp[0].reshape(NDEV, d // NDEV, d)
    gr_o = gr_qmp[1].reshape(NDEV, d // NDEV, d)
    gr_pool = jnp.stack([gr_qmp[2, g * gw:(g + 1) * gw, g * gw:(g + 1) * gw] for g in range(ngroups)])
    gr_pool = gr_pool.reshape(ngroups, NDEV, gw // NDEV, gw).transpose(1, 0, 2, 3).reshape(NDEV, -1, gw)
    rest = [gr_bout, gr_o, gr_pool]
    sib_rest = _run_job(_sibling_job(rest), "rs_sibling_rest")

    dmod = [red_pre_mix[0:1], red_pre_mix[1:2], red_mix[1:2], red_pre_ffn[0:1], red_pre_ffn[1:2], red_ffn[1:2]]
    small = [red_pre_mix[2:3], red_mix[0:1], red_pre_ffn[2:3], red_ffn[0:1], red_mix[2:3], red_mix[3:4],
             red_mix[4:5], red_mix[5:6], red_mix[6:7]] + dmod
    flat = jnp.concatenate(small + [red_fconv[0:4].reshape(1, 8 * fp), loss_part[0:1, 0:1]], axis=1)
    flat_n = flat.shape[1]
    width = 8 * LANES
    rows = _round_up(-(-flat_n // width), 8)
    flat = jnp.pad(flat, ((0, 0), (0, rows * width - flat_n))).reshape(rows, width)
    gat, tot, *fc_rest = _small_allreduce(flat, "allreduce_small_rs_rest",
                                          _chips_job(chip_sum(rest, sib_rest, "rs_chip_sum_rest")))

    def big(grad, from_sibling, from_chips, w, m, v, name, transposed=False):
        shape = w.shape
        flat = (lambda a: a[0].T) if transposed else (lambda a: a.reshape((-1, shape[-1])))
        outs = _reduce_adamw(grad, from_sibling, from_chips, sel, flat(w), flat(m), flat(v), name, transposed)
        return [(a.T if transposed else a).reshape(shape) for a in outs]

    g_w_up, d_w_up, nm_w_up, nv_w_up = big(gr_up, sib_ffn[0], fc_up, w_up, m_w_up, v_w_up, "adamw_up",
                                           transposed=up_n % LANES != 0)
    g_w_down, d_w_down, nm_w_down, nv_w_down = big(gr_down, sib_ffn[1], fc_down, w_down, m_w_down, v_w_down, "adamw_down")
    g_w_in, d_w_in, nm_w_in, nv_w_in = big(gr_in, sib_in[0], fc_in, w_in, m_w_in, v_w_in, "adamw_in")
    flat2 = lambda a: a.reshape((-1, a.shape[-1]))
    rest_w = [(w_bout, m_w_bout, v_w_bout), (w_o, m_w_o, v_w_o), (w_pool, m_w_pool, v_w_pool)]
    rest_out = _reduce_adamw_group(
        [(g, s, f, flat2(w), flat2(m), flat2(v)) for g, s, f, (w, m, v) in zip(rest, sib_rest, fc_rest, rest_w)],
        sel, "adamw_bout_o_pool")
    (g_w_bout, d_w_bout, nm_w_bout, nv_w_bout), (g_w_o, d_w_o, nm_w_o, nv_w_o), (g_w_pool, d_w_pool, nm_w_pool, nv_w_pool) = [
        [a.reshape(w.shape) for a in outs] for outs, (w, _, _) in zip(rest_out, rest_w)]

    tot = tot.reshape(1, rows * width)
    gat = gat.reshape(NDEV, rows * width)
    take = lambda k: tot[:, k * d:(k + 1) * d]
    g_g_pre_mix, g_g_post_mix, g_g_pre_ffn, g_g_post_ffn, g_pool_scale, g_conv_b = [take(k) for k in range(6)]
    g_conv_w_full = jnp.concatenate([take(6), take(7), take(8)], axis=0)
    g_conv_w = lax.dynamic_slice_in_dim(g_conv_w_full, me * cw_n, cw_n, axis=1)
    g_b_ada = tot[:, 9 * d:15 * d]
    dmod_all = gat[:, 9 * d:15 * d]
    fconv_tot = tot[:, 15 * d:15 * d + 8 * fp].reshape(4, 2 * fp)
    loss = 0.5 * tot[0, 15 * d + 8 * fp]
    g_ffn_conv_b = fconv_tot[3:4]
    g_ffn_conv_w = lax.dynamic_slice_in_dim(fconv_tot[0:3], me * up_n, up_n, axis=1)
    dmod_piece = lax.dynamic_slice_in_dim(dmod_all, me * ada_n, ada_n, axis=1)
    g_w_ada = _wada_grad(c16, jnp.pad(dmod_piece, ((0, 8), (0, 0))).astype(BF16))

    names_small = [(g_pre_mix, g_g_pre_mix, m_g_pre_mix, v_g_pre_mix), (g_post_mix, g_g_post_mix, m_g_post_mix, v_g_post_mix),
                   (g_pre_ffn, g_g_pre_ffn, m_g_pre_ffn, v_g_pre_ffn), (g_post_ffn, g_g_post_ffn, m_g_post_ffn, v_g_post_ffn),
                   (b_ada, g_b_ada, m_b_ada, v_b_ada), (pool_scale, g_pool_scale, m_pool_scale, v_pool_scale),
                   (conv_w, g_conv_w, m_conv_w, v_conv_w), (conv_b, g_conv_b, m_conv_b, v_conv_b),
                   (ffn_conv_w, g_ffn_conv_w, m_ffn_conv_w, v_ffn_conv_w), (ffn_conv_b, g_ffn_conv_b, m_ffn_conv_b, v_ffn_conv_b)]
    sizes = [w.size for w, _, _, _ in names_small]
    total = sum(sizes)
    prow = _round_up(-(-total // width), 8)

    def pack_small(k):
        a = jnp.concatenate([q[k].reshape(1, -1) for q in names_small], axis=1)
        return jnp.pad(a, ((0, 0), (0, prow * width - total)), constant_values=1.0).reshape(prow, width)

    ds, ms, vs = _adamw(pack_small(0), pack_small(1), pack_small(2), pack_small(3), "adamw_small")

    def unpack_small(a):
        a = a.reshape(-1)
        out, off = [], 0
        for (w, _, _, _), n in zip(names_small, sizes):
            out.append(a[off:off + n].reshape(w.shape))
            off += n
        return out

    (d_g_pre_mix, d_g_post_mix, d_g_pre_ffn, d_g_post_ffn, d_b_ada, d_pool_scale, d_conv_w, d_conv_b,
     d_ffn_conv_w, d_ffn_conv_b) = unpack_small(ds)
    (nm_g_pre_mix, nm_g_post_mix, nm_g_pre_ffn, nm_g_post_ffn, nm_b_ada, nm_pool_scale, nm_conv_w, nm_conv_b,
     nm_ffn_conv_w, nm_ffn_conv_b) = unpack_small(ms)
    (nv_g_pre_mix, nv_g_post_mix, nv_g_pre_ffn, nv_g_post_ffn, nv_b_ada, nv_pool_scale, nv_conv_w, nv_conv_b,
     nv_ffn_conv_w, nv_ffn_conv_b) = unpack_small(vs)
    d_w_ada, nm_w_ada, nv_w_ada = [a.reshape(w_ada.shape) for a in
                                   _adamw(w_ada[0], g_w_ada, m_w_ada[0], v_w_ada[0], "adamw_ada")]

    grads = [g_g_pre_mix, g_g_post_mix, g_g_pre_ffn, g_g_post_ffn, g_w_ada.reshape(w_ada.shape), g_b_ada, g_w_in,
             g_w_pool, g_pool_scale, g_conv_w.reshape(conv_w.shape), g_conv_b, g_w_bout, g_w_o, g_w_up,
             g_ffn_conv_w.reshape(ffn_conv_w.shape), g_ffn_conv_b, g_w_down]
    deltas = [d_g_pre_mix, d_g_post_mix, d_g_pre_ffn, d_g_post_ffn, d_w_ada, d_b_ada, d_w_in, d_w_pool, d_pool_scale,
              d_conv_w, d_conv_b, d_w_bout, d_w_o, d_w_up, d_ffn_conv_w, d_ffn_conv_b, d_w_down]
    new_m = [nm_g_pre_mix, nm_g_post_mix, nm_g_pre_ffn, nm_g_post_ffn, nm_w_ada, nm_b_ada, nm_w_in, nm_w_pool,
             nm_pool_scale, nm_conv_w, nm_conv_b, nm_w_bout, nm_w_o, nm_w_up, nm_ffn_conv_w, nm_ffn_conv_b, nm_w_down]
    new_v = [nv_g_pre_mix, nv_g_post_mix, nv_g_pre_ffn, nv_g_post_ffn, nv_w_ada, nv_b_ada, nv_w_in, nv_w_pool,
             nv_pool_scale, nv_conv_w, nv_conv_b, nv_w_bout, nv_w_o, nv_w_up, nv_ffn_conv_w, nv_ffn_conv_b, nv_w_down]
    return (loss, grad_x.reshape(x.shape), *grads, *deltas, *new_m, *new_v)
```

```python
import math

import jax
import jax.numpy as jnp
from jax import lax
from jax.experimental import pallas as pl
from jax.experimental.pallas import tpu as pltpu

F32 = jnp.float32
BF16 = jnp.bfloat16
MESH = pl.DeviceIdType.MESH

NDEV = 8
NCHIP = 4
EPS = 1e-6
POOL_WINDOWS = (2, 4, 8, 16)
LANES = 128
ADAM_LR = 0.001
ADAM_B1 = 0.9
ADAM_B2 = 0.999
ADAM_EPS = 1e-08
ADAM_WD = 0.01
ADAM_STEP = 10
GELU_C0 = math.sqrt(2.0 / math.pi)
GELU_C1 = 0.044715
VMEM_LIMIT = 56 * 2**20


def _vmem():
    return pl.BlockSpec(memory_space=pltpu.VMEM)


def _any():
    return pl.BlockSpec(memory_space=pl.ANY)


def _params(*sem):
    return pltpu.CompilerParams(dimension_semantics=sem, vmem_limit_bytes=VMEM_LIMIT)


def _sds(shape, dtype):
    return jax.ShapeDtypeStruct(tuple(shape), dtype)


def _position():
    return lax.axis_index("x"), lax.axis_index("y"), lax.axis_index("c")


def _linear(x, y, c):
    return 4 * x + 2 * y + c


def _dot(a, b):
    return jnp.dot(a, b, preferred_element_type=F32)


def _dot_nt(a, b):
    return lax.dot_general(a, b, (((1,), (1,)), ((), ())), preferred_element_type=F32)


def _dot_tn(a, b):
    return lax.dot_general(a, b, (((0,), (0,)), ((), ())), preferred_element_type=F32)


def _colsum(v):
    return jnp.sum(v, axis=0, keepdims=True)


def _rowmean(v):
    return jnp.mean(v, axis=-1, keepdims=True)


GROUP = 256


def _interleave(v):
    g, n = v.shape
    return jnp.swapaxes(v.reshape(8, g // 8, n), 0, 1).reshape(g, n)


def _deinterleave(v):
    g, n = v.shape
    return jnp.swapaxes(v.reshape(g // 8, 8, n), 0, 1).reshape(g, n)


def _halo_top(cur_last, prev_last):
    rows, n = cur_last.shape
    c3 = cur_last.reshape(rows // 8, 8, n)
    p3 = prev_last.reshape(rows // 8, 8, n)
    sub = lax.broadcasted_iota(jnp.int32, c3.shape, 1)
    return jnp.where(sub == 0, pltpu.roll(p3, 1, 1), pltpu.roll(c3, 1, 1)).reshape(rows, n)


def _halo_bottom(cur_first, next_first):
    rows, n = cur_first.shape
    c3 = cur_first.reshape(rows // 8, 8, n)
    n3 = next_first.reshape(rows // 8, 8, n)
    sub = lax.broadcasted_iota(jnp.int32, c3.shape, 1)
    return jnp.where(sub == 7, pltpu.roll(n3, 7, 1), pltpu.roll(c3, 7, 1)).reshape(rows, n)


def _shift_down(v, halo, k):
    rows = v.shape[0]
    return jnp.concatenate([halo[halo.shape[0] - 8 * k:, :], v[:rows - 8 * k, :]], axis=0)


def _shift_up(v, halo, k):
    return jnp.concatenate([v[8 * k:, :], halo[:8 * k, :]], axis=0)


def _inv_count(first_token, window):
    row = lax.broadcasted_iota(jnp.int32, (GROUP, 1), 0)
    t = first_token + (row % 8) * (GROUP // 8) + row // 8
    return 1.0 / jnp.minimum(t + 1, window).astype(F32)


class _Job:
    def __init__(self, inputs, out_shape, scratch, phases):
        self.inputs, self.out_shape, self.scratch, self.phases = list(inputs), list(out_shape), list(scratch), phases


def _call(body, name, grid, in_specs, out_specs, out_shape, scratch_shapes, params, operands, job=None):
    if job is None:
        return pl.pallas_call(body, name=name, grid=grid, in_specs=in_specs, out_specs=out_specs, out_shape=out_shape,
                              scratch_shapes=scratch_shapes, compiler_params=params)(*operands)
    n_in, n_out, n_scr = len(in_specs), len(out_specs), len(scratch_shapes)
    j_in, j_out = len(job.inputs), len(job.out_shape)
    steps = math.prod(grid)

    def hosted(*refs):
        own_in, refs = refs[:n_in], refs[n_in:]
        jin, refs = refs[:j_in], refs[j_in:]
        own_out, refs = refs[:n_out], refs[n_out:]
        jout, refs = refs[:j_out], refs[j_out:]
        own_scr, jscr = refs[:n_scr], refs[n_scr:]
        step = pl.program_id(0)
        for axis in range(1, len(grid)):
            step = step * grid[axis] + pl.program_id(axis)
        for frac, fn in job.phases[:-1]:
            pl.when(step == int(frac * (steps - 1)))(lambda fn=fn: fn(jin, jout, jscr))
        body(*own_in, *own_out, *own_scr)
        pl.when(step == steps - 1)(lambda: job.phases[-1][1](jin, jout, jscr))

    return pl.pallas_call(
        hosted, name=name, grid=grid, in_specs=list(in_specs) + [_any()] * j_in,
        out_specs=list(out_specs) + [_any()] * j_out, out_shape=list(out_shape) + job.out_shape,
        scratch_shapes=list(scratch_shapes) + job.scratch, compiler_params=params)(*operands, *job.inputs)


def _run_job(job, name):
    n_in, n_out = len(job.inputs), len(job.out_shape)

    def body(*refs):
        for _, fn in job.phases:
            fn(refs[:n_in], refs[n_in:n_in + n_out], refs[n_in + n_out:])

    return pl.pallas_call(body, name=name, out_shape=job.out_shape, in_specs=[_any()] * n_in,
                          out_specs=[_any()] * n_out, scratch_shapes=job.scratch)(*job.inputs)


def _peers(x, y, c):
    out = []
    for k in range(1, NDEV):
        out.append(((1 - x) if k & 4 else x, (1 - y) if k & 2 else y, (1 - c) if k & 1 else c))
    return out


def _small_allreduce(v, name, job):
    r, n = v.shape
    j_in, j_out = len(job.inputs), len(job.out_shape)

    def body(v_ref, *rest):
        jin, rest = rest[:j_in], rest[j_in:]
        gat_ref, sum_ref = rest[:2]
        jout, rest = rest[2:2 + j_out], rest[2 + j_out:]
        send_sems, recv_sems, local_sem = rest[:3]
        jscr = rest[3:]
        job.phases[0][1](jin, jout, jscr)
        x, y, c = _position()
        me = _linear(x, y, c)
        mine = pltpu.make_async_copy(v_ref, gat_ref.at[me], local_sem)
        mine.start()
        peers = _peers(x, y, c)
        sends = []
        for k, peer in enumerate(peers):
            cp = pltpu.make_async_remote_copy(src_ref=v_ref, dst_ref=gat_ref.at[me], send_sem=send_sems.at[k],
                                              recv_sem=recv_sems.at[k], device_id=peer, device_id_type=MESH)
            cp.start()
            sends.append(cp)
        for k, peer in enumerate(peers):
            pltpu.make_async_remote_copy(src_ref=v_ref, dst_ref=gat_ref.at[_linear(*peer)], send_sem=send_sems.at[k],
                                         recv_sem=recv_sems.at[k], device_id=peer, device_id_type=MESH).wait_recv()
        for cp in sends:
            cp.wait_send()
        mine.wait()
        acc = gat_ref[0]
        for j in range(1, NDEV):
            acc = acc + gat_ref[j]
        sum_ref[...] = acc
        job.phases[-1][1](jin, jout, jscr)

    return pl.pallas_call(
        body, name=name, out_shape=[_sds((NDEV, r, n), F32), _sds((r, n), F32)] + job.out_shape,
        in_specs=[_vmem()] + [_any()] * j_in, out_specs=[_vmem()] * 2 + [_any()] * j_out,
        scratch_shapes=[pltpu.SemaphoreType.DMA((NDEV - 1,)), pltpu.SemaphoreType.DMA((NDEV - 1,)),
                        pltpu.SemaphoreType.DMA(())] + job.scratch,
    )(v, *job.inputs)


def _exchange_rows(src_for, dst_ref, sems):
    send_sems, recv_sems, local_sem = sems
    x, y, c = _position()
    me = _linear(x, y, c)
    row = lambda j: dst_ref.at[pl.ds(j, 1), :]
    mine = pltpu.make_async_copy(src_for(me), row(me), local_sem)
    mine.start()
    peers = _peers(x, y, c)
    sends = []
    for k, peer in enumerate(peers):
        cp = pltpu.make_async_remote_copy(src_ref=src_for(_linear(*peer)), dst_ref=row(me), send_sem=send_sems.at[k],
                                          recv_sem=recv_sems.at[k], device_id=peer, device_id_type=MESH)
        cp.start()
        sends.append(cp)
    for k, peer in enumerate(peers):
        pltpu.make_async_remote_copy(src_ref=src_for(me), dst_ref=row(_linear(*peer)), send_sem=send_sems.at[k],
                                     recv_sem=recv_sems.at[k], device_id=peer, device_id_type=MESH).wait_recv()
    for cp in sends:
        cp.wait_send()
    mine.wait()


def _gather_weights_and_modulation(job, pack, w_ada, b_piece, learned, fcb, d, cw_n, up_n):
    n = pack.shape[1]
    m = w_ada.shape[1]
    j_in, j_out = len(job.inputs), len(job.out_shape)
    row_sems = [pltpu.SemaphoreType.DMA((NDEV - 1,)), pltpu.SemaphoreType.DMA((NDEV - 1,)), pltpu.SemaphoreType.DMA(())]

    def body(pack_ref, wada_ref, bp_ref, lrn_ref, fcb_ref, *rest):
        jin, rest = rest[:j_in], rest[j_in:]
        gat_ref, vpm_ref, vmix_ref, vpf_ref, vffn_ref, fcv_ref = rest[:6]
        jout, rest = rest[6:6 + j_out], rest[6 + j_out:]
        sems1, sems2, piece, mod_ref, modm, jscr = rest[0:3], rest[3:6], rest[6], rest[7], rest[8], rest[9:]
        phases = [fn for _, fn in job.phases]
        phases[0](jin, jout, jscr)
        _exchange_rows(lambda j: pack_ref, gat_ref, sems1)
        c16 = jnp.concatenate([gat_ref[:, 0:d], jnp.zeros((NDEV, d), F32)], axis=0).astype(BF16)
        piece[...] = (_dot(c16, wada_ref[...].astype(BF16)) + bp_ref[...])[0:NDEV, :]
        _exchange_rows(lambda j: piece.at[pl.ds(j, 1), :], mod_ref, sems2)

        for j in range(NDEV):
            done = 0
            while done < m:
                row, col = divmod(j * m + done, d)
                width = min(m - done, d - col)
                modm[row:row + 1, col:col + width] = mod_ref[j:j + 1, done:done + width]
                done += width
        for ref in (vpm_ref, vmix_ref, vpf_ref, vffn_ref, fcv_ref):
            ref[...] = jnp.zeros_like(ref)
        for ref, g_row, k in ((vpm_ref, 0, 0), (vpf_ref, 2, 3)):
            ref[0:1, :] = lrn_ref[g_row:g_row + 1, :]
            ref[1:2, :] = 1.0 + modm[k + 1:k + 2, :]
            ref[2:3, :] = modm[k:k + 1, :]
        vmix_ref[0:1, :] = modm[2:3, :]
        vmix_ref[1:2, :] = lrn_ref[1:2, :]
        vmix_ref[2:4, :] = lrn_ref[4:6, :]
        vffn_ref[0:1, :] = modm[5:6, :]
        vffn_ref[1:2, :] = lrn_ref[3:4, :]
        fcv_ref[3:4, :] = fcb_ref[...]
        for j in range(NDEV):
            for k in range(3):
                vmix_ref[4 + k:5 + k, j * cw_n:(j + 1) * cw_n] = gat_ref[j:j + 1, d + k * cw_n:d + (k + 1) * cw_n]
                off = d + 3 * cw_n + k * up_n
                fcv_ref[k:k + 1, j * up_n:(j + 1) * up_n] = gat_ref[j:j + 1, off:off + up_n]

        for fn in phases[1:]:
            fn(jin, jout, jscr)

    tables = [_sds((8, d), F32), _sds((16, d), F32), _sds((8, d), F32), _sds((8, d), F32), _sds((8, NDEV * up_n), F32)]
    return pl.pallas_call(
        body, name="gather_weights_and_modulation",
        out_shape=[_sds((NDEV, n), F32)] + tables + job.out_shape,
        in_specs=[_vmem()] * 5 + [_any()] * j_in, out_specs=[_vmem()] * 6 + [_any()] * j_out,
        scratch_shapes=row_sems + row_sems + [pltpu.VMEM((NDEV, m), F32), pltpu.VMEM((NDEV, m), F32),
                                              pltpu.VMEM((8, d), F32)] + job.scratch,
    )(pack, w_ada, b_piece, learned, fcb, *job.inputs)


def _gathered(shard, layout):
    if layout == "rows":
        return (NDEV,) + shard.shape, lambda ref, j: ref.at[j]
    if layout == "cols":
        r, c = shard.shape
        return (r, NDEV * c), lambda ref, j: ref.at[:, pl.ds(pl.multiple_of(j * c, LANES), c)]
    g, r, c = shard.shape
    return (g, NDEV * r, c), lambda ref, j: ref.at[:, pl.ds(pl.multiple_of(j * r, 16), r), :]


def _allgather_job(shards, layouts, relay_at, forward_at):
    n = len(shards)
    specs = [_gathered(s, l) for s, l in zip(shards, layouts)]
    halves = [s.shape[0] // 2 for s in shards]

    def plan(src, dst, sems):
        send_sems, recv_sems, _ = sems
        x, y, c = _position()
        me, sibling, xn, yn, dg = (x, y, c), (x, y, 1 - c), (1 - x, y, c), (x, 1 - y, c), (1 - x, 1 - y, c)

        def copy(a, k, block, to, half=None, from_src=False):
            blk = specs[a][1](dst[a], _linear(*block))
            if half is not None:
                blk = blk.at[pl.ds(half * halves[a], halves[a])]
            return pltpu.make_async_remote_copy(src_ref=src[a] if from_src else blk, dst_ref=blk,
                                                send_sem=send_sems.at[a, k], recv_sem=recv_sems.at[a, k],
                                                device_id=to, device_id_type=MESH)
        return copy, me, sibling, xn, yn, dg

    def local(src, dst, sems):
        x, y, c = _position()
        return [pltpu.make_async_copy(src[a], specs[a][1](dst[a], _linear(x, y, c)), sems[2].at[a]) for a in range(n)]

    def own(src, dst, sems):
        copy, me, sibling, xn, yn, dg = plan(src, dst, sems)
        return [copy(a, k, me, to, from_src=True) for k, to in ((1, xn), (2, yn), (0, sibling)) for a in range(n)]

    def relayed(src, dst, sems):
        copy, me, sibling, xn, yn, dg = plan(src, dst, sems)
        return ([copy(a, 3, xn, yn, half=0) for a in range(n)] + [copy(a, 5, xn, sibling) for a in range(n)],
                [copy(a, 4, yn, xn, half=1) for a in range(n)] + [copy(a, 6, yn, sibling) for a in range(n)])

    def diagonal(src, dst, sems):
        copy, me, sibling, xn, yn, dg = plan(src, dst, sems)
        return [copy(a, 7, dg, sibling) for a in range(n)]

    def start(src, dst, sems):
        for cp in local(src, dst, sems) + own(src, dst, sems):
            cp.start()

    def relay(src, dst, sems):
        copy, me, sibling, xn, yn, dg = plan(src, dst, sems)
        from_x, from_y = relayed(src, dst, sems)
        for a in range(n):
            copy(a, 1, xn, me).wait_recv()
        for cp in from_x:
            cp.start()
        for a in range(n):
            copy(a, 2, yn, me).wait_recv()
        for cp in from_y:
            cp.start()

    def forward(src, dst, sems):
        copy, me, sibling, xn, yn, dg = plan(src, dst, sems)
        for a in range(n):
            copy(a, 3, dg, me, half=0).wait_recv()
            copy(a, 4, dg, me, half=1).wait_recv()
        for cp in diagonal(src, dst, sems):
            cp.start()

    def finish(src, dst, sems):
        copy, me, sibling, xn, yn, dg = plan(src, dst, sems)
        other = lambda dev: (dev[0], dev[1], 1 - dev[2])
        for a in range(n):
            copy(a, 0, sibling, me).wait_recv()
            for k, dev in ((5, xn), (6, yn), (7, dg)):
                copy(a, k, other(dev), me).wait_recv()
        from_x, from_y = relayed(src, dst, sems)
        for cp in own(src, dst, sems) + from_x + from_y + diagonal(src, dst, sems):
            cp.wait_send()
        for cp in local(src, dst, sems):
            cp.wait()

    return _Job(shards, [_sds(spec[0], s.dtype) for spec, s in zip(specs, shards)],
                [pltpu.SemaphoreType.DMA((n, 8)), pltpu.SemaphoreType.DMA((n, 8)), pltpu.SemaphoreType.DMA((n,))],
                [(0.0, start), (relay_at, relay), (forward_at, forward), (1.0, finish)])


def _sibling_job(grads):
    n = len(grads)

    def copies(src, dst, sems):
        x, y, c = _position()
        return [pltpu.make_async_remote_copy(src_ref=src[a].at[2 * q + 1 - c], dst_ref=dst[a].at[q],
                                             send_sem=sems[0].at[a, q], recv_sem=sems[1].at[a, q],
                                             device_id=(x, y, 1 - c), device_id_type=MESH)
                for a in range(n) for q in range(NCHIP)]

    return _exchange_job(grads, NCHIP, copies)


def _chips_job(chip_sums):
    n = len(chip_sums)

    def copies(src, dst, sems):
        x, y, c = _position()
        chips = [(1 - x, y), (x, 1 - y), (1 - x, 1 - y)]
        return [pltpu.make_async_remote_copy(src_ref=src[a].at[2 * chip[0] + chip[1]], dst_ref=dst[a].at[j],
                                             send_sem=sems[0].at[a, j], recv_sem=sems[1].at[a, j],
                                             device_id=(*chip, c), device_id_type=MESH)
                for j, chip in enumerate(chips) for a in range(n)]

    return _exchange_job(chip_sums, 3, copies)


def _exchange_job(arrays, slots, copies):
    n = len(arrays)

    def start(src, dst, sems):
        for cp in copies(src, dst, sems):
            cp.start()

    def finish(src, dst, sems):
        cps = copies(src, dst, sems)
        for cp in cps:
            cp.wait_recv()
        for cp in cps:
            cp.wait_send()

    return _Job(arrays, [_sds((slots,) + a.shape[1:], a.dtype) for a in arrays],
                [pltpu.SemaphoreType.DMA((n, slots)), pltpu.SemaphoreType.DMA((n, slots))],
                [(0.0, start), (1.0, finish)])


def _row_block(r):
    if r <= 512:
        return r
    for rb in range(512, 15, -16):
        if r % rb == 0:
            return rb
    return r


def _chip_sums(grads, from_sibling, core, name):
    n = len(grads)

    def body(core_ref, *refs):
        del core_ref
        for a in range(n):
            refs[2 * n + a][...] = (refs[a][...].astype(F32) + refs[n + a][...].astype(F32)).astype(BF16)

    block = lambda g, index: pl.BlockSpec((None,) + g.shape[1:], index)
    grid_spec = pltpu.PrefetchScalarGridSpec(
        num_scalar_prefetch=1, grid=(NCHIP,),
        in_specs=[block(g, lambda q, core: (2 * q + core[0], 0, 0)) for g in grads]
        + [block(g, lambda q, core: (q, 0, 0)) for g in grads],
        out_specs=[block(g, lambda q, core: (q, 0, 0)) for g in grads])
    return pl.pallas_call(body, name=name, grid_spec=grid_spec,
                          out_shape=[_sds((NCHIP,) + g.shape[1:], BF16) for g in grads],
                          compiler_params=_params("parallel"))(core, *grads, *from_sibling)


def _adamw_math(w, g, m, v):
    m2 = ADAM_B1 * m + (1.0 - ADAM_B1) * g
    v2 = ADAM_B2 * v + (1.0 - ADAM_B2) * jnp.square(g)
    m_hat = m2 / (1.0 - ADAM_B1 ** ADAM_STEP)
    v_hat = v2 / (1.0 - ADAM_B2 ** ADAM_STEP)
    delta = -ADAM_LR * (m_hat / (jnp.sqrt(v_hat) + ADAM_EPS) + ADAM_WD * w)
    return delta, m2, v2


def _adamw(w, g, m, v, name):
    r, c = w.shape
    rb = _row_block(r)

    def body(w_ref, g_ref, m_ref, v_ref, d_ref, m2_ref, v2_ref):
        d, m2, v2 = _adamw_math(w_ref[...], g_ref[...], m_ref[...], v_ref[...])
        d_ref[...] = d
        m2_ref[...] = m2
        v2_ref[...] = v2

    blk = pl.BlockSpec((rb, c), lambda i: (i, 0))
    return pl.pallas_call(body, name=name, grid=(r // rb,), in_specs=[blk] * 4, out_specs=[blk] * 3,
                          out_shape=[_sds((r, c), F32)] * 3, compiler_params=_params("parallel"))(w, g, m, v)


def _adamw_group(items, name):
    n = len(items)

    def body(*refs):
        ins, outs = refs[:4 * n], refs[4 * n:]
        for a in range(n):
            w_ref, g_ref, m_ref, v_ref = ins[4 * a:4 * a + 4]
            for ref, val in zip(outs[3 * a:3 * a + 3], _adamw_math(w_ref[...], g_ref[...], m_ref[...], v_ref[...])):
                ref[...] = val

    outs = pl.pallas_call(body, name=name, in_specs=[_vmem()] * (4 * n), out_specs=[_vmem()] * (3 * n),
                          out_shape=[_sds(q[0].shape, F32) for q in items for _ in range(3)])(
        *[a for q in items for a in q])
    return [outs[3 * a:3 * a + 3] for a in range(n)]


def _reduce_adamw(grad, from_sibling, from_chips, sel, w, m, v, name, transposed=False):
    r, c = w.shape[::-1] if transposed else w.shape
    cp = grad.shape[2]
    rb = _row_block(r)

    def body(sel_ref, g_ref, s_ref, c0_ref, c1_ref, c2_ref, w_ref, m_ref, v_ref, go_ref, d_ref, m2_ref, v2_ref):
        del sel_ref
        g = g_ref[...].astype(F32) + s_ref[...].astype(F32)
        g = g + c0_ref[...].astype(F32)
        g = g + c1_ref[...].astype(F32)
        g = g + c2_ref[...].astype(F32)
        g = g.T[0:c, :] if transposed else g[:, 0:c]
        d, m2, v2 = _adamw_math(w_ref[...], g, m_ref[...], v_ref[...])
        go_ref[...] = g
        d_ref[...] = d
        m2_ref[...] = m2
        v2_ref[...] = v2

    blk = pl.BlockSpec((c, rb), lambda i, sel: (0, i)) if transposed else pl.BlockSpec((rb, c), lambda i, sel: (i, 0))
    grid_spec = pltpu.PrefetchScalarGridSpec(
        num_scalar_prefetch=1, grid=(r // rb,),
        in_specs=[pl.BlockSpec((None, rb, cp), lambda i, sel: (sel[0], i, 0)),
                  pl.BlockSpec((None, rb, cp), lambda i, sel: (sel[1], i, 0)),
                  pl.BlockSpec((None, rb, cp), lambda i, sel: (0, i, 0)),
                  pl.BlockSpec((None, rb, cp), lambda i, sel: (1, i, 0)),
                  pl.BlockSpec((None, rb, cp), lambda i, sel: (2, i, 0)),
                  blk, blk, blk],
        out_specs=[blk] * 4)
    return pl.pallas_call(body, name=name, grid_spec=grid_spec, out_shape=[_sds(w.shape, F32)] * 4,
                          compiler_params=_params("parallel"))(sel, grad, from_sibling, from_chips, from_chips,
                                                               from_chips, w, m, v)


def _reduce_adamw_group(items, sel, name):
    n = len(items)

    def body(sel_ref, *refs):
        del sel_ref
        ins, outs = refs[:8 * n], refs[8 * n:]
        for a in range(n):
            g_ref, s_ref, c0_ref, c1_ref, c2_ref, w_ref, m_ref, v_ref = ins[8 * a:8 * a + 8]
            g = g_ref[...].astype(F32) + s_ref[...].astype(F32)
            g = g + c0_ref[...].astype(F32)
            g = g + c1_ref[...].astype(F32)
            g = g + c2_ref[...].astype(F32)
            d, m2, v2 = _adamw_math(w_ref[...], g, m_ref[...], v_ref[...])
            for ref, val in zip(outs[4 * a:4 * a + 4], (g, d, m2, v2)):
                ref[...] = val

    in_specs, out_specs, out_shape, operands = [], [], [], []
    for grad, from_sibling, from_chips, w, m, v in items:
        slot = lambda index, shape=grad.shape[1:]: pl.BlockSpec((None,) + shape, index)
        full = pl.BlockSpec(w.shape, lambda i, sel: (0, 0))
        in_specs += [slot(lambda i, sel: (sel[0], 0, 0)), slot(lambda i, sel: (sel[1], 0, 0)),
                     slot(lambda i, sel: (0, 0, 0)), slot(lambda i, sel: (1, 0, 0)), slot(lambda i, sel: (2, 0, 0)),
                     full, full, full]
        out_specs += [full] * 4
        out_shape += [_sds(w.shape, F32)] * 4
        operands += [grad, from_sibling, from_chips, from_chips, from_chips, w, m, v]
    grid_spec = pltpu.PrefetchScalarGridSpec(num_scalar_prefetch=1, grid=(1,), in_specs=in_specs, out_specs=out_specs)
    outs = pl.pallas_call(body, name=name, grid_spec=grid_spec, out_shape=out_shape,
                          compiler_params=_params("arbitrary"))(sel, *operands)
    return [outs[4 * a:4 * a + 4] for a in range(n)]


def _wada_grad(c_all, dmod_piece):
    d = c_all.shape[1]
    n = dmod_piece.shape[1]

    def body(c_ref, dm_ref, o_ref):
        o_ref[...] = _dot_tn(c_ref[...], dm_ref[...])

    return pl.pallas_call(body, name="ada_wgrad", out_shape=_sds((d, n), F32),
                          in_specs=[_vmem()] * 2, out_specs=_vmem())(c_all, dmod_piece)


def _column_chunks(width):
    for n in (4, 2):
        if width % (n * LANES) == 0:
            return n
    return 1


def _conv_taps(ref, col):
    return ref[0:1, col], ref[1:2, col], ref[2:3, col]


def _gelu_parts(u):
    th = jnp.tanh(GELU_C0 * (u + GELU_C1 * (u * u * u)))
    cdf = 0.5 * (1.0 + th)
    return cdf, th


def _conv3_bwd(dv, carry, col, taps, x):
    halo = _halo_bottom(dv[:16, :], carry[:, col])
    carry[:, col] = dv[:16, :]
    d1 = _shift_up(dv, halo, 1)
    d2 = _shift_up(dv, halo, 2)
    w0, w1, w2 = taps
    dx = w2 * dv
    dx = dx + w1 * d1
    dx = dx + w0 * d2
    return dx, (_colsum(d2 * x), _colsum(d1 * x), _colsum(dv * x))


def _prenorm(x, vp_ref):
    r = lax.rsqrt(_rowmean(x * x) + EPS)
    nh = x * r
    return (nh * vp_ref[0:1, :]) * vp_ref[1:2, :] + vp_ref[2:3, :], r, nh


def _prenorm_bwd(dh, r, nh, vp_ref, red_ref):
    g, sc1 = vp_ref[0:1, :], vp_ref[1:2, :]
    red_ref[0:1, :] += _colsum(dh)
    red_ref[1:2, :] += _colsum(dh * (nh * g))
    red_ref[2:3, :] += _colsum(dh * nh * sc1)
    dnh = dh * g * sc1
    return r * (dnh - nh * _rowmean(dnh * nh))


def _postnorm_bwd(dres, z, gate, gpost, red_ref):
    r = lax.rsqrt(_rowmean(z * z) + EPS)
    nh = z * r
    dn = dres * gate
    red_ref[0:1, :] += _colsum(dn * nh)
    red_ref[1:2, :] += _colsum(dres * (nh * gpost))
    dnh = dn * gpost
    return r * (dnh - nh * _rowmean(dnh * nh))


def _mixer_block_fwd(x, vec_pre, vec, w_in, w_pool, w_bout, w_o, job):
    t, d = x.shape
    tm = GROUP
    gw = d // len(POOL_WINDOWS)
    pool_rows = 8 * (POOL_WINDOWS[-1] - 1)

    def body(x_ref, vp_ref, vec_ref, win_ref, wp_ref, wb_ref, wo_ref,
             hb_ref, p5_ref, qm_ref, cv_ref, yar_ref, yb_ref, o_ref, x1_ref, mbuf, ucarry, pcarry):
        i = pl.program_id(0)

        @pl.when(i == 0)
        def _():
            ucarry[...] = jnp.zeros_like(ucarry)
            pcarry[...] = jnp.zeros_like(pcarry)

        xp = _interleave(x_ref[...])
        hb = _prenorm(xp, vp_ref)[0].astype(BF16)
        hb_ref[...] = hb
        proj = lambda k: _dot(hb, win_ref[:, k * d:(k + 1) * d])

        za = proj(4)
        p5_ref[:, 3 * d:4 * d] = za.astype(BF16)
        sa = jax.nn.sigmoid(za)
        u_pool = proj(0)
        for g, window in enumerate(POOL_WINDOWS):
            cols = slice(g * gw, (g + 1) * gw)
            rows = 8 * (window - 1)
            u = u_pool[:, cols]
            halo = _halo_top(u[tm - rows:, :], ucarry[pool_rows - rows:, cols])
            s, shift = jnp.concatenate([halo, u], axis=0), 1
            while shift < window:
                s = s[8 * shift:, :] + s[:s.shape[0] - 8 * shift, :]
                shift *= 2
            pgb = (s * _inv_count(i * tm, window) - u).astype(BF16)
            qm_ref[:, 2 * d + g * gw:2 * d + (g + 1) * gw] = pgb
            yar = _dot(pgb, wp_ref[g])
            yar_ref[:, cols] = yar
            mbuf[:, cols] = sa[:, cols] * (yar * vec_ref[2:3, cols])
        ucarry[...] = u_pool[tm - pool_rows:, :]

        ux = proj(1)
        uc = proj(3)
        p5_ref[:, 0:d] = ux.astype(BF16)
        p5_ref[:, 2 * d:3 * d] = uc.astype(BF16)
        p = uc * ux
        halo = _halo_top(p[tm - 16:, :], pcarry[...])
        pcarry[...] = p[tm - 16:, :]
        cv = vec_ref[3:4, :] + vec_ref[4:5, :] * _shift_down(p, halo, 2)
        cv = cv + vec_ref[5:6, :] * _shift_down(p, halo, 1)
        cv = cv + vec_ref[6:7, :] * p
        cv_ref[...] = cv
        ub = proj(2)
        p5_ref[:, d:2 * d] = ub.astype(BF16)
        qb = (ub * cv).astype(BF16)
        qm_ref[:, 0:d] = qb
        yb = _dot(qb, wb_ref[...])
        yb_ref[...] = yb

        zb = proj(5)
        p5_ref[:, 4 * d:5 * d] = zb.astype(BF16)
        mb = (mbuf[...] + jax.nn.sigmoid(zb) * yb).astype(BF16)
        qm_ref[:, d:2 * d] = mb
        o = _dot(mb, wo_ref[...])
        o_ref[...] = o
        r2 = lax.rsqrt(_rowmean(o * o) + EPS)
        x1_ref[...] = xp + vec_ref[0:1, :] * ((o * r2) * vec_ref[1:2, :])

    row = lambda n: pl.BlockSpec((tm, n), lambda i: (i, 0))
    widths = [d, 5 * d, 3 * d, d, d, d, d, d]
    return _call(
        body, "mixer_block_fwd", (t // tm,), [row(d)] + [_vmem()] * 6, [row(n) for n in widths],
        [_sds((t, n), BF16) for n in widths[:3]] + [_sds((t, n), F32) for n in widths[3:]],
        [pltpu.VMEM((tm, d), F32), pltpu.VMEM((pool_rows, d), F32), pltpu.VMEM((16, d), F32)],
        _params("arbitrary"), (x, vec_pre, vec, w_in, w_pool, w_bout, w_o), job)


def _ffn_block_fwd(x1, target, vec_pre, vec, fcv, w_up, w_down):
    t, d = x1.shape
    tm = GROUP
    fp = w_down.shape[0]
    nch = _column_chunks(fp)
    cw = fp // nch

    def body(x1_ref, tg_ref, vp_ref, vec_ref, fcv_ref, wu_ref, wd_ref,
             hb_ref, upb_ref, upreb_ref, a_ref, ffb_ref, dy_ref, loss_ref, carry):
        i = pl.program_id(0)

        @pl.when(i == 0)
        def _():
            carry[...] = jnp.zeros_like(carry)
            loss_ref[...] = jnp.zeros_like(loss_ref)

        x1 = x1_ref[...]
        hb = _prenorm(x1, vp_ref)[0].astype(BF16)
        hb_ref[...] = hb

        cols = [(slice(j * cw, (j + 1) * cw), slice(fp + j * cw, fp + (j + 1) * cw)) for j in range(nch)]
        up_gate = _dot(hb, wu_ref[:, 0:fp])
        up_val = _dot(hb, wu_ref[:, fp:2 * fp])

        def conv(v, col):
            halo = _halo_top(v[tm - 16:, :], carry[:, col])
            carry[:, col] = v[tm - 16:, :]
            w0, w1, w2 = _conv_taps(fcv_ref, col)
            y = fcv_ref[3:4, col] + w0 * _shift_down(v, halo, 2)
            y = y + w1 * _shift_down(v, halo, 1)
            y = y + w2 * v
            upb_ref[:, col] = y.astype(BF16)
            upreb_ref[:, col] = v.astype(BF16)
            return y

        ff = None
        for j in range(nch):
            gc, vc = cols[j]
            gate = conv(up_gate[:, gc], gc)
            val = conv(up_val[:, gc], vc)
            ab = ((gate * _gelu_parts(gate)[0]) * val).astype(BF16)
            a_ref[:, gc] = ab
            part = _dot(ab, wd_ref[gc, :])
            ff = part if ff is None else ff + part
        ffb_ref[...] = ff.astype(BF16)
        r4 = lax.rsqrt(_rowmean(ff * ff) + EPS)
        y = x1 + vec_ref[0:1, :] * ((ff * r4) * vec_ref[1:2, :])
        e = y - _interleave(tg_ref[...])
        dy_ref[...] = e * (1.0 / d)
        loss_ref[...] += jnp.sum(_rowmean(e * e))

    row = lambda n: pl.BlockSpec((tm, n), lambda i: (i, 0))
    return pl.pallas_call(
        body, name="ffn_block_fwd", grid=(t // tm,),
        in_specs=[row(d), row(d)] + [_vmem()] * 5,
        out_specs=[row(d), row(2 * fp), row(2 * fp), row(fp), row(d), row(d), pl.BlockSpec((8, LANES), lambda i: (0, 0))],
        out_shape=[_sds((t, d), BF16), _sds((t, 2 * fp), BF16), _sds((t, 2 * fp), BF16), _sds((t, fp), BF16),
                   _sds((t, d), BF16), _sds((t, d), F32), _sds((8, LANES), F32)],
        scratch_shapes=[pltpu.VMEM((16, 2 * fp), F32)],
        compiler_params=_params("arbitrary"),
    )(x1, target, vec_pre, vec, fcv, w_up, w_down)


def _ffn_block_bwd(dy, ffb, x1, upb, upreb, vec_pre, vec, fcv, w_up, w_down):
    t, d = dy.shape
    tm = GROUP
    fp = w_down.shape[0]
    nch = _column_chunks(fp)
    cw = fp // nch
    nt = t // tm

    def body(dy_ref, ff_ref, x1_ref, upb_ref, upreb_ref, vp_ref, vec_ref, fcv_ref, wu_ref, wd_ref,
             dff_ref, dup_ref, dx1_ref, red_ref, cred_ref, pred_ref, carry):
        @pl.when(pl.program_id(0) == 0)
        def _():
            carry[...] = jnp.zeros_like(carry)
            red_ref[...] = jnp.zeros_like(red_ref)
            cred_ref[...] = jnp.zeros_like(cred_ref)
            pred_ref[...] = jnp.zeros_like(pred_ref)

        dy_v = dy_ref[...]
        dffb = _postnorm_bwd(dy_v, ff_ref[...].astype(F32), vec_ref[0:1, :], vec_ref[1:2, :], red_ref).astype(BF16)
        dff_ref[...] = dffb

        def conv_bwd(dv, col):
            dx, (t0, t1, t2) = _conv3_bwd(dv, carry, col, _conv_taps(fcv_ref, col), upreb_ref[:, col].astype(F32))
            cred_ref[0:1, col] += t0
            cred_ref[1:2, col] += t1
            cred_ref[2:3, col] += t2
            cred_ref[3:4, col] += _colsum(dv)
            dxb = dx.astype(BF16)
            dup_ref[:, col] = dxb
            return _dot_nt(dxb, wu_ref[:, col])

        dh = None
        for j in range(nch):
            gc = slice(j * cw, (j + 1) * cw)
            vc = slice(fp + j * cw, fp + (j + 1) * cw)
            da = _dot_nt(dffb, wd_ref[gc, :])
            gate = upb_ref[:, gc].astype(F32)
            val = upb_ref[:, vc].astype(F32)
            cdf, th = _gelu_parts(gate)
            dcdf = 0.5 * (1.0 - th * th) * (GELU_C0 * (1.0 + (3.0 * GELU_C1) * (gate * gate)))
            part = conv_bwd(da * val * (cdf + gate * dcdf), gc) + conv_bwd(da * (gate * cdf), vc)
            dh = part if dh is None else dh + part

        _, r, nh = _prenorm(x1_ref[...], vp_ref)
        dx1_ref[...] = dy_v + _prenorm_bwd(dh, r, nh, vp_ref, pred_ref)

    rev = lambda n: pl.BlockSpec((tm, n), lambda i: (nt - 1 - i, 0))
    fixed = lambda n: pl.BlockSpec((8, n), lambda i: (0, 0))
    return pl.pallas_call(
        body, name="ffn_block_bwd", grid=(nt,),
        in_specs=[rev(d), rev(d), rev(d), rev(2 * fp), rev(2 * fp)] + [_vmem()] * 5,
        out_specs=[rev(d), rev(2 * fp), rev(d), fixed(d), fixed(2 * fp), fixed(d)],
        out_shape=[_sds((t, d), BF16), _sds((t, 2 * fp), BF16), _sds((t, d), F32), _sds((8, d), F32),
                   _sds((8, 2 * fp), F32), _sds((8, d), F32)],
        scratch_shapes=[pltpu.VMEM((16, 2 * fp), F32)],
        compiler_params=_params("arbitrary"),
    )(dy, ffb, x1, upb, upreb, vec_pre, vec, fcv, w_up, w_down)


def _mixer_block_bwd(dx1, ob, yarb, ybb, cvb, p5b, x, vec_pre, vec, w_in, w_pool, w_bout, w_o, job):
    t, d = dx1.shape
    tm = GROUP
    gw = d // len(POOL_WINDOWS)
    nt = t // tm
    pool_rows = 8 * (POOL_WINDOWS[-1] - 1)

    def body(dx1_ref, o_ref, yar_ref, yb_ref, cv_ref, p5_ref, x_ref, vp_ref, vec_ref, win_ref, wp_ref, wb_ref, wo_ref,
             dqm_ref, dp_ref, gx_ref, red_ref, pred_ref, dpgcarry, dcvcarry):
        i = pl.program_id(0)
        tix = nt - 1 - i

        @pl.when(i == 0)
        def _():
            red_ref[...] = jnp.zeros_like(red_ref)
            pred_ref[...] = jnp.zeros_like(pred_ref)
            dpgcarry[...] = jnp.zeros_like(dpgcarry)
            dcvcarry[...] = jnp.zeros_like(dcvcarry)

        pscale = vec_ref[2:3, :]
        dx1_v = dx1_ref[...]
        dob = _postnorm_bwd(dx1_v, o_ref[...].astype(F32), vec_ref[0:1, :], vec_ref[1:2, :], red_ref).astype(BF16)
        dqm_ref[:, d:2 * d] = dob
        dm = _dot_nt(dob, wo_ref[...])

        def dproj(cols, value):
            vb = value.astype(BF16)
            dp_ref[:, cols] = vb
            return _dot_nt(vb, win_ref[:, cols])

        sa = jax.nn.sigmoid(p5_ref[:, 3 * d:4 * d].astype(F32))
        yar = yar_ref[...].astype(F32)
        dya = dm * sa
        dh = dproj(slice(4 * d, 5 * d), dm * (yar * pscale) * sa * (1.0 - sa))
        red_ref[2:3, :] += _colsum(dya * yar)
        dyarb = (dya * pscale).astype(BF16)
        dqm_ref[:, 2 * d:3 * d] = dyarb
        sb = jax.nn.sigmoid(p5_ref[:, 4 * d:5 * d].astype(F32))
        dybb = (dm * sb).astype(BF16)
        dqm_ref[:, 0:d] = dybb
        dh = dh + dproj(slice(5 * d, 6 * d), dm * yb_ref[...].astype(F32) * sb * (1.0 - sb))

        for g, window in enumerate(POOL_WINDOWS):
            cols = slice(g * gw, (g + 1) * gw)
            rows = 8 * (window - 1)
            dpg = _dot_nt(dyarb[:, cols], wp_ref[g])
            dpgs = dpg * _inv_count(tix * tm, window)
            halo = _halo_bottom(dpgs[:rows, :], dpgcarry[:rows, cols])
            dpgcarry[:, cols] = dpgs[:pool_rows, :]
            s, shift = jnp.concatenate([dpgs, halo], axis=0), 1
            while shift < window:
                s = s[:s.shape[0] - 8 * shift, :] + s[8 * shift:, :]
                shift *= 2
            dh = dh + dproj(cols, s - dpg)

        dq = _dot_nt(dybb, wb_ref[...])
        ux = p5_ref[:, 0:d].astype(F32)
        uc = p5_ref[:, 2 * d:3 * d].astype(F32)
        dh = dh + dproj(slice(2 * d, 3 * d), dq * cv_ref[...].astype(F32))
        dcv = dq * p5_ref[:, d:2 * d].astype(F32)
        taps = (vec_ref[4:5, :], vec_ref[5:6, :], vec_ref[6:7, :])
        dpv, (t0, t1, t2) = _conv3_bwd(dcv, dcvcarry, slice(0, d), taps, uc * ux)
        red_ref[3:4, :] += _colsum(dcv)
        red_ref[4:5, :] += t0
        red_ref[5:6, :] += t1
        red_ref[6:7, :] += t2
        dh = dh + dproj(slice(d, 2 * d), dpv * uc)
        dh = dh + dproj(slice(3 * d, 4 * d), dpv * ux)

        _, r, nh = _prenorm(_interleave(x_ref[...]), vp_ref)
        gx_ref[...] = _deinterleave(dx1_v + _prenorm_bwd(dh, r, nh, vp_ref, pred_ref))

    rev = lambda n: pl.BlockSpec((tm, n), lambda i: (nt - 1 - i, 0))
    return _call(
        body, "mixer_block_bwd", (nt,), [rev(d)] * 5 + [rev(5 * d), rev(d)] + [_vmem()] * 6,
        [rev(3 * d), rev(6 * d), rev(d), pl.BlockSpec((16, d), lambda i: (0, 0)),
         pl.BlockSpec((8, d), lambda i: (0, 0))],
        [_sds((t, 3 * d), BF16), _sds((t, 6 * d), BF16), _sds((t, d), F32), _sds((16, d), F32), _sds((8, d), F32)],
        [pltpu.VMEM((pool_rows, d), F32), pltpu.VMEM((16, d), F32)],
        _params("arbitrary"), (dx1, ob, yarb, ybb, cvb, p5b, x, vec_pre, vec, w_in, w_pool, w_bout, w_o), job)


def _matmul_tn(a, b, bm, bn, tk, by_col_block, name, job=None):
    t, m = a.shape
    n = b.shape[1]
    nk = t // tk
    parts = int(by_col_block)
    piece = bn // max(parts, 1)
    wide = _round_up(piece, LANES)

    def body(a_ref, b_ref, o_ref, acc_ref):
        k = pl.program_id(2)

        @pl.when(k == 0)
        def _():
            acc_ref[...] = jnp.zeros_like(acc_ref)

        acc_ref[...] += _dot_tn(a_ref[...], b_ref[...])

        @pl.when(k == nk - 1)
        def _():
            if parts:
                acc = acc_ref[...]
                for p in range(parts):
                    if wide > piece:
                        o_ref[p] = jnp.zeros((bm, wide), o_ref.dtype)
                    o_ref[p, :, 0:piece] = acc[:, p * piece:(p + 1) * piece].astype(o_ref.dtype)
            else:
                o_ref[...] = acc_ref[...].astype(o_ref.dtype)

    if by_col_block:
        out_shape = _sds((parts * n // bn, m, wide), BF16)
        out_spec = pl.BlockSpec((parts, bm, wide), lambda i, j, k: (j, i, 0))
    else:
        out_shape = _sds((m, n), BF16)
        out_spec = pl.BlockSpec((bm, bn), lambda i, j, k: (i, j))
    out = _call(body, name, (m // bm, n // bn, nk),
                [pl.BlockSpec((tk, bm), lambda i, j, k: (k, i)), pl.BlockSpec((tk, bn), lambda i, j, k: (k, j))],
                [out_spec], [out_shape], [pltpu.VMEM((bm, bn), F32)],
                _params("arbitrary", "arbitrary", "arbitrary"), (a, b), job)
    return out if job is not None else out[0]


def _side_by_side(blocks, name):
    n, r, c = blocks.shape
    rb = _row_block(r) // 2

    def body(in_ref, o_ref):
        for j in range(n):
            o_ref[:, j * c:(j + 1) * c] = in_ref[j]

    return pl.pallas_call(
        body, name=name, grid=(r // rb,), in_specs=[pl.BlockSpec((n, rb, c), lambda i: (0, i, 0))],
        out_specs=pl.BlockSpec((rb, n * c), lambda i: (i, 0)), out_shape=_sds((r, n * c), blocks.dtype),
        compiler_params=_params("parallel"))(blocks)


def _matmul_tn_groups(a, b, groups, tk, name, job=None):
    t, m = a.shape
    w = m // groups
    nk = t // tk

    def body(a_ref, b_ref, o_ref, acc_ref):
        k = pl.program_id(1)

        @pl.when(k == 0)
        def _():
            acc_ref[...] = jnp.zeros_like(acc_ref)

        acc_ref[...] += _dot_tn(a_ref[...], b_ref[...])

        @pl.when(k == nk - 1)
        def _():
            o_ref[...] = acc_ref[...].astype(o_ref.dtype)

    blk = pl.BlockSpec((tk, w), lambda g, k: (k, g))
    out = _call(body, name, (groups, nk), [blk, blk], [pl.BlockSpec((None, w, w), lambda g, k: (g, 0, 0))],
                [_sds((groups, w, w), BF16)], [pltpu.VMEM((w, w), F32)], _params("arbitrary", "arbitrary"), (a, b), job)
    return out if job is not None else out[0]


def _round_up(n, k):
    return (n + k - 1) // k * k


def _rows8(rows, width):
    n = _round_up(len(rows), 8)
    rows = list(rows) + [jnp.zeros((1, width), F32)] * (n - len(rows))
    return jnp.concatenate(rows, axis=0)


def kernel(x, c, g_pre_mix, g_post_mix, g_pre_ffn, g_post_ffn, w_ada, b_ada, w_in, w_pool, pool_scale, conv_w, conv_b, w_bout, w_o, w_up, ffn_conv_w, ffn_conv_b, w_down, loss_target, m_g_pre_mix, m_g_post_mix, m_g_pre_ffn, m_g_post_ffn, m_w_ada, m_b_ada, m_w_in, m_w_pool, m_pool_scale, m_conv_w, m_conv_b, m_w_bout, m_w_o, m_w_up, m_ffn_conv_w, m_ffn_conv_b, m_w_down, v_g_pre_mix, v_g_post_mix, v_g_pre_ffn, v_g_post_ffn, v_w_ada, v_b_ada, v_w_in, v_w_pool, v_pool_scale, v_conv_w, v_conv_b, v_w_bout, v_w_o, v_w_up, v_ffn_conv_w, v_ffn_conv_b, v_w_down):
    t, d = x.shape[1], x.shape[2]
    ngroups = len(POOL_WINDOWS)
    gw = d // ngroups
    ada_n = w_ada.shape[2]
    in_n = w_in.shape[2]
    up_n = w_up.shape[2]
    fp = NDEV * w_down.shape[1]

    xi, yi, ci = _position()
    me = _linear(xi, yi, ci)
    chip = 2 * xi + yi
    core = jnp.reshape(ci, (1,)).astype(jnp.int32)
    sel = jnp.stack([2 * chip + ci, chip]).astype(jnp.int32)

    x2 = x.reshape(t, d)
    target = loss_target.reshape(t, d)

    cw_n = conv_w.shape[2]
    pack = jnp.concatenate([c.reshape(1, d), conv_w[0].reshape(1, 3 * cw_n), ffn_conv_w[0].reshape(1, 3 * up_n)], axis=1)
    pack = jnp.pad(pack, ((0, 0), (0, _round_up(pack.shape[1], LANES) - pack.shape[1])))
    b_piece = lax.dynamic_slice_in_dim(b_ada, me * ada_n, ada_n, axis=1)
    mixer_weights = _allgather_job(
        [w_in[0].astype(BF16), w_bout[0].astype(BF16), w_o[0].astype(BF16), w_pool[0].astype(BF16)],
        ["cols", "rows", "rows", "mid"], 0.5, 0.75)
    learned = _rows8([g_pre_mix, g_post_mix, g_pre_ffn, g_post_ffn, pool_scale, conv_b], d)
    gathered, vec_pre_mix, vec_mix, vec_pre_ffn, vec_ffn, fcv, w_in_f, g_bout, g_o, w_pool_f = (
        _gather_weights_and_modulation(mixer_weights, pack, w_ada[0], b_piece, learned, ffn_conv_b, d, cw_n, up_n))
    w_bout_f = g_bout.reshape(d, d)
    w_o_f = g_o.reshape(d, d)
    c16 = jnp.pad(gathered[:, :d], ((0, 8), (0, 0))).astype(BF16)

    ffn_weights = _allgather_job([w_up[0].astype(BF16), w_down[0].astype(BF16)], ["rows", "rows"], 0.5, 0.8)
    h1b, p5b, qmb, cvb, yarb, ybb, ob, x1, g_up, g_down = _mixer_block_fwd(
        x2, vec_pre_mix, vec_mix, w_in_f, w_pool_f, w_bout_f, w_o_f, ffn_weights)
    w_up_f = _side_by_side(g_up, "w_up_side_by_side")
    w_down_f = g_down.reshape(fp, d)
    h2b, upb, upreb, ab, ffb, dy, loss_part = _ffn_block_fwd(x1, target, vec_pre_ffn, vec_ffn, fcv, w_up_f, w_down_f)

    tk, tk_wide = min(4096, t), min(2048, t)
    chip_sum = lambda gs, ss, name: _chip_sums(gs, ss, core, name)
    dffb, dupre, dx1, red_ffn, red_fconv, red_pre_ffn = _ffn_block_bwd(
        dy, ffb, x1, upb, upreb, vec_pre_ffn, vec_ffn, fcv, w_up_f, w_down_f)
    chunk = fp // _column_chunks(fp)
    gr_up = _matmul_tn(h2b, dupre, d, chunk, tk_wide, chunk // up_n, "wgrad_up")
    gr_down, sib_up = _matmul_tn(ab, dffb, chunk, d, tk_wide, False, "wgrad_down", _sibling_job([gr_up]))
    gr_down = gr_down.reshape(NDEV, fp // NDEV, d)
    sib_ffn = [sib_up] + list(_run_job(_sibling_job([gr_down]), "rs_sibling_down"))
    dqmb, dproj, grad_x, red_mix, red_pre_mix = _mixer_block_bwd(
        dx1, ob, yarb, ybb, cvb, p5b, x2, vec_pre_mix, vec_mix, w_in_f, w_pool_f, w_bout_f, w_o_f, None)
    gr_in, fc_up, fc_down = _matmul_tn(h1b, dproj, d, in_n, tk, True, "wgrad_in",
                                       _chips_job(chip_sum([gr_up, gr_down], sib_ffn, "rs_chip_sum_ffn")))
    sib_in = _run_job(_sibling_job([gr_in]), "rs_sibling_in")
    gr_qmp, fc_in = _matmul_tn_groups(qmb, dqmb, 3, tk, "wgrad_bout_o_pool",
                                      _chips_job(chip_sum([gr_in], sib_in, "rs_chip_sum_in")))
    gr_bout = gr_qmp[0].reshape(NDEV, d // NDEV, d)
    gr_o = gr_qmp[1].reshape(NDEV, d // NDEV, d)
    gr_pool = jnp.stack([gr_qmp[2, g * gw:(g + 1) * gw, g * gw:(g + 1) * gw] for g in range(ngroups)])
    gr_pool = gr_pool.reshape(ngroups, NDEV, gw // NDEV, gw).transpose(1, 0, 2, 3).reshape(NDEV, -1, gw)
    rest = [gr_bout, gr_o, gr_pool]
    sib_rest = _run_job(_sibling_job(rest), "rs_sibling_rest")

    dmod = [red_pre_mix[0:1], red_pre_mix[1:2], red_mix[1:2], red_pre_ffn[0:1], red_pre_ffn[1:2], red_ffn[1:2]]
    small = [red_pre_mix[2:3], red_mix[0:1], red_pre_ffn[2:3], red_ffn[0:1], red_mix[2:3], red_mix[3:4],
             red_mix[4:5], red_mix[5:6], red_mix[6:7]] + dmod
    flat = jnp.concatenate(small + [red_fconv[0:4].reshape(1, 8 * fp), loss_part[0:1, 0:1]], axis=1)
    flat_n = flat.shape[1]
    width = 8 * LANES
    rows = _round_up(-(-flat_n // width), 8)
    flat = jnp.pad(flat, ((0, 0), (0, rows * width - flat_n))).reshape(rows, width)
    gat, tot, *fc_rest = _small_allreduce(flat, "allreduce_small_rs_rest",
                                          _chips_job(chip_sum(rest, sib_rest, "rs_chip_sum_rest")))

    def big(grad, from_sibling, from_chips, w, m, v, name, transposed=False):
        shape = w.shape
        flat = (lambda a: a[0].T) if transposed else (lambda a: a.reshape((-1, shape[-1])))
        outs = _reduce_adamw(grad, from_sibling, from_chips, sel, flat(w), flat(m), flat(v), name, transposed)
        return [(a.T if transposed else a).reshape(shape) for a in outs]

    g_w_up, d_w_up, nm_w_up, nv_w_up = big(gr_up, sib_ffn[0], fc_up, w_up, m_w_up, v_w_up, "adamw_up",
                                           transposed=up_n % LANES != 0)
    g_w_down, d_w_down, nm_w_down, nv_w_down = big(gr_down, sib_ffn[1], fc_down, w_down, m_w_down, v_w_down, "adamw_down")
    g_w_in, d_w_in, nm_w_in, nv_w_in = big(gr_in, sib_in[0], fc_in, w_in, m_w_in, v_w_in, "adamw_in")
    flat2 = lambda a: a.reshape((-1, a.shape[-1]))
    rest_w = [(w_bout, m_w_bout, v_w_bout), (w_o, m_w_o, v_w_o), (w_pool, m_w_pool, v_w_pool)]
    rest_out = _reduce_adamw_group(
        [(g, s, f, flat2(w), flat2(m), flat2(v)) for g, s, f, (w, m, v) in zip(rest, sib_rest, fc_rest, rest_w)],
        sel, "adamw_bout_o_pool")
    (g_w_bout, d_w_bout, nm_w_bout, nv_w_bout), (g_w_o, d_w_o, nm_w_o, nv_w_o), (g_w_pool, d_w_pool, nm_w_pool, nv_w_pool) = [
        [a.reshape(w.shape) for a in outs] for outs, (w, _, _) in zip(rest_out, rest_w)]

    tot = tot.reshape(1, rows * width)
    gat = gat.reshape(NDEV, rows * width)
    take = lambda k: tot[:, k * d:(k + 1) * d]
    g_g_pre_mix, g_g_post_mix, g_g_pre_ffn, g_g_post_ffn, g_pool_scale, g_conv_b = [take(k) for k in range(6)]
    g_conv_w_full = jnp.concatenate([take(6), take(7), take(8)], axis=0)
    g_conv_w = lax.dynamic_slice_in_dim(g_conv_w_full, me * cw_n, cw_n, axis=1)
    g_b_ada = tot[:, 9 * d:15 * d]
    dmod_all = gat[:, 9 * d:15 * d]
    fconv_tot = tot[:, 15 * d:15 * d + 8 * fp].reshape(4, 2 * fp)
    loss = 0.5 * tot[0, 15 * d + 8 * fp]
    g_ffn_conv_b = fconv_tot[3:4]
    g_ffn_conv_w = lax.dynamic_slice_in_dim(fconv_tot[0:3], me * up_n, up_n, axis=1)
    dmod_piece = lax.dynamic_slice_in_dim(dmod_all, me * ada_n, ada_n, axis=1)
    g_w_ada = _wada_grad(c16, jnp.pad(dmod_piece, ((0, 8), (0, 0))).astype(BF16))

    names_small = [(g_pre_mix, g_g_pre_mix, m_g_pre_mix, v_g_pre_mix), (g_post_mix, g_g_post_mix, m_g_post_mix, v_g_post_mix),
                   (g_pre_ffn, g_g_pre_ffn, m_g_pre_ffn, v_g_pre_ffn), (g_post_ffn, g_g_post_ffn, m_g_post_ffn, v_g_post_ffn),
                   (b_ada, g_b_ada, m_b_ada, v_b_ada), (pool_scale, g_pool_scale, m_pool_scale, v_pool_scale),
                   (conv_w, g_conv_w, m_conv_w, v_conv_w), (conv_b, g_conv_b, m_conv_b, v_conv_b),
                   (ffn_conv_w, g_ffn_conv_w, m_ffn_conv_w, v_ffn_conv_w), (ffn_conv_b, g_ffn_conv_b, m_ffn_conv_b, v_ffn_conv_b)]
    small_out = _adamw_group([[a.reshape((-1, a.shape[-1])) for a in q] for q in names_small], "adamw_small")
    unpack_small = lambda k: [outs[k].reshape(q[0].shape) for outs, q in zip(small_out, names_small)]
    (d_g_pre_mix, d_g_post_mix, d_g_pre_ffn, d_g_post_ffn, d_b_ada, d_pool_scale, d_conv_w, d_conv_b,
     d_ffn_conv_w, d_ffn_conv_b) = unpack_small(0)
    (nm_g_pre_mix, nm_g_post_mix, nm_g_pre_ffn, nm_g_post_ffn, nm_b_ada, nm_pool_scale, nm_conv_w, nm_conv_b,
     nm_ffn_conv_w, nm_ffn_conv_b) = unpack_small(1)
    (nv_g_pre_mix, nv_g_post_mix, nv_g_pre_ffn, nv_g_post_ffn, nv_b_ada, nv_pool_scale, nv_conv_w, nv_conv_b,
     nv_ffn_conv_w, nv_ffn_conv_b) = unpack_small(2)
    d_w_ada, nm_w_ada, nv_w_ada = [a.reshape(w_ada.shape) for a in
                                   _adamw(w_ada[0], g_w_ada, m_w_ada[0], v_w_ada[0], "adamw_ada")]

    grads = [g_g_pre_mix, g_g_post_mix, g_g_pre_ffn, g_g_post_ffn, g_w_ada.reshape(w_ada.shape), g_b_ada, g_w_in,
             g_w_pool, g_pool_scale, g_conv_w.reshape(conv_w.shape), g_conv_b, g_w_bout, g_w_o, g_w_up,
             g_ffn_conv_w.reshape(ffn_conv_w.shape), g_ffn_conv_b, g_w_down]
    deltas = [d_g_pre_mix, d_g_post_mix, d_g_pre_ffn, d_g_post_ffn, d_w_ada, d_b_ada, d_w_in, d_w_pool, d_pool_scale,
              d_conv_w, d_conv_b, d_w_bout, d_w_o, d_w_up, d_ffn_conv_w, d_ffn_conv_b, d_w_down]
    new_m = [nm_g_pre_mix, nm_g_post_mix, nm_g_pre_ffn, nm_g_post_ffn, nm_w_ada, nm_b_ada, nm_w_in, nm_w_pool,
             nm_pool_scale, nm_conv_w, nm_conv_b, nm_w_bout, nm_w_o, nm_w_up, nm_ffn_conv_w, nm_ffn_conv_b, nm_w_down]
    new_v = [nv_g_pre_mix, nv_g_post_mix, nv_g_pre_ffn, nv_g_post_ffn, nv_w_ada, nv_b_ada, nv_w_in, nv_w_pool,
             nv_pool_scale, nv_conv_w, nv_conv_b, nv_w_bout, nv_w_o, nv_w_up, nv_ffn_conv_w, nv_ffn_conv_b, nv_w_down]
    return (loss, grad_x.reshape(x.shape), *grads, *deltas, *new_m, *new_v)
```

```python
import math

import jax
import jax.numpy as jnp
from jax import lax
from jax.experimental import pallas as pl
from jax.experimental.pallas import tpu as pltpu

F32 = jnp.float32
BF16 = jnp.bfloat16
MESH = pl.DeviceIdType.MESH

NDEV = 8
NCHIP = 4
EPS = 1e-6
POOL_WINDOWS = (2, 4, 8, 16)
LANES = 128
ADAM_LR = 0.001
ADAM_B1 = 0.9
ADAM_B2 = 0.999
ADAM_EPS = 1e-08
ADAM_WD = 0.01
ADAM_STEP = 10
GELU_C0 = math.sqrt(2.0 / math.pi)
GELU_C1 = 0.044715
VMEM_LIMIT = 56 * 2**20


def _vmem():
    return pl.BlockSpec(memory_space=pltpu.VMEM)


def _any():
    return pl.BlockSpec(memory_space=pl.ANY)


def _params(*sem):
    return pltpu.CompilerParams(dimension_semantics=sem, vmem_limit_bytes=VMEM_LIMIT)


def _sds(shape, dtype):
    return jax.ShapeDtypeStruct(tuple(shape), dtype)


def _position():
    return lax.axis_index("x"), lax.axis_index("y"), lax.axis_index("c")


def _linear(x, y, c):
    return 4 * x + 2 * y + c


def _dot(a, b):
    return jnp.dot(a, b, preferred_element_type=F32)


def _dot_nt(a, b):
    return lax.dot_general(a, b, (((1,), (1,)), ((), ())), preferred_element_type=F32)


def _dot_tn(a, b):
    return lax.dot_general(a, b, (((0,), (0,)), ((), ())), preferred_element_type=F32)


def _colsum(v):
    return jnp.sum(v, axis=0, keepdims=True)


def _rowmean(v):
    return jnp.mean(v, axis=-1, keepdims=True)


GROUP = 256


def _interleave(v):
    g, n = v.shape
    return jnp.swapaxes(v.reshape(8, g // 8, n), 0, 1).reshape(g, n)


def _deinterleave(v):
    g, n = v.shape
    return jnp.swapaxes(v.reshape(g // 8, 8, n), 0, 1).reshape(g, n)


def _halo_top(cur_last, prev_last):
    rows, n = cur_last.shape
    c3 = cur_last.reshape(rows // 8, 8, n)
    p3 = prev_last.reshape(rows // 8, 8, n)
    sub = lax.broadcasted_iota(jnp.int32, c3.shape, 1)
    return jnp.where(sub == 0, pltpu.roll(p3, 1, 1), pltpu.roll(c3, 1, 1)).reshape(rows, n)


def _halo_bottom(cur_first, next_first):
    rows, n = cur_first.shape
    c3 = cur_first.reshape(rows // 8, 8, n)
    n3 = next_first.reshape(rows // 8, 8, n)
    sub = lax.broadcasted_iota(jnp.int32, c3.shape, 1)
    return jnp.where(sub == 7, pltpu.roll(n3, 7, 1), pltpu.roll(c3, 7, 1)).reshape(rows, n)


def _shift_down(v, halo, k):
    rows = v.shape[0]
    return jnp.concatenate([halo[halo.shape[0] - 8 * k:, :], v[:rows - 8 * k, :]], axis=0)


def _shift_up(v, halo, k):
    return jnp.concatenate([v[8 * k:, :], halo[:8 * k, :]], axis=0)


def _inv_count(first_token, window):
    row = lax.broadcasted_iota(jnp.int32, (GROUP, 1), 0)
    t = first_token + (row % 8) * (GROUP // 8) + row // 8
    return 1.0 / jnp.minimum(t + 1, window).astype(F32)


class _Job:
    def __init__(self, inputs, out_shape, scratch, phases):
        self.inputs, self.out_shape, self.scratch, self.phases = list(inputs), list(out_shape), list(scratch), phases


def _call(body, name, grid, in_specs, out_specs, out_shape, scratch_shapes, params, operands, job=None):
    if job is None:
        return pl.pallas_call(body, name=name, grid=grid, in_specs=in_specs, out_specs=out_specs, out_shape=out_shape,
                              scratch_shapes=scratch_shapes, compiler_params=params)(*operands)
    n_in, n_out, n_scr = len(in_specs), len(out_specs), len(scratch_shapes)
    j_in, j_out = len(job.inputs), len(job.out_shape)
    steps = math.prod(grid)

    def hosted(*refs):
        own_in, refs = refs[:n_in], refs[n_in:]
        jin, refs = refs[:j_in], refs[j_in:]
        own_out, refs = refs[:n_out], refs[n_out:]
        jout, refs = refs[:j_out], refs[j_out:]
        own_scr, jscr = refs[:n_scr], refs[n_scr:]
        step = pl.program_id(0)
        for axis in range(1, len(grid)):
            step = step * grid[axis] + pl.program_id(axis)
        for frac, fn in job.phases[:-1]:
            pl.when(step == int(frac * (steps - 1)))(lambda fn=fn: fn(jin, jout, jscr))
        body(*own_in, *own_out, *own_scr)
        pl.when(step == steps - 1)(lambda: job.phases[-1][1](jin, jout, jscr))

    return pl.pallas_call(
        hosted, name=name, grid=grid, in_specs=list(in_specs) + [_any()] * j_in,
        out_specs=list(out_specs) + [_any()] * j_out, out_shape=list(out_shape) + job.out_shape,
        scratch_shapes=list(scratch_shapes) + job.scratch, compiler_params=params)(*operands, *job.inputs)


def _run_job(job, name):
    n_in, n_out = len(job.inputs), len(job.out_shape)

    def body(*refs):
        for _, fn in job.phases:
            fn(refs[:n_in], refs[n_in:n_in + n_out], refs[n_in + n_out:])

    return pl.pallas_call(body, name=name, out_shape=job.out_shape, in_specs=[_any()] * n_in,
                          out_specs=[_any()] * n_out, scratch_shapes=job.scratch)(*job.inputs)


def _peers(x, y, c):
    out = []
    for k in range(1, NDEV):
        out.append(((1 - x) if k & 4 else x, (1 - y) if k & 2 else y, (1 - c) if k & 1 else c))
    return out


def _small_allreduce(v, name, job):
    r, n = v.shape
    j_in, j_out = len(job.inputs), len(job.out_shape)

    def body(v_ref, *rest):
        jin, rest = rest[:j_in], rest[j_in:]
        gat_ref, sum_ref = rest[:2]
        jout, rest = rest[2:2 + j_out], rest[2 + j_out:]
        send_sems, recv_sems, local_sem = rest[:3]
        jscr = rest[3:]
        job.phases[0][1](jin, jout, jscr)
        x, y, c = _position()
        me = _linear(x, y, c)
        mine = pltpu.make_async_copy(v_ref, gat_ref.at[me], local_sem)
        mine.start()
        peers = _peers(x, y, c)
        sends = []
        for k, peer in enumerate(peers):
            cp = pltpu.make_async_remote_copy(src_ref=v_ref, dst_ref=gat_ref.at[me], send_sem=send_sems.at[k],
                                              recv_sem=recv_sems.at[k], device_id=peer, device_id_type=MESH)
            cp.start()
            sends.append(cp)
        for k, peer in enumerate(peers):
            pltpu.make_async_remote_copy(src_ref=v_ref, dst_ref=gat_ref.at[_linear(*peer)], send_sem=send_sems.at[k],
                                         recv_sem=recv_sems.at[k], device_id=peer, device_id_type=MESH).wait_recv()
        for cp in sends:
            cp.wait_send()
        mine.wait()
        acc = gat_ref[0]
        for j in range(1, NDEV):
            acc = acc + gat_ref[j]
        sum_ref[...] = acc
        job.phases[-1][1](jin, jout, jscr)

    return pl.pallas_call(
        body, name=name, out_shape=[_sds((NDEV, r, n), F32), _sds((r, n), F32)] + job.out_shape,
        in_specs=[_vmem()] + [_any()] * j_in, out_specs=[_vmem()] * 2 + [_any()] * j_out,
        scratch_shapes=[pltpu.SemaphoreType.DMA((NDEV - 1,)), pltpu.SemaphoreType.DMA((NDEV - 1,)),
                        pltpu.SemaphoreType.DMA(())] + job.scratch,
    )(v, *job.inputs)


def _exchange_rows(src_for, dst_ref, sems):
    send_sems, recv_sems, local_sem = sems
    x, y, c = _position()
    me = _linear(x, y, c)
    row = lambda j: dst_ref.at[pl.ds(j, 1), :]
    mine = pltpu.make_async_copy(src_for(me), row(me), local_sem)
    mine.start()
    peers = _peers(x, y, c)
    sends = []
    for k, peer in enumerate(peers):
        cp = pltpu.make_async_remote_copy(src_ref=src_for(_linear(*peer)), dst_ref=row(me), send_sem=send_sems.at[k],
                                          recv_sem=recv_sems.at[k], device_id=peer, device_id_type=MESH)
        cp.start()
        sends.append(cp)
    for k, peer in enumerate(peers):
        pltpu.make_async_remote_copy(src_ref=src_for(me), dst_ref=row(_linear(*peer)), send_sem=send_sems.at[k],
                                     recv_sem=recv_sems.at[k], device_id=peer, device_id_type=MESH).wait_recv()
    for cp in sends:
        cp.wait_send()
    mine.wait()


def _gather_weights_and_modulation(job, pack, w_ada, b_piece, learned, fcb, d, cw_n, up_n):
    n = pack.shape[1]
    m = w_ada.shape[1]
    j_in, j_out = len(job.inputs), len(job.out_shape)
    row_sems = [pltpu.SemaphoreType.DMA((NDEV - 1,)), pltpu.SemaphoreType.DMA((NDEV - 1,)), pltpu.SemaphoreType.DMA(())]

    def body(pack_ref, wada_ref, bp_ref, lrn_ref, fcb_ref, *rest):
        jin, rest = rest[:j_in], rest[j_in:]
        gat_ref, vpm_ref, vmix_ref, vpf_ref, vffn_ref, fcv_ref = rest[:6]
        jout, rest = rest[6:6 + j_out], rest[6 + j_out:]
        sems1, sems2, piece, mod_ref, modm, jscr = rest[0:3], rest[3:6], rest[6], rest[7], rest[8], rest[9:]
        phases = [fn for _, fn in job.phases]
        phases[0](jin, jout, jscr)
        _exchange_rows(lambda j: pack_ref, gat_ref, sems1)
        c16 = jnp.concatenate([gat_ref[:, 0:d], jnp.zeros((NDEV, d), F32)], axis=0).astype(BF16)
        piece[...] = (_dot(c16, wada_ref[...].astype(BF16)) + bp_ref[...])[0:NDEV, :]
        _exchange_rows(lambda j: piece.at[pl.ds(j, 1), :], mod_ref, sems2)

        for j in range(NDEV):
            done = 0
            while done < m:
                row, col = divmod(j * m + done, d)
                width = min(m - done, d - col)
                modm[row:row + 1, col:col + width] = mod_ref[j:j + 1, done:done + width]
                done += width
        for ref in (vpm_ref, vmix_ref, vpf_ref, vffn_ref, fcv_ref):
            ref[...] = jnp.zeros_like(ref)
        for ref, g_row, k in ((vpm_ref, 0, 0), (vpf_ref, 2, 3)):
            ref[0:1, :] = lrn_ref[g_row:g_row + 1, :]
            ref[1:2, :] = 1.0 + modm[k + 1:k + 2, :]
            ref[2:3, :] = modm[k:k + 1, :]
        vmix_ref[0:1, :] = modm[2:3, :]
        vmix_ref[1:2, :] = lrn_ref[1:2, :]
        vmix_ref[2:4, :] = lrn_ref[4:6, :]
        vffn_ref[0:1, :] = modm[5:6, :]
        vffn_ref[1:2, :] = lrn_ref[3:4, :]
        fcv_ref[3:4, :] = fcb_ref[...]
        for j in range(NDEV):
            for k in range(3):
                vmix_ref[4 + k:5 + k, j * cw_n:(j + 1) * cw_n] = gat_ref[j:j + 1, d + k * cw_n:d + (k + 1) * cw_n]
                off = d + 3 * cw_n + k * up_n
                fcv_ref[k:k + 1, j * up_n:(j + 1) * up_n] = gat_ref[j:j + 1, off:off + up_n]

        for fn in phases[1:]:
            fn(jin, jout, jscr)

    tables = [_sds((8, d), F32), _sds((16, d), F32), _sds((8, d), F32), _sds((8, d), F32), _sds((8, NDEV * up_n), F32)]
    return pl.pallas_call(
        body, name="gather_weights_and_modulation",
        out_shape=[_sds((NDEV, n), F32)] + tables + job.out_shape,
        in_specs=[_vmem()] * 5 + [_any()] * j_in, out_specs=[_vmem()] * 6 + [_any()] * j_out,
        scratch_shapes=row_sems + row_sems + [pltpu.VMEM((NDEV, m), F32), pltpu.VMEM((NDEV, m), F32),
                                              pltpu.VMEM((8, d), F32)] + job.scratch,
    )(pack, w_ada, b_piece, learned, fcb, *job.inputs)


def _gathered(shard, layout):
    if layout == "rows":
        return (NDEV,) + shard.shape, lambda ref, j: ref.at[j]
    if layout == "cols":
        r, c = shard.shape
        return (r, NDEV * c), lambda ref, j: ref.at[:, pl.ds(pl.multiple_of(j * c, LANES), c)]
    g, r, c = shard.shape
    return (g, NDEV * r, c), lambda ref, j: ref.at[:, pl.ds(pl.multiple_of(j * r, 16), r), :]


def _allgather_job(shards, layouts, relay_at, forward_at):
    n = len(shards)
    specs = [_gathered(s, l) for s, l in zip(shards, layouts)]
    halves = [s.shape[0] // 2 for s in shards]

    def plan(src, dst, sems):
        send_sems, recv_sems, _ = sems
        x, y, c = _position()
        me, sibling, xn, yn, dg = (x, y, c), (x, y, 1 - c), (1 - x, y, c), (x, 1 - y, c), (1 - x, 1 - y, c)

        def copy(a, k, block, to, half=None, from_src=False):
            blk = specs[a][1](dst[a], _linear(*block))
            if half is not None:
                blk = blk.at[pl.ds(half * halves[a], halves[a])]
            return pltpu.make_async_remote_copy(src_ref=src[a] if from_src else blk, dst_ref=blk,
                                                send_sem=send_sems.at[a, k], recv_sem=recv_sems.at[a, k],
                                                device_id=to, device_id_type=MESH)
        return copy, me, sibling, xn, yn, dg

    def local(src, dst, sems):
        x, y, c = _position()
        return [pltpu.make_async_copy(src[a], specs[a][1](dst[a], _linear(x, y, c)), sems[2].at[a]) for a in range(n)]

    def own(src, dst, sems):
        copy, me, sibling, xn, yn, dg = plan(src, dst, sems)
        return [copy(a, k, me, to, from_src=True) for k, to in ((1, xn), (2, yn), (0, sibling)) for a in range(n)]

    def relayed(src, dst, sems):
        copy, me, sibling, xn, yn, dg = plan(src, dst, sems)
        return ([copy(a, 3, xn, yn, half=0) for a in range(n)] + [copy(a, 5, xn, sibling) for a in range(n)],
                [copy(a, 4, yn, xn, half=1) for a in range(n)] + [copy(a, 6, yn, sibling) for a in range(n)])

    def diagonal(src, dst, sems):
        copy, me, sibling, xn, yn, dg = plan(src, dst, sems)
        return [copy(a, 7, dg, sibling) for a in range(n)]

    def start(src, dst, sems):
        for cp in local(src, dst, sems) + own(src, dst, sems):
            cp.start()

    def relay(src, dst, sems):
        copy, me, sibling, xn, yn, dg = plan(src, dst, sems)
        from_x, from_y = relayed(src, dst, sems)
        for a in range(n):
            copy(a, 1, xn, me).wait_recv()
        for cp in from_x:
            cp.start()
        for a in range(n):
            copy(a, 2, yn, me).wait_recv()
        for cp in from_y:
            cp.start()

    def forward(src, dst, sems):
        copy, me, sibling, xn, yn, dg = plan(src, dst, sems)
        for a in range(n):
            copy(a, 3, dg, me, half=0).wait_recv()
            copy(a, 4, dg, me, half=1).wait_recv()
        for cp in diagonal(src, dst, sems):
            cp.start()

    def finish(src, dst, sems):
        copy, me, sibling, xn, yn, dg = plan(src, dst, sems)
        other = lambda dev: (dev[0], dev[1], 1 - dev[2])
        for a in range(n):
            copy(a, 0, sibling, me).wait_recv()
            for k, dev in ((5, xn), (6, yn), (7, dg)):
                copy(a, k, other(dev), me).wait_recv()
        from_x, from_y = relayed(src, dst, sems)
        for cp in own(src, dst, sems) + from_x + from_y + diagonal(src, dst, sems):
            cp.wait_send()
        for cp in local(src, dst, sems):
            cp.wait()

    return _Job(shards, [_sds(spec[0], s.dtype) for spec, s in zip(specs, shards)],
                [pltpu.SemaphoreType.DMA((n, 8)), pltpu.SemaphoreType.DMA((n, 8)), pltpu.SemaphoreType.DMA((n,))],
                [(0.0, start), (relay_at, relay), (forward_at, forward), (1.0, finish)])


def _sibling_job(grads):
    n = len(grads)

    def copies(src, dst, sems):
        x, y, c = _position()
        return [pltpu.make_async_remote_copy(src_ref=src[a].at[2 * q + 1 - c], dst_ref=dst[a].at[q],
                                             send_sem=sems[0].at[a, q], recv_sem=sems[1].at[a, q],
                                             device_id=(x, y, 1 - c), device_id_type=MESH)
                for a in range(n) for q in range(NCHIP)]

    return _exchange_job(grads, NCHIP, copies)


def _chips_job(chip_sums):
    n = len(chip_sums)

    def copies(src, dst, sems):
        x, y, c = _position()
        chips = [(1 - x, y), (x, 1 - y), (1 - x, 1 - y)]
        return [pltpu.make_async_remote_copy(src_ref=src[a].at[2 * chip[0] + chip[1]], dst_ref=dst[a].at[j],
                                             send_sem=sems[0].at[a, j], recv_sem=sems[1].at[a, j],
                                             device_id=(*chip, c), device_id_type=MESH)
                for j, chip in enumerate(chips) for a in range(n)]

    return _exchange_job(chip_sums, 3, copies)


def _exchange_job(arrays, slots, copies):
    n = len(arrays)

    def start(src, dst, sems):
        for cp in copies(src, dst, sems):
            cp.start()

    def finish(src, dst, sems):
        cps = copies(src, dst, sems)
        for cp in cps:
            cp.wait_recv()
        for cp in cps:
            cp.wait_send()

    return _Job(arrays, [_sds((slots,) + a.shape[1:], a.dtype) for a in arrays],
                [pltpu.SemaphoreType.DMA((n, slots)), pltpu.SemaphoreType.DMA((n, slots))],
                [(0.0, start), (1.0, finish)])


def _row_block(r):
    if r <= 512:
        return r
    for rb in range(512, 15, -16):
        if r % rb == 0:
            return rb
    return r


def _chip_sums(grads, from_sibling, core, name):
    n = len(grads)

    def body(core_ref, *refs):
        del core_ref
        for a in range(n):
            refs[2 * n + a][...] = (refs[a][...].astype(F32) + refs[n + a][...].astype(F32)).astype(BF16)

    block = lambda g, index: pl.BlockSpec((None,) + g.shape[1:], index)
    grid_spec = pltpu.PrefetchScalarGridSpec(
        num_scalar_prefetch=1, grid=(NCHIP,),
        in_specs=[block(g, lambda q, core: (2 * q + core[0], 0, 0)) for g in grads]
        + [block(g, lambda q, core: (q, 0, 0)) for g in grads],
        out_specs=[block(g, lambda q, core: (q, 0, 0)) for g in grads])
    return pl.pallas_call(body, name=name, grid_spec=grid_spec,
                          out_shape=[_sds((NCHIP,) + g.shape[1:], BF16) for g in grads],
                          compiler_params=_params("parallel"))(core, *grads, *from_sibling)


def _adamw_math(w, g, m, v):
    m2 = ADAM_B1 * m + (1.0 - ADAM_B1) * g
    v2 = ADAM_B2 * v + (1.0 - ADAM_B2) * jnp.square(g)
    m_hat = m2 / (1.0 - ADAM_B1 ** ADAM_STEP)
    v_hat = v2 / (1.0 - ADAM_B2 ** ADAM_STEP)
    delta = -ADAM_LR * (m_hat / (jnp.sqrt(v_hat) + ADAM_EPS) + ADAM_WD * w)
    return delta, m2, v2


def _adamw(w, g, m, v, name):
    r, c = w.shape
    rb = _row_block(r)

    def body(w_ref, g_ref, m_ref, v_ref, d_ref, m2_ref, v2_ref):
        d, m2, v2 = _adamw_math(w_ref[...], g_ref[...], m_ref[...], v_ref[...])
        d_ref[...] = d
        m2_ref[...] = m2
        v2_ref[...] = v2

    blk = pl.BlockSpec((rb, c), lambda i: (i, 0))
    return pl.pallas_call(body, name=name, grid=(r // rb,), in_specs=[blk] * 4, out_specs=[blk] * 3,
                          out_shape=[_sds((r, c), F32)] * 3, compiler_params=_params("parallel"))(w, g, m, v)


def _adamw_group(items, name):
    n = len(items)

    def body(*refs):
        ins, outs = refs[:4 * n], refs[4 * n:]
        for a in range(n):
            w_ref, g_ref, m_ref, v_ref = ins[4 * a:4 * a + 4]
            for ref, val in zip(outs[3 * a:3 * a + 3], _adamw_math(w_ref[...], g_ref[...], m_ref[...], v_ref[...])):
                ref[...] = val

    outs = pl.pallas_call(body, name=name, in_specs=[_vmem()] * (4 * n), out_specs=[_vmem()] * (3 * n),
                          out_shape=[_sds(q[0].shape, F32) for q in items for _ in range(3)])(
        *[a for q in items for a in q])
    return [outs[3 * a:3 * a + 3] for a in range(n)]


def _reduce_adamw(grad, from_sibling, from_chips, sel, w, m, v, name, transposed=False):
    r, c = w.shape[::-1] if transposed else w.shape
    cp = grad.shape[2]
    rb = _row_block(r)

    def body(sel_ref, g_ref, s_ref, c0_ref, c1_ref, c2_ref, w_ref, m_ref, v_ref, go_ref, d_ref, m2_ref, v2_ref):
        del sel_ref
        g = g_ref[...].astype(F32) + s_ref[...].astype(F32)
        g = g + c0_ref[...].astype(F32)
        g = g + c1_ref[...].astype(F32)
        g = g + c2_ref[...].astype(F32)
        g = g.T[0:c, :] if transposed else g[:, 0:c]
        d, m2, v2 = _adamw_math(w_ref[...], g, m_ref[...], v_ref[...])
        go_ref[...] = g
        d_ref[...] = d
        m2_ref[...] = m2
        v2_ref[...] = v2

    blk = pl.BlockSpec((c, rb), lambda i, sel: (0, i)) if transposed else pl.BlockSpec((rb, c), lambda i, sel: (i, 0))
    grid_spec = pltpu.PrefetchScalarGridSpec(
        num_scalar_prefetch=1, grid=(r // rb,),
        in_specs=[pl.BlockSpec((None, rb, cp), lambda i, sel: (sel[0], i, 0)),
                  pl.BlockSpec((None, rb, cp), lambda i, sel: (sel[1], i, 0)),
                  pl.BlockSpec((None, rb, cp), lambda i, sel: (0, i, 0)),
                  pl.BlockSpec((None, rb, cp), lambda i, sel: (1, i, 0)),
                  pl.BlockSpec((None, rb, cp), lambda i, sel: (2, i, 0)),
                  blk, blk, blk],
        out_specs=[blk] * 4)
    return pl.pallas_call(body, name=name, grid_spec=grid_spec, out_shape=[_sds(w.shape, F32)] * 4,
                          compiler_params=_params("parallel"))(sel, grad, from_sibling, from_chips, from_chips,
                                                               from_chips, w, m, v)


def _reduce_adamw_group(items, sel, name):
    n = len(items)

    def body(sel_ref, *refs):
        del sel_ref
        ins, outs = refs[:8 * n], refs[8 * n:]
        for a in range(n):
            g_ref, s_ref, c0_ref, c1_ref, c2_ref, w_ref, m_ref, v_ref = ins[8 * a:8 * a + 8]
            g = g_ref[...].astype(F32) + s_ref[...].astype(F32)
            g = g + c0_ref[...].astype(F32)
            g = g + c1_ref[...].astype(F32)
            g = g + c2_ref[...].astype(F32)
            d, m2, v2 = _adamw_math(w_ref[...], g, m_ref[...], v_ref[...])
            for ref, val in zip(outs[4 * a:4 * a + 4], (g, d, m2, v2)):
                ref[...] = val

    in_specs, out_specs, out_shape, operands = [], [], [], []
    for grad, from_sibling, from_chips, w, m, v in items:
        slot = lambda index, shape=grad.shape[1:]: pl.BlockSpec((None,) + shape, index)
        full = pl.BlockSpec(w.shape, lambda i, sel: (0, 0))
        in_specs += [slot(lambda i, sel: (sel[0], 0, 0)), slot(lambda i, sel: (sel[1], 0, 0)),
                     slot(lambda i, sel: (0, 0, 0)), slot(lambda i, sel: (1, 0, 0)), slot(lambda i, sel: (2, 0, 0)),
                     full, full, full]
        out_specs += [full] * 4
        out_shape += [_sds(w.shape, F32)] * 4
        operands += [grad, from_sibling, from_chips, from_chips, from_chips, w, m, v]
    grid_spec = pltpu.PrefetchScalarGridSpec(num_scalar_prefetch=1, grid=(1,), in_specs=in_specs, out_specs=out_specs)
    outs = pl.pallas_call(body, name=name, grid_spec=grid_spec, out_shape=out_shape,
                          compiler_params=_params("arbitrary"))(sel, *operands)
    return [outs[4 * a:4 * a + 4] for a in range(n)]


def _wada_grad(c_all, dmod_piece):
    d = c_all.shape[1]
    n = dmod_piece.shape[1]

    def body(c_ref, dm_ref, o_ref):
        o_ref[...] = _dot_tn(c_ref[...], dm_ref[...])

    return pl.pallas_call(body, name="ada_wgrad", out_shape=_sds((d, n), F32),
                          in_specs=[_vmem()] * 2, out_specs=_vmem())(c_all, dmod_piece)


def _column_chunks(width):
    for n in (4, 2):
        if width % (n * LANES) == 0:
            return n
    return 1


def _conv_taps(ref, col):
    return ref[0:1, col], ref[1:2, col], ref[2:3, col]


def _gelu_parts(u):
    u2 = u * u
    th = jnp.tanh((GELU_C0 * u) * (1.0 + GELU_C1 * u2))
    dcdf = (1.0 - th * th) * ((0.5 * GELU_C0) * (1.0 + (3.0 * GELU_C1) * u2))
    return 0.5 * (1.0 + th), dcdf


def _conv3_bwd(dv, carry, col, taps, x):
    halo = _halo_bottom(dv[:16, :], carry[:, col])
    carry[:, col] = dv[:16, :]
    d1 = _shift_up(dv, halo, 1)
    d2 = _shift_up(dv, halo, 2)
    w0, w1, w2 = taps
    dx = w2 * dv
    dx = dx + w1 * d1
    dx = dx + w0 * d2
    return dx, (_colsum(d2 * x), _colsum(d1 * x), _colsum(dv * x))


def _prenorm(x, vp_ref):
    r = lax.rsqrt(_rowmean(x * x) + EPS)
    nh = x * r
    return (nh * vp_ref[0:1, :]) * vp_ref[1:2, :] + vp_ref[2:3, :], r, nh


def _prenorm_bwd(dh, r, nh, vp_ref, red_ref):
    g, sc1 = vp_ref[0:1, :], vp_ref[1:2, :]
    red_ref[0:1, :] += _colsum(dh)
    red_ref[1:2, :] += _colsum(dh * (nh * g))
    red_ref[2:3, :] += _colsum(dh * nh * sc1)
    dnh = dh * g * sc1
    return r * (dnh - nh * _rowmean(dnh * nh))


def _postnorm_bwd(dres, z, gate, gpost, red_ref):
    r = lax.rsqrt(_rowmean(z * z) + EPS)
    nh = z * r
    dn = dres * gate
    red_ref[0:1, :] += _colsum(dn * nh)
    red_ref[1:2, :] += _colsum(dres * (nh * gpost))
    dnh = dn * gpost
    return r * (dnh - nh * _rowmean(dnh * nh))


def _mixer_block_fwd(x, vec_pre, vec, w_in, w_pool, w_bout, w_o, job):
    t, d = x.shape
    tm = GROUP
    gw = d // len(POOL_WINDOWS)
    pool_rows = 8 * (POOL_WINDOWS[-1] - 1)

    def body(x_ref, vp_ref, vec_ref, win_ref, wp_ref, wb_ref, wo_ref,
             hb_ref, p5_ref, qm_ref, cv_ref, yar_ref, yb_ref, o_ref, x1_ref, mbuf, ucarry, pcarry):
        i = pl.program_id(0)

        @pl.when(i == 0)
        def _():
            ucarry[...] = jnp.zeros_like(ucarry)
            pcarry[...] = jnp.zeros_like(pcarry)

        xp = _interleave(x_ref[...])
        hb = _prenorm(xp, vp_ref)[0].astype(BF16)
        hb_ref[...] = hb
        proj = lambda k: _dot(hb, win_ref[:, k * d:(k + 1) * d])

        za = proj(4)
        p5_ref[:, 3 * d:4 * d] = za.astype(BF16)
        sa = jax.nn.sigmoid(za)
        u_pool = proj(0)
        for g, window in enumerate(POOL_WINDOWS):
            cols = slice(g * gw, (g + 1) * gw)
            rows = 8 * (window - 1)
            u = u_pool[:, cols]
            halo = _halo_top(u[tm - rows:, :], ucarry[pool_rows - rows:, cols])
            s, shift = jnp.concatenate([halo, u], axis=0), 1
            while shift < window:
                s = s[8 * shift:, :] + s[:s.shape[0] - 8 * shift, :]
                shift *= 2
            pgb = (s * _inv_count(i * tm, window) - u).astype(BF16)
            qm_ref[:, 2 * d + g * gw:2 * d + (g + 1) * gw] = pgb
            yar = _dot(pgb, wp_ref[g])
            yar_ref[:, cols] = yar
            mbuf[:, cols] = sa[:, cols] * (yar * vec_ref[2:3, cols])
        ucarry[...] = u_pool[tm - pool_rows:, :]

        ux = proj(1)
        uc = proj(3)
        p5_ref[:, 0:d] = ux.astype(BF16)
        p5_ref[:, 2 * d:3 * d] = uc.astype(BF16)
        p = uc * ux
        halo = _halo_top(p[tm - 16:, :], pcarry[...])
        pcarry[...] = p[tm - 16:, :]
        cv = vec_ref[3:4, :] + vec_ref[4:5, :] * _shift_down(p, halo, 2)
        cv = cv + vec_ref[5:6, :] * _shift_down(p, halo, 1)
        cv = cv + vec_ref[6:7, :] * p
        cv_ref[...] = cv
        ub = proj(2)
        p5_ref[:, d:2 * d] = ub.astype(BF16)
        qb = (ub * cv).astype(BF16)
        qm_ref[:, 0:d] = qb
        yb = _dot(qb, wb_ref[...])
        yb_ref[...] = yb

        zb = proj(5)
        p5_ref[:, 4 * d:5 * d] = zb.astype(BF16)
        mb = (mbuf[...] + jax.nn.sigmoid(zb) * yb).astype(BF16)
        qm_ref[:, d:2 * d] = mb
        o = _dot(mb, wo_ref[...])
        o_ref[...] = o
        r2 = lax.rsqrt(_rowmean(o * o) + EPS)
        x1_ref[...] = xp + vec_ref[0:1, :] * ((o * r2) * vec_ref[1:2, :])

    row = lambda n: pl.BlockSpec((tm, n), lambda i: (i, 0))
    widths = [d, 5 * d, 3 * d, d, d, d, d, d]
    return _call(
        body, "mixer_block_fwd", (t // tm,), [row(d)] + [_vmem()] * 6, [row(n) for n in widths],
        [_sds((t, n), BF16) for n in widths[:3]] + [_sds((t, n), F32) for n in widths[3:]],
        [pltpu.VMEM((tm, d), F32), pltpu.VMEM((pool_rows, d), F32), pltpu.VMEM((16, d), F32)],
        _params("arbitrary"), (x, vec_pre, vec, w_in, w_pool, w_bout, w_o), job)


def _ffn_block_fwd(x1, target, vec_pre, vec, fcv, w_up, w_down):
    t, d = x1.shape
    tm = GROUP
    fp = w_down.shape[0]
    nch = _column_chunks(fp)
    cw = fp // nch

    def body(x1_ref, tg_ref, vp_ref, vec_ref, fcv_ref, wu_ref, wd_ref,
             hb_ref, upb_ref, upreb_ref, a_ref, ffb_ref, dy_ref, loss_ref, carry):
        i = pl.program_id(0)

        @pl.when(i == 0)
        def _():
            carry[...] = jnp.zeros_like(carry)
            loss_ref[...] = jnp.zeros_like(loss_ref)

        x1 = x1_ref[...]
        hb = _prenorm(x1, vp_ref)[0].astype(BF16)
        hb_ref[...] = hb

        cols = [(slice(j * cw, (j + 1) * cw), slice(fp + j * cw, fp + (j + 1) * cw)) for j in range(nch)]
        up_gate = _dot(hb, wu_ref[:, 0:fp])
        up_val = _dot(hb, wu_ref[:, fp:2 * fp])

        def conv(v, col):
            halo = _halo_top(v[tm - 16:, :], carry[:, col])
            carry[:, col] = v[tm - 16:, :]
            w0, w1, w2 = _conv_taps(fcv_ref, col)
            y = fcv_ref[3:4, col] + w0 * _shift_down(v, halo, 2)
            y = y + w1 * _shift_down(v, halo, 1)
            y = y + w2 * v
            upb_ref[:, col] = y.astype(BF16)
            upreb_ref[:, col] = v.astype(BF16)
            return y

        ff = None
        for j in range(nch):
            gc, vc = cols[j]
            gate = conv(up_gate[:, gc], gc)
            val = conv(up_val[:, gc], vc)
            ab = ((gate * _gelu_parts(gate)[0]) * val).astype(BF16)
            a_ref[:, gc] = ab
            part = _dot(ab, wd_ref[gc, :])
            ff = part if ff is None else ff + part
        ffb_ref[...] = ff.astype(BF16)
        r4 = lax.rsqrt(_rowmean(ff * ff) + EPS)
        y = x1 + vec_ref[0:1, :] * ((ff * r4) * vec_ref[1:2, :])
        e = y - _interleave(tg_ref[...])
        dy_ref[...] = e * (1.0 / d)
        loss_ref[...] += jnp.sum(_rowmean(e * e))

    row = lambda n: pl.BlockSpec((tm, n), lambda i: (i, 0))
    return pl.pallas_call(
        body, name="ffn_block_fwd", grid=(t // tm,),
        in_specs=[row(d), row(d)] + [_vmem()] * 5,
        out_specs=[row(d), row(2 * fp), row(2 * fp), row(fp), row(d), row(d), pl.BlockSpec((8, LANES), lambda i: (0, 0))],
        out_shape=[_sds((t, d), BF16), _sds((t, 2 * fp), BF16), _sds((t, 2 * fp), BF16), _sds((t, fp), BF16),
                   _sds((t, d), BF16), _sds((t, d), F32), _sds((8, LANES), F32)],
        scratch_shapes=[pltpu.VMEM((16, 2 * fp), F32)],
        compiler_params=_params("arbitrary"),
    )(x1, target, vec_pre, vec, fcv, w_up, w_down)


def _ffn_block_bwd(dy, ffb, x1, upb, upreb, vec_pre, vec, fcv, w_up, w_down):
    t, d = dy.shape
    tm = GROUP
    fp = w_down.shape[0]
    nch = _column_chunks(fp)
    cw = fp // nch
    nt = t // tm

    def body(dy_ref, ff_ref, x1_ref, upb_ref, upreb_ref, vp_ref, vec_ref, fcv_ref, wu_ref, wd_ref,
             dff_ref, dup_ref, dx1_ref, red_ref, cred_ref, pred_ref, carry):
        @pl.when(pl.program_id(0) == 0)
        def _():
            carry[...] = jnp.zeros_like(carry)
            red_ref[...] = jnp.zeros_like(red_ref)
            cred_ref[...] = jnp.zeros_like(cred_ref)
            pred_ref[...] = jnp.zeros_like(pred_ref)

        dy_v = dy_ref[...]
        dffb = _postnorm_bwd(dy_v, ff_ref[...].astype(F32), vec_ref[0:1, :], vec_ref[1:2, :], red_ref).astype(BF16)
        dff_ref[...] = dffb

        def conv_bwd(dv, col):
            dx, (t0, t1, t2) = _conv3_bwd(dv, carry, col, _conv_taps(fcv_ref, col), upreb_ref[:, col].astype(F32))
            cred_ref[0:1, col] += t0
            cred_ref[1:2, col] += t1
            cred_ref[2:3, col] += t2
            cred_ref[3:4, col] += _colsum(dv)
            dxb = dx.astype(BF16)
            dup_ref[:, col] = dxb
            return _dot_nt(dxb, wu_ref[:, col])

        dh = None
        for j in range(nch):
            gc = slice(j * cw, (j + 1) * cw)
            vc = slice(fp + j * cw, fp + (j + 1) * cw)
            da = _dot_nt(dffb, wd_ref[gc, :])
            gate = upb_ref[:, gc].astype(F32)
            val = upb_ref[:, vc].astype(F32)
            cdf, dcdf = _gelu_parts(gate)
            part = conv_bwd(da * val * (cdf + gate * dcdf), gc) + conv_bwd(da * (gate * cdf), vc)
            dh = part if dh is None else dh + part

        _, r, nh = _prenorm(x1_ref[...], vp_ref)
        dx1_ref[...] = dy_v + _prenorm_bwd(dh, r, nh, vp_ref, pred_ref)

    rev = lambda n: pl.BlockSpec((tm, n), lambda i: (nt - 1 - i, 0))
    fixed = lambda n: pl.BlockSpec((8, n), lambda i: (0, 0))
    return pl.pallas_call(
        body, name="ffn_block_bwd", grid=(nt,),
        in_specs=[rev(d), rev(d), rev(d), rev(2 * fp), rev(2 * fp)] + [_vmem()] * 5,
        out_specs=[rev(d), rev(2 * fp), rev(d), fixed(d), fixed(2 * fp), fixed(d)],
        out_shape=[_sds((t, d), BF16), _sds((t, 2 * fp), BF16), _sds((t, d), F32), _sds((8, d), F32),
                   _sds((8, 2 * fp), F32), _sds((8, d), F32)],
        scratch_shapes=[pltpu.VMEM((16, 2 * fp), F32)],
        compiler_params=_params("arbitrary"),
    )(dy, ffb, x1, upb, upreb, vec_pre, vec, fcv, w_up, w_down)


def _mixer_block_bwd(dx1, ob, yarb, ybb, cvb, p5b, x, vec_pre, vec, w_in, w_pool, w_bout, w_o, job):
    t, d = dx1.shape
    tm = GROUP
    gw = d // len(POOL_WINDOWS)
    nt = t // tm
    pool_rows = 8 * (POOL_WINDOWS[-1] - 1)

    def body(dx1_ref, o_ref, yar_ref, yb_ref, cv_ref, p5_ref, x_ref, vp_ref, vec_ref, win_ref, wp_ref, wb_ref, wo_ref,
             dqm_ref, dp_ref, gx_ref, red_ref, pred_ref, dpgcarry, dcvcarry):
        i = pl.program_id(0)
        tix = nt - 1 - i

        @pl.when(i == 0)
        def _():
            red_ref[...] = jnp.zeros_like(red_ref)
            pred_ref[...] = jnp.zeros_like(pred_ref)
            dpgcarry[...] = jnp.zeros_like(dpgcarry)
            dcvcarry[...] = jnp.zeros_like(dcvcarry)

        pscale = vec_ref[2:3, :]
        dx1_v = dx1_ref[...]
        dob = _postnorm_bwd(dx1_v, o_ref[...].astype(F32), vec_ref[0:1, :], vec_ref[1:2, :], red_ref).astype(BF16)
        dqm_ref[:, d:2 * d] = dob
        dm = _dot_nt(dob, wo_ref[...])

        def dproj(cols, value):
            vb = value.astype(BF16)
            dp_ref[:, cols] = vb
            return _dot_nt(vb, win_ref[:, cols])

        sa = jax.nn.sigmoid(p5_ref[:, 3 * d:4 * d].astype(F32))
        yar = yar_ref[...].astype(F32)
        dya = dm * sa
        dh = dproj(slice(4 * d, 5 * d), dm * (yar * pscale) * sa * (1.0 - sa))
        red_ref[2:3, :] += _colsum(dya * yar)
        dyarb = (dya * pscale).astype(BF16)
        dqm_ref[:, 2 * d:3 * d] = dyarb
        sb = jax.nn.sigmoid(p5_ref[:, 4 * d:5 * d].astype(F32))
        dybb = (dm * sb).astype(BF16)
        dqm_ref[:, 0:d] = dybb
        dh = dh + dproj(slice(5 * d, 6 * d), dm * yb_ref[...].astype(F32) * sb * (1.0 - sb))

        for g, window in enumerate(POOL_WINDOWS):
            cols = slice(g * gw, (g + 1) * gw)
            rows = 8 * (window - 1)
            dpg = _dot_nt(dyarb[:, cols], wp_ref[g])
            dpgs = dpg * _inv_count(tix * tm, window)
            halo = _halo_bottom(dpgs[:rows, :], dpgcarry[:rows, cols])
            dpgcarry[:, cols] = dpgs[:pool_rows, :]
            s, shift = jnp.concatenate([dpgs, halo], axis=0), 1
            while shift < window:
                s = s[:s.shape[0] - 8 * shift, :] + s[8 * shift:, :]
                shift *= 2
            dh = dh + dproj(cols, s - dpg)

        dq = _dot_nt(dybb, wb_ref[...])
        ux = p5_ref[:, 0:d].astype(F32)
        uc = p5_ref[:, 2 * d:3 * d].astype(F32)
        dh = dh + dproj(slice(2 * d, 3 * d), dq * cv_ref[...].astype(F32))
        dcv = dq * p5_ref[:, d:2 * d].astype(F32)
        taps = (vec_ref[4:5, :], vec_ref[5:6, :], vec_ref[6:7, :])
        dpv, (t0, t1, t2) = _conv3_bwd(dcv, dcvcarry, slice(0, d), taps, uc * ux)
        red_ref[3:4, :] += _colsum(dcv)
        red_ref[4:5, :] += t0
        red_ref[5:6, :] += t1
        red_ref[6:7, :] += t2
        dh = dh + dproj(slice(d, 2 * d), dpv * uc)
        dh = dh + dproj(slice(3 * d, 4 * d), dpv * ux)

        _, r, nh = _prenorm(_interleave(x_ref[...]), vp_ref)
        gx_ref[...] = _deinterleave(dx1_v + _prenorm_bwd(dh, r, nh, vp_ref, pred_ref))

    rev = lambda n: pl.BlockSpec((tm, n), lambda i: (nt - 1 - i, 0))
    return _call(
        body, "mixer_block_bwd", (nt,), [rev(d)] * 5 + [rev(5 * d), rev(d)] + [_vmem()] * 6,
        [rev(3 * d), rev(6 * d), rev(d), pl.BlockSpec((16, d), lambda i: (0, 0)),
         pl.BlockSpec((8, d), lambda i: (0, 0))],
        [_sds((t, 3 * d), BF16), _sds((t, 6 * d), BF16), _sds((t, d), F32), _sds((16, d), F32), _sds((8, d), F32)],
        [pltpu.VMEM((pool_rows, d), F32), pltpu.VMEM((16, d), F32)],
        _params("arbitrary"), (dx1, ob, yarb, ybb, cvb, p5b, x, vec_pre, vec, w_in, w_pool, w_bout, w_o), job)


def _matmul_tn(a, b, bm, bn, tk, by_col_block, name, job=None):
    t, m = a.shape
    n = b.shape[1]
    nk = t // tk
    parts = int(by_col_block)
    piece = bn // max(parts, 1)
    wide = _round_up(piece, LANES)

    def body(a_ref, b_ref, o_ref, acc_ref):
        k = pl.program_id(2)

        @pl.when(k == 0)
        def _():
            acc_ref[...] = jnp.zeros_like(acc_ref)

        acc_ref[...] += _dot_tn(a_ref[...], b_ref[...])

        @pl.when(k == nk - 1)
        def _():
            if parts:
                acc = acc_ref[...]
                for p in range(parts):
                    if wide > piece:
                        o_ref[p] = jnp.zeros((bm, wide), o_ref.dtype)
                    o_ref[p, :, 0:piece] = acc[:, p * piece:(p + 1) * piece].astype(o_ref.dtype)
            else:
                o_ref[...] = acc_ref[...].astype(o_ref.dtype)

    if by_col_block:
        out_shape = _sds((parts * n // bn, m, wide), BF16)
        out_spec = pl.BlockSpec((parts, bm, wide), lambda i, j, k: (j, i, 0))
    else:
        out_shape = _sds((m, n), BF16)
        out_spec = pl.BlockSpec((bm, bn), lambda i, j, k: (i, j))
    out = _call(body, name, (m // bm, n // bn, nk),
                [pl.BlockSpec((tk, bm), lambda i, j, k: (k, i)), pl.BlockSpec((tk, bn), lambda i, j, k: (k, j))],
                [out_spec], [out_shape], [pltpu.VMEM((bm, bn), F32)],
                _params("arbitrary", "arbitrary", "arbitrary"), (a, b), job)
    return out if job is not None else out[0]


def _side_by_side(blocks, name):
    n, r, c = blocks.shape
    rb = _row_block(r) // 2

    def body(in_ref, o_ref):
        for j in range(n):
            o_ref[:, j * c:(j + 1) * c] = in_ref[j]

    return pl.pallas_call(
        body, name=name, grid=(r // rb,), in_specs=[pl.BlockSpec((n, rb, c), lambda i: (0, i, 0))],
        out_specs=pl.BlockSpec((rb, n * c), lambda i: (i, 0)), out_shape=_sds((r, n * c), blocks.dtype),
        compiler_params=_params("parallel"))(blocks)


def _matmul_tn_groups(a, b, groups, tk, name, job=None):
    t, m = a.shape
    w = m // groups
    nk = t // tk

    def body(a_ref, b_ref, o_ref, acc_ref):
        k = pl.program_id(1)

        @pl.when(k == 0)
        def _():
            acc_ref[...] = jnp.zeros_like(acc_ref)

        acc_ref[...] += _dot_tn(a_ref[...], b_ref[...])

        @pl.when(k == nk - 1)
        def _():
            o_ref[...] = acc_ref[...].astype(o_ref.dtype)

    blk = pl.BlockSpec((tk, w), lambda g, k: (k, g))
    out = _call(body, name, (groups, nk), [blk, blk], [pl.BlockSpec((None, w, w), lambda g, k: (g, 0, 0))],
                [_sds((groups, w, w), BF16)], [pltpu.VMEM((w, w), F32)], _params("arbitrary", "arbitrary"), (a, b), job)
    return out if job is not None else out[0]


def _round_up(n, k):
    return (n + k - 1) // k * k


def _rows8(rows, width):
    n = _round_up(len(rows), 8)
    rows = list(rows) + [jnp.zeros((1, width), F32)] * (n - len(rows))
    return jnp.concatenate(rows, axis=0)


def kernel(x, c, g_pre_mix, g_post_mix, g_pre_ffn, g_post_ffn, w_ada, b_ada, w_in, w_pool, pool_scale, conv_w, conv_b, w_bout, w_o, w_up, ffn_conv_w, ffn_conv_b, w_down, loss_target, m_g_pre_mix, m_g_post_mix, m_g_pre_ffn, m_g_post_ffn, m_w_ada, m_b_ada, m_w_in, m_w_pool, m_pool_scale, m_conv_w, m_conv_b, m_w_bout, m_w_o, m_w_up, m_ffn_conv_w, m_ffn_conv_b, m_w_down, v_g_pre_mix, v_g_post_mix, v_g_pre_ffn, v_g_post_ffn, v_w_ada, v_b_ada, v_w_in, v_w_pool, v_pool_scale, v_conv_w, v_conv_b, v_w_bout, v_w_o, v_w_up, v_ffn_conv_w, v_ffn_conv_b, v_w_down):
    t, d = x.shape[1], x.shape[2]
    ngroups = len(POOL_WINDOWS)
    gw = d // ngroups
    ada_n = w_ada.shape[2]
    in_n = w_in.shape[2]
    up_n = w_up.shape[2]
    fp = NDEV * w_down.shape[1]

    xi, yi, ci = _position()
    me = _linear(xi, yi, ci)
    chip = 2 * xi + yi
    core = jnp.reshape(ci, (1,)).astype(jnp.int32)
    sel = jnp.stack([2 * chip + ci, chip]).astype(jnp.int32)

    x2 = x.reshape(t, d)
    target = loss_target.reshape(t, d)

    cw_n = conv_w.shape[2]
    pack = jnp.concatenate([c.reshape(1, d), conv_w[0].reshape(1, 3 * cw_n), ffn_conv_w[0].reshape(1, 3 * up_n)], axis=1)
    pack = jnp.pad(pack, ((0, 0), (0, _round_up(pack.shape[1], LANES) - pack.shape[1])))
    b_piece = lax.dynamic_slice_in_dim(b_ada, me * ada_n, ada_n, axis=1)
    mixer_weights = _allgather_job(
        [w_in[0].astype(BF16), w_bout[0].astype(BF16), w_o[0].astype(BF16), w_pool[0].astype(BF16)],
        ["cols", "rows", "rows", "mid"], 0.5, 0.75)
    learned = _rows8([g_pre_mix, g_post_mix, g_pre_ffn, g_post_ffn, pool_scale, conv_b], d)
    gathered, vec_pre_mix, vec_mix, vec_pre_ffn, vec_ffn, fcv, w_in_f, g_bout, g_o, w_pool_f = (
        _gather_weights_and_modulation(mixer_weights, pack, w_ada[0], b_piece, learned, ffn_conv_b, d, cw_n, up_n))
    w_bout_f = g_bout.reshape(d, d)
    w_o_f = g_o.reshape(d, d)
    c16 = jnp.pad(gathered[:, :d], ((0, 8), (0, 0))).astype(BF16)

    ffn_weights = _allgather_job([w_up[0].astype(BF16), w_down[0].astype(BF16)], ["rows", "rows"], 0.5, 0.8)
    h1b, p5b, qmb, cvb, yarb, ybb, ob, x1, g_up, g_down = _mixer_block_fwd(
        x2, vec_pre_mix, vec_mix, w_in_f, w_pool_f, w_bout_f, w_o_f, ffn_weights)
    w_up_f = _side_by_side(g_up, "w_up_side_by_side")
    w_down_f = g_down.reshape(fp, d)
    h2b, upb, upreb, ab, ffb, dy, loss_part = _ffn_block_fwd(x1, target, vec_pre_ffn, vec_ffn, fcv, w_up_f, w_down_f)

    tk, tk_wide = min(4096, t), min(2048, t)
    chip_sum = lambda gs, ss, name: _chip_sums(gs, ss, core, name)
    dffb, dupre, dx1, red_ffn, red_fconv, red_pre_ffn = _ffn_block_bwd(
        dy, ffb, x1, upb, upreb, vec_pre_ffn, vec_ffn, fcv, w_up_f, w_down_f)
    chunk = fp // _column_chunks(fp)
    gr_up = _matmul_tn(h2b, dupre, d, chunk, tk_wide, chunk // up_n, "wgrad_up")
    gr_down, sib_up = _matmul_tn(ab, dffb, chunk, d, tk_wide, False, "wgrad_down", _sibling_job([gr_up]))
    gr_down = gr_down.reshape(NDEV, fp // NDEV, d)
    sib_ffn = [sib_up] + list(_run_job(_sibling_job([gr_down]), "rs_sibling_down"))
    dqmb, dproj, grad_x, red_mix, red_pre_mix = _mixer_block_bwd(
        dx1, ob, yarb, ybb, cvb, p5b, x2, vec_pre_mix, vec_mix, w_in_f, w_pool_f, w_bout_f, w_o_f, None)
    gr_in, fc_up, fc_down = _matmul_tn(h1b, dproj, d, in_n, tk, True, "wgrad_in",
                                       _chips_job(chip_sum([gr_up, gr_down], sib_ffn, "rs_chip_sum_ffn")))
    sib_in = _run_job(_sibling_job([gr_in]), "rs_sibling_in")
    gr_qmp, fc_in = _matmul_tn_groups(qmb, dqmb, 3, tk, "wgrad_bout_o_pool",
                                      _chips_job(chip_sum([gr_in], sib_in, "rs_chip_sum_in")))
    gr_bout = gr_qmp[0].reshape(NDEV, d // NDEV, d)
    gr_o = gr_qmp[1].reshape(NDEV, d // NDEV, d)
    gr_pool = jnp.stack([gr_qmp[2, g * gw:(g + 1) * gw, g * gw:(g + 1) * gw] for g in range(ngroups)])
    gr_pool = gr_pool.reshape(ngroups, NDEV, gw // NDEV, gw).transpose(1, 0, 2, 3).reshape(NDEV, -1, gw)
    rest = [gr_bout, gr_o, gr_pool]
    sib_rest = _run_job(_sibling_job(rest), "rs_sibling_rest")

    dmod = [red_pre_mix[0:1], red_pre_mix[1:2], red_mix[1:2], red_pre_ffn[0:1], red_pre_ffn[1:2], red_ffn[1:2]]
    small = [red_pre_mix[2:3], red_mix[0:1], red_pre_ffn[2:3], red_ffn[0:1], red_mix[2:3], red_mix[3:4],
             red_mix[4:5], red_mix[5:6], red_mix[6:7]] + dmod
    flat = jnp.concatenate(small + [red_fconv[0:4].reshape(1, 8 * fp), loss_part[0:1, 0:1]], axis=1)
    flat_n = flat.shape[1]
    width = 8 * LANES
    rows = _round_up(-(-flat_n // width), 8)
    flat = jnp.pad(flat, ((0, 0), (0, rows * width - flat_n))).reshape(rows, width)
    gat, tot, *fc_rest = _small_allreduce(flat, "allreduce_small_rs_rest",
                                          _chips_job(chip_sum(rest, sib_rest, "rs_chip_sum_rest")))

    def big(grad, from_sibling, from_chips, w, m, v, name, transposed=False):
        shape = w.shape
        flat = (lambda a: a[0].T) if transposed else (lambda a: a.reshape((-1, shape[-1])))
        outs = _reduce_adamw(grad, from_sibling, from_chips, sel, flat(w), flat(m), flat(v), name, transposed)
        return [(a.T if transposed else a).reshape(shape) for a in outs]

    g_w_up, d_w_up, nm_w_up, nv_w_up = big(gr_up, sib_ffn[0], fc_up, w_up, m_w_up, v_w_up, "adamw_up",
                                           transposed=up_n % LANES != 0)
    g_w_down, d_w_down, nm_w_down, nv_w_down = big(gr_down, sib_ffn[1], fc_down, w_down, m_w_down, v_w_down, "adamw_down")
    g_w_in, d_w_in, nm_w_in, nv_w_in = big(gr_in, sib_in[0], fc_in, w_in, m_w_in, v_w_in, "adamw_in")
    flat2 = lambda a: a.reshape((-1, a.shape[-1]))
    rest_w = [(w_bout, m_w_bout, v_w_bout), (w_o, m_w_o, v_w_o), (w_pool, m_w_pool, v_w_pool)]
    rest_out = _reduce_adamw_group(
        [(g, s, f, flat2(w), flat2(m), flat2(v)) for g, s, f, (w, m, v) in zip(rest, sib_rest, fc_rest, rest_w)],
        sel, "adamw_bout_o_pool")
    (g_w_bout, d_w_bout, nm_w_bout, nv_w_bout), (g_w_o, d_w_o, nm_w_o, nv_w_o), (g_w_pool, d_w_pool, nm_w_pool, nv_w_pool) = [
        [a.reshape(w.shape) for a in outs] for outs, (w, _, _) in zip(rest_out, rest_w)]

    tot = tot.reshape(1, rows * width)
    gat = gat.reshape(NDEV, rows * width)
    take = lambda k: tot[:, k * d:(k + 1) * d]
    g_g_pre_mix, g_g_post_mix, g_g_pre_ffn, g_g_post_ffn, g_pool_scale, g_conv_b = [take(k) for k in range(6)]
    g_conv_w_full = jnp.concatenate([take(6), take(7), take(8)], axis=0)
    g_conv_w = lax.dynamic_slice_in_dim(g_conv_w_full, me * cw_n, cw_n, axis=1)
    g_b_ada = tot[:, 9 * d:15 * d]
    dmod_all = gat[:, 9 * d:15 * d]
    fconv_tot = tot[:, 15 * d:15 * d + 8 * fp].reshape(4, 2 * fp)
    loss = 0.5 * tot[0, 15 * d + 8 * fp]
    g_ffn_conv_b = fconv_tot[3:4]
    g_ffn_conv_w = lax.dynamic_slice_in_dim(fconv_tot[0:3], me * up_n, up_n, axis=1)
    dmod_piece = lax.dynamic_slice_in_dim(dmod_all, me * ada_n, ada_n, axis=1)
    g_w_ada = _wada_grad(c16, jnp.pad(dmod_piece, ((0, 8), (0, 0))).astype(BF16))

    names_small = [(g_pre_mix, g_g_pre_mix, m_g_pre_mix, v_g_pre_mix), (g_post_mix, g_g_post_mix, m_g_post_mix, v_g_post_mix),
                   (g_pre_ffn, g_g_pre_ffn, m_g_pre_ffn, v_g_pre_ffn), (g_post_ffn, g_g_post_ffn, m_g_post_ffn, v_g_post_ffn),
                   (b_ada, g_b_ada, m_b_ada, v_b_ada), (pool_scale, g_pool_scale, m_pool_scale, v_pool_scale),
                   (conv_w, g_conv_w, m_conv_w, v_conv_w), (conv_b, g_conv_b, m_conv_b, v_conv_b),
                   (ffn_conv_w, g_ffn_conv_w, m_ffn_conv_w, v_ffn_conv_w), (ffn_conv_b, g_ffn_conv_b, m_ffn_conv_b, v_ffn_conv_b)]
    small_out = _adamw_group([[a.reshape((-1, a.shape[-1])) for a in q] for q in names_small], "adamw_small")
    unpack_small = lambda k: [outs[k].reshape(q[0].shape) for outs, q in zip(small_out, names_small)]
    (d_g_pre_mix, d_g_post_mix, d_g_pre_ffn, d_g_post_ffn, d_b_ada, d_pool_scale, d_conv_w, d_conv_b,
     d_ffn_conv_w, d_ffn_conv_b) = unpack_small(0)
    (nm_g_pre_mix, nm_g_post_mix, nm_g_pre_ffn, nm_g_post_ffn, nm_b_ada, nm_pool_scale, nm_conv_w, nm_conv_b,
     nm_ffn_conv_w, nm_ffn_conv_b) = unpack_small(1)
    (nv_g_pre_mix, nv_g_post_mix, nv_g_pre_ffn, nv_g_post_ffn, nv_b_ada, nv_pool_scale, nv_conv_w, nv_conv_b,
     nv_ffn_conv_w, nv_ffn_conv_b) = unpack_small(2)
    d_w_ada, nm_w_ada, nv_w_ada = [a.reshape(w_ada.shape) for a in
                                   _adamw(w_ada[0], g_w_ada, m_w_ada[0], v_w_ada[0], "adamw_ada")]

    grads = [g_g_pre_mix, g_g_post_mix, g_g_pre_ffn, g_g_post_ffn, g_w_ada.reshape(w_ada.shape), g_b_ada, g_w_in,
             g_w_pool, g_pool_scale, g_conv_w.reshape(conv_w.shape), g_conv_b, g_w_bout, g_w_o, g_w_up,
             g_ffn_conv_w.reshape(ffn_conv_w.shape), g_ffn_conv_b, g_w_down]
    deltas = [d_g_pre_mix, d_g_post_mix, d_g_pre_ffn, d_g_post_ffn, d_w_ada, d_b_ada, d_w_in, d_w_pool, d_pool_scale,
              d_conv_w, d_conv_b, d_w_bout, d_w_o, d_w_up, d_ffn_conv_w, d_ffn_conv_b, d_w_down]
    new_m = [nm_g_pre_mix, nm_g_post_mix, nm_g_pre_ffn, nm_g_post_ffn, nm_w_ada, nm_b_ada, nm_w_in, nm_w_pool,
             nm_pool_scale, nm_conv_w, nm_conv_b, nm_w_bout, nm_w_o, nm_w_up, nm_ffn_conv_w, nm_ffn_conv_b, nm_w_down]
    new_v = [nv_g_pre_mix, nv_g_post_mix, nv_g_pre_ffn, nv_g_post_ffn, nv_w_ada, nv_b_ada, nv_w_in, nv_w_pool,
             nv_pool_scale, nv_conv_w, nv_conv_b, nv_w_bout, nv_w_o, nv_w_up, nv_ffn_conv_w, nv_ffn_conv_b, nv_w_down]
    return (loss, grad_x.reshape(x.shape), *grads, *deltas, *new_m, *new_v)
```

```python
import math

import jax
import jax.numpy as jnp
from jax import lax
from jax.experimental import pallas as pl
from jax.experimental.pallas import tpu as pltpu

F32 = jnp.float32
BF16 = jnp.bfloat16
MESH = pl.DeviceIdType.MESH

NDEV = 8
NCHIP = 4
EPS = 1e-6
POOL_WINDOWS = (2, 4, 8, 16)
LANES = 128
ADAM_LR = 0.001
ADAM_B1 = 0.9
ADAM_B2 = 0.999
ADAM_EPS = 1e-08
ADAM_WD = 0.01
ADAM_STEP = 10
GELU_C0 = math.sqrt(2.0 / math.pi)
GELU_C1 = 0.044715
VMEM_LIMIT = 56 * 2**20


def _vmem():
    return pl.BlockSpec(memory_space=pltpu.VMEM)


def _any():
    return pl.BlockSpec(memory_space=pl.ANY)


def _params(*sem):
    return pltpu.CompilerParams(dimension_semantics=sem, vmem_limit_bytes=VMEM_LIMIT)


def _sds(shape, dtype):
    return jax.ShapeDtypeStruct(tuple(shape), dtype)


def _position():
    return lax.axis_index("x"), lax.axis_index("y"), lax.axis_index("c")


def _linear(x, y, c):
    return 4 * x + 2 * y + c


def _dot(a, b):
    return jnp.dot(a, b, preferred_element_type=F32)


def _dot_nt(a, b):
    return lax.dot_general(a, b, (((1,), (1,)), ((), ())), preferred_element_type=F32)


def _dot_tn(a, b):
    return lax.dot_general(a, b, (((0,), (0,)), ((), ())), preferred_element_type=F32)


def _colsum(v):
    return jnp.sum(v, axis=0, keepdims=True)


def _rowmean(v):
    return jnp.mean(v, axis=-1, keepdims=True)


GROUP = 256


def _interleave(v):
    g, n = v.shape
    return jnp.swapaxes(v.reshape(8, g // 8, n), 0, 1).reshape(g, n)


def _deinterleave(v):
    g, n = v.shape
    return jnp.swapaxes(v.reshape(g // 8, 8, n), 0, 1).reshape(g, n)


def _halo_top(cur_last, prev_last):
    rows, n = cur_last.shape
    c3 = cur_last.reshape(rows // 8, 8, n)
    p3 = prev_last.reshape(rows // 8, 8, n)
    sub = lax.broadcasted_iota(jnp.int32, c3.shape, 1)
    return jnp.where(sub == 0, pltpu.roll(p3, 1, 1), pltpu.roll(c3, 1, 1)).reshape(rows, n)


def _halo_bottom(cur_first, next_first):
    rows, n = cur_first.shape
    c3 = cur_first.reshape(rows // 8, 8, n)
    n3 = next_first.reshape(rows // 8, 8, n)
    sub = lax.broadcasted_iota(jnp.int32, c3.shape, 1)
    return jnp.where(sub == 7, pltpu.roll(n3, 7, 1), pltpu.roll(c3, 7, 1)).reshape(rows, n)


def _shift_down(v, halo, k):
    rows = v.shape[0]
    return jnp.concatenate([halo[halo.shape[0] - 8 * k:, :], v[:rows - 8 * k, :]], axis=0)


def _shift_up(v, halo, k):
    return jnp.concatenate([v[8 * k:, :], halo[:8 * k, :]], axis=0)


def _inv_count(first_token, window):
    row = lax.broadcasted_iota(jnp.int32, (GROUP, 1), 0)
    t = first_token + (row % 8) * (GROUP // 8) + row // 8
    return 1.0 / jnp.minimum(t + 1, window).astype(F32)


class _Job:
    def __init__(self, inputs, out_shape, scratch, phases):
        self.inputs, self.out_shape, self.scratch, self.phases = list(inputs), list(out_shape), list(scratch), phases


def _call(body, name, grid, in_specs, out_specs, out_shape, scratch_shapes, params, operands, job=None):
    if job is None:
        return pl.pallas_call(body, name=name, grid=grid, in_specs=in_specs, out_specs=out_specs, out_shape=out_shape,
                              scratch_shapes=scratch_shapes, compiler_params=params)(*operands)
    n_in, n_out, n_scr = len(in_specs), len(out_specs), len(scratch_shapes)
    j_in, j_out = len(job.inputs), len(job.out_shape)
    steps = math.prod(grid)

    def hosted(*refs):
        own_in, refs = refs[:n_in], refs[n_in:]
        jin, refs = refs[:j_in], refs[j_in:]
        own_out, refs = refs[:n_out], refs[n_out:]
        jout, refs = refs[:j_out], refs[j_out:]
        own_scr, jscr = refs[:n_scr], refs[n_scr:]
        step = pl.program_id(0)
        for axis in range(1, len(grid)):
            step = step * grid[axis] + pl.program_id(axis)
        for frac, fn in job.phases[:-1]:
            pl.when(step == int(frac * (steps - 1)))(lambda fn=fn: fn(jin, jout, jscr))
        body(*own_in, *own_out, *own_scr)
        pl.when(step == steps - 1)(lambda: job.phases[-1][1](jin, jout, jscr))

    return pl.pallas_call(
        hosted, name=name, grid=grid, in_specs=list(in_specs) + [_any()] * j_in,
        out_specs=list(out_specs) + [_any()] * j_out, out_shape=list(out_shape) + job.out_shape,
        scratch_shapes=list(scratch_shapes) + job.scratch, compiler_params=params)(*operands, *job.inputs)


def _run_job(job, name):
    n_in, n_out = len(job.inputs), len(job.out_shape)

    def body(*refs):
        for _, fn in job.phases:
            fn(refs[:n_in], refs[n_in:n_in + n_out], refs[n_in + n_out:])

    return pl.pallas_call(body, name=name, out_shape=job.out_shape, in_specs=[_any()] * n_in,
                          out_specs=[_any()] * n_out, scratch_shapes=job.scratch)(*job.inputs)


def _peers(x, y, c):
    out = []
    for k in range(1, NDEV):
        out.append(((1 - x) if k & 4 else x, (1 - y) if k & 2 else y, (1 - c) if k & 1 else c))
    return out


def _small_allreduce(v, name, job):
    r, n = v.shape
    j_in, j_out = len(job.inputs), len(job.out_shape)

    def body(v_ref, *rest):
        jin, rest = rest[:j_in], rest[j_in:]
        gat_ref, sum_ref = rest[:2]
        jout, rest = rest[2:2 + j_out], rest[2 + j_out:]
        send_sems, recv_sems, local_sem = rest[:3]
        jscr = rest[3:]
        job.phases[0][1](jin, jout, jscr)
        x, y, c = _position()
        me = _linear(x, y, c)
        mine = pltpu.make_async_copy(v_ref, gat_ref.at[me], local_sem)
        mine.start()
        peers = _peers(x, y, c)
        sends = []
        for k, peer in enumerate(peers):
            cp = pltpu.make_async_remote_copy(src_ref=v_ref, dst_ref=gat_ref.at[me], send_sem=send_sems.at[k],
                                              recv_sem=recv_sems.at[k], device_id=peer, device_id_type=MESH)
            cp.start()
            sends.append(cp)
        for k, peer in enumerate(peers):
            pltpu.make_async_remote_copy(src_ref=v_ref, dst_ref=gat_ref.at[_linear(*peer)], send_sem=send_sems.at[k],
                                         recv_sem=recv_sems.at[k], device_id=peer, device_id_type=MESH).wait_recv()
        for cp in sends:
            cp.wait_send()
        mine.wait()
        acc = gat_ref[0]
        for j in range(1, NDEV):
            acc = acc + gat_ref[j]
        sum_ref[...] = acc
        job.phases[-1][1](jin, jout, jscr)

    return pl.pallas_call(
        body, name=name, out_shape=[_sds((NDEV, r, n), F32), _sds((r, n), F32)] + job.out_shape,
        in_specs=[_vmem()] + [_any()] * j_in, out_specs=[_vmem()] * 2 + [_any()] * j_out,
        scratch_shapes=[pltpu.SemaphoreType.DMA((NDEV - 1,)), pltpu.SemaphoreType.DMA((NDEV - 1,)),
                        pltpu.SemaphoreType.DMA(())] + job.scratch,
    )(v, *job.inputs)


def _exchange_rows(src_for, dst_ref, sems):
    send_sems, recv_sems, local_sem = sems
    x, y, c = _position()
    me = _linear(x, y, c)
    row = lambda j: dst_ref.at[pl.ds(j, 1), :]
    mine = pltpu.make_async_copy(src_for(me), row(me), local_sem)
    mine.start()
    peers = _peers(x, y, c)
    sends = []
    for k, peer in enumerate(peers):
        cp = pltpu.make_async_remote_copy(src_ref=src_for(_linear(*peer)), dst_ref=row(me), send_sem=send_sems.at[k],
                                          recv_sem=recv_sems.at[k], device_id=peer, device_id_type=MESH)
        cp.start()
        sends.append(cp)
    for k, peer in enumerate(peers):
        pltpu.make_async_remote_copy(src_ref=src_for(me), dst_ref=row(_linear(*peer)), send_sem=send_sems.at[k],
                                     recv_sem=recv_sems.at[k], device_id=peer, device_id_type=MESH).wait_recv()
    for cp in sends:
        cp.wait_send()
    mine.wait()


def _gather_weights_and_modulation(job, pack, w_ada, b_piece, learned, fcb, d, cw_n, up_n):
    n = pack.shape[1]
    m = w_ada.shape[1]
    j_in, j_out = len(job.inputs), len(job.out_shape)
    row_sems = [pltpu.SemaphoreType.DMA((NDEV - 1,)), pltpu.SemaphoreType.DMA((NDEV - 1,)), pltpu.SemaphoreType.DMA(())]

    def body(pack_ref, wada_ref, bp_ref, lrn_ref, fcb_ref, *rest):
        jin, rest = rest[:j_in], rest[j_in:]
        gat_ref, vpm_ref, vmix_ref, vpf_ref, vffn_ref, fcv_ref = rest[:6]
        jout, rest = rest[6:6 + j_out], rest[6 + j_out:]
        sems1, sems2, piece, mod_ref, modm, jscr = rest[0:3], rest[3:6], rest[6], rest[7], rest[8], rest[9:]
        phases = [fn for _, fn in job.phases]
        phases[0](jin, jout, jscr)
        _exchange_rows(lambda j: pack_ref, gat_ref, sems1)
        c16 = jnp.concatenate([gat_ref[:, 0:d], jnp.zeros((NDEV, d), F32)], axis=0).astype(BF16)
        piece[...] = (_dot(c16, wada_ref[...].astype(BF16)) + bp_ref[...])[0:NDEV, :]
        _exchange_rows(lambda j: piece.at[pl.ds(j, 1), :], mod_ref, sems2)

        for j in range(NDEV):
            done = 0
            while done < m:
                row, col = divmod(j * m + done, d)
                width = min(m - done, d - col)
                modm[row:row + 1, col:col + width] = mod_ref[j:j + 1, done:done + width]
                done += width
        for ref in (vpm_ref, vmix_ref, vpf_ref, vffn_ref, fcv_ref):
            ref[...] = jnp.zeros_like(ref)
        for ref, g_row, k in ((vpm_ref, 0, 0), (vpf_ref, 2, 3)):
            ref[0:1, :] = lrn_ref[g_row:g_row + 1, :]
            ref[1:2, :] = 1.0 + modm[k + 1:k + 2, :]
            ref[2:3, :] = modm[k:k + 1, :]
        vmix_ref[0:1, :] = modm[2:3, :]
        vmix_ref[1:2, :] = lrn_ref[1:2, :]
        vmix_ref[2:4, :] = lrn_ref[4:6, :]
        vffn_ref[0:1, :] = modm[5:6, :]
        vffn_ref[1:2, :] = lrn_ref[3:4, :]
        fcv_ref[3:4, :] = fcb_ref[...]
        for j in range(NDEV):
            for k in range(3):
                vmix_ref[4 + k:5 + k, j * cw_n:(j + 1) * cw_n] = gat_ref[j:j + 1, d + k * cw_n:d + (k + 1) * cw_n]
                off = d + 3 * cw_n + k * up_n
                fcv_ref[k:k + 1, j * up_n:(j + 1) * up_n] = gat_ref[j:j + 1, off:off + up_n]

        for fn in phases[1:]:
            fn(jin, jout, jscr)

    tables = [_sds((8, d), F32), _sds((16, d), F32), _sds((8, d), F32), _sds((8, d), F32), _sds((8, NDEV * up_n), F32)]
    return pl.pallas_call(
        body, name="gather_weights_and_modulation",
        out_shape=[_sds((NDEV, n), F32)] + tables + job.out_shape,
        in_specs=[_vmem()] * 5 + [_any()] * j_in, out_specs=[_vmem()] * 6 + [_any()] * j_out,
        scratch_shapes=row_sems + row_sems + [pltpu.VMEM((NDEV, m), F32), pltpu.VMEM((NDEV, m), F32),
                                              pltpu.VMEM((8, d), F32)] + job.scratch,
    )(pack, w_ada, b_piece, learned, fcb, *job.inputs)


def _gathered(shard, layout):
    if layout == "rows":
        return (NDEV,) + shard.shape, lambda ref, j: ref.at[j]
    if layout == "cols":
        r, c = shard.shape
        return (r, NDEV * c), lambda ref, j: ref.at[:, pl.ds(pl.multiple_of(j * c, LANES), c)]
    g, r, c = shard.shape
    return (g, NDEV * r, c), lambda ref, j: ref.at[:, pl.ds(pl.multiple_of(j * r, 16), r), :]


def _allgather_job(shards, layouts, relay_at, forward_at):
    n = len(shards)
    specs = [_gathered(s, l) for s, l in zip(shards, layouts)]
    halves = [s.shape[0] // 2 for s in shards]

    def plan(src, dst, sems):
        send_sems, recv_sems, _ = sems
        x, y, c = _position()
        me, sibling, xn, yn, dg = (x, y, c), (x, y, 1 - c), (1 - x, y, c), (x, 1 - y, c), (1 - x, 1 - y, c)

        def copy(a, k, block, to, half=None, from_src=False):
            blk = specs[a][1](dst[a], _linear(*block))
            if half is not None:
                blk = blk.at[pl.ds(half * halves[a], halves[a])]
            return pltpu.make_async_remote_copy(src_ref=src[a] if from_src else blk, dst_ref=blk,
                                                send_sem=send_sems.at[a, k], recv_sem=recv_sems.at[a, k],
                                                device_id=to, device_id_type=MESH)
        return copy, me, sibling, xn, yn, dg

    def local(src, dst, sems):
        x, y, c = _position()
        return [pltpu.make_async_copy(src[a], specs[a][1](dst[a], _linear(x, y, c)), sems[2].at[a]) for a in range(n)]

    def own(src, dst, sems):
        copy, me, sibling, xn, yn, dg = plan(src, dst, sems)
        return [copy(a, k, me, to, from_src=True) for k, to in ((1, xn), (2, yn), (0, sibling)) for a in range(n)]

    def relayed(src, dst, sems):
        copy, me, sibling, xn, yn, dg = plan(src, dst, sems)
        return ([copy(a, 3, xn, yn, half=0) for a in range(n)] + [copy(a, 5, xn, sibling) for a in range(n)],
                [copy(a, 4, yn, xn, half=1) for a in range(n)] + [copy(a, 6, yn, sibling) for a in range(n)])

    def diagonal(src, dst, sems):
        copy, me, sibling, xn, yn, dg = plan(src, dst, sems)
        return [copy(a, 7, dg, sibling) for a in range(n)]

    def start(src, dst, sems):
        for cp in local(src, dst, sems) + own(src, dst, sems):
            cp.start()

    def relay(src, dst, sems):
        copy, me, sibling, xn, yn, dg = plan(src, dst, sems)
        from_x, from_y = relayed(src, dst, sems)
        for a in range(n):
            copy(a, 1, xn, me).wait_recv()
        for cp in from_x:
            cp.start()
        for a in range(n):
            copy(a, 2, yn, me).wait_recv()
        for cp in from_y:
            cp.start()

    def forward(src, dst, sems):
        copy, me, sibling, xn, yn, dg = plan(src, dst, sems)
        for a in range(n):
            copy(a, 3, dg, me, half=0).wait_recv()
            copy(a, 4, dg, me, half=1).wait_recv()
        for cp in diagonal(src, dst, sems):
            cp.start()

    def finish(src, dst, sems):
        copy, me, sibling, xn, yn, dg = plan(src, dst, sems)
        other = lambda dev: (dev[0], dev[1], 1 - dev[2])
        for a in range(n):
            copy(a, 0, sibling, me).wait_recv()
            for k, dev in ((5, xn), (6, yn), (7, dg)):
                copy(a, k, other(dev), me).wait_recv()
        from_x, from_y = relayed(src, dst, sems)
        for cp in own(src, dst, sems) + from_x + from_y + diagonal(src, dst, sems):
            cp.wait_send()
        for cp in local(src, dst, sems):
            cp.wait()

    return _Job(shards, [_sds(spec[0], s.dtype) for spec, s in zip(specs, shards)],
                [pltpu.SemaphoreType.DMA((n, 8)), pltpu.SemaphoreType.DMA((n, 8)), pltpu.SemaphoreType.DMA((n,))],
                [(0.0, start), (relay_at, relay), (forward_at, forward), (1.0, finish)])


def _sibling_job(grads):
    n = len(grads)

    def copies(src, dst, sems):
        x, y, c = _position()
        return [pltpu.make_async_remote_copy(src_ref=src[a].at[2 * q + 1 - c], dst_ref=dst[a].at[q],
                                             send_sem=sems[0].at[a, q], recv_sem=sems[1].at[a, q],
                                             device_id=(x, y, 1 - c), device_id_type=MESH)
                for a in range(n) for q in range(NCHIP)]

    return _exchange_job(grads, NCHIP, copies)


def _chips_job(chip_sums):
    n = len(chip_sums)

    def copies(src, dst, sems):
        x, y, c = _position()
        chips = [(1 - x, y), (x, 1 - y), (1 - x, 1 - y)]
        return [pltpu.make_async_remote_copy(src_ref=src[a].at[2 * chip[0] + chip[1]], dst_ref=dst[a].at[j],
                                             send_sem=sems[0].at[a, j], recv_sem=sems[1].at[a, j],
                                             device_id=(*chip, c), device_id_type=MESH)
                for j, chip in enumerate(chips) for a in range(n)]

    return _exchange_job(chip_sums, 3, copies)


def _exchange_job(arrays, slots, copies):
    n = len(arrays)

    def start(src, dst, sems):
        for cp in copies(src, dst, sems):
            cp.start()

    def finish(src, dst, sems):
        cps = copies(src, dst, sems)
        for cp in cps:
            cp.wait_recv()
        for cp in cps:
            cp.wait_send()

    return _Job(arrays, [_sds((slots,) + a.shape[1:], a.dtype) for a in arrays],
                [pltpu.SemaphoreType.DMA((n, slots)), pltpu.SemaphoreType.DMA((n, slots))],
                [(0.0, start), (1.0, finish)])


def _row_block(r):
    if r <= 512:
        return r
    for rb in range(512, 15, -16):
        if r % rb == 0:
            return rb
    return r


def _chip_sums(grads, from_sibling, core, name):
    n = len(grads)

    def body(core_ref, *refs):
        del core_ref
        for a in range(n):
            refs[2 * n + a][...] = (refs[a][...].astype(F32) + refs[n + a][...].astype(F32)).astype(BF16)

    block = lambda g, index: pl.BlockSpec((None,) + g.shape[1:], index)
    grid_spec = pltpu.PrefetchScalarGridSpec(
        num_scalar_prefetch=1, grid=(NCHIP,),
        in_specs=[block(g, lambda q, core: (2 * q + core[0], 0, 0)) for g in grads]
        + [block(g, lambda q, core: (q, 0, 0)) for g in grads],
        out_specs=[block(g, lambda q, core: (q, 0, 0)) for g in grads])
    return pl.pallas_call(body, name=name, grid_spec=grid_spec,
                          out_shape=[_sds((NCHIP,) + g.shape[1:], BF16) for g in grads],
                          compiler_params=_params("parallel"))(core, *grads, *from_sibling)


def _adamw_math(w, g, m, v):
    m2 = ADAM_B1 * m + (1.0 - ADAM_B1) * g
    v2 = ADAM_B2 * v + (1.0 - ADAM_B2) * jnp.square(g)
    m_hat = m2 / (1.0 - ADAM_B1 ** ADAM_STEP)
    v_hat = v2 / (1.0 - ADAM_B2 ** ADAM_STEP)
    delta = -ADAM_LR * (m_hat / (jnp.sqrt(v_hat) + ADAM_EPS) + ADAM_WD * w)
    return delta, m2, v2


def _adamw(w, g, m, v, name):
    r, c = w.shape
    rb = _row_block(r)

    def body(w_ref, g_ref, m_ref, v_ref, d_ref, m2_ref, v2_ref):
        d, m2, v2 = _adamw_math(w_ref[...], g_ref[...], m_ref[...], v_ref[...])
        d_ref[...] = d
        m2_ref[...] = m2
        v2_ref[...] = v2

    blk = pl.BlockSpec((rb, c), lambda i: (i, 0))
    return pl.pallas_call(body, name=name, grid=(r // rb,), in_specs=[blk] * 4, out_specs=[blk] * 3,
                          out_shape=[_sds((r, c), F32)] * 3, compiler_params=_params("parallel"))(w, g, m, v)


def _adamw_group(items, name):
    n = len(items)

    def body(*refs):
        ins, outs = refs[:4 * n], refs[4 * n:]
        for a in range(n):
            w_ref, g_ref, m_ref, v_ref = ins[4 * a:4 * a + 4]
            for ref, val in zip(outs[3 * a:3 * a + 3], _adamw_math(w_ref[...], g_ref[...], m_ref[...], v_ref[...])):
                ref[...] = val

    outs = pl.pallas_call(body, name=name, in_specs=[_vmem()] * (4 * n), out_specs=[_vmem()] * (3 * n),
                          out_shape=[_sds(q[0].shape, F32) for q in items for _ in range(3)])(
        *[a for q in items for a in q])
    return [outs[3 * a:3 * a + 3] for a in range(n)]


def _reduce_adamw(grad, from_sibling, from_chips, sel, w, m, v, name, transposed=False):
    r, c = w.shape[::-1] if transposed else w.shape
    cp = grad.shape[2]
    rb = _row_block(r)

    def body(sel_ref, g_ref, s_ref, c0_ref, c1_ref, c2_ref, w_ref, m_ref, v_ref, go_ref, d_ref, m2_ref, v2_ref):
        del sel_ref
        g = g_ref[...].astype(F32) + s_ref[...].astype(F32)
        g = g + c0_ref[...].astype(F32)
        g = g + c1_ref[...].astype(F32)
        g = g + c2_ref[...].astype(F32)
        g = g.T[0:c, :] if transposed else g[:, 0:c]
        d, m2, v2 = _adamw_math(w_ref[...], g, m_ref[...], v_ref[...])
        go_ref[...] = g
        d_ref[...] = d
        m2_ref[...] = m2
        v2_ref[...] = v2

    blk = pl.BlockSpec((c, rb), lambda i, sel: (0, i)) if transposed else pl.BlockSpec((rb, c), lambda i, sel: (i, 0))
    grid_spec = pltpu.PrefetchScalarGridSpec(
        num_scalar_prefetch=1, grid=(r // rb,),
        in_specs=[pl.BlockSpec((None, rb, cp), lambda i, sel: (sel[0], i, 0)),
                  pl.BlockSpec((None, rb, cp), lambda i, sel: (sel[1], i, 0)),
                  pl.BlockSpec((None, rb, cp), lambda i, sel: (0, i, 0)),
                  pl.BlockSpec((None, rb, cp), lambda i, sel: (1, i, 0)),
                  pl.BlockSpec((None, rb, cp), lambda i, sel: (2, i, 0)),
                  blk, blk, blk],
        out_specs=[blk] * 4)
    return pl.pallas_call(body, name=name, grid_spec=grid_spec, out_shape=[_sds(w.shape, F32)] * 4,
                          compiler_params=_params("parallel"))(sel, grad, from_sibling, from_chips, from_chips,
                                                               from_chips, w, m, v)


def _reduce_adamw_group(items, sel, name):
    n = len(items)

    def body(sel_ref, *refs):
        del sel_ref
        ins, outs = refs[:8 * n], refs[8 * n:]
        for a in range(n):
            g_ref, s_ref, c0_ref, c1_ref, c2_ref, w_ref, m_ref, v_ref = ins[8 * a:8 * a + 8]
            g = g_ref[...].astype(F32) + s_ref[...].astype(F32)
            g = g + c0_ref[...].astype(F32)
            g = g + c1_ref[...].astype(F32)
            g = g + c2_ref[...].astype(F32)
            d, m2, v2 = _adamw_math(w_ref[...], g, m_ref[...], v_ref[...])
            for ref, val in zip(outs[4 * a:4 * a + 4], (g, d, m2, v2)):
                ref[...] = val

    in_specs, out_specs, out_shape, operands = [], [], [], []
    for grad, from_sibling, from_chips, w, m, v in items:
        slot = lambda index, shape=grad.shape[1:]: pl.BlockSpec((None,) + shape, index)
        full = pl.BlockSpec(w.shape, lambda i, sel: (0, 0))
        in_specs += [slot(lambda i, sel: (sel[0], 0, 0)), slot(lambda i, sel: (sel[1], 0, 0)),
                     slot(lambda i, sel: (0, 0, 0)), slot(lambda i, sel: (1, 0, 0)), slot(lambda i, sel: (2, 0, 0)),
                     full, full, full]
        out_specs += [full] * 4
        out_shape += [_sds(w.shape, F32)] * 4
        operands += [grad, from_sibling, from_chips, from_chips, from_chips, w, m, v]
    grid_spec = pltpu.PrefetchScalarGridSpec(num_scalar_prefetch=1, grid=(1,), in_specs=in_specs, out_specs=out_specs)
    outs = pl.pallas_call(body, name=name, grid_spec=grid_spec, out_shape=out_shape,
                          compiler_params=_params("arbitrary"))(sel, *operands)
    return [outs[4 * a:4 * a + 4] for a in range(n)]


def _wada_grad(c_all, dmod_piece):
    d = c_all.shape[1]
    n = dmod_piece.shape[1]

    def body(c_ref, dm_ref, o_ref):
        o_ref[...] = _dot_tn(c_ref[...], dm_ref[...])

    return pl.pallas_call(body, name="ada_wgrad", out_shape=_sds((d, n), F32),
                          in_specs=[_vmem()] * 2, out_specs=_vmem())(c_all, dmod_piece)


def _column_chunks(width):
    for n in (4, 2):
        if width % (n * LANES) == 0:
            return n
    return 1


def _conv_taps(ref, col):
    return ref[0:1, col], ref[1:2, col], ref[2:3, col]


def _gelu_parts(u):
    u2 = u * u
    th = jnp.tanh((GELU_C0 * u) * (1.0 + GELU_C1 * u2))
    dcdf = (1.0 - th * th) * ((0.5 * GELU_C0) * (1.0 + (3.0 * GELU_C1) * u2))
    return 0.5 * (1.0 + th), dcdf


def _conv3_bwd(dv, carry, col, taps, x):
    halo = _halo_bottom(dv[:16, :], carry[:, col])
    carry[:, col] = dv[:16, :]
    d1 = _shift_up(dv, halo, 1)
    d2 = _shift_up(dv, halo, 2)
    w0, w1, w2 = taps
    dx = w2 * dv
    dx = dx + w1 * d1
    dx = dx + w0 * d2
    return dx, (_colsum(d2 * x), _colsum(d1 * x), _colsum(dv * x))


def _prenorm(x, vp_ref):
    r = lax.rsqrt(_rowmean(x * x) + EPS)
    nh = x * r
    return (nh * vp_ref[0:1, :]) * vp_ref[1:2, :] + vp_ref[2:3, :], r, nh


def _prenorm_bwd(dh, r, nh, vp_ref, red_ref):
    g, sc1 = vp_ref[0:1, :], vp_ref[1:2, :]
    red_ref[0:1, :] += _colsum(dh)
    red_ref[1:2, :] += _colsum(dh * (nh * g))
    red_ref[2:3, :] += _colsum(dh * nh * sc1)
    dnh = dh * g * sc1
    return r * (dnh - nh * _rowmean(dnh * nh))


def _postnorm_bwd(dres, z, gate, gpost, red_ref):
    r = lax.rsqrt(_rowmean(z * z) + EPS)
    nh = z * r
    dn = dres * gate
    red_ref[0:1, :] += _colsum(dn * nh)
    red_ref[1:2, :] += _colsum(dres * (nh * gpost))
    dnh = dn * gpost
    return r * (dnh - nh * _rowmean(dnh * nh))


def _mixer_block_fwd(x, vec_pre, vec, w_in, w_pool, w_bout, w_o, job):
    t, d = x.shape
    tm = GROUP
    gw = d // len(POOL_WINDOWS)
    pool_rows = 8 * (POOL_WINDOWS[-1] - 1)

    def body(x_ref, vp_ref, vec_ref, win_ref, wp_ref, wb_ref, wo_ref,
             hb_ref, p5_ref, qm_ref, cv_ref, yar_ref, yb_ref, o_ref, x1_ref, mbuf, ucarry, pcarry):
        i = pl.program_id(0)

        @pl.when(i == 0)
        def _():
            ucarry[...] = jnp.zeros_like(ucarry)
            pcarry[...] = jnp.zeros_like(pcarry)

        xp = _interleave(x_ref[...])
        hb = _prenorm(xp, vp_ref)[0].astype(BF16)
        hb_ref[...] = hb
        proj = lambda k: _dot(hb, win_ref[:, k * d:(k + 1) * d])

        za = proj(4)
        p5_ref[:, 3 * d:4 * d] = za.astype(BF16)
        sa = jax.nn.sigmoid(za)
        u_pool = proj(0)
        for g, window in enumerate(POOL_WINDOWS):
            cols = slice(g * gw, (g + 1) * gw)
            rows = 8 * (window - 1)
            u = u_pool[:, cols]
            halo = _halo_top(u[tm - rows:, :], ucarry[pool_rows - rows:, cols])
            s, shift = jnp.concatenate([halo, u], axis=0), 1
            while shift < window:
                s = s[8 * shift:, :] + s[:s.shape[0] - 8 * shift, :]
                shift *= 2
            pgb = (s * _inv_count(i * tm, window) - u).astype(BF16)
            qm_ref[:, 2 * d + g * gw:2 * d + (g + 1) * gw] = pgb
            yar = _dot(pgb, wp_ref[g])
            yar_ref[:, cols] = yar
            mbuf[:, cols] = sa[:, cols] * (yar * vec_ref[2:3, cols])
        ucarry[...] = u_pool[tm - pool_rows:, :]

        ux = proj(1)
        uc = proj(3)
        p5_ref[:, 0:d] = ux.astype(BF16)
        p5_ref[:, 2 * d:3 * d] = uc.astype(BF16)
        p = uc * ux
        halo = _halo_top(p[tm - 16:, :], pcarry[...])
        pcarry[...] = p[tm - 16:, :]
        cv = vec_ref[3:4, :] + vec_ref[4:5, :] * _shift_down(p, halo, 2)
        cv = cv + vec_ref[5:6, :] * _shift_down(p, halo, 1)
        cv = cv + vec_ref[6:7, :] * p
        cv_ref[...] = cv
        ub = proj(2)
        p5_ref[:, d:2 * d] = ub.astype(BF16)
        qb = (ub * cv).astype(BF16)
        qm_ref[:, 0:d] = qb
        yb = _dot(qb, wb_ref[...])
        yb_ref[...] = yb

        zb = proj(5)
        p5_ref[:, 4 * d:5 * d] = zb.astype(BF16)
        mb = (mbuf[...] + jax.nn.sigmoid(zb) * yb).astype(BF16)
        qm_ref[:, d:2 * d] = mb
        o = _dot(mb, wo_ref[...])
        o_ref[...] = o
        r2 = lax.rsqrt(_rowmean(o * o) + EPS)
        x1_ref[...] = xp + vec_ref[0:1, :] * ((o * r2) * vec_ref[1:2, :])

    row = lambda n: pl.BlockSpec((tm, n), lambda i: (i, 0))
    widths = [d, 5 * d, 3 * d, d, d, d, d, d]
    return _call(
        body, "mixer_block_fwd", (t // tm,), [row(d)] + [_vmem()] * 6, [row(n) for n in widths],
        [_sds((t, n), BF16) for n in widths[:3]] + [_sds((t, n), F32) for n in widths[3:]],
        [pltpu.VMEM((tm, d), F32), pltpu.VMEM((pool_rows, d), F32), pltpu.VMEM((16, d), F32)],
        _params("arbitrary"), (x, vec_pre, vec, w_in, w_pool, w_bout, w_o), job)


def _ffn_block_fwd(x1, target, vec_pre, vec, fcv, w_up, w_down):
    t, d = x1.shape
    tm = GROUP
    fp = w_down.shape[0]
    nch = _column_chunks(fp)
    cw = fp // nch

    def body(x1_ref, tg_ref, vp_ref, vec_ref, fcv_ref, wu_ref, wd_ref,
             hb_ref, upb_ref, upreb_ref, a_ref, ffb_ref, dy_ref, loss_ref, carry):
        i = pl.program_id(0)

        @pl.when(i == 0)
        def _():
            carry[...] = jnp.zeros_like(carry)
            loss_ref[...] = jnp.zeros_like(loss_ref)

        x1 = x1_ref[...]
        hb = _prenorm(x1, vp_ref)[0].astype(BF16)
        hb_ref[...] = hb

        cols = [(slice(j * cw, (j + 1) * cw), slice(fp + j * cw, fp + (j + 1) * cw)) for j in range(nch)]
        up_gate = _dot(hb, wu_ref[:, 0:fp])
        up_val = _dot(hb, wu_ref[:, fp:2 * fp])

        def conv(v, col):
            halo = _halo_top(v[tm - 16:, :], carry[:, col])
            carry[:, col] = v[tm - 16:, :]
            w0, w1, w2 = _conv_taps(fcv_ref, col)
            y = fcv_ref[3:4, col] + w0 * _shift_down(v, halo, 2)
            y = y + w1 * _shift_down(v, halo, 1)
            y = y + w2 * v
            upb_ref[:, col] = y.astype(BF16)
            upreb_ref[:, col] = v.astype(BF16)
            return y

        ff = None
        for j in range(nch):
            gc, vc = cols[j]
            gate = conv(up_gate[:, gc], gc)
            val = conv(up_val[:, gc], vc)
            ab = ((gate * _gelu_parts(gate)[0]) * val).astype(BF16)
            a_ref[:, gc] = ab
            part = _dot(ab, wd_ref[gc, :])
            ff = part if ff is None else ff + part
        ffb_ref[...] = ff.astype(BF16)
        r4 = lax.rsqrt(_rowmean(ff * ff) + EPS)
        y = x1 + vec_ref[0:1, :] * ((ff * r4) * vec_ref[1:2, :])
        e = y - _interleave(tg_ref[...])
        dy_ref[...] = e * (1.0 / d)
        loss_ref[...] += jnp.sum(_rowmean(e * e))

    row = lambda n: pl.BlockSpec((tm, n), lambda i: (i, 0))
    return pl.pallas_call(
        body, name="ffn_block_fwd", grid=(t // tm,),
        in_specs=[row(d), row(d)] + [_vmem()] * 5,
        out_specs=[row(d), row(2 * fp), row(2 * fp), row(fp), row(d), row(d), pl.BlockSpec((8, LANES), lambda i: (0, 0))],
        out_shape=[_sds((t, d), BF16), _sds((t, 2 * fp), BF16), _sds((t, 2 * fp), BF16), _sds((t, fp), BF16),
                   _sds((t, d), BF16), _sds((t, d), F32), _sds((8, LANES), F32)],
        scratch_shapes=[pltpu.VMEM((16, 2 * fp), F32)],
        compiler_params=_params("arbitrary"),
    )(x1, target, vec_pre, vec, fcv, w_up, w_down)


def _ffn_block_bwd(dy, ffb, x1, upb, upreb, vec_pre, vec, fcv, w_up, w_down):
    t, d = dy.shape
    tm = GROUP
    fp = w_down.shape[0]
    nch = _column_chunks(fp)
    cw = fp // nch
    nt = t // tm

    def body(dy_ref, ff_ref, x1_ref, upb_ref, upreb_ref, vp_ref, vec_ref, fcv_ref, wu_ref, wd_ref,
             dff_ref, dup_ref, dx1_ref, red_ref, cred_ref, pred_ref, carry):
        @pl.when(pl.program_id(0) == 0)
        def _():
            carry[...] = jnp.zeros_like(carry)
            red_ref[...] = jnp.zeros_like(red_ref)
            cred_ref[...] = jnp.zeros_like(cred_ref)
            pred_ref[...] = jnp.zeros_like(pred_ref)

        dy_v = dy_ref[...]
        dffb = _postnorm_bwd(dy_v, ff_ref[...].astype(F32), vec_ref[0:1, :], vec_ref[1:2, :], red_ref).astype(BF16)
        dff_ref[...] = dffb

        def conv_bwd(dv, col):
            dx, (t0, t1, t2) = _conv3_bwd(dv, carry, col, _conv_taps(fcv_ref, col), upreb_ref[:, col].astype(F32))
            cred_ref[0:1, col] += t0
            cred_ref[1:2, col] += t1
            cred_ref[2:3, col] += t2
            cred_ref[3:4, col] += _colsum(dv)
            dxb = dx.astype(BF16)
            dup_ref[:, col] = dxb
            return _dot_nt(dxb, wu_ref[:, col])

        dh = None
        for j in range(nch):
            gc = slice(j * cw, (j + 1) * cw)
            vc = slice(fp + j * cw, fp + (j + 1) * cw)
            da = _dot_nt(dffb, wd_ref[gc, :])
            gate = upb_ref[:, gc].astype(F32)
            val = upb_ref[:, vc].astype(F32)
            cdf, dcdf = _gelu_parts(gate)
            part = conv_bwd(da * val * (cdf + gate * dcdf), gc) + conv_bwd(da * (gate * cdf), vc)
            dh = part if dh is None else dh + part

        _, r, nh = _prenorm(x1_ref[...], vp_ref)
        dx1_ref[...] = dy_v + _prenorm_bwd(dh, r, nh, vp_ref, pred_ref)

    rev = lambda n: pl.BlockSpec((tm, n), lambda i: (nt - 1 - i, 0))
    fixed = lambda n: pl.BlockSpec((8, n), lambda i: (0, 0))
    return pl.pallas_call(
        body, name="ffn_block_bwd", grid=(nt,),
        in_specs=[rev(d), rev(d), rev(d), rev(2 * fp), rev(2 * fp)] + [_vmem()] * 5,
        out_specs=[rev(d), rev(2 * fp), rev(d), fixed(d), fixed(2 * fp), fixed(d)],
        out_shape=[_sds((t, d), BF16), _sds((t, 2 * fp), BF16), _sds((t, d), F32), _sds((8, d), F32),
                   _sds((8, 2 * fp), F32), _sds((8, d), F32)],
        scratch_shapes=[pltpu.VMEM((16, 2 * fp), F32)],
        compiler_params=_params("arbitrary"),
    )(dy, ffb, x1, upb, upreb, vec_pre, vec, fcv, w_up, w_down)


def _mixer_block_bwd(dx1, o, yar, yb, cv, p5b, x, vec_pre, vec, w_in, w_pool, w_bout, w_o, job):
    t, d = dx1.shape
    tm = GROUP
    gw = d // len(POOL_WINDOWS)
    nt = t // tm
    pool_rows = 8 * (POOL_WINDOWS[-1] - 1)

    def body(dx1_ref, o_ref, yar_ref, yb_ref, cv_ref, p5_ref, x_ref, vp_ref, vec_ref, win_ref, wp_ref, wb_ref, wo_ref,
             dqm_ref, dp_ref, gx_ref, red_ref, pred_ref, dpgcarry, dcvcarry):
        i = pl.program_id(0)
        tix = nt - 1 - i

        @pl.when(i == 0)
        def _():
            red_ref[...] = jnp.zeros_like(red_ref)
            pred_ref[...] = jnp.zeros_like(pred_ref)
            dpgcarry[...] = jnp.zeros_like(dpgcarry)
            dcvcarry[...] = jnp.zeros_like(dcvcarry)

        pscale = vec_ref[2:3, :]
        dx1_v = dx1_ref[...]
        dob = _postnorm_bwd(dx1_v, o_ref[...].astype(F32), vec_ref[0:1, :], vec_ref[1:2, :], red_ref).astype(BF16)
        dqm_ref[:, d:2 * d] = dob
        dm = _dot_nt(dob, wo_ref[...])

        def dproj(cols, value):
            vb = value.astype(BF16)
            dp_ref[:, cols] = vb
            return _dot_nt(vb, win_ref[:, cols])

        sa = jax.nn.sigmoid(p5_ref[:, 3 * d:4 * d].astype(F32))
        yar = yar_ref[...].astype(F32)
        dya = dm * sa
        dh = dproj(slice(4 * d, 5 * d), dm * (yar * pscale) * sa * (1.0 - sa))
        red_ref[2:3, :] += _colsum(dya * yar)
        dyarb = (dya * pscale).astype(BF16)
        dqm_ref[:, 2 * d:3 * d] = dyarb
        sb = jax.nn.sigmoid(p5_ref[:, 4 * d:5 * d].astype(F32))
        dybb = (dm * sb).astype(BF16)
        dqm_ref[:, 0:d] = dybb
        dh = dh + dproj(slice(5 * d, 6 * d), dm * yb_ref[...].astype(F32) * sb * (1.0 - sb))

        for g, window in enumerate(POOL_WINDOWS):
            cols = slice(g * gw, (g + 1) * gw)
            rows = 8 * (window - 1)
            dpg = _dot_nt(dyarb[:, cols], wp_ref[g])
            dpgs = dpg * _inv_count(tix * tm, window)
            halo = _halo_bottom(dpgs[:rows, :], dpgcarry[:rows, cols])
            dpgcarry[:, cols] = dpgs[:pool_rows, :]
            s, shift = jnp.concatenate([dpgs, halo], axis=0), 1
            while shift < window:
                s = s[:s.shape[0] - 8 * shift, :] + s[8 * shift:, :]
                shift *= 2
            dh = dh + dproj(cols, s - dpg)

        dq = _dot_nt(dybb, wb_ref[...])
        ux = p5_ref[:, 0:d].astype(F32)
        uc = p5_ref[:, 2 * d:3 * d].astype(F32)
        dh = dh + dproj(slice(2 * d, 3 * d), dq * cv_ref[...].astype(F32))
        dcv = dq * p5_ref[:, d:2 * d].astype(F32)
        taps = (vec_ref[4:5, :], vec_ref[5:6, :], vec_ref[6:7, :])
        dpv, (t0, t1, t2) = _conv3_bwd(dcv, dcvcarry, slice(0, d), taps, uc * ux)
        red_ref[3:4, :] += _colsum(dcv)
        red_ref[4:5, :] += t0
        red_ref[5:6, :] += t1
        red_ref[6:7, :] += t2
        dh = dh + dproj(slice(d, 2 * d), dpv * uc)
        dh = dh + dproj(slice(3 * d, 4 * d), dpv * ux)

        _, r, nh = _prenorm(_interleave(x_ref[...]), vp_ref)
        gx_ref[...] = _deinterleave(dx1_v + _prenorm_bwd(dh, r, nh, vp_ref, pred_ref))

    rev = lambda n: pl.BlockSpec((tm, n), lambda i: (nt - 1 - i, 0))
    return _call(
        body, "mixer_block_bwd", (nt,), [rev(d)] * 5 + [rev(5 * d), rev(d)] + [_vmem()] * 6,
        [rev(3 * d), rev(6 * d), rev(d), pl.BlockSpec((16, d), lambda i: (0, 0)),
         pl.BlockSpec((8, d), lambda i: (0, 0))],
        [_sds((t, 3 * d), BF16), _sds((t, 6 * d), BF16), _sds((t, d), F32), _sds((16, d), F32), _sds((8, d), F32)],
        [pltpu.VMEM((pool_rows, d), F32), pltpu.VMEM((16, d), F32)],
        _params("arbitrary"), (dx1, o, yar, yb, cv, p5b, x, vec_pre, vec, w_in, w_pool, w_bout, w_o), job)


def _matmul_tn(a, b, bm, bn, tk, by_col_block, name, job=None):
    t, m = a.shape
    n = b.shape[1]
    nk = t // tk
    parts = int(by_col_block)
    piece = bn // max(parts, 1)
    wide = _round_up(piece, LANES)

    def body(a_ref, b_ref, o_ref, acc_ref):
        k = pl.program_id(2)

        @pl.when(k == 0)
        def _():
            acc_ref[...] = jnp.zeros_like(acc_ref)

        acc_ref[...] += _dot_tn(a_ref[...], b_ref[...])

        @pl.when(k == nk - 1)
        def _():
            if parts:
                acc = acc_ref[...]
                for p in range(parts):
                    if wide > piece:
                        o_ref[p] = jnp.zeros((bm, wide), o_ref.dtype)
                    o_ref[p, :, 0:piece] = acc[:, p * piece:(p + 1) * piece].astype(o_ref.dtype)
            else:
                o_ref[...] = acc_ref[...].astype(o_ref.dtype)

    if by_col_block:
        out_shape = _sds((parts * n // bn, m, wide), BF16)
        out_spec = pl.BlockSpec((parts, bm, wide), lambda i, j, k: (j, i, 0))
    else:
        out_shape = _sds((m, n), BF16)
        out_spec = pl.BlockSpec((bm, bn), lambda i, j, k: (i, j))
    out = _call(body, name, (m // bm, n // bn, nk),
                [pl.BlockSpec((tk, bm), lambda i, j, k: (k, i)), pl.BlockSpec((tk, bn), lambda i, j, k: (k, j))],
                [out_spec], [out_shape], [pltpu.VMEM((bm, bn), F32)],
                _params("arbitrary", "arbitrary", "arbitrary"), (a, b), job)
    return out if job is not None else out[0]


def _side_by_side(blocks, name):
    n, r, c = blocks.shape
    rb = _row_block(r) // 2

    def body(in_ref, o_ref):
        for j in range(n):
            o_ref[:, j * c:(j + 1) * c] = in_ref[j]

    return pl.pallas_call(
        body, name=name, grid=(r // rb,), in_specs=[pl.BlockSpec((n, rb, c), lambda i: (0, i, 0))],
        out_specs=pl.BlockSpec((rb, n * c), lambda i: (i, 0)), out_shape=_sds((r, n * c), blocks.dtype),
        compiler_params=_params("parallel"))(blocks)


def _matmul_tn_groups(a, b, groups, tk, name, job=None):
    t, m = a.shape
    w = m // groups
    nk = t // tk

    def body(a_ref, b_ref, o_ref, acc_ref):
        k = pl.program_id(1)

        @pl.when(k == 0)
        def _():
            acc_ref[...] = jnp.zeros_like(acc_ref)

        acc_ref[...] += _dot_tn(a_ref[...], b_ref[...])

        @pl.when(k == nk - 1)
        def _():
            o_ref[...] = acc_ref[...].astype(o_ref.dtype)

    blk = pl.BlockSpec((tk, w), lambda g, k: (k, g))
    out = _call(body, name, (groups, nk), [blk, blk], [pl.BlockSpec((None, w, w), lambda g, k: (g, 0, 0))],
                [_sds((groups, w, w), BF16)], [pltpu.VMEM((w, w), F32)], _params("arbitrary", "arbitrary"), (a, b), job)
    return out if job is not None else out[0]


def _round_up(n, k):
    return (n + k - 1) // k * k


def _rows8(rows, width):
    n = _round_up(len(rows), 8)
    rows = list(rows) + [jnp.zeros((1, width), F32)] * (n - len(rows))
    return jnp.concatenate(rows, axis=0)


def kernel(x, c, g_pre_mix, g_post_mix, g_pre_ffn, g_post_ffn, w_ada, b_ada, w_in, w_pool, pool_scale, conv_w, conv_b, w_bout, w_o, w_up, ffn_conv_w, ffn_conv_b, w_down, loss_target, m_g_pre_mix, m_g_post_mix, m_g_pre_ffn, m_g_post_ffn, m_w_ada, m_b_ada, m_w_in, m_w_pool, m_pool_scale, m_conv_w, m_conv_b, m_w_bout, m_w_o, m_w_up, m_ffn_conv_w, m_ffn_conv_b, m_w_down, v_g_pre_mix, v_g_post_mix, v_g_pre_ffn, v_g_post_ffn, v_w_ada, v_b_ada, v_w_in, v_w_pool, v_pool_scale, v_conv_w, v_conv_b, v_w_bout, v_w_o, v_w_up, v_ffn_conv_w, v_ffn_conv_b, v_w_down):
    t, d = x.shape[1], x.shape[2]
    ngroups = len(POOL_WINDOWS)
    gw = d // ngroups
    ada_n = w_ada.shape[2]
    in_n = w_in.shape[2]
    up_n = w_up.shape[2]
    fp = NDEV * w_down.shape[1]
    assert x.shape[0] == 1 and t % GROUP == 0, "one sequence per device, a whole number of token groups"
    assert gw % LANES == 0 and in_n % LANES == 0 and ada_n % LANES == 0 and NDEV * in_n == 6 * d
    assert NDEV * up_n == 2 * fp and fp % (2 * LANES) == 0, "gate and value halves cut into lane-aligned chunks"

    xi, yi, ci = _position()
    me = _linear(xi, yi, ci)
    chip = 2 * xi + yi
    core = jnp.reshape(ci, (1,)).astype(jnp.int32)
    sel = jnp.stack([2 * chip + ci, chip]).astype(jnp.int32)

    x2 = x.reshape(t, d)
    target = loss_target.reshape(t, d)

    cw_n = conv_w.shape[2]
    pack = jnp.concatenate([c.reshape(1, d), conv_w[0].reshape(1, 3 * cw_n), ffn_conv_w[0].reshape(1, 3 * up_n)], axis=1)
    pack = jnp.pad(pack, ((0, 0), (0, _round_up(pack.shape[1], LANES) - pack.shape[1])))
    b_piece = lax.dynamic_slice_in_dim(b_ada, me * ada_n, ada_n, axis=1)
    mixer_weights = _allgather_job(
        [w_in[0].astype(BF16), w_bout[0].astype(BF16), w_o[0].astype(BF16), w_pool[0].astype(BF16)],
        ["cols", "rows", "rows", "mid"], 0.5, 0.75)
    learned = _rows8([g_pre_mix, g_post_mix, g_pre_ffn, g_post_ffn, pool_scale, conv_b], d)
    gathered, vec_pre_mix, vec_mix, vec_pre_ffn, vec_ffn, fcv, w_in_f, g_bout, g_o, w_pool_f = (
        _gather_weights_and_modulation(mixer_weights, pack, w_ada[0], b_piece, learned, ffn_conv_b, d, cw_n, up_n))
    w_bout_f = g_bout.reshape(d, d)
    w_o_f = g_o.reshape(d, d)
    c16 = jnp.pad(gathered[:, :d], ((0, 8), (0, 0))).astype(BF16)

    ffn_weights = _allgather_job([w_up[0].astype(BF16), w_down[0].astype(BF16)], ["rows", "rows"], 0.5, 0.8)
    h1b, p5b, qmb, cv, yar, yb, o, x1, g_up, g_down = _mixer_block_fwd(
        x2, vec_pre_mix, vec_mix, w_in_f, w_pool_f, w_bout_f, w_o_f, ffn_weights)
    w_up_f = _side_by_side(g_up, "w_up_side_by_side")
    w_down_f = g_down.reshape(fp, d)
    h2b, upb, upreb, ab, ffb, dy, loss_part = _ffn_block_fwd(x1, target, vec_pre_ffn, vec_ffn, fcv, w_up_f, w_down_f)

    tk, tk_wide = min(4096, t), min(2048, t)
    chip_sum = lambda gs, ss, name: _chip_sums(gs, ss, core, name)
    dffb, dupre, dx1, red_ffn, red_fconv, red_pre_ffn = _ffn_block_bwd(
        dy, ffb, x1, upb, upreb, vec_pre_ffn, vec_ffn, fcv, w_up_f, w_down_f)
    chunk = fp // _column_chunks(fp)
    gr_up = _matmul_tn(h2b, dupre, d // 2, chunk, tk, chunk // up_n, "wgrad_up")
    gr_down, sib_up = _matmul_tn(ab, dffb, chunk, d, tk_wide, False, "wgrad_down", _sibling_job([gr_up]))
    gr_down = gr_down.reshape(NDEV, fp // NDEV, d)
    sib_ffn = [sib_up] + list(_run_job(_sibling_job([gr_down]), "rs_sibling_down"))
    dqmb, dproj, grad_x, red_mix, red_pre_mix = _mixer_block_bwd(
        dx1, o, yar, yb, cv, p5b, x2, vec_pre_mix, vec_mix, w_in_f, w_pool_f, w_bout_f, w_o_f, None)
    gr_in, fc_up, fc_down = _matmul_tn(h1b, dproj, d, in_n, tk, True, "wgrad_in",
                                       _chips_job(chip_sum([gr_up, gr_down], sib_ffn, "rs_chip_sum_ffn")))
    sib_in = _run_job(_sibling_job([gr_in]), "rs_sibling_in")
    gr_qmp, fc_in = _matmul_tn_groups(qmb, dqmb, 3, tk, "wgrad_bout_o_pool",
                                      _chips_job(chip_sum([gr_in], sib_in, "rs_chip_sum_in")))
    gr_bout = gr_qmp[0].reshape(NDEV, d // NDEV, d)
    gr_o = gr_qmp[1].reshape(NDEV, d // NDEV, d)
    gr_pool = jnp.stack([gr_qmp[2, g * gw:(g + 1) * gw, g * gw:(g + 1) * gw] for g in range(ngroups)])
    gr_pool = gr_pool.reshape(ngroups, NDEV, gw // NDEV, gw).transpose(1, 0, 2, 3).reshape(NDEV, -1, gw)
    rest = [gr_bout, gr_o, gr_pool]
    sib_rest = _run_job(_sibling_job(rest), "rs_sibling_rest")

    dmod = [red_pre_mix[0:1], red_pre_mix[1:2], red_mix[1:2], red_pre_ffn[0:1], red_pre_ffn[1:2], red_ffn[1:2]]
    small = [red_pre_mix[2:3], red_mix[0:1], red_pre_ffn[2:3], red_ffn[0:1], red_mix[2:3], red_mix[3:4],
             red_mix[4:5], red_mix[5:6], red_mix[6:7]] + dmod
    flat = jnp.concatenate(small + [red_fconv[0:4].reshape(1, 8 * fp), loss_part[0:1, 0:1]], axis=1)
    flat_n = flat.shape[1]
    width = 8 * LANES
    rows = _round_up(-(-flat_n // width), 8)
    flat = jnp.pad(flat, ((0, 0), (0, rows * width - flat_n))).reshape(rows, width)
    gat, tot, *fc_rest = _small_allreduce(flat, "allreduce_small_rs_rest",
                                          _chips_job(chip_sum(rest, sib_rest, "rs_chip_sum_rest")))

    def big(grad, from_sibling, from_chips, w, m, v, name, transposed=False):
        shape = w.shape
        flat = (lambda a: a[0].T) if transposed else (lambda a: a.reshape((-1, shape[-1])))
        outs = _reduce_adamw(grad, from_sibling, from_chips, sel, flat(w), flat(m), flat(v), name, transposed)
        return [(a.T if transposed else a).reshape(shape) for a in outs]

    g_w_up, d_w_up, nm_w_up, nv_w_up = big(gr_up, sib_ffn[0], fc_up, w_up, m_w_up, v_w_up, "adamw_up",
                                           transposed=up_n % LANES != 0)
    g_w_down, d_w_down, nm_w_down, nv_w_down = big(gr_down, sib_ffn[1], fc_down, w_down, m_w_down, v_w_down, "adamw_down")
    g_w_in, d_w_in, nm_w_in, nv_w_in = big(gr_in, sib_in[0], fc_in, w_in, m_w_in, v_w_in, "adamw_in")
    flat2 = lambda a: a.reshape((-1, a.shape[-1]))
    rest_w = [(w_bout, m_w_bout, v_w_bout), (w_o, m_w_o, v_w_o), (w_pool, m_w_pool, v_w_pool)]
    rest_out = _reduce_adamw_group(
        [(g, s, f, flat2(w), flat2(m), flat2(v)) for g, s, f, (w, m, v) in zip(rest, sib_rest, fc_rest, rest_w)],
        sel, "adamw_bout_o_pool")
    (g_w_bout, d_w_bout, nm_w_bout, nv_w_bout), (g_w_o, d_w_o, nm_w_o, nv_w_o), (g_w_pool, d_w_pool, nm_w_pool, nv_w_pool) = [
        [a.reshape(w.shape) for a in outs] for outs, (w, _, _) in zip(rest_out, rest_w)]

    tot = tot.reshape(1, rows * width)
    gat = gat.reshape(NDEV, rows * width)
    take = lambda k: tot[:, k * d:(k + 1) * d]
    g_g_pre_mix, g_g_post_mix, g_g_pre_ffn, g_g_post_ffn, g_pool_scale, g_conv_b = [take(k) for k in range(6)]
    g_conv_w_full = jnp.concatenate([take(6), take(7), take(8)], axis=0)
    g_conv_w = lax.dynamic_slice_in_dim(g_conv_w_full, me * cw_n, cw_n, axis=1)
    g_b_ada = tot[:, 9 * d:15 * d]
    dmod_all = gat[:, 9 * d:15 * d]
    fconv_tot = tot[:, 15 * d:15 * d + 8 * fp].reshape(4, 2 * fp)
    loss = 0.5 * tot[0, 15 * d + 8 * fp]
    g_ffn_conv_b = fconv_tot[3:4]
    g_ffn_conv_w = lax.dynamic_slice_in_dim(fconv_tot[0:3], me * up_n, up_n, axis=1)
    dmod_piece = lax.dynamic_slice_in_dim(dmod_all, me * ada_n, ada_n, axis=1)
    g_w_ada = _wada_grad(c16, jnp.pad(dmod_piece, ((0, 8), (0, 0))).astype(BF16))

    names_small = [(g_pre_mix, g_g_pre_mix, m_g_pre_mix, v_g_pre_mix), (g_post_mix, g_g_post_mix, m_g_post_mix, v_g_post_mix),
                   (g_pre_ffn, g_g_pre_ffn, m_g_pre_ffn, v_g_pre_ffn), (g_post_ffn, g_g_post_ffn, m_g_post_ffn, v_g_post_ffn),
                   (b_ada, g_b_ada, m_b_ada, v_b_ada), (pool_scale, g_pool_scale, m_pool_scale, v_pool_scale),
                   (conv_w, g_conv_w, m_conv_w, v_conv_w), (conv_b, g_conv_b, m_conv_b, v_conv_b),
                   (ffn_conv_w, g_ffn_conv_w, m_ffn_conv_w, v_ffn_conv_w), (ffn_conv_b, g_ffn_conv_b, m_ffn_conv_b, v_ffn_conv_b)]
    small_out = _adamw_group([[a.reshape((-1, a.shape[-1])) for a in q] for q in names_small], "adamw_small")
    unpack_small = lambda k: [outs[k].reshape(q[0].shape) for outs, q in zip(small_out, names_small)]
    (d_g_pre_mix, d_g_post_mix, d_g_pre_ffn, d_g_post_ffn, d_b_ada, d_pool_scale, d_conv_w, d_conv_b,
     d_ffn_conv_w, d_ffn_conv_b) = unpack_small(0)
    (nm_g_pre_mix, nm_g_post_mix, nm_g_pre_ffn, nm_g_post_ffn, nm_b_ada, nm_pool_scale, nm_conv_w, nm_conv_b,
     nm_ffn_conv_w, nm_ffn_conv_b) = unpack_small(1)
    (nv_g_pre_mix, nv_g_post_mix, nv_g_pre_ffn, nv_g_post_ffn, nv_b_ada, nv_pool_scale, nv_conv_w, nv_conv_b,
     nv_ffn_conv_w, nv_ffn_conv_b) = unpack_small(2)
    d_w_ada, nm_w_ada, nv_w_ada = [a.reshape(w_ada.shape) for a in
                                   _adamw(w_ada[0], g_w_ada, m_w_ada[0], v_w_ada[0], "adamw_ada")]

    grads = [g_g_pre_mix, g_g_post_mix, g_g_pre_ffn, g_g_post_ffn, g_w_ada.reshape(w_ada.shape), g_b_ada, g_w_in,
             g_w_pool, g_pool_scale, g_conv_w.reshape(conv_w.shape), g_conv_b, g_w_bout, g_w_o, g_w_up,
             g_ffn_conv_w.reshape(ffn_conv_w.shape), g_ffn_conv_b, g_w_down]
    deltas = [d_g_pre_mix, d_g_post_mix, d_g_pre_ffn, d_g_post_ffn, d_w_ada, d_b_ada, d_w_in, d_w_pool, d_pool_scale,
              d_conv_w, d_conv_b, d_w_bout, d_w_o, d_w_up, d_ffn_conv_w, d_ffn_conv_b, d_w_down]
    new_m = [nm_g_pre_mix, nm_g_post_mix, nm_g_pre_ffn, nm_g_post_ffn, nm_w_ada, nm_b_ada, nm_w_in, nm_w_pool,
             nm_pool_scale, nm_conv_w, nm_conv_b, nm_w_bout, nm_w_o, nm_w_up, nm_ffn_conv_w, nm_ffn_conv_b, nm_w_down]
    new_v = [nv_g_pre_mix, nv_g_post_mix, nv_g_pre_ffn, nv_g_post_ffn, nv_w_ada, nv_b_ada, nv_w_in, nv_w_pool,
             nv_pool_scale, nv_conv_w, nv_conv_b, nv_w_bout, nv_w_o, nv_w_up, nv_ffn_conv_w, nv_ffn_conv_b, nv_w_down]
    return (loss, grad_x.reshape(x.shape), *grads, *deltas, *new_m, *new_v)
```

```python
import math

import jax
import jax.numpy as jnp
from jax import lax
from jax.experimental import pallas as pl
from jax.experimental.pallas import tpu as pltpu

F32 = jnp.float32
BF16 = jnp.bfloat16
MESH = pl.DeviceIdType.MESH

NDEV = 8
NCHIP = 4
EPS = 1e-6
POOL_WINDOWS = (2, 4, 8, 16)
LANES = 128
ADAM_LR = 0.001
ADAM_B1 = 0.9
ADAM_B2 = 0.999
ADAM_EPS = 1e-08
ADAM_WD = 0.01
ADAM_STEP = 10
GELU_C0 = math.sqrt(2.0 / math.pi)
GELU_C1 = 0.044715
VMEM_LIMIT = 56 * 2**20


def _vmem():
    return pl.BlockSpec(memory_space=pltpu.VMEM)


def _any():
    return pl.BlockSpec(memory_space=pl.ANY)


def _params(*sem):
    return pltpu.CompilerParams(dimension_semantics=sem, vmem_limit_bytes=VMEM_LIMIT)


def _sds(shape, dtype):
    return jax.ShapeDtypeStruct(tuple(shape), dtype)


def _position():
    return lax.axis_index("x"), lax.axis_index("y"), lax.axis_index("c")


def _linear(x, y, c):
    return 4 * x + 2 * y + c


def _dot(a, b):
    return jnp.dot(a, b, preferred_element_type=F32)


def _dot_nt(a, b):
    return lax.dot_general(a, b, (((1,), (1,)), ((), ())), preferred_element_type=F32)


def _dot_tn(a, b):
    return lax.dot_general(a, b, (((0,), (0,)), ((), ())), preferred_element_type=F32)


def _colsum(v):
    return jnp.sum(v, axis=0, keepdims=True)


def _rowmean(v):
    return jnp.mean(v, axis=-1, keepdims=True)


GROUP = 256


def _interleave(v):
    g, n = v.shape
    return jnp.swapaxes(v.reshape(8, g // 8, n), 0, 1).reshape(g, n)


def _deinterleave(v):
    g, n = v.shape
    return jnp.swapaxes(v.reshape(g // 8, 8, n), 0, 1).reshape(g, n)


def _halo_top(cur_last, prev_last):
    rows, n = cur_last.shape
    c3 = cur_last.reshape(rows // 8, 8, n)
    p3 = prev_last.reshape(rows // 8, 8, n)
    sub = lax.broadcasted_iota(jnp.int32, c3.shape, 1)
    return jnp.where(sub == 0, pltpu.roll(p3, 1, 1), pltpu.roll(c3, 1, 1)).reshape(rows, n)


def _halo_bottom(cur_first, next_first):
    rows, n = cur_first.shape
    c3 = cur_first.reshape(rows // 8, 8, n)
    n3 = next_first.reshape(rows // 8, 8, n)
    sub = lax.broadcasted_iota(jnp.int32, c3.shape, 1)
    return jnp.where(sub == 7, pltpu.roll(n3, 7, 1), pltpu.roll(c3, 7, 1)).reshape(rows, n)


def _shift_down(v, halo, k):
    rows = v.shape[0]
    return jnp.concatenate([halo[halo.shape[0] - 8 * k:, :], v[:rows - 8 * k, :]], axis=0)


def _shift_up(v, halo, k):
    return jnp.concatenate([v[8 * k:, :], halo[:8 * k, :]], axis=0)


def _inv_count(first_token, window):
    row = lax.broadcasted_iota(jnp.int32, (GROUP, 1), 0)
    t = first_token + (row % 8) * (GROUP // 8) + row // 8
    return 1.0 / jnp.minimum(t + 1, window).astype(F32)


class _Job:
    def __init__(self, inputs, out_shape, scratch, phases):
        self.inputs, self.out_shape, self.scratch, self.phases = list(inputs), list(out_shape), list(scratch), phases


def _call(body, name, grid, in_specs, out_specs, out_shape, scratch_shapes, params, operands, job=None):
    if job is None:
        return pl.pallas_call(body, name=name, grid=grid, in_specs=in_specs, out_specs=out_specs, out_shape=out_shape,
                              scratch_shapes=scratch_shapes, compiler_params=params)(*operands)
    n_in, n_out, n_scr = len(in_specs), len(out_specs), len(scratch_shapes)
    j_in, j_out = len(job.inputs), len(job.out_shape)
    steps = math.prod(grid)

    def hosted(*refs):
        own_in, refs = refs[:n_in], refs[n_in:]
        jin, refs = refs[:j_in], refs[j_in:]
        own_out, refs = refs[:n_out], refs[n_out:]
        jout, refs = refs[:j_out], refs[j_out:]
        own_scr, jscr = refs[:n_scr], refs[n_scr:]
        step = pl.program_id(0)
        for axis in range(1, len(grid)):
            step = step * grid[axis] + pl.program_id(axis)
        for frac, fn in job.phases[:-1]:
            pl.when(step == int(frac * (steps - 1)))(lambda fn=fn: fn(jin, jout, jscr))
        body(*own_in, *own_out, *own_scr)
        pl.when(step == steps - 1)(lambda: job.phases[-1][1](jin, jout, jscr))

    return pl.pallas_call(
        hosted, name=name, grid=grid, in_specs=list(in_specs) + [_any()] * j_in,
        out_specs=list(out_specs) + [_any()] * j_out, out_shape=list(out_shape) + job.out_shape,
        scratch_shapes=list(scratch_shapes) + job.scratch, compiler_params=params)(*operands, *job.inputs)


def _run_job(job, name):
    n_in, n_out = len(job.inputs), len(job.out_shape)

    def body(*refs):
        for _, fn in job.phases:
            fn(refs[:n_in], refs[n_in:n_in + n_out], refs[n_in + n_out:])

    return pl.pallas_call(body, name=name, out_shape=job.out_shape, in_specs=[_any()] * n_in,
                          out_specs=[_any()] * n_out, scratch_shapes=job.scratch)(*job.inputs)


def _peers(x, y, c):
    out = []
    for k in range(1, NDEV):
        out.append(((1 - x) if k & 4 else x, (1 - y) if k & 2 else y, (1 - c) if k & 1 else c))
    return out


def _small_allreduce(v, name, job):
    r, n = v.shape
    j_in, j_out = len(job.inputs), len(job.out_shape)

    def body(v_ref, *rest):
        jin, rest = rest[:j_in], rest[j_in:]
        gat_ref, sum_ref = rest[:2]
        jout, rest = rest[2:2 + j_out], rest[2 + j_out:]
        send_sems, recv_sems, local_sem = rest[:3]
        jscr = rest[3:]
        job.phases[0][1](jin, jout, jscr)
        x, y, c = _position()
        me = _linear(x, y, c)
        mine = pltpu.make_async_copy(v_ref, gat_ref.at[me], local_sem)
        mine.start()
        peers = _peers(x, y, c)
        sends = []
        for k, peer in enumerate(peers):
            cp = pltpu.make_async_remote_copy(src_ref=v_ref, dst_ref=gat_ref.at[me], send_sem=send_sems.at[k],
                                              recv_sem=recv_sems.at[k], device_id=peer, device_id_type=MESH)
            cp.start()
            sends.append(cp)
        for k, peer in enumerate(peers):
            pltpu.make_async_remote_copy(src_ref=v_ref, dst_ref=gat_ref.at[_linear(*peer)], send_sem=send_sems.at[k],
                                         recv_sem=recv_sems.at[k], device_id=peer, device_id_type=MESH).wait_recv()
        for cp in sends:
            cp.wait_send()
        mine.wait()
        acc = gat_ref[0]
        for j in range(1, NDEV):
            acc = acc + gat_ref[j]
        sum_ref[...] = acc
        job.phases[-1][1](jin, jout, jscr)

    return pl.pallas_call(
        body, name=name, out_shape=[_sds((NDEV, r, n), F32), _sds((r, n), F32)] + job.out_shape,
        in_specs=[_vmem()] + [_any()] * j_in, out_specs=[_vmem()] * 2 + [_any()] * j_out,
        scratch_shapes=[pltpu.SemaphoreType.DMA((NDEV - 1,)), pltpu.SemaphoreType.DMA((NDEV - 1,)),
                        pltpu.SemaphoreType.DMA(())] + job.scratch,
    )(v, *job.inputs)


def _exchange_rows(src_for, dst_ref, sems):
    send_sems, recv_sems, local_sem = sems
    x, y, c = _position()
    me = _linear(x, y, c)
    row = lambda j: dst_ref.at[pl.ds(j, 1), :]
    mine = pltpu.make_async_copy(src_for(me), row(me), local_sem)
    mine.start()
    peers = _peers(x, y, c)
    sends = []
    for k, peer in enumerate(peers):
        cp = pltpu.make_async_remote_copy(src_ref=src_for(_linear(*peer)), dst_ref=row(me), send_sem=send_sems.at[k],
                                          recv_sem=recv_sems.at[k], device_id=peer, device_id_type=MESH)
        cp.start()
        sends.append(cp)
    for k, peer in enumerate(peers):
        pltpu.make_async_remote_copy(src_ref=src_for(me), dst_ref=row(_linear(*peer)), send_sem=send_sems.at[k],
                                     recv_sem=recv_sems.at[k], device_id=peer, device_id_type=MESH).wait_recv()
    for cp in sends:
        cp.wait_send()
    mine.wait()


def _gather_weights_and_modulation(job, pack, w_ada, b_piece, learned, fcb, d, cw_n, up_n):
    n = pack.shape[1]
    m = w_ada.shape[1]
    j_in, j_out = len(job.inputs), len(job.out_shape)
    row_sems = [pltpu.SemaphoreType.DMA((NDEV - 1,)), pltpu.SemaphoreType.DMA((NDEV - 1,)), pltpu.SemaphoreType.DMA(())]

    def body(pack_ref, wada_ref, bp_ref, lrn_ref, fcb_ref, *rest):
        jin, rest = rest[:j_in], rest[j_in:]
        gat_ref, vpm_ref, vmix_ref, vpf_ref, vffn_ref, fcv_ref = rest[:6]
        jout, rest = rest[6:6 + j_out], rest[6 + j_out:]
        sems1, sems2, piece, mod_ref, modm, jscr = rest[0:3], rest[3:6], rest[6], rest[7], rest[8], rest[9:]
        phases = [fn for _, fn in job.phases]
        phases[0](jin, jout, jscr)
        _exchange_rows(lambda j: pack_ref, gat_ref, sems1)
        c16 = jnp.concatenate([gat_ref[:, 0:d], jnp.zeros((NDEV, d), F32)], axis=0).astype(BF16)
        piece[...] = (_dot(c16, wada_ref[...].astype(BF16)) + bp_ref[...])[0:NDEV, :]
        _exchange_rows(lambda j: piece.at[pl.ds(j, 1), :], mod_ref, sems2)

        for j in range(NDEV):
            done = 0
            while done < m:
                row, col = divmod(j * m + done, d)
                width = min(m - done, d - col)
                modm[row:row + 1, col:col + width] = mod_ref[j:j + 1, done:done + width]
                done += width
        for ref in (vpm_ref, vmix_ref, vpf_ref, vffn_ref, fcv_ref):
            ref[...] = jnp.zeros_like(ref)
        for ref, g_row, k in ((vpm_ref, 0, 0), (vpf_ref, 2, 3)):
            ref[0:1, :] = lrn_ref[g_row:g_row + 1, :]
            ref[1:2, :] = 1.0 + modm[k + 1:k + 2, :]
            ref[2:3, :] = modm[k:k + 1, :]
        vmix_ref[0:1, :] = modm[2:3, :]
        vmix_ref[1:2, :] = lrn_ref[1:2, :]
        vmix_ref[2:4, :] = lrn_ref[4:6, :]
        vffn_ref[0:1, :] = modm[5:6, :]
        vffn_ref[1:2, :] = lrn_ref[3:4, :]
        fcv_ref[3:4, :] = fcb_ref[...]
        for j in range(NDEV):
            for k in range(3):
                vmix_ref[4 + k:5 + k, j * cw_n:(j + 1) * cw_n] = gat_ref[j:j + 1, d + k * cw_n:d + (k + 1) * cw_n]
                off = d + 3 * cw_n + k * up_n
                fcv_ref[k:k + 1, j * up_n:(j + 1) * up_n] = gat_ref[j:j + 1, off:off + up_n]

        for fn in phases[1:]:
            fn(jin, jout, jscr)

    tables = [_sds((8, d), F32), _sds((16, d), F32), _sds((8, d), F32), _sds((8, d), F32), _sds((8, NDEV * up_n), F32)]
    return pl.pallas_call(
        body, name="gather_weights_and_modulation",
        out_shape=[_sds((NDEV, n), F32)] + tables + job.out_shape,
        in_specs=[_vmem()] * 5 + [_any()] * j_in, out_specs=[_vmem()] * 6 + [_any()] * j_out,
        scratch_shapes=row_sems + row_sems + [pltpu.VMEM((NDEV, m), F32), pltpu.VMEM((NDEV, m), F32),
                                              pltpu.VMEM((8, d), F32)] + job.scratch,
    )(pack, w_ada, b_piece, learned, fcb, *job.inputs)


def _gathered(shard, layout):
    if layout == "rows":
        return (NDEV,) + shard.shape, lambda ref, j: ref.at[j]
    if layout == "cols":
        r, c = shard.shape
        return (r, NDEV * c), lambda ref, j: ref.at[:, pl.ds(pl.multiple_of(j * c, LANES), c)]
    g, r, c = shard.shape
    return (g, NDEV * r, c), lambda ref, j: ref.at[:, pl.ds(pl.multiple_of(j * r, 16), r), :]


def _allgather_job(shards, layouts, relay_at, forward_at):
    n = len(shards)
    specs = [_gathered(s, l) for s, l in zip(shards, layouts)]
    halves = [s.shape[0] // 2 for s in shards]

    def plan(src, dst, sems):
        send_sems, recv_sems, _ = sems
        x, y, c = _position()
        me, sibling, xn, yn, dg = (x, y, c), (x, y, 1 - c), (1 - x, y, c), (x, 1 - y, c), (1 - x, 1 - y, c)

        def copy(a, k, block, to, half=None, from_src=False):
            blk = specs[a][1](dst[a], _linear(*block))
            if half is not None:
                blk = blk.at[pl.ds(half * halves[a], halves[a])]
            return pltpu.make_async_remote_copy(src_ref=src[a] if from_src else blk, dst_ref=blk,
                                                send_sem=send_sems.at[a, k], recv_sem=recv_sems.at[a, k],
                                                device_id=to, device_id_type=MESH)
        return copy, me, sibling, xn, yn, dg

    def local(src, dst, sems):
        x, y, c = _position()
        return [pltpu.make_async_copy(src[a], specs[a][1](dst[a], _linear(x, y, c)), sems[2].at[a]) for a in range(n)]

    def own(src, dst, sems):
        copy, me, sibling, xn, yn, dg = plan(src, dst, sems)
        return [copy(a, k, me, to, from_src=True) for k, to in ((1, xn), (2, yn), (0, sibling)) for a in range(n)]

    def relayed(src, dst, sems):
        copy, me, sibling, xn, yn, dg = plan(src, dst, sems)
        return ([copy(a, 3, xn, yn, half=0) for a in range(n)] + [copy(a, 5, xn, sibling) for a in range(n)],
                [copy(a, 4, yn, xn, half=1) for a in range(n)] + [copy(a, 6, yn, sibling) for a in range(n)])

    def diagonal(src, dst, sems):
        copy, me, sibling, xn, yn, dg = plan(src, dst, sems)
        return [copy(a, 7, dg, sibling) for a in range(n)]

    def start(src, dst, sems):
        for cp in local(src, dst, sems) + own(src, dst, sems):
            cp.start()

    def relay(src, dst, sems):
        copy, me, sibling, xn, yn, dg = plan(src, dst, sems)
        from_x, from_y = relayed(src, dst, sems)
        for a in range(n):
            copy(a, 1, xn, me).wait_recv()
        for cp in from_x:
            cp.start()
        for a in range(n):
            copy(a, 2, yn, me).wait_recv()
        for cp in from_y:
            cp.start()

    def forward(src, dst, sems):
        copy, me, sibling, xn, yn, dg = plan(src, dst, sems)
        for a in range(n):
            copy(a, 3, dg, me, half=0).wait_recv()
            copy(a, 4, dg, me, half=1).wait_recv()
        for cp in diagonal(src, dst, sems):
            cp.start()

    def finish(src, dst, sems):
        copy, me, sibling, xn, yn, dg = plan(src, dst, sems)
        other = lambda dev: (dev[0], dev[1], 1 - dev[2])
        for a in range(n):
            copy(a, 0, sibling, me).wait_recv()
            for k, dev in ((5, xn), (6, yn), (7, dg)):
                copy(a, k, other(dev), me).wait_recv()
        from_x, from_y = relayed(src, dst, sems)
        for cp in own(src, dst, sems) + from_x + from_y + diagonal(src, dst, sems):
            cp.wait_send()
        for cp in local(src, dst, sems):
            cp.wait()

    return _Job(shards, [_sds(spec[0], s.dtype) for spec, s in zip(specs, shards)],
                [pltpu.SemaphoreType.DMA((n, 8)), pltpu.SemaphoreType.DMA((n, 8)), pltpu.SemaphoreType.DMA((n,))],
                [(0.0, start), (relay_at, relay), (forward_at, forward), (1.0, finish)])


def _sibling_job(grads):
    n = len(grads)

    def copies(src, dst, sems):
        x, y, c = _position()
        return [pltpu.make_async_remote_copy(src_ref=src[a].at[2 * q + 1 - c], dst_ref=dst[a].at[q],
                                             send_sem=sems[0].at[a, q], recv_sem=sems[1].at[a, q],
                                             device_id=(x, y, 1 - c), device_id_type=MESH)
                for a in range(n) for q in range(NCHIP)]

    return _exchange_job(grads, NCHIP, copies)


def _chips_job(chip_sums):
    n = len(chip_sums)

    def copies(src, dst, sems):
        x, y, c = _position()
        chips = [(1 - x, y), (x, 1 - y), (1 - x, 1 - y)]
        return [pltpu.make_async_remote_copy(src_ref=src[a].at[2 * chip[0] + chip[1]], dst_ref=dst[a].at[j],
                                             send_sem=sems[0].at[a, j], recv_sem=sems[1].at[a, j],
                                             device_id=(*chip, c), device_id_type=MESH)
                for j, chip in enumerate(chips) for a in range(n)]

    return _exchange_job(chip_sums, 3, copies)


def _exchange_job(arrays, slots, copies):
    n = len(arrays)

    def start(src, dst, sems):
        for cp in copies(src, dst, sems):
            cp.start()

    def finish(src, dst, sems):
        cps = copies(src, dst, sems)
        for cp in cps:
            cp.wait_recv()
        for cp in cps:
            cp.wait_send()

    return _Job(arrays, [_sds((slots,) + a.shape[1:], a.dtype) for a in arrays],
                [pltpu.SemaphoreType.DMA((n, slots)), pltpu.SemaphoreType.DMA((n, slots))],
                [(0.0, start), (1.0, finish)])


def _row_block(r):
    if r <= 512:
        return r
    for rb in range(512, 15, -16):
        if r % rb == 0:
            return rb
    return r


def _chip_sums(grads, from_sibling, core, name):
    n = len(grads)

    def body(core_ref, *refs):
        del core_ref
        for a in range(n):
            refs[2 * n + a][...] = (refs[a][...].astype(F32) + refs[n + a][...].astype(F32)).astype(BF16)

    block = lambda g, index: pl.BlockSpec((None,) + g.shape[1:], index)
    grid_spec = pltpu.PrefetchScalarGridSpec(
        num_scalar_prefetch=1, grid=(NCHIP,),
        in_specs=[block(g, lambda q, core: (2 * q + core[0], 0, 0)) for g in grads]
        + [block(g, lambda q, core: (q, 0, 0)) for g in grads],
        out_specs=[block(g, lambda q, core: (q, 0, 0)) for g in grads])
    return pl.pallas_call(body, name=name, grid_spec=grid_spec,
                          out_shape=[_sds((NCHIP,) + g.shape[1:], BF16) for g in grads],
                          compiler_params=_params("parallel"))(core, *grads, *from_sibling)


def _adamw_math(w, g, m, v):
    m2 = ADAM_B1 * m + (1.0 - ADAM_B1) * g
    v2 = ADAM_B2 * v + (1.0 - ADAM_B2) * jnp.square(g)
    m_hat = m2 / (1.0 - ADAM_B1 ** ADAM_STEP)
    v_hat = v2 / (1.0 - ADAM_B2 ** ADAM_STEP)
    delta = -ADAM_LR * (m_hat / (jnp.sqrt(v_hat) + ADAM_EPS) + ADAM_WD * w)
    return delta, m2, v2


def _adamw(w, g, m, v, name):
    r, c = w.shape
    rb = _row_block(r)

    def body(w_ref, g_ref, m_ref, v_ref, d_ref, m2_ref, v2_ref):
        d, m2, v2 = _adamw_math(w_ref[...], g_ref[...], m_ref[...], v_ref[...])
        d_ref[...] = d
        m2_ref[...] = m2
        v2_ref[...] = v2

    blk = pl.BlockSpec((rb, c), lambda i: (i, 0))
    return pl.pallas_call(body, name=name, grid=(r // rb,), in_specs=[blk] * 4, out_specs=[blk] * 3,
                          out_shape=[_sds((r, c), F32)] * 3, compiler_params=_params("parallel"))(w, g, m, v)


def _adamw_group(items, name):
    n = len(items)

    def body(*refs):
        ins, outs = refs[:4 * n], refs[4 * n:]
        for a in range(n):
            w_ref, g_ref, m_ref, v_ref = ins[4 * a:4 * a + 4]
            for ref, val in zip(outs[3 * a:3 * a + 3], _adamw_math(w_ref[...], g_ref[...], m_ref[...], v_ref[...])):
                ref[...] = val

    outs = pl.pallas_call(body, name=name, in_specs=[_vmem()] * (4 * n), out_specs=[_vmem()] * (3 * n),
                          out_shape=[_sds(q[0].shape, F32) for q in items for _ in range(3)])(
        *[a for q in items for a in q])
    return [outs[3 * a:3 * a + 3] for a in range(n)]


def _reduce_adamw(grad, from_sibling, from_chips, sel, w, m, v, name, transposed=False):
    r, c = w.shape[::-1] if transposed else w.shape
    cp = grad.shape[2]
    rb = _row_block(r)

    def body(sel_ref, g_ref, s_ref, c0_ref, c1_ref, c2_ref, w_ref, m_ref, v_ref, go_ref, d_ref, m2_ref, v2_ref):
        del sel_ref
        g = g_ref[...].astype(F32) + s_ref[...].astype(F32)
        g = g + c0_ref[...].astype(F32)
        g = g + c1_ref[...].astype(F32)
        g = g + c2_ref[...].astype(F32)
        g = g.T[0:c, :] if transposed else g[:, 0:c]
        d, m2, v2 = _adamw_math(w_ref[...], g, m_ref[...], v_ref[...])
        go_ref[...] = g
        d_ref[...] = d
        m2_ref[...] = m2
        v2_ref[...] = v2

    blk = pl.BlockSpec((c, rb), lambda i, sel: (0, i)) if transposed else pl.BlockSpec((rb, c), lambda i, sel: (i, 0))
    grid_spec = pltpu.PrefetchScalarGridSpec(
        num_scalar_prefetch=1, grid=(r // rb,),
        in_specs=[pl.BlockSpec((None, rb, cp), lambda i, sel: (sel[0], i, 0)),
                  pl.BlockSpec((None, rb, cp), lambda i, sel: (sel[1], i, 0)),
                  pl.BlockSpec((None, rb, cp), lambda i, sel: (0, i, 0)),
                  pl.BlockSpec((None, rb, cp), lambda i, sel: (1, i, 0)),
                  pl.BlockSpec((None, rb, cp), lambda i, sel: (2, i, 0)),
                  blk, blk, blk],
        out_specs=[blk] * 4)
    return pl.pallas_call(body, name=name, grid_spec=grid_spec, out_shape=[_sds(w.shape, F32)] * 4,
                          compiler_params=_params("parallel"))(sel, grad, from_sibling, from_chips, from_chips,
                                                               from_chips, w, m, v)


def _reduce_adamw_group(items, sel, name):
    n = len(items)

    def body(sel_ref, *refs):
        del sel_ref
        ins, outs = refs[:8 * n], refs[8 * n:]
        for a in range(n):
            g_ref, s_ref, c0_ref, c1_ref, c2_ref, w_ref, m_ref, v_ref = ins[8 * a:8 * a + 8]
            g = g_ref[...].astype(F32) + s_ref[...].astype(F32)
            g = g + c0_ref[...].astype(F32)
            g = g + c1_ref[...].astype(F32)
            g = g + c2_ref[...].astype(F32)
            d, m2, v2 = _adamw_math(w_ref[...], g, m_ref[...], v_ref[...])
            for ref, val in zip(outs[4 * a:4 * a + 4], (g, d, m2, v2)):
                ref[...] = val

    in_specs, out_specs, out_shape, operands = [], [], [], []
    for grad, from_sibling, from_chips, w, m, v in items:
        slot = lambda index, shape=grad.shape[1:]: pl.BlockSpec((None,) + shape, index)
        full = pl.BlockSpec(w.shape, lambda i, sel: (0, 0))
        in_specs += [slot(lambda i, sel: (sel[0], 0, 0)), slot(lambda i, sel: (sel[1], 0, 0)),
                     slot(lambda i, sel: (0, 0, 0)), slot(lambda i, sel: (1, 0, 0)), slot(lambda i, sel: (2, 0, 0)),
                     full, full, full]
        out_specs += [full] * 4
        out_shape += [_sds(w.shape, F32)] * 4
        operands += [grad, from_sibling, from_chips, from_chips, from_chips, w, m, v]
    grid_spec = pltpu.PrefetchScalarGridSpec(num_scalar_prefetch=1, grid=(1,), in_specs=in_specs, out_specs=out_specs)
    outs = pl.pallas_call(body, name=name, grid_spec=grid_spec, out_shape=out_shape,
                          compiler_params=_params("arbitrary"))(sel, *operands)
    return [outs[4 * a:4 * a + 4] for a in range(n)]


def _wada_grad(c_all, dmod_piece):
    d = c_all.shape[1]
    n = dmod_piece.shape[1]

    def body(c_ref, dm_ref, o_ref):
        o_ref[...] = _dot_tn(c_ref[...], dm_ref[...])

    return pl.pallas_call(body, name="ada_wgrad", out_shape=_sds((d, n), F32),
                          in_specs=[_vmem()] * 2, out_specs=_vmem())(c_all, dmod_piece)


def _column_chunks(width):
    for n in (4, 2):
        if width % (n * LANES) == 0:
            return n
    return 1


def _conv_taps(ref, col):
    return ref[0:1, col], ref[1:2, col], ref[2:3, col]


def _gelu_parts(u):
    u2 = u * u
    th = jnp.tanh((GELU_C0 * u) * (1.0 + GELU_C1 * u2))
    dcdf = (1.0 - th * th) * ((0.5 * GELU_C0) * (1.0 + (3.0 * GELU_C1) * u2))
    return 0.5 * (1.0 + th), dcdf


def _conv3_bwd(dv, carry, col, taps, x):
    halo = _halo_bottom(dv[:16, :], carry[:, col])
    carry[:, col] = dv[:16, :]
    d1 = _shift_up(dv, halo, 1)
    d2 = _shift_up(dv, halo, 2)
    w0, w1, w2 = taps
    dx = w2 * dv
    dx = dx + w1 * d1
    dx = dx + w0 * d2
    return dx, (_colsum(d2 * x), _colsum(d1 * x), _colsum(dv * x))


def _prenorm(x, vp_ref):
    r = lax.rsqrt(_rowmean(x * x) + EPS)
    nh = x * r
    return (nh * vp_ref[0:1, :]) * vp_ref[1:2, :] + vp_ref[2:3, :], r, nh


def _prenorm_bwd(dh, r, nh, vp_ref, red_ref):
    g, sc1 = vp_ref[0:1, :], vp_ref[1:2, :]
    red_ref[0:1, :] += _colsum(dh)
    red_ref[1:2, :] += _colsum(dh * (nh * g))
    red_ref[2:3, :] += _colsum(dh * nh * sc1)
    dnh = dh * g * sc1
    return r * (dnh - nh * _rowmean(dnh * nh))


def _postnorm_bwd(dres, z, gate, gpost, red_ref):
    r = lax.rsqrt(_rowmean(z * z) + EPS)
    nh = z * r
    dn = dres * gate
    red_ref[0:1, :] += _colsum(dn * nh)
    red_ref[1:2, :] += _colsum(dres * (nh * gpost))
    dnh = dn * gpost
    return r * (dnh - nh * _rowmean(dnh * nh))


def _mixer_block_fwd(x, vec_pre, vec, w_in, w_pool, w_bout, w_o, job):
    t, d = x.shape
    tm = GROUP
    gw = d // len(POOL_WINDOWS)
    pool_rows = 8 * (POOL_WINDOWS[-1] - 1)

    def body(x_ref, vp_ref, vec_ref, win_ref, wp_ref, wb_ref, wo_ref,
             hb_ref, p5_ref, qm_ref, cv_ref, yar_ref, yb_ref, o_ref, x1_ref, mbuf, ucarry, pcarry):
        i = pl.program_id(0)

        @pl.when(i == 0)
        def _():
            ucarry[...] = jnp.zeros_like(ucarry)
            pcarry[...] = jnp.zeros_like(pcarry)

        xp = _interleave(x_ref[...])
        hb = _prenorm(xp, vp_ref)[0].astype(BF16)
        hb_ref[...] = hb
        proj = lambda k: _dot(hb, win_ref[:, k * d:(k + 1) * d])

        za = proj(4)
        p5_ref[:, 3 * d:4 * d] = za.astype(BF16)
        sa = jax.nn.sigmoid(za)
        u_pool = proj(0)
        for g, window in enumerate(POOL_WINDOWS):
            cols = slice(g * gw, (g + 1) * gw)
            rows = 8 * (window - 1)
            u = u_pool[:, cols]
            halo = _halo_top(u[tm - rows:, :], ucarry[pool_rows - rows:, cols])
            s, shift = jnp.concatenate([halo, u], axis=0), 1
            while shift < window:
                s = s[8 * shift:, :] + s[:s.shape[0] - 8 * shift, :]
                shift *= 2
            pgb = (s * _inv_count(i * tm, window) - u).astype(BF16)
            qm_ref[:, 2 * d + g * gw:2 * d + (g + 1) * gw] = pgb
            yar = _dot(pgb, wp_ref[g])
            yar_ref[:, cols] = yar
            mbuf[:, cols] = sa[:, cols] * (yar * vec_ref[2:3, cols])
        ucarry[...] = u_pool[tm - pool_rows:, :]

        ux = proj(1)
        uc = proj(3)
        p5_ref[:, 0:d] = ux.astype(BF16)
        p5_ref[:, 2 * d:3 * d] = uc.astype(BF16)
        p = uc * ux
        halo = _halo_top(p[tm - 16:, :], pcarry[...])
        pcarry[...] = p[tm - 16:, :]
        cv = vec_ref[3:4, :] + vec_ref[4:5, :] * _shift_down(p, halo, 2)
        cv = cv + vec_ref[5:6, :] * _shift_down(p, halo, 1)
        cv = cv + vec_ref[6:7, :] * p
        cv_ref[...] = cv
        ub = proj(2)
        p5_ref[:, d:2 * d] = ub.astype(BF16)
        qb = (ub * cv).astype(BF16)
        qm_ref[:, 0:d] = qb
        yb = _dot(qb, wb_ref[...])
        yb_ref[...] = yb

        zb = proj(5)
        p5_ref[:, 4 * d:5 * d] = zb.astype(BF16)
        mb = (mbuf[...] + jax.nn.sigmoid(zb) * yb).astype(BF16)
        qm_ref[:, d:2 * d] = mb
        o = _dot(mb, wo_ref[...])
        o_ref[...] = o
        r2 = lax.rsqrt(_rowmean(o * o) + EPS)
        x1_ref[...] = xp + vec_ref[0:1, :] * ((o * r2) * vec_ref[1:2, :])

    row = lambda n: pl.BlockSpec((tm, n), lambda i: (i, 0))
    widths = [d, 5 * d, 3 * d, d, d, d, d, d]
    return _call(
        body, "mixer_block_fwd", (t // tm,), [row(d)] + [_vmem()] * 6, [row(n) for n in widths],
        [_sds((t, n), BF16) for n in widths[:3]] + [_sds((t, n), F32) for n in widths[3:]],
        [pltpu.VMEM((tm, d), F32), pltpu.VMEM((pool_rows, d), F32), pltpu.VMEM((16, d), F32)],
        _params("arbitrary"), (x, vec_pre, vec, w_in, w_pool, w_bout, w_o), job)


def _ffn_block_fwd(x1, target, vec_pre, vec, fcv, w_up, w_down):
    t, d = x1.shape
    tm = GROUP
    fp = w_down.shape[0]
    nch = _column_chunks(fp)
    cw = fp // nch

    def body(x1_ref, tg_ref, vp_ref, vec_ref, fcv_ref, wu_ref, wd_ref,
             hb_ref, upb_ref, upreb_ref, a_ref, ffb_ref, dy_ref, loss_ref, carry):
        i = pl.program_id(0)

        @pl.when(i == 0)
        def _():
            carry[...] = jnp.zeros_like(carry)
            loss_ref[...] = jnp.zeros_like(loss_ref)

        x1 = x1_ref[...]
        hb = _prenorm(x1, vp_ref)[0].astype(BF16)
        hb_ref[...] = hb

        cols = [(slice(j * cw, (j + 1) * cw), slice(fp + j * cw, fp + (j + 1) * cw)) for j in range(nch)]
        up_gate = _dot(hb, wu_ref[:, 0:fp])
        up_val = _dot(hb, wu_ref[:, fp:2 * fp])

        def conv(v, col):
            halo = _halo_top(v[tm - 16:, :], carry[:, col])
            carry[:, col] = v[tm - 16:, :]
            w0, w1, w2 = _conv_taps(fcv_ref, col)
            y = fcv_ref[3:4, col] + w0 * _shift_down(v, halo, 2)
            y = y + w1 * _shift_down(v, halo, 1)
            y = y + w2 * v
            upb_ref[:, col] = y.astype(BF16)
            upreb_ref[:, col] = v.astype(BF16)
            return y

        ff = None
        for j in range(nch):
            gc, vc = cols[j]
            gate = conv(up_gate[:, gc], gc)
            val = conv(up_val[:, gc], vc)
            ab = ((gate * _gelu_parts(gate)[0]) * val).astype(BF16)
            a_ref[:, gc] = ab
            part = _dot(ab, wd_ref[gc, :])
            ff = part if ff is None else ff + part
        ffb_ref[...] = ff.astype(BF16)
        r4 = lax.rsqrt(_rowmean(ff * ff) + EPS)
        y = x1 + vec_ref[0:1, :] * ((ff * r4) * vec_ref[1:2, :])
        e = y - _interleave(tg_ref[...])
        dy_ref[...] = e * (1.0 / d)
        loss_ref[...] += jnp.sum(_rowmean(e * e))

    row = lambda n: pl.BlockSpec((tm, n), lambda i: (i, 0))
    return pl.pallas_call(
        body, name="ffn_block_fwd", grid=(t // tm,),
        in_specs=[row(d), row(d)] + [_vmem()] * 5,
        out_specs=[row(d), row(2 * fp), row(2 * fp), row(fp), row(d), row(d), pl.BlockSpec((8, LANES), lambda i: (0, 0))],
        out_shape=[_sds((t, d), BF16), _sds((t, 2 * fp), BF16), _sds((t, 2 * fp), BF16), _sds((t, fp), BF16),
                   _sds((t, d), BF16), _sds((t, d), F32), _sds((8, LANES), F32)],
        scratch_shapes=[pltpu.VMEM((16, 2 * fp), F32)],
        compiler_params=_params("arbitrary"),
    )(x1, target, vec_pre, vec, fcv, w_up, w_down)


def _ffn_block_bwd(dy, ffb, x1, upb, upreb, vec_pre, vec, fcv, w_up, w_down):
    t, d = dy.shape
    tm = GROUP
    fp = w_down.shape[0]
    nch = _column_chunks(fp)
    cw = fp // nch
    nt = t // tm

    def body(dy_ref, ff_ref, x1_ref, upb_ref, upreb_ref, vp_ref, vec_ref, fcv_ref, wu_ref, wd_ref,
             dff_ref, dup_ref, dx1_ref, red_ref, cred_ref, pred_ref, carry):
        @pl.when(pl.program_id(0) == 0)
        def _():
            carry[...] = jnp.zeros_like(carry)
            red_ref[...] = jnp.zeros_like(red_ref)
            cred_ref[...] = jnp.zeros_like(cred_ref)
            pred_ref[...] = jnp.zeros_like(pred_ref)

        dy_v = dy_ref[...]
        dffb = _postnorm_bwd(dy_v, ff_ref[...].astype(F32), vec_ref[0:1, :], vec_ref[1:2, :], red_ref).astype(BF16)
        dff_ref[...] = dffb

        def conv_bwd(dv, col):
            dx, (t0, t1, t2) = _conv3_bwd(dv, carry, col, _conv_taps(fcv_ref, col), upreb_ref[:, col].astype(F32))
            cred_ref[0:1, col] += t0
            cred_ref[1:2, col] += t1
            cred_ref[2:3, col] += t2
            cred_ref[3:4, col] += _colsum(dv)
            dxb = dx.astype(BF16)
            dup_ref[:, col] = dxb
            return _dot_nt(dxb, wu_ref[:, col])

        dh = None
        for j in range(nch):
            gc = slice(j * cw, (j + 1) * cw)
            vc = slice(fp + j * cw, fp + (j + 1) * cw)
            da = _dot_nt(dffb, wd_ref[gc, :])
            gate = upb_ref[:, gc].astype(F32)
            val = upb_ref[:, vc].astype(F32)
            cdf, dcdf = _gelu_parts(gate)
            part = conv_bwd(da * val * (cdf + gate * dcdf), gc) + conv_bwd(da * (gate * cdf), vc)
            dh = part if dh is None else dh + part

        _, r, nh = _prenorm(x1_ref[...], vp_ref)
        dx1_ref[...] = dy_v + _prenorm_bwd(dh, r, nh, vp_ref, pred_ref)

    rev = lambda n: pl.BlockSpec((tm, n), lambda i: (nt - 1 - i, 0))
    fixed = lambda n: pl.BlockSpec((8, n), lambda i: (0, 0))
    return pl.pallas_call(
        body, name="ffn_block_bwd", grid=(nt,),
        in_specs=[rev(d), rev(d), rev(d), rev(2 * fp), rev(2 * fp)] + [_vmem()] * 5,
        out_specs=[rev(d), rev(2 * fp), rev(d), fixed(d), fixed(2 * fp), fixed(d)],
        out_shape=[_sds((t, d), BF16), _sds((t, 2 * fp), BF16), _sds((t, d), F32), _sds((8, d), F32),
                   _sds((8, 2 * fp), F32), _sds((8, d), F32)],
        scratch_shapes=[pltpu.VMEM((16, 2 * fp), F32)],
        compiler_params=_params("arbitrary"),
    )(dy, ffb, x1, upb, upreb, vec_pre, vec, fcv, w_up, w_down)


def _mixer_block_bwd(dx1, o, yar, yb, cv, p5b, x, vec_pre, vec, w_in, w_pool, w_bout, w_o, job):
    t, d = dx1.shape
    tm = GROUP
    gw = d // len(POOL_WINDOWS)
    nt = t // tm
    pool_rows = 8 * (POOL_WINDOWS[-1] - 1)

    def body(dx1_ref, o_ref, yar_ref, yb_ref, cv_ref, p5_ref, x_ref, vp_ref, vec_ref, win_ref, wp_ref, wb_ref, wo_ref,
             dqm_ref, dp_ref, gx_ref, red_ref, pred_ref, dpgcarry, dcvcarry):
        i = pl.program_id(0)
        tix = nt - 1 - i

        @pl.when(i == 0)
        def _():
            red_ref[...] = jnp.zeros_like(red_ref)
            pred_ref[...] = jnp.zeros_like(pred_ref)
            dpgcarry[...] = jnp.zeros_like(dpgcarry)
            dcvcarry[...] = jnp.zeros_like(dcvcarry)

        pscale = vec_ref[2:3, :]
        dx1_v = dx1_ref[...]
        dob = _postnorm_bwd(dx1_v, o_ref[...].astype(F32), vec_ref[0:1, :], vec_ref[1:2, :], red_ref).astype(BF16)
        dqm_ref[:, d:2 * d] = dob
        dm = _dot_nt(dob, wo_ref[...])

        def dproj(cols, value):
            vb = value.astype(BF16)
            dp_ref[:, cols] = vb
            return _dot_nt(vb, win_ref[:, cols])

        sa = jax.nn.sigmoid(p5_ref[:, 3 * d:4 * d].astype(F32))
        yar = yar_ref[...].astype(F32)
        dya = dm * sa
        dh = dproj(slice(4 * d, 5 * d), dm * (yar * pscale) * sa * (1.0 - sa))
        red_ref[2:3, :] += _colsum(dya * yar)
        dyarb = (dya * pscale).astype(BF16)
        dqm_ref[:, 2 * d:3 * d] = dyarb
        sb = jax.nn.sigmoid(p5_ref[:, 4 * d:5 * d].astype(F32))
        dybb = (dm * sb).astype(BF16)
        dqm_ref[:, 0:d] = dybb
        dh = dh + dproj(slice(5 * d, 6 * d), dm * yb_ref[...].astype(F32) * sb * (1.0 - sb))

        for g, window in enumerate(POOL_WINDOWS):
            cols = slice(g * gw, (g + 1) * gw)
            rows = 8 * (window - 1)
            dpg = _dot_nt(dyarb[:, cols], wp_ref[g])
            dpgs = dpg * _inv_count(tix * tm, window)
            halo = _halo_bottom(dpgs[:rows, :], dpgcarry[:rows, cols])
            dpgcarry[:, cols] = dpgs[:pool_rows, :]
            s, shift = jnp.concatenate([dpgs, halo], axis=0), 1
            while shift < window:
                s = s[:s.shape[0] - 8 * shift, :] + s[8 * shift:, :]
                shift *= 2
            dh = dh + dproj(cols, s - dpg)

        dq = _dot_nt(dybb, wb_ref[...])
        ux = p5_ref[:, 0:d].astype(F32)
        uc = p5_ref[:, 2 * d:3 * d].astype(F32)
        dh = dh + dproj(slice(2 * d, 3 * d), dq * cv_ref[...].astype(F32))
        dcv = dq * p5_ref[:, d:2 * d].astype(F32)
        taps = (vec_ref[4:5, :], vec_ref[5:6, :], vec_ref[6:7, :])
        dpv, (t0, t1, t2) = _conv3_bwd(dcv, dcvcarry, slice(0, d), taps, uc * ux)
        red_ref[3:4, :] += _colsum(dcv)
        red_ref[4:5, :] += t0
        red_ref[5:6, :] += t1
        red_ref[6:7, :] += t2
        dh = dh + dproj(slice(d, 2 * d), dpv * uc)
        dh = dh + dproj(slice(3 * d, 4 * d), dpv * ux)

        _, r, nh = _prenorm(_interleave(x_ref[...]), vp_ref)
        gx_ref[...] = _deinterleave(dx1_v + _prenorm_bwd(dh, r, nh, vp_ref, pred_ref))

    rev = lambda n: pl.BlockSpec((tm, n), lambda i: (nt - 1 - i, 0))
    return _call(
        body, "mixer_block_bwd", (nt,), [rev(d)] * 5 + [rev(5 * d), rev(d)] + [_vmem()] * 6,
        [rev(3 * d), rev(6 * d), rev(d), pl.BlockSpec((16, d), lambda i: (0, 0)),
         pl.BlockSpec((8, d), lambda i: (0, 0))],
        [_sds((t, 3 * d), BF16), _sds((t, 6 * d), BF16), _sds((t, d), F32), _sds((16, d), F32), _sds((8, d), F32)],
        [pltpu.VMEM((pool_rows, d), F32), pltpu.VMEM((16, d), F32)],
        _params("arbitrary"), (dx1, o, yar, yb, cv, p5b, x, vec_pre, vec, w_in, w_pool, w_bout, w_o), job)


def _matmul_tn(a, b, bm, bn, tk, by_col_block, name, job=None):
    t, m = a.shape
    n = b.shape[1]
    nk = t // tk
    parts = int(by_col_block)
    piece = bn // max(parts, 1)
    wide = _round_up(piece, LANES)

    def body(a_ref, b_ref, o_ref, acc_ref):
        k = pl.program_id(2)

        @pl.when(k == 0)
        def _():
            acc_ref[...] = jnp.zeros_like(acc_ref)

        acc_ref[...] += _dot_tn(a_ref[...], b_ref[...])

        @pl.when(k == nk - 1)
        def _():
            if parts:
                acc = acc_ref[...]
                for p in range(parts):
                    if wide > piece:
                        o_ref[p] = jnp.zeros((bm, wide), o_ref.dtype)
                    o_ref[p, :, 0:piece] = acc[:, p * piece:(p + 1) * piece].astype(o_ref.dtype)
            else:
                o_ref[...] = acc_ref[...].astype(o_ref.dtype)

    if by_col_block:
        out_shape = _sds((parts * n // bn, m, wide), BF16)
        out_spec = pl.BlockSpec((parts, bm, wide), lambda i, j, k: (j, i, 0))
    else:
        out_shape = _sds((m, n), BF16)
        out_spec = pl.BlockSpec((bm, bn), lambda i, j, k: (i, j))
    out = _call(body, name, (m // bm, n // bn, nk),
                [pl.BlockSpec((tk, bm), lambda i, j, k: (k, i)), pl.BlockSpec((tk, bn), lambda i, j, k: (k, j))],
                [out_spec], [out_shape], [pltpu.VMEM((bm, bn), F32)],
                _params("arbitrary", "arbitrary", "arbitrary"), (a, b), job)
    return out if job is not None else out[0]


def _side_by_side(blocks, name):
    n, r, c = blocks.shape
    rb = _row_block(r) // 2

    def body(in_ref, o_ref):
        for j in range(n):
            o_ref[:, j * c:(j + 1) * c] = in_ref[j]

    return pl.pallas_call(
        body, name=name, grid=(r // rb,), in_specs=[pl.BlockSpec((n, rb, c), lambda i: (0, i, 0))],
        out_specs=pl.BlockSpec((rb, n * c), lambda i: (i, 0)), out_shape=_sds((r, n * c), blocks.dtype),
        compiler_params=_params("parallel"))(blocks)


def _matmul_tn_groups(a, b, groups, tk, name, job=None):
    t, m = a.shape
    w = m // groups
    nk = t // tk

    def body(a_ref, b_ref, o_ref, acc_ref):
        k = pl.program_id(1)

        @pl.when(k == 0)
        def _():
            acc_ref[...] = jnp.zeros_like(acc_ref)

        acc_ref[...] += _dot_tn(a_ref[...], b_ref[...])

        @pl.when(k == nk - 1)
        def _():
            o_ref[...] = acc_ref[...].astype(o_ref.dtype)

    blk = pl.BlockSpec((tk, w), lambda g, k: (k, g))
    out = _call(body, name, (groups, nk), [blk, blk], [pl.BlockSpec((None, w, w), lambda g, k: (g, 0, 0))],
                [_sds((groups, w, w), BF16)], [pltpu.VMEM((w, w), F32)], _params("arbitrary", "arbitrary"), (a, b), job)
    return out if job is not None else out[0]


def _round_up(n, k):
    return (n + k - 1) // k * k


def _rows8(rows, width):
    n = _round_up(len(rows), 8)
    rows = list(rows) + [jnp.zeros((1, width), F32)] * (n - len(rows))
    return jnp.concatenate(rows, axis=0)


def kernel(x, c, g_pre_mix, g_post_mix, g_pre_ffn, g_post_ffn, w_ada, b_ada, w_in, w_pool, pool_scale, conv_w, conv_b, w_bout, w_o, w_up, ffn_conv_w, ffn_conv_b, w_down, loss_target, m_g_pre_mix, m_g_post_mix, m_g_pre_ffn, m_g_post_ffn, m_w_ada, m_b_ada, m_w_in, m_w_pool, m_pool_scale, m_conv_w, m_conv_b, m_w_bout, m_w_o, m_w_up, m_ffn_conv_w, m_ffn_conv_b, m_w_down, v_g_pre_mix, v_g_post_mix, v_g_pre_ffn, v_g_post_ffn, v_w_ada, v_b_ada, v_w_in, v_w_pool, v_pool_scale, v_conv_w, v_conv_b, v_w_bout, v_w_o, v_w_up, v_ffn_conv_w, v_ffn_conv_b, v_w_down):
    t, d = x.shape[1], x.shape[2]
    ngroups = len(POOL_WINDOWS)
    gw = d // ngroups
    ada_n = w_ada.shape[2]
    in_n = w_in.shape[2]
    up_n = w_up.shape[2]
    fp = NDEV * w_down.shape[1]
    assert x.shape[0] == 1 and t % GROUP == 0, "one sequence per device, a whole number of token groups"
    assert gw % LANES == 0 and in_n % LANES == 0 and ada_n % LANES == 0 and NDEV * in_n == 6 * d
    assert NDEV * up_n == 2 * fp and fp % (2 * LANES) == 0, "gate and value halves cut into lane-aligned chunks"

    xi, yi, ci = _position()
    me = _linear(xi, yi, ci)
    chip = 2 * xi + yi
    core = jnp.reshape(ci, (1,)).astype(jnp.int32)
    sel = jnp.stack([2 * chip + ci, chip]).astype(jnp.int32)

    x2 = x.reshape(t, d)
    target = loss_target.reshape(t, d)

    cw_n = conv_w.shape[2]
    pack = jnp.concatenate([c.reshape(1, d), conv_w[0].reshape(1, 3 * cw_n), ffn_conv_w[0].reshape(1, 3 * up_n)], axis=1)
    pack = jnp.pad(pack, ((0, 0), (0, _round_up(pack.shape[1], LANES) - pack.shape[1])))
    b_piece = lax.dynamic_slice_in_dim(b_ada, me * ada_n, ada_n, axis=1)
    mixer_weights = _allgather_job(
        [w_in[0].astype(BF16), w_bout[0].astype(BF16), w_o[0].astype(BF16), w_pool[0].astype(BF16)],
        ["cols", "rows", "rows", "mid"], 0.5, 0.75)
    learned = _rows8([g_pre_mix, g_post_mix, g_pre_ffn, g_post_ffn, pool_scale, conv_b], d)
    gathered, vec_pre_mix, vec_mix, vec_pre_ffn, vec_ffn, fcv, w_in_f, g_bout, g_o, w_pool_f = (
        _gather_weights_and_modulation(mixer_weights, pack, w_ada[0], b_piece, learned, ffn_conv_b, d, cw_n, up_n))
    w_bout_f = g_bout.reshape(d, d)
    w_o_f = g_o.reshape(d, d)
    c16 = jnp.pad(gathered[:, :d], ((0, 8), (0, 0))).astype(BF16)

    ffn_weights = _allgather_job([w_up[0].astype(BF16), w_down[0].astype(BF16)], ["rows", "rows"], 0.5, 0.8)
    h1b, p5b, qmb, cv, yar, yb, o, x1, g_up, g_down = _mixer_block_fwd(
        x2, vec_pre_mix, vec_mix, w_in_f, w_pool_f, w_bout_f, w_o_f, ffn_weights)
    w_up_f = _side_by_side(g_up, "w_up_side_by_side")
    w_down_f = g_down.reshape(fp, d)
    h2b, upb, upreb, ab, ffb, dy, loss_part = _ffn_block_fwd(x1, target, vec_pre_ffn, vec_ffn, fcv, w_up_f, w_down_f)

    tk, tk_wide = min(4096, t), min(2048, t)
    chip_sum = lambda gs, ss, name: _chip_sums(gs, ss, core, name)
    dffb, dupre, dx1, red_ffn, red_fconv, red_pre_ffn = _ffn_block_bwd(
        dy, ffb, x1, upb, upreb, vec_pre_ffn, vec_ffn, fcv, w_up_f, w_down_f)
    chunk = fp // _column_chunks(fp)
    gr_up = _matmul_tn(h2b, dupre, d, chunk, tk_wide, chunk // up_n, "wgrad_up")
    gr_down, sib_up = _matmul_tn(ab, dffb, chunk, d, tk_wide, False, "wgrad_down", _sibling_job([gr_up]))
    gr_down = gr_down.reshape(NDEV, fp // NDEV, d)
    sib_ffn = [sib_up] + list(_run_job(_sibling_job([gr_down]), "rs_sibling_down"))
    dqmb, dproj, grad_x, red_mix, red_pre_mix = _mixer_block_bwd(
        dx1, o, yar, yb, cv, p5b, x2, vec_pre_mix, vec_mix, w_in_f, w_pool_f, w_bout_f, w_o_f, None)
    gr_in, fc_up, fc_down = _matmul_tn(h1b, dproj, d, in_n, tk, True, "wgrad_in",
                                       _chips_job(chip_sum([gr_up, gr_down], sib_ffn, "rs_chip_sum_ffn")))
    dmod = [red_pre_mix[0:1], red_pre_mix[1:2], red_mix[1:2], red_pre_ffn[0:1], red_pre_ffn[1:2], red_ffn[1:2]]
    small = [red_pre_mix[2:3], red_mix[0:1], red_pre_ffn[2:3], red_ffn[0:1], red_mix[2:3], red_mix[3:4],
             red_mix[4:5], red_mix[5:6], red_mix[6:7]] + dmod
    flat = jnp.concatenate(small + [red_fconv[0:4].reshape(1, 8 * fp), loss_part[0:1, 0:1]], axis=1)
    flat_n = flat.shape[1]
    width = 8 * LANES
    rows = _round_up(-(-flat_n // width), 8)
    flat = jnp.pad(flat, ((0, 0), (0, rows * width - flat_n))).reshape(rows, width)
    gat, tot, *sib_in = _small_allreduce(flat, "allreduce_small_rs_sibling_in", _sibling_job([gr_in]))
    gr_qmp, fc_in = _matmul_tn_groups(qmb, dqmb, 3, tk, "wgrad_bout_o_pool",
                                      _chips_job(chip_sum([gr_in], sib_in, "rs_chip_sum_in")))
    gr_bout = gr_qmp[0].reshape(NDEV, d // NDEV, d)
    gr_o = gr_qmp[1].reshape(NDEV, d // NDEV, d)
    gr_pool = jnp.stack([gr_qmp[2, g * gw:(g + 1) * gw, g * gw:(g + 1) * gw] for g in range(ngroups)])
    gr_pool = gr_pool.reshape(ngroups, NDEV, gw // NDEV, gw).transpose(1, 0, 2, 3).reshape(NDEV, -1, gw)
    rest = [gr_bout, gr_o, gr_pool]
    sib_rest = _run_job(_sibling_job(rest), "rs_sibling_rest")
    fc_rest = _run_job(_chips_job(chip_sum(rest, sib_rest, "rs_chip_sum_rest")), "rs_chips_rest")

    def big(grad, from_sibling, from_chips, w, m, v, name, transposed=False):
        shape = w.shape
        flat = (lambda a: a[0].T) if transposed else (lambda a: a.reshape((-1, shape[-1])))
        outs = _reduce_adamw(grad, from_sibling, from_chips, sel, flat(w), flat(m), flat(v), name, transposed)
        return [(a.T if transposed else a).reshape(shape) for a in outs]

    g_w_up, d_w_up, nm_w_up, nv_w_up = big(gr_up, sib_ffn[0], fc_up, w_up, m_w_up, v_w_up, "adamw_up",
                                           transposed=up_n % LANES != 0)
    g_w_down, d_w_down, nm_w_down, nv_w_down = big(gr_down, sib_ffn[1], fc_down, w_down, m_w_down, v_w_down, "adamw_down")
    g_w_in, d_w_in, nm_w_in, nv_w_in = big(gr_in, sib_in[0], fc_in, w_in, m_w_in, v_w_in, "adamw_in")
    flat2 = lambda a: a.reshape((-1, a.shape[-1]))
    rest_w = [(w_bout, m_w_bout, v_w_bout), (w_o, m_w_o, v_w_o), (w_pool, m_w_pool, v_w_pool)]
    rest_out = _reduce_adamw_group(
        [(g, s, f, flat2(w), flat2(m), flat2(v)) for g, s, f, (w, m, v) in zip(rest, sib_rest, fc_rest, rest_w)],
        sel, "adamw_bout_o_pool")
    (g_w_bout, d_w_bout, nm_w_bout, nv_w_bout), (g_w_o, d_w_o, nm_w_o, nv_w_o), (g_w_pool, d_w_pool, nm_w_pool, nv_w_pool) = [
        [a.reshape(w.shape) for a in outs] for outs, (w, _, _) in zip(rest_out, rest_w)]

    tot = tot.reshape(1, rows * width)
    gat = gat.reshape(NDEV, rows * width)
    take = lambda k: tot[:, k * d:(k + 1) * d]
    g_g_pre_mix, g_g_post_mix, g_g_pre_ffn, g_g_post_ffn, g_pool_scale, g_conv_b = [take(k) for k in range(6)]
    g_conv_w_full = jnp.concatenate([take(6), take(7), take(8)], axis=0)
    g_conv_w = lax.dynamic_slice_in_dim(g_conv_w_full, me * cw_n, cw_n, axis=1)
    g_b_ada = tot[:, 9 * d:15 * d]
    dmod_all = gat[:, 9 * d:15 * d]
    fconv_tot = tot[:, 15 * d:15 * d + 8 * fp].reshape(4, 2 * fp)
    loss = 0.5 * tot[0, 15 * d + 8 * fp]
    g_ffn_conv_b = fconv_tot[3:4]
    g_ffn_conv_w = lax.dynamic_slice_in_dim(fconv_tot[0:3], me * up_n, up_n, axis=1)
    dmod_piece = lax.dynamic_slice_in_dim(dmod_all, me * ada_n, ada_n, axis=1)
    g_w_ada = _wada_grad(c16, jnp.pad(dmod_piece, ((0, 8), (0, 0))).astype(BF16))

    names_small = [(g_pre_mix, g_g_pre_mix, m_g_pre_mix, v_g_pre_mix), (g_post_mix, g_g_post_mix, m_g_post_mix, v_g_post_mix),
                   (g_pre_ffn, g_g_pre_ffn, m_g_pre_ffn, v_g_pre_ffn), (g_post_ffn, g_g_post_ffn, m_g_post_ffn, v_g_post_ffn),
                   (b_ada, g_b_ada, m_b_ada, v_b_ada), (pool_scale, g_pool_scale, m_pool_scale, v_pool_scale),
                   (conv_w, g_conv_w, m_conv_w, v_conv_w), (conv_b, g_conv_b, m_conv_b, v_conv_b),
                   (ffn_conv_w, g_ffn_conv_w, m_ffn_conv_w, v_ffn_conv_w), (ffn_conv_b, g_ffn_conv_b, m_ffn_conv_b, v_ffn_conv_b)]
    small_out = _adamw_group([[a.reshape((-1, a.shape[-1])) for a in q] for q in names_small], "adamw_small")
    unpack_small = lambda k: [outs[k].reshape(q[0].shape) for outs, q in zip(small_out, names_small)]
    (d_g_pre_mix, d_g_post_mix, d_g_pre_ffn, d_g_post_ffn, d_b_ada, d_pool_scale, d_conv_w, d_conv_b,
     d_ffn_conv_w, d_ffn_conv_b) = unpack_small(0)
    (nm_g_pre_mix, nm_g_post_mix, nm_g_pre_ffn, nm_g_post_ffn, nm_b_ada, nm_pool_scale, nm_conv_w, nm_conv_b,
     nm_ffn_conv_w, nm_ffn_conv_b) = unpack_small(1)
    (nv_g_pre_mix, nv_g_post_mix, nv_g_pre_ffn, nv_g_post_ffn, nv_b_ada, nv_pool_scale, nv_conv_w, nv_conv_b,
     nv_ffn_conv_w, nv_ffn_conv_b) = unpack_small(2)
    d_w_ada, nm_w_ada, nv_w_ada = [a.reshape(w_ada.shape) for a in
                                   _adamw(w_ada[0], g_w_ada, m_w_ada[0], v_w_ada[0], "adamw_ada")]

    grads = [g_g_pre_mix, g_g_post_mix, g_g_pre_ffn, g_g_post_ffn, g_w_ada.reshape(w_ada.shape), g_b_ada, g_w_in,
             g_w_pool, g_pool_scale, g_conv_w.reshape(conv_w.shape), g_conv_b, g_w_bout, g_w_o, g_w_up,
             g_ffn_conv_w.reshape(ffn_conv_w.shape), g_ffn_conv_b, g_w_down]
    deltas = [d_g_pre_mix, d_g_post_mix, d_g_pre_ffn, d_g_post_ffn, d_w_ada, d_b_ada, d_w_in, d_w_pool, d_pool_scale,
              d_conv_w, d_conv_b, d_w_bout, d_w_o, d_w_up, d_ffn_conv_w, d_ffn_conv_b, d_w_down]
    new_m = [nm_g_pre_mix, nm_g_post_mix, nm_g_pre_ffn, nm_g_post_ffn, nm_w_ada, nm_b_ada, nm_w_in, nm_w_pool,
             nm_pool_scale, nm_conv_w, nm_conv_b, nm_w_bout, nm_w_o, nm_w_up, nm_ffn_conv_w, nm_ffn_conv_b, nm_w_down]
    new_v = [nv_g_pre_mix, nv_g_post_mix, nv_g_pre_ffn, nv_g_post_ffn, nv_w_ada, nv_b_ada, nv_w_in, nv_w_pool,
             nv_pool_scale, nv_conv_w, nv_conv_b, nv_w_bout, nv_w_o, nv_w_up, nv_ffn_conv_w, nv_ffn_conv_b, nv_w_down]
    return (loss, grad_x.reshape(x.shape), *grads, *deltas, *new_m, *new_v)
```

```python
import math

import jax
import jax.numpy as jnp
from jax import lax
from jax.experimental import pallas as pl
from jax.experimental.pallas import tpu as pltpu

F32 = jnp.float32
BF16 = jnp.bfloat16
MESH = pl.DeviceIdType.MESH

NDEV = 8
NCHIP = 4
EPS = 1e-6
POOL_WINDOWS = (2, 4, 8, 16)
LANES = 128
ADAM_LR = 0.001
ADAM_B1 = 0.9
ADAM_B2 = 0.999
ADAM_EPS = 1e-08
ADAM_WD = 0.01
ADAM_STEP = 10
GELU_C0 = math.sqrt(2.0 / math.pi)
GELU_C1 = 0.044715
VMEM_LIMIT = 56 * 2**20


def _vmem():
    return pl.BlockSpec(memory_space=pltpu.VMEM)


def _any():
    return pl.BlockSpec(memory_space=pl.ANY)


def _params(*sem):
    return pltpu.CompilerParams(dimension_semantics=sem, vmem_limit_bytes=VMEM_LIMIT)


def _sds(shape, dtype):
    return jax.ShapeDtypeStruct(tuple(shape), dtype)


def _position():
    return lax.axis_index("x"), lax.axis_index("y"), lax.axis_index("c")


def _linear(x, y, c):
    return 4 * x + 2 * y + c


def _dot(a, b):
    return jnp.dot(a, b, preferred_element_type=F32)


def _dot_nt(a, b):
    return lax.dot_general(a, b, (((1,), (1,)), ((), ())), preferred_element_type=F32)


def _dot_tn(a, b):
    return lax.dot_general(a, b, (((0,), (0,)), ((), ())), preferred_element_type=F32)


def _colsum(v):
    return jnp.sum(v, axis=0, keepdims=True)


def _rowmean(v):
    return jnp.mean(v, axis=-1, keepdims=True)


GROUP = 256


def _interleave(v):
    g, n = v.shape
    return jnp.swapaxes(v.reshape(8, g // 8, n), 0, 1).reshape(g, n)


def _deinterleave(v):
    g, n = v.shape
    return jnp.swapaxes(v.reshape(g // 8, 8, n), 0, 1).reshape(g, n)


def _halo_top(cur_last, prev_last):
    rows, n = cur_last.shape
    c3 = cur_last.reshape(rows // 8, 8, n)
    p3 = prev_last.reshape(rows // 8, 8, n)
    sub = lax.broadcasted_iota(jnp.int32, c3.shape, 1)
    return jnp.where(sub == 0, pltpu.roll(p3, 1, 1), pltpu.roll(c3, 1, 1)).reshape(rows, n)


def _halo_bottom(cur_first, next_first):
    rows, n = cur_first.shape
    c3 = cur_first.reshape(rows // 8, 8, n)
    n3 = next_first.reshape(rows // 8, 8, n)
    sub = lax.broadcasted_iota(jnp.int32, c3.shape, 1)
    return jnp.where(sub == 7, pltpu.roll(n3, 7, 1), pltpu.roll(c3, 7, 1)).reshape(rows, n)


def _shift_down(v, halo, k):
    rows = v.shape[0]
    return jnp.concatenate([halo[halo.shape[0] - 8 * k:, :], v[:rows - 8 * k, :]], axis=0)


def _shift_up(v, halo, k):
    return jnp.concatenate([v[8 * k:, :], halo[:8 * k, :]], axis=0)


def _inv_count(first_token, window):
    row = lax.broadcasted_iota(jnp.int32, (GROUP, 1), 0)
    t = first_token + (row % 8) * (GROUP // 8) + row // 8
    return 1.0 / jnp.minimum(t + 1, window).astype(F32)


class _Job:
    def __init__(self, inputs, out_shape, scratch, phases):
        self.inputs, self.out_shape, self.scratch, self.phases = list(inputs), list(out_shape), list(scratch), phases


def _call(body, name, grid, in_specs, out_specs, out_shape, scratch_shapes, params, operands, job=None):
    if job is None:
        return pl.pallas_call(body, name=name, grid=grid, in_specs=in_specs, out_specs=out_specs, out_shape=out_shape,
                              scratch_shapes=scratch_shapes, compiler_params=params)(*operands)
    n_in, n_out, n_scr = len(in_specs), len(out_specs), len(scratch_shapes)
    j_in, j_out = len(job.inputs), len(job.out_shape)
    steps = math.prod(grid)

    def hosted(*refs):
        own_in, refs = refs[:n_in], refs[n_in:]
        jin, refs = refs[:j_in], refs[j_in:]
        own_out, refs = refs[:n_out], refs[n_out:]
        jout, refs = refs[:j_out], refs[j_out:]
        own_scr, jscr = refs[:n_scr], refs[n_scr:]
        step = pl.program_id(0)
        for axis in range(1, len(grid)):
            step = step * grid[axis] + pl.program_id(axis)
        for frac, fn in job.phases[:-1]:
            pl.when(step == int(frac * (steps - 1)))(lambda fn=fn: fn(jin, jout, jscr))
        body(*own_in, *own_out, *own_scr)
        pl.when(step == steps - 1)(lambda: job.phases[-1][1](jin, jout, jscr))

    return pl.pallas_call(
        hosted, name=name, grid=grid, in_specs=list(in_specs) + [_any()] * j_in,
        out_specs=list(out_specs) + [_any()] * j_out, out_shape=list(out_shape) + job.out_shape,
        scratch_shapes=list(scratch_shapes) + job.scratch, compiler_params=params)(*operands, *job.inputs)


def _run_job(job, name):
    n_in, n_out = len(job.inputs), len(job.out_shape)

    def body(*refs):
        for _, fn in job.phases:
            fn(refs[:n_in], refs[n_in:n_in + n_out], refs[n_in + n_out:])

    return pl.pallas_call(body, name=name, out_shape=job.out_shape, in_specs=[_any()] * n_in,
                          out_specs=[_any()] * n_out, scratch_shapes=job.scratch)(*job.inputs)


def _peers(x, y, c):
    out = []
    for k in range(1, NDEV):
        out.append(((1 - x) if k & 4 else x, (1 - y) if k & 2 else y, (1 - c) if k & 1 else c))
    return out


def _small_allreduce(v, name, job):
    r, n = v.shape
    j_in, j_out = len(job.inputs), len(job.out_shape)

    def body(v_ref, *rest):
        jin, rest = rest[:j_in], rest[j_in:]
        gat_ref, sum_ref = rest[:2]
        jout, rest = rest[2:2 + j_out], rest[2 + j_out:]
        send_sems, recv_sems, local_sem = rest[:3]
        jscr = rest[3:]
        job.phases[0][1](jin, jout, jscr)
        x, y, c = _position()
        me = _linear(x, y, c)
        mine = pltpu.make_async_copy(v_ref, gat_ref.at[me], local_sem)
        mine.start()
        peers = _peers(x, y, c)
        sends = []
        for k, peer in enumerate(peers):
            cp = pltpu.make_async_remote_copy(src_ref=v_ref, dst_ref=gat_ref.at[me], send_sem=send_sems.at[k],
                                              recv_sem=recv_sems.at[k], device_id=peer, device_id_type=MESH)
            cp.start()
            sends.append(cp)
        for k, peer in enumerate(peers):
            pltpu.make_async_remote_copy(src_ref=v_ref, dst_ref=gat_ref.at[_linear(*peer)], send_sem=send_sems.at[k],
                                         recv_sem=recv_sems.at[k], device_id=peer, device_id_type=MESH).wait_recv()
        for cp in sends:
            cp.wait_send()
        mine.wait()
        acc = gat_ref[0]
        for j in range(1, NDEV):
            acc = acc + gat_ref[j]
        sum_ref[...] = acc
        job.phases[-1][1](jin, jout, jscr)

    return pl.pallas_call(
        body, name=name, out_shape=[_sds((NDEV, r, n), F32), _sds((r, n), F32)] + job.out_shape,
        in_specs=[_vmem()] + [_any()] * j_in, out_specs=[_vmem()] * 2 + [_any()] * j_out,
        scratch_shapes=[pltpu.SemaphoreType.DMA((NDEV - 1,)), pltpu.SemaphoreType.DMA((NDEV - 1,)),
                        pltpu.SemaphoreType.DMA(())] + job.scratch,
    )(v, *job.inputs)


def _exchange_rows(src_for, dst_ref, sems):
    send_sems, recv_sems, local_sem = sems
    x, y, c = _position()
    me = _linear(x, y, c)
    row = lambda j: dst_ref.at[pl.ds(j, 1), :]
    mine = pltpu.make_async_copy(src_for(me), row(me), local_sem)
    mine.start()
    peers = _peers(x, y, c)
    sends = []
    for k, peer in enumerate(peers):
        cp = pltpu.make_async_remote_copy(src_ref=src_for(_linear(*peer)), dst_ref=row(me), send_sem=send_sems.at[k],
                                          recv_sem=recv_sems.at[k], device_id=peer, device_id_type=MESH)
        cp.start()
        sends.append(cp)
    for k, peer in enumerate(peers):
        pltpu.make_async_remote_copy(src_ref=src_for(me), dst_ref=row(_linear(*peer)), send_sem=send_sems.at[k],
                                     recv_sem=recv_sems.at[k], device_id=peer, device_id_type=MESH).wait_recv()
    for cp in sends:
        cp.wait_send()
    mine.wait()


def _gather_weights_and_modulation(job, pack, w_ada, b_piece, learned, fcb, d, cw_n, up_n):
    n = pack.shape[1]
    m = w_ada.shape[1]
    j_in, j_out = len(job.inputs), len(job.out_shape)
    row_sems = [pltpu.SemaphoreType.DMA((NDEV - 1,)), pltpu.SemaphoreType.DMA((NDEV - 1,)), pltpu.SemaphoreType.DMA(())]

    def body(pack_ref, wada_ref, bp_ref, lrn_ref, fcb_ref, *rest):
        jin, rest = rest[:j_in], rest[j_in:]
        gat_ref, vpm_ref, vmix_ref, vpf_ref, vffn_ref, fcv_ref = rest[:6]
        jout, rest = rest[6:6 + j_out], rest[6 + j_out:]
        sems1, sems2, piece, mod_ref, modm, jscr = rest[0:3], rest[3:6], rest[6], rest[7], rest[8], rest[9:]
        phases = [fn for _, fn in job.phases]
        phases[0](jin, jout, jscr)
        _exchange_rows(lambda j: pack_ref, gat_ref, sems1)
        c16 = jnp.concatenate([gat_ref[:, 0:d], jnp.zeros((NDEV, d), F32)], axis=0).astype(BF16)
        piece[...] = (_dot(c16, wada_ref[...].astype(BF16)) + bp_ref[...])[0:NDEV, :]
        _exchange_rows(lambda j: piece.at[pl.ds(j, 1), :], mod_ref, sems2)

        for j in range(NDEV):
            done = 0
            while done < m:
                row, col = divmod(j * m + done, d)
                width = min(m - done, d - col)
                modm[row:row + 1, col:col + width] = mod_ref[j:j + 1, done:done + width]
                done += width
        for ref in (vpm_ref, vmix_ref, vpf_ref, vffn_ref, fcv_ref):
            ref[...] = jnp.zeros_like(ref)
        for ref, g_row, k in ((vpm_ref, 0, 0), (vpf_ref, 2, 3)):
            ref[0:1, :] = lrn_ref[g_row:g_row + 1, :]
            ref[1:2, :] = 1.0 + modm[k + 1:k + 2, :]
            ref[2:3, :] = modm[k:k + 1, :]
        vmix_ref[0:1, :] = modm[2:3, :]
        vmix_ref[1:2, :] = lrn_ref[1:2, :]
        vmix_ref[2:4, :] = lrn_ref[4:6, :]
        vffn_ref[0:1, :] = modm[5:6, :]
        vffn_ref[1:2, :] = lrn_ref[3:4, :]
        fcv_ref[3:4, :] = fcb_ref[...]
        for j in range(NDEV):
            for k in range(3):
                vmix_ref[4 + k:5 + k, j * cw_n:(j + 1) * cw_n] = gat_ref[j:j + 1, d + k * cw_n:d + (k + 1) * cw_n]
                off = d + 3 * cw_n + k * up_n
                fcv_ref[k:k + 1, j * up_n:(j + 1) * up_n] = gat_ref[j:j + 1, off:off + up_n]

        for fn in phases[1:]:
            fn(jin, jout, jscr)

    tables = [_sds((8, d), F32), _sds((16, d), F32), _sds((8, d), F32), _sds((8, d), F32), _sds((8, NDEV * up_n), F32)]
    return pl.pallas_call(
        body, name="gather_weights_and_modulation",
        out_shape=[_sds((NDEV, n), F32)] + tables + job.out_shape,
        in_specs=[_vmem()] * 5 + [_any()] * j_in, out_specs=[_vmem()] * 6 + [_any()] * j_out,
        scratch_shapes=row_sems + row_sems + [pltpu.VMEM((NDEV, m), F32), pltpu.VMEM((NDEV, m), F32),
                                              pltpu.VMEM((8, d), F32)] + job.scratch,
    )(pack, w_ada, b_piece, learned, fcb, *job.inputs)


def _gathered(shard, layout):
    if layout == "rows":
        return (NDEV,) + shard.shape, lambda ref, j: ref.at[j]
    if layout == "cols":
        r, c = shard.shape
        return (r, NDEV * c), lambda ref, j: ref.at[:, pl.ds(pl.multiple_of(j * c, LANES), c)]
    g, r, c = shard.shape
    return (g, NDEV * r, c), lambda ref, j: ref.at[:, pl.ds(pl.multiple_of(j * r, 16), r), :]


def _allgather_job(shards, layouts, relay_at, forward_at):
    n = len(shards)
    specs = [_gathered(s, l) for s, l in zip(shards, layouts)]
    halves = [s.shape[0] // 2 for s in shards]

    def plan(src, dst, sems):
        send_sems, recv_sems, _ = sems
        x, y, c = _position()
        me, sibling, xn, yn, dg = (x, y, c), (x, y, 1 - c), (1 - x, y, c), (x, 1 - y, c), (1 - x, 1 - y, c)

        def copy(a, k, block, to, half=None, from_src=False):
            blk = specs[a][1](dst[a], _linear(*block))
            if half is not None:
                blk = blk.at[pl.ds(half * halves[a], halves[a])]
            return pltpu.make_async_remote_copy(src_ref=src[a] if from_src else blk, dst_ref=blk,
                                                send_sem=send_sems.at[a, k], recv_sem=recv_sems.at[a, k],
                                                device_id=to, device_id_type=MESH)
        return copy, me, sibling, xn, yn, dg

    def local(src, dst, sems):
        x, y, c = _position()
        return [pltpu.make_async_copy(src[a], specs[a][1](dst[a], _linear(x, y, c)), sems[2].at[a]) for a in range(n)]

    def own(src, dst, sems):
        copy, me, sibling, xn, yn, dg = plan(src, dst, sems)
        return [copy(a, k, me, to, from_src=True) for k, to in ((1, xn), (2, yn), (0, sibling)) for a in range(n)]

    def relayed(src, dst, sems):
        copy, me, sibling, xn, yn, dg = plan(src, dst, sems)
        return ([copy(a, 3, xn, yn, half=0) for a in range(n)] + [copy(a, 5, xn, sibling) for a in range(n)],
                [copy(a, 4, yn, xn, half=1) for a in range(n)] + [copy(a, 6, yn, sibling) for a in range(n)])

    def diagonal(src, dst, sems):
        copy, me, sibling, xn, yn, dg = plan(src, dst, sems)
        return [copy(a, 7, dg, sibling) for a in range(n)]

    def start(src, dst, sems):
        for cp in local(src, dst, sems) + own(src, dst, sems):
            cp.start()

    def relay(src, dst, sems):
        copy, me, sibling, xn, yn, dg = plan(src, dst, sems)
        from_x, from_y = relayed(src, dst, sems)
        for a in range(n):
            copy(a, 1, xn, me).wait_recv()
        for cp in from_x:
            cp.start()
        for a in range(n):
            copy(a, 2, yn, me).wait_recv()
        for cp in from_y:
            cp.start()

    def forward(src, dst, sems):
        copy, me, sibling, xn, yn, dg = plan(src, dst, sems)
        for a in range(n):
            copy(a, 3, dg, me, half=0).wait_recv()
            copy(a, 4, dg, me, half=1).wait_recv()
        for cp in diagonal(src, dst, sems):
            cp.start()

    def finish(src, dst, sems):
        copy, me, sibling, xn, yn, dg = plan(src, dst, sems)
        other = lambda dev: (dev[0], dev[1], 1 - dev[2])
        for a in range(n):
            copy(a, 0, sibling, me).wait_recv()
            for k, dev in ((5, xn), (6, yn), (7, dg)):
                copy(a, k, other(dev), me).wait_recv()
        from_x, from_y = relayed(src, dst, sems)
        for cp in own(src, dst, sems) + from_x + from_y + diagonal(src, dst, sems):
            cp.wait_send()
        for cp in local(src, dst, sems):
            cp.wait()

    return _Job(shards, [_sds(spec[0], s.dtype) for spec, s in zip(specs, shards)],
                [pltpu.SemaphoreType.DMA((n, 8)), pltpu.SemaphoreType.DMA((n, 8)), pltpu.SemaphoreType.DMA((n,))],
                [(0.0, start), (relay_at, relay), (forward_at, forward), (1.0, finish)])


def _sibling_job(grads):
    n = len(grads)

    def copies(src, dst, sems):
        x, y, c = _position()
        return [pltpu.make_async_remote_copy(src_ref=src[a].at[2 * q + 1 - c], dst_ref=dst[a].at[q],
                                             send_sem=sems[0].at[a, q], recv_sem=sems[1].at[a, q],
                                             device_id=(x, y, 1 - c), device_id_type=MESH)
                for a in range(n) for q in range(NCHIP)]

    return _exchange_job(grads, NCHIP, copies)


def _chips_job(chip_sums):
    n = len(chip_sums)

    def copies(src, dst, sems):
        x, y, c = _position()
        chips = [(1 - x, y), (x, 1 - y), (1 - x, 1 - y)]
        return [pltpu.make_async_remote_copy(src_ref=src[a].at[2 * chip[0] + chip[1]], dst_ref=dst[a].at[j],
                                             send_sem=sems[0].at[a, j], recv_sem=sems[1].at[a, j],
                                             device_id=(*chip, c), device_id_type=MESH)
                for j, chip in enumerate(chips) for a in range(n)]

    return _exchange_job(chip_sums, 3, copies)


def _exchange_job(arrays, slots, copies):
    n = len(arrays)

    def start(src, dst, sems):
        for cp in copies(src, dst, sems):
            cp.start()

    def finish(src, dst, sems):
        cps = copies(src, dst, sems)
        for cp in cps:
            cp.wait_recv()
        for cp in cps:
            cp.wait_send()

    return _Job(arrays, [_sds((slots,) + a.shape[1:], a.dtype) for a in arrays],
                [pltpu.SemaphoreType.DMA((n, slots)), pltpu.SemaphoreType.DMA((n, slots))],
                [(0.0, start), (1.0, finish)])


def _row_block(r):
    if r <= 512:
        return r
    for rb in range(512, 15, -16):
        if r % rb == 0:
            return rb
    return r


def _chip_sums(grads, from_sibling, core, name):
    n = len(grads)

    def body(core_ref, *refs):
        del core_ref
        for a in range(n):
            refs[2 * n + a][...] = (refs[a][...].astype(F32) + refs[n + a][...].astype(F32)).astype(BF16)

    block = lambda g, index: pl.BlockSpec((None,) + g.shape[1:], index)
    grid_spec = pltpu.PrefetchScalarGridSpec(
        num_scalar_prefetch=1, grid=(NCHIP,),
        in_specs=[block(g, lambda q, core: (2 * q + core[0], 0, 0)) for g in grads]
        + [block(g, lambda q, core: (q, 0, 0)) for g in grads],
        out_specs=[block(g, lambda q, core: (q, 0, 0)) for g in grads])
    return pl.pallas_call(body, name=name, grid_spec=grid_spec,
                          out_shape=[_sds((NCHIP,) + g.shape[1:], BF16) for g in grads],
                          compiler_params=_params("parallel"))(core, *grads, *from_sibling)


def _adamw_math(w, g, m, v):
    m2 = ADAM_B1 * m + (1.0 - ADAM_B1) * g
    v2 = ADAM_B2 * v + (1.0 - ADAM_B2) * jnp.square(g)
    m_hat = m2 / (1.0 - ADAM_B1 ** ADAM_STEP)
    v_hat = v2 / (1.0 - ADAM_B2 ** ADAM_STEP)
    delta = -ADAM_LR * (m_hat / (jnp.sqrt(v_hat) + ADAM_EPS) + ADAM_WD * w)
    return delta, m2, v2


def _adamw(w, g, m, v, name):
    r, c = w.shape
    rb = _row_block(r)

    def body(w_ref, g_ref, m_ref, v_ref, d_ref, m2_ref, v2_ref):
        d, m2, v2 = _adamw_math(w_ref[...], g_ref[...], m_ref[...], v_ref[...])
        d_ref[...] = d
        m2_ref[...] = m2
        v2_ref[...] = v2

    blk = pl.BlockSpec((rb, c), lambda i: (i, 0))
    return pl.pallas_call(body, name=name, grid=(r // rb,), in_specs=[blk] * 4, out_specs=[blk] * 3,
                          out_shape=[_sds((r, c), F32)] * 3, compiler_params=_params("parallel"))(w, g, m, v)


def _adamw_group(items, name):
    n = len(items)

    def body(*refs):
        ins, outs = refs[:4 * n], refs[4 * n:]
        for a in range(n):
            w_ref, g_ref, m_ref, v_ref = ins[4 * a:4 * a + 4]
            for ref, val in zip(outs[3 * a:3 * a + 3], _adamw_math(w_ref[...], g_ref[...], m_ref[...], v_ref[...])):
                ref[...] = val

    outs = pl.pallas_call(body, name=name, in_specs=[_vmem()] * (4 * n), out_specs=[_vmem()] * (3 * n),
                          out_shape=[_sds(q[0].shape, F32) for q in items for _ in range(3)])(
        *[a for q in items for a in q])
    return [outs[3 * a:3 * a + 3] for a in range(n)]


def _reduce_adamw(grad, from_sibling, from_chips, sel, w, m, v, name, transposed=False, job=None):
    r, c = w.shape[::-1] if transposed else w.shape
    cp = grad.shape[2]
    rb = _row_block(r)

    def body(sel_ref, g_ref, s_ref, c0_ref, c1_ref, c2_ref, w_ref, m_ref, v_ref, go_ref, d_ref, m2_ref, v2_ref):
        del sel_ref
        g = g_ref[...].astype(F32) + s_ref[...].astype(F32)
        g = g + c0_ref[...].astype(F32)
        g = g + c1_ref[...].astype(F32)
        g = g + c2_ref[...].astype(F32)
        g = g.T[0:c, :] if transposed else g[:, 0:c]
        d, m2, v2 = _adamw_math(w_ref[...], g, m_ref[...], v_ref[...])
        go_ref[...] = g
        d_ref[...] = d
        m2_ref[...] = m2
        v2_ref[...] = v2

    blk = pl.BlockSpec((c, rb), lambda i, sel: (0, i)) if transposed else pl.BlockSpec((rb, c), lambda i, sel: (i, 0))
    in_specs = [pl.BlockSpec((None, rb, cp), lambda i, sel: (sel[0], i, 0)),
                pl.BlockSpec((None, rb, cp), lambda i, sel: (sel[1], i, 0)),
                pl.BlockSpec((None, rb, cp), lambda i, sel: (0, i, 0)),
                pl.BlockSpec((None, rb, cp), lambda i, sel: (1, i, 0)),
                pl.BlockSpec((None, rb, cp), lambda i, sel: (2, i, 0)),
                blk, blk, blk]
    operands = (sel, grad, from_sibling, from_chips, from_chips, from_chips, w, m, v)
    out_shape = [_sds(w.shape, F32)] * 4
    if job is None:
        grid_spec = pltpu.PrefetchScalarGridSpec(num_scalar_prefetch=1, grid=(r // rb,), in_specs=in_specs,
                                                 out_specs=[blk] * 4)
        return pl.pallas_call(body, name=name, grid_spec=grid_spec, out_shape=out_shape,
                              compiler_params=_params("parallel"))(*operands)

    j_in, j_out, steps = len(job.inputs), len(job.out_shape), r // rb

    def hosted(sel_ref, *refs):
        own_in, refs = refs[:8], refs[8:]
        jin, refs = refs[:j_in], refs[j_in:]
        own_out, refs = refs[:4], refs[4:]
        jout, jscr = refs[:j_out], refs[j_out:]
        pl.when(pl.program_id(0) == 0)(lambda: job.phases[0][1](jin, jout, jscr))
        body(sel_ref, *own_in, *own_out)
        pl.when(pl.program_id(0) == steps - 1)(lambda: job.phases[-1][1](jin, jout, jscr))

    grid_spec = pltpu.PrefetchScalarGridSpec(
        num_scalar_prefetch=1, grid=(steps,), in_specs=in_specs + [_any()] * j_in,
        out_specs=[blk] * 4 + [_any()] * j_out, scratch_shapes=job.scratch)
    return pl.pallas_call(hosted, name=name, grid_spec=grid_spec, out_shape=out_shape + job.out_shape,
                          compiler_params=_params("arbitrary"))(*operands, *job.inputs)


def _reduce_adamw_group(items, sel, name):
    n = len(items)

    def body(sel_ref, *refs):
        del sel_ref
        ins, outs = refs[:8 * n], refs[8 * n:]
        for a in range(n):
            g_ref, s_ref, c0_ref, c1_ref, c2_ref, w_ref, m_ref, v_ref = ins[8 * a:8 * a + 8]
            g = g_ref[...].astype(F32) + s_ref[...].astype(F32)
            g = g + c0_ref[...].astype(F32)
            g = g + c1_ref[...].astype(F32)
            g = g + c2_ref[...].astype(F32)
            d, m2, v2 = _adamw_math(w_ref[...], g, m_ref[...], v_ref[...])
            for ref, val in zip(outs[4 * a:4 * a + 4], (g, d, m2, v2)):
                ref[...] = val

    in_specs, out_specs, out_shape, operands = [], [], [], []
    for grad, from_sibling, from_chips, w, m, v in items:
        slot = lambda index, shape=grad.shape[1:]: pl.BlockSpec((None,) + shape, index)
        full = pl.BlockSpec(w.shape, lambda i, sel: (0, 0))
        in_specs += [slot(lambda i, sel: (sel[0], 0, 0)), slot(lambda i, sel: (sel[1], 0, 0)),
                     slot(lambda i, sel: (0, 0, 0)), slot(lambda i, sel: (1, 0, 0)), slot(lambda i, sel: (2, 0, 0)),
                     full, full, full]
        out_specs += [full] * 4
        out_shape += [_sds(w.shape, F32)] * 4
        operands += [grad, from_sibling, from_chips, from_chips, from_chips, w, m, v]
    grid_spec = pltpu.PrefetchScalarGridSpec(num_scalar_prefetch=1, grid=(1,), in_specs=in_specs, out_specs=out_specs)
    outs = pl.pallas_call(body, name=name, grid_spec=grid_spec, out_shape=out_shape,
                          compiler_params=_params("arbitrary"))(sel, *operands)
    return [outs[4 * a:4 * a + 4] for a in range(n)]


def _wada_grad(c_all, dmod_piece):
    d = c_all.shape[1]
    n = dmod_piece.shape[1]

    def body(c_ref, dm_ref, o_ref):
        o_ref[...] = _dot_tn(c_ref[...], dm_ref[...])

    return pl.pallas_call(body, name="ada_wgrad", out_shape=_sds((d, n), F32),
                          in_specs=[_vmem()] * 2, out_specs=_vmem())(c_all, dmod_piece)


def _column_chunks(width):
    for n in (4, 2):
        if width % (n * LANES) == 0:
            return n
    return 1


def _conv_taps(ref, col):
    return ref[0:1, col], ref[1:2, col], ref[2:3, col]


def _gelu_parts(u):
    u2 = u * u
    th = jnp.tanh((GELU_C0 * u) * (1.0 + GELU_C1 * u2))
    dcdf = (1.0 - th * th) * ((0.5 * GELU_C0) * (1.0 + (3.0 * GELU_C1) * u2))
    return 0.5 * (1.0 + th), dcdf


def _conv3_bwd(dv, carry, col, taps, x):
    halo = _halo_bottom(dv[:16, :], carry[:, col])
    carry[:, col] = dv[:16, :]
    d1 = _shift_up(dv, halo, 1)
    d2 = _shift_up(dv, halo, 2)
    w0, w1, w2 = taps
    dx = w2 * dv
    dx = dx + w1 * d1
    dx = dx + w0 * d2
    return dx, (_colsum(d2 * x), _colsum(d1 * x), _colsum(dv * x))


def _prenorm(x, vp_ref):
    r = lax.rsqrt(_rowmean(x * x) + EPS)
    nh = x * r
    return (nh * vp_ref[0:1, :]) * vp_ref[1:2, :] + vp_ref[2:3, :], r, nh


def _prenorm_bwd(dh, r, nh, vp_ref, red_ref):
    g, sc1 = vp_ref[0:1, :], vp_ref[1:2, :]
    red_ref[0:1, :] += _colsum(dh)
    red_ref[1:2, :] += _colsum(dh * (nh * g))
    red_ref[2:3, :] += _colsum(dh * nh * sc1)
    dnh = dh * g * sc1
    return r * (dnh - nh * _rowmean(dnh * nh))


def _postnorm_bwd(dres, z, gate, gpost, red_ref):
    r = lax.rsqrt(_rowmean(z * z) + EPS)
    nh = z * r
    dn = dres * gate
    red_ref[0:1, :] += _colsum(dn * nh)
    red_ref[1:2, :] += _colsum(dres * (nh * gpost))
    dnh = dn * gpost
    return r * (dnh - nh * _rowmean(dnh * nh))


def _mixer_block_fwd(x, vec_pre, vec, w_in, w_pool, w_bout, w_o, job):
    t, d = x.shape
    tm = GROUP
    gw = d // len(POOL_WINDOWS)
    pool_rows = 8 * (POOL_WINDOWS[-1] - 1)

    def body(x_ref, vp_ref, vec_ref, win_ref, wp_ref, wb_ref, wo_ref,
             hb_ref, p5_ref, qm_ref, cv_ref, yar_ref, yb_ref, o_ref, x1_ref, mbuf, ucarry, pcarry):
        i = pl.program_id(0)

        @pl.when(i == 0)
        def _():
            ucarry[...] = jnp.zeros_like(ucarry)
            pcarry[...] = jnp.zeros_like(pcarry)

        xp = _interleave(x_ref[...])
        hb = _prenorm(xp, vp_ref)[0].astype(BF16)
        hb_ref[...] = hb
        proj = lambda k: _dot(hb, win_ref[:, k * d:(k + 1) * d])

        za = proj(4)
        p5_ref[:, 3 * d:4 * d] = za.astype(BF16)
        sa = jax.nn.sigmoid(za)
        u_pool = proj(0)
        for g, window in enumerate(POOL_WINDOWS):
            cols = slice(g * gw, (g + 1) * gw)
            rows = 8 * (window - 1)
            u = u_pool[:, cols]
            halo = _halo_top(u[tm - rows:, :], ucarry[pool_rows - rows:, cols])
            s, shift = jnp.concatenate([halo, u], axis=0), 1
            while shift < window:
                s = s[8 * shift:, :] + s[:s.shape[0] - 8 * shift, :]
                shift *= 2
            pgb = (s * _inv_count(i * tm, window) - u).astype(BF16)
            qm_ref[:, 2 * d + g * gw:2 * d + (g + 1) * gw] = pgb
            yar = _dot(pgb, wp_ref[g])
            yar_ref[:, cols] = yar
            mbuf[:, cols] = sa[:, cols] * (yar * vec_ref[2:3, cols])
        ucarry[...] = u_pool[tm - pool_rows:, :]

        ux = proj(1)
        uc = proj(3)
        p5_ref[:, 0:d] = ux.astype(BF16)
        p5_ref[:, 2 * d:3 * d] = uc.astype(BF16)
        p = uc * ux
        halo = _halo_top(p[tm - 16:, :], pcarry[...])
        pcarry[...] = p[tm - 16:, :]
        cv = vec_ref[3:4, :] + vec_ref[4:5, :] * _shift_down(p, halo, 2)
        cv = cv + vec_ref[5:6, :] * _shift_down(p, halo, 1)
        cv = cv + vec_ref[6:7, :] * p
        cv_ref[...] = cv
        ub = proj(2)
        p5_ref[:, d:2 * d] = ub.astype(BF16)
        qb = (ub * cv).astype(BF16)
        qm_ref[:, 0:d] = qb
        yb = _dot(qb, wb_ref[...])
        yb_ref[...] = yb

        zb = proj(5)
        p5_ref[:, 4 * d:5 * d] = zb.astype(BF16)
        mb = (mbuf[...] + jax.nn.sigmoid(zb) * yb).astype(BF16)
        qm_ref[:, d:2 * d] = mb
        o = _dot(mb, wo_ref[...])
        o_ref[...] = o
        r2 = lax.rsqrt(_rowmean(o * o) + EPS)
        x1_ref[...] = xp + vec_ref[0:1, :] * ((o * r2) * vec_ref[1:2, :])

    row = lambda n: pl.BlockSpec((tm, n), lambda i: (i, 0))
    widths = [d, 5 * d, 3 * d, d, d, d, d, d]
    return _call(
        body, "mixer_block_fwd", (t // tm,), [row(d)] + [_vmem()] * 6, [row(n) for n in widths],
        [_sds((t, n), BF16) for n in widths[:3]] + [_sds((t, n), F32) for n in widths[3:]],
        [pltpu.VMEM((tm, d), F32), pltpu.VMEM((pool_rows, d), F32), pltpu.VMEM((16, d), F32)],
        _params("arbitrary"), (x, vec_pre, vec, w_in, w_pool, w_bout, w_o), job)


def _ffn_block_fwd(x1, target, vec_pre, vec, fcv, w_up, w_down):
    t, d = x1.shape
    tm = GROUP
    fp = w_down.shape[0]
    nch = _column_chunks(fp)
    cw = fp // nch

    def body(x1_ref, tg_ref, vp_ref, vec_ref, fcv_ref, wu_ref, wd_ref,
             hb_ref, upb_ref, upreb_ref, a_ref, ffb_ref, dy_ref, loss_ref, carry):
        i = pl.program_id(0)

        @pl.when(i == 0)
        def _():
            carry[...] = jnp.zeros_like(carry)
            loss_ref[...] = jnp.zeros_like(loss_ref)

        x1 = x1_ref[...]
        hb = _prenorm(x1, vp_ref)[0].astype(BF16)
        hb_ref[...] = hb

        cols = [(slice(j * cw, (j + 1) * cw), slice(fp + j * cw, fp + (j + 1) * cw)) for j in range(nch)]
        up_gate = _dot(hb, wu_ref[:, 0:fp])
        up_val = _dot(hb, wu_ref[:, fp:2 * fp])

        def conv(v, col):
            halo = _halo_top(v[tm - 16:, :], carry[:, col])
            carry[:, col] = v[tm - 16:, :]
            w0, w1, w2 = _conv_taps(fcv_ref, col)
            y = fcv_ref[3:4, col] + w0 * _shift_down(v, halo, 2)
            y = y + w1 * _shift_down(v, halo, 1)
            y = y + w2 * v
            upb_ref[:, col] = y.astype(BF16)
            upreb_ref[:, col] = v.astype(BF16)
            return y

        ff = None
        for j in range(nch):
            gc, vc = cols[j]
            gate = conv(up_gate[:, gc], gc)
            val = conv(up_val[:, gc], vc)
            ab = ((gate * _gelu_parts(gate)[0]) * val).astype(BF16)
            a_ref[:, gc] = ab
            part = _dot(ab, wd_ref[gc, :])
            ff = part if ff is None else ff + part
        ffb_ref[...] = ff.astype(BF16)
        r4 = lax.rsqrt(_rowmean(ff * ff) + EPS)
        y = x1 + vec_ref[0:1, :] * ((ff * r4) * vec_ref[1:2, :])
        e = y - _interleave(tg_ref[...])
        dy_ref[...] = e * (1.0 / d)
        loss_ref[...] += jnp.sum(_rowmean(e * e))

    row = lambda n: pl.BlockSpec((tm, n), lambda i: (i, 0))
    return pl.pallas_call(
        body, name="ffn_block_fwd", grid=(t // tm,),
        in_specs=[row(d), row(d)] + [_vmem()] * 5,
        out_specs=[row(d), row(2 * fp), row(2 * fp), row(fp), row(d), row(d), pl.BlockSpec((8, LANES), lambda i: (0, 0))],
        out_shape=[_sds((t, d), BF16), _sds((t, 2 * fp), BF16), _sds((t, 2 * fp), BF16), _sds((t, fp), BF16),
                   _sds((t, d), BF16), _sds((t, d), F32), _sds((8, LANES), F32)],
        scratch_shapes=[pltpu.VMEM((16, 2 * fp), F32)],
        compiler_params=_params("arbitrary"),
    )(x1, target, vec_pre, vec, fcv, w_up, w_down)


def _ffn_block_bwd(dy, ffb, x1, upb, upreb, vec_pre, vec, fcv, w_up, w_down):
    t, d = dy.shape
    tm = GROUP
    fp = w_down.shape[0]
    nch = _column_chunks(fp)
    cw = fp // nch
    nt = t // tm

    def body(dy_ref, ff_ref, x1_ref, upb_ref, upreb_ref, vp_ref, vec_ref, fcv_ref, wu_ref, wd_ref,
             dff_ref, dup_ref, dx1_ref, red_ref, cred_ref, pred_ref, carry):
        @pl.when(pl.program_id(0) == 0)
        def _():
            carry[...] = jnp.zeros_like(carry)
            red_ref[...] = jnp.zeros_like(red_ref)
            cred_ref[...] = jnp.zeros_like(cred_ref)
            pred_ref[...] = jnp.zeros_like(pred_ref)

        dy_v = dy_ref[...]
        dffb = _postnorm_bwd(dy_v, ff_ref[...].astype(F32), vec_ref[0:1, :], vec_ref[1:2, :], red_ref).astype(BF16)
        dff_ref[...] = dffb

        def conv_bwd(dv, col):
            dx, (t0, t1, t2) = _conv3_bwd(dv, carry, col, _conv_taps(fcv_ref, col), upreb_ref[:, col].astype(F32))
            cred_ref[0:1, col] += t0
            cred_ref[1:2, col] += t1
            cred_ref[2:3, col] += t2
            cred_ref[3:4, col] += _colsum(dv)
            dxb = dx.astype(BF16)
            dup_ref[:, col] = dxb
            return _dot_nt(dxb, wu_ref[:, col])

        dh = None
        for j in range(nch):
            gc = slice(j * cw, (j + 1) * cw)
            vc = slice(fp + j * cw, fp + (j + 1) * cw)
            da = _dot_nt(dffb, wd_ref[gc, :])
            gate = upb_ref[:, gc].astype(F32)
            val = upb_ref[:, vc].astype(F32)
            cdf, dcdf = _gelu_parts(gate)
            part = conv_bwd(da * val * (cdf + gate * dcdf), gc) + conv_bwd(da * (gate * cdf), vc)
            dh = part if dh is None else dh + part

        _, r, nh = _prenorm(x1_ref[...], vp_ref)
        dx1_ref[...] = dy_v + _prenorm_bwd(dh, r, nh, vp_ref, pred_ref)

    rev = lambda n: pl.BlockSpec((tm, n), lambda i: (nt - 1 - i, 0))
    fixed = lambda n: pl.BlockSpec((8, n), lambda i: (0, 0))
    return pl.pallas_call(
        body, name="ffn_block_bwd", grid=(nt,),
        in_specs=[rev(d), rev(d), rev(d), rev(2 * fp), rev(2 * fp)] + [_vmem()] * 5,
        out_specs=[rev(d), rev(2 * fp), rev(d), fixed(d), fixed(2 * fp), fixed(d)],
        out_shape=[_sds((t, d), BF16), _sds((t, 2 * fp), BF16), _sds((t, d), F32), _sds((8, d), F32),
                   _sds((8, 2 * fp), F32), _sds((8, d), F32)],
        scratch_shapes=[pltpu.VMEM((16, 2 * fp), F32)],
        compiler_params=_params("arbitrary"),
    )(dy, ffb, x1, upb, upreb, vec_pre, vec, fcv, w_up, w_down)


def _mixer_block_bwd(dx1, o, yar, yb, cv, p5b, x, vec_pre, vec, w_in, w_pool, w_bout, w_o, job):
    t, d = dx1.shape
    tm = GROUP
    gw = d // len(POOL_WINDOWS)
    nt = t // tm
    pool_rows = 8 * (POOL_WINDOWS[-1] - 1)

    def body(dx1_ref, o_ref, yar_ref, yb_ref, cv_ref, p5_ref, x_ref, vp_ref, vec_ref, win_ref, wp_ref, wb_ref, wo_ref,
             dqm_ref, dp_ref, gx_ref, red_ref, pred_ref, dpgcarry, dcvcarry):
        i = pl.program_id(0)
        tix = nt - 1 - i

        @pl.when(i == 0)
        def _():
            red_ref[...] = jnp.zeros_like(red_ref)
            pred_ref[...] = jnp.zeros_like(pred_ref)
            dpgcarry[...] = jnp.zeros_like(dpgcarry)
            dcvcarry[...] = jnp.zeros_like(dcvcarry)

        pscale = vec_ref[2:3, :]
        dx1_v = dx1_ref[...]
        dob = _postnorm_bwd(dx1_v, o_ref[...].astype(F32), vec_ref[0:1, :], vec_ref[1:2, :], red_ref).astype(BF16)
        dqm_ref[:, d:2 * d] = dob
        dm = _dot_nt(dob, wo_ref[...])

        def dproj(cols, value):
            vb = value.astype(BF16)
            dp_ref[:, cols] = vb
            return _dot_nt(vb, win_ref[:, cols])

        sa = jax.nn.sigmoid(p5_ref[:, 3 * d:4 * d].astype(F32))
        yar = yar_ref[...].astype(F32)
        dya = dm * sa
        dh = dproj(slice(4 * d, 5 * d), dm * (yar * pscale) * sa * (1.0 - sa))
        red_ref[2:3, :] += _colsum(dya * yar)
        dyarb = (dya * pscale).astype(BF16)
        dqm_ref[:, 2 * d:3 * d] = dyarb
        sb = jax.nn.sigmoid(p5_ref[:, 4 * d:5 * d].astype(F32))
        dybb = (dm * sb).astype(BF16)
        dqm_ref[:, 0:d] = dybb
        dh = dh + dproj(slice(5 * d, 6 * d), dm * yb_ref[...].astype(F32) * sb * (1.0 - sb))

        for g, window in enumerate(POOL_WINDOWS):
            cols = slice(g * gw, (g + 1) * gw)
            rows = 8 * (window - 1)
            dpg = _dot_nt(dyarb[:, cols], wp_ref[g])
            dpgs = dpg * _inv_count(tix * tm, window)
            halo = _halo_bottom(dpgs[:rows, :], dpgcarry[:rows, cols])
            dpgcarry[:, cols] = dpgs[:pool_rows, :]
            s, shift = jnp.concatenate([dpgs, halo], axis=0), 1
            while shift < window:
                s = s[:s.shape[0] - 8 * shift, :] + s[8 * shift:, :]
                shift *= 2
            dh = dh + dproj(cols, s - dpg)

        dq = _dot_nt(dybb, wb_ref[...])
        ux = p5_ref[:, 0:d].astype(F32)
        uc = p5_ref[:, 2 * d:3 * d].astype(F32)
        dh = dh + dproj(slice(2 * d, 3 * d), dq * cv_ref[...].astype(F32))
        dcv = dq * p5_ref[:, d:2 * d].astype(F32)
        taps = (vec_ref[4:5, :], vec_ref[5:6, :], vec_ref[6:7, :])
        dpv, (t0, t1, t2) = _conv3_bwd(dcv, dcvcarry, slice(0, d), taps, uc * ux)
        red_ref[3:4, :] += _colsum(dcv)
        red_ref[4:5, :] += t0
        red_ref[5:6, :] += t1
        red_ref[6:7, :] += t2
        dh = dh + dproj(slice(d, 2 * d), dpv * uc)
        dh = dh + dproj(slice(3 * d, 4 * d), dpv * ux)

        _, r, nh = _prenorm(_interleave(x_ref[...]), vp_ref)
        gx_ref[...] = _deinterleave(dx1_v + _prenorm_bwd(dh, r, nh, vp_ref, pred_ref))

    rev = lambda n: pl.BlockSpec((tm, n), lambda i: (nt - 1 - i, 0))
    return _call(
        body, "mixer_block_bwd", (nt,), [rev(d)] * 5 + [rev(5 * d), rev(d)] + [_vmem()] * 6,
        [rev(3 * d), rev(6 * d), rev(d), pl.BlockSpec((16, d), lambda i: (0, 0)),
         pl.BlockSpec((8, d), lambda i: (0, 0))],
        [_sds((t, 3 * d), BF16), _sds((t, 6 * d), BF16), _sds((t, d), F32), _sds((16, d), F32), _sds((8, d), F32)],
        [pltpu.VMEM((pool_rows, d), F32), pltpu.VMEM((16, d), F32)],
        _params("arbitrary"), (dx1, o, yar, yb, cv, p5b, x, vec_pre, vec, w_in, w_pool, w_bout, w_o), job)


def _matmul_tn(a, b, bm, bn, tk, by_col_block, name, job=None):
    t, m = a.shape
    n = b.shape[1]
    nk = t // tk
    parts = int(by_col_block)
    piece = bn // max(parts, 1)
    wide = _round_up(piece, LANES)

    def body(a_ref, b_ref, o_ref, acc_ref):
        k = pl.program_id(2)

        @pl.when(k == 0)
        def _():
            acc_ref[...] = jnp.zeros_like(acc_ref)

        acc_ref[...] += _dot_tn(a_ref[...], b_ref[...])

        @pl.when(k == nk - 1)
        def _():
            if parts:
                acc = acc_ref[...]
                for p in range(parts):
                    if wide > piece:
                        o_ref[p] = jnp.zeros((bm, wide), o_ref.dtype)
                    o_ref[p, :, 0:piece] = acc[:, p * piece:(p + 1) * piece].astype(o_ref.dtype)
            else:
                o_ref[...] = acc_ref[...].astype(o_ref.dtype)

    if by_col_block:
        out_shape = _sds((parts * n // bn, m, wide), BF16)
        out_spec = pl.BlockSpec((parts, bm, wide), lambda i, j, k: (j, i, 0))
    else:
        out_shape = _sds((m, n), BF16)
        out_spec = pl.BlockSpec((bm, bn), lambda i, j, k: (i, j))
    out = _call(body, name, (m // bm, n // bn, nk),
                [pl.BlockSpec((tk, bm), lambda i, j, k: (k, i)), pl.BlockSpec((tk, bn), lambda i, j, k: (k, j))],
                [out_spec], [out_shape], [pltpu.VMEM((bm, bn), F32)],
                _params("arbitrary", "arbitrary", "arbitrary"), (a, b), job)
    return out if job is not None else out[0]


def _side_by_side(blocks, name):
    n, r, c = blocks.shape
    rb = _row_block(r) // 2

    def body(in_ref, o_ref):
        for j in range(n):
            o_ref[:, j * c:(j + 1) * c] = in_ref[j]

    return pl.pallas_call(
        body, name=name, grid=(r // rb,), in_specs=[pl.BlockSpec((n, rb, c), lambda i: (0, i, 0))],
        out_specs=pl.BlockSpec((rb, n * c), lambda i: (i, 0)), out_shape=_sds((r, n * c), blocks.dtype),
        compiler_params=_params("parallel"))(blocks)


def _matmul_tn_groups(a, b, groups, tk, name, job=None):
    t, m = a.shape
    w = m // groups
    nk = t // tk

    def body(a_ref, b_ref, o_ref, acc_ref):
        k = pl.program_id(1)

        @pl.when(k == 0)
        def _():
            acc_ref[...] = jnp.zeros_like(acc_ref)

        acc_ref[...] += _dot_tn(a_ref[...], b_ref[...])

        @pl.when(k == nk - 1)
        def _():
            o_ref[...] = acc_ref[...].astype(o_ref.dtype)

    blk = pl.BlockSpec((tk, w), lambda g, k: (k, g))
    out = _call(body, name, (groups, nk), [blk, blk], [pl.BlockSpec((None, w, w), lambda g, k: (g, 0, 0))],
                [_sds((groups, w, w), BF16)], [pltpu.VMEM((w, w), F32)], _params("arbitrary", "arbitrary"), (a, b), job)
    return out if job is not None else out[0]


def _round_up(n, k):
    return (n + k - 1) // k * k


def _rows8(rows, width):
    n = _round_up(len(rows), 8)
    rows = list(rows) + [jnp.zeros((1, width), F32)] * (n - len(rows))
    return jnp.concatenate(rows, axis=0)


def kernel(x, c, g_pre_mix, g_post_mix, g_pre_ffn, g_post_ffn, w_ada, b_ada, w_in, w_pool, pool_scale, conv_w, conv_b, w_bout, w_o, w_up, ffn_conv_w, ffn_conv_b, w_down, loss_target, m_g_pre_mix, m_g_post_mix, m_g_pre_ffn, m_g_post_ffn, m_w_ada, m_b_ada, m_w_in, m_w_pool, m_pool_scale, m_conv_w, m_conv_b, m_w_bout, m_w_o, m_w_up, m_ffn_conv_w, m_ffn_conv_b, m_w_down, v_g_pre_mix, v_g_post_mix, v_g_pre_ffn, v_g_post_ffn, v_w_ada, v_b_ada, v_w_in, v_w_pool, v_pool_scale, v_conv_w, v_conv_b, v_w_bout, v_w_o, v_w_up, v_ffn_conv_w, v_ffn_conv_b, v_w_down):
    t, d = x.shape[1], x.shape[2]
    ngroups = len(POOL_WINDOWS)
    gw = d // ngroups
    ada_n = w_ada.shape[2]
    in_n = w_in.shape[2]
    up_n = w_up.shape[2]
    fp = NDEV * w_down.shape[1]
    assert x.shape[0] == 1 and t % GROUP == 0, "one sequence per device, a whole number of token groups"
    assert gw % LANES == 0 and in_n % LANES == 0 and ada_n % LANES == 0 and NDEV * in_n == 6 * d
    assert NDEV * up_n == 2 * fp and fp % (2 * LANES) == 0, "gate and value halves cut into lane-aligned chunks"

    xi, yi, ci = _position()
    me = _linear(xi, yi, ci)
    chip = 2 * xi + yi
    core = jnp.reshape(ci, (1,)).astype(jnp.int32)
    sel = jnp.stack([2 * chip + ci, chip]).astype(jnp.int32)

    x2 = x.reshape(t, d)
    target = loss_target.reshape(t, d)

    cw_n = conv_w.shape[2]
    pack = jnp.concatenate([c.reshape(1, d), conv_w[0].reshape(1, 3 * cw_n), ffn_conv_w[0].reshape(1, 3 * up_n)], axis=1)
    pack = jnp.pad(pack, ((0, 0), (0, _round_up(pack.shape[1], LANES) - pack.shape[1])))
    b_piece = lax.dynamic_slice_in_dim(b_ada, me * ada_n, ada_n, axis=1)
    mixer_weights = _allgather_job(
        [w_in[0].astype(BF16), w_bout[0].astype(BF16), w_o[0].astype(BF16), w_pool[0].astype(BF16)],
        ["cols", "rows", "rows", "mid"], 0.5, 0.75)
    learned = _rows8([g_pre_mix, g_post_mix, g_pre_ffn, g_post_ffn, pool_scale, conv_b], d)
    gathered, vec_pre_mix, vec_mix, vec_pre_ffn, vec_ffn, fcv, w_in_f, g_bout, g_o, w_pool_f = (
        _gather_weights_and_modulation(mixer_weights, pack, w_ada[0], b_piece, learned, ffn_conv_b, d, cw_n, up_n))
    w_bout_f = g_bout.reshape(d, d)
    w_o_f = g_o.reshape(d, d)
    c16 = jnp.pad(gathered[:, :d], ((0, 8), (0, 0))).astype(BF16)

    ffn_weights = _allgather_job([w_up[0].astype(BF16), w_down[0].astype(BF16)], ["rows", "rows"], 0.5, 0.8)
    h1b, p5b, qmb, cv, yar, yb, o, x1, g_up, g_down = _mixer_block_fwd(
        x2, vec_pre_mix, vec_mix, w_in_f, w_pool_f, w_bout_f, w_o_f, ffn_weights)
    w_up_f = _side_by_side(g_up, "w_up_side_by_side")
    w_down_f = g_down.reshape(fp, d)
    h2b, upb, upreb, ab, ffb, dy, loss_part = _ffn_block_fwd(x1, target, vec_pre_ffn, vec_ffn, fcv, w_up_f, w_down_f)

    tk, tk_wide = min(4096, t), min(2048, t)
    chip_sum = lambda gs, ss, name: _chip_sums(gs, ss, core, name)
    dffb, dupre, dx1, red_ffn, red_fconv, red_pre_ffn = _ffn_block_bwd(
        dy, ffb, x1, upb, upreb, vec_pre_ffn, vec_ffn, fcv, w_up_f, w_down_f)
    chunk = fp // _column_chunks(fp)
    gr_up = _matmul_tn(h2b, dupre, d, chunk, tk_wide, chunk // up_n, "wgrad_up")
    gr_down, sib_up = _matmul_tn(ab, dffb, chunk, d, tk_wide, False, "wgrad_down", _sibling_job([gr_up]))
    gr_down = gr_down.reshape(NDEV, fp // NDEV, d)
    sib_ffn = [sib_up] + list(_run_job(_sibling_job([gr_down]), "rs_sibling_down"))
    dqmb, dproj, grad_x, red_mix, red_pre_mix = _mixer_block_bwd(
        dx1, o, yar, yb, cv, p5b, x2, vec_pre_mix, vec_mix, w_in_f, w_pool_f, w_bout_f, w_o_f, None)
    gr_in, fc_up, fc_down = _matmul_tn(h1b, dproj, d, in_n, tk, True, "wgrad_in",
                                       _chips_job(chip_sum([gr_up, gr_down], sib_ffn, "rs_chip_sum_ffn")))
    dmod = [red_pre_mix[0:1], red_pre_mix[1:2], red_mix[1:2], red_pre_ffn[0:1], red_pre_ffn[1:2], red_ffn[1:2]]
    small = [red_pre_mix[2:3], red_mix[0:1], red_pre_ffn[2:3], red_ffn[0:1], red_mix[2:3], red_mix[3:4],
             red_mix[4:5], red_mix[5:6], red_mix[6:7]] + dmod
    flat = jnp.concatenate(small + [red_fconv[0:4].reshape(1, 8 * fp), loss_part[0:1, 0:1]], axis=1)
    flat_n = flat.shape[1]
    width = 8 * LANES
    rows = _round_up(-(-flat_n // width), 8)
    flat = jnp.pad(flat, ((0, 0), (0, rows * width - flat_n))).reshape(rows, width)
    gat, tot, *sib_in = _small_allreduce(flat, "allreduce_small_rs_sibling_in", _sibling_job([gr_in]))
    gr_qmp, fc_in = _matmul_tn_groups(qmb, dqmb, 3, tk, "wgrad_bout_o_pool",
                                      _chips_job(chip_sum([gr_in], sib_in, "rs_chip_sum_in")))
    gr_bout = gr_qmp[0].reshape(NDEV, d // NDEV, d)
    gr_o = gr_qmp[1].reshape(NDEV, d // NDEV, d)
    gr_pool = jnp.stack([gr_qmp[2, g * gw:(g + 1) * gw, g * gw:(g + 1) * gw] for g in range(ngroups)])
    gr_pool = gr_pool.reshape(ngroups, NDEV, gw // NDEV, gw).transpose(1, 0, 2, 3).reshape(NDEV, -1, gw)
    rest = [gr_bout, gr_o, gr_pool]
    sib_rest = _run_job(_sibling_job(rest), "rs_sibling_rest")

    def big(grad, from_sibling, from_chips, w, m, v, name, transposed=False, job=None):
        shape = w.shape
        flat = (lambda a: a[0].T) if transposed else (lambda a: a.reshape((-1, shape[-1])))
        outs = _reduce_adamw(grad, from_sibling, from_chips, sel, flat(w), flat(m), flat(v), name, transposed, job)
        return [(a.T if transposed else a).reshape(shape) for a in outs[:4]] + list(outs[4:])

    g_w_in, d_w_in, nm_w_in, nv_w_in, *fc_rest = big(
        gr_in, sib_in[0], fc_in, w_in, m_w_in, v_w_in, "adamw_in",
        job=_chips_job(chip_sum(rest, sib_rest, "rs_chip_sum_rest")))

    g_w_up, d_w_up, nm_w_up, nv_w_up = big(gr_up, sib_ffn[0], fc_up, w_up, m_w_up, v_w_up, "adamw_up",
                                           transposed=up_n % LANES != 0)
    g_w_down, d_w_down, nm_w_down, nv_w_down = big(gr_down, sib_ffn[1], fc_down, w_down, m_w_down, v_w_down, "adamw_down")
    flat2 = lambda a: a.reshape((-1, a.shape[-1]))
    rest_w = [(w_bout, m_w_bout, v_w_bout), (w_o, m_w_o, v_w_o), (w_pool, m_w_pool, v_w_pool)]
    rest_out = _reduce_adamw_group(
        [(g, s, f, flat2(w), flat2(m), flat2(v)) for g, s, f, (w, m, v) in zip(rest, sib_rest, fc_rest, rest_w)],
        sel, "adamw_bout_o_pool")
    (g_w_bout, d_w_bout, nm_w_bout, nv_w_bout), (g_w_o, d_w_o, nm_w_o, nv_w_o), (g_w_pool, d_w_pool, nm_w_pool, nv_w_pool) = [
        [a.reshape(w.shape) for a in outs] for outs, (w, _, _) in zip(rest_out, rest_w)]

    tot = tot.reshape(1, rows * width)
    gat = gat.reshape(NDEV, rows * width)
    take = lambda k: tot[:, k * d:(k + 1) * d]
    g_g_pre_mix, g_g_post_mix, g_g_pre_ffn, g_g_post_ffn, g_pool_scale, g_conv_b = [take(k) for k in range(6)]
    g_conv_w_full = jnp.concatenate([take(6), take(7), take(8)], axis=0)
    g_conv_w = lax.dynamic_slice_in_dim(g_conv_w_full, me * cw_n, cw_n, axis=1)
    g_b_ada = tot[:, 9 * d:15 * d]
    dmod_all = gat[:, 9 * d:15 * d]
    fconv_tot = tot[:, 15 * d:15 * d + 8 * fp].reshape(4, 2 * fp)
    loss = 0.5 * tot[0, 15 * d + 8 * fp]
    g_ffn_conv_b = fconv_tot[3:4]
    g_ffn_conv_w = lax.dynamic_slice_in_dim(fconv_tot[0:3], me * up_n, up_n, axis=1)
    dmod_piece = lax.dynamic_slice_in_dim(dmod_all, me * ada_n, ada_n, axis=1)
    g_w_ada = _wada_grad(c16, jnp.pad(dmod_piece, ((0, 8), (0, 0))).astype(BF16))

    names_small = [(g_pre_mix, g_g_pre_mix, m_g_pre_mix, v_g_pre_mix), (g_post_mix, g_g_post_mix, m_g_post_mix, v_g_post_mix),
                   (g_pre_ffn, g_g_pre_ffn, m_g_pre_ffn, v_g_pre_ffn), (g_post_ffn, g_g_post_ffn, m_g_post_ffn, v_g_post_ffn),
                   (b_ada, g_b_ada, m_b_ada, v_b_ada), (pool_scale, g_pool_scale, m_pool_scale, v_pool_scale),
                   (conv_w, g_conv_w, m_conv_w, v_conv_w), (conv_b, g_conv_b, m_conv_b, v_conv_b),
                   (ffn_conv_w, g_ffn_conv_w, m_ffn_conv_w, v_ffn_conv_w), (ffn_conv_b, g_ffn_conv_b, m_ffn_conv_b, v_ffn_conv_b)]
    small_out = _adamw_group([[a.reshape((-1, a.shape[-1])) for a in q] for q in names_small], "adamw_small")
    unpack_small = lambda k: [outs[k].reshape(q[0].shape) for outs, q in zip(small_out, names_small)]
    (d_g_pre_mix, d_g_post_mix, d_g_pre_ffn, d_g_post_ffn, d_b_ada, d_pool_scale, d_conv_w, d_conv_b,
     d_ffn_conv_w, d_ffn_conv_b) = unpack_small(0)
    (nm_g_pre_mix, nm_g_post_mix, nm_g_pre_ffn, nm_g_post_ffn, nm_b_ada, nm_pool_scale, nm_conv_w, nm_conv_b,
     nm_ffn_conv_w, nm_ffn_conv_b) = unpack_small(1)
    (nv_g_pre_mix, nv_g_post_mix, nv_g_pre_ffn, nv_g_post_ffn, nv_b_ada, nv_pool_scale, nv_conv_w, nv_conv_b,
     nv_ffn_conv_w, nv_ffn_conv_b) = unpack_small(2)
    d_w_ada, nm_w_ada, nv_w_ada = [a.reshape(w_ada.shape) for a in
                                   _adamw(w_ada[0], g_w_ada, m_w_ada[0], v_w_ada[0], "adamw_ada")]

    grads = [g_g_pre_mix, g_g_post_mix, g_g_pre_ffn, g_g_post_ffn, g_w_ada.reshape(w_ada.shape), g_b_ada, g_w_in,
             g_w_pool, g_pool_scale, g_conv_w.reshape(conv_w.shape), g_conv_b, g_w_bout, g_w_o, g_w_up,
             g_ffn_conv_w.reshape(ffn_conv_w.shape), g_ffn_conv_b, g_w_down]
    deltas = [d_g_pre_mix, d_g_post_mix, d_g_pre_ffn, d_g_post_ffn, d_w_ada, d_b_ada, d_w_in, d_w_pool, d_pool_scale,
              d_conv_w, d_conv_b, d_w_bout, d_w_o, d_w_up, d_ffn_conv_w, d_ffn_conv_b, d_w_down]
    new_m = [nm_g_pre_mix, nm_g_post_mix, nm_g_pre_ffn, nm_g_post_ffn, nm_w_ada, nm_b_ada, nm_w_in, nm_w_pool,
             nm_pool_scale, nm_conv_w, nm_conv_b, nm_w_bout, nm_w_o, nm_w_up, nm_ffn_conv_w, nm_ffn_conv_b, nm_w_down]
    new_v = [nv_g_pre_mix, nv_g_post_mix, nv_g_pre_ffn, nv_g_post_ffn, nv_w_ada, nv_b_ada, nv_w_in, nv_w_pool,
             nv_pool_scale, nv_conv_w, nv_conv_b, nv_w_bout, nv_w_o, nv_w_up, nv_ffn_conv_w, nv_ffn_conv_b, nv_w_down]
    return (loss, grad_x.reshape(x.shape), *grads, *deltas, *new_m, *new_v)
```

```python
import math

import jax
import jax.numpy as jnp
from jax import lax
from jax.experimental import pallas as pl
from jax.experimental.pallas import tpu as pltpu

F32 = jnp.float32
BF16 = jnp.bfloat16
MESH = pl.DeviceIdType.MESH

NDEV = 8
NCHIP = 4
EPS = 1e-6
POOL_WINDOWS = (2, 4, 8, 16)
LANES = 128
ADAM_LR = 0.001
ADAM_B1 = 0.9
ADAM_B2 = 0.999
ADAM_EPS = 1e-08
ADAM_WD = 0.01
ADAM_STEP = 10
GELU_C0 = math.sqrt(2.0 / math.pi)
GELU_C1 = 0.044715
VMEM_LIMIT = 56 * 2**20


def _vmem():
    return pl.BlockSpec(memory_space=pltpu.VMEM)


def _any():
    return pl.BlockSpec(memory_space=pl.ANY)


def _params(*sem):
    return pltpu.CompilerParams(dimension_semantics=sem, vmem_limit_bytes=VMEM_LIMIT)


def _sds(shape, dtype):
    return jax.ShapeDtypeStruct(tuple(shape), dtype)


def _position():
    return lax.axis_index("x"), lax.axis_index("y"), lax.axis_index("c")


def _linear(x, y, c):
    return 4 * x + 2 * y + c


def _dot(a, b):
    return jnp.dot(a, b, preferred_element_type=F32)


def _dot_nt(a, b):
    return lax.dot_general(a, b, (((1,), (1,)), ((), ())), preferred_element_type=F32)


def _dot_tn(a, b):
    return lax.dot_general(a, b, (((0,), (0,)), ((), ())), preferred_element_type=F32)


def _colsum(v):
    return jnp.sum(v, axis=0, keepdims=True)


def _rowmean(v):
    return jnp.mean(v, axis=-1, keepdims=True)


GROUP = 256


def _interleave(v):
    g, n = v.shape
    return jnp.swapaxes(v.reshape(8, g // 8, n), 0, 1).reshape(g, n)


def _deinterleave(v):
    g, n = v.shape
    return jnp.swapaxes(v.reshape(g // 8, 8, n), 0, 1).reshape(g, n)


def _halo_top(cur_last, prev_last):
    rows, n = cur_last.shape
    c3 = cur_last.reshape(rows // 8, 8, n)
    p3 = prev_last.reshape(rows // 8, 8, n)
    sub = lax.broadcasted_iota(jnp.int32, c3.shape, 1)
    return jnp.where(sub == 0, pltpu.roll(p3, 1, 1), pltpu.roll(c3, 1, 1)).reshape(rows, n)


def _halo_bottom(cur_first, next_first):
    rows, n = cur_first.shape
    c3 = cur_first.reshape(rows // 8, 8, n)
    n3 = next_first.reshape(rows // 8, 8, n)
    sub = lax.broadcasted_iota(jnp.int32, c3.shape, 1)
    return jnp.where(sub == 7, pltpu.roll(n3, 7, 1), pltpu.roll(c3, 7, 1)).reshape(rows, n)


def _shift_down(v, halo, k):
    rows = v.shape[0]
    return jnp.concatenate([halo[halo.shape[0] - 8 * k:, :], v[:rows - 8 * k, :]], axis=0)


def _shift_up(v, halo, k):
    return jnp.concatenate([v[8 * k:, :], halo[:8 * k, :]], axis=0)


def _inv_count(first_token, window):
    row = lax.broadcasted_iota(jnp.int32, (GROUP, 1), 0)
    t = first_token + (row % 8) * (GROUP // 8) + row // 8
    return 1.0 / jnp.minimum(t + 1, window).astype(F32)


class _Job:
    def __init__(self, inputs, out_shape, scratch, phases):
        self.inputs, self.out_shape, self.scratch, self.phases = list(inputs), list(out_shape), list(scratch), phases


def _call(body, name, grid, in_specs, out_specs, out_shape, scratch_shapes, params, operands, job=None):
    if job is None:
        return pl.pallas_call(body, name=name, grid=grid, in_specs=in_specs, out_specs=out_specs, out_shape=out_shape,
                              scratch_shapes=scratch_shapes, compiler_params=params)(*operands)
    n_in, n_out, n_scr = len(in_specs), len(out_specs), len(scratch_shapes)
    j_in, j_out = len(job.inputs), len(job.out_shape)
    steps = math.prod(grid)

    def hosted(*refs):
        own_in, refs = refs[:n_in], refs[n_in:]
        jin, refs = refs[:j_in], refs[j_in:]
        own_out, refs = refs[:n_out], refs[n_out:]
        jout, refs = refs[:j_out], refs[j_out:]
        own_scr, jscr = refs[:n_scr], refs[n_scr:]
        step = pl.program_id(0)
        for axis in range(1, len(grid)):
            step = step * grid[axis] + pl.program_id(axis)
        for frac, fn in job.phases[:-1]:
            pl.when(step == int(frac * (steps - 1)))(lambda fn=fn: fn(jin, jout, jscr))
        body(*own_in, *own_out, *own_scr)
        pl.when(step == steps - 1)(lambda: job.phases[-1][1](jin, jout, jscr))

    return pl.pallas_call(
        hosted, name=name, grid=grid, in_specs=list(in_specs) + [_any()] * j_in,
        out_specs=list(out_specs) + [_any()] * j_out, out_shape=list(out_shape) + job.out_shape,
        scratch_shapes=list(scratch_shapes) + job.scratch, compiler_params=params)(*operands, *job.inputs)


def _run_job(job, name):
    n_in, n_out = len(job.inputs), len(job.out_shape)

    def body(*refs):
        for _, fn in job.phases:
            fn(refs[:n_in], refs[n_in:n_in + n_out], refs[n_in + n_out:])

    return pl.pallas_call(body, name=name, out_shape=job.out_shape, in_specs=[_any()] * n_in,
                          out_specs=[_any()] * n_out, scratch_shapes=job.scratch)(*job.inputs)


def _peers(x, y, c):
    out = []
    for k in range(1, NDEV):
        out.append(((1 - x) if k & 4 else x, (1 - y) if k & 2 else y, (1 - c) if k & 1 else c))
    return out


def _small_allreduce(v, name, job):
    r, n = v.shape
    j_in, j_out = len(job.inputs), len(job.out_shape)

    def body(v_ref, *rest):
        jin, rest = rest[:j_in], rest[j_in:]
        gat_ref, sum_ref = rest[:2]
        jout, rest = rest[2:2 + j_out], rest[2 + j_out:]
        send_sems, recv_sems, local_sem = rest[:3]
        jscr = rest[3:]
        job.phases[0][1](jin, jout, jscr)
        x, y, c = _position()
        me = _linear(x, y, c)
        mine = pltpu.make_async_copy(v_ref, gat_ref.at[me], local_sem)
        mine.start()
        peers = _peers(x, y, c)
        sends = []
        for k, peer in enumerate(peers):
            cp = pltpu.make_async_remote_copy(src_ref=v_ref, dst_ref=gat_ref.at[me], send_sem=send_sems.at[k],
                                              recv_sem=recv_sems.at[k], device_id=peer, device_id_type=MESH)
            cp.start()
            sends.append(cp)
        for k, peer in enumerate(peers):
            pltpu.make_async_remote_copy(src_ref=v_ref, dst_ref=gat_ref.at[_linear(*peer)], send_sem=send_sems.at[k],
                                         recv_sem=recv_sems.at[k], device_id=peer, device_id_type=MESH).wait_recv()
        for cp in sends:
            cp.wait_send()
        mine.wait()
        acc = gat_ref[0]
        for j in range(1, NDEV):
            acc = acc + gat_ref[j]
        sum_ref[...] = acc
        job.phases[-1][1](jin, jout, jscr)

    return pl.pallas_call(
        body, name=name, out_shape=[_sds((NDEV, r, n), F32), _sds((r, n), F32)] + job.out_shape,
        in_specs=[_vmem()] + [_any()] * j_in, out_specs=[_vmem()] * 2 + [_any()] * j_out,
        scratch_shapes=[pltpu.SemaphoreType.DMA((NDEV - 1,)), pltpu.SemaphoreType.DMA((NDEV - 1,)),
                        pltpu.SemaphoreType.DMA(())] + job.scratch,
    )(v, *job.inputs)


def _exchange_rows(src_for, dst_ref, sems):
    send_sems, recv_sems, local_sem = sems
    x, y, c = _position()
    me = _linear(x, y, c)
    row = lambda j: dst_ref.at[pl.ds(j, 1), :]
    mine = pltpu.make_async_copy(src_for(me), row(me), local_sem)
    mine.start()
    peers = _peers(x, y, c)
    sends = []
    for k, peer in enumerate(peers):
        cp = pltpu.make_async_remote_copy(src_ref=src_for(_linear(*peer)), dst_ref=row(me), send_sem=send_sems.at[k],
                                          recv_sem=recv_sems.at[k], device_id=peer, device_id_type=MESH)
        cp.start()
        sends.append(cp)
    for k, peer in enumerate(peers):
        pltpu.make_async_remote_copy(src_ref=src_for(me), dst_ref=row(_linear(*peer)), send_sem=send_sems.at[k],
                                     recv_sem=recv_sems.at[k], device_id=peer, device_id_type=MESH).wait_recv()
    for cp in sends:
        cp.wait_send()
    mine.wait()


def _gather_weights_and_modulation(job, pack, w_ada, b_piece, learned, fcb, d, cw_n, up_n):
    n = pack.shape[1]
    m = w_ada.shape[1]
    j_in, j_out = len(job.inputs), len(job.out_shape)
    row_sems = [pltpu.SemaphoreType.DMA((NDEV - 1,)), pltpu.SemaphoreType.DMA((NDEV - 1,)), pltpu.SemaphoreType.DMA(())]

    def body(pack_ref, wada_ref, bp_ref, lrn_ref, fcb_ref, *rest):
        jin, rest = rest[:j_in], rest[j_in:]
        gat_ref, vpm_ref, vmix_ref, vpf_ref, vffn_ref, fcv_ref = rest[:6]
        jout, rest = rest[6:6 + j_out], rest[6 + j_out:]
        sems1, sems2, piece, mod_ref, modm, jscr = rest[0:3], rest[3:6], rest[6], rest[7], rest[8], rest[9:]
        phases = [fn for _, fn in job.phases]
        phases[0](jin, jout, jscr)
        _exchange_rows(lambda j: pack_ref, gat_ref, sems1)
        c16 = jnp.concatenate([gat_ref[:, 0:d], jnp.zeros((NDEV, d), F32)], axis=0).astype(BF16)
        piece[...] = (_dot(c16, wada_ref[...].astype(BF16)) + bp_ref[...])[0:NDEV, :]
        _exchange_rows(lambda j: piece.at[pl.ds(j, 1), :], mod_ref, sems2)

        for j in range(NDEV):
            done = 0
            while done < m:
                row, col = divmod(j * m + done, d)
                width = min(m - done, d - col)
                modm[row:row + 1, col:col + width] = mod_ref[j:j + 1, done:done + width]
                done += width
        for ref in (vpm_ref, vmix_ref, vpf_ref, vffn_ref, fcv_ref):
            ref[...] = jnp.zeros_like(ref)
        for ref, g_row, k in ((vpm_ref, 0, 0), (vpf_ref, 2, 3)):
            ref[0:1, :] = lrn_ref[g_row:g_row + 1, :]
            ref[1:2, :] = 1.0 + modm[k + 1:k + 2, :]
            ref[2:3, :] = modm[k:k + 1, :]
        vmix_ref[0:1, :] = modm[2:3, :]
        vmix_ref[1:2, :] = lrn_ref[1:2, :]
        vmix_ref[2:4, :] = lrn_ref[4:6, :]
        vffn_ref[0:1, :] = modm[5:6, :]
        vffn_ref[1:2, :] = lrn_ref[3:4, :]
        fcv_ref[3:4, :] = fcb_ref[...]
        for j in range(NDEV):
            for k in range(3):
                vmix_ref[4 + k:5 + k, j * cw_n:(j + 1) * cw_n] = gat_ref[j:j + 1, d + k * cw_n:d + (k + 1) * cw_n]
                off = d + 3 * cw_n + k * up_n
                fcv_ref[k:k + 1, j * up_n:(j + 1) * up_n] = gat_ref[j:j + 1, off:off + up_n]

        for fn in phases[1:]:
            fn(jin, jout, jscr)

    tables = [_sds((8, d), F32), _sds((16, d), F32), _sds((8, d), F32), _sds((8, d), F32), _sds((8, NDEV * up_n), F32)]
    return pl.pallas_call(
        body, name="gather_weights_and_modulation",
        out_shape=[_sds((NDEV, n), F32)] + tables + job.out_shape,
        in_specs=[_vmem()] * 5 + [_any()] * j_in, out_specs=[_vmem()] * 6 + [_any()] * j_out,
        scratch_shapes=row_sems + row_sems + [pltpu.VMEM((NDEV, m), F32), pltpu.VMEM((NDEV, m), F32),
                                              pltpu.VMEM((8, d), F32)] + job.scratch,
    )(pack, w_ada, b_piece, learned, fcb, *job.inputs)


def _gathered(shard, layout):
    if layout == "rows":
        return (NDEV,) + shard.shape, lambda ref, j: ref.at[j]
    if layout == "cols":
        r, c = shard.shape
        return (r, NDEV * c), lambda ref, j: ref.at[:, pl.ds(pl.multiple_of(j * c, LANES), c)]
    g, r, c = shard.shape
    return (g, NDEV * r, c), lambda ref, j: ref.at[:, pl.ds(pl.multiple_of(j * r, 16), r), :]


def _allgather_job(shards, layouts, relay_at, forward_at):
    n = len(shards)
    specs = [_gathered(s, l) for s, l in zip(shards, layouts)]
    halves = [s.shape[0] // 2 for s in shards]

    def plan(src, dst, sems):
        send_sems, recv_sems, _ = sems
        x, y, c = _position()
        me, sibling, xn, yn, dg = (x, y, c), (x, y, 1 - c), (1 - x, y, c), (x, 1 - y, c), (1 - x, 1 - y, c)

        def copy(a, k, block, to, half=None, from_src=False):
            blk = specs[a][1](dst[a], _linear(*block))
            if half is not None:
                blk = blk.at[pl.ds(half * halves[a], halves[a])]
            return pltpu.make_async_remote_copy(src_ref=src[a] if from_src else blk, dst_ref=blk,
                                                send_sem=send_sems.at[a, k], recv_sem=recv_sems.at[a, k],
                                                device_id=to, device_id_type=MESH)
        return copy, me, sibling, xn, yn, dg

    def local(src, dst, sems):
        x, y, c = _position()
        return [pltpu.make_async_copy(src[a], specs[a][1](dst[a], _linear(x, y, c)), sems[2].at[a]) for a in range(n)]

    def own(src, dst, sems):
        copy, me, sibling, xn, yn, dg = plan(src, dst, sems)
        return [copy(a, k, me, to, from_src=True) for k, to in ((1, xn), (2, yn), (0, sibling)) for a in range(n)]

    def relayed(src, dst, sems):
        copy, me, sibling, xn, yn, dg = plan(src, dst, sems)
        return ([copy(a, 3, xn, yn, half=0) for a in range(n)] + [copy(a, 5, xn, sibling) for a in range(n)],
                [copy(a, 4, yn, xn, half=1) for a in range(n)] + [copy(a, 6, yn, sibling) for a in range(n)])

    def diagonal(src, dst, sems):
        copy, me, sibling, xn, yn, dg = plan(src, dst, sems)
        return [copy(a, 7, dg, sibling) for a in range(n)]

    def start(src, dst, sems):
        for cp in local(src, dst, sems) + own(src, dst, sems):
            cp.start()

    def relay(src, dst, sems):
        copy, me, sibling, xn, yn, dg = plan(src, dst, sems)
        from_x, from_y = relayed(src, dst, sems)
        for a in range(n):
            copy(a, 1, xn, me).wait_recv()
        for cp in from_x:
            cp.start()
        for a in range(n):
            copy(a, 2, yn, me).wait_recv()
        for cp in from_y:
            cp.start()

    def forward(src, dst, sems):
        copy, me, sibling, xn, yn, dg = plan(src, dst, sems)
        for a in range(n):
            copy(a, 3, dg, me, half=0).wait_recv()
            copy(a, 4, dg, me, half=1).wait_recv()
        for cp in diagonal(src, dst, sems):
            cp.start()

    def finish(src, dst, sems):
        copy, me, sibling, xn, yn, dg = plan(src, dst, sems)
        other = lambda dev: (dev[0], dev[1], 1 - dev[2])
        for a in range(n):
            copy(a, 0, sibling, me).wait_recv()
            for k, dev in ((5, xn), (6, yn), (7, dg)):
                copy(a, k, other(dev), me).wait_recv()
        from_x, from_y = relayed(src, dst, sems)
        for cp in own(src, dst, sems) + from_x + from_y + diagonal(src, dst, sems):
            cp.wait_send()
        for cp in local(src, dst, sems):
            cp.wait()

    return _Job(shards, [_sds(spec[0], s.dtype) for spec, s in zip(specs, shards)],
                [pltpu.SemaphoreType.DMA((n, 8)), pltpu.SemaphoreType.DMA((n, 8)), pltpu.SemaphoreType.DMA((n,))],
                [(0.0, start), (relay_at, relay), (forward_at, forward), (1.0, finish)])


def _sibling_job(grads):
    n = len(grads)

    def copies(src, dst, sems):
        x, y, c = _position()
        return [pltpu.make_async_remote_copy(src_ref=src[a].at[2 * q + 1 - c], dst_ref=dst[a].at[q],
                                             send_sem=sems[0].at[a, q], recv_sem=sems[1].at[a, q],
                                             device_id=(x, y, 1 - c), device_id_type=MESH)
                for a in range(n) for q in range(NCHIP)]

    return _exchange_job(grads, NCHIP, copies)


def _chips_job(chip_sums):
    n = len(chip_sums)

    def copies(src, dst, sems):
        x, y, c = _position()
        chips = [(1 - x, y), (x, 1 - y), (1 - x, 1 - y)]
        return [pltpu.make_async_remote_copy(src_ref=src[a].at[2 * chip[0] + chip[1]], dst_ref=dst[a].at[j],
                                             send_sem=sems[0].at[a, j], recv_sem=sems[1].at[a, j],
                                             device_id=(*chip, c), device_id_type=MESH)
                for j, chip in enumerate(chips) for a in range(n)]

    return _exchange_job(chip_sums, 3, copies)


def _reduce_scatter_small(grads, name):
    n = len(grads)
    stage1, stage2 = _sibling_job(grads), _chips_job([_sds((NCHIP,) + g.shape[1:], g.dtype) for g in grads])

    def body(*refs):
        src, refs = refs[:n], refs[n:]
        sib, refs = refs[:n], refs[n:]
        far, refs = refs[:n], refs[n:]
        sums, sems1, sems2 = refs[:n], refs[n:n + 2], refs[n + 2:]
        for _, fn in stage1.phases:
            fn(src, sib, sems1)
        core = lax.axis_index("c")
        for a in range(n):
            for q in range(NCHIP):
                sums[a][q] = (src[a][2 * q + core].astype(F32) + sib[a][q].astype(F32)).astype(BF16)
        for _, fn in stage2.phases:
            fn(sums, far, sems2)

    outs = pl.pallas_call(
        body, name=name, out_shape=stage1.out_shape + stage2.out_shape, in_specs=[_vmem()] * n,
        out_specs=[_vmem()] * (2 * n),
        scratch_shapes=[pltpu.VMEM((NCHIP,) + g.shape[1:], BF16) for g in grads] + stage1.scratch + stage2.scratch,
    )(*grads)
    return outs[:n], outs[n:]


def _exchange_job(arrays, slots, copies):
    n = len(arrays)

    def start(src, dst, sems):
        for cp in copies(src, dst, sems):
            cp.start()

    def finish(src, dst, sems):
        cps = copies(src, dst, sems)
        for cp in cps:
            cp.wait_recv()
        for cp in cps:
            cp.wait_send()

    return _Job(arrays, [_sds((slots,) + a.shape[1:], a.dtype) for a in arrays],
                [pltpu.SemaphoreType.DMA((n, slots)), pltpu.SemaphoreType.DMA((n, slots))],
                [(0.0, start), (1.0, finish)])


def _row_block(r):
    if r <= 512:
        return r
    for rb in range(512, 15, -16):
        if r % rb == 0:
            return rb
    return r


def _chip_sums(grads, from_sibling, core, name):
    n = len(grads)

    def body(core_ref, *refs):
        del core_ref
        for a in range(n):
            refs[2 * n + a][...] = (refs[a][...].astype(F32) + refs[n + a][...].astype(F32)).astype(BF16)

    block = lambda g, index: pl.BlockSpec((None,) + g.shape[1:], index)
    grid_spec = pltpu.PrefetchScalarGridSpec(
        num_scalar_prefetch=1, grid=(NCHIP,),
        in_specs=[block(g, lambda q, core: (2 * q + core[0], 0, 0)) for g in grads]
        + [block(g, lambda q, core: (q, 0, 0)) for g in grads],
        out_specs=[block(g, lambda q, core: (q, 0, 0)) for g in grads])
    return pl.pallas_call(body, name=name, grid_spec=grid_spec,
                          out_shape=[_sds((NCHIP,) + g.shape[1:], BF16) for g in grads],
                          compiler_params=_params("parallel"))(core, *grads, *from_sibling)


def _adamw_math(w, g, m, v):
    m2 = ADAM_B1 * m + (1.0 - ADAM_B1) * g
    v2 = ADAM_B2 * v + (1.0 - ADAM_B2) * jnp.square(g)
    m_hat = m2 / (1.0 - ADAM_B1 ** ADAM_STEP)
    v_hat = v2 / (1.0 - ADAM_B2 ** ADAM_STEP)
    delta = -ADAM_LR * (m_hat / (jnp.sqrt(v_hat) + ADAM_EPS) + ADAM_WD * w)
    return delta, m2, v2


def _adamw(w, g, m, v, name):
    r, c = w.shape
    rb = _row_block(r)

    def body(w_ref, g_ref, m_ref, v_ref, d_ref, m2_ref, v2_ref):
        d, m2, v2 = _adamw_math(w_ref[...], g_ref[...], m_ref[...], v_ref[...])
        d_ref[...] = d
        m2_ref[...] = m2
        v2_ref[...] = v2

    blk = pl.BlockSpec((rb, c), lambda i: (i, 0))
    return pl.pallas_call(body, name=name, grid=(r // rb,), in_specs=[blk] * 4, out_specs=[blk] * 3,
                          out_shape=[_sds((r, c), F32)] * 3, compiler_params=_params("parallel"))(w, g, m, v)


def _adamw_group(items, name):
    n = len(items)

    def body(*refs):
        ins, outs = refs[:4 * n], refs[4 * n:]
        for a in range(n):
            w_ref, g_ref, m_ref, v_ref = ins[4 * a:4 * a + 4]
            for ref, val in zip(outs[3 * a:3 * a + 3], _adamw_math(w_ref[...], g_ref[...], m_ref[...], v_ref[...])):
                ref[...] = val

    outs = pl.pallas_call(body, name=name, in_specs=[_vmem()] * (4 * n), out_specs=[_vmem()] * (3 * n),
                          out_shape=[_sds(q[0].shape, F32) for q in items for _ in range(3)])(
        *[a for q in items for a in q])
    return [outs[3 * a:3 * a + 3] for a in range(n)]


def _reduce_adamw(grad, from_sibling, from_chips, sel, w, m, v, name, transposed=False):
    r, c = w.shape[::-1] if transposed else w.shape
    cp = grad.shape[2]
    rb = _row_block(r)

    def body(sel_ref, g_ref, s_ref, c0_ref, c1_ref, c2_ref, w_ref, m_ref, v_ref, go_ref, d_ref, m2_ref, v2_ref):
        del sel_ref
        g = g_ref[...].astype(F32) + s_ref[...].astype(F32)
        g = g + c0_ref[...].astype(F32)
        g = g + c1_ref[...].astype(F32)
        g = g + c2_ref[...].astype(F32)
        g = g.T[0:c, :] if transposed else g[:, 0:c]
        d, m2, v2 = _adamw_math(w_ref[...], g, m_ref[...], v_ref[...])
        go_ref[...] = g
        d_ref[...] = d
        m2_ref[...] = m2
        v2_ref[...] = v2

    blk = pl.BlockSpec((c, rb), lambda i, sel: (0, i)) if transposed else pl.BlockSpec((rb, c), lambda i, sel: (i, 0))
    grid_spec = pltpu.PrefetchScalarGridSpec(
        num_scalar_prefetch=1, grid=(r // rb,),
        in_specs=[pl.BlockSpec((None, rb, cp), lambda i, sel: (sel[0], i, 0)),
                  pl.BlockSpec((None, rb, cp), lambda i, sel: (sel[1], i, 0)),
                  pl.BlockSpec((None, rb, cp), lambda i, sel: (0, i, 0)),
                  pl.BlockSpec((None, rb, cp), lambda i, sel: (1, i, 0)),
                  pl.BlockSpec((None, rb, cp), lambda i, sel: (2, i, 0)),
                  blk, blk, blk],
        out_specs=[blk] * 4)
    return pl.pallas_call(body, name=name, grid_spec=grid_spec, out_shape=[_sds(w.shape, F32)] * 4,
                          compiler_params=_params("parallel"))(sel, grad, from_sibling, from_chips, from_chips,
                                                               from_chips, w, m, v)


def _reduce_adamw_group(items, sel, name):
    n = len(items)

    def body(sel_ref, *refs):
        del sel_ref
        ins, outs = refs[:8 * n], refs[8 * n:]
        for a in range(n):
            g_ref, s_ref, c0_ref, c1_ref, c2_ref, w_ref, m_ref, v_ref = ins[8 * a:8 * a + 8]
            g = g_ref[...].astype(F32) + s_ref[...].astype(F32)
            g = g + c0_ref[...].astype(F32)
            g = g + c1_ref[...].astype(F32)
            g = g + c2_ref[...].astype(F32)
            d, m2, v2 = _adamw_math(w_ref[...], g, m_ref[...], v_ref[...])
            for ref, val in zip(outs[4 * a:4 * a + 4], (g, d, m2, v2)):
                ref[...] = val

    in_specs, out_specs, out_shape, operands = [], [], [], []
    for grad, from_sibling, from_chips, w, m, v in items:
        slot = lambda index, shape=grad.shape[1:]: pl.BlockSpec((None,) + shape, index)
        full = pl.BlockSpec(w.shape, lambda i, sel: (0, 0))
        in_specs += [slot(lambda i, sel: (sel[0], 0, 0)), slot(lambda i, sel: (sel[1], 0, 0)),
                     slot(lambda i, sel: (0, 0, 0)), slot(lambda i, sel: (1, 0, 0)), slot(lambda i, sel: (2, 0, 0)),
                     full, full, full]
        out_specs += [full] * 4
        out_shape += [_sds(w.shape, F32)] * 4
        operands += [grad, from_sibling, from_chips, from_chips, from_chips, w, m, v]
    grid_spec = pltpu.PrefetchScalarGridSpec(num_scalar_prefetch=1, grid=(1,), in_specs=in_specs, out_specs=out_specs)
    outs = pl.pallas_call(body, name=name, grid_spec=grid_spec, out_shape=out_shape,
                          compiler_params=_params("arbitrary"))(sel, *operands)
    return [outs[4 * a:4 * a + 4] for a in range(n)]


def _wada_grad(c_all, dmod_piece):
    d = c_all.shape[1]
    n = dmod_piece.shape[1]

    def body(c_ref, dm_ref, o_ref):
        o_ref[...] = _dot_tn(c_ref[...], dm_ref[...])

    return pl.pallas_call(body, name="ada_wgrad", out_shape=_sds((d, n), F32),
                          in_specs=[_vmem()] * 2, out_specs=_vmem())(c_all, dmod_piece)


def _column_chunks(width):
    for n in (4, 2):
        if width % (n * LANES) == 0:
            return n
    return 1


def _conv_taps(ref, col):
    return ref[0:1, col], ref[1:2, col], ref[2:3, col]


def _gelu_parts(u):
    u2 = u * u
    th = jnp.tanh((GELU_C0 * u) * (1.0 + GELU_C1 * u2))
    dcdf = (1.0 - th * th) * ((0.5 * GELU_C0) * (1.0 + (3.0 * GELU_C1) * u2))
    return 0.5 * (1.0 + th), dcdf


def _conv3_bwd(dv, carry, col, taps, x):
    halo = _halo_bottom(dv[:16, :], carry[:, col])
    carry[:, col] = dv[:16, :]
    d1 = _shift_up(dv, halo, 1)
    d2 = _shift_up(dv, halo, 2)
    w0, w1, w2 = taps
    dx = w2 * dv
    dx = dx + w1 * d1
    dx = dx + w0 * d2
    return dx, (_colsum(d2 * x), _colsum(d1 * x), _colsum(dv * x))


def _prenorm(x, vp_ref):
    r = lax.rsqrt(_rowmean(x * x) + EPS)
    nh = x * r
    return (nh * vp_ref[0:1, :]) * vp_ref[1:2, :] + vp_ref[2:3, :], r, nh


def _prenorm_bwd(dh, r, nh, vp_ref, red_ref):
    g, sc1 = vp_ref[0:1, :], vp_ref[1:2, :]
    red_ref[0:1, :] += _colsum(dh)
    red_ref[1:2, :] += _colsum(dh * (nh * g))
    red_ref[2:3, :] += _colsum(dh * nh * sc1)
    dnh = dh * g * sc1
    return r * (dnh - nh * _rowmean(dnh * nh))


def _postnorm_bwd(dres, z, gate, gpost, red_ref):
    r = lax.rsqrt(_rowmean(z * z) + EPS)
    nh = z * r
    dn = dres * gate
    red_ref[0:1, :] += _colsum(dn * nh)
    red_ref[1:2, :] += _colsum(dres * (nh * gpost))
    dnh = dn * gpost
    return r * (dnh - nh * _rowmean(dnh * nh))


def _mixer_block_fwd(x, vec_pre, vec, w_in, w_pool, w_bout, w_o, job):
    t, d = x.shape
    tm = GROUP
    gw = d // len(POOL_WINDOWS)
    pool_rows = 8 * (POOL_WINDOWS[-1] - 1)

    def body(x_ref, vp_ref, vec_ref, win_ref, wp_ref, wb_ref, wo_ref,
             hb_ref, p5_ref, qm_ref, cv_ref, yar_ref, yb_ref, o_ref, x1_ref, mbuf, ucarry, pcarry):
        i = pl.program_id(0)

        @pl.when(i == 0)
        def _():
            ucarry[...] = jnp.zeros_like(ucarry)
            pcarry[...] = jnp.zeros_like(pcarry)

        xp = _interleave(x_ref[...])
        hb = _prenorm(xp, vp_ref)[0].astype(BF16)
        hb_ref[...] = hb
        proj = lambda k: _dot(hb, win_ref[:, k * d:(k + 1) * d])

        za = proj(4)
        p5_ref[:, 3 * d:4 * d] = za.astype(BF16)
        sa = jax.nn.sigmoid(za)
        u_pool = proj(0)
        for g, window in enumerate(POOL_WINDOWS):
            cols = slice(g * gw, (g + 1) * gw)
            rows = 8 * (window - 1)
            u = u_pool[:, cols]
            halo = _halo_top(u[tm - rows:, :], ucarry[pool_rows - rows:, cols])
            s, shift = jnp.concatenate([halo, u], axis=0), 1
            while shift < window:
                s = s[8 * shift:, :] + s[:s.shape[0] - 8 * shift, :]
                shift *= 2
            pgb = (s * _inv_count(i * tm, window) - u).astype(BF16)
            qm_ref[:, 2 * d + g * gw:2 * d + (g + 1) * gw] = pgb
            yar = _dot(pgb, wp_ref[g])
            yar_ref[:, cols] = yar
            mbuf[:, cols] = sa[:, cols] * (yar * vec_ref[2:3, cols])
        ucarry[...] = u_pool[tm - pool_rows:, :]

        ux = proj(1)
        uc = proj(3)
        p5_ref[:, 0:d] = ux.astype(BF16)
        p5_ref[:, 2 * d:3 * d] = uc.astype(BF16)
        p = uc * ux
        halo = _halo_top(p[tm - 16:, :], pcarry[...])
        pcarry[...] = p[tm - 16:, :]
        cv = vec_ref[3:4, :] + vec_ref[4:5, :] * _shift_down(p, halo, 2)
        cv = cv + vec_ref[5:6, :] * _shift_down(p, halo, 1)
        cv = cv + vec_ref[6:7, :] * p
        cv_ref[...] = cv
        ub = proj(2)
        p5_ref[:, d:2 * d] = ub.astype(BF16)
        qb = (ub * cv).astype(BF16)
        qm_ref[:, 0:d] = qb
        yb = _dot(qb, wb_ref[...])
        yb_ref[...] = yb

        zb = proj(5)
        p5_ref[:, 4 * d:5 * d] = zb.astype(BF16)
        mb = (mbuf[...] + jax.nn.sigmoid(zb) * yb).astype(BF16)
        qm_ref[:, d:2 * d] = mb
        o = _dot(mb, wo_ref[...])
        o_ref[...] = o
        r2 = lax.rsqrt(_rowmean(o * o) + EPS)
        x1_ref[...] = xp + vec_ref[0:1, :] * ((o * r2) * vec_ref[1:2, :])

    row = lambda n: pl.BlockSpec((tm, n), lambda i: (i, 0))
    widths = [d, 5 * d, 3 * d, d, d, d, d, d]
    return _call(
        body, "mixer_block_fwd", (t // tm,), [row(d)] + [_vmem()] * 6, [row(n) for n in widths],
        [_sds((t, n), BF16) for n in widths[:3]] + [_sds((t, n), F32) for n in widths[3:]],
        [pltpu.VMEM((tm, d), F32), pltpu.VMEM((pool_rows, d), F32), pltpu.VMEM((16, d), F32)],
        _params("arbitrary"), (x, vec_pre, vec, w_in, w_pool, w_bout, w_o), job)


def _ffn_block_fwd(x1, target, vec_pre, vec, fcv, w_up, w_down):
    t, d = x1.shape
    tm = GROUP
    fp = w_down.shape[0]
    nch = _column_chunks(fp)
    cw = fp // nch

    def body(x1_ref, tg_ref, vp_ref, vec_ref, fcv_ref, wu_ref, wd_ref,
             hb_ref, upb_ref, upreb_ref, a_ref, ffb_ref, dy_ref, loss_ref, carry):
        i = pl.program_id(0)

        @pl.when(i == 0)
        def _():
            carry[...] = jnp.zeros_like(carry)
            loss_ref[...] = jnp.zeros_like(loss_ref)

        x1 = x1_ref[...]
        hb = _prenorm(x1, vp_ref)[0].astype(BF16)
        hb_ref[...] = hb

        cols = [(slice(j * cw, (j + 1) * cw), slice(fp + j * cw, fp + (j + 1) * cw)) for j in range(nch)]
        up_gate = _dot(hb, wu_ref[:, 0:fp])
        up_val = _dot(hb, wu_ref[:, fp:2 * fp])

        def conv(v, col):
            halo = _halo_top(v[tm - 16:, :], carry[:, col])
            carry[:, col] = v[tm - 16:, :]
            w0, w1, w2 = _conv_taps(fcv_ref, col)
            y = fcv_ref[3:4, col] + w0 * _shift_down(v, halo, 2)
            y = y + w1 * _shift_down(v, halo, 1)
            y = y + w2 * v
            upb_ref[:, col] = y.astype(BF16)
            upreb_ref[:, col] = v.astype(BF16)
            return y

        ff = None
        for j in range(nch):
            gc, vc = cols[j]
            gate = conv(up_gate[:, gc], gc)
            val = conv(up_val[:, gc], vc)
            ab = ((gate * _gelu_parts(gate)[0]) * val).astype(BF16)
            a_ref[:, gc] = ab
            part = _dot(ab, wd_ref[gc, :])
            ff = part if ff is None else ff + part
        ffb_ref[...] = ff.astype(BF16)
        r4 = lax.rsqrt(_rowmean(ff * ff) + EPS)
        y = x1 + vec_ref[0:1, :] * ((ff * r4) * vec_ref[1:2, :])
        e = y - _interleave(tg_ref[...])
        dy_ref[...] = e * (1.0 / d)
        loss_ref[...] += jnp.sum(_rowmean(e * e))

    row = lambda n: pl.BlockSpec((tm, n), lambda i: (i, 0))
    return pl.pallas_call(
        body, name="ffn_block_fwd", grid=(t // tm,),
        in_specs=[row(d), row(d)] + [_vmem()] * 5,
        out_specs=[row(d), row(2 * fp), row(2 * fp), row(fp), row(d), row(d), pl.BlockSpec((8, LANES), lambda i: (0, 0))],
        out_shape=[_sds((t, d), BF16), _sds((t, 2 * fp), BF16), _sds((t, 2 * fp), BF16), _sds((t, fp), BF16),
                   _sds((t, d), BF16), _sds((t, d), F32), _sds((8, LANES), F32)],
        scratch_shapes=[pltpu.VMEM((16, 2 * fp), F32)],
        compiler_params=_params("arbitrary"),
    )(x1, target, vec_pre, vec, fcv, w_up, w_down)


def _ffn_block_bwd(dy, ffb, x1, upb, upreb, vec_pre, vec, fcv, w_up, w_down):
    t, d = dy.shape
    tm = GROUP
    fp = w_down.shape[0]
    nch = _column_chunks(fp)
    cw = fp // nch
    nt = t // tm

    def body(dy_ref, ff_ref, x1_ref, upb_ref, upreb_ref, vp_ref, vec_ref, fcv_ref, wu_ref, wd_ref,
             dff_ref, dup_ref, dx1_ref, red_ref, cred_ref, pred_ref, carry):
        @pl.when(pl.program_id(0) == 0)
        def _():
            carry[...] = jnp.zeros_like(carry)
            red_ref[...] = jnp.zeros_like(red_ref)
            cred_ref[...] = jnp.zeros_like(cred_ref)
            pred_ref[...] = jnp.zeros_like(pred_ref)

        dy_v = dy_ref[...]
        dffb = _postnorm_bwd(dy_v, ff_ref[...].astype(F32), vec_ref[0:1, :], vec_ref[1:2, :], red_ref).astype(BF16)
        dff_ref[...] = dffb

        def conv_bwd(dv, col):
            dx, (t0, t1, t2) = _conv3_bwd(dv, carry, col, _conv_taps(fcv_ref, col), upreb_ref[:, col].astype(F32))
            cred_ref[0:1, col] += t0
            cred_ref[1:2, col] += t1
            cred_ref[2:3, col] += t2
            cred_ref[3:4, col] += _colsum(dv)
            dxb = dx.astype(BF16)
            dup_ref[:, col] = dxb
            return _dot_nt(dxb, wu_ref[:, col])

        dh = None
        for j in range(nch):
            gc = slice(j * cw, (j + 1) * cw)
            vc = slice(fp + j * cw, fp + (j + 1) * cw)
            da = _dot_nt(dffb, wd_ref[gc, :])
            gate = upb_ref[:, gc].astype(F32)
            val = upb_ref[:, vc].astype(F32)
            cdf, dcdf = _gelu_parts(gate)
            part = conv_bwd(da * val * (cdf + gate * dcdf), gc) + conv_bwd(da * (gate * cdf), vc)
            dh = part if dh is None else dh + part

        _, r, nh = _prenorm(x1_ref[...], vp_ref)
        dx1_ref[...] = dy_v + _prenorm_bwd(dh, r, nh, vp_ref, pred_ref)

    rev = lambda n: pl.BlockSpec((tm, n), lambda i: (nt - 1 - i, 0))
    fixed = lambda n: pl.BlockSpec((8, n), lambda i: (0, 0))
    return pl.pallas_call(
        body, name="ffn_block_bwd", grid=(nt,),
        in_specs=[rev(d), rev(d), rev(d), rev(2 * fp), rev(2 * fp)] + [_vmem()] * 5,
        out_specs=[rev(d), rev(2 * fp), rev(d), fixed(d), fixed(2 * fp), fixed(d)],
        out_shape=[_sds((t, d), BF16), _sds((t, 2 * fp), BF16), _sds((t, d), F32), _sds((8, d), F32),
                   _sds((8, 2 * fp), F32), _sds((8, d), F32)],
        scratch_shapes=[pltpu.VMEM((16, 2 * fp), F32)],
        compiler_params=_params("arbitrary"),
    )(dy, ffb, x1, upb, upreb, vec_pre, vec, fcv, w_up, w_down)


def _mixer_block_bwd(dx1, o, yar, yb, cv, p5b, x, vec_pre, vec, w_in, w_pool, w_bout, w_o, job):
    t, d = dx1.shape
    tm = GROUP
    gw = d // len(POOL_WINDOWS)
    nt = t // tm
    pool_rows = 8 * (POOL_WINDOWS[-1] - 1)

    def body(dx1_ref, o_ref, yar_ref, yb_ref, cv_ref, p5_ref, x_ref, vp_ref, vec_ref, win_ref, wp_ref, wb_ref, wo_ref,
             dqm_ref, dp_ref, gx_ref, red_ref, pred_ref, dpgcarry, dcvcarry):
        i = pl.program_id(0)
        tix = nt - 1 - i

        @pl.when(i == 0)
        def _():
            red_ref[...] = jnp.zeros_like(red_ref)
            pred_ref[...] = jnp.zeros_like(pred_ref)
            dpgcarry[...] = jnp.zeros_like(dpgcarry)
            dcvcarry[...] = jnp.zeros_like(dcvcarry)

        pscale = vec_ref[2:3, :]
        dx1_v = dx1_ref[...]
        dob = _postnorm_bwd(dx1_v, o_ref[...].astype(F32), vec_ref[0:1, :], vec_ref[1:2, :], red_ref).astype(BF16)
        dqm_ref[:, d:2 * d] = dob
        dm = _dot_nt(dob, wo_ref[...])

        def dproj(cols, value):
            vb = value.astype(BF16)
            dp_ref[:, cols] = vb
            return _dot_nt(vb, win_ref[:, cols])

        sa = jax.nn.sigmoid(p5_ref[:, 3 * d:4 * d].astype(F32))
        yar = yar_ref[...].astype(F32)
        dya = dm * sa
        dh = dproj(slice(4 * d, 5 * d), dm * (yar * pscale) * sa * (1.0 - sa))
        red_ref[2:3, :] += _colsum(dya * yar)
        dyarb = (dya * pscale).astype(BF16)
        dqm_ref[:, 2 * d:3 * d] = dyarb
        sb = jax.nn.sigmoid(p5_ref[:, 4 * d:5 * d].astype(F32))
        dybb = (dm * sb).astype(BF16)
        dqm_ref[:, 0:d] = dybb
        dh = dh + dproj(slice(5 * d, 6 * d), dm * yb_ref[...].astype(F32) * sb * (1.0 - sb))

        for g, window in enumerate(POOL_WINDOWS):
            cols = slice(g * gw, (g + 1) * gw)
            rows = 8 * (window - 1)
            dpg = _dot_nt(dyarb[:, cols], wp_ref[g])
            dpgs = dpg * _inv_count(tix * tm, window)
            halo = _halo_bottom(dpgs[:rows, :], dpgcarry[:rows, cols])
            dpgcarry[:, cols] = dpgs[:pool_rows, :]
            s, shift = jnp.concatenate([dpgs, halo], axis=0), 1
            while shift < window:
                s = s[:s.shape[0] - 8 * shift, :] + s[8 * shift:, :]
                shift *= 2
            dh = dh + dproj(cols, s - dpg)

        dq = _dot_nt(dybb, wb_ref[...])
        ux = p5_ref[:, 0:d].astype(F32)
        uc = p5_ref[:, 2 * d:3 * d].astype(F32)
        dh = dh + dproj(slice(2 * d, 3 * d), dq * cv_ref[...].astype(F32))
        dcv = dq * p5_ref[:, d:2 * d].astype(F32)
        taps = (vec_ref[4:5, :], vec_ref[5:6, :], vec_ref[6:7, :])
        dpv, (t0, t1, t2) = _conv3_bwd(dcv, dcvcarry, slice(0, d), taps, uc * ux)
        red_ref[3:4, :] += _colsum(dcv)
        red_ref[4:5, :] += t0
        red_ref[5:6, :] += t1
        red_ref[6:7, :] += t2
        dh = dh + dproj(slice(d, 2 * d), dpv * uc)
        dh = dh + dproj(slice(3 * d, 4 * d), dpv * ux)

        _, r, nh = _prenorm(_interleave(x_ref[...]), vp_ref)
        gx_ref[...] = _deinterleave(dx1_v + _prenorm_bwd(dh, r, nh, vp_ref, pred_ref))

    rev = lambda n: pl.BlockSpec((tm, n), lambda i: (nt - 1 - i, 0))
    return _call(
        body, "mixer_block_bwd", (nt,), [rev(d)] * 5 + [rev(5 * d), rev(d)] + [_vmem()] * 6,
        [rev(3 * d), rev(6 * d), rev(d), pl.BlockSpec((16, d), lambda i: (0, 0)),
         pl.BlockSpec((8, d), lambda i: (0, 0))],
        [_sds((t, 3 * d), BF16), _sds((t, 6 * d), BF16), _sds((t, d), F32), _sds((16, d), F32), _sds((8, d), F32)],
        [pltpu.VMEM((pool_rows, d), F32), pltpu.VMEM((16, d), F32)],
        _params("arbitrary"), (dx1, o, yar, yb, cv, p5b, x, vec_pre, vec, w_in, w_pool, w_bout, w_o), job)


def _matmul_tn(a, b, bm, bn, tk, by_col_block, name, job=None):
    t, m = a.shape
    n = b.shape[1]
    nk = t // tk
    parts = int(by_col_block)
    piece = bn // max(parts, 1)
    wide = _round_up(piece, LANES)

    def body(a_ref, b_ref, o_ref, acc_ref):
        k = pl.program_id(2)

        @pl.when(k == 0)
        def _():
            acc_ref[...] = jnp.zeros_like(acc_ref)

        acc_ref[...] += _dot_tn(a_ref[...], b_ref[...])

        @pl.when(k == nk - 1)
        def _():
            if parts:
                acc = acc_ref[...]
                for p in range(parts):
                    if wide > piece:
                        o_ref[p] = jnp.zeros((bm, wide), o_ref.dtype)
                    o_ref[p, :, 0:piece] = acc[:, p * piece:(p + 1) * piece].astype(o_ref.dtype)
            else:
                o_ref[...] = acc_ref[...].astype(o_ref.dtype)

    if by_col_block:
        out_shape = _sds((parts * n // bn, m, wide), BF16)
        out_spec = pl.BlockSpec((parts, bm, wide), lambda i, j, k: (j, i, 0))
    else:
        out_shape = _sds((m, n), BF16)
        out_spec = pl.BlockSpec((bm, bn), lambda i, j, k: (i, j))
    out = _call(body, name, (m // bm, n // bn, nk),
                [pl.BlockSpec((tk, bm), lambda i, j, k: (k, i)), pl.BlockSpec((tk, bn), lambda i, j, k: (k, j))],
                [out_spec], [out_shape], [pltpu.VMEM((bm, bn), F32)],
                _params("arbitrary", "arbitrary", "arbitrary"), (a, b), job)
    return out if job is not None else out[0]


def _side_by_side(blocks, name):
    n, r, c = blocks.shape
    rb = _row_block(r) // 2

    def body(in_ref, o_ref):
        for j in range(n):
            o_ref[:, j * c:(j + 1) * c] = in_ref[j]

    return pl.pallas_call(
        body, name=name, grid=(r // rb,), in_specs=[pl.BlockSpec((n, rb, c), lambda i: (0, i, 0))],
        out_specs=pl.BlockSpec((rb, n * c), lambda i: (i, 0)), out_shape=_sds((r, n * c), blocks.dtype),
        compiler_params=_params("parallel"))(blocks)


def _matmul_tn_groups(a, b, groups, tk, name, job=None):
    t, m = a.shape
    w = m // groups
    nk = t // tk

    def body(a_ref, b_ref, o_ref, acc_ref):
        k = pl.program_id(1)

        @pl.when(k == 0)
        def _():
            acc_ref[...] = jnp.zeros_like(acc_ref)

        acc_ref[...] += _dot_tn(a_ref[...], b_ref[...])

        @pl.when(k == nk - 1)
        def _():
            o_ref[...] = acc_ref[...].astype(o_ref.dtype)

    blk = pl.BlockSpec((tk, w), lambda g, k: (k, g))
    out = _call(body, name, (groups, nk), [blk, blk], [pl.BlockSpec((None, w, w), lambda g, k: (g, 0, 0))],
                [_sds((groups, w, w), BF16)], [pltpu.VMEM((w, w), F32)], _params("arbitrary", "arbitrary"), (a, b), job)
    return out if job is not None else out[0]


def _round_up(n, k):
    return (n + k - 1) // k * k


def _rows8(rows, width):
    n = _round_up(len(rows), 8)
    rows = list(rows) + [jnp.zeros((1, width), F32)] * (n - len(rows))
    return jnp.concatenate(rows, axis=0)


def kernel(x, c, g_pre_mix, g_post_mix, g_pre_ffn, g_post_ffn, w_ada, b_ada, w_in, w_pool, pool_scale, conv_w, conv_b, w_bout, w_o, w_up, ffn_conv_w, ffn_conv_b, w_down, loss_target, m_g_pre_mix, m_g_post_mix, m_g_pre_ffn, m_g_post_ffn, m_w_ada, m_b_ada, m_w_in, m_w_pool, m_pool_scale, m_conv_w, m_conv_b, m_w_bout, m_w_o, m_w_up, m_ffn_conv_w, m_ffn_conv_b, m_w_down, v_g_pre_mix, v_g_post_mix, v_g_pre_ffn, v_g_post_ffn, v_w_ada, v_b_ada, v_w_in, v_w_pool, v_pool_scale, v_conv_w, v_conv_b, v_w_bout, v_w_o, v_w_up, v_ffn_conv_w, v_ffn_conv_b, v_w_down):
    t, d = x.shape[1], x.shape[2]
    ngroups = len(POOL_WINDOWS)
    gw = d // ngroups
    ada_n = w_ada.shape[2]
    in_n = w_in.shape[2]
    up_n = w_up.shape[2]
    fp = NDEV * w_down.shape[1]
    assert x.shape[0] == 1 and t % GROUP == 0, "one sequence per device, a whole number of token groups"
    assert gw % LANES == 0 and in_n % LANES == 0 and ada_n % LANES == 0 and NDEV * in_n == 6 * d
    assert NDEV * up_n == 2 * fp and fp % (2 * LANES) == 0, "gate and value halves cut into lane-aligned chunks"

    xi, yi, ci = _position()
    me = _linear(xi, yi, ci)
    chip = 2 * xi + yi
    core = jnp.reshape(ci, (1,)).astype(jnp.int32)
    sel = jnp.stack([2 * chip + ci, chip]).astype(jnp.int32)

    x2 = x.reshape(t, d)
    target = loss_target.reshape(t, d)

    cw_n = conv_w.shape[2]
    pack = jnp.concatenate([c.reshape(1, d), conv_w[0].reshape(1, 3 * cw_n), ffn_conv_w[0].reshape(1, 3 * up_n)], axis=1)
    pack = jnp.pad(pack, ((0, 0), (0, _round_up(pack.shape[1], LANES) - pack.shape[1])))
    b_piece = lax.dynamic_slice_in_dim(b_ada, me * ada_n, ada_n, axis=1)
    mixer_weights = _allgather_job(
        [w_in[0].astype(BF16), w_bout[0].astype(BF16), w_o[0].astype(BF16), w_pool[0].astype(BF16)],
        ["cols", "rows", "rows", "mid"], 0.5, 0.75)
    learned = _rows8([g_pre_mix, g_post_mix, g_pre_ffn, g_post_ffn, pool_scale, conv_b], d)
    gathered, vec_pre_mix, vec_mix, vec_pre_ffn, vec_ffn, fcv, w_in_f, g_bout, g_o, w_pool_f = (
        _gather_weights_and_modulation(mixer_weights, pack, w_ada[0], b_piece, learned, ffn_conv_b, d, cw_n, up_n))
    w_bout_f = g_bout.reshape(d, d)
    w_o_f = g_o.reshape(d, d)
    c16 = jnp.pad(gathered[:, :d], ((0, 8), (0, 0))).astype(BF16)

    ffn_weights = _allgather_job([w_up[0].astype(BF16), w_down[0].astype(BF16)], ["rows", "rows"], 0.5, 0.8)
    h1b, p5b, qmb, cv, yar, yb, o, x1, g_up, g_down = _mixer_block_fwd(
        x2, vec_pre_mix, vec_mix, w_in_f, w_pool_f, w_bout_f, w_o_f, ffn_weights)
    w_up_f = _side_by_side(g_up, "w_up_side_by_side")
    w_down_f = g_down.reshape(fp, d)
    h2b, upb, upreb, ab, ffb, dy, loss_part = _ffn_block_fwd(x1, target, vec_pre_ffn, vec_ffn, fcv, w_up_f, w_down_f)

    tk, tk_wide = min(4096, t), min(2048, t)
    chip_sum = lambda gs, ss, name: _chip_sums(gs, ss, core, name)
    dffb, dupre, dx1, red_ffn, red_fconv, red_pre_ffn = _ffn_block_bwd(
        dy, ffb, x1, upb, upreb, vec_pre_ffn, vec_ffn, fcv, w_up_f, w_down_f)
    chunk = fp // _column_chunks(fp)
    gr_up = _matmul_tn(h2b, dupre, d, chunk, tk_wide, chunk // up_n, "wgrad_up")
    gr_down, sib_up = _matmul_tn(ab, dffb, chunk, d, tk_wide, False, "wgrad_down", _sibling_job([gr_up]))
    gr_down = gr_down.reshape(NDEV, fp // NDEV, d)
    sib_ffn = [sib_up] + list(_run_job(_sibling_job([gr_down]), "rs_sibling_down"))
    dqmb, dproj, grad_x, red_mix, red_pre_mix = _mixer_block_bwd(
        dx1, o, yar, yb, cv, p5b, x2, vec_pre_mix, vec_mix, w_in_f, w_pool_f, w_bout_f, w_o_f, None)
    gr_in, fc_up, fc_down = _matmul_tn(h1b, dproj, d, in_n, tk, True, "wgrad_in",
                                       _chips_job(chip_sum([gr_up, gr_down], sib_ffn, "rs_chip_sum_ffn")))
    dmod = [red_pre_mix[0:1], red_pre_mix[1:2], red_mix[1:2], red_pre_ffn[0:1], red_pre_ffn[1:2], red_ffn[1:2]]
    small = [red_pre_mix[2:3], red_mix[0:1], red_pre_ffn[2:3], red_ffn[0:1], red_mix[2:3], red_mix[3:4],
             red_mix[4:5], red_mix[5:6], red_mix[6:7]] + dmod
    flat = jnp.concatenate(small + [red_fconv[0:4].reshape(1, 8 * fp), loss_part[0:1, 0:1]], axis=1)
    flat_n = flat.shape[1]
    width = 8 * LANES
    rows = _round_up(-(-flat_n // width), 8)
    flat = jnp.pad(flat, ((0, 0), (0, rows * width - flat_n))).reshape(rows, width)
    gat, tot, *sib_in = _small_allreduce(flat, "allreduce_small_rs_sibling_in", _sibling_job([gr_in]))
    gr_qmp, fc_in = _matmul_tn_groups(qmb, dqmb, 3, tk, "wgrad_bout_o_pool",
                                      _chips_job(chip_sum([gr_in], sib_in, "rs_chip_sum_in")))
    gr_bout = gr_qmp[0].reshape(NDEV, d // NDEV, d)
    gr_o = gr_qmp[1].reshape(NDEV, d // NDEV, d)
    gr_pool = jnp.stack([gr_qmp[2, g * gw:(g + 1) * gw, g * gw:(g + 1) * gw] for g in range(ngroups)])
    gr_pool = gr_pool.reshape(ngroups, NDEV, gw // NDEV, gw).transpose(1, 0, 2, 3).reshape(NDEV, -1, gw)
    rest = [gr_bout, gr_o, gr_pool]
    sib_rest, fc_rest = _reduce_scatter_small(rest, "rs_rest")

    def big(grad, from_sibling, from_chips, w, m, v, name, transposed=False):
        shape = w.shape
        flat = (lambda a: a[0].T) if transposed else (lambda a: a.reshape((-1, shape[-1])))
        outs = _reduce_adamw(grad, from_sibling, from_chips, sel, flat(w), flat(m), flat(v), name, transposed)
        return [(a.T if transposed else a).reshape(shape) for a in outs]

    g_w_up, d_w_up, nm_w_up, nv_w_up = big(gr_up, sib_ffn[0], fc_up, w_up, m_w_up, v_w_up, "adamw_up",
                                           transposed=up_n % LANES != 0)
    g_w_down, d_w_down, nm_w_down, nv_w_down = big(gr_down, sib_ffn[1], fc_down, w_down, m_w_down, v_w_down, "adamw_down")
    g_w_in, d_w_in, nm_w_in, nv_w_in = big(gr_in, sib_in[0], fc_in, w_in, m_w_in, v_w_in, "adamw_in")
    flat2 = lambda a: a.reshape((-1, a.shape[-1]))
    rest_w = [(w_bout, m_w_bout, v_w_bout), (w_o, m_w_o, v_w_o), (w_pool, m_w_pool, v_w_pool)]
    rest_out = _reduce_adamw_group(
        [(g, s, f, flat2(w), flat2(m), flat2(v)) for g, s, f, (w, m, v) in zip(rest, sib_rest, fc_rest, rest_w)],
        sel, "adamw_bout_o_pool")
    (g_w_bout, d_w_bout, nm_w_bout, nv_w_bout), (g_w_o, d_w_o, nm_w_o, nv_w_o), (g_w_pool, d_w_pool, nm_w_pool, nv_w_pool) = [
        [a.reshape(w.shape) for a in outs] for outs, (w, _, _) in zip(rest_out, rest_w)]

    tot = tot.reshape(1, rows * width)
    gat = gat.reshape(NDEV, rows * width)
    take = lambda k: tot[:, k * d:(k + 1) * d]
    g_g_pre_mix, g_g_post_mix, g_g_pre_ffn, g_g_post_ffn, g_pool_scale, g_conv_b = [take(k) for k in range(6)]
    g_conv_w_full = jnp.concatenate([take(6), take(7), take(8)], axis=0)
    g_conv_w = lax.dynamic_slice_in_dim(g_conv_w_full, me * cw_n, cw_n, axis=1)
    g_b_ada = tot[:, 9 * d:15 * d]
    dmod_all = gat[:, 9 * d:15 * d]
    fconv_tot = tot[:, 15 * d:15 * d + 8 * fp].reshape(4, 2 * fp)
    loss = 0.5 * tot[0, 15 * d + 8 * fp]
    g_ffn_conv_b = fconv_tot[3:4]
    g_ffn_conv_w = lax.dynamic_slice_in_dim(fconv_tot[0:3], me * up_n, up_n, axis=1)
    dmod_piece = lax.dynamic_slice_in_dim(dmod_all, me * ada_n, ada_n, axis=1)
    g_w_ada = _wada_grad(c16, jnp.pad(dmod_piece, ((0, 8), (0, 0))).astype(BF16))

    names_small = [(g_pre_mix, g_g_pre_mix, m_g_pre_mix, v_g_pre_mix), (g_post_mix, g_g_post_mix, m_g_post_mix, v_g_post_mix),
                   (g_pre_ffn, g_g_pre_ffn, m_g_pre_ffn, v_g_pre_ffn), (g_post_ffn, g_g_post_ffn, m_g_post_ffn, v_g_post_ffn),
                   (b_ada, g_b_ada, m_b_ada, v_b_ada), (pool_scale, g_pool_scale, m_pool_scale, v_pool_scale),
                   (conv_w, g_conv_w, m_conv_w, v_conv_w), (conv_b, g_conv_b, m_conv_b, v_conv_b),
                   (ffn_conv_w, g_ffn_conv_w, m_ffn_conv_w, v_ffn_conv_w), (ffn_conv_b, g_ffn_conv_b, m_ffn_conv_b, v_ffn_conv_b)]
    small_out = _adamw_group([[a.reshape((-1, a.shape[-1])) for a in q] for q in names_small], "adamw_small")
    unpack_small = lambda k: [outs[k].reshape(q[0].shape) for outs, q in zip(small_out, names_small)]
    (d_g_pre_mix, d_g_post_mix, d_g_pre_ffn, d_g_post_ffn, d_b_ada, d_pool_scale, d_conv_w, d_conv_b,
     d_ffn_conv_w, d_ffn_conv_b) = unpack_small(0)
    (nm_g_pre_mix, nm_g_post_mix, nm_g_pre_ffn, nm_g_post_ffn, nm_b_ada, nm_pool_scale, nm_conv_w, nm_conv_b,
     nm_ffn_conv_w, nm_ffn_conv_b) = unpack_small(1)
    (nv_g_pre_mix, nv_g_post_mix, nv_g_pre_ffn, nv_g_post_ffn, nv_b_ada, nv_pool_scale, nv_conv_w, nv_conv_b,
     nv_ffn_conv_w, nv_ffn_conv_b) = unpack_small(2)
    d_w_ada, nm_w_ada, nv_w_ada = [a.reshape(w_ada.shape) for a in
                                   _adamw(w_ada[0], g_w_ada, m_w_ada[0], v_w_ada[0], "adamw_ada")]

    grads = [g_g_pre_mix, g_g_post_mix, g_g_pre_ffn, g_g_post_ffn, g_w_ada.reshape(w_ada.shape), g_b_ada, g_w_in,
             g_w_pool, g_pool_scale, g_conv_w.reshape(conv_w.shape), g_conv_b, g_w_bout, g_w_o, g_w_up,
             g_ffn_conv_w.reshape(ffn_conv_w.shape), g_ffn_conv_b, g_w_down]
    deltas = [d_g_pre_mix, d_g_post_mix, d_g_pre_ffn, d_g_post_ffn, d_w_ada, d_b_ada, d_w_in, d_w_pool, d_pool_scale,
              d_conv_w, d_conv_b, d_w_bout, d_w_o, d_w_up, d_ffn_conv_w, d_ffn_conv_b, d_w_down]
    new_m = [nm_g_pre_mix, nm_g_post_mix, nm_g_pre_ffn, nm_g_post_ffn, nm_w_ada, nm_b_ada, nm_w_in, nm_w_pool,
             nm_pool_scale, nm_conv_w, nm_conv_b, nm_w_bout, nm_w_o, nm_w_up, nm_ffn_conv_w, nm_ffn_conv_b, nm_w_down]
    new_v = [nv_g_pre_mix, nv_g_post_mix, nv_g_pre_ffn, nv_g_post_ffn, nv_w_ada, nv_b_ada, nv_w_in, nv_w_pool,
             nv_pool_scale, nv_conv_w, nv_conv_b, nv_w_bout, nv_w_o, nv_w_up, nv_ffn_conv_w, nv_ffn_conv_b, nv_w_down]
    return (loss, grad_x.reshape(x.shape), *grads, *deltas, *new_m, *new_v)
```

```python
import math

import jax
import jax.numpy as jnp
from jax import lax
from jax.experimental import pallas as pl
from jax.experimental.pallas import tpu as pltpu

F32 = jnp.float32
BF16 = jnp.bfloat16
MESH = pl.DeviceIdType.MESH

NDEV = 8
NCHIP = 4
EPS = 1e-6
POOL_WINDOWS = (2, 4, 8, 16)
LANES = 128
ADAM_LR = 0.001
ADAM_B1 = 0.9
ADAM_B2 = 0.999
ADAM_EPS = 1e-08
ADAM_WD = 0.01
ADAM_STEP = 10
GELU_C0 = math.sqrt(2.0 / math.pi)
GELU_C1 = 0.044715
VMEM_LIMIT = 56 * 2**20


def _vmem():
    return pl.BlockSpec(memory_space=pltpu.VMEM)


def _any():
    return pl.BlockSpec(memory_space=pl.ANY)


def _params(*sem):
    return pltpu.CompilerParams(dimension_semantics=sem, vmem_limit_bytes=VMEM_LIMIT)


def _sds(shape, dtype):
    return jax.ShapeDtypeStruct(tuple(shape), dtype)


def _position():
    return lax.axis_index("x"), lax.axis_index("y"), lax.axis_index("c")


def _linear(x, y, c):
    return 4 * x + 2 * y + c


def _dot(a, b):
    return jnp.dot(a, b, preferred_element_type=F32)


def _dot_nt(a, b):
    return lax.dot_general(a, b, (((1,), (1,)), ((), ())), preferred_element_type=F32)


def _dot_tn(a, b):
    return lax.dot_general(a, b, (((0,), (0,)), ((), ())), preferred_element_type=F32)


def _colsum(v):
    return jnp.sum(v, axis=0, keepdims=True)


def _rowmean(v):
    return jnp.mean(v, axis=-1, keepdims=True)


GROUP = 256


def _interleave(v):
    g, n = v.shape
    return jnp.swapaxes(v.reshape(8, g // 8, n), 0, 1).reshape(g, n)


def _deinterleave(v):
    g, n = v.shape
    return jnp.swapaxes(v.reshape(g // 8, 8, n), 0, 1).reshape(g, n)


def _halo_top(cur_last, prev_last):
    rows, n = cur_last.shape
    c3 = cur_last.reshape(rows // 8, 8, n)
    p3 = prev_last.reshape(rows // 8, 8, n)
    sub = lax.broadcasted_iota(jnp.int32, c3.shape, 1)
    return jnp.where(sub == 0, pltpu.roll(p3, 1, 1), pltpu.roll(c3, 1, 1)).reshape(rows, n)


def _halo_bottom(cur_first, next_first):
    rows, n = cur_first.shape
    c3 = cur_first.reshape(rows // 8, 8, n)
    n3 = next_first.reshape(rows // 8, 8, n)
    sub = lax.broadcasted_iota(jnp.int32, c3.shape, 1)
    return jnp.where(sub == 7, pltpu.roll(n3, 7, 1), pltpu.roll(c3, 7, 1)).reshape(rows, n)


def _shift_down(v, halo, k):
    rows = v.shape[0]
    return jnp.concatenate([halo[halo.shape[0] - 8 * k:, :], v[:rows - 8 * k, :]], axis=0)


def _shift_up(v, halo, k):
    return jnp.concatenate([v[8 * k:, :], halo[:8 * k, :]], axis=0)


def _inv_count(first_token, window):
    row = lax.broadcasted_iota(jnp.int32, (GROUP, 1), 0)
    t = first_token + (row % 8) * (GROUP // 8) + row // 8
    return 1.0 / jnp.minimum(t + 1, window).astype(F32)


class _Job:
    def __init__(self, inputs, out_shape, scratch, phases):
        self.inputs, self.out_shape, self.scratch, self.phases = list(inputs), list(out_shape), list(scratch), phases


def _call(body, name, grid, in_specs, out_specs, out_shape, scratch_shapes, params, operands, job=None):
    if job is None:
        return pl.pallas_call(body, name=name, grid=grid, in_specs=in_specs, out_specs=out_specs, out_shape=out_shape,
                              scratch_shapes=scratch_shapes, compiler_params=params)(*operands)
    n_in, n_out, n_scr = len(in_specs), len(out_specs), len(scratch_shapes)
    j_in, j_out = len(job.inputs), len(job.out_shape)
    steps = math.prod(grid)

    def hosted(*refs):
        own_in, refs = refs[:n_in], refs[n_in:]
        jin, refs = refs[:j_in], refs[j_in:]
        own_out, refs = refs[:n_out], refs[n_out:]
        jout, refs = refs[:j_out], refs[j_out:]
        own_scr, jscr = refs[:n_scr], refs[n_scr:]
        step = pl.program_id(0)
        for axis in range(1, len(grid)):
            step = step * grid[axis] + pl.program_id(axis)
        for frac, fn in job.phases[:-1]:
            pl.when(step == int(frac * (steps - 1)))(lambda fn=fn: fn(jin, jout, jscr))
        body(*own_in, *own_out, *own_scr)
        pl.when(step == steps - 1)(lambda: job.phases[-1][1](jin, jout, jscr))

    return pl.pallas_call(
        hosted, name=name, grid=grid, in_specs=list(in_specs) + [_any()] * j_in,
        out_specs=list(out_specs) + [_any()] * j_out, out_shape=list(out_shape) + job.out_shape,
        scratch_shapes=list(scratch_shapes) + job.scratch, compiler_params=params)(*operands, *job.inputs)


def _run_job(job, name):
    n_in, n_out = len(job.inputs), len(job.out_shape)

    def body(*refs):
        for _, fn in job.phases:
            fn(refs[:n_in], refs[n_in:n_in + n_out], refs[n_in + n_out:])

    return pl.pallas_call(body, name=name, out_shape=job.out_shape, in_specs=[_any()] * n_in,
                          out_specs=[_any()] * n_out, scratch_shapes=job.scratch)(*job.inputs)


def _peers(x, y, c):
    out = []
    for k in range(1, NDEV):
        out.append(((1 - x) if k & 4 else x, (1 - y) if k & 2 else y, (1 - c) if k & 1 else c))
    return out


def _small_allreduce(v, name, job):
    r, n = v.shape
    j_in, j_out = len(job.inputs), len(job.out_shape)

    def body(v_ref, *rest):
        jin, rest = rest[:j_in], rest[j_in:]
        gat_ref, sum_ref = rest[:2]
        jout, rest = rest[2:2 + j_out], rest[2 + j_out:]
        send_sems, recv_sems, local_sem = rest[:3]
        jscr = rest[3:]
        job.phases[0][1](jin, jout, jscr)
        x, y, c = _position()
        me = _linear(x, y, c)
        mine = pltpu.make_async_copy(v_ref, gat_ref.at[me], local_sem)
        mine.start()
        peers = _peers(x, y, c)
        sends = []
        for k, peer in enumerate(peers):
            cp = pltpu.make_async_remote_copy(src_ref=v_ref, dst_ref=gat_ref.at[me], send_sem=send_sems.at[k],
                                              recv_sem=recv_sems.at[k], device_id=peer, device_id_type=MESH)
            cp.start()
            sends.append(cp)
        for k, peer in enumerate(peers):
            pltpu.make_async_remote_copy(src_ref=v_ref, dst_ref=gat_ref.at[_linear(*peer)], send_sem=send_sems.at[k],
                                         recv_sem=recv_sems.at[k], device_id=peer, device_id_type=MESH).wait_recv()
        for cp in sends:
            cp.wait_send()
        mine.wait()
        acc = gat_ref[0]
        for j in range(1, NDEV):
            acc = acc + gat_ref[j]
        sum_ref[...] = acc
        job.phases[-1][1](jin, jout, jscr)

    return pl.pallas_call(
        body, name=name, out_shape=[_sds((NDEV, r, n), F32), _sds((r, n), F32)] + job.out_shape,
        in_specs=[_vmem()] + [_any()] * j_in, out_specs=[_vmem()] * 2 + [_any()] * j_out,
        scratch_shapes=[pltpu.SemaphoreType.DMA((NDEV - 1,)), pltpu.SemaphoreType.DMA((NDEV - 1,)),
                        pltpu.SemaphoreType.DMA(())] + job.scratch,
    )(v, *job.inputs)


def _exchange_rows(src_for, dst_ref, sems):
    send_sems, recv_sems, local_sem = sems
    x, y, c = _position()
    me = _linear(x, y, c)
    row = lambda j: dst_ref.at[pl.ds(j, 1), :]
    mine = pltpu.make_async_copy(src_for(me), row(me), local_sem)
    mine.start()
    peers = _peers(x, y, c)
    sends = []
    for k, peer in enumerate(peers):
        cp = pltpu.make_async_remote_copy(src_ref=src_for(_linear(*peer)), dst_ref=row(me), send_sem=send_sems.at[k],
                                          recv_sem=recv_sems.at[k], device_id=peer, device_id_type=MESH)
        cp.start()
        sends.append(cp)
    for k, peer in enumerate(peers):
        pltpu.make_async_remote_copy(src_ref=src_for(me), dst_ref=row(_linear(*peer)), send_sem=send_sems.at[k],
                                     recv_sem=recv_sems.at[k], device_id=peer, device_id_type=MESH).wait_recv()
    for cp in sends:
        cp.wait_send()
    mine.wait()


def _gather_weights_and_modulation(job, pack, w_ada, b_piece, learned, fcb, d, cw_n, up_n):
    n = pack.shape[1]
    m = w_ada.shape[1]
    j_in, j_out = len(job.inputs), len(job.out_shape)
    row_sems = [pltpu.SemaphoreType.DMA((NDEV - 1,)), pltpu.SemaphoreType.DMA((NDEV - 1,)), pltpu.SemaphoreType.DMA(())]

    def body(pack_ref, wada_ref, bp_ref, lrn_ref, fcb_ref, *rest):
        jin, rest = rest[:j_in], rest[j_in:]
        gat_ref, vpm_ref, vmix_ref, vpf_ref, vffn_ref, fcv_ref = rest[:6]
        jout, rest = rest[6:6 + j_out], rest[6 + j_out:]
        sems1, sems2, piece, mod_ref, modm, jscr = rest[0:3], rest[3:6], rest[6], rest[7], rest[8], rest[9:]
        phases = [fn for _, fn in job.phases]
        phases[0](jin, jout, jscr)
        _exchange_rows(lambda j: pack_ref, gat_ref, sems1)
        c16 = jnp.concatenate([gat_ref[:, 0:d], jnp.zeros((NDEV, d), F32)], axis=0).astype(BF16)
        piece[...] = (_dot(c16, wada_ref[...].astype(BF16)) + bp_ref[...])[0:NDEV, :]
        _exchange_rows(lambda j: piece.at[pl.ds(j, 1), :], mod_ref, sems2)

        for j in range(NDEV):
            done = 0
            while done < m:
                row, col = divmod(j * m + done, d)
                width = min(m - done, d - col)
                modm[row:row + 1, col:col + width] = mod_ref[j:j + 1, done:done + width]
                done += width
        for ref in (vpm_ref, vmix_ref, vpf_ref, vffn_ref, fcv_ref):
            ref[...] = jnp.zeros_like(ref)
        for ref, g_row, k in ((vpm_ref, 0, 0), (vpf_ref, 2, 3)):
            ref[0:1, :] = lrn_ref[g_row:g_row + 1, :]
            ref[1:2, :] = 1.0 + modm[k + 1:k + 2, :]
            ref[2:3, :] = modm[k:k + 1, :]
        vmix_ref[0:1, :] = modm[2:3, :]
        vmix_ref[1:2, :] = lrn_ref[1:2, :]
        vmix_ref[2:4, :] = lrn_ref[4:6, :]
        vffn_ref[0:1, :] = modm[5:6, :]
        vffn_ref[1:2, :] = lrn_ref[3:4, :]
        fcv_ref[3:4, :] = fcb_ref[...]
        for j in range(NDEV):
            for k in range(3):
                vmix_ref[4 + k:5 + k, j * cw_n:(j + 1) * cw_n] = gat_ref[j:j + 1, d + k * cw_n:d + (k + 1) * cw_n]
                off = d + 3 * cw_n + k * up_n
                fcv_ref[k:k + 1, j * up_n:(j + 1) * up_n] = gat_ref[j:j + 1, off:off + up_n]

        for fn in phases[1:]:
            fn(jin, jout, jscr)

    tables = [_sds((8, d), F32), _sds((16, d), F32), _sds((8, d), F32), _sds((8, d), F32), _sds((8, NDEV * up_n), F32)]
    return pl.pallas_call(
        body, name="gather_weights_and_modulation",
        out_shape=[_sds((NDEV, n), F32)] + tables + job.out_shape,
        in_specs=[_vmem()] * 5 + [_any()] * j_in, out_specs=[_vmem()] * 6 + [_any()] * j_out,
        scratch_shapes=row_sems + row_sems + [pltpu.VMEM((NDEV, m), F32), pltpu.VMEM((NDEV, m), F32),
                                              pltpu.VMEM((8, d), F32)] + job.scratch,
    )(pack, w_ada, b_piece, learned, fcb, *job.inputs)


def _gathered(shard, layout):
    if layout == "rows":
        return (NDEV,) + shard.shape, lambda ref, j: ref.at[j]
    if layout == "cols":
        r, c = shard.shape
        return (r, NDEV * c), lambda ref, j: ref.at[:, pl.ds(pl.multiple_of(j * c, LANES), c)]
    g, r, c = shard.shape
    return (g, NDEV * r, c), lambda ref, j: ref.at[:, pl.ds(pl.multiple_of(j * r, 16), r), :]


def _allgather_job(shards, layouts, relay_at, forward_at):
    n = len(shards)
    specs = [_gathered(s, l) for s, l in zip(shards, layouts)]
    halves = [s.shape[0] // 2 for s in shards]
    cast = shards[0].dtype != BF16
    own_shard = lambda src, sems, a: sems[3 + a] if cast else src[a]

    def plan(src, dst, sems):
        send_sems, recv_sems = sems[0], sems[1]
        x, y, c = _position()
        me, sibling, xn, yn, dg = (x, y, c), (x, y, 1 - c), (1 - x, y, c), (x, 1 - y, c), (1 - x, 1 - y, c)

        def copy(a, k, block, to, half=None, from_src=False):
            blk = specs[a][1](dst[a], _linear(*block))
            if half is not None:
                blk = blk.at[pl.ds(half * halves[a], halves[a])]
            return pltpu.make_async_remote_copy(src_ref=own_shard(src, sems, a) if from_src else blk, dst_ref=blk,
                                                send_sem=send_sems.at[a, k], recv_sem=recv_sems.at[a, k],
                                                device_id=to, device_id_type=MESH)
        return copy, me, sibling, xn, yn, dg

    def local(src, dst, sems):
        x, y, c = _position()
        return [pltpu.make_async_copy(own_shard(src, sems, a), specs[a][1](dst[a], _linear(x, y, c)), sems[2].at[a])
                for a in range(n)]

    def own(src, dst, sems):
        copy, me, sibling, xn, yn, dg = plan(src, dst, sems)
        return [copy(a, k, me, to, from_src=True) for k, to in ((1, xn), (2, yn), (0, sibling)) for a in range(n)]

    def relayed(src, dst, sems):
        copy, me, sibling, xn, yn, dg = plan(src, dst, sems)
        return ([copy(a, 3, xn, yn, half=0) for a in range(n)] + [copy(a, 5, xn, sibling) for a in range(n)],
                [copy(a, 4, yn, xn, half=1) for a in range(n)] + [copy(a, 6, yn, sibling) for a in range(n)])

    def diagonal(src, dst, sems):
        copy, me, sibling, xn, yn, dg = plan(src, dst, sems)
        return [copy(a, 7, dg, sibling) for a in range(n)]

    def start(src, dst, sems):
        if cast:
            loads = [pltpu.make_async_copy(src[a], sems[3 + n + a], sems[2].at[a]) for a in range(n)]
            for cp in loads:
                cp.start()
            for a in range(n):
                loads[a].wait()
                sems[3 + a][...] = sems[3 + n + a][...].astype(BF16)
        for cp in local(src, dst, sems) + own(src, dst, sems):
            cp.start()

    def relay(src, dst, sems):
        copy, me, sibling, xn, yn, dg = plan(src, dst, sems)
        from_x, from_y = relayed(src, dst, sems)
        for a in range(n):
            copy(a, 1, xn, me).wait_recv()
        for cp in from_x:
            cp.start()
        for a in range(n):
            copy(a, 2, yn, me).wait_recv()
        for cp in from_y:
            cp.start()

    def forward(src, dst, sems):
        copy, me, sibling, xn, yn, dg = plan(src, dst, sems)
        for a in range(n):
            copy(a, 3, dg, me, half=0).wait_recv()
            copy(a, 4, dg, me, half=1).wait_recv()
        for cp in diagonal(src, dst, sems):
            cp.start()

    def finish(src, dst, sems):
        copy, me, sibling, xn, yn, dg = plan(src, dst, sems)
        other = lambda dev: (dev[0], dev[1], 1 - dev[2])
        for a in range(n):
            copy(a, 0, sibling, me).wait_recv()
            for k, dev in ((5, xn), (6, yn), (7, dg)):
                copy(a, k, other(dev), me).wait_recv()
        from_x, from_y = relayed(src, dst, sems)
        for cp in own(src, dst, sems) + from_x + from_y + diagonal(src, dst, sems):
            cp.wait_send()
        for cp in local(src, dst, sems):
            cp.wait()

    staging = [pltpu.VMEM(s.shape, dt) for dt in (BF16, F32) for s in shards] if cast else []
    return _Job(shards, [_sds(spec[0], BF16) for spec in specs],
                [pltpu.SemaphoreType.DMA((n, 8)), pltpu.SemaphoreType.DMA((n, 8)), pltpu.SemaphoreType.DMA((n,))] + staging,
                [(0.0, start), (relay_at, relay), (forward_at, forward), (1.0, finish)])


def _sibling_job(grads):
    n = len(grads)

    def copies(src, dst, sems):
        x, y, c = _position()
        return [pltpu.make_async_remote_copy(src_ref=src[a].at[2 * q + 1 - c], dst_ref=dst[a].at[q],
                                             send_sem=sems[0].at[a, q], recv_sem=sems[1].at[a, q],
                                             device_id=(x, y, 1 - c), device_id_type=MESH)
                for a in range(n) for q in range(NCHIP)]

    return _exchange_job(grads, NCHIP, copies)


def _chips_job(chip_sums):
    n = len(chip_sums)

    def copies(src, dst, sems):
        x, y, c = _position()
        chips = [(1 - x, y), (x, 1 - y), (1 - x, 1 - y)]
        return [pltpu.make_async_remote_copy(src_ref=src[a].at[2 * chip[0] + chip[1]], dst_ref=dst[a].at[j],
                                             send_sem=sems[0].at[a, j], recv_sem=sems[1].at[a, j],
                                             device_id=(*chip, c), device_id_type=MESH)
                for j, chip in enumerate(chips) for a in range(n)]

    return _exchange_job(chip_sums, 3, copies)


def _reduce_scatter_small(grads, name):
    n = len(grads)
    stage1, stage2 = _sibling_job(grads), _chips_job([_sds((NCHIP,) + g.shape[1:], g.dtype) for g in grads])

    def body(*refs):
        src, refs = refs[:n], refs[n:]
        sib, refs = refs[:n], refs[n:]
        far, refs = refs[:n], refs[n:]
        sums, sems1, sems2 = refs[:n], refs[n:n + 2], refs[n + 2:]
        for _, fn in stage1.phases:
            fn(src, sib, sems1)
        core = lax.axis_index("c")
        for a in range(n):
            for q in range(NCHIP):
                sums[a][q] = (src[a][2 * q + core].astype(F32) + sib[a][q].astype(F32)).astype(BF16)
        for _, fn in stage2.phases:
            fn(sums, far, sems2)

    outs = pl.pallas_call(
        body, name=name, out_shape=stage1.out_shape + stage2.out_shape, in_specs=[_vmem()] * n,
        out_specs=[_vmem()] * (2 * n),
        scratch_shapes=[pltpu.VMEM((NCHIP,) + g.shape[1:], BF16) for g in grads] + stage1.scratch + stage2.scratch,
    )(*grads)
    return outs[:n], outs[n:]


def _exchange_job(arrays, slots, copies):
    n = len(arrays)

    def start(src, dst, sems):
        for cp in copies(src, dst, sems):
            cp.start()

    def finish(src, dst, sems):
        cps = copies(src, dst, sems)
        for cp in cps:
            cp.wait_recv()
        for cp in cps:
            cp.wait_send()

    return _Job(arrays, [_sds((slots,) + a.shape[1:], a.dtype) for a in arrays],
                [pltpu.SemaphoreType.DMA((n, slots)), pltpu.SemaphoreType.DMA((n, slots))],
                [(0.0, start), (1.0, finish)])


def _row_block(r):
    if r <= 512:
        return r
    for rb in range(512, 15, -16):
        if r % rb == 0:
            return rb
    return r


def _chip_sums(grads, from_sibling, core, name):
    n = len(grads)

    def body(core_ref, *refs):
        del core_ref
        for a in range(n):
            refs[2 * n + a][...] = (refs[a][...].astype(F32) + refs[n + a][...].astype(F32)).astype(BF16)

    block = lambda g, index: pl.BlockSpec((None,) + g.shape[1:], index)
    grid_spec = pltpu.PrefetchScalarGridSpec(
        num_scalar_prefetch=1, grid=(NCHIP,),
        in_specs=[block(g, lambda q, core: (2 * q + core[0], 0, 0)) for g in grads]
        + [block(g, lambda q, core: (q, 0, 0)) for g in grads],
        out_specs=[block(g, lambda q, core: (q, 0, 0)) for g in grads])
    return pl.pallas_call(body, name=name, grid_spec=grid_spec,
                          out_shape=[_sds((NCHIP,) + g.shape[1:], BF16) for g in grads],
                          compiler_params=_params("parallel"))(core, *grads, *from_sibling)


def _adamw_math(w, g, m, v):
    m2 = ADAM_B1 * m + (1.0 - ADAM_B1) * g
    v2 = ADAM_B2 * v + (1.0 - ADAM_B2) * jnp.square(g)
    m_hat = m2 / (1.0 - ADAM_B1 ** ADAM_STEP)
    v_hat = v2 / (1.0 - ADAM_B2 ** ADAM_STEP)
    delta = -ADAM_LR * (m_hat / (jnp.sqrt(v_hat) + ADAM_EPS) + ADAM_WD * w)
    return delta, m2, v2


def _adamw(w, g, m, v, name):
    r, c = w.shape
    rb = _row_block(r)

    def body(w_ref, g_ref, m_ref, v_ref, d_ref, m2_ref, v2_ref):
        d, m2, v2 = _adamw_math(w_ref[...], g_ref[...], m_ref[...], v_ref[...])
        d_ref[...] = d
        m2_ref[...] = m2
        v2_ref[...] = v2

    blk = pl.BlockSpec((rb, c), lambda i: (i, 0))
    return pl.pallas_call(body, name=name, grid=(r // rb,), in_specs=[blk] * 4, out_specs=[blk] * 3,
                          out_shape=[_sds((r, c), F32)] * 3, compiler_params=_params("parallel"))(w, g, m, v)


def _adamw_group(items, name):
    n = len(items)

    def body(*refs):
        ins, outs = refs[:4 * n], refs[4 * n:]
        for a in range(n):
            w_ref, g_ref, m_ref, v_ref = ins[4 * a:4 * a + 4]
            for ref, val in zip(outs[3 * a:3 * a + 3], _adamw_math(w_ref[...], g_ref[...], m_ref[...], v_ref[...])):
                ref[...] = val

    outs = pl.pallas_call(body, name=name, in_specs=[_vmem()] * (4 * n), out_specs=[_vmem()] * (3 * n),
                          out_shape=[_sds(q[0].shape, F32) for q in items for _ in range(3)])(
        *[a for q in items for a in q])
    return [outs[3 * a:3 * a + 3] for a in range(n)]


def _reduce_adamw(grad, from_sibling, from_chips, sel, w, m, v, name, transposed=False):
    r, c = w.shape[::-1] if transposed else w.shape
    cp = grad.shape[2]
    rb = _row_block(r)

    def body(sel_ref, g_ref, s_ref, c0_ref, c1_ref, c2_ref, w_ref, m_ref, v_ref, go_ref, d_ref, m2_ref, v2_ref):
        del sel_ref
        g = g_ref[...].astype(F32) + s_ref[...].astype(F32)
        g = g + c0_ref[...].astype(F32)
        g = g + c1_ref[...].astype(F32)
        g = g + c2_ref[...].astype(F32)
        g = g.T[0:c, :] if transposed else g[:, 0:c]
        d, m2, v2 = _adamw_math(w_ref[...], g, m_ref[...], v_ref[...])
        go_ref[...] = g
        d_ref[...] = d
        m2_ref[...] = m2
        v2_ref[...] = v2

    blk = pl.BlockSpec((c, rb), lambda i, sel: (0, i)) if transposed else pl.BlockSpec((rb, c), lambda i, sel: (i, 0))
    grid_spec = pltpu.PrefetchScalarGridSpec(
        num_scalar_prefetch=1, grid=(r // rb,),
        in_specs=[pl.BlockSpec((None, rb, cp), lambda i, sel: (sel[0], i, 0)),
                  pl.BlockSpec((None, rb, cp), lambda i, sel: (sel[1], i, 0)),
                  pl.BlockSpec((None, rb, cp), lambda i, sel: (0, i, 0)),
                  pl.BlockSpec((None, rb, cp), lambda i, sel: (1, i, 0)),
                  pl.BlockSpec((None, rb, cp), lambda i, sel: (2, i, 0)),
                  blk, blk, blk],
        out_specs=[blk] * 4)
    return pl.pallas_call(body, name=name, grid_spec=grid_spec, out_shape=[_sds(w.shape, F32)] * 4,
                          compiler_params=_params("parallel"))(sel, grad, from_sibling, from_chips, from_chips,
                                                               from_chips, w, m, v)


def _reduce_adamw_group(items, sel, name):
    n = len(items)

    def body(sel_ref, *refs):
        del sel_ref
        ins, outs = refs[:8 * n], refs[8 * n:]
        for a in range(n):
            g_ref, s_ref, c0_ref, c1_ref, c2_ref, w_ref, m_ref, v_ref = ins[8 * a:8 * a + 8]
            g = g_ref[...].astype(F32) + s_ref[...].astype(F32)
            g = g + c0_ref[...].astype(F32)
            g = g + c1_ref[...].astype(F32)
            g = g + c2_ref[...].astype(F32)
            d, m2, v2 = _adamw_math(w_ref[...], g, m_ref[...], v_ref[...])
            for ref, val in zip(outs[4 * a:4 * a + 4], (g, d, m2, v2)):
                ref[...] = val

    in_specs, out_specs, out_shape, operands = [], [], [], []
    for grad, from_sibling, from_chips, w, m, v in items:
        slot = lambda index, shape=grad.shape[1:]: pl.BlockSpec((None,) + shape, index)
        full = pl.BlockSpec(w.shape, lambda i, sel: (0, 0))
        in_specs += [slot(lambda i, sel: (sel[0], 0, 0)), slot(lambda i, sel: (sel[1], 0, 0)),
                     slot(lambda i, sel: (0, 0, 0)), slot(lambda i, sel: (1, 0, 0)), slot(lambda i, sel: (2, 0, 0)),
                     full, full, full]
        out_specs += [full] * 4
        out_shape += [_sds(w.shape, F32)] * 4
        operands += [grad, from_sibling, from_chips, from_chips, from_chips, w, m, v]
    grid_spec = pltpu.PrefetchScalarGridSpec(num_scalar_prefetch=1, grid=(1,), in_specs=in_specs, out_specs=out_specs)
    outs = pl.pallas_call(body, name=name, grid_spec=grid_spec, out_shape=out_shape,
                          compiler_params=_params("arbitrary"))(sel, *operands)
    return [outs[4 * a:4 * a + 4] for a in range(n)]


def _wada_grad(c_all, dmod_piece):
    d = c_all.shape[1]
    n = dmod_piece.shape[1]

    def body(c_ref, dm_ref, o_ref):
        o_ref[...] = _dot_tn(c_ref[...], dm_ref[...])

    return pl.pallas_call(body, name="ada_wgrad", out_shape=_sds((d, n), F32),
                          in_specs=[_vmem()] * 2, out_specs=_vmem())(c_all, dmod_piece)


def _column_chunks(width):
    for n in (4, 2):
        if width % (n * LANES) == 0:
            return n
    return 1


def _conv_taps(ref, col):
    return ref[0:1, col], ref[1:2, col], ref[2:3, col]


def _gelu_parts(u):
    u2 = u * u
    th = jnp.tanh((GELU_C0 * u) * (1.0 + GELU_C1 * u2))
    dcdf = (1.0 - th * th) * ((0.5 * GELU_C0) * (1.0 + (3.0 * GELU_C1) * u2))
    return 0.5 * (1.0 + th), dcdf


def _conv3_bwd(dv, carry, col, taps, x):
    halo = _halo_bottom(dv[:16, :], carry[:, col])
    carry[:, col] = dv[:16, :]
    d1 = _shift_up(dv, halo, 1)
    d2 = _shift_up(dv, halo, 2)
    w0, w1, w2 = taps
    dx = w2 * dv
    dx = dx + w1 * d1
    dx = dx + w0 * d2
    return dx, (_colsum(d2 * x), _colsum(d1 * x), _colsum(dv * x))


def _prenorm(x, vp_ref):
    r = lax.rsqrt(_rowmean(x * x) + EPS)
    nh = x * r
    return (nh * vp_ref[0:1, :]) * vp_ref[1:2, :] + vp_ref[2:3, :], r, nh


def _prenorm_bwd(dh, r, nh, vp_ref, red_ref):
    g, sc1 = vp_ref[0:1, :], vp_ref[1:2, :]
    red_ref[0:1, :] += _colsum(dh)
    red_ref[1:2, :] += _colsum(dh * (nh * g))
    red_ref[2:3, :] += _colsum(dh * nh * sc1)
    dnh = dh * g * sc1
    return r * (dnh - nh * _rowmean(dnh * nh))


def _postnorm_bwd(dres, z, gate, gpost, red_ref):
    r = lax.rsqrt(_rowmean(z * z) + EPS)
    nh = z * r
    dn = dres * gate
    red_ref[0:1, :] += _colsum(dn * nh)
    red_ref[1:2, :] += _colsum(dres * (nh * gpost))
    dnh = dn * gpost
    return r * (dnh - nh * _rowmean(dnh * nh))


def _mixer_block_fwd(x, vec_pre, vec, w_in, w_pool, w_bout, w_o, job):
    t, d = x.shape
    tm = GROUP
    gw = d // len(POOL_WINDOWS)
    pool_rows = 8 * (POOL_WINDOWS[-1] - 1)

    def body(x_ref, vp_ref, vec_ref, win_ref, wp_ref, wb_ref, wo_ref,
             hb_ref, p5_ref, qm_ref, cv_ref, yar_ref, yb_ref, o_ref, x1_ref, mbuf, ucarry, pcarry):
        i = pl.program_id(0)

        @pl.when(i == 0)
        def _():
            ucarry[...] = jnp.zeros_like(ucarry)
            pcarry[...] = jnp.zeros_like(pcarry)

        xp = _interleave(x_ref[...])
        hb = _prenorm(xp, vp_ref)[0].astype(BF16)
        hb_ref[...] = hb
        proj = lambda k: _dot(hb, win_ref[:, k * d:(k + 1) * d])

        za = proj(4)
        p5_ref[:, 3 * d:4 * d] = za.astype(BF16)
        sa = jax.nn.sigmoid(za)
        u_pool = proj(0)
        for g, window in enumerate(POOL_WINDOWS):
            cols = slice(g * gw, (g + 1) * gw)
            rows = 8 * (window - 1)
            u = u_pool[:, cols]
            halo = _halo_top(u[tm - rows:, :], ucarry[pool_rows - rows:, cols])
            s, shift = jnp.concatenate([halo, u], axis=0), 1
            while shift < window:
                s = s[8 * shift:, :] + s[:s.shape[0] - 8 * shift, :]
                shift *= 2
            pgb = (s * _inv_count(i * tm, window) - u).astype(BF16)
            qm_ref[:, 2 * d + g * gw:2 * d + (g + 1) * gw] = pgb
            yar = _dot(pgb, wp_ref[g])
            yar_ref[:, cols] = yar
            mbuf[:, cols] = sa[:, cols] * (yar * vec_ref[2:3, cols])
        ucarry[...] = u_pool[tm - pool_rows:, :]

        ux = proj(1)
        uc = proj(3)
        p5_ref[:, 0:d] = ux.astype(BF16)
        p5_ref[:, 2 * d:3 * d] = uc.astype(BF16)
        p = uc * ux
        halo = _halo_top(p[tm - 16:, :], pcarry[...])
        pcarry[...] = p[tm - 16:, :]
        cv = vec_ref[3:4, :] + vec_ref[4:5, :] * _shift_down(p, halo, 2)
        cv = cv + vec_ref[5:6, :] * _shift_down(p, halo, 1)
        cv = cv + vec_ref[6:7, :] * p
        cv_ref[...] = cv
        ub = proj(2)
        p5_ref[:, d:2 * d] = ub.astype(BF16)
        qb = (ub * cv).astype(BF16)
        qm_ref[:, 0:d] = qb
        yb = _dot(qb, wb_ref[...])
        yb_ref[...] = yb

        zb = proj(5)
        p5_ref[:, 4 * d:5 * d] = zb.astype(BF16)
        mb = (mbuf[...] + jax.nn.sigmoid(zb) * yb).astype(BF16)
        qm_ref[:, d:2 * d] = mb
        o = _dot(mb, wo_ref[...])
        o_ref[...] = o
        r2 = lax.rsqrt(_rowmean(o * o) + EPS)
        x1_ref[...] = xp + vec_ref[0:1, :] * ((o * r2) * vec_ref[1:2, :])

    row = lambda n: pl.BlockSpec((tm, n), lambda i: (i, 0))
    widths = [d, 5 * d, 3 * d, d, d, d, d, d]
    return _call(
        body, "mixer_block_fwd", (t // tm,), [row(d)] + [_vmem()] * 6, [row(n) for n in widths],
        [_sds((t, n), BF16) for n in widths[:3]] + [_sds((t, n), F32) for n in widths[3:]],
        [pltpu.VMEM((tm, d), F32), pltpu.VMEM((pool_rows, d), F32), pltpu.VMEM((16, d), F32)],
        _params("arbitrary"), (x, vec_pre, vec, w_in, w_pool, w_bout, w_o), job)


def _ffn_block_fwd(x1, target, vec_pre, vec, fcv, w_up, w_down):
    t, d = x1.shape
    tm = GROUP
    fp = w_down.shape[0]
    nch = _column_chunks(fp)
    cw = fp // nch

    def body(x1_ref, tg_ref, vp_ref, vec_ref, fcv_ref, wu_ref, wd_ref,
             hb_ref, upb_ref, upreb_ref, a_ref, ffb_ref, dy_ref, loss_ref, carry):
        i = pl.program_id(0)

        @pl.when(i == 0)
        def _():
            carry[...] = jnp.zeros_like(carry)
            loss_ref[...] = jnp.zeros_like(loss_ref)

        x1 = x1_ref[...]
        hb = _prenorm(x1, vp_ref)[0].astype(BF16)
        hb_ref[...] = hb

        cols = [(slice(j * cw, (j + 1) * cw), slice(fp + j * cw, fp + (j + 1) * cw)) for j in range(nch)]
        up_gate = _dot(hb, wu_ref[:, 0:fp])
        up_val = _dot(hb, wu_ref[:, fp:2 * fp])

        def conv(v, col):
            halo = _halo_top(v[tm - 16:, :], carry[:, col])
            carry[:, col] = v[tm - 16:, :]
            w0, w1, w2 = _conv_taps(fcv_ref, col)
            y = fcv_ref[3:4, col] + w0 * _shift_down(v, halo, 2)
            y = y + w1 * _shift_down(v, halo, 1)
            y = y + w2 * v
            upb_ref[:, col] = y.astype(BF16)
            upreb_ref[:, col] = v.astype(BF16)
            return y

        ff = None
        for j in range(nch):
            gc, vc = cols[j]
            gate = conv(up_gate[:, gc], gc)
            val = conv(up_val[:, gc], vc)
            ab = ((gate * _gelu_parts(gate)[0]) * val).astype(BF16)
            a_ref[:, gc] = ab
            part = _dot(ab, wd_ref[gc, :])
            ff = part if ff is None else ff + part
        ffb_ref[...] = ff.astype(BF16)
        r4 = lax.rsqrt(_rowmean(ff * ff) + EPS)
        y = x1 + vec_ref[0:1, :] * ((ff * r4) * vec_ref[1:2, :])
        e = y - _interleave(tg_ref[...])
        dy_ref[...] = e * (1.0 / d)
        loss_ref[...] += jnp.sum(_rowmean(e * e))

    row = lambda n: pl.BlockSpec((tm, n), lambda i: (i, 0))
    return pl.pallas_call(
        body, name="ffn_block_fwd", grid=(t // tm,),
        in_specs=[row(d), row(d)] + [_vmem()] * 5,
        out_specs=[row(d), row(2 * fp), row(2 * fp), row(fp), row(d), row(d), pl.BlockSpec((8, LANES), lambda i: (0, 0))],
        out_shape=[_sds((t, d), BF16), _sds((t, 2 * fp), BF16), _sds((t, 2 * fp), BF16), _sds((t, fp), BF16),
                   _sds((t, d), BF16), _sds((t, d), F32), _sds((8, LANES), F32)],
        scratch_shapes=[pltpu.VMEM((16, 2 * fp), F32)],
        compiler_params=_params("arbitrary"),
    )(x1, target, vec_pre, vec, fcv, w_up, w_down)


def _ffn_block_bwd(dy, ffb, x1, upb, upreb, vec_pre, vec, fcv, w_up, w_down):
    t, d = dy.shape
    tm = GROUP
    fp = w_down.shape[0]
    nch = _column_chunks(fp)
    cw = fp // nch
    nt = t // tm

    def body(dy_ref, ff_ref, x1_ref, upb_ref, upreb_ref, vp_ref, vec_ref, fcv_ref, wu_ref, wd_ref,
             dff_ref, dup_ref, dx1_ref, red_ref, cred_ref, pred_ref, carry):
        @pl.when(pl.program_id(0) == 0)
        def _():
            carry[...] = jnp.zeros_like(carry)
            red_ref[...] = jnp.zeros_like(red_ref)
            cred_ref[...] = jnp.zeros_like(cred_ref)
            pred_ref[...] = jnp.zeros_like(pred_ref)

        dy_v = dy_ref[...]
        dffb = _postnorm_bwd(dy_v, ff_ref[...].astype(F32), vec_ref[0:1, :], vec_ref[1:2, :], red_ref).astype(BF16)
        dff_ref[...] = dffb

        def conv_bwd(dv, col):
            dx, (t0, t1, t2) = _conv3_bwd(dv, carry, col, _conv_taps(fcv_ref, col), upreb_ref[:, col].astype(F32))
            cred_ref[0:1, col] += t0
            cred_ref[1:2, col] += t1
            cred_ref[2:3, col] += t2
            cred_ref[3:4, col] += _colsum(dv)
            dxb = dx.astype(BF16)
            dup_ref[:, col] = dxb
            return _dot_nt(dxb, wu_ref[:, col])

        dh = None
        for j in range(nch):
            gc = slice(j * cw, (j + 1) * cw)
            vc = slice(fp + j * cw, fp + (j + 1) * cw)
            da = _dot_nt(dffb, wd_ref[gc, :])
            gate = upb_ref[:, gc].astype(F32)
            val = upb_ref[:, vc].astype(F32)
            cdf, dcdf = _gelu_parts(gate)
            part = conv_bwd(da * val * (cdf + gate * dcdf), gc) + conv_bwd(da * (gate * cdf), vc)
            dh = part if dh is None else dh + part

        _, r, nh = _prenorm(x1_ref[...], vp_ref)
        dx1_ref[...] = dy_v + _prenorm_bwd(dh, r, nh, vp_ref, pred_ref)

    rev = lambda n: pl.BlockSpec((tm, n), lambda i: (nt - 1 - i, 0))
    fixed = lambda n: pl.BlockSpec((8, n), lambda i: (0, 0))
    return pl.pallas_call(
        body, name="ffn_block_bwd", grid=(nt,),
        in_specs=[rev(d), rev(d), rev(d), rev(2 * fp), rev(2 * fp)] + [_vmem()] * 5,
        out_specs=[rev(d), rev(2 * fp), rev(d), fixed(d), fixed(2 * fp), fixed(d)],
        out_shape=[_sds((t, d), BF16), _sds((t, 2 * fp), BF16), _sds((t, d), F32), _sds((8, d), F32),
                   _sds((8, 2 * fp), F32), _sds((8, d), F32)],
        scratch_shapes=[pltpu.VMEM((16, 2 * fp), F32)],
        compiler_params=_params("arbitrary"),
    )(dy, ffb, x1, upb, upreb, vec_pre, vec, fcv, w_up, w_down)


def _mixer_block_bwd(dx1, o, yar, yb, cv, p5b, x, vec_pre, vec, w_in, w_pool, w_bout, w_o, job):
    t, d = dx1.shape
    tm = GROUP
    gw = d // len(POOL_WINDOWS)
    nt = t // tm
    pool_rows = 8 * (POOL_WINDOWS[-1] - 1)

    def body(dx1_ref, o_ref, yar_ref, yb_ref, cv_ref, p5_ref, x_ref, vp_ref, vec_ref, win_ref, wp_ref, wb_ref, wo_ref,
             dqm_ref, dp_ref, gx_ref, red_ref, pred_ref, dpgcarry, dcvcarry):
        i = pl.program_id(0)
        tix = nt - 1 - i

        @pl.when(i == 0)
        def _():
            red_ref[...] = jnp.zeros_like(red_ref)
            pred_ref[...] = jnp.zeros_like(pred_ref)
            dpgcarry[...] = jnp.zeros_like(dpgcarry)
            dcvcarry[...] = jnp.zeros_like(dcvcarry)

        pscale = vec_ref[2:3, :]
        dx1_v = dx1_ref[...]
        dob = _postnorm_bwd(dx1_v, o_ref[...].astype(F32), vec_ref[0:1, :], vec_ref[1:2, :], red_ref).astype(BF16)
        dqm_ref[:, d:2 * d] = dob
        dm = _dot_nt(dob, wo_ref[...])

        def dproj(cols, value):
            vb = value.astype(BF16)
            dp_ref[:, cols] = vb
            return _dot_nt(vb, win_ref[:, cols])

        sa = jax.nn.sigmoid(p5_ref[:, 3 * d:4 * d].astype(F32))
        yar = yar_ref[...].astype(F32)
        dya = dm * sa
        dh = dproj(slice(4 * d, 5 * d), dm * (yar * pscale) * sa * (1.0 - sa))
        red_ref[2:3, :] += _colsum(dya * yar)
        dyarb = (dya * pscale).astype(BF16)
        dqm_ref[:, 2 * d:3 * d] = dyarb
        sb = jax.nn.sigmoid(p5_ref[:, 4 * d:5 * d].astype(F32))
        dybb = (dm * sb).astype(BF16)
        dqm_ref[:, 0:d] = dybb
        dh = dh + dproj(slice(5 * d, 6 * d), dm * yb_ref[...].astype(F32) * sb * (1.0 - sb))

        for g, window in enumerate(POOL_WINDOWS):
            cols = slice(g * gw, (g + 1) * gw)
            rows = 8 * (window - 1)
            dpg = _dot_nt(dyarb[:, cols], wp_ref[g])
            dpgs = dpg * _inv_count(tix * tm, window)
            halo = _halo_bottom(dpgs[:rows, :], dpgcarry[:rows, cols])
            dpgcarry[:, cols] = dpgs[:pool_rows, :]
            s, shift = jnp.concatenate([dpgs, halo], axis=0), 1
            while shift < window:
                s = s[:s.shape[0] - 8 * shift, :] + s[8 * shift:, :]
                shift *= 2
            dh = dh + dproj(cols, s - dpg)

        dq = _dot_nt(dybb, wb_ref[...])
        ux = p5_ref[:, 0:d].astype(F32)
        uc = p5_ref[:, 2 * d:3 * d].astype(F32)
        dh = dh + dproj(slice(2 * d, 3 * d), dq * cv_ref[...].astype(F32))
        dcv = dq * p5_ref[:, d:2 * d].astype(F32)
        taps = (vec_ref[4:5, :], vec_ref[5:6, :], vec_ref[6:7, :])
        dpv, (t0, t1, t2) = _conv3_bwd(dcv, dcvcarry, slice(0, d), taps, uc * ux)
        red_ref[3:4, :] += _colsum(dcv)
        red_ref[4:5, :] += t0
        red_ref[5:6, :] += t1
        red_ref[6:7, :] += t2
        dh = dh + dproj(slice(d, 2 * d), dpv * uc)
        dh = dh + dproj(slice(3 * d, 4 * d), dpv * ux)

        _, r, nh = _prenorm(_interleave(x_ref[...]), vp_ref)
        gx_ref[...] = _deinterleave(dx1_v + _prenorm_bwd(dh, r, nh, vp_ref, pred_ref))

    rev = lambda n: pl.BlockSpec((tm, n), lambda i: (nt - 1 - i, 0))
    return _call(
        body, "mixer_block_bwd", (nt,), [rev(d)] * 5 + [rev(5 * d), rev(d)] + [_vmem()] * 6,
        [rev(3 * d), rev(6 * d), rev(d), pl.BlockSpec((16, d), lambda i: (0, 0)),
         pl.BlockSpec((8, d), lambda i: (0, 0))],
        [_sds((t, 3 * d), BF16), _sds((t, 6 * d), BF16), _sds((t, d), F32), _sds((16, d), F32), _sds((8, d), F32)],
        [pltpu.VMEM((pool_rows, d), F32), pltpu.VMEM((16, d), F32)],
        _params("arbitrary"), (dx1, o, yar, yb, cv, p5b, x, vec_pre, vec, w_in, w_pool, w_bout, w_o), job)


def _matmul_tn(a, b, bm, bn, tk, by_col_block, name, job=None):
    t, m = a.shape
    n = b.shape[1]
    nk = t // tk
    parts = int(by_col_block)
    piece = bn // max(parts, 1)
    wide = _round_up(piece, LANES)

    def body(a_ref, b_ref, o_ref, acc_ref):
        k = pl.program_id(2)

        @pl.when(k == 0)
        def _():
            acc_ref[...] = jnp.zeros_like(acc_ref)

        acc_ref[...] += _dot_tn(a_ref[...], b_ref[...])

        @pl.when(k == nk - 1)
        def _():
            if parts:
                acc = acc_ref[...]
                for p in range(parts):
                    if wide > piece:
                        o_ref[p] = jnp.zeros((bm, wide), o_ref.dtype)
                    o_ref[p, :, 0:piece] = acc[:, p * piece:(p + 1) * piece].astype(o_ref.dtype)
            else:
                o_ref[...] = acc_ref[...].astype(o_ref.dtype)

    if by_col_block:
        out_shape = _sds((parts * n // bn, m, wide), BF16)
        out_spec = pl.BlockSpec((parts, bm, wide), lambda i, j, k: (j, i, 0))
    else:
        out_shape = _sds((m, n), BF16)
        out_spec = pl.BlockSpec((bm, bn), lambda i, j, k: (i, j))
    out = _call(body, name, (m // bm, n // bn, nk),
                [pl.BlockSpec((tk, bm), lambda i, j, k: (k, i)), pl.BlockSpec((tk, bn), lambda i, j, k: (k, j))],
                [out_spec], [out_shape], [pltpu.VMEM((bm, bn), F32)],
                _params("arbitrary", "arbitrary", "arbitrary"), (a, b), job)
    return out if job is not None else out[0]


def _side_by_side(blocks, name):
    n, r, c = blocks.shape
    rb = _row_block(r) // 2

    def body(in_ref, o_ref):
        for j in range(n):
            o_ref[:, j * c:(j + 1) * c] = in_ref[j]

    return pl.pallas_call(
        body, name=name, grid=(r // rb,), in_specs=[pl.BlockSpec((n, rb, c), lambda i: (0, i, 0))],
        out_specs=pl.BlockSpec((rb, n * c), lambda i: (i, 0)), out_shape=_sds((r, n * c), blocks.dtype),
        compiler_params=_params("parallel"))(blocks)


def _matmul_tn_groups(a, b, groups, tk, name, job=None):
    t, m = a.shape
    w = m // groups
    nk = t // tk

    def body(a_ref, b_ref, o_ref, acc_ref):
        k = pl.program_id(1)

        @pl.when(k == 0)
        def _():
            acc_ref[...] = jnp.zeros_like(acc_ref)

        acc_ref[...] += _dot_tn(a_ref[...], b_ref[...])

        @pl.when(k == nk - 1)
        def _():
            o_ref[...] = acc_ref[...].astype(o_ref.dtype)

    blk = pl.BlockSpec((tk, w), lambda g, k: (k, g))
    out = _call(body, name, (groups, nk), [blk, blk], [pl.BlockSpec((None, w, w), lambda g, k: (g, 0, 0))],
                [_sds((groups, w, w), BF16)], [pltpu.VMEM((w, w), F32)], _params("arbitrary", "arbitrary"), (a, b), job)
    return out if job is not None else out[0]


def _round_up(n, k):
    return (n + k - 1) // k * k


def _rows8(rows, width):
    n = _round_up(len(rows), 8)
    rows = list(rows) + [jnp.zeros((1, width), F32)] * (n - len(rows))
    return jnp.concatenate(rows, axis=0)


def kernel(x, c, g_pre_mix, g_post_mix, g_pre_ffn, g_post_ffn, w_ada, b_ada, w_in, w_pool, pool_scale, conv_w, conv_b, w_bout, w_o, w_up, ffn_conv_w, ffn_conv_b, w_down, loss_target, m_g_pre_mix, m_g_post_mix, m_g_pre_ffn, m_g_post_ffn, m_w_ada, m_b_ada, m_w_in, m_w_pool, m_pool_scale, m_conv_w, m_conv_b, m_w_bout, m_w_o, m_w_up, m_ffn_conv_w, m_ffn_conv_b, m_w_down, v_g_pre_mix, v_g_post_mix, v_g_pre_ffn, v_g_post_ffn, v_w_ada, v_b_ada, v_w_in, v_w_pool, v_pool_scale, v_conv_w, v_conv_b, v_w_bout, v_w_o, v_w_up, v_ffn_conv_w, v_ffn_conv_b, v_w_down):
    t, d = x.shape[1], x.shape[2]
    ngroups = len(POOL_WINDOWS)
    gw = d // ngroups
    ada_n = w_ada.shape[2]
    in_n = w_in.shape[2]
    up_n = w_up.shape[2]
    fp = NDEV * w_down.shape[1]
    assert x.shape[0] == 1 and t % GROUP == 0, "one sequence per device, a whole number of token groups"
    assert gw % LANES == 0 and in_n % LANES == 0 and ada_n % LANES == 0 and NDEV * in_n == 6 * d
    assert NDEV * up_n == 2 * fp and fp % (2 * LANES) == 0, "gate and value halves cut into lane-aligned chunks"

    xi, yi, ci = _position()
    me = _linear(xi, yi, ci)
    chip = 2 * xi + yi
    core = jnp.reshape(ci, (1,)).astype(jnp.int32)
    sel = jnp.stack([2 * chip + ci, chip]).astype(jnp.int32)

    x2 = x.reshape(t, d)
    target = loss_target.reshape(t, d)

    cw_n = conv_w.shape[2]
    pack = jnp.concatenate([c.reshape(1, d), conv_w[0].reshape(1, 3 * cw_n), ffn_conv_w[0].reshape(1, 3 * up_n)], axis=1)
    pack = jnp.pad(pack, ((0, 0), (0, _round_up(pack.shape[1], LANES) - pack.shape[1])))
    b_piece = lax.dynamic_slice_in_dim(b_ada, me * ada_n, ada_n, axis=1)
    mixer_weights = _allgather_job(
        [w_in[0], w_bout[0], w_o[0], w_pool[0]], ["cols", "rows", "rows", "mid"], 0.5, 0.75)
    learned = _rows8([g_pre_mix, g_post_mix, g_pre_ffn, g_post_ffn, pool_scale, conv_b], d)
    gathered, vec_pre_mix, vec_mix, vec_pre_ffn, vec_ffn, fcv, w_in_f, g_bout, g_o, w_pool_f = (
        _gather_weights_and_modulation(mixer_weights, pack, w_ada[0], b_piece, learned, ffn_conv_b, d, cw_n, up_n))
    w_bout_f = g_bout.reshape(d, d)
    w_o_f = g_o.reshape(d, d)
    c16 = jnp.pad(gathered[:, :d], ((0, 8), (0, 0))).astype(BF16)

    ffn_weights = _allgather_job([w_up[0].astype(BF16), w_down[0].astype(BF16)], ["rows", "rows"], 0.5, 0.8)
    h1b, p5b, qmb, cv, yar, yb, o, x1, g_up, g_down = _mixer_block_fwd(
        x2, vec_pre_mix, vec_mix, w_in_f, w_pool_f, w_bout_f, w_o_f, ffn_weights)
    w_up_f = _side_by_side(g_up, "w_up_side_by_side")
    w_down_f = g_down.reshape(fp, d)
    h2b, upb, upreb, ab, ffb, dy, loss_part = _ffn_block_fwd(x1, target, vec_pre_ffn, vec_ffn, fcv, w_up_f, w_down_f)

    tk, tk_wide = min(4096, t), min(2048, t)
    chip_sum = lambda gs, ss, name: _chip_sums(gs, ss, core, name)
    dffb, dupre, dx1, red_ffn, red_fconv, red_pre_ffn = _ffn_block_bwd(
        dy, ffb, x1, upb, upreb, vec_pre_ffn, vec_ffn, fcv, w_up_f, w_down_f)
    chunk = fp // _column_chunks(fp)
    gr_up = _matmul_tn(h2b, dupre, d, chunk, tk_wide, chunk // up_n, "wgrad_up")
    gr_down, sib_up = _matmul_tn(ab, dffb, chunk, d, tk_wide, False, "wgrad_down", _sibling_job([gr_up]))
    gr_down = gr_down.reshape(NDEV, fp // NDEV, d)
    sib_ffn = [sib_up] + list(_run_job(_sibling_job([gr_down]), "rs_sibling_down"))
    dqmb, dproj, grad_x, red_mix, red_pre_mix = _mixer_block_bwd(
        dx1, o, yar, yb, cv, p5b, x2, vec_pre_mix, vec_mix, w_in_f, w_pool_f, w_bout_f, w_o_f, None)
    gr_in, fc_up, fc_down = _matmul_tn(h1b, dproj, d, in_n, tk, True, "wgrad_in",
                                       _chips_job(chip_sum([gr_up, gr_down], sib_ffn, "rs_chip_sum_ffn")))
    dmod = [red_pre_mix[0:1], red_pre_mix[1:2], red_mix[1:2], red_pre_ffn[0:1], red_pre_ffn[1:2], red_ffn[1:2]]
    small = [red_pre_mix[2:3], red_mix[0:1], red_pre_ffn[2:3], red_ffn[0:1], red_mix[2:3], red_mix[3:4],
             red_mix[4:5], red_mix[5:6], red_mix[6:7]] + dmod
    flat = jnp.concatenate(small + [red_fconv[0:4].reshape(1, 8 * fp), loss_part[0:1, 0:1]], axis=1)
    flat_n = flat.shape[1]
    width = 8 * LANES
    rows = _round_up(-(-flat_n // width), 8)
    flat = jnp.pad(flat, ((0, 0), (0, rows * width - flat_n))).reshape(rows, width)
    gat, tot, *sib_in = _small_allreduce(flat, "allreduce_small_rs_sibling_in", _sibling_job([gr_in]))
    gr_qmp, fc_in = _matmul_tn_groups(qmb, dqmb, 3, tk, "wgrad_bout_o_pool",
                                      _chips_job(chip_sum([gr_in], sib_in, "rs_chip_sum_in")))
    gr_bout = gr_qmp[0].reshape(NDEV, d // NDEV, d)
    gr_o = gr_qmp[1].reshape(NDEV, d // NDEV, d)
    gr_pool = jnp.stack([gr_qmp[2, g * gw:(g + 1) * gw, g * gw:(g + 1) * gw] for g in range(ngroups)])
    gr_pool = gr_pool.reshape(ngroups, NDEV, gw // NDEV, gw).transpose(1, 0, 2, 3).reshape(NDEV, -1, gw)
    rest = [gr_bout, gr_o, gr_pool]
    sib_rest, fc_rest = _reduce_scatter_small(rest, "rs_rest")

    def big(grad, from_sibling, from_chips, w, m, v, name, transposed=False):
        shape = w.shape
        flat = (lambda a: a[0].T) if transposed else (lambda a: a.reshape((-1, shape[-1])))
        outs = _reduce_adamw(grad, from_sibling, from_chips, sel, flat(w), flat(m), flat(v), name, transposed)
        return [(a.T if transposed else a).reshape(shape) for a in outs]

    g_w_up, d_w_up, nm_w_up, nv_w_up = big(gr_up, sib_ffn[0], fc_up, w_up, m_w_up, v_w_up, "adamw_up",
                                           transposed=up_n % LANES != 0)
    g_w_down, d_w_down, nm_w_down, nv_w_down = big(gr_down, sib_ffn[1], fc_down, w_down, m_w_down, v_w_down, "adamw_down")
    g_w_in, d_w_in, nm_w_in, nv_w_in = big(gr_in, sib_in[0], fc_in, w_in, m_w_in, v_w_in, "adamw_in")
    flat2 = lambda a: a.reshape((-1, a.shape[-1]))
    rest_w = [(w_bout, m_w_bout, v_w_bout), (w_o, m_w_o, v_w_o), (w_pool, m_w_pool, v_w_pool)]
    rest_out = _reduce_adamw_group(
        [(g, s, f, flat2(w), flat2(m), flat2(v)) for g, s, f, (w, m, v) in zip(rest, sib_rest, fc_rest, rest_w)],
        sel, "adamw_bout_o_pool")
    (g_w_bout, d_w_bout, nm_w_bout, nv_w_bout), (g_w_o, d_w_o, nm_w_o, nv_w_o), (g_w_pool, d_w_pool, nm_w_pool, nv_w_pool) = [
        [a.reshape(w.shape) for a in outs] for outs, (w, _, _) in zip(rest_out, rest_w)]

    tot = tot.reshape(1, rows * width)
    gat = gat.reshape(NDEV, rows * width)
    take = lambda k: tot[:, k * d:(k + 1) * d]
    g_g_pre_mix, g_g_post_mix, g_g_pre_ffn, g_g_post_ffn, g_pool_scale, g_conv_b = [take(k) for k in range(6)]
    g_conv_w_full = jnp.concatenate([take(6), take(7), take(8)], axis=0)
    g_conv_w = lax.dynamic_slice_in_dim(g_conv_w_full, me * cw_n, cw_n, axis=1)
    g_b_ada = tot[:, 9 * d:15 * d]
    dmod_all = gat[:, 9 * d:15 * d]
    fconv_tot = tot[:, 15 * d:15 * d + 8 * fp].reshape(4, 2 * fp)
    loss = 0.5 * tot[0, 15 * d + 8 * fp]
    g_ffn_conv_b = fconv_tot[3:4]
    g_ffn_conv_w = lax.dynamic_slice_in_dim(fconv_tot[0:3], me * up_n, up_n, axis=1)
    dmod_piece = lax.dynamic_slice_in_dim(dmod_all, me * ada_n, ada_n, axis=1)
    g_w_ada = _wada_grad(c16, jnp.pad(dmod_piece, ((0, 8), (0, 0))).astype(BF16))

    names_small = [(g_pre_mix, g_g_pre_mix, m_g_pre_mix, v_g_pre_mix), (g_post_mix, g_g_post_mix, m_g_post_mix, v_g_post_mix),
                   (g_pre_ffn, g_g_pre_ffn, m_g_pre_ffn, v_g_pre_ffn), (g_post_ffn, g_g_post_ffn, m_g_post_ffn, v_g_post_ffn),
                   (b_ada, g_b_ada, m_b_ada, v_b_ada), (pool_scale, g_pool_scale, m_pool_scale, v_pool_scale),
                   (conv_w, g_conv_w, m_conv_w, v_conv_w), (conv_b, g_conv_b, m_conv_b, v_conv_b),
                   (ffn_conv_w, g_ffn_conv_w, m_ffn_conv_w, v_ffn_conv_w), (ffn_conv_b, g_ffn_conv_b, m_ffn_conv_b, v_ffn_conv_b)]
    small_out = _adamw_group([[a.reshape((-1, a.shape[-1])) for a in q] for q in names_small], "adamw_small")
    unpack_small = lambda k: [outs[k].reshape(q[0].shape) for outs, q in zip(small_out, names_small)]
    (d_g_pre_mix, d_g_post_mix, d_g_pre_ffn, d_g_post_ffn, d_b_ada, d_pool_scale, d_conv_w, d_conv_b,
     d_ffn_conv_w, d_ffn_conv_b) = unpack_small(0)
    (nm_g_pre_mix, nm_g_post_mix, nm_g_pre_ffn, nm_g_post_ffn, nm_b_ada, nm_pool_scale, nm_conv_w, nm_conv_b,
     nm_ffn_conv_w, nm_ffn_conv_b) = unpack_small(1)
    (nv_g_pre_mix, nv_g_post_mix, nv_g_pre_ffn, nv_g_post_ffn, nv_b_ada, nv_pool_scale, nv_conv_w, nv_conv_b,
     nv_ffn_conv_w, nv_ffn_conv_b) = unpack_small(2)
    d_w_ada, nm_w_ada, nv_w_ada = [a.reshape(w_ada.shape) for a in
                                   _adamw(w_ada[0], g_w_ada, m_w_ada[0], v_w_ada[0], "adamw_ada")]

    grads = [g_g_pre_mix, g_g_post_mix, g_g_pre_ffn, g_g_post_ffn, g_w_ada.reshape(w_ada.shape), g_b_ada, g_w_in,
             g_w_pool, g_pool_scale, g_conv_w.reshape(conv_w.shape), g_conv_b, g_w_bout, g_w_o, g_w_up,
             g_ffn_conv_w.reshape(ffn_conv_w.shape), g_ffn_conv_b, g_w_down]
    deltas = [d_g_pre_mix, d_g_post_mix, d_g_pre_ffn, d_g_post_ffn, d_w_ada, d_b_ada, d_w_in, d_w_pool, d_pool_scale,
              d_conv_w, d_conv_b, d_w_bout, d_w_o, d_w_up, d_ffn_conv_w, d_ffn_conv_b, d_w_down]
    new_m = [nm_g_pre_mix, nm_g_post_mix, nm_g_pre_ffn, nm_g_post_ffn, nm_w_ada, nm_b_ada, nm_w_in, nm_w_pool,
             nm_pool_scale, nm_conv_w, nm_conv_b, nm_w_bout, nm_w_o, nm_w_up, nm_ffn_conv_w, nm_ffn_conv_b, nm_w_down]
    new_v = [nv_g_pre_mix, nv_g_post_mix, nv_g_pre_ffn, nv_g_post_ffn, nv_w_ada, nv_b_ada, nv_w_in, nv_w_pool,
             nv_pool_scale, nv_conv_w, nv_conv_b, nv_w_bout, nv_w_o, nv_w_up, nv_ffn_conv_w, nv_ffn_conv_b, nv_w_down]
    return (loss, grad_x.reshape(x.shape), *grads, *deltas, *new_m, *new_v)
```
